```python
import jax
import jax.numpy as jnp
from jax import lax
import numpy as np

D_MODEL = 2048
BATCH = 4
SEQ = 2048
DEPTH = 2
DEC_BATCH = 128
DEC_SEQ = 1
PAST_LEN = 2048
PAGE_SIZE = 128

D_FF = 5504
LRU_W = D_MODEL // 2
LRU_BLOCKS = 16
LRU_BS = LRU_W // LRU_BLOCKS
CONV_W = 4
LRU_C = 8.0
RWKV_W = D_MODEL // 2
RWKV_HD = 64
RWKV_H = RWKV_W // RWKV_HD
W_LORA = 64
A_LORA = 64
G_LORA = 160
SHIFT_W = 3 * RWKV_W + W_LORA + A_LORA + G_LORA
AB_COLS = 2 * LRU_W + SHIFT_W
NSA_H = 16
NSA_G = 4
NSA_HPG = NSA_H // NSA_G
NSA_HD = 64
NSA_W = NSA_H * NSA_HD
ROPE_DIMS = NSA_HD // 4
ROPE_THETA = 500000.0
CMP_BLOCK = 32
CMP_STRIDE = 16
CMP_R = CMP_BLOCK // CMP_STRIDE
CMP_HID = 256
SEL_BLOCK = 64
SEL_TOP = 16
SEL_Q_BLOCK = 64
WINDOW = 512
WIN_BLOCK = 128
FORCE_SCORE = 1e4
KV_SLOTS = 4
RET_H = 8
RET_DK = 64
RET_DV = 128
RET_W = RET_H * RET_DV
RET_CHUNK = 128
RET_THETA = 10000.0
CD_COLS = NSA_W + 6 * NSA_G * NSA_HD + 3 * NSA_H + 2 * RET_H * RET_DK + 2 * RET_W

kernel_name = 'hybrid_lru_rwkv7_nsa_retention_step'


def rms_norm(x, g, eps=1e-6):
    xf = x.astype(jnp.float32)
    y = xf * lax.rsqrt(jnp.mean(xf * xf, axis=-1, keepdims=True) + eps)
    return (y * g.astype(jnp.float32)).astype(x.dtype)


def head_group_norm(y, g, b, eps):
    yf = y.astype(jnp.float32)
    mu = jnp.mean(yf, axis=-1, keepdims=True)
    var = jnp.mean(jnp.square(yf - mu), axis=-1, keepdims=True)
    yn = ((yf - mu) * lax.rsqrt(var + eps)).reshape(y.shape[:-2] + (-1,))
    return (yn * g.astype(jnp.float32) + b.astype(jnp.float32)).astype(y.dtype)


def swiglu(x, w_in, w_out):
    g, u = jnp.split(x @ w_in, 2, axis=-1)
    return (jax.nn.silu(g) * u) @ w_out


def masked_softmax(s, mask):
    s = jnp.where(mask, s.astype(jnp.float32), -jnp.inf)
    m = jnp.max(s, axis=-1, keepdims=True)
    e = jnp.exp(s - jnp.where(jnp.isfinite(m), m, 0.0))
    den = jnp.sum(e, axis=-1, keepdims=True)
    return e / jnp.where(den > 0, den, 1.0)


def rope(x, pos, n_rot, theta):
    half = n_rot // 2
    inv = jnp.exp(-jnp.log(jnp.float32(theta)) * jnp.arange(half, dtype=jnp.float32) / half)
    ang = pos.astype(jnp.float32)[:, None] * inv[None, :]
    cos = jnp.cos(ang)[None, :, None, :]
    sin = jnp.sin(ang)[None, :, None, :]
    xf = x.astype(jnp.float32)
    x1, x2 = xf[..., :half], xf[..., half:n_rot]
    out = jnp.concatenate([x1 * cos - x2 * sin, x2 * cos + x1 * sin, xf[..., n_rot:]], axis=-1)
    return out.astype(x.dtype)


def linear_scan(a, b, h0):
    b = b.at[:, 0].add(a[:, 0] * h0)

    def combine(left, right):
        return left[0] * right[0], right[0] * left[1] + right[1]

    return lax.associative_scan(combine, (a, b), axis=1)[1]


def wkv7_scan(r, w, k, v, a, b, s0):
    xs = tuple(jnp.moveaxis(z.astype(jnp.float32), 1, 0) for z in (r, w, k, v, a, b))

    def step(S, inp):
        r_t, w_t, k_t, v_t, a_t, b_t = inp
        sa = jnp.einsum('bhij,bhj->bhi', S, a_t)
        S = S * w_t[:, :, None, :] + sa[..., None] * b_t[:, :, None, :] + v_t[..., None] * k_t[:, :, None, :]
        return S, jnp.einsum('bhij,bhj->bhi', S, r_t)

    S, ys = lax.scan(step, s0.astype(jnp.float32), xs)
    return jnp.moveaxis(ys, 0, 1), S


def even_mixer(h, p, lru_h0, lru_conv0, shift0, wkv0):
    B, T, _ = h.shape
    f32 = jnp.float32
    proj = h @ p['w_in']
    xb, gb, rw = jnp.split(proj, [LRU_W, 2 * LRU_W], axis=-1)
    xcat = jnp.concatenate([lru_conv0.astype(h.dtype), xb], axis=1)
    xc = p['conv_b'] + sum(p['conv_w'][j] * xcat[:, j:j + T] for j in range(CONV_W))
    xbd = xc.reshape(B, T, LRU_BLOCKS, LRU_BS)
    gate_r = jax.nn.sigmoid(jnp.einsum('btnc,ncd->btnd', xbd, p['wa']).reshape(B, T, LRU_W) + p['ba'])
    gate_i = jax.nn.sigmoid(jnp.einsum('btnc,ncd->btnd', xbd, p['wx']).reshape(B, T, LRU_W) + p['bx'])
    log_a = -LRU_C * gate_r.astype(f32) * jax.nn.softplus(-p['lam'].astype(f32))
    u = jnp.sqrt(-jnp.expm1(2.0 * log_a)) * (gate_i * xc).astype(f32)
    hs = linear_scan(jnp.exp(log_a), u, lru_h0.astype(f32))
    y_lru = hs.astype(h.dtype) * jax.nn.gelu(gb)
    prev = jnp.concatenate([shift0.astype(h.dtype)[:, None], rw[:, :-1]], axis=1)
    rs = rw + p['mu'] * (prev - rw)
    r, k, v, xw, xa, xg = jnp.split(
        rs, [RWKV_W, 2 * RWKV_W, 3 * RWKV_W, 3 * RWKV_W + W_LORA, 3 * RWKV_W + W_LORA + A_LORA], axis=-1)
    w_log = -jax.nn.softplus(-(p['w0'] + jnp.tanh(xw) @ p['w2']).astype(f32)) - 0.5
    decay = jnp.exp(-jnp.exp(w_log))
    a_icl = jax.nn.sigmoid(p['a0'] + xa @ p['a2'])
    g = jax.nn.sigmoid(xg) @ p['g2']
    heads = (B, T, RWKV_H, RWKV_HD)
    kk = (k * p['k_k']).reshape(heads).astype(f32)
    kk = kk / jnp.maximum(jnp.sqrt(jnp.sum(kk * kk, axis=-1, keepdims=True)), 1e-12)
    k = k * (1.0 + (a_icl - 1.0) * p['k_a'])
    rh, kh, vh, ah = (z.reshape(heads) for z in (r, k, v, a_icl))
    y, wkv = wkv7_scan(rh, decay.reshape(heads), kh, vh, -kk, kk * ah.astype(f32), wkv0)
    y = head_group_norm(y, p['ln_g'], p['ln_b'], 64e-5).astype(h.dtype)
    bonus = (jnp.sum(rh * kh * p['r_k'], axis=-1, keepdims=True) * vh).reshape(B, T, RWKV_W)
    y_rwkv = (y + bonus) * g
    out = jnp.concatenate([y_lru, y_rwkv], axis=-1) @ p['w_out']
    return out, hs[:, -1], xcat[:, T:], rw[:, -1], wkv


def odd_project(h, p, pos):
    B, T, _ = h.shape
    sizes = [NSA_W] + [NSA_G * NSA_HD] * 6 + [3 * NSA_H, RET_H * RET_DK, RET_H * RET_DK, RET_W, RET_W]
    q, kc, vc, ks, vs, kw, vw, gt, rq, rk, rv, rg = jnp.split(
        h @ p['w_in'], np.cumsum(sizes)[:-1].tolist(), axis=-1)
    kvs = (B, T, NSA_G, NSA_HD)
    q_n = rms_norm(q.reshape(B, T, NSA_H, NSA_HD), p['q_norm'])
    return {
        'q_n': q_n,
        'q_r': rope(q_n, pos, ROPE_DIMS, ROPE_THETA),
        'kc': kc.reshape(kvs), 'vc': vc.reshape(kvs),
        'ks': rope(rms_norm(ks.reshape(kvs), p['k_norm'][1]), pos, ROPE_DIMS, ROPE_THETA),
        'vs': vs.reshape(kvs),
        'kw': rope(rms_norm(kw.reshape(kvs), p['k_norm'][2]), pos, ROPE_DIMS, ROPE_THETA),
        'vw': vw.reshape(kvs),
        'gates': jax.nn.sigmoid(gt).reshape(B, T, NSA_H, 3),
        'rq': rope(rq.reshape(B, T, RET_H, RET_DK), pos, RET_DK, RET_THETA),
        'rk': rope(rk.reshape(B, T, RET_H, RET_DK), pos, RET_DK, RET_THETA) * (RET_DK ** -0.5),
        'rv': rv.reshape(B, T, RET_H, RET_DV),
        'rg': rg,
    }


def to_groups_q(q):
    B, T = q.shape[:2]
    return jnp.moveaxis(q.reshape(B, T, NSA_G, NSA_HPG, NSA_HD), 1, 3)


def to_groups_k(k):
    return jnp.moveaxis(k, 1, 2)


def nsa_compress(x, w1, b1, w2, b2):
    B, L = x.shape[:2]
    n_chunk = L // CMP_STRIDE
    n_cmp = n_chunk - CMP_R + 1
    ch = x[:, :n_chunk * CMP_STRIDE].reshape(B, n_chunk, CMP_STRIDE, NSA_G, NSA_HD)
    ch = jnp.moveaxis(ch, 3, 2).reshape(B, n_chunk, NSA_G, CMP_STRIDE * NSA_HD)
    part = jnp.einsum('bngc,rch->bngrh', ch, w1)
    pre = b1 + sum(part[:, m:m + n_cmp, :, m] for m in range(CMP_R))
    return jax.nn.gelu(pre) @ w2 + b2


def nsa_compressed_branch(qn, kc_raw, vc_raw, p, q_pos):
    kc = to_groups_k(rms_norm(nsa_compress(kc_raw, *p['ck']), p['k_norm'][0]))
    vc = to_groups_k(nsa_compress(vc_raw, *p['cv']))
    s = jnp.einsum('bghqd,bgcd->bghqc', qn, kc) * NSA_HD ** -0.5
    ends = jnp.arange(kc.shape[2]) * CMP_STRIDE + CMP_BLOCK - 1
    prob = masked_softmax(s, ends[None, :] <= q_pos[:, None])
    return jnp.einsum('bghqc,bgcd->bghqd', prob.astype(vc.dtype), vc), prob


def cmp_sel_overlap(n_cmp, n_sel):
    cs = np.arange(n_cmp) * CMP_STRIDE
    ss = np.arange(n_sel) * SEL_BLOCK
    ov = np.minimum(cs[None] + CMP_BLOCK, ss[:, None] + SEL_BLOCK) - np.maximum(cs[None], ss[:, None])
    return jnp.asarray(np.clip(ov, 0, None) / CMP_BLOCK, dtype=jnp.float32)


def nsa_select(p_cmp, q_pos, n_sel):
    imp = jnp.einsum('bgqc,sc->bgqs', p_cmp.sum(axis=2), cmp_sel_overlap(p_cmp.shape[-1], n_sel))
    j = jnp.arange(n_sel)[None, :]
    qb = (q_pos // SEL_BLOCK)[:, None]
    valid = j <= qb
    forced = (j == 0) | (j == qb) | (j == qb - 1)
    score = jnp.where(valid, jnp.where(forced, FORCE_SCORE, imp), -jnp.inf)
    _, idx = lax.top_k(score, min(SEL_TOP, n_sel))
    sel_ok = jnp.take_along_axis(jnp.broadcast_to(valid, score.shape), idx, axis=-1)
    return idx, sel_ok


def sel_blocks(x, n_sel):
    B, L = x.shape[:2]
    x = jnp.pad(x, ((0, 0), (0, n_sel * SEL_BLOCK - L), (0, 0), (0, 0)))
    return jnp.moveaxis(x.reshape(B, n_sel, SEL_BLOCK, NSA_G, NSA_HD), 3, 1)


def nsa_slc_attend(q, kb, vb, idx, sel_ok, q_pos):
    B, G = kb.shape[:2]
    bi = jnp.arange(B)[:, None, None, None]
    gi = jnp.arange(G)[None, :, None, None]
    kg = kb[bi, gi, idx]
    vg = vb[bi, gi, idx]
    s = jnp.einsum('bghqd,bgqnld->bghqnl', q, kg) * NSA_HD ** -0.5
    kpos = idx[..., None] * SEL_BLOCK + jnp.arange(SEL_BLOCK)
    mask = (kpos <= q_pos[None, None, :, None, None]) & sel_ok[..., None]
    sh = s.shape
    prob = masked_softmax(s.reshape(sh[:4] + (-1,)), mask.reshape(B, G, 1, sh[3], -1))
    return jnp.einsum('bghqnl,bgqnld->bghqd', prob.reshape(sh).astype(vg.dtype), vg)


def window_attend_banded(q, k, v):
    B, G, HPG, T, HD = q.shape
    nb = T // WIN_BLOCK
    npv = WINDOW // WIN_BLOCK
    pad = ((0, 0), (0, 0), (npv * WIN_BLOCK, 0), (0, 0))

    def band(z):
        zb = jnp.pad(z, pad).reshape(B, G, nb + npv, WIN_BLOCK, HD)
        return jnp.concatenate([zb[:, :, j:j + nb] for j in range(npv + 1)], axis=3)

    kb, vb = band(k), band(v)
    qb = q.reshape(B, G, HPG, nb, WIN_BLOCK, HD)
    s = jnp.einsum('bghiqd,bgikd->bghiqk', qb, kb) * NSA_HD ** -0.5
    blk = jnp.arange(nb)[:, None]
    q_pos = blk * WIN_BLOCK + jnp.arange(WIN_BLOCK)[None]
    k_pos = (blk - npv) * WIN_BLOCK + jnp.arange((npv + 1) * WIN_BLOCK)[None]
    diff = q_pos[:, :, None] - k_pos[:, None, :]
    mask = (diff >= 0) & (diff < WINDOW) & (k_pos[:, None, :] >= 0)
    prob = masked_softmax(s, mask)
    return jnp.einsum('bghiqk,bgikd->bghiqd', prob.astype(v.dtype), vb).reshape(B, G, HPG, T, HD)


def window_attend_cached(q, k, v, q_pos, k_pos):
    s = jnp.einsum('bghqd,blgd->bghql', q, k) * NSA_HD ** -0.5
    diff = q_pos[:, None] - k_pos[None, :]
    prob = masked_softmax(s, (diff >= 0) & (diff < WINDOW))
    return jnp.einsum('bghql,blgd->bghqd', prob.astype(v.dtype), v)


def retention_chunk(S, q, k, v):
    f32 = jnp.float32
    C = q.shape[1]
    lg = jnp.log1p(-jnp.exp2(-5.0 - jnp.arange(RET_H, dtype=f32)))
    i = jnp.arange(C, dtype=f32)
    diff = i[:, None] - i[None, :]
    causal = diff >= 0
    dmask = jnp.where(causal, jnp.exp(jnp.where(causal, diff, 0.0)[None] * lg[:, None, None]), 0.0)
    qf, kf, vf = q.astype(f32), k.astype(f32), v.astype(f32)
    s = jnp.einsum('bihd,bjhd->bhij', qf, kf) * dmask
    o = jnp.einsum('bhij,bjhe->bihe', s, vf)
    o = o + jnp.einsum('bihd,bhde->bihe', qf, S) * jnp.exp((i[:, None] + 1.0) * lg[None, :])[None, :, :, None]
    k_dec = kf * jnp.exp((C - 1.0 - i)[:, None] * lg[None, :])[None, :, :, None]
    S = S * jnp.exp(C * lg)[None, :, None, None] + jnp.einsum('bjhd,bjhe->bhde', k_dec, vf)
    return S, o


def retention_prompt(q, k, v):
    B, T = q.shape[:2]
    n = T // RET_CHUNK
    xs = tuple(jnp.moveaxis(z.reshape((B, n, RET_CHUNK) + z.shape[2:]), 1, 0) for z in (q, k, v))
    s0 = jnp.zeros((B, RET_H, RET_DK, RET_DV), jnp.float32)
    S, o = lax.scan(lambda S, c: retention_chunk(S, c[0], c[1], c[2]), s0, xs)
    return S, jnp.moveaxis(o, 0, 1).reshape(B, T, RET_H, RET_DV)


def odd_output(o_cmp, o_slc, o_win, o_ret, pr, p):
    gates = pr['gates']
    B, T = gates.shape[:2]
    gg = jnp.moveaxis(gates.reshape(B, T, NSA_G, NSA_HPG, 3), 1, 3)[..., None]
    o = gg[..., 0, :] * o_cmp + gg[..., 1, :] * o_slc + gg[..., 2, :] * o_win
    o_nsa = jnp.moveaxis(o, 3, 1).reshape(B, T, NSA_W)
    y_ret = head_group_norm(o_ret, p['gn_g'], p['gn_b'], 1e-5).astype(o_nsa.dtype) * jax.nn.silu(pr['rg'])
    return jnp.concatenate([o_nsa, y_ret], axis=-1) @ p['w_out']


def odd_mixer_prompt(h, p):
    B, T, _ = h.shape
    pos = jnp.arange(T)
    pr = odd_project(h, p, pos)
    qn, qr = to_groups_q(pr['q_n']), to_groups_q(pr['q_r'])
    o_cmp, p_cmp = nsa_compressed_branch(qn, pr['kc'], pr['vc'], p, pos)
    n_sel = -(-T // SEL_BLOCK)
    idx, sel_ok = nsa_select(p_cmp, pos, n_sel)
    kb, vb = sel_blocks(pr['ks'], n_sel), sel_blocks(pr['vs'], n_sel)
    nqb = T // SEL_Q_BLOCK
    k_top = idx.shape[-1]
    q_blk = jnp.moveaxis(qr.reshape(B, NSA_G, NSA_HPG, nqb, SEL_Q_BLOCK, NSA_HD), 3, 0)
    i_blk = jnp.moveaxis(idx.reshape(B, NSA_G, nqb, SEL_Q_BLOCK, k_top), 2, 0)
    m_blk = jnp.moveaxis(sel_ok.reshape(B, NSA_G, nqb, SEL_Q_BLOCK, k_top), 2, 0)
    p_blk = pos.reshape(nqb, SEL_Q_BLOCK)
    o_slc = lax.map(lambda a: nsa_slc_attend(a[0], kb, vb, a[1], a[2], a[3]), (q_blk, i_blk, m_blk, p_blk))
    o_slc = jnp.moveaxis(o_slc, 0, 3).reshape(B, NSA_G, NSA_HPG, T, NSA_HD)
    o_win = window_attend_banded(qr, to_groups_k(pr['kw']), to_groups_k(pr['vw']))
    S, o_ret = retention_prompt(pr['rq'], pr['rk'], pr['rv'])
    out = odd_output(o_cmp, o_slc, o_win, o_ret, pr, p)
    kv_rows = jnp.stack([pr['kc'], pr['vc'], pr['ks'], pr['vs']], axis=2)
    win = jnp.stack([pr['kw'], pr['vw']], axis=2)[:, T - min(WINDOW, T):]
    return out, kv_rows, win, S


def odd_mixer_sample(h, p, past_kv, win_buf, ret_s0):
    B, T, _ = h.shape
    pos = PAST_LEN + jnp.arange(T)
    pr = odd_project(h, p, pos)
    qn, qr = to_groups_q(pr['q_n']), to_groups_q(pr['q_r'])
    rows = jnp.stack([pr['kc'], pr['vc'], pr['ks'], pr['vs']], axis=2).astype(past_kv.dtype)
    full = jnp.concatenate([past_kv, rows], axis=1)
    o_cmp, p_cmp = nsa_compressed_branch(qn, full[:, :, 0], full[:, :, 1], p, pos)
    n_sel = -(-full.shape[1] // SEL_BLOCK)
    idx, sel_ok = nsa_select(p_cmp, pos, n_sel)
    o_slc = nsa_slc_attend(qr, sel_blocks(full[:, :, 2], n_sel), sel_blocks(full[:, :, 3], n_sel),
                           idx, sel_ok, pos)
    nbuf = win_buf.shape[1]
    wfull = jnp.concatenate([win_buf, jnp.stack([pr['kw'], pr['vw']], axis=2).astype(win_buf.dtype)], axis=1)
    k_pos = PAST_LEN - nbuf + jnp.arange(nbuf + T)
    o_win = window_attend_cached(qr, wfull[:, :, 0], wfull[:, :, 1], pos, k_pos)
    S, o_ret = retention_chunk(ret_s0.astype(jnp.float32), pr['rq'], pr['rk'], pr['rv'])
    out = odd_output(o_cmp, o_slc, o_win, o_ret, pr, p)
    return out, rows, wfull[:, T:], S


def _stack(xs, dt):
    return jnp.stack(xs).astype(dt)


def setup_inputs(seed: int = 0) -> dict:
    key = jax.random.key(seed)
    keys = iter(jax.random.split(key, 80))
    f32 = jnp.float32
    ne, no = (DEPTH + 1) // 2, DEPTH // 2
    n_pages = PAST_LEN // PAGE_SIZE
    n_pool = (DEC_BATCH * n_pages * 5) // 4
    win_buf = min(WINDOW, PAST_LEN)

    def nrm(shape, scale=1.0):
        return scale * jax.random.normal(next(keys), shape, f32)

    def gain(shape):
        return 1.0 + nrm(shape, 0.05)

    def unif(shape, lo, hi):
        return jax.random.uniform(next(keys), shape, f32, lo, hi)

    lam_a = unif((ne, LRU_W), 0.9, 0.999)
    perm = jax.random.permutation(next(keys), n_pool)
    return {
        'x_prompt': nrm((BATCH, SEQ, D_MODEL)),
        'x_sample': nrm((DEC_BATCH, DEC_SEQ, D_MODEL)),
        'state_lru_h': nrm((ne, DEC_BATCH, LRU_W), 0.5),
        'state_lru_conv': nrm((ne, DEC_BATCH, CONV_W - 1, LRU_W)),
        'state_rwkv_shift': nrm((ne, DEC_BATCH, SHIFT_W)),
        'state_rwkv_wkv': nrm((ne, DEC_BATCH, RWKV_H, RWKV_HD, RWKV_HD), 0.3),
        'cache_nsa_kv': nrm((no, n_pool, PAGE_SIZE, KV_SLOTS, NSA_G, NSA_HD)),
        'cache_nsa_win': nrm((no, DEC_BATCH, win_buf, 2, NSA_G, NSA_HD)),
        'state_ret': nrm((no, DEC_BATCH, RET_H, RET_DK, RET_DV)),
        'page_table': perm[:DEC_BATCH * n_pages].reshape(DEC_BATCH, n_pages).astype(jnp.int32),
        'norm_ffn1': gain((DEPTH, D_MODEL)),
        'ffn1_w_in': nrm((DEPTH, D_MODEL, 2 * D_FF), D_MODEL ** -0.5),
        'ffn1_w_out': nrm((DEPTH, D_FF, D_MODEL), D_FF ** -0.5),
        'norm_mix': gain((DEPTH, D_MODEL)),
        'norm_ffn2': gain((DEPTH, D_MODEL)),
        'ffn2_w_in': nrm((DEPTH, D_MODEL, 2 * D_FF), D_MODEL ** -0.5),
        'ffn2_w_out': nrm((DEPTH, D_FF, D_MODEL), D_FF ** -0.5),
        'ab_w_in': nrm((ne, D_MODEL, AB_COLS), D_MODEL ** -0.5),
        'lru_conv_w': nrm((ne, CONV_W, LRU_W), CONV_W ** -0.5),
        'lru_conv_b': nrm((ne, LRU_W), 0.01),
        'lru_wa': nrm((ne, LRU_BLOCKS, LRU_BS, LRU_BS), LRU_BS ** -0.5),
        'lru_ba': nrm((ne, LRU_W), 0.01),
        'lru_wx': nrm((ne, LRU_BLOCKS, LRU_BS, LRU_BS), LRU_BS ** -0.5),
        'lru_bx': nrm((ne, LRU_W), 0.01),
        'lru_lambda': jnp.log(lam_a) - jnp.log1p(-lam_a),
        'rwkv_mu': unif((ne, SHIFT_W), 0.0, 1.0),
        'rwkv_w0': unif((ne, RWKV_W), -6.0, 1.0),
        'rwkv_w2': nrm((ne, W_LORA, RWKV_W), 0.1 * W_LORA ** -0.5),
        'rwkv_a0': nrm((ne, RWKV_W), 0.1),
        'rwkv_a2': nrm((ne, A_LORA, RWKV_W), 0.1 * A_LORA ** -0.5),
        'rwkv_g2': nrm((ne, G_LORA, RWKV_W), G_LORA ** -0.5),
        'rwkv_k_k': 0.85 + nrm((ne, RWKV_W), 0.05),
        'rwkv_k_a': gain((ne, RWKV_W)),
        'rwkv_r_k': nrm((ne, RWKV_H, RWKV_HD), 0.1),
        'rwkv_ln_g': gain((ne, RWKV_W)),
        'rwkv_ln_b': nrm((ne, RWKV_W), 0.01),
        'ab_w_out': nrm((ne, LRU_W + RWKV_W, D_MODEL), (LRU_W + RWKV_W) ** -0.5),
        'cd_w_in': nrm((no, D_MODEL, CD_COLS), D_MODEL ** -0.5),
        'nsa_q_norm': gain((no, NSA_HD)),
        'nsa_k_norm': gain((no, 3, NSA_HD)),
        'cmp_k_w1': nrm((no, CMP_R, CMP_STRIDE * NSA_HD, CMP_HID), (CMP_BLOCK * NSA_HD) ** -0.5),
        'cmp_k_b1': nrm((no, CMP_HID), 0.01),
        'cmp_k_w2': nrm((no, CMP_HID, NSA_HD), CMP_HID ** -0.5),
        'cmp_k_b2': nrm((no, NSA_HD), 0.01),
        'cmp_v_w1': nrm((no, CMP_R, CMP_STRIDE * NSA_HD, CMP_HID), (CMP_BLOCK * NSA_HD) ** -0.5),
        'cmp_v_b1': nrm((no, CMP_HID), 0.01),
        'cmp_v_w2': nrm((no, CMP_HID, NSA_HD), CMP_HID ** -0.5),
        'cmp_v_b2': nrm((no, NSA_HD), 0.01),
        'ret_gn_g': gain((no, RET_W)),
        'ret_gn_b': nrm((no, RET_W), 0.01),
        'cd_w_out': nrm((no, NSA_W + RET_W, D_MODEL), (NSA_W + RET_W) ** -0.5),
    }


def reference(x_prompt, x_sample, state_lru_h, state_lru_conv, state_rwkv_shift, state_rwkv_wkv,
              cache_nsa_kv, cache_nsa_win, state_ret, page_table,
              norm_ffn1, ffn1_w_in, ffn1_w_out, norm_mix, norm_ffn2, ffn2_w_in, ffn2_w_out,
              ab_w_in, lru_conv_w, lru_conv_b, lru_wa, lru_ba, lru_wx, lru_bx, lru_lambda,
              rwkv_mu, rwkv_w0, rwkv_w2, rwkv_a0, rwkv_a2, rwkv_g2, rwkv_k_k, rwkv_k_a, rwkv_r_k,
              rwkv_ln_g, rwkv_ln_b, ab_w_out,
              cd_w_in, nsa_q_norm, nsa_k_norm, cmp_k_w1, cmp_k_b1, cmp_k_w2, cmp_k_b2,
              cmp_v_w1, cmp_v_b1, cmp_v_w2, cmp_v_b2, ret_gn_g, ret_gn_b, cd_w_out):
    dt = x_prompt.dtype
    B = x_prompt.shape[0]
    DB = x_sample.shape[0]
    yp, ys = x_prompt, x_sample
    lru_h_p, lru_h_s, lru_c_p, lru_c_s, sh_p, sh_s, wkv_p, wkv_s = [], [], [], [], [], [], [], []
    kv_p, kv_s, win_p, win_s, ret_p, ret_s = [], [], [], [], [], []
    for layer in range(DEPTH):
        li = layer // 2
        yp = yp + 0.5 * swiglu(rms_norm(yp, norm_ffn1[layer]), ffn1_w_in[layer], ffn1_w_out[layer])
        ys = ys + 0.5 * swiglu(rms_norm(ys, norm_ffn1[layer]), ffn1_w_in[layer], ffn1_w_out[layer])
        hp = rms_norm(yp, norm_mix[layer])
        hs = rms_norm(ys, norm_mix[layer])
        if layer % 2 == 0:
            p = {'w_in': ab_w_in[li], 'conv_w': lru_conv_w[li], 'conv_b': lru_conv_b[li],
                 'wa': lru_wa[li], 'ba': lru_ba[li], 'wx': lru_wx[li], 'bx': lru_bx[li], 'lam': lru_lambda[li],
                 'mu': rwkv_mu[li], 'w0': rwkv_w0[li], 'w2': rwkv_w2[li], 'a0': rwkv_a0[li], 'a2': rwkv_a2[li],
                 'g2': rwkv_g2[li], 'k_k': rwkv_k_k[li], 'k_a': rwkv_k_a[li], 'r_k': rwkv_r_k[li],
                 'ln_g': rwkv_ln_g[li], 'ln_b': rwkv_ln_b[li], 'w_out': ab_w_out[li]}
            mp, a0, a1, a2, a3 = even_mixer(hp, p, jnp.zeros((B, LRU_W), dt), jnp.zeros((B, CONV_W - 1, LRU_W), dt),
                                            jnp.zeros((B, SHIFT_W), dt), jnp.zeros((B, RWKV_H, RWKV_HD, RWKV_HD), dt))
            ms, b0, b1, b2, b3 = even_mixer(hs, p, state_lru_h[li], state_lru_conv[li],
                                            state_rwkv_shift[li], state_rwkv_wkv[li])
            lru_h_p.append(a0); lru_c_p.append(a1); sh_p.append(a2); wkv_p.append(a3)
            lru_h_s.append(b0); lru_c_s.append(b1); sh_s.append(b2); wkv_s.append(b3)
        else:
            p = {'w_in': cd_w_in[li], 'q_norm': nsa_q_norm[li], 'k_norm': nsa_k_norm[li],
                 'ck': (cmp_k_w1[li], cmp_k_b1[li], cmp_k_w2[li], cmp_k_b2[li]),
                 'cv': (cmp_v_w1[li], cmp_v_b1[li], cmp_v_w2[li], cmp_v_b2[li]),
                 'gn_g': ret_gn_g[li], 'gn_b': ret_gn_b[li], 'w_out': cd_w_out[li]}
            n_pages = page_table.shape[1]
            past = cache_nsa_kv[li][page_table].reshape(DB, n_pages * PAGE_SIZE, KV_SLOTS, NSA_G, NSA_HD)
            mp, a0, a1, a2 = odd_mixer_prompt(hp, p)
            ms, b0, b1, b2 = odd_mixer_sample(hs, p, past, cache_nsa_win[li], state_ret[li])
            kv_p.append(a0); win_p.append(a1); ret_p.append(a2)
            kv_s.append(b0); win_s.append(b1); ret_s.append(b2)
        yp = yp + mp
        ys = ys + ms
        yp = yp + 0.5 * swiglu(rms_norm(yp, norm_ffn2[layer]), ffn2_w_in[layer], ffn2_w_out[layer])
        ys = ys + 0.5 * swiglu(rms_norm(ys, norm_ffn2[layer]), ffn2_w_in[layer], ffn2_w_out[layer])
    return (yp, ys,
            _stack(lru_h_p, dt), _stack(lru_h_s, dt), _stack(lru_c_p, dt), _stack(lru_c_s, dt),
            _stack(sh_p, dt), _stack(sh_s, dt), _stack(wkv_p, dt), _stack(wkv_s, dt),
            _stack(kv_p, dt), _stack(kv_s, dt), _stack(win_p, dt), _stack(win_s, dt),
            _stack(ret_p, dt), _stack(ret_s, dt))
```

```python
import functools

import jax
import jax.numpy as jnp
import numpy as np
from jax import lax
from jax.experimental import pallas as pl
from jax.experimental.pallas import tpu as pltpu

D_MODEL = 2048
BATCH = 4
SEQ = 2048
DEPTH = 2
DEC_BATCH = 128
DEC_SEQ = 1
PAST_LEN = 2048
PAGE_SIZE = 128
D_FF = 5504
LRU_W = D_MODEL // 2
LRU_BLOCKS = 16
LRU_BS = LRU_W // LRU_BLOCKS
CONV_W = 4
LRU_C = 8.0
RWKV_W = D_MODEL // 2
RWKV_HD = 64
RWKV_H = RWKV_W // RWKV_HD
W_LORA = 64
A_LORA = 64
G_LORA = 160
SHIFT_W = 3 * RWKV_W + W_LORA + A_LORA + G_LORA
AB_COLS = 2 * LRU_W + SHIFT_W
NSA_H = 16
NSA_G = 4
NSA_HPG = NSA_H // NSA_G
NSA_HD = 64
NSA_W = NSA_H * NSA_HD
ROPE_DIMS = NSA_HD // 4
ROPE_THETA = 500000.0
CMP_BLOCK = 32
CMP_STRIDE = 16
CMP_R = CMP_BLOCK // CMP_STRIDE
CMP_HID = 256
SEL_BLOCK = 64
SEL_TOP = 16
SEL_Q_BLOCK = 64
WINDOW = 512
WIN_BLOCK = 128
FORCE_SCORE = 1e4
KV_SLOTS = 4
RET_H = 8
RET_DK = 64
RET_DV = 128
RET_W = RET_H * RET_DV
RET_CHUNK = 128
RET_THETA = 10000.0
CD_COLS = NSA_W + 6 * NSA_G * NSA_HD + 3 * NSA_H + 2 * RET_H * RET_DK + 2 * RET_W

N_TOK = BATCH * SEQ + DEC_BATCH * DEC_SEQ
N_PROMPT = BATCH * SEQ

LANE = 128
VMEM_LIMIT_BYTES = 56 * 1024 * 1024
ROW_TILE = 640
FF_TILE = 512
D_FF_PAD = 5632
COL_TILE = 512

BF16 = jnp.bfloat16
F32 = jnp.float32


def _round_up(n, m):
    return -(-n // m) * m


def _rms_rows(x, g):
    ms = jnp.mean(x * x, axis=-1, keepdims=True)
    return x * lax.rsqrt(ms + 1e-6) * g


def _ffn_kernel(x_ref, g_ref, wg_ref, wu_ref, wo_ref, o_ref, xn_ref, acc_ref):
    k = pl.program_id(1)

    @pl.when(k == 0)
    def _():
        xn_ref[...] = _rms_rows(x_ref[...], g_ref[...]).astype(BF16)
        acc_ref[...] = jnp.zeros_like(acc_ref)

    xn = xn_ref[...]
    gate = jnp.dot(xn, wg_ref[...], preferred_element_type=F32)
    up = jnp.dot(xn, wu_ref[...], preferred_element_type=F32)
    act = gate * jax.nn.sigmoid(gate) * up
    acc_ref[...] += jnp.dot(act.astype(BF16), wo_ref[...], preferred_element_type=F32)

    @pl.when(k == pl.num_programs(1) - 1)
    def _():
        o_ref[...] = x_ref[...] + 0.5 * acc_ref[...]


def ffn_block(x, g, wg, wu, wo):
    m, d = x.shape
    return pl.pallas_call(
        _ffn_kernel,
        grid=(m // ROW_TILE, D_FF_PAD // FF_TILE),
        in_specs=[
            pl.BlockSpec((ROW_TILE, d), lambda i, k: (i, 0)),
            pl.BlockSpec((1, d), lambda i, k: (0, 0)),
            pl.BlockSpec((d, FF_TILE), lambda i, k: (0, k)),
            pl.BlockSpec((d, FF_TILE), lambda i, k: (0, k)),
            pl.BlockSpec((FF_TILE, d), lambda i, k: (k, 0)),
        ],
        out_specs=pl.BlockSpec((ROW_TILE, d), lambda i, k: (i, 0)),
        out_shape=jax.ShapeDtypeStruct((m, d), F32),
        scratch_shapes=[pltpu.VMEM((ROW_TILE, d), BF16), pltpu.VMEM((ROW_TILE, d), F32)],
        compiler_params=pltpu.CompilerParams(
            dimension_semantics=("parallel", "arbitrary"), vmem_limit_bytes=VMEM_LIMIT_BYTES),
        name="ffn_block",
    )(x, g.reshape(1, d), wg, wu, wo)


def _norm_matmul_kernel(x_ref, g_ref, w_ref, o_ref, xn_ref):
    @pl.when(pl.program_id(1) == 0)
    def _():
        xn_ref[...] = _rms_rows(x_ref[...], g_ref[...]).astype(BF16)

    o_ref[...] = jnp.dot(xn_ref[...], w_ref[...], preferred_element_type=F32)


def norm_matmul(x, g, w):
    m, k = x.shape
    n = w.shape[1]
    return pl.pallas_call(
        _norm_matmul_kernel,
        grid=(m // ROW_TILE, n // COL_TILE),
        in_specs=[
            pl.BlockSpec((ROW_TILE, k), lambda i, j: (i, 0)),
            pl.BlockSpec((1, k), lambda i, j: (0, 0)),
            pl.BlockSpec((k, COL_TILE), lambda i, j: (0, j)),
        ],
        out_specs=pl.BlockSpec((ROW_TILE, COL_TILE), lambda i, j: (i, j)),
        out_shape=jax.ShapeDtypeStruct((m, n), F32),
        scratch_shapes=[pltpu.VMEM((ROW_TILE, k), BF16)],
        compiler_params=pltpu.CompilerParams(
            dimension_semantics=("parallel", "arbitrary"), vmem_limit_bytes=VMEM_LIMIT_BYTES),
        name="norm_matmul",
    )(x, g.reshape(1, k), w)


def _matmul_residual_kernel(a_ref, w_ref, r_ref, o_ref):
    o_ref[...] = r_ref[...] + jnp.dot(a_ref[...].astype(BF16), w_ref[...], preferred_element_type=F32)


def matmul_residual(a, w, res):
    m, k = a.shape
    n = w.shape[1]
    return pl.pallas_call(
        _matmul_residual_kernel,
        grid=(m // ROW_TILE, n // COL_TILE),
        in_specs=[
            pl.BlockSpec((ROW_TILE, k), lambda i, j: (i, 0)),
            pl.BlockSpec((k, COL_TILE), lambda i, j: (0, j)),
            pl.BlockSpec((ROW_TILE, COL_TILE), lambda i, j: (i, j)),
        ],
        out_specs=pl.BlockSpec((ROW_TILE, COL_TILE), lambda i, j: (i, j)),
        out_shape=jax.ShapeDtypeStruct((m, n), F32),
        compiler_params=pltpu.CompilerParams(
            dimension_semantics=("parallel", "arbitrary"), vmem_limit_bytes=VMEM_LIMIT_BYTES),
        name="matmul_residual",
    )(a, w, res)


def _prep_ffn_weights(w_in, w_out):
    pad = D_FF_PAD - D_FF
    wg = jnp.pad(w_in[:, :D_FF], ((0, 0), (0, pad))).astype(BF16)
    wu = jnp.pad(w_in[:, D_FF:], ((0, 0), (0, pad))).astype(BF16)
    wo = jnp.pad(w_out, ((0, pad), (0, 0))).astype(BF16)
    return wg, wu, wo


def _prep_cols(w):
    n = w.shape[1]
    return jnp.pad(w, ((0, 0), (0, _round_up(n, COL_TILE) - n))).astype(BF16)


def rms_norm(x, g, eps=1e-6):
    xf = x.astype(jnp.float32)
    y = xf * lax.rsqrt(jnp.mean(xf * xf, axis=-1, keepdims=True) + eps)
    return (y * g.astype(jnp.float32)).astype(x.dtype)


def head_group_norm(y, g, b, eps):
    yf = y.astype(jnp.float32)
    mu = jnp.mean(yf, axis=-1, keepdims=True)
    var = jnp.mean(jnp.square(yf - mu), axis=-1, keepdims=True)
    yn = ((yf - mu) * lax.rsqrt(var + eps)).reshape(y.shape[:-2] + (-1,))
    return (yn * g.astype(jnp.float32) + b.astype(jnp.float32)).astype(y.dtype)


def masked_softmax(s, mask):
    s = jnp.where(mask, s.astype(jnp.float32), -jnp.inf)
    m = jnp.max(s, axis=-1, keepdims=True)
    e = jnp.exp(s - jnp.where(jnp.isfinite(m), m, 0.0))
    den = jnp.sum(e, axis=-1, keepdims=True)
    return e / jnp.where(den > 0, den, 1.0)


def rope(x, pos, n_rot, theta):
    half = n_rot // 2
    inv = jnp.exp(-jnp.log(jnp.float32(theta)) * jnp.arange(half, dtype=jnp.float32) / half)
    ang = pos.astype(jnp.float32)[:, None] * inv[None, :]
    cos = jnp.cos(ang)[None, :, None, :]
    sin = jnp.sin(ang)[None, :, None, :]
    xf = x.astype(jnp.float32)
    x1, x2 = xf[..., :half], xf[..., half:n_rot]
    out = jnp.concatenate([x1 * cos - x2 * sin, x2 * cos + x1 * sin, xf[..., n_rot:]], axis=-1)
    return out.astype(x.dtype)


def linear_scan(a, b, h0):
    b = b.at[:, 0].add(a[:, 0] * h0)

    def combine(left, right):
        return left[0] * right[0], right[0] * left[1] + right[1]

    return lax.associative_scan(combine, (a, b), axis=1)[1]


def wkv7_scan(r, w, k, v, a, b, s0):
    xs = tuple(jnp.moveaxis(z.astype(jnp.float32), 1, 0) for z in (r, w, k, v, a, b))

    def step(S, inp):
        r_t, w_t, k_t, v_t, a_t, b_t = inp
        sa = jnp.einsum('bhij,bhj->bhi', S, a_t)
        S = S * w_t[:, :, None, :] + sa[..., None] * b_t[:, :, None, :] + v_t[..., None] * k_t[:, :, None, :]
        return S, jnp.einsum('bhij,bhj->bhi', S, r_t)

    S, ys = lax.scan(step, s0.astype(jnp.float32), xs)
    return jnp.moveaxis(ys, 0, 1), S


def even_mixer_core(proj, p, lru_h0, lru_conv0, shift0, wkv0):
    B, T, _ = proj.shape
    f32 = jnp.float32
    dt = proj.dtype
    xb, gb, rw = jnp.split(proj, [LRU_W, 2 * LRU_W], axis=-1)
    xcat = jnp.concatenate([lru_conv0.astype(dt), xb], axis=1)
    xc = p['conv_b'] + sum(p['conv_w'][j] * xcat[:, j:j + T] for j in range(CONV_W))
    xbd = xc.reshape(B, T, LRU_BLOCKS, LRU_BS)
    gate_r = jax.nn.sigmoid(jnp.einsum('btnc,ncd->btnd', xbd, p['wa']).reshape(B, T, LRU_W) + p['ba'])
    gate_i = jax.nn.sigmoid(jnp.einsum('btnc,ncd->btnd', xbd, p['wx']).reshape(B, T, LRU_W) + p['bx'])
    log_a = -LRU_C * gate_r.astype(f32) * jax.nn.softplus(-p['lam'].astype(f32))
    u = jnp.sqrt(-jnp.expm1(2.0 * log_a)) * (gate_i * xc).astype(f32)
    hs = linear_scan(jnp.exp(log_a), u, lru_h0.astype(f32))
    y_lru = hs.astype(dt) * jax.nn.gelu(gb)
    prev = jnp.concatenate([shift0.astype(dt)[:, None], rw[:, :-1]], axis=1)
    rs = rw + p['mu'] * (prev - rw)
    r, k, v, xw, xa, xg = jnp.split(
        rs, [RWKV_W, 2 * RWKV_W, 3 * RWKV_W, 3 * RWKV_W + W_LORA, 3 * RWKV_W + W_LORA + A_LORA], axis=-1)
    w_log = -jax.nn.softplus(-(p['w0'] + jnp.tanh(xw) @ p['w2']).astype(f32)) - 0.5
    decay = jnp.exp(-jnp.exp(w_log))
    a_icl = jax.nn.sigmoid(p['a0'] + xa @ p['a2'])
    g = jax.nn.sigmoid(xg) @ p['g2']
    heads = (B, T, RWKV_H, RWKV_HD)
    kk = (k * p['k_k']).reshape(heads).astype(f32)
    kk = kk / jnp.maximum(jnp.sqrt(jnp.sum(kk * kk, axis=-1, keepdims=True)), 1e-12)
    k = k * (1.0 + (a_icl - 1.0) * p['k_a'])
    rh, kh, vh, ah = (z.reshape(heads) for z in (r, k, v, a_icl))
    y, wkv = wkv7_scan(rh, decay.reshape(heads), kh, vh, -kk, kk * ah.astype(f32), wkv0)
    y = head_group_norm(y, p['ln_g'], p['ln_b'], 64e-5).astype(dt)
    bonus = (jnp.sum(rh * kh * p['r_k'], axis=-1, keepdims=True) * vh).reshape(B, T, RWKV_W)
    y_rwkv = (y + bonus) * g
    cat = jnp.concatenate([y_lru, y_rwkv], axis=-1)
    return cat, hs[:, -1], xcat[:, T:], rw[:, -1], wkv


def odd_project(proj, p, pos):
    B, T, _ = proj.shape
    sizes = [NSA_W] + [NSA_G * NSA_HD] * 6 + [3 * NSA_H, RET_H * RET_DK, RET_H * RET_DK, RET_W, RET_W]
    q, kc, vc, ks, vs, kw, vw, gt, rq, rk, rv, rg = jnp.split(
        proj, np.cumsum(sizes)[:-1].tolist(), axis=-1)
    kvs = (B, T, NSA_G, NSA_HD)
    q_n = rms_norm(q.reshape(B, T, NSA_H, NSA_HD), p['q_norm'])
    return {
        'q_n': q_n,
        'q_r': rope(q_n, pos, ROPE_DIMS, ROPE_THETA),
        'kc': kc.reshape(kvs), 'vc': vc.reshape(kvs),
        'ks': rope(rms_norm(ks.reshape(kvs), p['k_norm'][1]), pos, ROPE_DIMS, ROPE_THETA),
        'vs': vs.reshape(kvs),
        'kw': rope(rms_norm(kw.reshape(kvs), p['k_norm'][2]), pos, ROPE_DIMS, ROPE_THETA),
        'vw': vw.reshape(kvs),
        'gates': jax.nn.sigmoid(gt).reshape(B, T, NSA_H, 3),
        'rq': rope(rq.reshape(B, T, RET_H, RET_DK), pos, RET_DK, RET_THETA),
        'rk': rope(rk.reshape(B, T, RET_H, RET_DK), pos, RET_DK, RET_THETA) * (RET_DK ** -0.5),
        'rv': rv.reshape(B, T, RET_H, RET_DV),
        'rg': rg,
    }


def to_groups_q(q):
    B, T = q.shape[:2]
    return jnp.moveaxis(q.reshape(B, T, NSA_G, NSA_HPG, NSA_HD), 1, 3)


def to_groups_k(k):
    return jnp.moveaxis(k, 1, 2)


def nsa_compress(x, w1, b1, w2, b2):
    B, L = x.shape[:2]
    n_chunk = L // CMP_STRIDE
    n_cmp = n_chunk - CMP_R + 1
    ch = x[:, :n_chunk * CMP_STRIDE].reshape(B, n_chunk, CMP_STRIDE, NSA_G, NSA_HD)
    ch = jnp.moveaxis(ch, 3, 2).reshape(B, n_chunk, NSA_G, CMP_STRIDE * NSA_HD)
    part = jnp.einsum('bngc,rch->bngrh', ch, w1)
    pre = b1 + sum(part[:, m:m + n_cmp, :, m] for m in range(CMP_R))
    return jax.nn.gelu(pre) @ w2 + b2


def nsa_compressed_branch(qn, kc_raw, vc_raw, p, q_pos):
    kc = to_groups_k(rms_norm(nsa_compress(kc_raw, *p['ck']), p['k_norm'][0]))
    vc = to_groups_k(nsa_compress(vc_raw, *p['cv']))
    s = jnp.einsum('bghqd,bgcd->bghqc', qn, kc) * NSA_HD ** -0.5
    ends = jnp.arange(kc.shape[2]) * CMP_STRIDE + CMP_BLOCK - 1
    prob = masked_softmax(s, ends[None, :] <= q_pos[:, None])
    return jnp.einsum('bghqc,bgcd->bghqd', prob.astype(vc.dtype), vc), prob


def cmp_sel_overlap(n_cmp, n_sel):
    cs = np.arange(n_cmp) * CMP_STRIDE
    ss = np.arange(n_sel) * SEL_BLOCK
    ov = np.minimum(cs[None] + CMP_BLOCK, ss[:, None] + SEL_BLOCK) - np.maximum(cs[None], ss[:, None])
    return jnp.asarray(np.clip(ov, 0, None) / CMP_BLOCK, dtype=jnp.float32)


def nsa_select(p_cmp, q_pos, n_sel):
    imp = jnp.einsum('bgqc,sc->bgqs', p_cmp.sum(axis=2), cmp_sel_overlap(p_cmp.shape[-1], n_sel))
    j = jnp.arange(n_sel)[None, :]
    qb = (q_pos // SEL_BLOCK)[:, None]
    valid = j <= qb
    forced = (j == 0) | (j == qb) | (j == qb - 1)
    score = jnp.where(valid, jnp.where(forced, FORCE_SCORE, imp), -jnp.inf)
    _, idx = lax.top_k(score, min(SEL_TOP, n_sel))
    sel_ok = jnp.take_along_axis(jnp.broadcast_to(valid, score.shape), idx, axis=-1)
    return idx, sel_ok


def sel_blocks(x, n_sel):
    B, L = x.shape[:2]
    x = jnp.pad(x, ((0, 0), (0, n_sel * SEL_BLOCK - L), (0, 0), (0, 0)))
    return jnp.moveaxis(x.reshape(B, n_sel, SEL_BLOCK, NSA_G, NSA_HD), 3, 1)


def nsa_slc_attend(q, kb, vb, idx, sel_ok, q_pos):
    B, G = kb.shape[:2]
    bi = jnp.arange(B)[:, None, None, None]
    gi = jnp.arange(G)[None, :, None, None]
    kg = kb[bi, gi, idx]
    vg = vb[bi, gi, idx]
    s = jnp.einsum('bghqd,bgqnld->bghqnl', q, kg) * NSA_HD ** -0.5
    kpos = idx[..., None] * SEL_BLOCK + jnp.arange(SEL_BLOCK)
    mask = (kpos <= q_pos[None, None, :, None, None]) & sel_ok[..., None]
    sh = s.shape
    prob = masked_softmax(s.reshape(sh[:4] + (-1,)), mask.reshape(B, G, 1, sh[3], -1))
    return jnp.einsum('bghqnl,bgqnld->bghqd', prob.reshape(sh).astype(vg.dtype), vg)


def window_attend_banded(q, k, v):
    B, G, HPG, T, HD = q.shape
    nb = T // WIN_BLOCK
    npv = WINDOW // WIN_BLOCK
    pad = ((0, 0), (0, 0), (npv * WIN_BLOCK, 0), (0, 0))

    def band(z):
        zb = jnp.pad(z, pad).reshape(B, G, nb + npv, WIN_BLOCK, HD)
        return jnp.concatenate([zb[:, :, j:j + nb] for j in range(npv + 1)], axis=3)

    kb, vb = band(k), band(v)
    qb = q.reshape(B, G, HPG, nb, WIN_BLOCK, HD)
    s = jnp.einsum('bghiqd,bgikd->bghiqk', qb, kb) * NSA_HD ** -0.5
    blk = jnp.arange(nb)[:, None]
    q_pos = blk * WIN_BLOCK + jnp.arange(WIN_BLOCK)[None]
    k_pos = (blk - npv) * WIN_BLOCK + jnp.arange((npv + 1) * WIN_BLOCK)[None]
    diff = q_pos[:, :, None] - k_pos[:, None, :]
    mask = (diff >= 0) & (diff < WINDOW) & (k_pos[:, None, :] >= 0)
    prob = masked_softmax(s, mask)
    return jnp.einsum('bghiqk,bgikd->bghiqd', prob.astype(v.dtype), vb).reshape(B, G, HPG, T, HD)


def window_attend_cached(q, k, v, q_pos, k_pos):
    s = jnp.einsum('bghqd,blgd->bghql', q, k) * NSA_HD ** -0.5
    diff = q_pos[:, None] - k_pos[None, :]
    prob = masked_softmax(s, (diff >= 0) & (diff < WINDOW))
    return jnp.einsum('bghql,blgd->bghqd', prob.astype(v.dtype), v)


def retention_chunk(S, q, k, v):
    f32 = jnp.float32
    C = q.shape[1]
    lg = jnp.log1p(-jnp.exp2(-5.0 - jnp.arange(RET_H, dtype=f32)))
    i = jnp.arange(C, dtype=f32)
    diff = i[:, None] - i[None, :]
    causal = diff >= 0
    dmask = jnp.where(causal, jnp.exp(jnp.where(causal, diff, 0.0)[None] * lg[:, None, None]), 0.0)
    qf, kf, vf = q.astype(f32), k.astype(f32), v.astype(f32)
    s = jnp.einsum('bihd,bjhd->bhij', qf, kf) * dmask
    o = jnp.einsum('bhij,bjhe->bihe', s, vf)
    o = o + jnp.einsum('bihd,bhde->bihe', qf, S) * jnp.exp((i[:, None] + 1.0) * lg[None, :])[None, :, :, None]
    k_dec = kf * jnp.exp((C - 1.0 - i)[:, None] * lg[None, :])[None, :, :, None]
    S = S * jnp.exp(C * lg)[None, :, None, None] + jnp.einsum('bjhd,bjhe->bhde', k_dec, vf)
    return S, o


def retention_prompt(q, k, v):
    B, T = q.shape[:2]
    n = T // RET_CHUNK
    xs = tuple(jnp.moveaxis(z.reshape((B, n, RET_CHUNK) + z.shape[2:]), 1, 0) for z in (q, k, v))
    s0 = jnp.zeros((B, RET_H, RET_DK, RET_DV), jnp.float32)
    S, o = lax.scan(lambda S, c: retention_chunk(S, c[0], c[1], c[2]), s0, xs)
    return S, jnp.moveaxis(o, 0, 1).reshape(B, T, RET_H, RET_DV)


def odd_output(o_cmp, o_slc, o_win, o_ret, pr, p):
    gates = pr['gates']
    B, T = gates.shape[:2]
    gg = jnp.moveaxis(gates.reshape(B, T, NSA_G, NSA_HPG, 3), 1, 3)[..., None]
    o = gg[..., 0, :] * o_cmp + gg[..., 1, :] * o_slc + gg[..., 2, :] * o_win
    o_nsa = jnp.moveaxis(o, 3, 1).reshape(B, T, NSA_W)
    y_ret = head_group_norm(o_ret, p['gn_g'], p['gn_b'], 1e-5).astype(o_nsa.dtype) * jax.nn.silu(pr['rg'])
    return jnp.concatenate([o_nsa, y_ret], axis=-1)


def odd_mixer_prompt(proj, p):
    B, T, _ = proj.shape
    pos = jnp.arange(T)
    pr = odd_project(proj, p, pos)
    qn, qr = to_groups_q(pr['q_n']), to_groups_q(pr['q_r'])
    o_cmp, p_cmp = nsa_compressed_branch(qn, pr['kc'], pr['vc'], p, pos)
    n_sel = -(-T // SEL_BLOCK)
    idx, sel_ok = nsa_select(p_cmp, pos, n_sel)
    kb, vb = sel_blocks(pr['ks'], n_sel), sel_blocks(pr['vs'], n_sel)
    nqb = T // SEL_Q_BLOCK
    k_top = idx.shape[-1]
    q_blk = jnp.moveaxis(qr.reshape(B, NSA_G, NSA_HPG, nqb, SEL_Q_BLOCK, NSA_HD), 3, 0)
    i_blk = jnp.moveaxis(idx.reshape(B, NSA_G, nqb, SEL_Q_BLOCK, k_top), 2, 0)
    m_blk = jnp.moveaxis(sel_ok.reshape(B, NSA_G, nqb, SEL_Q_BLOCK, k_top), 2, 0)
    p_blk = pos.reshape(nqb, SEL_Q_BLOCK)
    o_slc = lax.map(lambda a: nsa_slc_attend(a[0], kb, vb, a[1], a[2], a[3]), (q_blk, i_blk, m_blk, p_blk))
    o_slc = jnp.moveaxis(o_slc, 0, 3).reshape(B, NSA_G, NSA_HPG, T, NSA_HD)
    o_win = window_attend_banded(qr, to_groups_k(pr['kw']), to_groups_k(pr['vw']))
    S, o_ret = retention_prompt(pr['rq'], pr['rk'], pr['rv'])
    out = odd_output(o_cmp, o_slc, o_win, o_ret, pr, p)
    kv_rows = jnp.stack([pr['kc'], pr['vc'], pr['ks'], pr['vs']], axis=2)
    win = jnp.stack([pr['kw'], pr['vw']], axis=2)[:, T - min(WINDOW, T):]
    return out, kv_rows, win, S


def odd_mixer_sample(proj, p, past_kv, win_buf, ret_s0):
    B, T, _ = proj.shape
    pos = PAST_LEN + jnp.arange(T)
    pr = odd_project(proj, p, pos)
    qn, qr = to_groups_q(pr['q_n']), to_groups_q(pr['q_r'])
    rows = jnp.stack([pr['kc'], pr['vc'], pr['ks'], pr['vs']], axis=2).astype(past_kv.dtype)
    full = jnp.concatenate([past_kv, rows], axis=1)
    o_cmp, p_cmp = nsa_compressed_branch(qn, full[:, :, 0], full[:, :, 1], p, pos)
    n_sel = -(-full.shape[1] // SEL_BLOCK)
    idx, sel_ok = nsa_select(p_cmp, pos, n_sel)
    o_slc = nsa_slc_attend(qr, sel_blocks(full[:, :, 2], n_sel), sel_blocks(full[:, :, 3], n_sel),
                           idx, sel_ok, pos)
    nbuf = win_buf.shape[1]
    wfull = jnp.concatenate([win_buf, jnp.stack([pr['kw'], pr['vw']], axis=2).astype(win_buf.dtype)], axis=1)
    k_pos = PAST_LEN - nbuf + jnp.arange(nbuf + T)
    o_win = window_attend_cached(qr, wfull[:, :, 0], wfull[:, :, 1], pos, k_pos)
    S, o_ret = retention_chunk(ret_s0.astype(jnp.float32), pr['rq'], pr['rk'], pr['rv'])
    out = odd_output(o_cmp, o_slc, o_win, o_ret, pr, p)
    return out, rows, wfull[:, T:], S


def _stack(xs, dt):
    return jnp.stack(xs).astype(dt)


def kernel(x_prompt, x_sample, state_lru_h, state_lru_conv, state_rwkv_shift, state_rwkv_wkv,
           cache_nsa_kv, cache_nsa_win, state_ret, page_table,
           norm_ffn1, ffn1_w_in, ffn1_w_out, norm_mix, norm_ffn2, ffn2_w_in, ffn2_w_out,
           ab_w_in, lru_conv_w, lru_conv_b, lru_wa, lru_ba, lru_wx, lru_bx, lru_lambda,
           rwkv_mu, rwkv_w0, rwkv_w2, rwkv_a0, rwkv_a2, rwkv_g2, rwkv_k_k, rwkv_k_a, rwkv_r_k,
           rwkv_ln_g, rwkv_ln_b, ab_w_out,
           cd_w_in, nsa_q_norm, nsa_k_norm, cmp_k_w1, cmp_k_b1, cmp_k_w2, cmp_k_b2,
           cmp_v_w1, cmp_v_b1, cmp_v_w2, cmp_v_b2, ret_gn_g, ret_gn_b, cd_w_out):
    dt = x_prompt.dtype
    B = x_prompt.shape[0]
    DB = x_sample.shape[0]
    y = jnp.concatenate([x_prompt.reshape(N_PROMPT, D_MODEL), x_sample.reshape(DB * DEC_SEQ, D_MODEL)], axis=0)
    lru_h_p, lru_h_s, lru_c_p, lru_c_s, sh_p, sh_s, wkv_p, wkv_s = [], [], [], [], [], [], [], []
    kv_p, kv_s, win_p, win_s, ret_p, ret_s = [], [], [], [], [], []
    for layer in range(DEPTH):
        li = layer // 2
        y = ffn_block(y, norm_ffn1[layer], *_prep_ffn_weights(ffn1_w_in[layer], ffn1_w_out[layer]))
        if layer % 2 == 0:
            p = {'conv_w': lru_conv_w[li], 'conv_b': lru_conv_b[li],
                 'wa': lru_wa[li], 'ba': lru_ba[li], 'wx': lru_wx[li], 'bx': lru_bx[li], 'lam': lru_lambda[li],
                 'mu': rwkv_mu[li], 'w0': rwkv_w0[li], 'w2': rwkv_w2[li], 'a0': rwkv_a0[li], 'a2': rwkv_a2[li],
                 'g2': rwkv_g2[li], 'k_k': rwkv_k_k[li], 'k_a': rwkv_k_a[li], 'r_k': rwkv_r_k[li],
                 'ln_g': rwkv_ln_g[li], 'ln_b': rwkv_ln_b[li]}
            proj = norm_matmul(y, norm_mix[layer], _prep_cols(ab_w_in[li]))[:, :AB_COLS]
            proj_p = proj[:N_PROMPT].reshape(B, SEQ, AB_COLS)
            proj_s = proj[N_PROMPT:].reshape(DB, DEC_SEQ, AB_COLS)
            cp, a0, a1, a2, a3 = even_mixer_core(
                proj_p, p, jnp.zeros((B, LRU_W), dt), jnp.zeros((B, CONV_W - 1, LRU_W), dt),
                jnp.zeros((B, SHIFT_W), dt), jnp.zeros((B, RWKV_H, RWKV_HD, RWKV_HD), dt))
            cs, b0, b1, b2, b3 = even_mixer_core(
                proj_s, p, state_lru_h[li], state_lru_conv[li], state_rwkv_shift[li], state_rwkv_wkv[li])
            lru_h_p.append(a0); lru_c_p.append(a1); sh_p.append(a2); wkv_p.append(a3)
            lru_h_s.append(b0); lru_c_s.append(b1); sh_s.append(b2); wkv_s.append(b3)
            w_out = ab_w_out[li]
        else:
            p = {'q_norm': nsa_q_norm[li], 'k_norm': nsa_k_norm[li],
                 'ck': (cmp_k_w1[li], cmp_k_b1[li], cmp_k_w2[li], cmp_k_b2[li]),
                 'cv': (cmp_v_w1[li], cmp_v_b1[li], cmp_v_w2[li], cmp_v_b2[li]),
                 'gn_g': ret_gn_g[li], 'gn_b': ret_gn_b[li]}
            proj = norm_matmul(y, norm_mix[layer], _prep_cols(cd_w_in[li]))[:, :CD_COLS]
            proj_p = proj[:N_PROMPT].reshape(B, SEQ, CD_COLS)
            proj_s = proj[N_PROMPT:].reshape(DB, DEC_SEQ, CD_COLS)
            n_pages = page_table.shape[1]
            past = cache_nsa_kv[li][page_table].reshape(DB, n_pages * PAGE_SIZE, KV_SLOTS, NSA_G, NSA_HD)
            cp, a0, a1, a2 = odd_mixer_prompt(proj_p, p)
            cs, b0, b1, b2 = odd_mixer_sample(proj_s, p, past, cache_nsa_win[li], state_ret[li])
            kv_p.append(a0); win_p.append(a1); ret_p.append(a2)
            kv_s.append(b0); win_s.append(b1); ret_s.append(b2)
            w_out = cd_w_out[li]
        cat = jnp.concatenate([cp.reshape(N_PROMPT, D_MODEL), cs.reshape(DB * DEC_SEQ, D_MODEL)], axis=0)
        y = matmul_residual(cat, w_out.astype(BF16), y)
        y = ffn_block(y, norm_ffn2[layer], *_prep_ffn_weights(ffn2_w_in[layer], ffn2_w_out[layer]))
    yp = y[:N_PROMPT].reshape(B, SEQ, D_MODEL)
    ys = y[N_PROMPT:].reshape(DB, DEC_SEQ, D_MODEL)
    return (yp, ys,
            _stack(lru_h_p, dt), _stack(lru_h_s, dt), _stack(lru_c_p, dt), _stack(lru_c_s, dt),
            _stack(sh_p, dt), _stack(sh_s, dt), _stack(wkv_p, dt), _stack(wkv_s, dt),
            _stack(kv_p, dt), _stack(kv_s, dt), _stack(win_p, dt), _stack(win_s, dt),
            _stack(ret_p, dt), _stack(ret_s, dt))
```

```python
import functools

import jax
import jax.numpy as jnp
import numpy as np
from jax import lax
from jax.experimental import pallas as pl
from jax.experimental.pallas import tpu as pltpu

D_MODEL = 2048
BATCH = 4
SEQ = 2048
DEPTH = 2
DEC_BATCH = 128
DEC_SEQ = 1
PAST_LEN = 2048
PAGE_SIZE = 128
D_FF = 5504
LRU_W = D_MODEL // 2
LRU_BLOCKS = 16
LRU_BS = LRU_W // LRU_BLOCKS
CONV_W = 4
LRU_C = 8.0
RWKV_W = D_MODEL // 2
RWKV_HD = 64
RWKV_H = RWKV_W // RWKV_HD
W_LORA = 64
A_LORA = 64
G_LORA = 160
SHIFT_W = 3 * RWKV_W + W_LORA + A_LORA + G_LORA
AB_COLS = 2 * LRU_W + SHIFT_W
NSA_H = 16
NSA_G = 4
NSA_HPG = NSA_H // NSA_G
NSA_HD = 64
NSA_W = NSA_H * NSA_HD
ROPE_DIMS = NSA_HD // 4
ROPE_THETA = 500000.0
CMP_BLOCK = 32
CMP_STRIDE = 16
CMP_R = CMP_BLOCK // CMP_STRIDE
CMP_HID = 256
SEL_BLOCK = 64
SEL_TOP = 16
SEL_Q_BLOCK = 64
WINDOW = 512
WIN_BLOCK = 128
FORCE_SCORE = 1e4
KV_SLOTS = 4
RET_H = 8
RET_DK = 64
RET_DV = 128
RET_W = RET_H * RET_DV
RET_CHUNK = 128
RET_THETA = 10000.0
CD_COLS = NSA_W + 6 * NSA_G * NSA_HD + 3 * NSA_H + 2 * RET_H * RET_DK + 2 * RET_W

N_TOK = BATCH * SEQ + DEC_BATCH * DEC_SEQ
N_PROMPT = BATCH * SEQ

LANE = 128
VMEM_LIMIT_BYTES = 56 * 1024 * 1024
ROW_TILE = 640
FF_TILE = 512
D_FF_PAD = 5632
COL_TILE = 512

BF16 = jnp.bfloat16
F32 = jnp.float32


def _round_up(n, m):
    return -(-n // m) * m


def _rms_rows(x, g):
    ms = jnp.mean(x * x, axis=-1, keepdims=True)
    return x * lax.rsqrt(ms + 1e-6) * g


def _ffn_kernel(x_ref, g_ref, wg_ref, wu_ref, wo_ref, o_ref, xn_ref, acc_ref):
    k = pl.program_id(1)

    @pl.when(k == 0)
    def _():
        xn_ref[...] = _rms_rows(x_ref[...], g_ref[...]).astype(BF16)
        acc_ref[...] = jnp.zeros_like(acc_ref)

    xn = xn_ref[...]
    gate = jnp.dot(xn, wg_ref[...], preferred_element_type=F32)
    up = jnp.dot(xn, wu_ref[...], preferred_element_type=F32)
    act = gate * jax.nn.sigmoid(gate) * up
    acc_ref[...] += jnp.dot(act.astype(BF16), wo_ref[...], preferred_element_type=F32)

    @pl.when(k == pl.num_programs(1) - 1)
    def _():
        o_ref[...] = x_ref[...] + 0.5 * acc_ref[...]


def ffn_block(x, g, wg, wu, wo):
    m, d = x.shape
    return pl.pallas_call(
        _ffn_kernel,
        grid=(m // ROW_TILE, D_FF_PAD // FF_TILE),
        in_specs=[
            pl.BlockSpec((ROW_TILE, d), lambda i, k: (i, 0)),
            pl.BlockSpec((1, d), lambda i, k: (0, 0)),
            pl.BlockSpec((d, FF_TILE), lambda i, k: (0, k)),
            pl.BlockSpec((d, FF_TILE), lambda i, k: (0, k)),
            pl.BlockSpec((FF_TILE, d), lambda i, k: (k, 0)),
        ],
        out_specs=pl.BlockSpec((ROW_TILE, d), lambda i, k: (i, 0)),
        out_shape=jax.ShapeDtypeStruct((m, d), F32),
        scratch_shapes=[pltpu.VMEM((ROW_TILE, d), BF16), pltpu.VMEM((ROW_TILE, d), F32)],
        compiler_params=pltpu.CompilerParams(
            dimension_semantics=("parallel", "arbitrary"), vmem_limit_bytes=VMEM_LIMIT_BYTES),
        name="ffn_block",
    )(x, g.reshape(1, d), wg, wu, wo)


def _norm_matmul_kernel(x_ref, g_ref, w_ref, o_ref, xn_ref):
    @pl.when(pl.program_id(1) == 0)
    def _():
        xn_ref[...] = _rms_rows(x_ref[...], g_ref[...]).astype(BF16)

    o_ref[...] = jnp.dot(xn_ref[...], w_ref[...], preferred_element_type=F32)


def norm_matmul(x, g, w):
    m, k = x.shape
    n = w.shape[1]
    return pl.pallas_call(
        _norm_matmul_kernel,
        grid=(m // ROW_TILE, n // COL_TILE),
        in_specs=[
            pl.BlockSpec((ROW_TILE, k), lambda i, j: (i, 0)),
            pl.BlockSpec((1, k), lambda i, j: (0, 0)),
            pl.BlockSpec((k, COL_TILE), lambda i, j: (0, j)),
        ],
        out_specs=pl.BlockSpec((ROW_TILE, COL_TILE), lambda i, j: (i, j)),
        out_shape=jax.ShapeDtypeStruct((m, n), F32),
        scratch_shapes=[pltpu.VMEM((ROW_TILE, k), BF16)],
        compiler_params=pltpu.CompilerParams(
            dimension_semantics=("parallel", "arbitrary"), vmem_limit_bytes=VMEM_LIMIT_BYTES),
        name="norm_matmul",
    )(x, g.reshape(1, k), w)


def _matmul_residual_kernel(a_ref, w_ref, r_ref, o_ref):
    o_ref[...] = r_ref[...] + jnp.dot(a_ref[...].astype(BF16), w_ref[...], preferred_element_type=F32)


def matmul_residual(a, w, res):
    m, k = a.shape
    n = w.shape[1]
    return pl.pallas_call(
        _matmul_residual_kernel,
        grid=(m // ROW_TILE, n // COL_TILE),
        in_specs=[
            pl.BlockSpec((ROW_TILE, k), lambda i, j: (i, 0)),
            pl.BlockSpec((k, COL_TILE), lambda i, j: (0, j)),
            pl.BlockSpec((ROW_TILE, COL_TILE), lambda i, j: (i, j)),
        ],
        out_specs=pl.BlockSpec((ROW_TILE, COL_TILE), lambda i, j: (i, j)),
        out_shape=jax.ShapeDtypeStruct((m, n), F32),
        compiler_params=pltpu.CompilerParams(
            dimension_semantics=("parallel", "arbitrary"), vmem_limit_bytes=VMEM_LIMIT_BYTES),
        name="matmul_residual",
    )(a, w, res)


def _prep_ffn_weights(w_in, w_out):
    pad = D_FF_PAD - D_FF
    wg = jnp.pad(w_in[:, :D_FF], ((0, 0), (0, pad))).astype(BF16)
    wu = jnp.pad(w_in[:, D_FF:], ((0, 0), (0, pad))).astype(BF16)
    wo = jnp.pad(w_out, ((0, pad), (0, 0))).astype(BF16)
    return wg, wu, wo


def _prep_cols(w):
    n = w.shape[1]
    return jnp.pad(w, ((0, 0), (0, _round_up(n, COL_TILE) - n))).astype(BF16)


SCAN_TILE = 256


def _lru_scan_kernel(a_ref, b_ref, h0_ref, o_ref, carry_ref):
    @pl.when(pl.program_id(1) == 0)
    def _():
        carry_ref[...] = h0_ref[...]

    a = a_ref[...]
    b = b_ref[...]
    rows = lax.broadcasted_iota(jnp.int32, a.shape, 0)
    k = 1
    while k < a.shape[0]:
        keep = rows >= k
        b = jnp.where(keep, a * pltpu.roll(b, k, 0) + b, b)
        a = jnp.where(keep, a * pltpu.roll(a, k, 0), a)
        k *= 2
    h = a * carry_ref[...] + b
    o_ref[...] = h
    carry_ref[...] = h[a.shape[0] - 1:, :]


def lru_scan(a, b, h0):
    B, T, W = a.shape
    tt = min(SCAN_TILE, T)
    return pl.pallas_call(
        _lru_scan_kernel,
        grid=(B, T // tt),
        in_specs=[
            pl.BlockSpec((None, tt, W), lambda i, t: (i, t, 0)),
            pl.BlockSpec((None, tt, W), lambda i, t: (i, t, 0)),
            pl.BlockSpec((None, 1, W), lambda i, t: (i, 0, 0)),
        ],
        out_specs=pl.BlockSpec((None, tt, W), lambda i, t: (i, t, 0)),
        out_shape=jax.ShapeDtypeStruct((B, T, W), F32),
        scratch_shapes=[pltpu.VMEM((1, W), F32)],
        compiler_params=pltpu.CompilerParams(
            dimension_semantics=("parallel", "arbitrary"), vmem_limit_bytes=VMEM_LIMIT_BYTES),
        name="lru_scan",
    )(a, b, h0.reshape(B, 1, W))


GROUP_W = NSA_HPG * NSA_HD
ATT_Q_TILE = 128
ATT_K_TILE = 256
CMP_PAD = 128
NEG_BIG = -1e30


def _stack_heads(q):
    head = lax.broadcasted_iota(jnp.int32, q.shape, 1) // NSA_HD
    return jnp.concatenate([jnp.where(head == h, q, 0.0) for h in range(NSA_HPG)], axis=0)


def _unstack_heads(o, tq):
    head = lax.broadcasted_iota(jnp.int32, (tq, GROUP_W), 1) // NSA_HD
    out = jnp.zeros((tq, GROUP_W), F32)
    for h in range(NSA_HPG):
        out = out + jnp.where(head == h, o[h * tq:(h + 1) * tq], 0.0)
    return out


def _cmp_select_kernel(q_ref, k_ref, v_ref, ov_ref, o_ref, sel_ref, *, n_cmp, n_sel, q_pos0):
    tq = q_ref.shape[0]
    i = pl.program_id(2)
    qs = _stack_heads(q_ref[...] * (NSA_HD ** -0.5)).astype(BF16)
    s = lax.dot_general(qs, k_ref[...], (((1,), (1,)), ((), ())), preferred_element_type=F32)
    q_pos = q_pos0 + i * tq + lax.broadcasted_iota(jnp.int32, (tq, CMP_PAD), 0)
    c = lax.broadcasted_iota(jnp.int32, (tq, CMP_PAD), 1)
    mask1 = (c < n_cmp) & (c * CMP_STRIDE + (CMP_BLOCK - 1) <= q_pos)
    mask = jnp.concatenate([mask1] * NSA_HPG, axis=0)
    s = jnp.where(mask, s, NEG_BIG)
    m = jnp.max(s, axis=-1, keepdims=True)
    e = jnp.where(mask, jnp.exp(s - m), 0.0)
    den = jnp.sum(e, axis=-1, keepdims=True)
    prob = e / jnp.where(den > 0, den, 1.0)
    o = jnp.dot(prob.astype(BF16), v_ref[...], preferred_element_type=F32)
    o_ref[...] = _unstack_heads(o, tq)
    psum = prob[0:tq]
    for h in range(1, NSA_HPG):
        psum = psum + prob[h * tq:(h + 1) * tq]
    imp = jnp.dot(psum.astype(BF16), ov_ref[...], preferred_element_type=F32)
    qb = q_pos // SEL_BLOCK
    valid = (c <= qb) & (c < n_sel)
    forced = (c == 0) | (c == qb) | (c == qb - 1)
    score = jnp.where(valid, jnp.where(forced, FORCE_SCORE, imp), -jnp.inf)
    rank = jnp.zeros((tq, CMP_PAD), F32)
    for jp in range(n_sel):
        col = score[:, jp:jp + 1]
        beats = (col > score) | ((col == score) & (c > jp))
        rank = rank + jnp.where(beats, 1.0, 0.0)
    sel_ref[...] = jnp.where((rank < min(SEL_TOP, n_sel)) & (c < n_sel), 1.0, 0.0)


def nsa_cmp_select(qn, kc4, vc4, ovT, *, n_cmp, n_sel, q_pos0):
    B, T, _ = qn.shape
    tq = min(ATT_Q_TILE, T)
    return pl.pallas_call(
        functools.partial(_cmp_select_kernel, n_cmp=n_cmp, n_sel=n_sel, q_pos0=q_pos0),
        grid=(B, NSA_G, T // tq),
        in_specs=[
            pl.BlockSpec((None, tq, GROUP_W), lambda b, g, i: (b, i, g)),
            pl.BlockSpec((None, None, CMP_PAD, GROUP_W), lambda b, g, i: (b, g, 0, 0)),
            pl.BlockSpec((None, None, CMP_PAD, GROUP_W), lambda b, g, i: (b, g, 0, 0)),
            pl.BlockSpec((CMP_PAD, CMP_PAD), lambda b, g, i: (0, 0)),
        ],
        out_specs=[
            pl.BlockSpec((None, tq, GROUP_W), lambda b, g, i: (b, i, g)),
            pl.BlockSpec((None, None, tq, CMP_PAD), lambda b, g, i: (b, g, i, 0)),
        ],
        out_shape=[jax.ShapeDtypeStruct((B, T, NSA_W), F32),
                   jax.ShapeDtypeStruct((B, NSA_G, T, CMP_PAD), F32)],
        compiler_params=pltpu.CompilerParams(
            dimension_semantics=("parallel", "parallel", "parallel"), vmem_limit_bytes=VMEM_LIMIT_BYTES),
        name="nsa_cmp_select",
    )(qn, kc4, vc4, ovT)


def _flash_kernel(*refs, selected):
    if selected:
        q_ref, k_ref, v_ref, sel_ref, exp_ref, o_ref, m_ref, l_ref, acc_ref = refs
    else:
        q_ref, k_ref, v_ref, o_ref, m_ref, l_ref, acc_ref = refs
    tq = q_ref.shape[0]
    tk = ATT_K_TILE
    i = pl.program_id(2)
    qs = _stack_heads(q_ref[...] * (NSA_HD ** -0.5)).astype(BF16)
    m_ref[...] = jnp.full(m_ref.shape, NEG_BIG, F32)
    l_ref[...] = jnp.zeros(l_ref.shape, F32)
    acc_ref[...] = jnp.zeros(acc_ref.shape, F32)
    q_pos = i * tq + lax.broadcasted_iota(jnp.int32, (tq, tk), 0)
    col = lax.broadcasted_iota(jnp.int32, (tq, tk), 1)
    if selected:
        sel = sel_ref[...].astype(BF16)
        lo = 0
    else:
        lo = jnp.maximum(i * tq - (WINDOW - 1), 0) // tk
    hi = (i * tq + tq - 1) // tk + 1

    def body(j, carry):
        start = pl.multiple_of(j * tk, tk)
        k = k_ref[pl.ds(start, tk), :]
        v = v_ref[pl.ds(start, tk), :]
        s = lax.dot_general(qs, k, (((1,), (1,)), ((), ())), preferred_element_type=F32)
        k_pos = start + col
        mask1 = k_pos <= q_pos
        if selected:
            mask1 = mask1 & (jnp.dot(sel, exp_ref[j], preferred_element_type=F32) > 0.5)
        else:
            mask1 = mask1 & (q_pos - k_pos < WINDOW)
        mask = jnp.concatenate([mask1] * NSA_HPG, axis=0)
        s = jnp.where(mask, s, NEG_BIG)
        m_old = m_ref[...]
        m_new = jnp.maximum(m_old, jnp.max(s, axis=-1, keepdims=True))
        alpha = jnp.exp(m_old - m_new)
        p = jnp.where(mask, jnp.exp(s - m_new), 0.0)
        l_ref[...] = alpha * l_ref[...] + jnp.sum(p, axis=-1, keepdims=True)
        acc_ref[...] = alpha * acc_ref[...] + jnp.dot(p.astype(BF16), v, preferred_element_type=F32)
        m_ref[...] = m_new
        return carry

    lax.fori_loop(lo, hi, body, 0)
    den = l_ref[...]
    o_ref[...] = _unstack_heads(acc_ref[...] / jnp.where(den > 0, den, 1.0), tq)


def nsa_flash(qr, k4, v4, sel=None, expand=None):
    B, T, _ = qr.shape
    tq = ATT_Q_TILE
    selected = sel is not None
    in_specs = [
        pl.BlockSpec((None, tq, GROUP_W), lambda b, g, i: (b, i, g)),
        pl.BlockSpec((None, T, GROUP_W), lambda b, g, i: (b, 0, g)),
        pl.BlockSpec((None, T, GROUP_W), lambda b, g, i: (b, 0, g)),
    ]
    args = [qr, k4, v4]
    if selected:
        in_specs += [
            pl.BlockSpec((None, None, tq, CMP_PAD), lambda b, g, i: (b, g, i, 0)),
            pl.BlockSpec(expand.shape, lambda b, g, i: (0, 0, 0)),
        ]
        args += [sel, expand]
    return pl.pallas_call(
        functools.partial(_flash_kernel, selected=selected),
        grid=(B, NSA_G, T // tq),
        in_specs=in_specs,
        out_specs=pl.BlockSpec((None, tq, GROUP_W), lambda b, g, i: (b, i, g)),
        out_shape=jax.ShapeDtypeStruct((B, T, NSA_W), F32),
        scratch_shapes=[pltpu.VMEM((NSA_HPG * tq, 1), F32), pltpu.VMEM((NSA_HPG * tq, 1), F32),
                        pltpu.VMEM((NSA_HPG * tq, GROUP_W), F32)],
        compiler_params=pltpu.CompilerParams(
            dimension_semantics=("parallel", "parallel", "parallel"), vmem_limit_bytes=VMEM_LIMIT_BYTES),
        name="nsa_flash_sel" if selected else "nsa_flash_win",
    )(*args)


def _tile_groups(x):
    B, T = x.shape[:2]
    return jnp.broadcast_to(x[:, :, :, None, :], (B, T, NSA_G, NSA_HPG, NSA_HD)).reshape(B, T, NSA_W).astype(BF16)


def _tile_cmp(x):
    B, n = x.shape[:2]
    x = jnp.pad(jnp.moveaxis(x, 1, 2), ((0, 0), (0, 0), (0, CMP_PAD - n), (0, 0)))
    return jnp.tile(x, (1, 1, 1, NSA_HPG)).astype(BF16)


def _overlap_T(n_cmp, n_sel):
    ov = np.zeros((CMP_PAD, CMP_PAD), np.float32)
    cs = np.arange(n_cmp) * CMP_STRIDE
    ss = np.arange(n_sel) * SEL_BLOCK
    o = np.minimum(cs[None] + CMP_BLOCK, ss[:, None] + SEL_BLOCK) - np.maximum(cs[None], ss[:, None])
    ov[:n_cmp, :n_sel] = (np.clip(o, 0, None) / CMP_BLOCK).T
    return jnp.asarray(ov, dtype=BF16)


def _sel_expand(T):
    t = np.arange(T)
    e = (np.arange(CMP_PAD)[:, None] == (t // SEL_BLOCK)[None, :]).astype(np.float32)
    return jnp.asarray(e.reshape(CMP_PAD, T // ATT_K_TILE, ATT_K_TILE).transpose(1, 0, 2), dtype=BF16)


WKV_C = 64
WKV_PAIR = 2 * RWKV_HD
WKV_T_TILE = 512


def _split_bf16(x):
    hi = x.astype(BF16)
    return hi, (x - hi.astype(F32)).astype(BF16)


def _dot3(a, b):
    a_hi, a_lo = _split_bf16(a)
    b_hi, b_lo = _split_bf16(b)
    return (jnp.dot(a_hi, b_hi, preferred_element_type=F32) + jnp.dot(a_hi, b_lo, preferred_element_type=F32)
            + jnp.dot(a_lo, b_hi, preferred_element_type=F32))


def _wkv_kernel(r_ref, lw_ref, k_ref, v_ref, a_ref, b_ref, s0_ref, y_ref, sT_ref, s_scr):
    C = WKV_C
    P = WKV_PAIR
    n_chunks = r_ref.shape[0] // C

    @pl.when(pl.program_id(2) == 0)
    def _():
        s_scr[...] = s0_ref[...]

    lo_lane = lax.broadcasted_iota(jnp.int32, (C, P), 1) < RWKV_HD
    row = lax.broadcasted_iota(jnp.int32, (2 * C, 2 * C), 0)
    col = lax.broadcasted_iota(jnp.int32, (2 * C, 2 * C), 1)
    same_head = (row // C) == (col // C)
    strict = same_head & (row > col)
    lower = same_head & (row >= col)
    eye = jnp.where(row == col, 1.0, 0.0)
    tril = jnp.where(lax.broadcasted_iota(jnp.int32, (C, C), 0) >= lax.broadcasted_iota(jnp.int32, (C, C), 1),
                     1.0, 0.0).astype(BF16)

    def stack(x):
        return jnp.concatenate([jnp.where(lo_lane, x, 0.0), jnp.where(lo_lane, 0.0, x)], axis=0)

    def chunk(c, carry):
        sl = pl.ds(pl.multiple_of(c * C, C), C)
        r, lw, k, v, a, b = (ref[sl, :] for ref in (r_ref, lw_ref, k_ref, v_ref, a_ref, b_ref))
        lw_hi, lw_mid = _split_bf16(lw)
        lw_lo = (lw - lw_hi.astype(F32) - lw_mid.astype(F32)).astype(BF16)
        cs = (jnp.dot(tril, lw_hi, preferred_element_type=F32) + jnp.dot(tril, lw_mid, preferred_element_type=F32)
              + jnp.dot(tril, lw_lo, preferred_element_type=F32))
        g_inv = jnp.exp(-cs)
        g_end = jnp.exp(cs[C - 1:C, :] - cs)
        a2 = stack(a * jnp.exp(cs - lw))
        r2 = stack(r * jnp.exp(cs))
        b2 = stack(b * g_inv)
        k2 = stack(k * g_inv)
        v2 = stack(v)
        s_old = s_scr[...]
        ar = jnp.concatenate([a2, r2], axis=0).astype(BF16)
        bk = jnp.concatenate([b2, k2], axis=0).astype(BF16)
        nt = (((1,), (1,)), ((), ()))
        pp = lax.dot_general(ar, bk, nt, preferred_element_type=F32)
        from_state = lax.dot_general(ar, s_old.astype(BF16), nt, preferred_element_type=F32)
        l_ab = jnp.where(strict, pp[:2 * C, :2 * C], 0.0)
        l_ak = jnp.where(strict, pp[:2 * C, 2 * C:], 0.0)
        m_rb = jnp.where(lower, pp[2 * C:, :2 * C], 0.0)
        m_rk = jnp.where(lower, pp[2 * C:, 2 * C:], 0.0)
        v2b = v2.astype(BF16)
        rhs = from_state[:2 * C] + jnp.dot(l_ak.astype(BF16), v2b, preferred_element_type=F32)
        n = l_ab
        x = eye + n
        span = 2
        while span < C:
            n = _dot3(n, n)
            x = x + _dot3(n, x)
            span *= 2
        u2 = _dot3(x, rhs)
        uv = jnp.concatenate([u2, v2], axis=0).astype(BF16)
        y2 = from_state[2 * C:] + jnp.dot(jnp.concatenate([m_rb, m_rk], axis=1).astype(BF16), uv,
                                          preferred_element_type=F32)
        y_ref[sl, :] = y2[:C] + y2[C:]
        bk_end = jnp.concatenate([stack(b * g_end), stack(k * g_end)], axis=0).astype(BF16)
        s_scr[...] = s_old * jnp.exp(cs[C - 1:C, :]) + lax.dot_general(
            uv, bk_end, (((0,), (0,)), ((), ())), preferred_element_type=F32)
        return carry

    lax.fori_loop(0, n_chunks, chunk, 0)

    @pl.when(pl.program_id(2) == pl.num_programs(2) - 1)
    def _():
        sT_ref[...] = s_scr[...]


def wkv7_chunked(r, lw, k, v, a, b, s0):
    B, T, W = r.shape
    n_pair = W // WKV_PAIR
    tt = min(WKV_T_TILE, T)
    s0p = s0.astype(F32).reshape(B, n_pair, 2, RWKV_HD, RWKV_HD)
    zero = jnp.zeros_like(s0p[:, :, 0])
    s0_bd = jnp.concatenate([jnp.concatenate([s0p[:, :, 0], zero], axis=-1),
                             jnp.concatenate([zero, s0p[:, :, 1]], axis=-1)], axis=-2)
    seq = pl.BlockSpec((None, tt, WKV_PAIR), lambda i, p, t: (i, t, p))
    st = pl.BlockSpec((None, None, WKV_PAIR, WKV_PAIR), lambda i, p, t: (i, p, 0, 0))
    y, s_bd = pl.pallas_call(
        _wkv_kernel,
        grid=(B, n_pair, T // tt),
        in_specs=[seq] * 6 + [st],
        out_specs=[seq, st],
        out_shape=[jax.ShapeDtypeStruct((B, T, W), F32),
                   jax.ShapeDtypeStruct((B, n_pair, WKV_PAIR, WKV_PAIR), F32)],
        scratch_shapes=[pltpu.VMEM((WKV_PAIR, WKV_PAIR), F32)],
        compiler_params=pltpu.CompilerParams(
            dimension_semantics=("parallel", "parallel", "arbitrary"), vmem_limit_bytes=VMEM_LIMIT_BYTES),
        name="wkv7_chunked",
    )(r, lw, k, v, a, b, s0_bd)
    s_fin = jnp.stack([s_bd[:, :, :RWKV_HD, :RWKV_HD], s_bd[:, :, RWKV_HD:, RWKV_HD:]], axis=2)
    return y, s_fin.reshape(B, W // RWKV_HD, RWKV_HD, RWKV_HD)


def rms_norm(x, g, eps=1e-6):
    xf = x.astype(jnp.float32)
    y = xf * lax.rsqrt(jnp.mean(xf * xf, axis=-1, keepdims=True) + eps)
    return (y * g.astype(jnp.float32)).astype(x.dtype)


def head_group_norm(y, g, b, eps):
    yf = y.astype(jnp.float32)
    mu = jnp.mean(yf, axis=-1, keepdims=True)
    var = jnp.mean(jnp.square(yf - mu), axis=-1, keepdims=True)
    yn = ((yf - mu) * lax.rsqrt(var + eps)).reshape(y.shape[:-2] + (-1,))
    return (yn * g.astype(jnp.float32) + b.astype(jnp.float32)).astype(y.dtype)


def masked_softmax(s, mask):
    s = jnp.where(mask, s.astype(jnp.float32), -jnp.inf)
    m = jnp.max(s, axis=-1, keepdims=True)
    e = jnp.exp(s - jnp.where(jnp.isfinite(m), m, 0.0))
    den = jnp.sum(e, axis=-1, keepdims=True)
    return e / jnp.where(den > 0, den, 1.0)


def rope(x, pos, n_rot, theta):
    half = n_rot // 2
    inv = jnp.exp(-jnp.log(jnp.float32(theta)) * jnp.arange(half, dtype=jnp.float32) / half)
    ang = pos.astype(jnp.float32)[:, None] * inv[None, :]
    cos = jnp.cos(ang)[None, :, None, :]
    sin = jnp.sin(ang)[None, :, None, :]
    xf = x.astype(jnp.float32)
    x1, x2 = xf[..., :half], xf[..., half:n_rot]
    out = jnp.concatenate([x1 * cos - x2 * sin, x2 * cos + x1 * sin, xf[..., n_rot:]], axis=-1)
    return out.astype(x.dtype)


def linear_scan(a, b, h0):
    b = b.at[:, 0].add(a[:, 0] * h0)

    def combine(left, right):
        return left[0] * right[0], right[0] * left[1] + right[1]

    return lax.associative_scan(combine, (a, b), axis=1)[1]


def wkv7_scan(r, w, k, v, a, b, s0):
    xs = tuple(jnp.moveaxis(z.astype(jnp.float32), 1, 0) for z in (r, w, k, v, a, b))

    def step(S, inp):
        r_t, w_t, k_t, v_t, a_t, b_t = inp
        sa = jnp.einsum('bhij,bhj->bhi', S, a_t)
        S = S * w_t[:, :, None, :] + sa[..., None] * b_t[:, :, None, :] + v_t[..., None] * k_t[:, :, None, :]
        return S, jnp.einsum('bhij,bhj->bhi', S, r_t)

    S, ys = lax.scan(step, s0.astype(jnp.float32), xs)
    return jnp.moveaxis(ys, 0, 1), S


def even_mixer_core(proj, p, lru_h0, lru_conv0, shift0, wkv0):
    B, T, _ = proj.shape
    f32 = jnp.float32
    dt = proj.dtype
    xb, gb, rw = jnp.split(proj, [LRU_W, 2 * LRU_W], axis=-1)
    xcat = jnp.concatenate([lru_conv0.astype(dt), xb], axis=1)
    xc = p['conv_b'] + sum(p['conv_w'][j] * xcat[:, j:j + T] for j in range(CONV_W))
    xbd = xc.reshape(B, T, LRU_BLOCKS, LRU_BS)
    gate_r = jax.nn.sigmoid(jnp.einsum('btnc,ncd->btnd', xbd, p['wa']).reshape(B, T, LRU_W) + p['ba'])
    gate_i = jax.nn.sigmoid(jnp.einsum('btnc,ncd->btnd', xbd, p['wx']).reshape(B, T, LRU_W) + p['bx'])
    log_a = -LRU_C * gate_r.astype(f32) * jax.nn.softplus(-p['lam'].astype(f32))
    u = jnp.sqrt(-jnp.expm1(2.0 * log_a)) * (gate_i * xc).astype(f32)
    hs = lru_scan(jnp.exp(log_a), u, lru_h0.astype(f32))
    y_lru = hs.astype(dt) * jax.nn.gelu(gb)
    prev = jnp.concatenate([shift0.astype(dt)[:, None], rw[:, :-1]], axis=1)
    rs = rw + p['mu'] * (prev - rw)
    r, k, v, xw, xa, xg = jnp.split(
        rs, [RWKV_W, 2 * RWKV_W, 3 * RWKV_W, 3 * RWKV_W + W_LORA, 3 * RWKV_W + W_LORA + A_LORA], axis=-1)
    w_log = -jax.nn.softplus(-(p['w0'] + jnp.tanh(xw) @ p['w2']).astype(f32)) - 0.5
    log_decay = -jnp.exp(w_log)
    decay = jnp.exp(log_decay)
    a_icl = jax.nn.sigmoid(p['a0'] + xa @ p['a2'])
    g = jax.nn.sigmoid(xg) @ p['g2']
    heads = (B, T, RWKV_H, RWKV_HD)
    kk = (k * p['k_k']).reshape(heads).astype(f32)
    kk = kk / jnp.maximum(jnp.sqrt(jnp.sum(kk * kk, axis=-1, keepdims=True)), 1e-12)
    k = k * (1.0 + (a_icl - 1.0) * p['k_a'])
    rh, kh, vh, ah = (z.reshape(heads) for z in (r, k, v, a_icl))
    if T % WKV_C == 0:
        y, wkv = wkv7_chunked(r.astype(f32), log_decay, k.astype(f32), v.astype(f32),
                              (-kk).reshape(B, T, RWKV_W), (kk * ah.astype(f32)).reshape(B, T, RWKV_W), wkv0)
        y = y.reshape(heads)
    else:
        y, wkv = wkv7_scan(rh, decay.reshape(heads), kh, vh, -kk, kk * ah.astype(f32), wkv0)
    y = head_group_norm(y, p['ln_g'], p['ln_b'], 64e-5).astype(dt)
    bonus = (jnp.sum(rh * kh * p['r_k'], axis=-1, keepdims=True) * vh).reshape(B, T, RWKV_W)
    y_rwkv = (y + bonus) * g
    cat = jnp.concatenate([y_lru, y_rwkv], axis=-1)
    return cat, hs[:, -1], xcat[:, T:], rw[:, -1], wkv


def odd_project(proj, p, pos):
    B, T, _ = proj.shape
    sizes = [NSA_W] + [NSA_G * NSA_HD] * 6 + [3 * NSA_H, RET_H * RET_DK, RET_H * RET_DK, RET_W, RET_W]
    q, kc, vc, ks, vs, kw, vw, gt, rq, rk, rv, rg = jnp.split(
        proj, np.cumsum(sizes)[:-1].tolist(), axis=-1)
    kvs = (B, T, NSA_G, NSA_HD)
    q_n = rms_norm(q.reshape(B, T, NSA_H, NSA_HD), p['q_norm'])
    return {
        'q_n': q_n,
        'q_r': rope(q_n, pos, ROPE_DIMS, ROPE_THETA),
        'kc': kc.reshape(kvs), 'vc': vc.reshape(kvs),
        'ks': rope(rms_norm(ks.reshape(kvs), p['k_norm'][1]), pos, ROPE_DIMS, ROPE_THETA),
        'vs': vs.reshape(kvs),
        'kw': rope(rms_norm(kw.reshape(kvs), p['k_norm'][2]), pos, ROPE_DIMS, ROPE_THETA),
        'vw': vw.reshape(kvs),
        'gates': jax.nn.sigmoid(gt).reshape(B, T, NSA_H, 3),
        'rq': rope(rq.reshape(B, T, RET_H, RET_DK), pos, RET_DK, RET_THETA),
        'rk': rope(rk.reshape(B, T, RET_H, RET_DK), pos, RET_DK, RET_THETA) * (RET_DK ** -0.5),
        'rv': rv.reshape(B, T, RET_H, RET_DV),
        'rg': rg,
    }


def to_groups_q(q):
    B, T = q.shape[:2]
    return jnp.moveaxis(q.reshape(B, T, NSA_G, NSA_HPG, NSA_HD), 1, 3)


def to_groups_k(k):
    return jnp.moveaxis(k, 1, 2)


def nsa_compress(x, w1, b1, w2, b2):
    B, L = x.shape[:2]
    n_chunk = L // CMP_STRIDE
    n_cmp = n_chunk - CMP_R + 1
    ch = x[:, :n_chunk * CMP_STRIDE].reshape(B, n_chunk, CMP_STRIDE, NSA_G, NSA_HD)
    ch = jnp.moveaxis(ch, 3, 2).reshape(B, n_chunk, NSA_G, CMP_STRIDE * NSA_HD)
    part = jnp.einsum('bngc,rch->bngrh', ch, w1)
    pre = b1 + sum(part[:, m:m + n_cmp, :, m] for m in range(CMP_R))
    return jax.nn.gelu(pre) @ w2 + b2


def nsa_compressed_branch(qn, kc_raw, vc_raw, p, q_pos):
    kc = to_groups_k(rms_norm(nsa_compress(kc_raw, *p['ck']), p['k_norm'][0]))
    vc = to_groups_k(nsa_compress(vc_raw, *p['cv']))
    s = jnp.einsum('bghqd,bgcd->bghqc', qn, kc) * NSA_HD ** -0.5
    ends = jnp.arange(kc.shape[2]) * CMP_STRIDE + CMP_BLOCK - 1
    prob = masked_softmax(s, ends[None, :] <= q_pos[:, None])
    return jnp.einsum('bghqc,bgcd->bghqd', prob.astype(vc.dtype), vc), prob


def cmp_sel_overlap(n_cmp, n_sel):
    cs = np.arange(n_cmp) * CMP_STRIDE
    ss = np.arange(n_sel) * SEL_BLOCK
    ov = np.minimum(cs[None] + CMP_BLOCK, ss[:, None] + SEL_BLOCK) - np.maximum(cs[None], ss[:, None])
    return jnp.asarray(np.clip(ov, 0, None) / CMP_BLOCK, dtype=jnp.float32)


def nsa_select(p_cmp, q_pos, n_sel):
    imp = jnp.einsum('bgqc,sc->bgqs', p_cmp.sum(axis=2), cmp_sel_overlap(p_cmp.shape[-1], n_sel))
    j = jnp.arange(n_sel)[None, :]
    qb = (q_pos // SEL_BLOCK)[:, None]
    valid = j <= qb
    forced = (j == 0) | (j == qb) | (j == qb - 1)
    score = jnp.where(valid, jnp.where(forced, FORCE_SCORE, imp), -jnp.inf)
    _, idx = lax.top_k(score, min(SEL_TOP, n_sel))
    sel_ok = jnp.take_along_axis(jnp.broadcast_to(valid, score.shape), idx, axis=-1)
    return idx, sel_ok


def sel_blocks(x, n_sel):
    B, L = x.shape[:2]
    x = jnp.pad(x, ((0, 0), (0, n_sel * SEL_BLOCK - L), (0, 0), (0, 0)))
    return jnp.moveaxis(x.reshape(B, n_sel, SEL_BLOCK, NSA_G, NSA_HD), 3, 1)


def nsa_slc_attend(q, kb, vb, idx, sel_ok, q_pos):
    B, G = kb.shape[:2]
    bi = jnp.arange(B)[:, None, None, None]
    gi = jnp.arange(G)[None, :, None, None]
    kg = kb[bi, gi, idx]
    vg = vb[bi, gi, idx]
    s = jnp.einsum('bghqd,bgqnld->bghqnl', q, kg) * NSA_HD ** -0.5
    kpos = idx[..., None] * SEL_BLOCK + jnp.arange(SEL_BLOCK)
    mask = (kpos <= q_pos[None, None, :, None, None]) & sel_ok[..., None]
    sh = s.shape
    prob = masked_softmax(s.reshape(sh[:4] + (-1,)), mask.reshape(B, G, 1, sh[3], -1))
    return jnp.einsum('bghqnl,bgqnld->bghqd', prob.reshape(sh).astype(vg.dtype), vg)


def window_attend_banded(q, k, v):
    B, G, HPG, T, HD = q.shape
    nb = T // WIN_BLOCK
    npv = WINDOW // WIN_BLOCK
    pad = ((0, 0), (0, 0), (npv * WIN_BLOCK, 0), (0, 0))

    def band(z):
        zb = jnp.pad(z, pad).reshape(B, G, nb + npv, WIN_BLOCK, HD)
        return jnp.concatenate([zb[:, :, j:j + nb] for j in range(npv + 1)], axis=3)

    kb, vb = band(k), band(v)
    qb = q.reshape(B, G, HPG, nb, WIN_BLOCK, HD)
    s = jnp.einsum('bghiqd,bgikd->bghiqk', qb, kb) * NSA_HD ** -0.5
    blk = jnp.arange(nb)[:, None]
    q_pos = blk * WIN_BLOCK + jnp.arange(WIN_BLOCK)[None]
    k_pos = (blk - npv) * WIN_BLOCK + jnp.arange((npv + 1) * WIN_BLOCK)[None]
    diff = q_pos[:, :, None] - k_pos[:, None, :]
    mask = (diff >= 0) & (diff < WINDOW) & (k_pos[:, None, :] >= 0)
    prob = masked_softmax(s, mask)
    return jnp.einsum('bghiqk,bgikd->bghiqd', prob.astype(v.dtype), vb).reshape(B, G, HPG, T, HD)


def window_attend_cached(q, k, v, q_pos, k_pos):
    s = jnp.einsum('bghqd,blgd->bghql', q, k) * NSA_HD ** -0.5
    diff = q_pos[:, None] - k_pos[None, :]
    prob = masked_softmax(s, (diff >= 0) & (diff < WINDOW))
    return jnp.einsum('bghql,blgd->bghqd', prob.astype(v.dtype), v)


def retention_chunk(S, q, k, v):
    f32 = jnp.float32
    C = q.shape[1]
    lg = jnp.log1p(-jnp.exp2(-5.0 - jnp.arange(RET_H, dtype=f32)))
    i = jnp.arange(C, dtype=f32)
    diff = i[:, None] - i[None, :]
    causal = diff >= 0
    dmask = jnp.where(causal, jnp.exp(jnp.where(causal, diff, 0.0)[None] * lg[:, None, None]), 0.0)
    qf, kf, vf = q.astype(f32), k.astype(f32), v.astype(f32)
    s = jnp.einsum('bihd,bjhd->bhij', qf, kf) * dmask
    o = jnp.einsum('bhij,bjhe->bihe', s, vf)
    o = o + jnp.einsum('bihd,bhde->bihe', qf, S) * jnp.exp((i[:, None] + 1.0) * lg[None, :])[None, :, :, None]
    k_dec = kf * jnp.exp((C - 1.0 - i)[:, None] * lg[None, :])[None, :, :, None]
    S = S * jnp.exp(C * lg)[None, :, None, None] + jnp.einsum('bjhd,bjhe->bhde', k_dec, vf)
    return S, o


def retention_prompt(q, k, v):
    B, T = q.shape[:2]
    n = T // RET_CHUNK
    xs = tuple(jnp.moveaxis(z.reshape((B, n, RET_CHUNK) + z.shape[2:]), 1, 0) for z in (q, k, v))
    s0 = jnp.zeros((B, RET_H, RET_DK, RET_DV), jnp.float32)
    S, o = lax.scan(lambda S, c: retention_chunk(S, c[0], c[1], c[2]), s0, xs)
    return S, jnp.moveaxis(o, 0, 1).reshape(B, T, RET_H, RET_DV)


def odd_output(o_cmp, o_slc, o_win, o_ret, pr, p):
    gates = pr['gates']
    B, T = gates.shape[:2]
    gg = jnp.moveaxis(gates.reshape(B, T, NSA_G, NSA_HPG, 3), 1, 3)[..., None]
    o = gg[..., 0, :] * o_cmp + gg[..., 1, :] * o_slc + gg[..., 2, :] * o_win
    o_nsa = jnp.moveaxis(o, 3, 1).reshape(B, T, NSA_W)
    y_ret = head_group_norm(o_ret, p['gn_g'], p['gn_b'], 1e-5).astype(o_nsa.dtype) * jax.nn.silu(pr['rg'])
    return jnp.concatenate([o_nsa, y_ret], axis=-1)


def odd_mixer_prompt(proj, p):
    B, T, _ = proj.shape
    pos = jnp.arange(T)
    pr = odd_project(proj, p, pos)
    qn = pr['q_n'].reshape(B, T, NSA_W)
    qr = pr['q_r'].reshape(B, T, NSA_W)
    kc = rms_norm(nsa_compress(pr['kc'], *p['ck']), p['k_norm'][0])
    vc = nsa_compress(pr['vc'], *p['cv'])
    n_cmp = kc.shape[1]
    n_sel = -(-T // SEL_BLOCK)
    o_cmp, sel = nsa_cmp_select(qn, _tile_cmp(kc), _tile_cmp(vc), _overlap_T(n_cmp, n_sel),
                                n_cmp=n_cmp, n_sel=n_sel, q_pos0=0)
    o_slc = nsa_flash(qr, _tile_groups(pr['ks']), _tile_groups(pr['vs']), sel, _sel_expand(T))
    o_win = nsa_flash(qr, _tile_groups(pr['kw']), _tile_groups(pr['vw']))
    S, o_ret = retention_prompt(pr['rq'], pr['rk'], pr['rv'])
    gates = pr['gates']
    heads = (B, T, NSA_H, NSA_HD)
    o_nsa = (gates[..., 0:1] * o_cmp.reshape(heads) + gates[..., 1:2] * o_slc.reshape(heads)
             + gates[..., 2:3] * o_win.reshape(heads)).reshape(B, T, NSA_W)
    y_ret = head_group_norm(o_ret, p['gn_g'], p['gn_b'], 1e-5).astype(o_nsa.dtype) * jax.nn.silu(pr['rg'])
    out = jnp.concatenate([o_nsa, y_ret], axis=-1)
    kv_rows = jnp.stack([pr['kc'], pr['vc'], pr['ks'], pr['vs']], axis=2)
    win = jnp.stack([pr['kw'], pr['vw']], axis=2)[:, T - min(WINDOW, T):]
    return out, kv_rows, win, S


def odd_mixer_sample(proj, p, past_kv, win_buf, ret_s0):
    B, T, _ = proj.shape
    pos = PAST_LEN + jnp.arange(T)
    pr = odd_project(proj, p, pos)
    qn, qr = to_groups_q(pr['q_n']), to_groups_q(pr['q_r'])
    rows = jnp.stack([pr['kc'], pr['vc'], pr['ks'], pr['vs']], axis=2).astype(past_kv.dtype)
    full = jnp.concatenate([past_kv, rows], axis=1)
    o_cmp, p_cmp = nsa_compressed_branch(qn, full[:, :, 0], full[:, :, 1], p, pos)
    n_sel = -(-full.shape[1] // SEL_BLOCK)
    idx, sel_ok = nsa_select(p_cmp, pos, n_sel)
    o_slc = nsa_slc_attend(qr, sel_blocks(full[:, :, 2], n_sel), sel_blocks(full[:, :, 3], n_sel),
                           idx, sel_ok, pos)
    nbuf = win_buf.shape[1]
    wfull = jnp.concatenate([win_buf, jnp.stack([pr['kw'], pr['vw']], axis=2).astype(win_buf.dtype)], axis=1)
    k_pos = PAST_LEN - nbuf + jnp.arange(nbuf + T)
    o_win = window_attend_cached(qr, wfull[:, :, 0], wfull[:, :, 1], pos, k_pos)
    S, o_ret = retention_chunk(ret_s0.astype(jnp.float32), pr['rq'], pr['rk'], pr['rv'])
    out = odd_output(o_cmp, o_slc, o_win, o_ret, pr, p)
    return out, rows, wfull[:, T:], S


def _stack(xs, dt):
    return jnp.stack(xs).astype(dt)


def kernel(x_prompt, x_sample, state_lru_h, state_lru_conv, state_rwkv_shift, state_rwkv_wkv,
           cache_nsa_kv, cache_nsa_win, state_ret, page_table,
           norm_ffn1, ffn1_w_in, ffn1_w_out, norm_mix, norm_ffn2, ffn2_w_in, ffn2_w_out,
           ab_w_in, lru_conv_w, lru_conv_b, lru_wa, lru_ba, lru_wx, lru_bx, lru_lambda,
           rwkv_mu, rwkv_w0, rwkv_w2, rwkv_a0, rwkv_a2, rwkv_g2, rwkv_k_k, rwkv_k_a, rwkv_r_k,
           rwkv_ln_g, rwkv_ln_b, ab_w_out,
           cd_w_in, nsa_q_norm, nsa_k_norm, cmp_k_w1, cmp_k_b1, cmp_k_w2, cmp_k_b2,
           cmp_v_w1, cmp_v_b1, cmp_v_w2, cmp_v_b2, ret_gn_g, ret_gn_b, cd_w_out):
    dt = x_prompt.dtype
    B = x_prompt.shape[0]
    DB = x_sample.shape[0]
    y = jnp.concatenate([x_prompt.reshape(N_PROMPT, D_MODEL), x_sample.reshape(DB * DEC_SEQ, D_MODEL)], axis=0)
    lru_h_p, lru_h_s, lru_c_p, lru_c_s, sh_p, sh_s, wkv_p, wkv_s = [], [], [], [], [], [], [], []
    kv_p, kv_s, win_p, win_s, ret_p, ret_s = [], [], [], [], [], []
    for layer in range(DEPTH):
        li = layer // 2
        y = ffn_block(y, norm_ffn1[layer], *_prep_ffn_weights(ffn1_w_in[layer], ffn1_w_out[layer]))
        if layer % 2 == 0:
            p = {'conv_w': lru_conv_w[li], 'conv_b': lru_conv_b[li],
                 'wa': lru_wa[li], 'ba': lru_ba[li], 'wx': lru_wx[li], 'bx': lru_bx[li], 'lam': lru_lambda[li],
                 'mu': rwkv_mu[li], 'w0': rwkv_w0[li], 'w2': rwkv_w2[li], 'a0': rwkv_a0[li], 'a2': rwkv_a2[li],
                 'g2': rwkv_g2[li], 'k_k': rwkv_k_k[li], 'k_a': rwkv_k_a[li], 'r_k': rwkv_r_k[li],
                 'ln_g': rwkv_ln_g[li], 'ln_b': rwkv_ln_b[li]}
            proj = norm_matmul(y, norm_mix[layer], _prep_cols(ab_w_in[li]))[:, :AB_COLS]
            proj_p = proj[:N_PROMPT].reshape(B, SEQ, AB_COLS)
            proj_s = proj[N_PROMPT:].reshape(DB, DEC_SEQ, AB_COLS)
            cp, a0, a1, a2, a3 = even_mixer_core(
                proj_p, p, jnp.zeros((B, LRU_W), dt), jnp.zeros((B, CONV_W - 1, LRU_W), dt),
                jnp.zeros((B, SHIFT_W), dt), jnp.zeros((B, RWKV_H, RWKV_HD, RWKV_HD), dt))
            cs, b0, b1, b2, b3 = even_mixer_core(
                proj_s, p, state_lru_h[li], state_lru_conv[li], state_rwkv_shift[li], state_rwkv_wkv[li])
            lru_h_p.append(a0); lru_c_p.append(a1); sh_p.append(a2); wkv_p.append(a3)
            lru_h_s.append(b0); lru_c_s.append(b1); sh_s.append(b2); wkv_s.append(b3)
            w_out = ab_w_out[li]
        else:
            p = {'q_norm': nsa_q_norm[li], 'k_norm': nsa_k_norm[li],
                 'ck': (cmp_k_w1[li], cmp_k_b1[li], cmp_k_w2[li], cmp_k_b2[li]),
                 'cv': (cmp_v_w1[li], cmp_v_b1[li], cmp_v_w2[li], cmp_v_b2[li]),
                 'gn_g': ret_gn_g[li], 'gn_b': ret_gn_b[li]}
            proj = norm_matmul(y, norm_mix[layer], _prep_cols(cd_w_in[li]))[:, :CD_COLS]
            proj_p = proj[:N_PROMPT].reshape(B, SEQ, CD_COLS)
            proj_s = proj[N_PROMPT:].reshape(DB, DEC_SEQ, CD_COLS)
            n_pages = page_table.shape[1]
            past = cache_nsa_kv[li][page_table].reshape(DB, n_pages * PAGE_SIZE, KV_SLOTS, NSA_G, NSA_HD)
            cp, a0, a1, a2 = odd_mixer_prompt(proj_p, p)
            cs, b0, b1, b2 = odd_mixer_sample(proj_s, p, past, cache_nsa_win[li], state_ret[li])
            kv_p.append(a0); win_p.append(a1); ret_p.append(a2)
            kv_s.append(b0); win_s.append(b1); ret_s.append(b2)
            w_out = cd_w_out[li]
        cat = jnp.concatenate([cp.reshape(N_PROMPT, D_MODEL), cs.reshape(DB * DEC_SEQ, D_MODEL)], axis=0)
        y = matmul_residual(cat, w_out.astype(BF16), y)
        y = ffn_block(y, norm_ffn2[layer], *_prep_ffn_weights(ffn2_w_in[layer], ffn2_w_out[layer]))
    yp = y[:N_PROMPT].reshape(B, SEQ, D_MODEL)
    ys = y[N_PROMPT:].reshape(DB, DEC_SEQ, D_MODEL)
    return (yp, ys,
            _stack(lru_h_p, dt), _stack(lru_h_s, dt), _stack(lru_c_p, dt), _stack(lru_c_s, dt),
            _stack(sh_p, dt), _stack(sh_s, dt), _stack(wkv_p, dt), _stack(wkv_s, dt),
            _stack(kv_p, dt), _stack(kv_s, dt), _stack(win_p, dt), _stack(win_s, dt),
            _stack(ret_p, dt), _stack(ret_s, dt))
```

```python
import functools

import jax
import jax.numpy as jnp
import numpy as np
from jax import lax
from jax.experimental import pallas as pl
from jax.experimental.pallas import tpu as pltpu

D_MODEL = 2048
BATCH = 4
SEQ = 2048
DEPTH = 2
DEC_BATCH = 128
DEC_SEQ = 1
PAST_LEN = 2048
PAGE_SIZE = 128
D_FF = 5504
LRU_W = D_MODEL // 2
LRU_BLOCKS = 16
LRU_BS = LRU_W // LRU_BLOCKS
CONV_W = 4
LRU_C = 8.0
RWKV_W = D_MODEL // 2
RWKV_HD = 64
RWKV_H = RWKV_W // RWKV_HD
W_LORA = 64
A_LORA = 64
G_LORA = 160
SHIFT_W = 3 * RWKV_W + W_LORA + A_LORA + G_LORA
AB_COLS = 2 * LRU_W + SHIFT_W
NSA_H = 16
NSA_G = 4
NSA_HPG = NSA_H // NSA_G
NSA_HD = 64
NSA_W = NSA_H * NSA_HD
ROPE_DIMS = NSA_HD // 4
ROPE_THETA = 500000.0
CMP_BLOCK = 32
CMP_STRIDE = 16
CMP_R = CMP_BLOCK // CMP_STRIDE
CMP_HID = 256
SEL_BLOCK = 64
SEL_TOP = 16
SEL_Q_BLOCK = 64
WINDOW = 512
WIN_BLOCK = 128
FORCE_SCORE = 1e4
KV_SLOTS = 4
RET_H = 8
RET_DK = 64
RET_DV = 128
RET_W = RET_H * RET_DV
RET_CHUNK = 128
RET_THETA = 10000.0
CD_COLS = NSA_W + 6 * NSA_G * NSA_HD + 3 * NSA_H + 2 * RET_H * RET_DK + 2 * RET_W

N_TOK = BATCH * SEQ + DEC_BATCH * DEC_SEQ
N_PROMPT = BATCH * SEQ

LANE = 128
VMEM_LIMIT_BYTES = 56 * 1024 * 1024
ROW_TILE = 640
FF_TILE = 512
D_FF_PAD = 5632
COL_TILE = 512

BF16 = jnp.bfloat16
F32 = jnp.float32


def _round_up(n, m):
    return -(-n // m) * m


def _rms_rows(x, g):
    ms = jnp.mean(x * x, axis=-1, keepdims=True)
    return x * lax.rsqrt(ms + 1e-6) * g


def _ffn_kernel(x_ref, g_ref, wg_ref, wu_ref, wo_ref, o_ref, xn_ref, acc_ref):
    k = pl.program_id(1)

    @pl.when(k == 0)
    def _():
        xn_ref[...] = _rms_rows(x_ref[...], g_ref[...]).astype(BF16)
        acc_ref[...] = jnp.zeros_like(acc_ref)

    xn = xn_ref[...]
    gate = jnp.dot(xn, wg_ref[...], preferred_element_type=F32)
    up = jnp.dot(xn, wu_ref[...], preferred_element_type=F32)
    act = gate * jax.nn.sigmoid(gate) * up
    acc_ref[...] += jnp.dot(act.astype(BF16), wo_ref[...], preferred_element_type=F32)

    @pl.when(k == pl.num_programs(1) - 1)
    def _():
        o_ref[...] = x_ref[...] + 0.5 * acc_ref[...]


def ffn_block(x, g, wg, wu, wo):
    m, d = x.shape
    return pl.pallas_call(
        _ffn_kernel,
        grid=(m // ROW_TILE, D_FF_PAD // FF_TILE),
        in_specs=[
            pl.BlockSpec((ROW_TILE, d), lambda i, k: (i, 0)),
            pl.BlockSpec((1, d), lambda i, k: (0, 0)),
            pl.BlockSpec((d, FF_TILE), lambda i, k: (0, k)),
            pl.BlockSpec((d, FF_TILE), lambda i, k: (0, k)),
            pl.BlockSpec((FF_TILE, d), lambda i, k: (k, 0)),
        ],
        out_specs=pl.BlockSpec((ROW_TILE, d), lambda i, k: (i, 0)),
        out_shape=jax.ShapeDtypeStruct((m, d), F32),
        scratch_shapes=[pltpu.VMEM((ROW_TILE, d), BF16), pltpu.VMEM((ROW_TILE, d), F32)],
        compiler_params=pltpu.CompilerParams(
            dimension_semantics=("parallel", "arbitrary"), vmem_limit_bytes=VMEM_LIMIT_BYTES),
        name="ffn_block",
    )(x, g.reshape(1, d), wg, wu, wo)


def _norm_matmul_kernel(x_ref, g_ref, w_ref, o_ref, xn_ref):
    @pl.when(pl.program_id(1) == 0)
    def _():
        xn_ref[...] = _rms_rows(x_ref[...], g_ref[...]).astype(BF16)

    o_ref[...] = jnp.dot(xn_ref[...], w_ref[...], preferred_element_type=F32)


def norm_matmul(x, g, w):
    m, k = x.shape
    n = w.shape[1]
    return pl.pallas_call(
        _norm_matmul_kernel,
        grid=(m // ROW_TILE, n // COL_TILE),
        in_specs=[
            pl.BlockSpec((ROW_TILE, k), lambda i, j: (i, 0)),
            pl.BlockSpec((1, k), lambda i, j: (0, 0)),
            pl.BlockSpec((k, COL_TILE), lambda i, j: (0, j)),
        ],
        out_specs=pl.BlockSpec((ROW_TILE, COL_TILE), lambda i, j: (i, j)),
        out_shape=jax.ShapeDtypeStruct((m, n), F32),
        scratch_shapes=[pltpu.VMEM((ROW_TILE, k), BF16)],
        compiler_params=pltpu.CompilerParams(
            dimension_semantics=("parallel", "arbitrary"), vmem_limit_bytes=VMEM_LIMIT_BYTES),
        name="norm_matmul",
    )(x, g.reshape(1, k), w)


def _matmul_residual_kernel(a_ref, w_ref, r_ref, o_ref):
    o_ref[...] = r_ref[...] + jnp.dot(a_ref[...].astype(BF16), w_ref[...], preferred_element_type=F32)


def matmul_residual(a, w, res):
    m, k = a.shape
    n = w.shape[1]
    return pl.pallas_call(
        _matmul_residual_kernel,
        grid=(m // ROW_TILE, n // COL_TILE),
        in_specs=[
            pl.BlockSpec((ROW_TILE, k), lambda i, j: (i, 0)),
            pl.BlockSpec((k, COL_TILE), lambda i, j: (0, j)),
            pl.BlockSpec((ROW_TILE, COL_TILE), lambda i, j: (i, j)),
        ],
        out_specs=pl.BlockSpec((ROW_TILE, COL_TILE), lambda i, j: (i, j)),
        out_shape=jax.ShapeDtypeStruct((m, n), F32),
        compiler_params=pltpu.CompilerParams(
            dimension_semantics=("parallel", "arbitrary"), vmem_limit_bytes=VMEM_LIMIT_BYTES),
        name="matmul_residual",
    )(a, w, res)


def _prep_ffn_weights(w_in, w_out):
    pad = D_FF_PAD - D_FF
    wg = jnp.pad(w_in[:, :D_FF], ((0, 0), (0, pad))).astype(BF16)
    wu = jnp.pad(w_in[:, D_FF:], ((0, 0), (0, pad))).astype(BF16)
    wo = jnp.pad(w_out, ((0, pad), (0, 0))).astype(BF16)
    return wg, wu, wo


def _prep_cols(w):
    n = w.shape[1]
    return jnp.pad(w, ((0, 0), (0, _round_up(n, COL_TILE) - n))).astype(BF16)


SCAN_TILE = 256


def _lru_scan_kernel(a_ref, b_ref, h0_ref, o_ref, carry_ref):
    @pl.when(pl.program_id(1) == 0)
    def _():
        carry_ref[...] = h0_ref[...]

    a = a_ref[...]
    b = b_ref[...]
    rows = lax.broadcasted_iota(jnp.int32, a.shape, 0)
    k = 1
    while k < a.shape[0]:
        keep = rows >= k
        b = jnp.where(keep, a * pltpu.roll(b, k, 0) + b, b)
        a = jnp.where(keep, a * pltpu.roll(a, k, 0), a)
        k *= 2
    h = a * carry_ref[...] + b
    o_ref[...] = h
    carry_ref[...] = h[a.shape[0] - 1:, :]


def lru_scan(a, b, h0):
    B, T, W = a.shape
    tt = min(SCAN_TILE, T)
    return pl.pallas_call(
        _lru_scan_kernel,
        grid=(B, T // tt),
        in_specs=[
            pl.BlockSpec((None, tt, W), lambda i, t: (i, t, 0)),
            pl.BlockSpec((None, tt, W), lambda i, t: (i, t, 0)),
            pl.BlockSpec((None, 1, W), lambda i, t: (i, 0, 0)),
        ],
        out_specs=pl.BlockSpec((None, tt, W), lambda i, t: (i, t, 0)),
        out_shape=jax.ShapeDtypeStruct((B, T, W), F32),
        scratch_shapes=[pltpu.VMEM((1, W), F32)],
        compiler_params=pltpu.CompilerParams(
            dimension_semantics=("parallel", "arbitrary"), vmem_limit_bytes=VMEM_LIMIT_BYTES),
        name="lru_scan",
    )(a, b, h0.reshape(B, 1, W))


GROUP_W = NSA_HPG * NSA_HD
ATT_Q_TILE = 128
ATT_K_TILE = 256
CMP_PAD = 128
NEG_BIG = -1e30


def _stack_heads(q):
    head = lax.broadcasted_iota(jnp.int32, q.shape, 1) // NSA_HD
    return jnp.concatenate([jnp.where(head == h, q, 0.0) for h in range(NSA_HPG)], axis=0)


def _unstack_heads(o, tq):
    head = lax.broadcasted_iota(jnp.int32, (tq, GROUP_W), 1) // NSA_HD
    out = jnp.zeros((tq, GROUP_W), F32)
    for h in range(NSA_HPG):
        out = out + jnp.where(head == h, o[h * tq:(h + 1) * tq], 0.0)
    return out


def _cmp_select_kernel(q_ref, k_ref, v_ref, ov_ref, o_ref, sel_ref, *, n_cmp, n_sel, q_pos0):
    tq = q_ref.shape[0]
    i = pl.program_id(2)
    qs = _stack_heads(q_ref[...] * (NSA_HD ** -0.5)).astype(BF16)
    s = lax.dot_general(qs, k_ref[...], (((1,), (1,)), ((), ())), preferred_element_type=F32)
    q_pos = q_pos0 + i * tq + lax.broadcasted_iota(jnp.int32, (tq, CMP_PAD), 0)
    c = lax.broadcasted_iota(jnp.int32, (tq, CMP_PAD), 1)
    mask1 = (c < n_cmp) & (c * CMP_STRIDE + (CMP_BLOCK - 1) <= q_pos)
    mask = jnp.concatenate([mask1] * NSA_HPG, axis=0)
    s = jnp.where(mask, s, NEG_BIG)
    m = jnp.max(s, axis=-1, keepdims=True)
    e = jnp.where(mask, jnp.exp(s - m), 0.0)
    den = jnp.sum(e, axis=-1, keepdims=True)
    prob = e / jnp.where(den > 0, den, 1.0)
    o = jnp.dot(prob.astype(BF16), v_ref[...], preferred_element_type=F32)
    o_ref[...] = _unstack_heads(o, tq)
    psum = prob[0:tq]
    for h in range(1, NSA_HPG):
        psum = psum + prob[h * tq:(h + 1) * tq]
    imp = jnp.dot(psum.astype(BF16), ov_ref[...], preferred_element_type=F32)
    qb = q_pos // SEL_BLOCK
    valid = (c <= qb) & (c < n_sel)
    forced = (c == 0) | (c == qb) | (c == qb - 1)
    score = jnp.where(valid, jnp.where(forced, FORCE_SCORE, imp), -jnp.inf)
    rank = jnp.zeros((tq, CMP_PAD), F32)
    for jp in range(n_sel):
        col = score[:, jp:jp + 1]
        beats = (col > score) | ((col == score) & (c > jp))
        rank = rank + jnp.where(beats, 1.0, 0.0)
    sel_ref[...] = jnp.where((rank < min(SEL_TOP, n_sel)) & (c < n_sel), 1.0, 0.0)


def nsa_cmp_select(qn, kc4, vc4, ovT, *, n_cmp, n_sel, q_pos0):
    B, T, _ = qn.shape
    tq = min(ATT_Q_TILE, T)
    return pl.pallas_call(
        functools.partial(_cmp_select_kernel, n_cmp=n_cmp, n_sel=n_sel, q_pos0=q_pos0),
        grid=(B, NSA_G, T // tq),
        in_specs=[
            pl.BlockSpec((None, tq, GROUP_W), lambda b, g, i: (b, i, g)),
            pl.BlockSpec((None, None, CMP_PAD, GROUP_W), lambda b, g, i: (b, g, 0, 0)),
            pl.BlockSpec((None, None, CMP_PAD, GROUP_W), lambda b, g, i: (b, g, 0, 0)),
            pl.BlockSpec((CMP_PAD, CMP_PAD), lambda b, g, i: (0, 0)),
        ],
        out_specs=[
            pl.BlockSpec((None, tq, GROUP_W), lambda b, g, i: (b, i, g)),
            pl.BlockSpec((None, None, tq, CMP_PAD), lambda b, g, i: (b, g, i, 0)),
        ],
        out_shape=[jax.ShapeDtypeStruct((B, T, NSA_W), F32),
                   jax.ShapeDtypeStruct((B, NSA_G, T, CMP_PAD), F32)],
        compiler_params=pltpu.CompilerParams(
            dimension_semantics=("parallel", "parallel", "parallel"), vmem_limit_bytes=VMEM_LIMIT_BYTES),
        name="nsa_cmp_select",
    )(qn, kc4, vc4, ovT)


def _flash_kernel(*refs, selected):
    if selected:
        q_ref, k_ref, v_ref, sel_ref, exp_ref, o_ref, m_ref, l_ref, acc_ref = refs
    else:
        q_ref, k_ref, v_ref, o_ref, m_ref, l_ref, acc_ref = refs
    tq = q_ref.shape[0]
    tk = ATT_K_TILE
    i = pl.program_id(2)
    qs = _stack_heads(q_ref[...] * (NSA_HD ** -0.5)).astype(BF16)
    m_ref[...] = jnp.full(m_ref.shape, NEG_BIG, F32)
    l_ref[...] = jnp.zeros(l_ref.shape, F32)
    acc_ref[...] = jnp.zeros(acc_ref.shape, F32)
    q_pos = i * tq + lax.broadcasted_iota(jnp.int32, (tq, tk), 0)
    col = lax.broadcasted_iota(jnp.int32, (tq, tk), 1)
    if selected:
        sel = sel_ref[...].astype(BF16)
        lo = 0
    else:
        lo = jnp.maximum(i * tq - (WINDOW - 1), 0) // tk
    hi = (i * tq + tq - 1) // tk + 1

    def body(j, carry):
        start = pl.multiple_of(j * tk, tk)
        k = k_ref[pl.ds(start, tk), :]
        v = v_ref[pl.ds(start, tk), :]
        s = lax.dot_general(qs, k, (((1,), (1,)), ((), ())), preferred_element_type=F32)
        k_pos = start + col
        mask1 = k_pos <= q_pos
        if selected:
            mask1 = mask1 & (jnp.dot(sel, exp_ref[j], preferred_element_type=F32) > 0.5)
        else:
            mask1 = mask1 & (q_pos - k_pos < WINDOW)
        mask = jnp.concatenate([mask1] * NSA_HPG, axis=0)
        s = jnp.where(mask, s, NEG_BIG)
        m_old = m_ref[...]
        m_new = jnp.maximum(m_old, jnp.max(s, axis=-1, keepdims=True))
        alpha = jnp.exp(m_old - m_new)
        p = jnp.where(mask, jnp.exp(s - m_new), 0.0)
        l_ref[...] = alpha * l_ref[...] + jnp.sum(p, axis=-1, keepdims=True)
        acc_ref[...] = alpha * acc_ref[...] + jnp.dot(p.astype(BF16), v, preferred_element_type=F32)
        m_ref[...] = m_new
        return carry

    lax.fori_loop(lo, hi, body, 0)
    den = l_ref[...]
    o_ref[...] = _unstack_heads(acc_ref[...] / jnp.where(den > 0, den, 1.0), tq)


def nsa_flash(qr, k4, v4, sel=None, expand=None):
    B, T, _ = qr.shape
    tq = ATT_Q_TILE
    selected = sel is not None
    in_specs = [
        pl.BlockSpec((None, tq, GROUP_W), lambda b, g, i: (b, i, g)),
        pl.BlockSpec((None, T, GROUP_W), lambda b, g, i: (b, 0, g)),
        pl.BlockSpec((None, T, GROUP_W), lambda b, g, i: (b, 0, g)),
    ]
    args = [qr, k4, v4]
    if selected:
        in_specs += [
            pl.BlockSpec((None, None, tq, CMP_PAD), lambda b, g, i: (b, g, i, 0)),
            pl.BlockSpec(expand.shape, lambda b, g, i: (0, 0, 0)),
        ]
        args += [sel, expand]
    return pl.pallas_call(
        functools.partial(_flash_kernel, selected=selected),
        grid=(B, NSA_G, T // tq),
        in_specs=in_specs,
        out_specs=pl.BlockSpec((None, tq, GROUP_W), lambda b, g, i: (b, i, g)),
        out_shape=jax.ShapeDtypeStruct((B, T, NSA_W), F32),
        scratch_shapes=[pltpu.VMEM((NSA_HPG * tq, 1), F32), pltpu.VMEM((NSA_HPG * tq, 1), F32),
                        pltpu.VMEM((NSA_HPG * tq, GROUP_W), F32)],
        compiler_params=pltpu.CompilerParams(
            dimension_semantics=("parallel", "parallel", "parallel"), vmem_limit_bytes=VMEM_LIMIT_BYTES),
        name="nsa_flash_sel" if selected else "nsa_flash_win",
    )(*args)


def _tile_groups(x):
    B, T = x.shape[:2]
    return jnp.broadcast_to(x[:, :, :, None, :], (B, T, NSA_G, NSA_HPG, NSA_HD)).reshape(B, T, NSA_W).astype(BF16)


def _tile_cmp(x):
    B, n = x.shape[:2]
    x = jnp.pad(jnp.moveaxis(x, 1, 2), ((0, 0), (0, 0), (0, CMP_PAD - n), (0, 0)))
    return jnp.tile(x, (1, 1, 1, NSA_HPG)).astype(BF16)


def _overlap_T(n_cmp, n_sel):
    ov = np.zeros((CMP_PAD, CMP_PAD), np.float32)
    cs = np.arange(n_cmp) * CMP_STRIDE
    ss = np.arange(n_sel) * SEL_BLOCK
    o = np.minimum(cs[None] + CMP_BLOCK, ss[:, None] + SEL_BLOCK) - np.maximum(cs[None], ss[:, None])
    ov[:n_cmp, :n_sel] = (np.clip(o, 0, None) / CMP_BLOCK).T
    return jnp.asarray(ov, dtype=BF16)


def _sel_expand(T):
    t = np.arange(T)
    e = (np.arange(CMP_PAD)[:, None] == (t // SEL_BLOCK)[None, :]).astype(np.float32)
    return jnp.asarray(e.reshape(CMP_PAD, T // ATT_K_TILE, ATT_K_TILE).transpose(1, 0, 2), dtype=BF16)


KV_ROWS = KV_SLOTS * NSA_G * NSA_HD
SLOT_ROWS = NSA_G * NSA_HD
N_PAGES = PAST_LEN // PAGE_SIZE
DEC_N_CHUNK = (PAST_LEN + DEC_SEQ) // CMP_STRIDE
DEC_N_CMP = DEC_N_CHUNK - CMP_R + 1
DEC_N_SEL = -(-(PAST_LEN + DEC_SEQ) // SEL_BLOCK)
WIN_BUF = min(WINDOW, PAST_LEN)


def _softmax_rows(s, mask, s_new=None):
    s = jnp.where(mask, s, NEG_BIG)
    m = jnp.max(s, axis=-1, keepdims=True)
    if s_new is not None:
        m = jnp.maximum(m, s_new)
    e = jnp.where(mask, jnp.exp(s - m), 0.0)
    den = jnp.sum(e, axis=-1, keepdims=True)
    if s_new is None:
        return e, den
    e_new = jnp.exp(s_new - m)
    return e, e_new, den + e_new


def _dec_nsa_kernel(pt_ref, *refs):
    pages = refs[:N_PAGES]
    (win_ref, qn_ref, qr_ref, new_ref, gate_ref, w1_ref, b1_ref, w2_ref, b2_ref, kn_ref,
     ov_ref, exp_ref, grp_ref, o_ref, xt_ref, acc_ref) = refs[N_PAGES:]
    del pt_ref
    f32 = F32
    half = 2 * NSA_HD
    n_chunk = DEC_N_CHUNK

    for p in range(N_PAGES):
        for sg in range(4):
            xt_ref[sg, p * PAGE_SIZE:(p + 1) * PAGE_SIZE, :] = pages[p][sg * half:(sg + 1) * half, :].T

    lane_lo = lax.broadcasted_iota(jnp.int32, (n_chunk, 2 * half), 1) % half < NSA_HD
    lane_grp = lax.broadcasted_iota(jnp.int32, (n_chunk, SLOT_ROWS), 1) // NSA_HD
    cmp_rows = []
    for slot in range(2):
        for gp in range(2):
            for rp in range(CMP_STRIDE // 2):
                xr = jnp.concatenate(
                    [xt_ref[slot * 2 + gp, pl.ds(2 * rp + j, n_chunk, stride=CMP_STRIDE), :] for j in range(2)],
                    axis=1)
                xs = jnp.concatenate([jnp.where(lane_lo, xr, 0.0), jnp.where(lane_lo, 0.0, xr)],
                                     axis=0).astype(BF16)
                part = jnp.dot(xs, w1_ref[slot, rp], preferred_element_type=f32)
                rows = pl.ds(gp * 2 * n_chunk, 2 * n_chunk)
                if rp == 0:
                    acc_ref[rows, :] = part
                else:
                    acc_ref[rows, :] += part
        acc = acc_ref[...]
        pre = b1_ref[slot] + acc[:, :CMP_HID] + pltpu.roll(acc[:, CMP_HID:], NSA_G * n_chunk - 1, 0)
        out = jnp.dot(jax.nn.gelu(pre).astype(BF16), w2_ref[slot], preferred_element_type=f32) + b2_ref[slot]
        if slot == 0:
            out = _rms_rows(out, kn_ref[...])
        sel_rows = jnp.zeros((n_chunk, SLOT_ROWS), f32)
        for g in range(NSA_G):
            sel_rows = sel_rows + jnp.where(lane_grp == g, out[g * n_chunk:(g + 1) * n_chunk], 0.0)
        cmp_rows.append(sel_rows.astype(BF16))
    kc, vc = cmp_rows

    qn = qn_ref[...].astype(BF16)
    qr = qr_ref[...].astype(BF16)
    nt = (((1,), (1,)), ((), ()))
    c = lax.broadcasted_iota(jnp.int32, (NSA_H, CMP_PAD), 1)
    s = lax.dot_general(qn, kc, nt, preferred_element_type=f32)
    e, den = _softmax_rows(s, c < DEC_N_CMP)
    prob = e / jnp.where(den > 0, den, 1.0)
    o_cmp = jnp.dot(prob.astype(BF16), vc, preferred_element_type=f32)
    p_hi, p_mid = _split_bf16(prob)
    p_lo = (prob - p_hi.astype(f32) - p_mid.astype(f32)).astype(BF16)
    grp = grp_ref[...]
    psum = (jnp.dot(grp, p_hi, preferred_element_type=f32) + jnp.dot(grp, p_mid, preferred_element_type=f32)
            + jnp.dot(grp, p_lo, preferred_element_type=f32))
    imp = jnp.dot(psum.astype(BF16), ov_ref[...], preferred_element_type=f32)
    qb = (PAST_LEN + DEC_SEQ - 1) // SEL_BLOCK
    valid = c <= qb
    forced = (c == 0) | (c == qb) | (c == qb - 1)
    score = jnp.where(valid, jnp.where(forced, FORCE_SCORE, imp), -jnp.inf)
    rank = jnp.zeros((NSA_H, CMP_PAD), f32)
    for jp in range(DEC_N_SEL):
        col = score[:, jp:jp + 1]
        rank = rank + jnp.where((col > score) | ((col == score) & (c > jp)), 1.0, 0.0)
    sel = jnp.where((rank < min(SEL_TOP, DEC_N_SEL)) & (c < DEC_N_SEL), 1.0, 0.0).astype(BF16)

    new = new_ref[...]
    new_b = new.astype(BF16).astype(f32)
    qr_f = qr.astype(f32)
    s_pages = [jnp.dot(qr, pages[p][2 * SLOT_ROWS:3 * SLOT_ROWS, :].astype(BF16), preferred_element_type=f32)
               for p in range(N_PAGES)]
    s = jnp.concatenate(s_pages, axis=1)
    mask = jnp.dot(sel, exp_ref[...], preferred_element_type=f32) > 0.5
    s_new = jnp.sum(qr_f * new_b[0:1], axis=-1, keepdims=True)
    e, e_new, den = _softmax_rows(s, mask, s_new)
    e = e.astype(BF16)
    o_slc = e_new.astype(BF16).astype(f32) * new_b[1:2]
    for p in range(N_PAGES):
        o_slc = o_slc + lax.dot_general(e[:, p * PAGE_SIZE:(p + 1) * PAGE_SIZE],
                                        pages[p][3 * SLOT_ROWS:4 * SLOT_ROWS, :].astype(BF16), nt,
                                        preferred_element_type=f32)
    o_slc = o_slc / den

    s = jnp.dot(qr, win_ref[0:SLOT_ROWS, :].astype(BF16), preferred_element_type=f32)
    i_buf = lax.broadcasted_iota(jnp.int32, (NSA_H, WIN_BUF), 1)
    s_new = jnp.sum(qr_f * new_b[2:3], axis=-1, keepdims=True)
    e, e_new, den = _softmax_rows(s, WIN_BUF - i_buf < WINDOW, s_new)
    o_win = e_new.astype(BF16).astype(f32) * new_b[3:4] + lax.dot_general(
        e.astype(BF16), win_ref[SLOT_ROWS:2 * SLOT_ROWS, :].astype(BF16), nt, preferred_element_type=f32)
    o_win = o_win / den

    gates = gate_ref[...]
    o_ref[...] = gates[:, 0:1] * o_cmp + gates[:, 1:2] * o_slc + gates[:, 2:3] * o_win


def dec_nsa(page_table, cache_t, win_t, qn16, qr16, new_rows, gates, w1t, b1, w2t, b2t, kn, ovT, expand, grp):
    DB = qn16.shape[0]
    const = lambda shape: pl.BlockSpec(shape, lambda b, pt: (0,) * len(shape))
    per_b = lambda shape: pl.BlockSpec((None,) + shape, lambda b, pt: (b,) + (0,) * len(shape))
    page_specs = [pl.BlockSpec((None, KV_ROWS, PAGE_SIZE), functools.partial(lambda b, pt, p: (pt[b, p], 0, 0), p=p))
                  for p in range(N_PAGES)]
    in_specs = page_specs + [
        per_b((2 * SLOT_ROWS, WIN_BUF)), per_b((NSA_H, SLOT_ROWS)), per_b((NSA_H, SLOT_ROWS)),
        per_b((4, SLOT_ROWS)), per_b((NSA_H, 3)),
        const(w1t.shape), const(b1.shape), const(w2t.shape), const(b2t.shape), const(kn.shape),
        const(ovT.shape), const(expand.shape), const(grp.shape),
    ]
    grid_spec = pltpu.PrefetchScalarGridSpec(
        num_scalar_prefetch=1, grid=(DB,), in_specs=in_specs,
        out_specs=pl.BlockSpec((None, NSA_H, SLOT_ROWS), lambda b, pt: (b, 0, 0)),
        scratch_shapes=[pltpu.VMEM((4, PAST_LEN, 2 * NSA_HD), F32),
                        pltpu.VMEM((NSA_G * DEC_N_CHUNK, CMP_R * CMP_HID), F32)])
    return pl.pallas_call(
        _dec_nsa_kernel,
        grid_spec=grid_spec,
        out_shape=jax.ShapeDtypeStruct((DB, NSA_H, SLOT_ROWS), F32),
        compiler_params=pltpu.CompilerParams(
            dimension_semantics=("arbitrary",), vmem_limit_bytes=VMEM_LIMIT_BYTES),
        name="dec_nsa",
    )(page_table, *([cache_t] * N_PAGES), win_t, qn16, qr16, new_rows, gates, w1t, b1, w2t, b2t, kn, ovT, expand, grp)


def _dec_cmp_weights(w1, b1, w2, b2):
    w = w1.reshape(CMP_R, CMP_STRIDE // 2, 2, 1, NSA_HD, CMP_HID)
    w = jnp.broadcast_to(w, (CMP_R, CMP_STRIDE // 2, 2, 2, NSA_HD, CMP_HID))
    w = jnp.moveaxis(w, 0, 4).reshape(CMP_STRIDE // 2, 4 * NSA_HD, CMP_R * CMP_HID)
    return (w.astype(BF16), b1.reshape(1, CMP_HID), jnp.tile(w2, (1, NSA_G)).astype(BF16),
            jnp.tile(b2, NSA_G).reshape(1, SLOT_ROWS))


def _place_heads(q):
    own = (jnp.arange(NSA_H)[:, None] // NSA_HPG) == jnp.arange(NSA_G)[None, :]
    return jnp.where(own[None, :, :, None], q[:, :, None, :], 0.0).reshape(q.shape[0], NSA_H, SLOT_ROWS)


def _take_heads(o):
    o = o.reshape(o.shape[0], NSA_H, NSA_G, NSA_HD)
    return o[:, jnp.arange(NSA_H), jnp.arange(NSA_H) // NSA_HPG, :].reshape(o.shape[0], NSA_W)


WKV_C = 64
WKV_PAIR = 2 * RWKV_HD
WKV_T_TILE = 512


def _split_bf16(x):
    hi = x.astype(BF16)
    return hi, (x - hi.astype(F32)).astype(BF16)


def _dot3(a, b):
    a_hi, a_lo = _split_bf16(a)
    b_hi, b_lo = _split_bf16(b)
    return (jnp.dot(a_hi, b_hi, preferred_element_type=F32) + jnp.dot(a_hi, b_lo, preferred_element_type=F32)
            + jnp.dot(a_lo, b_hi, preferred_element_type=F32))


def _wkv_kernel(r_ref, lw_ref, k_ref, v_ref, a_ref, b_ref, s0_ref, y_ref, sT_ref, s_scr):
    C = WKV_C
    P = WKV_PAIR
    n_chunks = r_ref.shape[0] // C

    @pl.when(pl.program_id(2) == 0)
    def _():
        s_scr[...] = s0_ref[...]

    lo_lane = lax.broadcasted_iota(jnp.int32, (C, P), 1) < RWKV_HD
    row = lax.broadcasted_iota(jnp.int32, (2 * C, 2 * C), 0)
    col = lax.broadcasted_iota(jnp.int32, (2 * C, 2 * C), 1)
    same_head = (row // C) == (col // C)
    strict = same_head & (row > col)
    lower = same_head & (row >= col)
    eye = jnp.where(row == col, 1.0, 0.0)
    tril = jnp.where(lax.broadcasted_iota(jnp.int32, (C, C), 0) >= lax.broadcasted_iota(jnp.int32, (C, C), 1),
                     1.0, 0.0).astype(BF16)

    def stack(x):
        return jnp.concatenate([jnp.where(lo_lane, x, 0.0), jnp.where(lo_lane, 0.0, x)], axis=0)

    def chunk(c, carry):
        sl = pl.ds(pl.multiple_of(c * C, C), C)
        r, lw, k, v, a, b = (ref[sl, :] for ref in (r_ref, lw_ref, k_ref, v_ref, a_ref, b_ref))
        lw_hi, lw_mid = _split_bf16(lw)
        lw_lo = (lw - lw_hi.astype(F32) - lw_mid.astype(F32)).astype(BF16)
        cs = (jnp.dot(tril, lw_hi, preferred_element_type=F32) + jnp.dot(tril, lw_mid, preferred_element_type=F32)
              + jnp.dot(tril, lw_lo, preferred_element_type=F32))
        g_inv = jnp.exp(-cs)
        g_end = jnp.exp(cs[C - 1:C, :] - cs)
        a2 = stack(a * jnp.exp(cs - lw))
        r2 = stack(r * jnp.exp(cs))
        b2 = stack(b * g_inv)
        k2 = stack(k * g_inv)
        v2 = stack(v)
        s_old = s_scr[...]
        ar = jnp.concatenate([a2, r2], axis=0).astype(BF16)
        bk = jnp.concatenate([b2, k2], axis=0).astype(BF16)
        nt = (((1,), (1,)), ((), ()))
        pp = lax.dot_general(ar, bk, nt, preferred_element_type=F32)
        from_state = lax.dot_general(ar, s_old.astype(BF16), nt, preferred_element_type=F32)
        l_ab = jnp.where(strict, pp[:2 * C, :2 * C], 0.0)
        l_ak = jnp.where(strict, pp[:2 * C, 2 * C:], 0.0)
        m_rb = jnp.where(lower, pp[2 * C:, :2 * C], 0.0)
        m_rk = jnp.where(lower, pp[2 * C:, 2 * C:], 0.0)
        v2b = v2.astype(BF16)
        rhs = from_state[:2 * C] + jnp.dot(l_ak.astype(BF16), v2b, preferred_element_type=F32)
        n = l_ab
        x = eye + n
        span = 2
        while span < C:
            n = _dot3(n, n)
            x = x + _dot3(n, x)
            span *= 2
        u2 = _dot3(x, rhs)
        uv = jnp.concatenate([u2, v2], axis=0).astype(BF16)
        y2 = from_state[2 * C:] + jnp.dot(jnp.concatenate([m_rb, m_rk], axis=1).astype(BF16), uv,
                                          preferred_element_type=F32)
        y_ref[sl, :] = y2[:C] + y2[C:]
        bk_end = jnp.concatenate([stack(b * g_end), stack(k * g_end)], axis=0).astype(BF16)
        s_scr[...] = s_old * jnp.exp(cs[C - 1:C, :]) + lax.dot_general(
            uv, bk_end, (((0,), (0,)), ((), ())), preferred_element_type=F32)
        return carry

    lax.fori_loop(0, n_chunks, chunk, 0)

    @pl.when(pl.program_id(2) == pl.num_programs(2) - 1)
    def _():
        sT_ref[...] = s_scr[...]


def wkv7_chunked(r, lw, k, v, a, b, s0):
    B, T, W = r.shape
    n_pair = W // WKV_PAIR
    tt = min(WKV_T_TILE, T)
    s0p = s0.astype(F32).reshape(B, n_pair, 2, RWKV_HD, RWKV_HD)
    zero = jnp.zeros_like(s0p[:, :, 0])
    s0_bd = jnp.concatenate([jnp.concatenate([s0p[:, :, 0], zero], axis=-1),
                             jnp.concatenate([zero, s0p[:, :, 1]], axis=-1)], axis=-2)
    seq = pl.BlockSpec((None, tt, WKV_PAIR), lambda i, p, t: (i, t, p))
    st = pl.BlockSpec((None, None, WKV_PAIR, WKV_PAIR), lambda i, p, t: (i, p, 0, 0))
    y, s_bd = pl.pallas_call(
        _wkv_kernel,
        grid=(B, n_pair, T // tt),
        in_specs=[seq] * 6 + [st],
        out_specs=[seq, st],
        out_shape=[jax.ShapeDtypeStruct((B, T, W), F32),
                   jax.ShapeDtypeStruct((B, n_pair, WKV_PAIR, WKV_PAIR), F32)],
        scratch_shapes=[pltpu.VMEM((WKV_PAIR, WKV_PAIR), F32)],
        compiler_params=pltpu.CompilerParams(
            dimension_semantics=("parallel", "parallel", "arbitrary"), vmem_limit_bytes=VMEM_LIMIT_BYTES),
        name="wkv7_chunked",
    )(r, lw, k, v, a, b, s0_bd)
    s_fin = jnp.stack([s_bd[:, :, :RWKV_HD, :RWKV_HD], s_bd[:, :, RWKV_HD:, RWKV_HD:]], axis=2)
    return y, s_fin.reshape(B, W // RWKV_HD, RWKV_HD, RWKV_HD)


def rms_norm(x, g, eps=1e-6):
    xf = x.astype(jnp.float32)
    y = xf * lax.rsqrt(jnp.mean(xf * xf, axis=-1, keepdims=True) + eps)
    return (y * g.astype(jnp.float32)).astype(x.dtype)


def head_group_norm(y, g, b, eps):
    yf = y.astype(jnp.float32)
    mu = jnp.mean(yf, axis=-1, keepdims=True)
    var = jnp.mean(jnp.square(yf - mu), axis=-1, keepdims=True)
    yn = ((yf - mu) * lax.rsqrt(var + eps)).reshape(y.shape[:-2] + (-1,))
    return (yn * g.astype(jnp.float32) + b.astype(jnp.float32)).astype(y.dtype)


def masked_softmax(s, mask):
    s = jnp.where(mask, s.astype(jnp.float32), -jnp.inf)
    m = jnp.max(s, axis=-1, keepdims=True)
    e = jnp.exp(s - jnp.where(jnp.isfinite(m), m, 0.0))
    den = jnp.sum(e, axis=-1, keepdims=True)
    return e / jnp.where(den > 0, den, 1.0)


def rope(x, pos, n_rot, theta):
    half = n_rot // 2
    inv = jnp.exp(-jnp.log(jnp.float32(theta)) * jnp.arange(half, dtype=jnp.float32) / half)
    ang = pos.astype(jnp.float32)[:, None] * inv[None, :]
    cos = jnp.cos(ang)[None, :, None, :]
    sin = jnp.sin(ang)[None, :, None, :]
    xf = x.astype(jnp.float32)
    x1, x2 = xf[..., :half], xf[..., half:n_rot]
    out = jnp.concatenate([x1 * cos - x2 * sin, x2 * cos + x1 * sin, xf[..., n_rot:]], axis=-1)
    return out.astype(x.dtype)


def linear_scan(a, b, h0):
    b = b.at[:, 0].add(a[:, 0] * h0)

    def combine(left, right):
        return left[0] * right[0], right[0] * left[1] + right[1]

    return lax.associative_scan(combine, (a, b), axis=1)[1]


def wkv7_scan(r, w, k, v, a, b, s0):
    xs = tuple(jnp.moveaxis(z.astype(jnp.float32), 1, 0) for z in (r, w, k, v, a, b))

    def step(S, inp):
        r_t, w_t, k_t, v_t, a_t, b_t = inp
        sa = jnp.einsum('bhij,bhj->bhi', S, a_t)
        S = S * w_t[:, :, None, :] + sa[..., None] * b_t[:, :, None, :] + v_t[..., None] * k_t[:, :, None, :]
        return S, jnp.einsum('bhij,bhj->bhi', S, r_t)

    S, ys = lax.scan(step, s0.astype(jnp.float32), xs)
    return jnp.moveaxis(ys, 0, 1), S


def even_mixer_core(proj, p, lru_h0, lru_conv0, shift0, wkv0):
    B, T, _ = proj.shape
    f32 = jnp.float32
    dt = proj.dtype
    xb, gb, rw = jnp.split(proj, [LRU_W, 2 * LRU_W], axis=-1)
    xcat = jnp.concatenate([lru_conv0.astype(dt), xb], axis=1)
    xc = p['conv_b'] + sum(p['conv_w'][j] * xcat[:, j:j + T] for j in range(CONV_W))
    xbd = xc.reshape(B, T, LRU_BLOCKS, LRU_BS)
    gate_r = jax.nn.sigmoid(jnp.einsum('btnc,ncd->btnd', xbd, p['wa']).reshape(B, T, LRU_W) + p['ba'])
    gate_i = jax.nn.sigmoid(jnp.einsum('btnc,ncd->btnd', xbd, p['wx']).reshape(B, T, LRU_W) + p['bx'])
    log_a = -LRU_C * gate_r.astype(f32) * jax.nn.softplus(-p['lam'].astype(f32))
    u = jnp.sqrt(-jnp.expm1(2.0 * log_a)) * (gate_i * xc).astype(f32)
    hs = lru_scan(jnp.exp(log_a), u, lru_h0.astype(f32))
    y_lru = hs.astype(dt) * jax.nn.gelu(gb)
    prev = jnp.concatenate([shift0.astype(dt)[:, None], rw[:, :-1]], axis=1)
    rs = rw + p['mu'] * (prev - rw)
    r, k, v, xw, xa, xg = jnp.split(
        rs, [RWKV_W, 2 * RWKV_W, 3 * RWKV_W, 3 * RWKV_W + W_LORA, 3 * RWKV_W + W_LORA + A_LORA], axis=-1)
    w_log = -jax.nn.softplus(-(p['w0'] + jnp.tanh(xw) @ p['w2']).astype(f32)) - 0.5
    log_decay = -jnp.exp(w_log)
    decay = jnp.exp(log_decay)
    a_icl = jax.nn.sigmoid(p['a0'] + xa @ p['a2'])
    g = jax.nn.sigmoid(xg) @ p['g2']
    heads = (B, T, RWKV_H, RWKV_HD)
    kk = (k * p['k_k']).reshape(heads).astype(f32)
    kk = kk / jnp.maximum(jnp.sqrt(jnp.sum(kk * kk, axis=-1, keepdims=True)), 1e-12)
    k = k * (1.0 + (a_icl - 1.0) * p['k_a'])
    rh, kh, vh, ah = (z.reshape(heads) for z in (r, k, v, a_icl))
    if T % WKV_C == 0:
        y, wkv = wkv7_chunked(r.astype(f32), log_decay, k.astype(f32), v.astype(f32),
                              (-kk).reshape(B, T, RWKV_W), (kk * ah.astype(f32)).reshape(B, T, RWKV_W), wkv0)
        y = y.reshape(heads)
    else:
        y, wkv = wkv7_scan(rh, decay.reshape(heads), kh, vh, -kk, kk * ah.astype(f32), wkv0)
    y = head_group_norm(y, p['ln_g'], p['ln_b'], 64e-5).astype(dt)
    bonus = (jnp.sum(rh * kh * p['r_k'], axis=-1, keepdims=True) * vh).reshape(B, T, RWKV_W)
    y_rwkv = (y + bonus) * g
    cat = jnp.concatenate([y_lru, y_rwkv], axis=-1)
    return cat, hs[:, -1], xcat[:, T:], rw[:, -1], wkv


def odd_project(proj, p, pos):
    B, T, _ = proj.shape
    sizes = [NSA_W] + [NSA_G * NSA_HD] * 6 + [3 * NSA_H, RET_H * RET_DK, RET_H * RET_DK, RET_W, RET_W]
    q, kc, vc, ks, vs, kw, vw, gt, rq, rk, rv, rg = jnp.split(
        proj, np.cumsum(sizes)[:-1].tolist(), axis=-1)
    kvs = (B, T, NSA_G, NSA_HD)
    q_n = rms_norm(q.reshape(B, T, NSA_H, NSA_HD), p['q_norm'])
    return {
        'q_n': q_n,
        'q_r': rope(q_n, pos, ROPE_DIMS, ROPE_THETA),
        'kc': kc.reshape(kvs), 'vc': vc.reshape(kvs),
        'ks': rope(rms_norm(ks.reshape(kvs), p['k_norm'][1]), pos, ROPE_DIMS, ROPE_THETA),
        'vs': vs.reshape(kvs),
        'kw': rope(rms_norm(kw.reshape(kvs), p['k_norm'][2]), pos, ROPE_DIMS, ROPE_THETA),
        'vw': vw.reshape(kvs),
        'gates': jax.nn.sigmoid(gt).reshape(B, T, NSA_H, 3),
        'rq': rope(rq.reshape(B, T, RET_H, RET_DK), pos, RET_DK, RET_THETA),
        'rk': rope(rk.reshape(B, T, RET_H, RET_DK), pos, RET_DK, RET_THETA) * (RET_DK ** -0.5),
        'rv': rv.reshape(B, T, RET_H, RET_DV),
        'rg': rg,
    }


def to_groups_q(q):
    B, T = q.shape[:2]
    return jnp.moveaxis(q.reshape(B, T, NSA_G, NSA_HPG, NSA_HD), 1, 3)


def to_groups_k(k):
    return jnp.moveaxis(k, 1, 2)


def nsa_compress(x, w1, b1, w2, b2):
    B, L = x.shape[:2]
    n_chunk = L // CMP_STRIDE
    n_cmp = n_chunk - CMP_R + 1
    ch = x[:, :n_chunk * CMP_STRIDE].reshape(B, n_chunk, CMP_STRIDE, NSA_G, NSA_HD)
    ch = jnp.moveaxis(ch, 3, 2).reshape(B, n_chunk, NSA_G, CMP_STRIDE * NSA_HD)
    part = jnp.einsum('bngc,rch->bngrh', ch, w1)
    pre = b1 + sum(part[:, m:m + n_cmp, :, m] for m in range(CMP_R))
    return jax.nn.gelu(pre) @ w2 + b2


def nsa_compressed_branch(qn, kc_raw, vc_raw, p, q_pos):
    kc = to_groups_k(rms_norm(nsa_compress(kc_raw, *p['ck']), p['k_norm'][0]))
    vc = to_groups_k(nsa_compress(vc_raw, *p['cv']))
    s = jnp.einsum('bghqd,bgcd->bghqc', qn, kc) * NSA_HD ** -0.5
    ends = jnp.arange(kc.shape[2]) * CMP_STRIDE + CMP_BLOCK - 1
    prob = masked_softmax(s, ends[None, :] <= q_pos[:, None])
    return jnp.einsum('bghqc,bgcd->bghqd', prob.astype(vc.dtype), vc), prob


def cmp_sel_overlap(n_cmp, n_sel):
    cs = np.arange(n_cmp) * CMP_STRIDE
    ss = np.arange(n_sel) * SEL_BLOCK
    ov = np.minimum(cs[None] + CMP_BLOCK, ss[:, None] + SEL_BLOCK) - np.maximum(cs[None], ss[:, None])
    return jnp.asarray(np.clip(ov, 0, None) / CMP_BLOCK, dtype=jnp.float32)


def nsa_select(p_cmp, q_pos, n_sel):
    imp = jnp.einsum('bgqc,sc->bgqs', p_cmp.sum(axis=2), cmp_sel_overlap(p_cmp.shape[-1], n_sel))
    j = jnp.arange(n_sel)[None, :]
    qb = (q_pos // SEL_BLOCK)[:, None]
    valid = j <= qb
    forced = (j == 0) | (j == qb) | (j == qb - 1)
    score = jnp.where(valid, jnp.where(forced, FORCE_SCORE, imp), -jnp.inf)
    _, idx = lax.top_k(score, min(SEL_TOP, n_sel))
    sel_ok = jnp.take_along_axis(jnp.broadcast_to(valid, score.shape), idx, axis=-1)
    return idx, sel_ok


def sel_blocks(x, n_sel):
    B, L = x.shape[:2]
    x = jnp.pad(x, ((0, 0), (0, n_sel * SEL_BLOCK - L), (0, 0), (0, 0)))
    return jnp.moveaxis(x.reshape(B, n_sel, SEL_BLOCK, NSA_G, NSA_HD), 3, 1)


def nsa_slc_attend(q, kb, vb, idx, sel_ok, q_pos):
    B, G = kb.shape[:2]
    bi = jnp.arange(B)[:, None, None, None]
    gi = jnp.arange(G)[None, :, None, None]
    kg = kb[bi, gi, idx]
    vg = vb[bi, gi, idx]
    s = jnp.einsum('bghqd,bgqnld->bghqnl', q, kg) * NSA_HD ** -0.5
    kpos = idx[..., None] * SEL_BLOCK + jnp.arange(SEL_BLOCK)
    mask = (kpos <= q_pos[None, None, :, None, None]) & sel_ok[..., None]
    sh = s.shape
    prob = masked_softmax(s.reshape(sh[:4] + (-1,)), mask.reshape(B, G, 1, sh[3], -1))
    return jnp.einsum('bghqnl,bgqnld->bghqd', prob.reshape(sh).astype(vg.dtype), vg)


def window_attend_banded(q, k, v):
    B, G, HPG, T, HD = q.shape
    nb = T // WIN_BLOCK
    npv = WINDOW // WIN_BLOCK
    pad = ((0, 0), (0, 0), (npv * WIN_BLOCK, 0), (0, 0))

    def band(z):
        zb = jnp.pad(z, pad).reshape(B, G, nb + npv, WIN_BLOCK, HD)
        return jnp.concatenate([zb[:, :, j:j + nb] for j in range(npv + 1)], axis=3)

    kb, vb = band(k), band(v)
    qb = q.reshape(B, G, HPG, nb, WIN_BLOCK, HD)
    s = jnp.einsum('bghiqd,bgikd->bghiqk', qb, kb) * NSA_HD ** -0.5
    blk = jnp.arange(nb)[:, None]
    q_pos = blk * WIN_BLOCK + jnp.arange(WIN_BLOCK)[None]
    k_pos = (blk - npv) * WIN_BLOCK + jnp.arange((npv + 1) * WIN_BLOCK)[None]
    diff = q_pos[:, :, None] - k_pos[:, None, :]
    mask = (diff >= 0) & (diff < WINDOW) & (k_pos[:, None, :] >= 0)
    prob = masked_softmax(s, mask)
    return jnp.einsum('bghiqk,bgikd->bghiqd', prob.astype(v.dtype), vb).reshape(B, G, HPG, T, HD)


def window_attend_cached(q, k, v, q_pos, k_pos):
    s = jnp.einsum('bghqd,blgd->bghql', q, k) * NSA_HD ** -0.5
    diff = q_pos[:, None] - k_pos[None, :]
    prob = masked_softmax(s, (diff >= 0) & (diff < WINDOW))
    return jnp.einsum('bghql,blgd->bghqd', prob.astype(v.dtype), v)


def retention_chunk(S, q, k, v):
    f32 = jnp.float32
    C = q.shape[1]
    lg = jnp.log1p(-jnp.exp2(-5.0 - jnp.arange(RET_H, dtype=f32)))
    i = jnp.arange(C, dtype=f32)
    diff = i[:, None] - i[None, :]
    causal = diff >= 0
    dmask = jnp.where(causal, jnp.exp(jnp.where(causal, diff, 0.0)[None] * lg[:, None, None]), 0.0)
    qf, kf, vf = q.astype(f32), k.astype(f32), v.astype(f32)
    s = jnp.einsum('bihd,bjhd->bhij', qf, kf) * dmask
    o = jnp.einsum('bhij,bjhe->bihe', s, vf)
    o = o + jnp.einsum('bihd,bhde->bihe', qf, S) * jnp.exp((i[:, None] + 1.0) * lg[None, :])[None, :, :, None]
    k_dec = kf * jnp.exp((C - 1.0 - i)[:, None] * lg[None, :])[None, :, :, None]
    S = S * jnp.exp(C * lg)[None, :, None, None] + jnp.einsum('bjhd,bjhe->bhde', k_dec, vf)
    return S, o


def retention_prompt(q, k, v):
    B, T = q.shape[:2]
    n = T // RET_CHUNK
    xs = tuple(jnp.moveaxis(z.reshape((B, n, RET_CHUNK) + z.shape[2:]), 1, 0) for z in (q, k, v))
    s0 = jnp.zeros((B, RET_H, RET_DK, RET_DV), jnp.float32)
    S, o = lax.scan(lambda S, c: retention_chunk(S, c[0], c[1], c[2]), s0, xs)
    return S, jnp.moveaxis(o, 0, 1).reshape(B, T, RET_H, RET_DV)


def odd_output(o_cmp, o_slc, o_win, o_ret, pr, p):
    gates = pr['gates']
    B, T = gates.shape[:2]
    gg = jnp.moveaxis(gates.reshape(B, T, NSA_G, NSA_HPG, 3), 1, 3)[..., None]
    o = gg[..., 0, :] * o_cmp + gg[..., 1, :] * o_slc + gg[..., 2, :] * o_win
    o_nsa = jnp.moveaxis(o, 3, 1).reshape(B, T, NSA_W)
    y_ret = head_group_norm(o_ret, p['gn_g'], p['gn_b'], 1e-5).astype(o_nsa.dtype) * jax.nn.silu(pr['rg'])
    return jnp.concatenate([o_nsa, y_ret], axis=-1)


def odd_mixer_prompt(proj, p):
    B, T, _ = proj.shape
    pos = jnp.arange(T)
    pr = odd_project(proj, p, pos)
    qn = pr['q_n'].reshape(B, T, NSA_W)
    qr = pr['q_r'].reshape(B, T, NSA_W)
    kc = rms_norm(nsa_compress(pr['kc'], *p['ck']), p['k_norm'][0])
    vc = nsa_compress(pr['vc'], *p['cv'])
    n_cmp = kc.shape[1]
    n_sel = -(-T // SEL_BLOCK)
    o_cmp, sel = nsa_cmp_select(qn, _tile_cmp(kc), _tile_cmp(vc), _overlap_T(n_cmp, n_sel),
                                n_cmp=n_cmp, n_sel=n_sel, q_pos0=0)
    o_slc = nsa_flash(qr, _tile_groups(pr['ks']), _tile_groups(pr['vs']), sel, _sel_expand(T))
    o_win = nsa_flash(qr, _tile_groups(pr['kw']), _tile_groups(pr['vw']))
    S, o_ret = retention_prompt(pr['rq'], pr['rk'], pr['rv'])
    gates = pr['gates']
    heads = (B, T, NSA_H, NSA_HD)
    o_nsa = (gates[..., 0:1] * o_cmp.reshape(heads) + gates[..., 1:2] * o_slc.reshape(heads)
             + gates[..., 2:3] * o_win.reshape(heads)).reshape(B, T, NSA_W)
    y_ret = head_group_norm(o_ret, p['gn_g'], p['gn_b'], 1e-5).astype(o_nsa.dtype) * jax.nn.silu(pr['rg'])
    out = jnp.concatenate([o_nsa, y_ret], axis=-1)
    kv_rows = jnp.stack([pr['kc'], pr['vc'], pr['ks'], pr['vs']], axis=2)
    win = jnp.stack([pr['kw'], pr['vw']], axis=2)[:, T - min(WINDOW, T):]
    return out, kv_rows, win, S


def odd_mixer_sample(proj, p, cache_layer, page_table, win_buf, ret_s0):
    B, T, _ = proj.shape
    assert T == DEC_SEQ == 1 and win_buf.shape[1] == WIN_BUF
    pos = PAST_LEN + jnp.arange(T)
    pr = odd_project(proj, p, pos)
    scale = NSA_HD ** -0.5
    new_rows = jnp.stack([pr['ks'], pr['vs'], pr['kw'], pr['vw']], axis=2)[:, 0].reshape(B, 4, SLOT_ROWS)
    cache_t = jnp.transpose(cache_layer, (0, 2, 3, 4, 1)).reshape(cache_layer.shape[0], KV_ROWS, PAGE_SIZE)
    win_t = jnp.transpose(win_buf, (0, 2, 3, 4, 1)).reshape(B, 2 * SLOT_ROWS, WIN_BUF)
    wk = _dec_cmp_weights(*p['ck'])
    wv = _dec_cmp_weights(*p['cv'])
    w1t, b1, w2t, b2t = (jnp.stack([a, b]) for a, b in zip(wk, wv))
    kn = jnp.tile(p['k_norm'][0], NSA_G).reshape(1, SLOT_ROWS)
    t = np.arange(PAST_LEN)
    expand = jnp.asarray(np.arange(CMP_PAD)[:, None] == (t // SEL_BLOCK)[None, :], dtype=BF16)
    h = np.arange(NSA_H)
    grp = jnp.asarray((h[:, None] // NSA_HPG) == (h[None, :] // NSA_HPG), dtype=BF16)
    o16 = dec_nsa(page_table, cache_t, win_t, _place_heads(pr['q_n'][:, 0] * scale),
                  _place_heads(pr['q_r'][:, 0] * scale), new_rows, pr['gates'][:, 0],
                  w1t, b1, w2t, b2t, kn, _overlap_T(DEC_N_CMP, DEC_N_SEL), expand, grp)
    o_nsa = _take_heads(o16)[:, None, :]
    S, o_ret = retention_chunk(ret_s0.astype(jnp.float32), pr['rq'], pr['rk'], pr['rv'])
    y_ret = head_group_norm(o_ret, p['gn_g'], p['gn_b'], 1e-5).astype(o_nsa.dtype) * jax.nn.silu(pr['rg'])
    out = jnp.concatenate([o_nsa, y_ret], axis=-1)
    rows = jnp.stack([pr['kc'], pr['vc'], pr['ks'], pr['vs']], axis=2).astype(cache_layer.dtype)
    new_col = jnp.stack([pr['kw'], pr['vw']], axis=2)[:, 0].reshape(B, 2 * SLOT_ROWS, 1).astype(win_buf.dtype)
    win_new = jnp.concatenate([win_t[:, :, T:], new_col], axis=2).reshape(B, 2, NSA_G, NSA_HD, WIN_BUF)
    return out, rows, jnp.transpose(win_new, (0, 4, 1, 2, 3)), S


def _stack(xs, dt):
    return jnp.stack(xs).astype(dt)


def kernel(x_prompt, x_sample, state_lru_h, state_lru_conv, state_rwkv_shift, state_rwkv_wkv,
           cache_nsa_kv, cache_nsa_win, state_ret, page_table,
           norm_ffn1, ffn1_w_in, ffn1_w_out, norm_mix, norm_ffn2, ffn2_w_in, ffn2_w_out,
           ab_w_in, lru_conv_w, lru_conv_b, lru_wa, lru_ba, lru_wx, lru_bx, lru_lambda,
           rwkv_mu, rwkv_w0, rwkv_w2, rwkv_a0, rwkv_a2, rwkv_g2, rwkv_k_k, rwkv_k_a, rwkv_r_k,
           rwkv_ln_g, rwkv_ln_b, ab_w_out,
           cd_w_in, nsa_q_norm, nsa_k_norm, cmp_k_w1, cmp_k_b1, cmp_k_w2, cmp_k_b2,
           cmp_v_w1, cmp_v_b1, cmp_v_w2, cmp_v_b2, ret_gn_g, ret_gn_b, cd_w_out):
    dt = x_prompt.dtype
    B = x_prompt.shape[0]
    DB = x_sample.shape[0]
    y = jnp.concatenate([x_prompt.reshape(N_PROMPT, D_MODEL), x_sample.reshape(DB * DEC_SEQ, D_MODEL)], axis=0)
    lru_h_p, lru_h_s, lru_c_p, lru_c_s, sh_p, sh_s, wkv_p, wkv_s = [], [], [], [], [], [], [], []
    kv_p, kv_s, win_p, win_s, ret_p, ret_s = [], [], [], [], [], []
    for layer in range(DEPTH):
        li = layer // 2
        y = ffn_block(y, norm_ffn1[layer], *_prep_ffn_weights(ffn1_w_in[layer], ffn1_w_out[layer]))
        if layer % 2 == 0:
            p = {'conv_w': lru_conv_w[li], 'conv_b': lru_conv_b[li],
                 'wa': lru_wa[li], 'ba': lru_ba[li], 'wx': lru_wx[li], 'bx': lru_bx[li], 'lam': lru_lambda[li],
                 'mu': rwkv_mu[li], 'w0': rwkv_w0[li], 'w2': rwkv_w2[li], 'a0': rwkv_a0[li], 'a2': rwkv_a2[li],
                 'g2': rwkv_g2[li], 'k_k': rwkv_k_k[li], 'k_a': rwkv_k_a[li], 'r_k': rwkv_r_k[li],
                 'ln_g': rwkv_ln_g[li], 'ln_b': rwkv_ln_b[li]}
            proj = norm_matmul(y, norm_mix[layer], _prep_cols(ab_w_in[li]))[:, :AB_COLS]
            proj_p = proj[:N_PROMPT].reshape(B, SEQ, AB_COLS)
            proj_s = proj[N_PROMPT:].reshape(DB, DEC_SEQ, AB_COLS)
            cp, a0, a1, a2, a3 = even_mixer_core(
                proj_p, p, jnp.zeros((B, LRU_W), dt), jnp.zeros((B, CONV_W - 1, LRU_W), dt),
                jnp.zeros((B, SHIFT_W), dt), jnp.zeros((B, RWKV_H, RWKV_HD, RWKV_HD), dt))
            cs, b0, b1, b2, b3 = even_mixer_core(
                proj_s, p, state_lru_h[li], state_lru_conv[li], state_rwkv_shift[li], state_rwkv_wkv[li])
            lru_h_p.append(a0); lru_c_p.append(a1); sh_p.append(a2); wkv_p.append(a3)
            lru_h_s.append(b0); lru_c_s.append(b1); sh_s.append(b2); wkv_s.append(b3)
            w_out = ab_w_out[li]
        else:
            p = {'q_norm': nsa_q_norm[li], 'k_norm': nsa_k_norm[li],
                 'ck': (cmp_k_w1[li], cmp_k_b1[li], cmp_k_w2[li], cmp_k_b2[li]),
                 'cv': (cmp_v_w1[li], cmp_v_b1[li], cmp_v_w2[li], cmp_v_b2[li]),
                 'gn_g': ret_gn_g[li], 'gn_b': ret_gn_b[li]}
            proj = norm_matmul(y, norm_mix[layer], _prep_cols(cd_w_in[li]))[:, :CD_COLS]
            proj_p = proj[:N_PROMPT].reshape(B, SEQ, CD_COLS)
            proj_s = proj[N_PROMPT:].reshape(DB, DEC_SEQ, CD_COLS)
            cp, a0, a1, a2 = odd_mixer_prompt(proj_p, p)
            cs, b0, b1, b2 = odd_mixer_sample(proj_s, p, cache_nsa_kv[li], page_table, cache_nsa_win[li],
                                              state_ret[li])
            kv_p.append(a0); win_p.append(a1); ret_p.append(a2)
            kv_s.append(b0); win_s.append(b1); ret_s.append(b2)
            w_out = cd_w_out[li]
        cat = jnp.concatenate([cp.reshape(N_PROMPT, D_MODEL), cs.reshape(DB * DEC_SEQ, D_MODEL)], axis=0)
        y = matmul_residual(cat, w_out.astype(BF16), y)
        y = ffn_block(y, norm_ffn2[layer], *_prep_ffn_weights(ffn2_w_in[layer], ffn2_w_out[layer]))
    yp = y[:N_PROMPT].reshape(B, SEQ, D_MODEL)
    ys = y[N_PROMPT:].reshape(DB, DEC_SEQ, D_MODEL)
    return (yp, ys,
            _stack(lru_h_p, dt), _stack(lru_h_s, dt), _stack(lru_c_p, dt), _stack(lru_c_s, dt),
            _stack(sh_p, dt), _stack(sh_s, dt), _stack(wkv_p, dt), _stack(wkv_s, dt),
            _stack(kv_p, dt), _stack(kv_s, dt), _stack(win_p, dt), _stack(win_s, dt),
            _stack(ret_p, dt), _stack(ret_s, dt))
```

```python
import functools

import jax
import jax.numpy as jnp
import numpy as np
from jax import lax
from jax.experimental import pallas as pl
from jax.experimental.pallas import tpu as pltpu

D_MODEL = 2048
BATCH = 4
SEQ = 2048
DEPTH = 2
DEC_BATCH = 128
DEC_SEQ = 1
PAST_LEN = 2048
PAGE_SIZE = 128
D_FF = 5504
LRU_W = D_MODEL // 2
LRU_BLOCKS = 16
LRU_BS = LRU_W // LRU_BLOCKS
CONV_W = 4
LRU_C = 8.0
RWKV_W = D_MODEL // 2
RWKV_HD = 64
RWKV_H = RWKV_W // RWKV_HD
W_LORA = 64
A_LORA = 64
G_LORA = 160
SHIFT_W = 3 * RWKV_W + W_LORA + A_LORA + G_LORA
AB_COLS = 2 * LRU_W + SHIFT_W
NSA_H = 16
NSA_G = 4
NSA_HPG = NSA_H // NSA_G
NSA_HD = 64
NSA_W = NSA_H * NSA_HD
ROPE_DIMS = NSA_HD // 4
ROPE_THETA = 500000.0
CMP_BLOCK = 32
CMP_STRIDE = 16
CMP_R = CMP_BLOCK // CMP_STRIDE
CMP_HID = 256
SEL_BLOCK = 64
SEL_TOP = 16
SEL_Q_BLOCK = 64
WINDOW = 512
WIN_BLOCK = 128
FORCE_SCORE = 1e4
KV_SLOTS = 4
RET_H = 8
RET_DK = 64
RET_DV = 128
RET_W = RET_H * RET_DV
RET_CHUNK = 128
RET_THETA = 10000.0
CD_COLS = NSA_W + 6 * NSA_G * NSA_HD + 3 * NSA_H + 2 * RET_H * RET_DK + 2 * RET_W

N_TOK = BATCH * SEQ + DEC_BATCH * DEC_SEQ
N_PROMPT = BATCH * SEQ

LANE = 128
VMEM_LIMIT_BYTES = 56 * 1024 * 1024
ROW_TILE = 640
FF_TILE = 512
D_FF_PAD = 5632
COL_TILE = 512

BF16 = jnp.bfloat16
F32 = jnp.float32


def _round_up(n, m):
    return -(-n // m) * m


def _rms_rows(x, g):
    ms = jnp.mean(x * x, axis=-1, keepdims=True)
    return x * lax.rsqrt(ms + 1e-6) * g


def _ffn_kernel(x_ref, g_ref, wg_ref, wu_ref, wo_ref, o_ref, xn_ref, acc_ref):
    k = pl.program_id(1)

    @pl.when(k == 0)
    def _():
        xn_ref[...] = _rms_rows(x_ref[...], g_ref[...]).astype(BF16)
        acc_ref[...] = jnp.zeros_like(acc_ref)

    xn = xn_ref[...]
    gate = jnp.dot(xn, wg_ref[...], preferred_element_type=F32)
    up = jnp.dot(xn, wu_ref[...], preferred_element_type=F32)
    act = gate * jax.nn.sigmoid(gate) * up
    acc_ref[...] += jnp.dot(act.astype(BF16), wo_ref[...], preferred_element_type=F32)

    @pl.when(k == pl.num_programs(1) - 1)
    def _():
        o_ref[...] = x_ref[...] + 0.5 * acc_ref[...]


def ffn_block(x, g, wg, wu, wo):
    m, d = x.shape
    return pl.pallas_call(
        _ffn_kernel,
        grid=(m // ROW_TILE, D_FF_PAD // FF_TILE),
        in_specs=[
            pl.BlockSpec((ROW_TILE, d), lambda i, k: (i, 0)),
            pl.BlockSpec((1, d), lambda i, k: (0, 0)),
            pl.BlockSpec((d, FF_TILE), lambda i, k: (0, k)),
            pl.BlockSpec((d, FF_TILE), lambda i, k: (0, k)),
            pl.BlockSpec((FF_TILE, d), lambda i, k: (k, 0)),
        ],
        out_specs=pl.BlockSpec((ROW_TILE, d), lambda i, k: (i, 0)),
        out_shape=jax.ShapeDtypeStruct((m, d), F32),
        scratch_shapes=[pltpu.VMEM((ROW_TILE, d), BF16), pltpu.VMEM((ROW_TILE, d), F32)],
        compiler_params=pltpu.CompilerParams(
            dimension_semantics=("parallel", "arbitrary"), vmem_limit_bytes=VMEM_LIMIT_BYTES),
        name="ffn_block",
    )(x, g.reshape(1, d), wg, wu, wo)


def _norm_matmul_kernel(x_ref, g_ref, w_ref, o_ref, xn_ref):
    @pl.when(pl.program_id(1) == 0)
    def _():
        xn_ref[...] = _rms_rows(x_ref[...], g_ref[...]).astype(BF16)

    o_ref[...] = jnp.dot(xn_ref[...], w_ref[...], preferred_element_type=F32)


def norm_matmul(x, g, w):
    m, k = x.shape
    n = w.shape[1]
    return pl.pallas_call(
        _norm_matmul_kernel,
        grid=(m // ROW_TILE, n // COL_TILE),
        in_specs=[
            pl.BlockSpec((ROW_TILE, k), lambda i, j: (i, 0)),
            pl.BlockSpec((1, k), lambda i, j: (0, 0)),
            pl.BlockSpec((k, COL_TILE), lambda i, j: (0, j)),
        ],
        out_specs=pl.BlockSpec((ROW_TILE, COL_TILE), lambda i, j: (i, j)),
        out_shape=jax.ShapeDtypeStruct((m, n), F32),
        scratch_shapes=[pltpu.VMEM((ROW_TILE, k), BF16)],
        compiler_params=pltpu.CompilerParams(
            dimension_semantics=("parallel", "arbitrary"), vmem_limit_bytes=VMEM_LIMIT_BYTES),
        name="norm_matmul",
    )(x, g.reshape(1, k), w)


def _matmul_residual_kernel(a_ref, w_ref, r_ref, o_ref):
    o_ref[...] = r_ref[...] + jnp.dot(a_ref[...].astype(BF16), w_ref[...], preferred_element_type=F32)


def matmul_residual(a, w, res):
    m, k = a.shape
    n = w.shape[1]
    return pl.pallas_call(
        _matmul_residual_kernel,
        grid=(m // ROW_TILE, n // COL_TILE),
        in_specs=[
            pl.BlockSpec((ROW_TILE, k), lambda i, j: (i, 0)),
            pl.BlockSpec((k, COL_TILE), lambda i, j: (0, j)),
            pl.BlockSpec((ROW_TILE, COL_TILE), lambda i, j: (i, j)),
        ],
        out_specs=pl.BlockSpec((ROW_TILE, COL_TILE), lambda i, j: (i, j)),
        out_shape=jax.ShapeDtypeStruct((m, n), F32),
        compiler_params=pltpu.CompilerParams(
            dimension_semantics=("parallel", "arbitrary"), vmem_limit_bytes=VMEM_LIMIT_BYTES),
        name="matmul_residual",
    )(a, w, res)


def _prep_ffn_weights(w_in, w_out):
    pad = D_FF_PAD - D_FF
    wg = jnp.pad(w_in[:, :D_FF], ((0, 0), (0, pad))).astype(BF16)
    wu = jnp.pad(w_in[:, D_FF:], ((0, 0), (0, pad))).astype(BF16)
    wo = jnp.pad(w_out, ((0, pad), (0, 0))).astype(BF16)
    return wg, wu, wo


def _prep_cols(w):
    n = w.shape[1]
    return jnp.pad(w, ((0, 0), (0, _round_up(n, COL_TILE) - n))).astype(BF16)


SCAN_TILE = 256


def _lru_scan_kernel(a_ref, b_ref, h0_ref, o_ref, carry_ref):
    @pl.when(pl.program_id(1) == 0)
    def _():
        carry_ref[...] = h0_ref[...]

    a = a_ref[...]
    b = b_ref[...]
    rows = lax.broadcasted_iota(jnp.int32, a.shape, 0)
    k = 1
    while k < a.shape[0]:
        keep = rows >= k
        b = jnp.where(keep, a * pltpu.roll(b, k, 0) + b, b)
        a = jnp.where(keep, a * pltpu.roll(a, k, 0), a)
        k *= 2
    h = a * carry_ref[...] + b
    o_ref[...] = h
    carry_ref[...] = h[a.shape[0] - 1:, :]


def lru_scan(a, b, h0):
    B, T, W = a.shape
    tt = min(SCAN_TILE, T)
    return pl.pallas_call(
        _lru_scan_kernel,
        grid=(B, T // tt),
        in_specs=[
            pl.BlockSpec((None, tt, W), lambda i, t: (i, t, 0)),
            pl.BlockSpec((None, tt, W), lambda i, t: (i, t, 0)),
            pl.BlockSpec((None, 1, W), lambda i, t: (i, 0, 0)),
        ],
        out_specs=pl.BlockSpec((None, tt, W), lambda i, t: (i, t, 0)),
        out_shape=jax.ShapeDtypeStruct((B, T, W), F32),
        scratch_shapes=[pltpu.VMEM((1, W), F32)],
        compiler_params=pltpu.CompilerParams(
            dimension_semantics=("parallel", "arbitrary"), vmem_limit_bytes=VMEM_LIMIT_BYTES),
        name="lru_scan",
    )(a, b, h0.reshape(B, 1, W))


GROUP_W = NSA_HPG * NSA_HD
ATT_Q_TILE = 128
ATT_K_TILE = 256
CMP_PAD = 128
NEG_BIG = -1e30


def _stack_heads(q):
    head = lax.broadcasted_iota(jnp.int32, q.shape, 1) // NSA_HD
    return jnp.concatenate([jnp.where(head == h, q, 0.0) for h in range(NSA_HPG)], axis=0)


def _unstack_heads(o, tq):
    head = lax.broadcasted_iota(jnp.int32, (tq, GROUP_W), 1) // NSA_HD
    out = jnp.zeros((tq, GROUP_W), F32)
    for h in range(NSA_HPG):
        out = out + jnp.where(head == h, o[h * tq:(h + 1) * tq], 0.0)
    return out


def _cmp_select_kernel(q_ref, k_ref, v_ref, ov_ref, o_ref, sel_ref, *, n_cmp, n_sel, q_pos0):
    tq = q_ref.shape[0]
    i = pl.program_id(2)
    qs = _stack_heads(q_ref[...] * (NSA_HD ** -0.5)).astype(BF16)
    s = lax.dot_general(qs, k_ref[...], (((1,), (1,)), ((), ())), preferred_element_type=F32)
    q_pos = q_pos0 + i * tq + lax.broadcasted_iota(jnp.int32, (tq, CMP_PAD), 0)
    c = lax.broadcasted_iota(jnp.int32, (tq, CMP_PAD), 1)
    mask1 = (c < n_cmp) & (c * CMP_STRIDE + (CMP_BLOCK - 1) <= q_pos)
    mask = jnp.concatenate([mask1] * NSA_HPG, axis=0)
    s = jnp.where(mask, s, NEG_BIG)
    m = jnp.max(s, axis=-1, keepdims=True)
    e = jnp.where(mask, jnp.exp(s - m), 0.0)
    den = jnp.sum(e, axis=-1, keepdims=True)
    prob = e / jnp.where(den > 0, den, 1.0)
    o = jnp.dot(prob.astype(BF16), v_ref[...], preferred_element_type=F32)
    o_ref[...] = _unstack_heads(o, tq)
    psum = prob[0:tq]
    for h in range(1, NSA_HPG):
        psum = psum + prob[h * tq:(h + 1) * tq]
    imp = jnp.dot(psum.astype(BF16), ov_ref[...], preferred_element_type=F32)
    qb = q_pos // SEL_BLOCK
    valid = (c <= qb) & (c < n_sel)
    forced = (c == 0) | (c == qb) | (c == qb - 1)
    score = jnp.where(valid, jnp.where(forced, FORCE_SCORE, imp), -jnp.inf)
    rank = jnp.zeros((tq, CMP_PAD), F32)
    for jp in range(n_sel):
        col = score[:, jp:jp + 1]
        beats = (col > score) | ((col == score) & (c > jp))
        rank = rank + jnp.where(beats, 1.0, 0.0)
    sel_ref[...] = jnp.where((rank < min(SEL_TOP, n_sel)) & (c < n_sel), 1.0, 0.0)


def nsa_cmp_select(qn, kc4, vc4, ovT, *, n_cmp, n_sel, q_pos0):
    B, T, _ = qn.shape
    tq = min(ATT_Q_TILE, T)
    return pl.pallas_call(
        functools.partial(_cmp_select_kernel, n_cmp=n_cmp, n_sel=n_sel, q_pos0=q_pos0),
        grid=(B, NSA_G, T // tq),
        in_specs=[
            pl.BlockSpec((None, tq, GROUP_W), lambda b, g, i: (b, i, g)),
            pl.BlockSpec((None, None, CMP_PAD, GROUP_W), lambda b, g, i: (b, g, 0, 0)),
            pl.BlockSpec((None, None, CMP_PAD, GROUP_W), lambda b, g, i: (b, g, 0, 0)),
            pl.BlockSpec((CMP_PAD, CMP_PAD), lambda b, g, i: (0, 0)),
        ],
        out_specs=[
            pl.BlockSpec((None, tq, GROUP_W), lambda b, g, i: (b, i, g)),
            pl.BlockSpec((None, None, tq, CMP_PAD), lambda b, g, i: (b, g, i, 0)),
        ],
        out_shape=[jax.ShapeDtypeStruct((B, T, NSA_W), F32),
                   jax.ShapeDtypeStruct((B, NSA_G, T, CMP_PAD), F32)],
        compiler_params=pltpu.CompilerParams(
            dimension_semantics=("parallel", "parallel", "parallel"), vmem_limit_bytes=VMEM_LIMIT_BYTES),
        name="nsa_cmp_select",
    )(qn, kc4, vc4, ovT)


def _flash_kernel(*refs, selected):
    if selected:
        q_ref, k_ref, v_ref, sel_ref, exp_ref, o_ref, m_ref, l_ref, acc_ref, s_a, s_b = refs
    else:
        q_ref, k_ref, v_ref, o_ref, m_ref, l_ref, acc_ref, s_a, s_b = refs
    tq = q_ref.shape[0]
    tk = ATT_K_TILE
    n_tiles = k_ref.shape[0] // tk
    i = pl.program_id(2)
    q = q_ref[...] * (NSA_HD ** -0.5)
    head = lax.broadcasted_iota(jnp.int32, q.shape, 1) // NSA_HD
    q4 = _stack_heads(q).astype(BF16)
    m_ref[...] = jnp.full(m_ref.shape, NEG_BIG, F32)
    l_ref[...] = jnp.zeros(l_ref.shape, F32)
    acc_ref[...] = jnp.zeros(acc_ref.shape, F32)
    q_pos = i * tq + lax.broadcasted_iota(jnp.int32, (tq, tk), 0)
    col = lax.broadcasted_iota(jnp.int32, (tq, tk), 1)
    if selected:
        sel = sel_ref[...].astype(BF16)
        lo = 0
    else:
        lo = jnp.maximum(i * tq - (WINDOW - 1), 0) // tk
    hi = (i * tq + tq - 1) // tk + 1

    def tile_start(j):
        return pl.multiple_of(jnp.minimum(j, n_tiles - 1) * tk, tk)

    def scores(j, s_ref):
        s_ref[...] = lax.dot_general(q4, k_ref[pl.ds(tile_start(j), tk), :], (((1,), (1,)), ((), ())),
                                     preferred_element_type=F32)

    def consume(j, s_ref):
        v = v_ref[pl.ds(tile_start(j), tk), :]
        k_pos = j * tk + col
        mask = k_pos <= q_pos
        if selected:
            mask = mask & (jnp.dot(sel, exp_ref[jnp.minimum(j, n_tiles - 1)], preferred_element_type=F32) > 0.5)
        else:
            mask = mask & (q_pos - k_pos < WINDOW)
        for h in range(NSA_HPG):
            s = jnp.where(mask, s_ref[h * tq:(h + 1) * tq, :], NEG_BIG)
            m_old = m_ref[h]
            m_new = jnp.maximum(m_old, jnp.max(s, axis=-1, keepdims=True))
            alpha = jnp.exp(m_old - m_new)
            p = jnp.where(mask, jnp.exp(s - pltpu.repeat(m_new, tk // LANE, axis=1)), 0.0)
            l_ref[h] = alpha * l_ref[h] + jnp.sum(p, axis=-1, keepdims=True)
            acc_ref[h] = (pltpu.repeat(alpha, GROUP_W // LANE, axis=1) * acc_ref[h]
                          + jnp.dot(p.astype(BF16), v, preferred_element_type=F32))
            m_ref[h] = m_new

    scores(lo, s_a)

    def body(t, carry):
        j = lo + 2 * t
        scores(j + 1, s_b)
        consume(j, s_a)
        scores(j + 2, s_a)
        consume(j + 1, s_b)
        return carry

    lax.fori_loop(0, (hi - lo + 1) // 2, body, 0)
    out = jnp.zeros((tq, GROUP_W), F32)
    for h in range(NSA_HPG):
        den = pltpu.repeat(l_ref[h], GROUP_W // LANE, axis=1)
        out = out + jnp.where(head == h, acc_ref[h] / jnp.where(den > 0, den, 1.0), 0.0)
    o_ref[...] = out


def nsa_flash(qr, k4, v4, sel=None, expand=None):
    B, T, _ = qr.shape
    tq = ATT_Q_TILE
    selected = sel is not None
    in_specs = [
        pl.BlockSpec((None, tq, GROUP_W), lambda b, g, i: (b, i, g)),
        pl.BlockSpec((None, T, GROUP_W), lambda b, g, i: (b, 0, g)),
        pl.BlockSpec((None, T, GROUP_W), lambda b, g, i: (b, 0, g)),
    ]
    args = [qr, k4, v4]
    if selected:
        in_specs += [
            pl.BlockSpec((None, None, tq, CMP_PAD), lambda b, g, i: (b, g, i, 0)),
            pl.BlockSpec(expand.shape, lambda b, g, i: (0, 0, 0)),
        ]
        args += [sel, expand]
    return pl.pallas_call(
        functools.partial(_flash_kernel, selected=selected),
        grid=(B, NSA_G, T // tq),
        in_specs=in_specs,
        out_specs=pl.BlockSpec((None, tq, GROUP_W), lambda b, g, i: (b, i, g)),
        out_shape=jax.ShapeDtypeStruct((B, T, NSA_W), F32),
        scratch_shapes=[pltpu.VMEM((NSA_HPG, tq, LANE), F32), pltpu.VMEM((NSA_HPG, tq, LANE), F32),
                        pltpu.VMEM((NSA_HPG, tq, GROUP_W), F32),
                        pltpu.VMEM((NSA_HPG * tq, ATT_K_TILE), F32), pltpu.VMEM((NSA_HPG * tq, ATT_K_TILE), F32)],
        compiler_params=pltpu.CompilerParams(
            dimension_semantics=("parallel", "parallel", "parallel"), vmem_limit_bytes=VMEM_LIMIT_BYTES),
        name="nsa_flash_sel" if selected else "nsa_flash_win",
    )(*args)


def _tile_groups(x):
    B, T = x.shape[:2]
    return jnp.broadcast_to(x[:, :, :, None, :], (B, T, NSA_G, NSA_HPG, NSA_HD)).reshape(B, T, NSA_W).astype(BF16)


def _tile_cmp(x):
    B, n = x.shape[:2]
    x = jnp.pad(jnp.moveaxis(x, 1, 2), ((0, 0), (0, 0), (0, CMP_PAD - n), (0, 0)))
    return jnp.tile(x, (1, 1, 1, NSA_HPG)).astype(BF16)


def _overlap_T(n_cmp, n_sel):
    ov = np.zeros((CMP_PAD, CMP_PAD), np.float32)
    cs = np.arange(n_cmp) * CMP_STRIDE
    ss = np.arange(n_sel) * SEL_BLOCK
    o = np.minimum(cs[None] + CMP_BLOCK, ss[:, None] + SEL_BLOCK) - np.maximum(cs[None], ss[:, None])
    ov[:n_cmp, :n_sel] = (np.clip(o, 0, None) / CMP_BLOCK).T
    return jnp.asarray(ov, dtype=BF16)


def _sel_expand(T):
    t = np.arange(T)
    e = (np.arange(CMP_PAD)[:, None] == (t // SEL_BLOCK)[None, :]).astype(np.float32)
    return jnp.asarray(e.reshape(CMP_PAD, T // ATT_K_TILE, ATT_K_TILE).transpose(1, 0, 2), dtype=BF16)


KV_ROWS = KV_SLOTS * NSA_G * NSA_HD
SLOT_ROWS = NSA_G * NSA_HD
N_PAGES = PAST_LEN // PAGE_SIZE
DEC_N_CHUNK = (PAST_LEN + DEC_SEQ) // CMP_STRIDE
DEC_N_CMP = DEC_N_CHUNK - CMP_R + 1
DEC_N_SEL = -(-(PAST_LEN + DEC_SEQ) // SEL_BLOCK)
WIN_BUF = min(WINDOW, PAST_LEN)


def _softmax_rows(s, mask, s_new=None):
    s = jnp.where(mask, s, NEG_BIG)
    m = jnp.max(s, axis=-1, keepdims=True)
    if s_new is not None:
        m = jnp.maximum(m, s_new)
    e = jnp.where(mask, jnp.exp(s - m), 0.0)
    den = jnp.sum(e, axis=-1, keepdims=True)
    if s_new is None:
        return e, den
    e_new = jnp.exp(s_new - m)
    return e, e_new, den + e_new


def _dec_nsa_kernel(pt_ref, *refs):
    pages = refs[:N_PAGES]
    (win_ref, qn_ref, qr_ref, new_ref, gate_ref, w1_ref, b1_ref, w2_ref, b2_ref, kn_ref,
     ov_ref, exp_ref, grp_ref, o_ref, xt_ref, acc_ref) = refs[N_PAGES:]
    del pt_ref
    f32 = F32
    half = 2 * NSA_HD
    n_chunk = DEC_N_CHUNK

    for p in range(N_PAGES):
        for sg in range(4):
            xt_ref[sg, p * PAGE_SIZE:(p + 1) * PAGE_SIZE, :] = pages[p][sg * half:(sg + 1) * half, :].T

    lane_lo = lax.broadcasted_iota(jnp.int32, (n_chunk, 2 * half), 1) % half < NSA_HD
    lane_grp = lax.broadcasted_iota(jnp.int32, (n_chunk, SLOT_ROWS), 1) // NSA_HD
    cmp_rows = []
    for slot in range(2):
        for gp in range(2):
            for rp in range(CMP_STRIDE // 2):
                xr = jnp.concatenate(
                    [xt_ref[slot * 2 + gp, pl.ds(2 * rp + j, n_chunk, stride=CMP_STRIDE), :] for j in range(2)],
                    axis=1)
                xs = jnp.concatenate([jnp.where(lane_lo, xr, 0.0), jnp.where(lane_lo, 0.0, xr)],
                                     axis=0).astype(BF16)
                part = jnp.dot(xs, w1_ref[slot, rp], preferred_element_type=f32)
                rows = pl.ds(gp * 2 * n_chunk, 2 * n_chunk)
                if rp == 0:
                    acc_ref[rows, :] = part
                else:
                    acc_ref[rows, :] += part
        acc = acc_ref[...]
        pre = b1_ref[slot] + acc[:, :CMP_HID] + pltpu.roll(acc[:, CMP_HID:], NSA_G * n_chunk - 1, 0)
        out = jnp.dot(jax.nn.gelu(pre).astype(BF16), w2_ref[slot], preferred_element_type=f32) + b2_ref[slot]
        if slot == 0:
            out = _rms_rows(out, kn_ref[...])
        sel_rows = jnp.zeros((n_chunk, SLOT_ROWS), f32)
        for g in range(NSA_G):
            sel_rows = sel_rows + jnp.where(lane_grp == g, out[g * n_chunk:(g + 1) * n_chunk], 0.0)
        cmp_rows.append(sel_rows.astype(BF16))
    kc, vc = cmp_rows

    qn = qn_ref[...].astype(BF16)
    qr = qr_ref[...].astype(BF16)
    nt = (((1,), (1,)), ((), ()))
    c = lax.broadcasted_iota(jnp.int32, (NSA_H, CMP_PAD), 1)
    s = lax.dot_general(qn, kc, nt, preferred_element_type=f32)
    e, den = _softmax_rows(s, c < DEC_N_CMP)
    prob = e / jnp.where(den > 0, den, 1.0)
    o_cmp = jnp.dot(prob.astype(BF16), vc, preferred_element_type=f32)
    p_hi, p_mid = _split_bf16(prob)
    p_lo = (prob - p_hi.astype(f32) - p_mid.astype(f32)).astype(BF16)
    grp = grp_ref[...]
    psum = (jnp.dot(grp, p_hi, preferred_element_type=f32) + jnp.dot(grp, p_mid, preferred_element_type=f32)
            + jnp.dot(grp, p_lo, preferred_element_type=f32))
    imp = jnp.dot(psum.astype(BF16), ov_ref[...], preferred_element_type=f32)
    qb = (PAST_LEN + DEC_SEQ - 1) // SEL_BLOCK
    valid = c <= qb
    forced = (c == 0) | (c == qb) | (c == qb - 1)
    score = jnp.where(valid, jnp.where(forced, FORCE_SCORE, imp), -jnp.inf)
    rank = jnp.zeros((NSA_H, CMP_PAD), f32)
    for jp in range(DEC_N_SEL):
        col = score[:, jp:jp + 1]
        rank = rank + jnp.where((col > score) | ((col == score) & (c > jp)), 1.0, 0.0)
    sel = jnp.where((rank < min(SEL_TOP, DEC_N_SEL)) & (c < DEC_N_SEL), 1.0, 0.0).astype(BF16)

    new = new_ref[...]
    new_b = new.astype(BF16).astype(f32)
    qr_f = qr.astype(f32)
    s_pages = [jnp.dot(qr, pages[p][2 * SLOT_ROWS:3 * SLOT_ROWS, :].astype(BF16), preferred_element_type=f32)
               for p in range(N_PAGES)]
    s = jnp.concatenate(s_pages, axis=1)
    mask = jnp.dot(sel, exp_ref[...], preferred_element_type=f32) > 0.5
    s_new = jnp.sum(qr_f * new_b[0:1], axis=-1, keepdims=True)
    e, e_new, den = _softmax_rows(s, mask, s_new)
    e = e.astype(BF16)
    o_slc = e_new.astype(BF16).astype(f32) * new_b[1:2]
    for p in range(N_PAGES):
        o_slc = o_slc + lax.dot_general(e[:, p * PAGE_SIZE:(p + 1) * PAGE_SIZE],
                                        pages[p][3 * SLOT_ROWS:4 * SLOT_ROWS, :].astype(BF16), nt,
                                        preferred_element_type=f32)
    o_slc = o_slc / den

    s = jnp.dot(qr, win_ref[0:SLOT_ROWS, :].astype(BF16), preferred_element_type=f32)
    i_buf = lax.broadcasted_iota(jnp.int32, (NSA_H, WIN_BUF), 1)
    s_new = jnp.sum(qr_f * new_b[2:3], axis=-1, keepdims=True)
    e, e_new, den = _softmax_rows(s, WIN_BUF - i_buf < WINDOW, s_new)
    o_win = e_new.astype(BF16).astype(f32) * new_b[3:4] + lax.dot_general(
        e.astype(BF16), win_ref[SLOT_ROWS:2 * SLOT_ROWS, :].astype(BF16), nt, preferred_element_type=f32)
    o_win = o_win / den

    gates = gate_ref[...]
    o_ref[...] = gates[:, 0:1] * o_cmp + gates[:, 1:2] * o_slc + gates[:, 2:3] * o_win


def dec_nsa(page_table, cache_t, win_t, qn16, qr16, new_rows, gates, w1t, b1, w2t, b2t, kn, ovT, expand, grp):
    DB = qn16.shape[0]
    const = lambda shape: pl.BlockSpec(shape, lambda b, pt: (0,) * len(shape))
    per_b = lambda shape: pl.BlockSpec((None,) + shape, lambda b, pt: (b,) + (0,) * len(shape))
    page_specs = [pl.BlockSpec((None, KV_ROWS, PAGE_SIZE), functools.partial(lambda b, pt, p: (pt[b, p], 0, 0), p=p))
                  for p in range(N_PAGES)]
    in_specs = page_specs + [
        per_b((2 * SLOT_ROWS, WIN_BUF)), per_b((NSA_H, SLOT_ROWS)), per_b((NSA_H, SLOT_ROWS)),
        per_b((4, SLOT_ROWS)), per_b((NSA_H, 3)),
        const(w1t.shape), const(b1.shape), const(w2t.shape), const(b2t.shape), const(kn.shape),
        const(ovT.shape), const(expand.shape), const(grp.shape),
    ]
    grid_spec = pltpu.PrefetchScalarGridSpec(
        num_scalar_prefetch=1, grid=(DB,), in_specs=in_specs,
        out_specs=pl.BlockSpec((None, NSA_H, SLOT_ROWS), lambda b, pt: (b, 0, 0)),
        scratch_shapes=[pltpu.VMEM((4, PAST_LEN, 2 * NSA_HD), F32),
                        pltpu.VMEM((NSA_G * DEC_N_CHUNK, CMP_R * CMP_HID), F32)])
    return pl.pallas_call(
        _dec_nsa_kernel,
        grid_spec=grid_spec,
        out_shape=jax.ShapeDtypeStruct((DB, NSA_H, SLOT_ROWS), F32),
        compiler_params=pltpu.CompilerParams(
            dimension_semantics=("arbitrary",), vmem_limit_bytes=VMEM_LIMIT_BYTES),
        name="dec_nsa",
    )(page_table, *([cache_t] * N_PAGES), win_t, qn16, qr16, new_rows, gates, w1t, b1, w2t, b2t, kn, ovT, expand, grp)


def _dec_cmp_weights(w1, b1, w2, b2):
    w = w1.reshape(CMP_R, CMP_STRIDE // 2, 2, 1, NSA_HD, CMP_HID)
    w = jnp.broadcast_to(w, (CMP_R, CMP_STRIDE // 2, 2, 2, NSA_HD, CMP_HID))
    w = jnp.moveaxis(w, 0, 4).reshape(CMP_STRIDE // 2, 4 * NSA_HD, CMP_R * CMP_HID)
    return (w.astype(BF16), b1.reshape(1, CMP_HID), jnp.tile(w2, (1, NSA_G)).astype(BF16),
            jnp.tile(b2, NSA_G).reshape(1, SLOT_ROWS))


def _place_heads(q):
    own = (jnp.arange(NSA_H)[:, None] // NSA_HPG) == jnp.arange(NSA_G)[None, :]
    return jnp.where(own[None, :, :, None], q[:, :, None, :], 0.0).reshape(q.shape[0], NSA_H, SLOT_ROWS)


def _take_heads(o):
    o = o.reshape(o.shape[0], NSA_H, NSA_G, NSA_HD)
    return o[:, jnp.arange(NSA_H), jnp.arange(NSA_H) // NSA_HPG, :].reshape(o.shape[0], NSA_W)


WKV_C = 64
WKV_PAIR = 2 * RWKV_HD
WKV_T_TILE = 512
WKV_PAIRS_PER_STEP = 4


def _split_bf16(x):
    hi = x.astype(BF16)
    return hi, (x - hi.astype(F32)).astype(BF16)


def _dot3(a, b):
    a_hi, a_lo = _split_bf16(a)
    b_hi, b_lo = _split_bf16(b)
    return (jnp.dot(a_hi, b_hi, preferred_element_type=F32) + jnp.dot(a_hi, b_lo, preferred_element_type=F32)
            + jnp.dot(a_lo, b_hi, preferred_element_type=F32))


def _wkv_kernel(r_ref, lw_ref, k_ref, v_ref, a_ref, b_ref, s0_ref, y_ref, sT_ref, s_scr):
    C = WKV_C
    P = WKV_PAIR
    n_chunks = r_ref.shape[0] // C

    @pl.when(pl.program_id(2) == 0)
    def _():
        s_scr[...] = s0_ref[...]

    lo_lane = lax.broadcasted_iota(jnp.int32, (C, P), 1) < RWKV_HD
    row = lax.broadcasted_iota(jnp.int32, (2 * C, 2 * C), 0)
    col = lax.broadcasted_iota(jnp.int32, (2 * C, 2 * C), 1)
    same_head = (row // C) == (col // C)
    strict = same_head & (row > col)
    lower = same_head & (row >= col)
    eye = jnp.where(row == col, 1.0, 0.0)
    tril = jnp.where(lax.broadcasted_iota(jnp.int32, (C, C), 0) >= lax.broadcasted_iota(jnp.int32, (C, C), 1),
                     1.0, 0.0).astype(BF16)

    def stack(x):
        return jnp.concatenate([jnp.where(lo_lane, x, 0.0), jnp.where(lo_lane, 0.0, x)], axis=0)

    def chunk(c, carry):
        stages = [pair_chunk(c, q) for q in range(WKV_PAIRS_PER_STEP)]
        while stages:
            stages = [g for g in stages if next(g, True) is None]
        return carry

    def pair_chunk(c, q):
        sl = pl.ds(pl.multiple_of(c * C, C), C)
        lanes = slice(q * P, (q + 1) * P)
        r, lw, k, v, a, b = (ref[sl, lanes] for ref in (r_ref, lw_ref, k_ref, v_ref, a_ref, b_ref))
        lw_hi, lw_mid = _split_bf16(lw)
        lw_lo = (lw - lw_hi.astype(F32) - lw_mid.astype(F32)).astype(BF16)
        cs = (jnp.dot(tril, lw_hi, preferred_element_type=F32) + jnp.dot(tril, lw_mid, preferred_element_type=F32)
              + jnp.dot(tril, lw_lo, preferred_element_type=F32))
        yield
        g_inv = jnp.exp(-cs)
        g_end = jnp.exp(cs[C - 1:C, :] - cs)
        a2 = stack(a * jnp.exp(cs - lw))
        r2 = stack(r * jnp.exp(cs))
        b2 = stack(b * g_inv)
        k2 = stack(k * g_inv)
        v2 = stack(v)
        s_old = s_scr[q]
        ar = jnp.concatenate([a2, r2], axis=0).astype(BF16)
        bk = jnp.concatenate([b2, k2], axis=0).astype(BF16)
        nt = (((1,), (1,)), ((), ()))
        pp = lax.dot_general(ar, bk, nt, preferred_element_type=F32)
        from_state = lax.dot_general(ar, s_old.astype(BF16), nt, preferred_element_type=F32)
        yield
        l_ab = jnp.where(strict, pp[:2 * C, :2 * C], 0.0)
        l_ak = jnp.where(strict, pp[:2 * C, 2 * C:], 0.0)
        m_rb = jnp.where(lower, pp[2 * C:, :2 * C], 0.0)
        m_rk = jnp.where(lower, pp[2 * C:, 2 * C:], 0.0)
        v2b = v2.astype(BF16)
        rhs = from_state[:2 * C] + jnp.dot(l_ak.astype(BF16), v2b, preferred_element_type=F32)
        yield
        n = l_ab
        x = eye + n
        span = 2
        while span < C:
            n = _dot3(n, n)
            yield
            x = x + _dot3(n, x)
            yield
            span *= 2
        u2 = _dot3(x, rhs)
        yield
        uv = jnp.concatenate([u2, v2], axis=0).astype(BF16)
        y2 = from_state[2 * C:] + jnp.dot(jnp.concatenate([m_rb, m_rk], axis=1).astype(BF16), uv,
                                          preferred_element_type=F32)
        yield
        y_ref[sl, lanes] = y2[:C] + y2[C:]
        bk_end = jnp.concatenate([stack(b * g_end), stack(k * g_end)], axis=0).astype(BF16)
        s_scr[q] = s_old * jnp.exp(cs[C - 1:C, :]) + lax.dot_general(
            uv, bk_end, (((0,), (0,)), ((), ())), preferred_element_type=F32)

    lax.fori_loop(0, n_chunks, chunk, 0)

    @pl.when(pl.program_id(2) == pl.num_programs(2) - 1)
    def _():
        sT_ref[...] = s_scr[...]


def wkv7_chunked(r, lw, k, v, a, b, s0):
    B, T, W = r.shape
    n_pair = W // WKV_PAIR
    tt = min(WKV_T_TILE, T)
    s0p = s0.astype(F32).reshape(B, n_pair, 2, RWKV_HD, RWKV_HD)
    zero = jnp.zeros_like(s0p[:, :, 0])
    s0_bd = jnp.concatenate([jnp.concatenate([s0p[:, :, 0], zero], axis=-1),
                             jnp.concatenate([zero, s0p[:, :, 1]], axis=-1)], axis=-2)
    pps = WKV_PAIRS_PER_STEP
    seq = pl.BlockSpec((None, tt, pps * WKV_PAIR), lambda i, p, t: (i, t, p))
    st = pl.BlockSpec((None, pps, WKV_PAIR, WKV_PAIR), lambda i, p, t: (i, p, 0, 0))
    y, s_bd = pl.pallas_call(
        _wkv_kernel,
        grid=(B, n_pair // pps, T // tt),
        in_specs=[seq] * 6 + [st],
        out_specs=[seq, st],
        out_shape=[jax.ShapeDtypeStruct((B, T, W), F32),
                   jax.ShapeDtypeStruct((B, n_pair, WKV_PAIR, WKV_PAIR), F32)],
        scratch_shapes=[pltpu.VMEM((pps, WKV_PAIR, WKV_PAIR), F32)],
        compiler_params=pltpu.CompilerParams(
            dimension_semantics=("parallel", "parallel", "arbitrary"), vmem_limit_bytes=VMEM_LIMIT_BYTES),
        name="wkv7_chunked",
    )(r, lw, k, v, a, b, s0_bd)
    s_fin = jnp.stack([s_bd[:, :, :RWKV_HD, :RWKV_HD], s_bd[:, :, RWKV_HD:, RWKV_HD:]], axis=2)
    return y, s_fin.reshape(B, W // RWKV_HD, RWKV_HD, RWKV_HD)


def rms_norm(x, g, eps=1e-6):
    xf = x.astype(jnp.float32)
    y = xf * lax.rsqrt(jnp.mean(xf * xf, axis=-1, keepdims=True) + eps)
    return (y * g.astype(jnp.float32)).astype(x.dtype)


def head_group_norm(y, g, b, eps):
    yf = y.astype(jnp.float32)
    mu = jnp.mean(yf, axis=-1, keepdims=True)
    var = jnp.mean(jnp.square(yf - mu), axis=-1, keepdims=True)
    yn = ((yf - mu) * lax.rsqrt(var + eps)).reshape(y.shape[:-2] + (-1,))
    return (yn * g.astype(jnp.float32) + b.astype(jnp.float32)).astype(y.dtype)


def masked_softmax(s, mask):
    s = jnp.where(mask, s.astype(jnp.float32), -jnp.inf)
    m = jnp.max(s, axis=-1, keepdims=True)
    e = jnp.exp(s - jnp.where(jnp.isfinite(m), m, 0.0))
    den = jnp.sum(e, axis=-1, keepdims=True)
    return e / jnp.where(den > 0, den, 1.0)


def rope(x, pos, n_rot, theta):
    half = n_rot // 2
    inv = jnp.exp(-jnp.log(jnp.float32(theta)) * jnp.arange(half, dtype=jnp.float32) / half)
    ang = pos.astype(jnp.float32)[:, None] * inv[None, :]
    cos = jnp.cos(ang)[None, :, None, :]
    sin = jnp.sin(ang)[None, :, None, :]
    xf = x.astype(jnp.float32)
    x1, x2 = xf[..., :half], xf[..., half:n_rot]
    out = jnp.concatenate([x1 * cos - x2 * sin, x2 * cos + x1 * sin, xf[..., n_rot:]], axis=-1)
    return out.astype(x.dtype)


def linear_scan(a, b, h0):
    b = b.at[:, 0].add(a[:, 0] * h0)

    def combine(left, right):
        return left[0] * right[0], right[0] * left[1] + right[1]

    return lax.associative_scan(combine, (a, b), axis=1)[1]


def wkv7_scan(r, w, k, v, a, b, s0):
    xs = tuple(jnp.moveaxis(z.astype(jnp.float32), 1, 0) for z in (r, w, k, v, a, b))

    def step(S, inp):
        r_t, w_t, k_t, v_t, a_t, b_t = inp
        sa = jnp.einsum('bhij,bhj->bhi', S, a_t)
        S = S * w_t[:, :, None, :] + sa[..., None] * b_t[:, :, None, :] + v_t[..., None] * k_t[:, :, None, :]
        return S, jnp.einsum('bhij,bhj->bhi', S, r_t)

    S, ys = lax.scan(step, s0.astype(jnp.float32), xs)
    return jnp.moveaxis(ys, 0, 1), S


def even_mixer_core(proj, p, lru_h0, lru_conv0, shift0, wkv0):
    B, T, _ = proj.shape
    f32 = jnp.float32
    dt = proj.dtype
    xb, gb, rw = jnp.split(proj, [LRU_W, 2 * LRU_W], axis=-1)
    xcat = jnp.concatenate([lru_conv0.astype(dt), xb], axis=1)
    xc = p['conv_b'] + sum(p['conv_w'][j] * xcat[:, j:j + T] for j in range(CONV_W))
    xbd = xc.reshape(B, T, LRU_BLOCKS, LRU_BS)
    gate_r = jax.nn.sigmoid(jnp.einsum('btnc,ncd->btnd', xbd, p['wa']).reshape(B, T, LRU_W) + p['ba'])
    gate_i = jax.nn.sigmoid(jnp.einsum('btnc,ncd->btnd', xbd, p['wx']).reshape(B, T, LRU_W) + p['bx'])
    log_a = -LRU_C * gate_r.astype(f32) * jax.nn.softplus(-p['lam'].astype(f32))
    u = jnp.sqrt(-jnp.expm1(2.0 * log_a)) * (gate_i * xc).astype(f32)
    hs = lru_scan(jnp.exp(log_a), u, lru_h0.astype(f32))
    y_lru = hs.astype(dt) * jax.nn.gelu(gb)
    prev = jnp.concatenate([shift0.astype(dt)[:, None], rw[:, :-1]], axis=1)
    rs = rw + p['mu'] * (prev - rw)
    r, k, v, xw, xa, xg = jnp.split(
        rs, [RWKV_W, 2 * RWKV_W, 3 * RWKV_W, 3 * RWKV_W + W_LORA, 3 * RWKV_W + W_LORA + A_LORA], axis=-1)
    w_log = -jax.nn.softplus(-(p['w0'] + jnp.tanh(xw) @ p['w2']).astype(f32)) - 0.5
    log_decay = -jnp.exp(w_log)
    decay = jnp.exp(log_decay)
    a_icl = jax.nn.sigmoid(p['a0'] + xa @ p['a2'])
    g = jax.nn.sigmoid(xg) @ p['g2']
    heads = (B, T, RWKV_H, RWKV_HD)
    kk = (k * p['k_k']).reshape(heads).astype(f32)
    kk = kk / jnp.maximum(jnp.sqrt(jnp.sum(kk * kk, axis=-1, keepdims=True)), 1e-12)
    k = k * (1.0 + (a_icl - 1.0) * p['k_a'])
    rh, kh, vh, ah = (z.reshape(heads) for z in (r, k, v, a_icl))
    if T % WKV_C == 0:
        y, wkv = wkv7_chunked(r.astype(f32), log_decay, k.astype(f32), v.astype(f32),
                              (-kk).reshape(B, T, RWKV_W), (kk * ah.astype(f32)).reshape(B, T, RWKV_W), wkv0)
        y = y.reshape(heads)
    else:
        y, wkv = wkv7_scan(rh, decay.reshape(heads), kh, vh, -kk, kk * ah.astype(f32), wkv0)
    y = head_group_norm(y, p['ln_g'], p['ln_b'], 64e-5).astype(dt)
    bonus = (jnp.sum(rh * kh * p['r_k'], axis=-1, keepdims=True) * vh).reshape(B, T, RWKV_W)
    y_rwkv = (y + bonus) * g
    cat = jnp.concatenate([y_lru, y_rwkv], axis=-1)
    return cat, hs[:, -1], xcat[:, T:], rw[:, -1], wkv


def odd_project(proj, p, pos):
    B, T, _ = proj.shape
    sizes = [NSA_W] + [NSA_G * NSA_HD] * 6 + [3 * NSA_H, RET_H * RET_DK, RET_H * RET_DK, RET_W, RET_W]
    q, kc, vc, ks, vs, kw, vw, gt, rq, rk, rv, rg = jnp.split(
        proj, np.cumsum(sizes)[:-1].tolist(), axis=-1)
    kvs = (B, T, NSA_G, NSA_HD)
    q_n = rms_norm(q.reshape(B, T, NSA_H, NSA_HD), p['q_norm'])
    return {
        'q_n': q_n,
        'q_r': rope(q_n, pos, ROPE_DIMS, ROPE_THETA),
        'kc': kc.reshape(kvs), 'vc': vc.reshape(kvs),
        'ks': rope(rms_norm(ks.reshape(kvs), p['k_norm'][1]), pos, ROPE_DIMS, ROPE_THETA),
        'vs': vs.reshape(kvs),
        'kw': rope(rms_norm(kw.reshape(kvs), p['k_norm'][2]), pos, ROPE_DIMS, ROPE_THETA),
        'vw': vw.reshape(kvs),
        'gates': jax.nn.sigmoid(gt).reshape(B, T, NSA_H, 3),
        'rq': rope(rq.reshape(B, T, RET_H, RET_DK), pos, RET_DK, RET_THETA),
        'rk': rope(rk.reshape(B, T, RET_H, RET_DK), pos, RET_DK, RET_THETA) * (RET_DK ** -0.5),
        'rv': rv.reshape(B, T, RET_H, RET_DV),
        'rg': rg,
    }


def to_groups_q(q):
    B, T = q.shape[:2]
    return jnp.moveaxis(q.reshape(B, T, NSA_G, NSA_HPG, NSA_HD), 1, 3)


def to_groups_k(k):
    return jnp.moveaxis(k, 1, 2)


def nsa_compress(x, w1, b1, w2, b2):
    B, L = x.shape[:2]
    n_chunk = L // CMP_STRIDE
    n_cmp = n_chunk - CMP_R + 1
    ch = x[:, :n_chunk * CMP_STRIDE].reshape(B, n_chunk, CMP_STRIDE, NSA_G, NSA_HD)
    ch = jnp.moveaxis(ch, 3, 2).reshape(B, n_chunk, NSA_G, CMP_STRIDE * NSA_HD)
    part = jnp.einsum('bngc,rch->bngrh', ch, w1)
    pre = b1 + sum(part[:, m:m + n_cmp, :, m] for m in range(CMP_R))
    return jax.nn.gelu(pre) @ w2 + b2


def nsa_compressed_branch(qn, kc_raw, vc_raw, p, q_pos):
    kc = to_groups_k(rms_norm(nsa_compress(kc_raw, *p['ck']), p['k_norm'][0]))
    vc = to_groups_k(nsa_compress(vc_raw, *p['cv']))
    s = jnp.einsum('bghqd,bgcd->bghqc', qn, kc) * NSA_HD ** -0.5
    ends = jnp.arange(kc.shape[2]) * CMP_STRIDE + CMP_BLOCK - 1
    prob = masked_softmax(s, ends[None, :] <= q_pos[:, None])
    return jnp.einsum('bghqc,bgcd->bghqd', prob.astype(vc.dtype), vc), prob


def cmp_sel_overlap(n_cmp, n_sel):
    cs = np.arange(n_cmp) * CMP_STRIDE
    ss = np.arange(n_sel) * SEL_BLOCK
    ov = np.minimum(cs[None] + CMP_BLOCK, ss[:, None] + SEL_BLOCK) - np.maximum(cs[None], ss[:, None])
    return jnp.asarray(np.clip(ov, 0, None) / CMP_BLOCK, dtype=jnp.float32)


def nsa_select(p_cmp, q_pos, n_sel):
    imp = jnp.einsum('bgqc,sc->bgqs', p_cmp.sum(axis=2), cmp_sel_overlap(p_cmp.shape[-1], n_sel))
    j = jnp.arange(n_sel)[None, :]
    qb = (q_pos // SEL_BLOCK)[:, None]
    valid = j <= qb
    forced = (j == 0) | (j == qb) | (j == qb - 1)
    score = jnp.where(valid, jnp.where(forced, FORCE_SCORE, imp), -jnp.inf)
    _, idx = lax.top_k(score, min(SEL_TOP, n_sel))
    sel_ok = jnp.take_along_axis(jnp.broadcast_to(valid, score.shape), idx, axis=-1)
    return idx, sel_ok


def sel_blocks(x, n_sel):
    B, L = x.shape[:2]
    x = jnp.pad(x, ((0, 0), (0, n_sel * SEL_BLOCK - L), (0, 0), (0, 0)))
    return jnp.moveaxis(x.reshape(B, n_sel, SEL_BLOCK, NSA_G, NSA_HD), 3, 1)


def nsa_slc_attend(q, kb, vb, idx, sel_ok, q_pos):
    B, G = kb.shape[:2]
    bi = jnp.arange(B)[:, None, None, None]
    gi = jnp.arange(G)[None, :, None, None]
    kg = kb[bi, gi, idx]
    vg = vb[bi, gi, idx]
    s = jnp.einsum('bghqd,bgqnld->bghqnl', q, kg) * NSA_HD ** -0.5
    kpos = idx[..., None] * SEL_BLOCK + jnp.arange(SEL_BLOCK)
    mask = (kpos <= q_pos[None, None, :, None, None]) & sel_ok[..., None]
    sh = s.shape
    prob = masked_softmax(s.reshape(sh[:4] + (-1,)), mask.reshape(B, G, 1, sh[3], -1))
    return jnp.einsum('bghqnl,bgqnld->bghqd', prob.reshape(sh).astype(vg.dtype), vg)


def window_attend_banded(q, k, v):
    B, G, HPG, T, HD = q.shape
    nb = T // WIN_BLOCK
    npv = WINDOW // WIN_BLOCK
    pad = ((0, 0), (0, 0), (npv * WIN_BLOCK, 0), (0, 0))

    def band(z):
        zb = jnp.pad(z, pad).reshape(B, G, nb + npv, WIN_BLOCK, HD)
        return jnp.concatenate([zb[:, :, j:j + nb] for j in range(npv + 1)], axis=3)

    kb, vb = band(k), band(v)
    qb = q.reshape(B, G, HPG, nb, WIN_BLOCK, HD)
    s = jnp.einsum('bghiqd,bgikd->bghiqk', qb, kb) * NSA_HD ** -0.5
    blk = jnp.arange(nb)[:, None]
    q_pos = blk * WIN_BLOCK + jnp.arange(WIN_BLOCK)[None]
    k_pos = (blk - npv) * WIN_BLOCK + jnp.arange((npv + 1) * WIN_BLOCK)[None]
    diff = q_pos[:, :, None] - k_pos[:, None, :]
    mask = (diff >= 0) & (diff < WINDOW) & (k_pos[:, None, :] >= 0)
    prob = masked_softmax(s, mask)
    return jnp.einsum('bghiqk,bgikd->bghiqd', prob.astype(v.dtype), vb).reshape(B, G, HPG, T, HD)


def window_attend_cached(q, k, v, q_pos, k_pos):
    s = jnp.einsum('bghqd,blgd->bghql', q, k) * NSA_HD ** -0.5
    diff = q_pos[:, None] - k_pos[None, :]
    prob = masked_softmax(s, (diff >= 0) & (diff < WINDOW))
    return jnp.einsum('bghql,blgd->bghqd', prob.astype(v.dtype), v)


def retention_chunk(S, q, k, v):
    f32 = jnp.float32
    C = q.shape[1]
    lg = jnp.log1p(-jnp.exp2(-5.0 - jnp.arange(RET_H, dtype=f32)))
    i = jnp.arange(C, dtype=f32)
    diff = i[:, None] - i[None, :]
    causal = diff >= 0
    dmask = jnp.where(causal, jnp.exp(jnp.where(causal, diff, 0.0)[None] * lg[:, None, None]), 0.0)
    qf, kf, vf = q.astype(f32), k.astype(f32), v.astype(f32)
    s = jnp.einsum('bihd,bjhd->bhij', qf, kf) * dmask
    o = jnp.einsum('bhij,bjhe->bihe', s, vf)
    o = o + jnp.einsum('bihd,bhde->bihe', qf, S) * jnp.exp((i[:, None] + 1.0) * lg[None, :])[None, :, :, None]
    k_dec = kf * jnp.exp((C - 1.0 - i)[:, None] * lg[None, :])[None, :, :, None]
    S = S * jnp.exp(C * lg)[None, :, None, None] + jnp.einsum('bjhd,bjhe->bhde', k_dec, vf)
    return S, o


def retention_prompt(q, k, v):
    B, T = q.shape[:2]
    n = T // RET_CHUNK
    xs = tuple(jnp.moveaxis(z.reshape((B, n, RET_CHUNK) + z.shape[2:]), 1, 0) for z in (q, k, v))
    s0 = jnp.zeros((B, RET_H, RET_DK, RET_DV), jnp.float32)
    S, o = lax.scan(lambda S, c: retention_chunk(S, c[0], c[1], c[2]), s0, xs)
    return S, jnp.moveaxis(o, 0, 1).reshape(B, T, RET_H, RET_DV)


def odd_output(o_cmp, o_slc, o_win, o_ret, pr, p):
    gates = pr['gates']
    B, T = gates.shape[:2]
    gg = jnp.moveaxis(gates.reshape(B, T, NSA_G, NSA_HPG, 3), 1, 3)[..., None]
    o = gg[..., 0, :] * o_cmp + gg[..., 1, :] * o_slc + gg[..., 2, :] * o_win
    o_nsa = jnp.moveaxis(o, 3, 1).reshape(B, T, NSA_W)
    y_ret = head_group_norm(o_ret, p['gn_g'], p['gn_b'], 1e-5).astype(o_nsa.dtype) * jax.nn.silu(pr['rg'])
    return jnp.concatenate([o_nsa, y_ret], axis=-1)


def odd_mixer_prompt(proj, p):
    B, T, _ = proj.shape
    pos = jnp.arange(T)
    pr = odd_project(proj, p, pos)
    qn = pr['q_n'].reshape(B, T, NSA_W)
    qr = pr['q_r'].reshape(B, T, NSA_W)
    kc = rms_norm(nsa_compress(pr['kc'], *p['ck']), p['k_norm'][0])
    vc = nsa_compress(pr['vc'], *p['cv'])
    n_cmp = kc.shape[1]
    n_sel = -(-T // SEL_BLOCK)
    o_cmp, sel = nsa_cmp_select(qn, _tile_cmp(kc), _tile_cmp(vc), _overlap_T(n_cmp, n_sel),
                                n_cmp=n_cmp, n_sel=n_sel, q_pos0=0)
    o_slc = nsa_flash(qr, _tile_groups(pr['ks']), _tile_groups(pr['vs']), sel, _sel_expand(T))
    o_win = nsa_flash(qr, _tile_groups(pr['kw']), _tile_groups(pr['vw']))
    S, o_ret = retention_prompt(pr['rq'], pr['rk'], pr['rv'])
    gates = pr['gates']
    heads = (B, T, NSA_H, NSA_HD)
    o_nsa = (gates[..., 0:1] * o_cmp.reshape(heads) + gates[..., 1:2] * o_slc.reshape(heads)
             + gates[..., 2:3] * o_win.reshape(heads)).reshape(B, T, NSA_W)
    y_ret = head_group_norm(o_ret, p['gn_g'], p['gn_b'], 1e-5).astype(o_nsa.dtype) * jax.nn.silu(pr['rg'])
    out = jnp.concatenate([o_nsa, y_ret], axis=-1)
    kv_rows = jnp.stack([pr['kc'], pr['vc'], pr['ks'], pr['vs']], axis=2)
    win = jnp.stack([pr['kw'], pr['vw']], axis=2)[:, T - min(WINDOW, T):]
    return out, kv_rows, win, S


def odd_mixer_sample(proj, p, cache_layer, page_table, win_buf, ret_s0):
    B, T, _ = proj.shape
    assert T == DEC_SEQ == 1 and win_buf.shape[1] == WIN_BUF
    pos = PAST_LEN + jnp.arange(T)
    pr = odd_project(proj, p, pos)
    scale = NSA_HD ** -0.5
    new_rows = jnp.stack([pr['ks'], pr['vs'], pr['kw'], pr['vw']], axis=2)[:, 0].reshape(B, 4, SLOT_ROWS)
    cache_t = jnp.transpose(cache_layer, (0, 2, 3, 4, 1)).reshape(cache_layer.shape[0], KV_ROWS, PAGE_SIZE)
    win_t = jnp.transpose(win_buf, (0, 2, 3, 4, 1)).reshape(B, 2 * SLOT_ROWS, WIN_BUF)
    wk = _dec_cmp_weights(*p['ck'])
    wv = _dec_cmp_weights(*p['cv'])
    w1t, b1, w2t, b2t = (jnp.stack([a, b]) for a, b in zip(wk, wv))
    kn = jnp.tile(p['k_norm'][0], NSA_G).reshape(1, SLOT_ROWS)
    t = np.arange(PAST_LEN)
    expand = jnp.asarray(np.arange(CMP_PAD)[:, None] == (t // SEL_BLOCK)[None, :], dtype=BF16)
    h = np.arange(NSA_H)
    grp = jnp.asarray((h[:, None] // NSA_HPG) == (h[None, :] // NSA_HPG), dtype=BF16)
    o16 = dec_nsa(page_table, cache_t, win_t, _place_heads(pr['q_n'][:, 0] * scale),
                  _place_heads(pr['q_r'][:, 0] * scale), new_rows, pr['gates'][:, 0],
                  w1t, b1, w2t, b2t, kn, _overlap_T(DEC_N_CMP, DEC_N_SEL), expand, grp)
    o_nsa = _take_heads(o16)[:, None, :]
    S, o_ret = retention_chunk(ret_s0.astype(jnp.float32), pr['rq'], pr['rk'], pr['rv'])
    y_ret = head_group_norm(o_ret, p['gn_g'], p['gn_b'], 1e-5).astype(o_nsa.dtype) * jax.nn.silu(pr['rg'])
    out = jnp.concatenate([o_nsa, y_ret], axis=-1)
    rows = jnp.stack([pr['kc'], pr['vc'], pr['ks'], pr['vs']], axis=2).astype(cache_layer.dtype)
    new_col = jnp.stack([pr['kw'], pr['vw']], axis=2)[:, 0].reshape(B, 2 * SLOT_ROWS, 1).astype(win_buf.dtype)
    win_new = jnp.concatenate([win_t[:, :, T:], new_col], axis=2).reshape(B, 2, NSA_G, NSA_HD, WIN_BUF)
    return out, rows, jnp.transpose(win_new, (0, 4, 1, 2, 3)), S


def _stack(xs, dt):
    return jnp.stack(xs).astype(dt)


def kernel(x_prompt, x_sample, state_lru_h, state_lru_conv, state_rwkv_shift, state_rwkv_wkv,
           cache_nsa_kv, cache_nsa_win, state_ret, page_table,
           norm_ffn1, ffn1_w_in, ffn1_w_out, norm_mix, norm_ffn2, ffn2_w_in, ffn2_w_out,
           ab_w_in, lru_conv_w, lru_conv_b, lru_wa, lru_ba, lru_wx, lru_bx, lru_lambda,
           rwkv_mu, rwkv_w0, rwkv_w2, rwkv_a0, rwkv_a2, rwkv_g2, rwkv_k_k, rwkv_k_a, rwkv_r_k,
           rwkv_ln_g, rwkv_ln_b, ab_w_out,
           cd_w_in, nsa_q_norm, nsa_k_norm, cmp_k_w1, cmp_k_b1, cmp_k_w2, cmp_k_b2,
           cmp_v_w1, cmp_v_b1, cmp_v_w2, cmp_v_b2, ret_gn_g, ret_gn_b, cd_w_out):
    dt = x_prompt.dtype
    B = x_prompt.shape[0]
    DB = x_sample.shape[0]
    y = jnp.concatenate([x_prompt.reshape(N_PROMPT, D_MODEL), x_sample.reshape(DB * DEC_SEQ, D_MODEL)], axis=0)
    lru_h_p, lru_h_s, lru_c_p, lru_c_s, sh_p, sh_s, wkv_p, wkv_s = [], [], [], [], [], [], [], []
    kv_p, kv_s, win_p, win_s, ret_p, ret_s = [], [], [], [], [], []
    for layer in range(DEPTH):
        li = layer // 2
        y = ffn_block(y, norm_ffn1[layer], *_prep_ffn_weights(ffn1_w_in[layer], ffn1_w_out[layer]))
        if layer % 2 == 0:
            p = {'conv_w': lru_conv_w[li], 'conv_b': lru_conv_b[li],
                 'wa': lru_wa[li], 'ba': lru_ba[li], 'wx': lru_wx[li], 'bx': lru_bx[li], 'lam': lru_lambda[li],
                 'mu': rwkv_mu[li], 'w0': rwkv_w0[li], 'w2': rwkv_w2[li], 'a0': rwkv_a0[li], 'a2': rwkv_a2[li],
                 'g2': rwkv_g2[li], 'k_k': rwkv_k_k[li], 'k_a': rwkv_k_a[li], 'r_k': rwkv_r_k[li],
                 'ln_g': rwkv_ln_g[li], 'ln_b': rwkv_ln_b[li]}
            proj = norm_matmul(y, norm_mix[layer], _prep_cols(ab_w_in[li]))[:, :AB_COLS]
            proj_p = proj[:N_PROMPT].reshape(B, SEQ, AB_COLS)
            proj_s = proj[N_PROMPT:].reshape(DB, DEC_SEQ, AB_COLS)
            cp, a0, a1, a2, a3 = even_mixer_core(
                proj_p, p, jnp.zeros((B, LRU_W), dt), jnp.zeros((B, CONV_W - 1, LRU_W), dt),
                jnp.zeros((B, SHIFT_W), dt), jnp.zeros((B, RWKV_H, RWKV_HD, RWKV_HD), dt))
            cs, b0, b1, b2, b3 = even_mixer_core(
                proj_s, p, state_lru_h[li], state_lru_conv[li], state_rwkv_shift[li], state_rwkv_wkv[li])
            lru_h_p.append(a0); lru_c_p.append(a1); sh_p.append(a2); wkv_p.append(a3)
            lru_h_s.append(b0); lru_c_s.append(b1); sh_s.append(b2); wkv_s.append(b3)
            w_out = ab_w_out[li]
        else:
            p = {'q_norm': nsa_q_norm[li], 'k_norm': nsa_k_norm[li],
                 'ck': (cmp_k_w1[li], cmp_k_b1[li], cmp_k_w2[li], cmp_k_b2[li]),
                 'cv': (cmp_v_w1[li], cmp_v_b1[li], cmp_v_w2[li], cmp_v_b2[li]),
                 'gn_g': ret_gn_g[li], 'gn_b': ret_gn_b[li]}
            proj = norm_matmul(y, norm_mix[layer], _prep_cols(cd_w_in[li]))[:, :CD_COLS]
            proj_p = proj[:N_PROMPT].reshape(B, SEQ, CD_COLS)
            proj_s = proj[N_PROMPT:].reshape(DB, DEC_SEQ, CD_COLS)
            cp, a0, a1, a2 = odd_mixer_prompt(proj_p, p)
            cs, b0, b1, b2 = odd_mixer_sample(proj_s, p, cache_nsa_kv[li], page_table, cache_nsa_win[li],
                                              state_ret[li])
            kv_p.append(a0); win_p.append(a1); ret_p.append(a2)
            kv_s.append(b0); win_s.append(b1); ret_s.append(b2)
            w_out = cd_w_out[li]
        cat = jnp.concatenate([cp.reshape(N_PROMPT, D_MODEL), cs.reshape(DB * DEC_SEQ, D_MODEL)], axis=0)
        y = matmul_residual(cat, w_out.astype(BF16), y)
        y = ffn_block(y, norm_ffn2[layer], *_prep_ffn_weights(ffn2_w_in[layer], ffn2_w_out[layer]))
    yp = y[:N_PROMPT].reshape(B, SEQ, D_MODEL)
    ys = y[N_PROMPT:].reshape(DB, DEC_SEQ, D_MODEL)
    return (yp, ys,
            _stack(lru_h_p, dt), _stack(lru_h_s, dt), _stack(lru_c_p, dt), _stack(lru_c_s, dt),
            _stack(sh_p, dt), _stack(sh_s, dt), _stack(wkv_p, dt), _stack(wkv_s, dt),
            _stack(kv_p, dt), _stack(kv_s, dt), _stack(win_p, dt), _stack(win_s, dt),
            _stack(ret_p, dt), _stack(ret_s, dt))
```

```python
import functools

import jax
import jax.numpy as jnp
import numpy as np
from jax import lax
from jax.experimental import pallas as pl
from jax.experimental.pallas import tpu as pltpu

D_MODEL = 2048
BATCH = 4
SEQ = 2048
DEPTH = 2
DEC_BATCH = 128
DEC_SEQ = 1
PAST_LEN = 2048
PAGE_SIZE = 128
D_FF = 5504
LRU_W = D_MODEL // 2
LRU_BLOCKS = 16
LRU_BS = LRU_W // LRU_BLOCKS
CONV_W = 4
LRU_C = 8.0
RWKV_W = D_MODEL // 2
RWKV_HD = 64
RWKV_H = RWKV_W // RWKV_HD
W_LORA = 64
A_LORA = 64
G_LORA = 160
SHIFT_W = 3 * RWKV_W + W_LORA + A_LORA + G_LORA
AB_COLS = 2 * LRU_W + SHIFT_W
NSA_H = 16
NSA_G = 4
NSA_HPG = NSA_H // NSA_G
NSA_HD = 64
NSA_W = NSA_H * NSA_HD
ROPE_DIMS = NSA_HD // 4
ROPE_THETA = 500000.0
CMP_BLOCK = 32
CMP_STRIDE = 16
CMP_R = CMP_BLOCK // CMP_STRIDE
CMP_HID = 256
SEL_BLOCK = 64
SEL_TOP = 16
SEL_Q_BLOCK = 64
WINDOW = 512
WIN_BLOCK = 128
FORCE_SCORE = 1e4
KV_SLOTS = 4
RET_H = 8
RET_DK = 64
RET_DV = 128
RET_W = RET_H * RET_DV
RET_CHUNK = 128
RET_THETA = 10000.0
CD_COLS = NSA_W + 6 * NSA_G * NSA_HD + 3 * NSA_H + 2 * RET_H * RET_DK + 2 * RET_W

N_TOK = BATCH * SEQ + DEC_BATCH * DEC_SEQ
N_PROMPT = BATCH * SEQ

LANE = 128
VMEM_LIMIT_BYTES = 56 * 1024 * 1024
ROW_TILE = 640
FF_TILE = 512
D_FF_PAD = 5632
COL_TILE = 512

BF16 = jnp.bfloat16
F32 = jnp.float32


def _round_up(n, m):
    return -(-n // m) * m


def _rms_rows(x, g):
    ms = jnp.mean(x * x, axis=-1, keepdims=True)
    return x * lax.rsqrt(ms + 1e-6) * g


def _ffn_kernel(x_ref, g_ref, wg_ref, wu_ref, wo_ref, o_ref, xn_ref, acc_ref):
    k = pl.program_id(1)

    @pl.when(k == 0)
    def _():
        xn_ref[...] = _rms_rows(x_ref[...], g_ref[...]).astype(BF16)
        acc_ref[...] = jnp.zeros_like(acc_ref)

    xn = xn_ref[...]
    gate = jnp.dot(xn, wg_ref[...], preferred_element_type=F32)
    up = jnp.dot(xn, wu_ref[...], preferred_element_type=F32)
    act = gate * jax.nn.sigmoid(gate) * up
    acc_ref[...] += jnp.dot(act.astype(BF16), wo_ref[...], preferred_element_type=F32)

    @pl.when(k == pl.num_programs(1) - 1)
    def _():
        o_ref[...] = x_ref[...] + 0.5 * acc_ref[...]


def ffn_block(x, g, wg, wu, wo):
    m, d = x.shape
    return pl.pallas_call(
        _ffn_kernel,
        grid=(m // ROW_TILE, D_FF_PAD // FF_TILE),
        in_specs=[
            pl.BlockSpec((ROW_TILE, d), lambda i, k: (i, 0)),
            pl.BlockSpec((1, d), lambda i, k: (0, 0)),
            pl.BlockSpec((d, FF_TILE), lambda i, k: (0, k)),
            pl.BlockSpec((d, FF_TILE), lambda i, k: (0, k)),
            pl.BlockSpec((FF_TILE, d), lambda i, k: (k, 0)),
        ],
        out_specs=pl.BlockSpec((ROW_TILE, d), lambda i, k: (i, 0)),
        out_shape=jax.ShapeDtypeStruct((m, d), F32),
        scratch_shapes=[pltpu.VMEM((ROW_TILE, d), BF16), pltpu.VMEM((ROW_TILE, d), F32)],
        compiler_params=pltpu.CompilerParams(
            dimension_semantics=("parallel", "arbitrary"), vmem_limit_bytes=VMEM_LIMIT_BYTES),
        name="ffn_block",
    )(x, g.reshape(1, d), wg, wu, wo)


def _norm_matmul_kernel(x_ref, g_ref, w_ref, o_ref, xn_ref):
    @pl.when(pl.program_id(1) == 0)
    def _():
        xn_ref[...] = _rms_rows(x_ref[...], g_ref[...]).astype(BF16)

    o_ref[...] = jnp.dot(xn_ref[...], w_ref[...], preferred_element_type=F32)


def norm_matmul(x, g, w):
    m, k = x.shape
    n = w.shape[1]
    return pl.pallas_call(
        _norm_matmul_kernel,
        grid=(m // ROW_TILE, n // COL_TILE),
        in_specs=[
            pl.BlockSpec((ROW_TILE, k), lambda i, j: (i, 0)),
            pl.BlockSpec((1, k), lambda i, j: (0, 0)),
            pl.BlockSpec((k, COL_TILE), lambda i, j: (0, j)),
        ],
        out_specs=pl.BlockSpec((ROW_TILE, COL_TILE), lambda i, j: (i, j)),
        out_shape=jax.ShapeDtypeStruct((m, n), F32),
        scratch_shapes=[pltpu.VMEM((ROW_TILE, k), BF16)],
        compiler_params=pltpu.CompilerParams(
            dimension_semantics=("parallel", "arbitrary"), vmem_limit_bytes=VMEM_LIMIT_BYTES),
        name="norm_matmul",
    )(x, g.reshape(1, k), w)


def _matmul_residual_kernel(a_ref, w_ref, r_ref, o_ref):
    o_ref[...] = r_ref[...] + jnp.dot(a_ref[...].astype(BF16), w_ref[...], preferred_element_type=F32)


def matmul_residual(a, w, res):
    m, k = a.shape
    n = w.shape[1]
    return pl.pallas_call(
        _matmul_residual_kernel,
        grid=(m // ROW_TILE, n // COL_TILE),
        in_specs=[
            pl.BlockSpec((ROW_TILE, k), lambda i, j: (i, 0)),
            pl.BlockSpec((k, COL_TILE), lambda i, j: (0, j)),
            pl.BlockSpec((ROW_TILE, COL_TILE), lambda i, j: (i, j)),
        ],
        out_specs=pl.BlockSpec((ROW_TILE, COL_TILE), lambda i, j: (i, j)),
        out_shape=jax.ShapeDtypeStruct((m, n), F32),
        compiler_params=pltpu.CompilerParams(
            dimension_semantics=("parallel", "arbitrary"), vmem_limit_bytes=VMEM_LIMIT_BYTES),
        name="matmul_residual",
    )(a, w, res)


def _prep_ffn_weights(w_in, w_out):
    pad = D_FF_PAD - D_FF
    wg = jnp.pad(w_in[:, :D_FF], ((0, 0), (0, pad))).astype(BF16)
    wu = jnp.pad(w_in[:, D_FF:], ((0, 0), (0, pad))).astype(BF16)
    wo = jnp.pad(w_out, ((0, pad), (0, 0))).astype(BF16)
    return wg, wu, wo


def _prep_cols(w):
    n = w.shape[1]
    return jnp.pad(w, ((0, 0), (0, _round_up(n, COL_TILE) - n))).astype(BF16)


SCAN_TILE = 256


def _lru_scan_kernel(a_ref, b_ref, h0_ref, o_ref, carry_ref):
    @pl.when(pl.program_id(1) == 0)
    def _():
        carry_ref[...] = h0_ref[...]

    a = a_ref[...]
    b = b_ref[...]
    rows = lax.broadcasted_iota(jnp.int32, a.shape, 0)
    k = 1
    while k < a.shape[0]:
        keep = rows >= k
        b = jnp.where(keep, a * pltpu.roll(b, k, 0) + b, b)
        a = jnp.where(keep, a * pltpu.roll(a, k, 0), a)
        k *= 2
    h = a * carry_ref[...] + b
    o_ref[...] = h
    carry_ref[...] = h[a.shape[0] - 1:, :]


def lru_scan(a, b, h0):
    B, T, W = a.shape
    tt = min(SCAN_TILE, T)
    return pl.pallas_call(
        _lru_scan_kernel,
        grid=(B, T // tt),
        in_specs=[
            pl.BlockSpec((None, tt, W), lambda i, t: (i, t, 0)),
            pl.BlockSpec((None, tt, W), lambda i, t: (i, t, 0)),
            pl.BlockSpec((None, 1, W), lambda i, t: (i, 0, 0)),
        ],
        out_specs=pl.BlockSpec((None, tt, W), lambda i, t: (i, t, 0)),
        out_shape=jax.ShapeDtypeStruct((B, T, W), F32),
        scratch_shapes=[pltpu.VMEM((1, W), F32)],
        compiler_params=pltpu.CompilerParams(
            dimension_semantics=("parallel", "arbitrary"), vmem_limit_bytes=VMEM_LIMIT_BYTES),
        name="lru_scan",
    )(a, b, h0.reshape(B, 1, W))


GROUP_W = NSA_HPG * NSA_HD
ATT_Q_TILE = 128
ATT_K_TILE = 256
CMP_PAD = 128
NEG_BIG = -1e30


def _stack_heads(q):
    head = lax.broadcasted_iota(jnp.int32, q.shape, 1) // NSA_HD
    return jnp.concatenate([jnp.where(head == h, q, 0.0) for h in range(NSA_HPG)], axis=0)


def _unstack_heads(o, tq):
    head = lax.broadcasted_iota(jnp.int32, (tq, GROUP_W), 1) // NSA_HD
    out = jnp.zeros((tq, GROUP_W), F32)
    for h in range(NSA_HPG):
        out = out + jnp.where(head == h, o[h * tq:(h + 1) * tq], 0.0)
    return out


def _cmp_select_kernel(q_ref, k_ref, v_ref, ov_ref, o_ref, sel_ref, *, n_cmp, n_sel, q_pos0):
    tq = q_ref.shape[0]
    i = pl.program_id(2)
    qs = _stack_heads(q_ref[...] * (NSA_HD ** -0.5)).astype(BF16)
    s = lax.dot_general(qs, k_ref[...], (((1,), (1,)), ((), ())), preferred_element_type=F32)
    q_pos = q_pos0 + i * tq + lax.broadcasted_iota(jnp.int32, (tq, CMP_PAD), 0)
    c = lax.broadcasted_iota(jnp.int32, (tq, CMP_PAD), 1)
    mask1 = (c < n_cmp) & (c * CMP_STRIDE + (CMP_BLOCK - 1) <= q_pos)
    mask = jnp.concatenate([mask1] * NSA_HPG, axis=0)
    s = jnp.where(mask, s, NEG_BIG)
    m = jnp.max(s, axis=-1, keepdims=True)
    e = jnp.where(mask, jnp.exp(s - m), 0.0)
    den = jnp.sum(e, axis=-1, keepdims=True)
    prob = e / jnp.where(den > 0, den, 1.0)
    o = jnp.dot(prob.astype(BF16), v_ref[...], preferred_element_type=F32)
    o_ref[...] = _unstack_heads(o, tq)
    psum = prob[0:tq]
    for h in range(1, NSA_HPG):
        psum = psum + prob[h * tq:(h + 1) * tq]
    imp = jnp.dot(psum.astype(BF16), ov_ref[...], preferred_element_type=F32)
    qb = q_pos // SEL_BLOCK
    valid = (c <= qb) & (c < n_sel)
    forced = (c == 0) | (c == qb) | (c == qb - 1)
    score = jnp.where(valid, jnp.where(forced, FORCE_SCORE, imp), -jnp.inf)
    rank = jnp.zeros((tq, CMP_PAD), F32)
    for jp in range(n_sel):
        col = score[:, jp:jp + 1]
        beats = (col > score) | ((col == score) & (c > jp))
        rank = rank + jnp.where(beats, 1.0, 0.0)
    sel_ref[...] = jnp.where((rank < min(SEL_TOP, n_sel)) & (c < n_sel), 1.0, 0.0)


def nsa_cmp_select(qn, kc4, vc4, ovT, *, n_cmp, n_sel, q_pos0):
    B, T, _ = qn.shape
    tq = min(ATT_Q_TILE, T)
    return pl.pallas_call(
        functools.partial(_cmp_select_kernel, n_cmp=n_cmp, n_sel=n_sel, q_pos0=q_pos0),
        grid=(B, NSA_G, T // tq),
        in_specs=[
            pl.BlockSpec((None, tq, GROUP_W), lambda b, g, i: (b, i, g)),
            pl.BlockSpec((None, None, CMP_PAD, GROUP_W), lambda b, g, i: (b, g, 0, 0)),
            pl.BlockSpec((None, None, CMP_PAD, GROUP_W), lambda b, g, i: (b, g, 0, 0)),
            pl.BlockSpec((CMP_PAD, CMP_PAD), lambda b, g, i: (0, 0)),
        ],
        out_specs=[
            pl.BlockSpec((None, tq, GROUP_W), lambda b, g, i: (b, i, g)),
            pl.BlockSpec((None, None, tq, CMP_PAD), lambda b, g, i: (b, g, i, 0)),
        ],
        out_shape=[jax.ShapeDtypeStruct((B, T, NSA_W), F32),
                   jax.ShapeDtypeStruct((B, NSA_G, T, CMP_PAD), F32)],
        compiler_params=pltpu.CompilerParams(
            dimension_semantics=("parallel", "parallel", "parallel"), vmem_limit_bytes=VMEM_LIMIT_BYTES),
        name="nsa_cmp_select",
    )(qn, kc4, vc4, ovT)


def _flash_kernel(*refs, selected):
    if selected:
        q_ref, k_ref, v_ref, sel_ref, exp_ref, o_ref, m_ref, l_ref, acc_ref, s_a, s_b = refs
    else:
        q_ref, k_ref, v_ref, o_ref, m_ref, l_ref, acc_ref, s_a, s_b = refs
    tq = q_ref.shape[0]
    tk = ATT_K_TILE
    n_tiles = k_ref.shape[0] // tk
    i = pl.program_id(2)
    q = q_ref[...] * (NSA_HD ** -0.5)
    head = lax.broadcasted_iota(jnp.int32, q.shape, 1) // NSA_HD
    q4 = _stack_heads(q).astype(BF16)
    m_ref[...] = jnp.full(m_ref.shape, NEG_BIG, F32)
    l_ref[...] = jnp.zeros(l_ref.shape, F32)
    acc_ref[...] = jnp.zeros(acc_ref.shape, F32)
    q_pos = i * tq + lax.broadcasted_iota(jnp.int32, (tq, tk), 0)
    col = lax.broadcasted_iota(jnp.int32, (tq, tk), 1)
    if selected:
        sel = sel_ref[...].astype(BF16)
        lo = 0
    else:
        lo = jnp.maximum(i * tq - (WINDOW - 1), 0) // tk
    hi = (i * tq + tq - 1) // tk + 1

    def tile_start(j):
        return pl.multiple_of(jnp.minimum(j, n_tiles - 1) * tk, tk)

    def scores(j, s_ref):
        s_ref[...] = lax.dot_general(q4, k_ref[pl.ds(tile_start(j), tk), :], (((1,), (1,)), ((), ())),
                                     preferred_element_type=F32)

    def consume(j, s_ref):
        v = v_ref[pl.ds(tile_start(j), tk), :]
        k_pos = j * tk + col
        mask = k_pos <= q_pos
        if selected:
            mask = mask & (jnp.dot(sel, exp_ref[jnp.minimum(j, n_tiles - 1)], preferred_element_type=F32) > 0.5)
        else:
            mask = mask & (q_pos - k_pos < WINDOW)
        for h in range(NSA_HPG):
            s = jnp.where(mask, s_ref[h * tq:(h + 1) * tq, :], NEG_BIG)
            m_old = m_ref[h]
            m_new = jnp.maximum(m_old, jnp.max(s, axis=-1, keepdims=True))
            alpha = jnp.exp(m_old - m_new)
            p = jnp.where(mask, jnp.exp(s - pltpu.repeat(m_new, tk // LANE, axis=1)), 0.0)
            l_ref[h] = alpha * l_ref[h] + jnp.sum(p, axis=-1, keepdims=True)
            acc_ref[h] = (pltpu.repeat(alpha, GROUP_W // LANE, axis=1) * acc_ref[h]
                          + jnp.dot(p.astype(BF16), v, preferred_element_type=F32))
            m_ref[h] = m_new

    scores(lo, s_a)

    def body(t, carry):
        j = lo + 2 * t
        scores(j + 1, s_b)
        consume(j, s_a)
        scores(j + 2, s_a)
        consume(j + 1, s_b)
        return carry

    lax.fori_loop(0, (hi - lo + 1) // 2, body, 0)
    out = jnp.zeros((tq, GROUP_W), F32)
    for h in range(NSA_HPG):
        den = pltpu.repeat(l_ref[h], GROUP_W // LANE, axis=1)
        out = out + jnp.where(head == h, acc_ref[h] / jnp.where(den > 0, den, 1.0), 0.0)
    o_ref[...] = out


def nsa_flash(qr, k4, v4, sel=None, expand=None):
    B, T, _ = qr.shape
    tq = ATT_Q_TILE
    selected = sel is not None
    in_specs = [
        pl.BlockSpec((None, tq, GROUP_W), lambda b, g, i: (b, i, g)),
        pl.BlockSpec((None, T, GROUP_W), lambda b, g, i: (b, 0, g)),
        pl.BlockSpec((None, T, GROUP_W), lambda b, g, i: (b, 0, g)),
    ]
    args = [qr, k4, v4]
    if selected:
        in_specs += [
            pl.BlockSpec((None, None, tq, CMP_PAD), lambda b, g, i: (b, g, i, 0)),
            pl.BlockSpec(expand.shape, lambda b, g, i: (0, 0, 0)),
        ]
        args += [sel, expand]
    return pl.pallas_call(
        functools.partial(_flash_kernel, selected=selected),
        grid=(B, NSA_G, T // tq),
        in_specs=in_specs,
        out_specs=pl.BlockSpec((None, tq, GROUP_W), lambda b, g, i: (b, i, g)),
        out_shape=jax.ShapeDtypeStruct((B, T, NSA_W), F32),
        scratch_shapes=[pltpu.VMEM((NSA_HPG, tq, LANE), F32), pltpu.VMEM((NSA_HPG, tq, LANE), F32),
                        pltpu.VMEM((NSA_HPG, tq, GROUP_W), F32),
                        pltpu.VMEM((NSA_HPG * tq, ATT_K_TILE), F32), pltpu.VMEM((NSA_HPG * tq, ATT_K_TILE), F32)],
        compiler_params=pltpu.CompilerParams(
            dimension_semantics=("parallel", "parallel", "parallel"), vmem_limit_bytes=VMEM_LIMIT_BYTES),
        name="nsa_flash_sel" if selected else "nsa_flash_win",
    )(*args)


def _tile_groups(x):
    B, T = x.shape[:2]
    return jnp.broadcast_to(x[:, :, :, None, :], (B, T, NSA_G, NSA_HPG, NSA_HD)).reshape(B, T, NSA_W).astype(BF16)


def _tile_cmp(x):
    B, n = x.shape[:2]
    x = jnp.pad(jnp.moveaxis(x, 1, 2), ((0, 0), (0, 0), (0, CMP_PAD - n), (0, 0)))
    return jnp.tile(x, (1, 1, 1, NSA_HPG)).astype(BF16)


def _overlap_T(n_cmp, n_sel):
    ov = np.zeros((CMP_PAD, CMP_PAD), np.float32)
    cs = np.arange(n_cmp) * CMP_STRIDE
    ss = np.arange(n_sel) * SEL_BLOCK
    o = np.minimum(cs[None] + CMP_BLOCK, ss[:, None] + SEL_BLOCK) - np.maximum(cs[None], ss[:, None])
    ov[:n_cmp, :n_sel] = (np.clip(o, 0, None) / CMP_BLOCK).T
    return jnp.asarray(ov, dtype=BF16)


def _sel_expand(T):
    t = np.arange(T)
    e = (np.arange(CMP_PAD)[:, None] == (t // SEL_BLOCK)[None, :]).astype(np.float32)
    return jnp.asarray(e.reshape(CMP_PAD, T // ATT_K_TILE, ATT_K_TILE).transpose(1, 0, 2), dtype=BF16)


KV_ROWS = KV_SLOTS * NSA_G * NSA_HD
SLOT_ROWS = NSA_G * NSA_HD
N_PAGES = PAST_LEN // PAGE_SIZE
DEC_N_CHUNK = (PAST_LEN + DEC_SEQ) // CMP_STRIDE
DEC_N_CMP = DEC_N_CHUNK - CMP_R + 1
DEC_N_SEL = -(-(PAST_LEN + DEC_SEQ) // SEL_BLOCK)
WIN_BUF = min(WINDOW, PAST_LEN)


def _softmax_rows(s, mask, s_new=None):
    s = jnp.where(mask, s, NEG_BIG)
    m = jnp.max(s, axis=-1, keepdims=True)
    if s_new is not None:
        m = jnp.maximum(m, s_new)
    e = jnp.where(mask, jnp.exp(s - m), 0.0)
    den = jnp.sum(e, axis=-1, keepdims=True)
    if s_new is None:
        return e, den
    e_new = jnp.exp(s_new - m)
    return e, e_new, den + e_new


def _dec_nsa_kernel(pt_ref, *refs):
    pages = refs[:N_PAGES]
    (win_ref, qn_ref, qr_ref, new_ref, gate_ref, w1_ref, b1_ref, w2_ref, b2_ref, kn_ref,
     ov_ref, exp_ref, grp_ref, o_ref, xt_ref, acc_ref) = refs[N_PAGES:]
    del pt_ref
    f32 = F32
    half = 2 * NSA_HD
    n_chunk = DEC_N_CHUNK

    for p in range(N_PAGES):
        for sg in range(4):
            xt_ref[sg, p * PAGE_SIZE:(p + 1) * PAGE_SIZE, :] = pages[p][sg * half:(sg + 1) * half, :].T

    lane_lo = lax.broadcasted_iota(jnp.int32, (n_chunk, 2 * half), 1) % half < NSA_HD
    lane_grp = lax.broadcasted_iota(jnp.int32, (n_chunk, SLOT_ROWS), 1) // NSA_HD
    cmp_rows = []
    for slot in range(2):
        for gp in range(2):
            for rp in range(CMP_STRIDE // 2):
                xr = jnp.concatenate(
                    [xt_ref[slot * 2 + gp, pl.ds(2 * rp + j, n_chunk, stride=CMP_STRIDE), :] for j in range(2)],
                    axis=1)
                xs = jnp.concatenate([jnp.where(lane_lo, xr, 0.0), jnp.where(lane_lo, 0.0, xr)],
                                     axis=0).astype(BF16)
                part = jnp.dot(xs, w1_ref[slot, rp], preferred_element_type=f32)
                rows = pl.ds(gp * 2 * n_chunk, 2 * n_chunk)
                if rp == 0:
                    acc_ref[rows, :] = part
                else:
                    acc_ref[rows, :] += part
        acc = acc_ref[...]
        pre = b1_ref[slot] + acc[:, :CMP_HID] + pltpu.roll(acc[:, CMP_HID:], NSA_G * n_chunk - 1, 0)
        out = jnp.dot(jax.nn.gelu(pre).astype(BF16), w2_ref[slot], preferred_element_type=f32) + b2_ref[slot]
        if slot == 0:
            out = _rms_rows(out, kn_ref[...])
        sel_rows = jnp.zeros((n_chunk, SLOT_ROWS), f32)
        for g in range(NSA_G):
            sel_rows = sel_rows + jnp.where(lane_grp == g, out[g * n_chunk:(g + 1) * n_chunk], 0.0)
        cmp_rows.append(sel_rows.astype(BF16))
    kc, vc = cmp_rows

    qn = qn_ref[...].astype(BF16)
    qr = qr_ref[...].astype(BF16)
    nt = (((1,), (1,)), ((), ()))
    c = lax.broadcasted_iota(jnp.int32, (NSA_H, CMP_PAD), 1)
    s = lax.dot_general(qn, kc, nt, preferred_element_type=f32)
    e, den = _softmax_rows(s, c < DEC_N_CMP)
    prob = e / jnp.where(den > 0, den, 1.0)
    o_cmp = jnp.dot(prob.astype(BF16), vc, preferred_element_type=f32)
    p_hi, p_mid = _split_bf16(prob)
    p_lo = (prob - p_hi.astype(f32) - p_mid.astype(f32)).astype(BF16)
    grp = grp_ref[...]
    psum = (jnp.dot(grp, p_hi, preferred_element_type=f32) + jnp.dot(grp, p_mid, preferred_element_type=f32)
            + jnp.dot(grp, p_lo, preferred_element_type=f32))
    imp = jnp.dot(psum.astype(BF16), ov_ref[...], preferred_element_type=f32)
    qb = (PAST_LEN + DEC_SEQ - 1) // SEL_BLOCK
    valid = c <= qb
    forced = (c == 0) | (c == qb) | (c == qb - 1)
    score = jnp.where(valid, jnp.where(forced, FORCE_SCORE, imp), -jnp.inf)
    rank = jnp.zeros((NSA_H, CMP_PAD), f32)
    for jp in range(DEC_N_SEL):
        col = score[:, jp:jp + 1]
        rank = rank + jnp.where((col > score) | ((col == score) & (c > jp)), 1.0, 0.0)
    sel = jnp.where((rank < min(SEL_TOP, DEC_N_SEL)) & (c < DEC_N_SEL), 1.0, 0.0).astype(BF16)

    new = new_ref[...]
    new_b = new.astype(BF16).astype(f32)
    qr_f = qr.astype(f32)
    s_pages = [jnp.dot(qr, pages[p][2 * SLOT_ROWS:3 * SLOT_ROWS, :].astype(BF16), preferred_element_type=f32)
               for p in range(N_PAGES)]
    s = jnp.concatenate(s_pages, axis=1)
    mask = jnp.dot(sel, exp_ref[...], preferred_element_type=f32) > 0.5
    s_new = jnp.sum(qr_f * new_b[0:1], axis=-1, keepdims=True)
    e, e_new, den = _softmax_rows(s, mask, s_new)
    e = e.astype(BF16)
    o_slc = e_new.astype(BF16).astype(f32) * new_b[1:2]
    for p in range(N_PAGES):
        o_slc = o_slc + lax.dot_general(e[:, p * PAGE_SIZE:(p + 1) * PAGE_SIZE],
                                        pages[p][3 * SLOT_ROWS:4 * SLOT_ROWS, :].astype(BF16), nt,
                                        preferred_element_type=f32)
    o_slc = o_slc / den

    s = jnp.dot(qr, win_ref[0:SLOT_ROWS, :].astype(BF16), preferred_element_type=f32)
    i_buf = lax.broadcasted_iota(jnp.int32, (NSA_H, WIN_BUF), 1)
    s_new = jnp.sum(qr_f * new_b[2:3], axis=-1, keepdims=True)
    e, e_new, den = _softmax_rows(s, WIN_BUF - i_buf < WINDOW, s_new)
    o_win = e_new.astype(BF16).astype(f32) * new_b[3:4] + lax.dot_general(
        e.astype(BF16), win_ref[SLOT_ROWS:2 * SLOT_ROWS, :].astype(BF16), nt, preferred_element_type=f32)
    o_win = o_win / den

    gates = gate_ref[...]
    o_ref[...] = gates[:, 0:1] * o_cmp + gates[:, 1:2] * o_slc + gates[:, 2:3] * o_win


def dec_nsa(page_table, cache_t, win_t, qn16, qr16, new_rows, gates, w1t, b1, w2t, b2t, kn, ovT, expand, grp):
    DB = qn16.shape[0]
    const = lambda shape: pl.BlockSpec(shape, lambda b, pt: (0,) * len(shape))
    per_b = lambda shape: pl.BlockSpec((None,) + shape, lambda b, pt: (b,) + (0,) * len(shape))
    page_specs = [pl.BlockSpec((None, KV_ROWS, PAGE_SIZE), functools.partial(lambda b, pt, p: (pt[b, p], 0, 0), p=p))
                  for p in range(N_PAGES)]
    in_specs = page_specs + [
        per_b((2 * SLOT_ROWS, WIN_BUF)), per_b((NSA_H, SLOT_ROWS)), per_b((NSA_H, SLOT_ROWS)),
        per_b((4, SLOT_ROWS)), per_b((NSA_H, 3)),
        const(w1t.shape), const(b1.shape), const(w2t.shape), const(b2t.shape), const(kn.shape),
        const(ovT.shape), const(expand.shape), const(grp.shape),
    ]
    grid_spec = pltpu.PrefetchScalarGridSpec(
        num_scalar_prefetch=1, grid=(DB,), in_specs=in_specs,
        out_specs=pl.BlockSpec((None, NSA_H, SLOT_ROWS), lambda b, pt: (b, 0, 0)),
        scratch_shapes=[pltpu.VMEM((4, PAST_LEN, 2 * NSA_HD), F32),
                        pltpu.VMEM((NSA_G * DEC_N_CHUNK, CMP_R * CMP_HID), F32)])
    return pl.pallas_call(
        _dec_nsa_kernel,
        grid_spec=grid_spec,
        out_shape=jax.ShapeDtypeStruct((DB, NSA_H, SLOT_ROWS), F32),
        compiler_params=pltpu.CompilerParams(
            dimension_semantics=("arbitrary",), vmem_limit_bytes=VMEM_LIMIT_BYTES),
        name="dec_nsa",
    )(page_table, *([cache_t] * N_PAGES), win_t, qn16, qr16, new_rows, gates, w1t, b1, w2t, b2t, kn, ovT, expand, grp)


def _dec_cmp_weights(w1, b1, w2, b2):
    w = w1.reshape(CMP_R, CMP_STRIDE // 2, 2, 1, NSA_HD, CMP_HID)
    w = jnp.broadcast_to(w, (CMP_R, CMP_STRIDE // 2, 2, 2, NSA_HD, CMP_HID))
    w = jnp.moveaxis(w, 0, 4).reshape(CMP_STRIDE // 2, 4 * NSA_HD, CMP_R * CMP_HID)
    return (w.astype(BF16), b1.reshape(1, CMP_HID), jnp.tile(w2, (1, NSA_G)).astype(BF16),
            jnp.tile(b2, NSA_G).reshape(1, SLOT_ROWS))


def _place_heads(q):
    own = (jnp.arange(NSA_H)[:, None] // NSA_HPG) == jnp.arange(NSA_G)[None, :]
    return jnp.where(own[None, :, :, None], q[:, :, None, :], 0.0).reshape(q.shape[0], NSA_H, SLOT_ROWS)


def _take_heads(o):
    o = o.reshape(o.shape[0], NSA_H, NSA_G, NSA_HD)
    return o[:, jnp.arange(NSA_H), jnp.arange(NSA_H) // NSA_HPG, :].reshape(o.shape[0], NSA_W)


WKV_C = 64
WKV_PAIR = 2 * RWKV_HD
WKV_T_TILE = 512
WKV_PAIRS_PER_STEP = 4


def _split_bf16(x):
    hi = x.astype(BF16)
    return hi, (x - hi.astype(F32)).astype(BF16)


def _dot3(a, b):
    a_hi, a_lo = _split_bf16(a)
    b_hi, b_lo = _split_bf16(b)
    return (jnp.dot(a_hi, b_hi, preferred_element_type=F32) + jnp.dot(a_hi, b_lo, preferred_element_type=F32)
            + jnp.dot(a_lo, b_hi, preferred_element_type=F32))


def _wkv_kernel(r_ref, lw_ref, k_ref, v_ref, a_ref, b_ref, s0_ref, y_ref, sT_ref, s_scr):
    C = WKV_C
    P = WKV_PAIR
    n_chunks = r_ref.shape[0] // C

    @pl.when(pl.program_id(2) == 0)
    def _():
        s_scr[...] = s0_ref[...]

    lo_lane = lax.broadcasted_iota(jnp.int32, (C, P), 1) < RWKV_HD
    row = lax.broadcasted_iota(jnp.int32, (2 * C, 2 * C), 0)
    col = lax.broadcasted_iota(jnp.int32, (2 * C, 2 * C), 1)
    same_head = (row // C) == (col // C)
    strict = same_head & (row > col)
    lower = same_head & (row >= col)
    eye = jnp.where(row == col, 1.0, 0.0)
    tril = jnp.where(lax.broadcasted_iota(jnp.int32, (C, C), 0) >= lax.broadcasted_iota(jnp.int32, (C, C), 1),
                     1.0, 0.0).astype(BF16)

    def stack(x):
        return jnp.concatenate([jnp.where(lo_lane, x, 0.0), jnp.where(lo_lane, 0.0, x)], axis=0)

    def chunk(c, carry):
        stages = [pair_chunk(c, q) for q in range(WKV_PAIRS_PER_STEP)]
        while stages:
            stages = [g for g in stages if next(g, True) is None]
        return carry

    def pair_chunk(c, q):
        sl = pl.ds(pl.multiple_of(c * C, C), C)
        lanes = slice(q * P, (q + 1) * P)
        r, lw, k, v, a, b = (ref[sl, lanes] for ref in (r_ref, lw_ref, k_ref, v_ref, a_ref, b_ref))
        lw_hi, lw_mid = _split_bf16(lw)
        lw_lo = (lw - lw_hi.astype(F32) - lw_mid.astype(F32)).astype(BF16)
        cs = (jnp.dot(tril, lw_hi, preferred_element_type=F32) + jnp.dot(tril, lw_mid, preferred_element_type=F32)
              + jnp.dot(tril, lw_lo, preferred_element_type=F32))
        yield
        g_inv = jnp.exp(-cs)
        g_end = jnp.exp(cs[C - 1:C, :] - cs)
        a2 = stack(a * jnp.exp(cs - lw))
        r2 = stack(r * jnp.exp(cs))
        b2 = stack(b * g_inv)
        k2 = stack(k * g_inv)
        v2 = stack(v)
        s_old = s_scr[q]
        ar = jnp.concatenate([a2, r2], axis=0).astype(BF16)
        bk = jnp.concatenate([b2, k2], axis=0).astype(BF16)
        nt = (((1,), (1,)), ((), ()))
        pp = lax.dot_general(ar, bk, nt, preferred_element_type=F32)
        from_state = lax.dot_general(ar, s_old.astype(BF16), nt, preferred_element_type=F32)
        yield
        l_ab = jnp.where(strict, pp[:2 * C, :2 * C], 0.0)
        l_ak = jnp.where(strict, pp[:2 * C, 2 * C:], 0.0)
        m_rb = jnp.where(lower, pp[2 * C:, :2 * C], 0.0)
        m_rk = jnp.where(lower, pp[2 * C:, 2 * C:], 0.0)
        v2b = v2.astype(BF16)
        rhs = from_state[:2 * C] + jnp.dot(l_ak.astype(BF16), v2b, preferred_element_type=F32)
        yield
        n = l_ab
        x = eye + n
        span = 2
        while span < C:
            n = _dot3(n, n)
            yield
            x = x + _dot3(n, x)
            yield
            span *= 2
        u2 = _dot3(x, rhs)
        yield
        uv = jnp.concatenate([u2, v2], axis=0).astype(BF16)
        y2 = from_state[2 * C:] + jnp.dot(jnp.concatenate([m_rb, m_rk], axis=1).astype(BF16), uv,
                                          preferred_element_type=F32)
        yield
        y_ref[sl, lanes] = y2[:C] + y2[C:]
        bk_end = jnp.concatenate([stack(b * g_end), stack(k * g_end)], axis=0).astype(BF16)
        s_scr[q] = s_old * jnp.exp(cs[C - 1:C, :]) + lax.dot_general(
            uv, bk_end, (((0,), (0,)), ((), ())), preferred_element_type=F32)

    lax.fori_loop(0, n_chunks, chunk, 0)

    @pl.when(pl.program_id(2) == pl.num_programs(2) - 1)
    def _():
        sT_ref[...] = s_scr[...]


def wkv7_chunked(r, lw, k, v, a, b, s0):
    B, T, W = r.shape
    n_pair = W // WKV_PAIR
    tt = min(WKV_T_TILE, T)
    s0p = s0.astype(F32).reshape(B, n_pair, 2, RWKV_HD, RWKV_HD)
    zero = jnp.zeros_like(s0p[:, :, 0])
    s0_bd = jnp.concatenate([jnp.concatenate([s0p[:, :, 0], zero], axis=-1),
                             jnp.concatenate([zero, s0p[:, :, 1]], axis=-1)], axis=-2)
    pps = WKV_PAIRS_PER_STEP
    seq = pl.BlockSpec((None, tt, pps * WKV_PAIR), lambda i, p, t: (i, t, p))
    st = pl.BlockSpec((None, pps, WKV_PAIR, WKV_PAIR), lambda i, p, t: (i, p, 0, 0))
    y, s_bd = pl.pallas_call(
        _wkv_kernel,
        grid=(B, n_pair // pps, T // tt),
        in_specs=[seq] * 6 + [st],
        out_specs=[seq, st],
        out_shape=[jax.ShapeDtypeStruct((B, T, W), F32),
                   jax.ShapeDtypeStruct((B, n_pair, WKV_PAIR, WKV_PAIR), F32)],
        scratch_shapes=[pltpu.VMEM((pps, WKV_PAIR, WKV_PAIR), F32)],
        compiler_params=pltpu.CompilerParams(
            dimension_semantics=("parallel", "parallel", "arbitrary"), vmem_limit_bytes=VMEM_LIMIT_BYTES),
        name="wkv7_chunked",
    )(r, lw, k, v, a, b, s0_bd)
    s_fin = jnp.stack([s_bd[:, :, :RWKV_HD, :RWKV_HD], s_bd[:, :, RWKV_HD:, RWKV_HD:]], axis=2)
    return y, s_fin.reshape(B, W // RWKV_HD, RWKV_HD, RWKV_HD)


AB_PAD = _round_up(AB_COLS, COL_TILE)
SHIFT_PAD = _round_up(SHIFT_W, LANE)
LORA_PAD = SHIFT_PAD - 3 * RWKV_W
EVEN_ROWS = 128
N_EVEN_PRE_OUT = 10


def _split3(x):
    hi = x.astype(BF16)
    r1 = x - hi.astype(F32)
    mid = r1.astype(BF16)
    return hi, mid, (r1 - mid.astype(F32)).astype(BF16)


def _dot_01(x, m):
    return sum(jnp.dot(part, m, preferred_element_type=F32) for part in _split3(x))


def _head_sum(x, red_ref, exp_ref):
    return _dot_01(_dot_01(x, red_ref[...]), exp_ref[...])


def _expm1(x):
    u = jnp.exp(x)
    d = u - 1.0
    log_u = jnp.where((d == 0.0) | (d == -1.0), 1.0, jnp.log(u))
    return jnp.where(d == 0.0, x, jnp.where(d == -1.0, -1.0, d * x / log_u))


def _even_pre_math(x_ref, prev, taps, prm, outs):
    (cw_ref, cb_ref, wa_ref, ba_ref, wx_ref, bx_ref, lam_ref, mu_ref, w0_ref, a0_ref, wl_ref,
     kk_ref, ka_ref, red_ref, exp_ref) = prm
    a_o, u_o, gate_o, r_o, lw_o, k_o, v_o, na_o, nb_o, g_o = outs
    t1, t2, t3 = taps
    xb = x_ref[:, 0:LRU_W]
    xc = cb_ref[...] + cw_ref[0:1] * t3 + cw_ref[1:2] * t2 + cw_ref[2:3] * t1 + cw_ref[3:4] * xb
    xcb = xc.astype(BF16)
    gate_r = jax.nn.sigmoid(jnp.dot(xcb, wa_ref[...], preferred_element_type=F32) + ba_ref[...])
    gate_i = jax.nn.sigmoid(jnp.dot(xcb, wx_ref[...], preferred_element_type=F32) + bx_ref[...])
    log_a = -LRU_C * gate_r * lam_ref[...]
    a_o[...] = jnp.exp(log_a)
    u_o[...] = jnp.sqrt(-_expm1(2.0 * log_a)) * (gate_i * xc)
    gate_o[...] = jax.nn.gelu(x_ref[:, LRU_W:2 * LRU_W])
    rw = x_ref[:, 2 * LRU_W:2 * LRU_W + SHIFT_PAD]
    rs = rw + mu_ref[...] * (prev - rw)
    r_o[...] = rs[:, 0:RWKV_W]
    k = rs[:, RWKV_W:2 * RWKV_W]
    v_o[...] = rs[:, 2 * RWKV_W:3 * RWKV_W]
    tail = rs[:, 3 * RWKV_W:]
    lane = lax.broadcasted_iota(jnp.int32, tail.shape, 1)
    act = jnp.where(lane < W_LORA, jnp.tanh(tail), jnp.where(lane < W_LORA + A_LORA, tail, jax.nn.sigmoid(tail)))
    z = jnp.dot(act.astype(BF16), wl_ref[...], preferred_element_type=F32)
    w_log = -jax.nn.softplus(-(w0_ref[...] + z[:, 0:RWKV_W])) - 0.5
    lw_o[...] = -jnp.exp(w_log)
    a_icl = jax.nn.sigmoid(a0_ref[...] + z[:, RWKV_W:2 * RWKV_W])
    g_o[...] = z[:, 2 * RWKV_W:]
    kk = k * kk_ref[...]
    kk = kk / jnp.maximum(jnp.sqrt(_head_sum(kk * kk, red_ref, exp_ref)), 1e-12)
    k_o[...] = k * (1.0 + (a_icl - 1.0) * ka_ref[...])
    na_o[...] = -kk
    nb_o[...] = kk * a_icl


def _even_pre_seq_kernel(x_ref, conv0_ref, shift0_ref, *refs):
    prm = refs[:15]
    outs = refs[15:15 + N_EVEN_PRE_OUT]
    conv_c, shift_c = refs[15 + N_EVEN_PRE_OUT:]
    rows = x_ref.shape[0]

    @pl.when(pl.program_id(1) == 0)
    def _():
        conv_c[...] = conv0_ref[...]
        shift_c[...] = shift0_ref[...]

    xb = x_ref[:, 0:LRU_W]
    row = lax.broadcasted_iota(jnp.int32, xb.shape, 0)
    taps = []
    for j in (1, 2, 3):
        tap = pltpu.roll(xb, j, 0)
        for i in range(j):
            tap = jnp.where(row == i, conv_c[8 - j + i:9 - j + i, :], tap)
        taps.append(tap)
    rw = x_ref[:, 2 * LRU_W:2 * LRU_W + SHIFT_PAD]
    row_w = lax.broadcasted_iota(jnp.int32, rw.shape, 0)
    prev = jnp.where(row_w == 0, shift_c[7:8, :], pltpu.roll(rw, 1, 0))
    _even_pre_math(x_ref, prev, taps, prm, outs)
    conv_c[...] = x_ref[rows - 8:rows, 0:LRU_W]
    shift_c[...] = x_ref[rows - 8:rows, 2 * LRU_W:2 * LRU_W + SHIFT_PAD]


def _even_pre_step_kernel(x_ref, prev_ref, t1_ref, t2_ref, t3_ref, *refs):
    _even_pre_math(x_ref, prev_ref[...], (t1_ref[...], t2_ref[...], t3_ref[...]), refs[:15], refs[15:])


def _even_params(p):
    def bd(w):
        eye = jnp.eye(LRU_BLOCKS, dtype=w.dtype)
        return (eye[:, None, :, None] * w[:, :, None, :]).reshape(LRU_W, LRU_W).astype(BF16)
    row = lambda v: v.reshape(1, -1).astype(F32)
    wl = jnp.zeros((LORA_PAD, 3 * RWKV_W), F32)
    wl = wl.at[0:W_LORA, 0:RWKV_W].set(p['w2'])
    wl = wl.at[W_LORA:W_LORA + A_LORA, RWKV_W:2 * RWKV_W].set(p['a2'])
    wl = wl.at[W_LORA + A_LORA:W_LORA + A_LORA + G_LORA, 2 * RWKV_W:].set(p['g2'])
    head = np.arange(RWKV_W) // RWKV_HD
    red = jnp.asarray(head[:, None] == np.arange(LANE)[None, :], dtype=BF16)
    mu = jnp.pad(p['mu'], (0, SHIFT_PAD - SHIFT_W))
    return [p['conv_w'].astype(F32), row(p['conv_b']), bd(p['wa']), row(p['ba']), bd(p['wx']), row(p['bx']),
            row(jax.nn.softplus(-p['lam'].astype(F32))), row(mu), row(p['w0']), row(p['a0']), wl.astype(BF16),
            row(p['k_k']), row(p['k_a']), red, red.T]


def _const_spec(a, n_grid):
    return pl.BlockSpec(a.shape, lambda *_: (0,) * a.ndim)


def even_pre_seq(proj, conv0, shift0, prm, B, T):
    tr = EVEN_ROWS
    nt = T // tr
    conv_pad = jnp.pad(conv0.astype(F32), ((0, 0), (8 - (CONV_W - 1), 0), (0, 0)))
    shift_pad = jnp.pad(shift0.astype(F32)[:, None, :], ((0, 0), (7, 0), (0, SHIFT_PAD - SHIFT_W)))
    out_spec = pl.BlockSpec((tr, LRU_W), lambda b, t: (b * nt + t, 0))
    return pl.pallas_call(
        _even_pre_seq_kernel,
        grid=(B, nt),
        in_specs=[pl.BlockSpec((tr, AB_PAD), lambda b, t: (b * nt + t, 0)),
                  pl.BlockSpec((None, 8, LRU_W), lambda b, t: (b, 0, 0)),
                  pl.BlockSpec((None, 8, SHIFT_PAD), lambda b, t: (b, 0, 0))] + [_const_spec(a, 2) for a in prm],
        out_specs=[out_spec] * N_EVEN_PRE_OUT,
        out_shape=[jax.ShapeDtypeStruct((B * T, LRU_W), F32)] * N_EVEN_PRE_OUT,
        scratch_shapes=[pltpu.VMEM((8, LRU_W), F32), pltpu.VMEM((8, SHIFT_PAD), F32)],
        compiler_params=pltpu.CompilerParams(
            dimension_semantics=("parallel", "arbitrary"), vmem_limit_bytes=VMEM_LIMIT_BYTES),
        name="even_pre_seq",
    )(proj, conv_pad, shift_pad, *prm)


def even_pre_step(proj, row0, conv0, shift0, prm):
    n = conv0.shape[0]
    shift_pad = jnp.pad(shift0.astype(F32), ((0, 0), (0, SHIFT_PAD - SHIFT_W)))
    taps = [conv0[:, CONV_W - 1 - j].astype(F32) for j in (1, 2, 3)]
    full = lambda w: pl.BlockSpec((n, w), lambda i: (0, 0))
    return pl.pallas_call(
        _even_pre_step_kernel,
        grid=(1,),
        in_specs=[pl.BlockSpec((n, AB_PAD), lambda i: (row0 // n, 0)), full(SHIFT_PAD)] + [full(LRU_W)] * 3
        + [_const_spec(a, 1) for a in prm],
        out_specs=[full(LRU_W)] * N_EVEN_PRE_OUT,
        out_shape=[jax.ShapeDtypeStruct((n, LRU_W), F32)] * N_EVEN_PRE_OUT,
        compiler_params=pltpu.CompilerParams(
            dimension_semantics=("arbitrary",), vmem_limit_bytes=VMEM_LIMIT_BYTES),
        name="even_pre_step",
    )(proj, shift_pad, *taps, *prm)


def _even_post_kernel(hs_ref, gate_ref, y_ref, r_ref, k_ref, v_ref, g_ref, lng_ref, lnb_ref, rk_ref,
                      red_ref, exp_ref, *rest):
    o_ref = rest[-1]
    y = y_ref[...]
    mu = _head_sum(y, red_ref, exp_ref) * (1.0 / RWKV_HD)
    d = y - mu
    var = _head_sum(d * d, red_ref, exp_ref) * (1.0 / RWKV_HD)
    yn = d * lax.rsqrt(var + 64e-5) * lng_ref[...] + lnb_ref[...]
    bonus = _head_sum(r_ref[...] * k_ref[...] * rk_ref[...], red_ref, exp_ref) * v_ref[...]
    o_ref[:, 0:LRU_W] = (hs_ref[...] * gate_ref[...]).astype(o_ref.dtype)
    o_ref[:, LRU_W:] = ((yn + bonus) * g_ref[...]).astype(o_ref.dtype)


def even_post(hs, gate, y, r, k, v, g, p, red, n_total, row0, prior=None):
    n = hs.shape[0]
    tr = EVEN_ROWS
    row = lambda a: a.reshape(1, -1).astype(F32)
    consts = [row(p['ln_g']), row(p['ln_b']), row(p['r_k']), red, red.T]
    seq = pl.BlockSpec((tr, LRU_W), lambda i: (i, 0))
    args = [hs, gate, y, r, k, v, g] + consts
    in_specs = [seq] * 7 + [_const_spec(a, 1) for a in consts]
    aliases = {}
    if prior is not None:
        args.append(prior)
        in_specs.append(pl.BlockSpec(memory_space=pl.ANY))
        aliases = {len(args) - 1: 0}
    return pl.pallas_call(
        _even_post_kernel,
        grid=(n // tr,),
        in_specs=in_specs,
        out_specs=pl.BlockSpec((tr, D_MODEL), lambda i: (row0 // tr + i, 0)),
        out_shape=jax.ShapeDtypeStruct((n_total, D_MODEL), BF16),
        input_output_aliases=aliases,
        compiler_params=pltpu.CompilerParams(
            dimension_semantics=("parallel",), vmem_limit_bytes=VMEM_LIMIT_BYTES),
        name="even_post",
    )(*args)


def _retention_kernel(q_ref, k_ref, v_ref, dm_ref, rd_ref, kd_ref, sd_ref, o_ref, s_out_ref, s_scr):
    C = q_ref.shape[0]

    @pl.when(pl.program_id(2) == 0)
    def _():
        s_scr[...] = jnp.zeros(s_scr.shape, F32)

    lo = lax.broadcasted_iota(jnp.int32, (C, 2 * RET_DK), 1) < RET_DK

    def stack(x):
        return jnp.concatenate([jnp.where(lo, x, 0.0), jnp.where(lo, 0.0, x)], axis=0)

    q2 = stack(q_ref[...]).astype(BF16)
    k2 = stack(k_ref[...])
    v2 = jnp.concatenate([v_ref[:, 0:RET_DV], v_ref[:, RET_DV:]], axis=0).astype(BF16)
    s = lax.dot_general(q2, k2.astype(BF16), (((1,), (1,)), ((), ())), preferred_element_type=F32) * dm_ref[...]
    s_old = s_scr[...]
    o2 = jnp.dot(s.astype(BF16), v2, preferred_element_type=F32) + jnp.dot(
        q2, s_old.astype(BF16), preferred_element_type=F32) * rd_ref[...]
    o_ref[:, 0:RET_DV] = o2[:C]
    o_ref[:, RET_DV:] = o2[C:]
    s_new = s_old * sd_ref[...] + lax.dot_general((k2 * kd_ref[...]).astype(BF16), v2, (((0,), (0,)), ((), ())),
                                                  preferred_element_type=F32)
    s_scr[...] = s_new

    @pl.when(pl.program_id(2) == pl.num_programs(2) - 1)
    def _():
        s_out_ref[...] = s_new


def retention_prompt_pallas(rq, rk, rv):
    B, T, _ = rq.shape
    C = RET_CHUNK
    f32 = F32
    lg = jnp.log1p(-jnp.exp2(-5.0 - jnp.arange(RET_H, dtype=f32))).reshape(RET_H // 2, 2)
    i = jnp.arange(C, dtype=f32)
    diff = i[:, None] - i[None, :]
    causal = diff >= 0
    dmask = jnp.where(causal, jnp.exp(jnp.where(causal, diff, 0.0)[None, None] * lg[:, :, None, None]), 0.0)
    zero = jnp.zeros_like(dmask[:, 0])
    dm = jnp.concatenate([jnp.concatenate([dmask[:, 0], zero], axis=-1),
                          jnp.concatenate([zero, dmask[:, 1]], axis=-1)], axis=-2)
    rows = lambda x, w: jnp.broadcast_to(x[:, :, :, None], x.shape + (w,)).reshape(RET_H // 2, -1, w)
    rd = rows(jnp.exp((i[None, None, :] + 1.0) * lg[:, :, None]), RET_DV)
    kd = rows(jnp.exp((C - 1.0 - i)[None, None, :] * lg[:, :, None]), 2 * RET_DK)
    sd = rows(jnp.broadcast_to(jnp.exp(C * lg)[:, :, None], (RET_H // 2, 2, RET_DK)), RET_DV)
    qk_spec = pl.BlockSpec((None, C, 2 * RET_DK), lambda b, p, c: (b, c, p))
    v_spec = pl.BlockSpec((None, C, 2 * RET_DV), lambda b, p, c: (b, c, p))
    const = lambda a: pl.BlockSpec((None,) + a.shape[1:], lambda b, p, c: (p, 0, 0))
    o, s = pl.pallas_call(
        _retention_kernel,
        grid=(B, RET_H // 2, T // C),
        in_specs=[qk_spec, qk_spec, v_spec, const(dm), const(rd), const(kd), const(sd)],
        out_specs=[v_spec, pl.BlockSpec((None, None, 2 * RET_DK, RET_DV), lambda b, p, c: (b, p, 0, 0))],
        out_shape=[jax.ShapeDtypeStruct((B, T, RET_W), f32),
                   jax.ShapeDtypeStruct((B, RET_H // 2, 2 * RET_DK, RET_DV), f32)],
        scratch_shapes=[pltpu.VMEM((2 * RET_DK, RET_DV), f32)],
        compiler_params=pltpu.CompilerParams(
            dimension_semantics=("parallel", "parallel", "arbitrary"), vmem_limit_bytes=VMEM_LIMIT_BYTES),
        name="retention_prompt",
    )(rq, rk, rv, dm, rd, kd, sd)
    return s.reshape(B, RET_H, RET_DK, RET_DV), o


def rms_norm(x, g, eps=1e-6):
    xf = x.astype(jnp.float32)
    y = xf * lax.rsqrt(jnp.mean(xf * xf, axis=-1, keepdims=True) + eps)
    return (y * g.astype(jnp.float32)).astype(x.dtype)


def head_group_norm(y, g, b, eps):
    yf = y.astype(jnp.float32)
    mu = jnp.mean(yf, axis=-1, keepdims=True)
    var = jnp.mean(jnp.square(yf - mu), axis=-1, keepdims=True)
    yn = ((yf - mu) * lax.rsqrt(var + eps)).reshape(y.shape[:-2] + (-1,))
    return (yn * g.astype(jnp.float32) + b.astype(jnp.float32)).astype(y.dtype)


def masked_softmax(s, mask):
    s = jnp.where(mask, s.astype(jnp.float32), -jnp.inf)
    m = jnp.max(s, axis=-1, keepdims=True)
    e = jnp.exp(s - jnp.where(jnp.isfinite(m), m, 0.0))
    den = jnp.sum(e, axis=-1, keepdims=True)
    return e / jnp.where(den > 0, den, 1.0)


def rope(x, pos, n_rot, theta):
    half = n_rot // 2
    inv = jnp.exp(-jnp.log(jnp.float32(theta)) * jnp.arange(half, dtype=jnp.float32) / half)
    ang = pos.astype(jnp.float32)[:, None] * inv[None, :]
    cos = jnp.cos(ang)[None, :, None, :]
    sin = jnp.sin(ang)[None, :, None, :]
    xf = x.astype(jnp.float32)
    x1, x2 = xf[..., :half], xf[..., half:n_rot]
    out = jnp.concatenate([x1 * cos - x2 * sin, x2 * cos + x1 * sin, xf[..., n_rot:]], axis=-1)
    return out.astype(x.dtype)


def linear_scan(a, b, h0):
    b = b.at[:, 0].add(a[:, 0] * h0)

    def combine(left, right):
        return left[0] * right[0], right[0] * left[1] + right[1]

    return lax.associative_scan(combine, (a, b), axis=1)[1]


def wkv7_scan(r, w, k, v, a, b, s0):
    xs = tuple(jnp.moveaxis(z.astype(jnp.float32), 1, 0) for z in (r, w, k, v, a, b))

    def step(S, inp):
        r_t, w_t, k_t, v_t, a_t, b_t = inp
        sa = jnp.einsum('bhij,bhj->bhi', S, a_t)
        S = S * w_t[:, :, None, :] + sa[..., None] * b_t[:, :, None, :] + v_t[..., None] * k_t[:, :, None, :]
        return S, jnp.einsum('bhij,bhj->bhi', S, r_t)

    S, ys = lax.scan(step, s0.astype(jnp.float32), xs)
    return jnp.moveaxis(ys, 0, 1), S


def even_mixer_core(proj, p, lru_h0, lru_conv0, shift0, wkv0):
    B, T, _ = proj.shape
    f32 = jnp.float32
    dt = proj.dtype
    xb, gb, rw = jnp.split(proj, [LRU_W, 2 * LRU_W], axis=-1)
    xcat = jnp.concatenate([lru_conv0.astype(dt), xb], axis=1)
    xc = p['conv_b'] + sum(p['conv_w'][j] * xcat[:, j:j + T] for j in range(CONV_W))
    xbd = xc.reshape(B, T, LRU_BLOCKS, LRU_BS)
    gate_r = jax.nn.sigmoid(jnp.einsum('btnc,ncd->btnd', xbd, p['wa']).reshape(B, T, LRU_W) + p['ba'])
    gate_i = jax.nn.sigmoid(jnp.einsum('btnc,ncd->btnd', xbd, p['wx']).reshape(B, T, LRU_W) + p['bx'])
    log_a = -LRU_C * gate_r.astype(f32) * jax.nn.softplus(-p['lam'].astype(f32))
    u = jnp.sqrt(-jnp.expm1(2.0 * log_a)) * (gate_i * xc).astype(f32)
    hs = lru_scan(jnp.exp(log_a), u, lru_h0.astype(f32))
    y_lru = hs.astype(dt) * jax.nn.gelu(gb)
    prev = jnp.concatenate([shift0.astype(dt)[:, None], rw[:, :-1]], axis=1)
    rs = rw + p['mu'] * (prev - rw)
    r, k, v, xw, xa, xg = jnp.split(
        rs, [RWKV_W, 2 * RWKV_W, 3 * RWKV_W, 3 * RWKV_W + W_LORA, 3 * RWKV_W + W_LORA + A_LORA], axis=-1)
    w_log = -jax.nn.softplus(-(p['w0'] + jnp.tanh(xw) @ p['w2']).astype(f32)) - 0.5
    log_decay = -jnp.exp(w_log)
    decay = jnp.exp(log_decay)
    a_icl = jax.nn.sigmoid(p['a0'] + xa @ p['a2'])
    g = jax.nn.sigmoid(xg) @ p['g2']
    heads = (B, T, RWKV_H, RWKV_HD)
    kk = (k * p['k_k']).reshape(heads).astype(f32)
    kk = kk / jnp.maximum(jnp.sqrt(jnp.sum(kk * kk, axis=-1, keepdims=True)), 1e-12)
    k = k * (1.0 + (a_icl - 1.0) * p['k_a'])
    rh, kh, vh, ah = (z.reshape(heads) for z in (r, k, v, a_icl))
    if T % WKV_C == 0:
        y, wkv = wkv7_chunked(r.astype(f32), log_decay, k.astype(f32), v.astype(f32),
                              (-kk).reshape(B, T, RWKV_W), (kk * ah.astype(f32)).reshape(B, T, RWKV_W), wkv0)
        y = y.reshape(heads)
    else:
        y, wkv = wkv7_scan(rh, decay.reshape(heads), kh, vh, -kk, kk * ah.astype(f32), wkv0)
    y = head_group_norm(y, p['ln_g'], p['ln_b'], 64e-5).astype(dt)
    bonus = (jnp.sum(rh * kh * p['r_k'], axis=-1, keepdims=True) * vh).reshape(B, T, RWKV_W)
    y_rwkv = (y + bonus) * g
    cat = jnp.concatenate([y_lru, y_rwkv], axis=-1)
    return cat, hs[:, -1], xcat[:, T:], rw[:, -1], wkv


def even_mixer(proj, p, B, T, DB, lru_h0, lru_conv0, shift0, wkv0):
    f32 = F32
    prm = _even_params(p)
    red = prm[-2]
    n_p = B * T
    zeros = lambda *s: jnp.zeros(s, f32)
    a, u, gate, r, lw, k, v, na, nb, g = even_pre_seq(proj, zeros(B, CONV_W - 1, LRU_W), zeros(B, SHIFT_W), prm, B, T)
    seq = lambda z: z.reshape(B, T, LRU_W)
    hs = lru_scan(seq(a), seq(u), zeros(B, LRU_W))
    yw, wkv_p = wkv7_chunked(seq(r), seq(lw), seq(k), seq(v), seq(na), seq(nb), zeros(B, RWKV_H, RWKV_HD, RWKV_HD))
    cat = even_post(hs.reshape(n_p, LRU_W), gate, yw.reshape(n_p, RWKV_W), r, k, v, g, p, red, n_p + DB, 0)
    xb_p = proj[:n_p].reshape(B, T, AB_PAD)
    st_p = (hs[:, -1], xb_p[:, T - (CONV_W - 1):, :LRU_W], xb_p[:, -1, 2 * LRU_W:AB_COLS], wkv_p)
    a, u, gate, r, lw, k, v, na, nb, g = even_pre_step(proj, n_p, lru_conv0, shift0, prm)
    hs_s = a * lru_h0.astype(f32) + u
    heads = (DB, 1, RWKV_H, RWKV_HD)
    yw, wkv_s = wkv7_scan(r.reshape(heads), jnp.exp(lw).reshape(heads), k.reshape(heads), v.reshape(heads),
                          na.reshape(heads), nb.reshape(heads), wkv0)
    cat = even_post(hs_s, gate, yw.reshape(DB, RWKV_W), r, k, v, g, p, red, n_p + DB, n_p, prior=cat)
    xb_s = proj[n_p:]
    conv_s = jnp.concatenate([lru_conv0[:, 1:].astype(f32), xb_s[:, None, :LRU_W]], axis=1)
    st_s = (hs_s, conv_s, xb_s[:, 2 * LRU_W:AB_COLS], wkv_s)
    return cat, st_p, st_s


def odd_project(proj, p, pos):
    B, T, _ = proj.shape
    sizes = [NSA_W] + [NSA_G * NSA_HD] * 6 + [3 * NSA_H, RET_H * RET_DK, RET_H * RET_DK, RET_W, RET_W]
    q, kc, vc, ks, vs, kw, vw, gt, rq, rk, rv, rg = jnp.split(
        proj, np.cumsum(sizes).tolist(), axis=-1)[:len(sizes)]
    kvs = (B, T, NSA_G, NSA_HD)
    q_n = rms_norm(q.reshape(B, T, NSA_H, NSA_HD), p['q_norm'])
    return {
        'q_n': q_n,
        'q_r': rope(q_n, pos, ROPE_DIMS, ROPE_THETA),
        'kc': kc.reshape(kvs), 'vc': vc.reshape(kvs),
        'ks': rope(rms_norm(ks.reshape(kvs), p['k_norm'][1]), pos, ROPE_DIMS, ROPE_THETA),
        'vs': vs.reshape(kvs),
        'kw': rope(rms_norm(kw.reshape(kvs), p['k_norm'][2]), pos, ROPE_DIMS, ROPE_THETA),
        'vw': vw.reshape(kvs),
        'gates': jax.nn.sigmoid(gt).reshape(B, T, NSA_H, 3),
        'rq': rope(rq.reshape(B, T, RET_H, RET_DK), pos, RET_DK, RET_THETA),
        'rk': rope(rk.reshape(B, T, RET_H, RET_DK), pos, RET_DK, RET_THETA) * (RET_DK ** -0.5),
        'rv': rv.reshape(B, T, RET_H, RET_DV),
        'rg': rg,
    }


def to_groups_q(q):
    B, T = q.shape[:2]
    return jnp.moveaxis(q.reshape(B, T, NSA_G, NSA_HPG, NSA_HD), 1, 3)


def to_groups_k(k):
    return jnp.moveaxis(k, 1, 2)


def nsa_compress(x, w1, b1, w2, b2):
    B, L = x.shape[:2]
    n_chunk = L // CMP_STRIDE
    n_cmp = n_chunk - CMP_R + 1
    ch = x[:, :n_chunk * CMP_STRIDE].reshape(B, n_chunk, CMP_STRIDE, NSA_G, NSA_HD)
    ch = jnp.moveaxis(ch, 3, 2).reshape(B, n_chunk, NSA_G, CMP_STRIDE * NSA_HD)
    part = jnp.einsum('bngc,rch->bngrh', ch, w1)
    pre = b1 + sum(part[:, m:m + n_cmp, :, m] for m in range(CMP_R))
    return jax.nn.gelu(pre) @ w2 + b2


def nsa_compressed_branch(qn, kc_raw, vc_raw, p, q_pos):
    kc = to_groups_k(rms_norm(nsa_compress(kc_raw, *p['ck']), p['k_norm'][0]))
    vc = to_groups_k(nsa_compress(vc_raw, *p['cv']))
    s = jnp.einsum('bghqd,bgcd->bghqc', qn, kc) * NSA_HD ** -0.5
    ends = jnp.arange(kc.shape[2]) * CMP_STRIDE + CMP_BLOCK - 1
    prob = masked_softmax(s, ends[None, :] <= q_pos[:, None])
    return jnp.einsum('bghqc,bgcd->bghqd', prob.astype(vc.dtype), vc), prob


def cmp_sel_overlap(n_cmp, n_sel):
    cs = np.arange(n_cmp) * CMP_STRIDE
    ss = np.arange(n_sel) * SEL_BLOCK
    ov = np.minimum(cs[None] + CMP_BLOCK, ss[:, None] + SEL_BLOCK) - np.maximum(cs[None], ss[:, None])
    return jnp.asarray(np.clip(ov, 0, None) / CMP_BLOCK, dtype=jnp.float32)


def nsa_select(p_cmp, q_pos, n_sel):
    imp = jnp.einsum('bgqc,sc->bgqs', p_cmp.sum(axis=2), cmp_sel_overlap(p_cmp.shape[-1], n_sel))
    j = jnp.arange(n_sel)[None, :]
    qb = (q_pos // SEL_BLOCK)[:, None]
    valid = j <= qb
    forced = (j == 0) | (j == qb) | (j == qb - 1)
    score = jnp.where(valid, jnp.where(forced, FORCE_SCORE, imp), -jnp.inf)
    _, idx = lax.top_k(score, min(SEL_TOP, n_sel))
    sel_ok = jnp.take_along_axis(jnp.broadcast_to(valid, score.shape), idx, axis=-1)
    return idx, sel_ok


def sel_blocks(x, n_sel):
    B, L = x.shape[:2]
    x = jnp.pad(x, ((0, 0), (0, n_sel * SEL_BLOCK - L), (0, 0), (0, 0)))
    return jnp.moveaxis(x.reshape(B, n_sel, SEL_BLOCK, NSA_G, NSA_HD), 3, 1)


def nsa_slc_attend(q, kb, vb, idx, sel_ok, q_pos):
    B, G = kb.shape[:2]
    bi = jnp.arange(B)[:, None, None, None]
    gi = jnp.arange(G)[None, :, None, None]
    kg = kb[bi, gi, idx]
    vg = vb[bi, gi, idx]
    s = jnp.einsum('bghqd,bgqnld->bghqnl', q, kg) * NSA_HD ** -0.5
    kpos = idx[..., None] * SEL_BLOCK + jnp.arange(SEL_BLOCK)
    mask = (kpos <= q_pos[None, None, :, None, None]) & sel_ok[..., None]
    sh = s.shape
    prob = masked_softmax(s.reshape(sh[:4] + (-1,)), mask.reshape(B, G, 1, sh[3], -1))
    return jnp.einsum('bghqnl,bgqnld->bghqd', prob.reshape(sh).astype(vg.dtype), vg)


def window_attend_banded(q, k, v):
    B, G, HPG, T, HD = q.shape
    nb = T // WIN_BLOCK
    npv = WINDOW // WIN_BLOCK
    pad = ((0, 0), (0, 0), (npv * WIN_BLOCK, 0), (0, 0))

    def band(z):
        zb = jnp.pad(z, pad).reshape(B, G, nb + npv, WIN_BLOCK, HD)
        return jnp.concatenate([zb[:, :, j:j + nb] for j in range(npv + 1)], axis=3)

    kb, vb = band(k), band(v)
    qb = q.reshape(B, G, HPG, nb, WIN_BLOCK, HD)
    s = jnp.einsum('bghiqd,bgikd->bghiqk', qb, kb) * NSA_HD ** -0.5
    blk = jnp.arange(nb)[:, None]
    q_pos = blk * WIN_BLOCK + jnp.arange(WIN_BLOCK)[None]
    k_pos = (blk - npv) * WIN_BLOCK + jnp.arange((npv + 1) * WIN_BLOCK)[None]
    diff = q_pos[:, :, None] - k_pos[:, None, :]
    mask = (diff >= 0) & (diff < WINDOW) & (k_pos[:, None, :] >= 0)
    prob = masked_softmax(s, mask)
    return jnp.einsum('bghiqk,bgikd->bghiqd', prob.astype(v.dtype), vb).reshape(B, G, HPG, T, HD)


def window_attend_cached(q, k, v, q_pos, k_pos):
    s = jnp.einsum('bghqd,blgd->bghql', q, k) * NSA_HD ** -0.5
    diff = q_pos[:, None] - k_pos[None, :]
    prob = masked_softmax(s, (diff >= 0) & (diff < WINDOW))
    return jnp.einsum('bghql,blgd->bghqd', prob.astype(v.dtype), v)


def retention_chunk(S, q, k, v):
    f32 = jnp.float32
    C = q.shape[1]
    lg = jnp.log1p(-jnp.exp2(-5.0 - jnp.arange(RET_H, dtype=f32)))
    i = jnp.arange(C, dtype=f32)
    diff = i[:, None] - i[None, :]
    causal = diff >= 0
    dmask = jnp.where(causal, jnp.exp(jnp.where(causal, diff, 0.0)[None] * lg[:, None, None]), 0.0)
    qf, kf, vf = q.astype(f32), k.astype(f32), v.astype(f32)
    s = jnp.einsum('bihd,bjhd->bhij', qf, kf) * dmask
    o = jnp.einsum('bhij,bjhe->bihe', s, vf)
    o = o + jnp.einsum('bihd,bhde->bihe', qf, S) * jnp.exp((i[:, None] + 1.0) * lg[None, :])[None, :, :, None]
    k_dec = kf * jnp.exp((C - 1.0 - i)[:, None] * lg[None, :])[None, :, :, None]
    S = S * jnp.exp(C * lg)[None, :, None, None] + jnp.einsum('bjhd,bjhe->bhde', k_dec, vf)
    return S, o


def retention_prompt(q, k, v):
    B, T = q.shape[:2]
    n = T // RET_CHUNK
    xs = tuple(jnp.moveaxis(z.reshape((B, n, RET_CHUNK) + z.shape[2:]), 1, 0) for z in (q, k, v))
    s0 = jnp.zeros((B, RET_H, RET_DK, RET_DV), jnp.float32)
    S, o = lax.scan(lambda S, c: retention_chunk(S, c[0], c[1], c[2]), s0, xs)
    return S, jnp.moveaxis(o, 0, 1).reshape(B, T, RET_H, RET_DV)


def odd_output(o_cmp, o_slc, o_win, o_ret, pr, p):
    gates = pr['gates']
    B, T = gates.shape[:2]
    gg = jnp.moveaxis(gates.reshape(B, T, NSA_G, NSA_HPG, 3), 1, 3)[..., None]
    o = gg[..., 0, :] * o_cmp + gg[..., 1, :] * o_slc + gg[..., 2, :] * o_win
    o_nsa = jnp.moveaxis(o, 3, 1).reshape(B, T, NSA_W)
    y_ret = head_group_norm(o_ret, p['gn_g'], p['gn_b'], 1e-5).astype(o_nsa.dtype) * jax.nn.silu(pr['rg'])
    return jnp.concatenate([o_nsa, y_ret], axis=-1)


def odd_mixer_prompt(proj, p):
    B, T, _ = proj.shape
    pos = jnp.arange(T)
    pr = odd_project(proj, p, pos)
    qn = pr['q_n'].reshape(B, T, NSA_W)
    qr = pr['q_r'].reshape(B, T, NSA_W)
    kc = rms_norm(nsa_compress(pr['kc'], *p['ck']), p['k_norm'][0])
    vc = nsa_compress(pr['vc'], *p['cv'])
    n_cmp = kc.shape[1]
    n_sel = -(-T // SEL_BLOCK)
    o_cmp, sel = nsa_cmp_select(qn, _tile_cmp(kc), _tile_cmp(vc), _overlap_T(n_cmp, n_sel),
                                n_cmp=n_cmp, n_sel=n_sel, q_pos0=0)
    o_slc = nsa_flash(qr, _tile_groups(pr['ks']), _tile_groups(pr['vs']), sel, _sel_expand(T))
    o_win = nsa_flash(qr, _tile_groups(pr['kw']), _tile_groups(pr['vw']))
    S, o_ret = retention_prompt_pallas(pr['rq'].reshape(B, T, -1), pr['rk'].reshape(B, T, -1),
                                       pr['rv'].reshape(B, T, -1))
    o_ret = o_ret.reshape(B, T, RET_H, RET_DV)
    gates = pr['gates']
    heads = (B, T, NSA_H, NSA_HD)
    o_nsa = (gates[..., 0:1] * o_cmp.reshape(heads) + gates[..., 1:2] * o_slc.reshape(heads)
             + gates[..., 2:3] * o_win.reshape(heads)).reshape(B, T, NSA_W)
    y_ret = head_group_norm(o_ret, p['gn_g'], p['gn_b'], 1e-5).astype(o_nsa.dtype) * jax.nn.silu(pr['rg'])
    out = jnp.concatenate([o_nsa, y_ret], axis=-1)
    kv_rows = jnp.stack([pr['kc'], pr['vc'], pr['ks'], pr['vs']], axis=2)
    win = jnp.stack([pr['kw'], pr['vw']], axis=2)[:, T - min(WINDOW, T):]
    return out, kv_rows, win, S


def odd_mixer_sample(proj, p, cache_layer, page_table, win_buf, ret_s0):
    B, T, _ = proj.shape
    assert T == DEC_SEQ == 1 and win_buf.shape[1] == WIN_BUF
    pos = PAST_LEN + jnp.arange(T)
    pr = odd_project(proj, p, pos)
    scale = NSA_HD ** -0.5
    new_rows = jnp.stack([pr['ks'], pr['vs'], pr['kw'], pr['vw']], axis=2)[:, 0].reshape(B, 4, SLOT_ROWS)
    cache_t = jnp.transpose(cache_layer, (0, 2, 3, 4, 1)).reshape(cache_layer.shape[0], KV_ROWS, PAGE_SIZE)
    win_t = jnp.transpose(win_buf, (0, 2, 3, 4, 1)).reshape(B, 2 * SLOT_ROWS, WIN_BUF)
    wk = _dec_cmp_weights(*p['ck'])
    wv = _dec_cmp_weights(*p['cv'])
    w1t, b1, w2t, b2t = (jnp.stack([a, b]) for a, b in zip(wk, wv))
    kn = jnp.tile(p['k_norm'][0], NSA_G).reshape(1, SLOT_ROWS)
    t = np.arange(PAST_LEN)
    expand = jnp.asarray(np.arange(CMP_PAD)[:, None] == (t // SEL_BLOCK)[None, :], dtype=BF16)
    h = np.arange(NSA_H)
    grp = jnp.asarray((h[:, None] // NSA_HPG) == (h[None, :] // NSA_HPG), dtype=BF16)
    o16 = dec_nsa(page_table, cache_t, win_t, _place_heads(pr['q_n'][:, 0] * scale),
                  _place_heads(pr['q_r'][:, 0] * scale), new_rows, pr['gates'][:, 0],
                  w1t, b1, w2t, b2t, kn, _overlap_T(DEC_N_CMP, DEC_N_SEL), expand, grp)
    o_nsa = _take_heads(o16)[:, None, :]
    S, o_ret = retention_chunk(ret_s0.astype(jnp.float32), pr['rq'], pr['rk'], pr['rv'])
    y_ret = head_group_norm(o_ret, p['gn_g'], p['gn_b'], 1e-5).astype(o_nsa.dtype) * jax.nn.silu(pr['rg'])
    out = jnp.concatenate([o_nsa, y_ret], axis=-1)
    rows = jnp.stack([pr['kc'], pr['vc'], pr['ks'], pr['vs']], axis=2).astype(cache_layer.dtype)
    new_col = jnp.stack([pr['kw'], pr['vw']], axis=2)[:, 0].reshape(B, 2 * SLOT_ROWS, 1).astype(win_buf.dtype)
    win_new = jnp.concatenate([win_t[:, :, T:], new_col], axis=2).reshape(B, 2, NSA_G, NSA_HD, WIN_BUF)
    return out, rows, jnp.transpose(win_new, (0, 4, 1, 2, 3)), S


def _stack(xs, dt):
    return jnp.stack(xs).astype(dt)


def kernel(x_prompt, x_sample, state_lru_h, state_lru_conv, state_rwkv_shift, state_rwkv_wkv,
           cache_nsa_kv, cache_nsa_win, state_ret, page_table,
           norm_ffn1, ffn1_w_in, ffn1_w_out, norm_mix, norm_ffn2, ffn2_w_in, ffn2_w_out,
           ab_w_in, lru_conv_w, lru_conv_b, lru_wa, lru_ba, lru_wx, lru_bx, lru_lambda,
           rwkv_mu, rwkv_w0, rwkv_w2, rwkv_a0, rwkv_a2, rwkv_g2, rwkv_k_k, rwkv_k_a, rwkv_r_k,
           rwkv_ln_g, rwkv_ln_b, ab_w_out,
           cd_w_in, nsa_q_norm, nsa_k_norm, cmp_k_w1, cmp_k_b1, cmp_k_w2, cmp_k_b2,
           cmp_v_w1, cmp_v_b1, cmp_v_w2, cmp_v_b2, ret_gn_g, ret_gn_b, cd_w_out):
    dt = x_prompt.dtype
    B = x_prompt.shape[0]
    DB = x_sample.shape[0]
    y = jnp.concatenate([x_prompt.reshape(N_PROMPT, D_MODEL), x_sample.reshape(DB * DEC_SEQ, D_MODEL)], axis=0)
    lru_h_p, lru_h_s, lru_c_p, lru_c_s, sh_p, sh_s, wkv_p, wkv_s = [], [], [], [], [], [], [], []
    kv_p, kv_s, win_p, win_s, ret_p, ret_s = [], [], [], [], [], []
    for layer in range(DEPTH):
        li = layer // 2
        y = ffn_block(y, norm_ffn1[layer], *_prep_ffn_weights(ffn1_w_in[layer], ffn1_w_out[layer]))
        if layer % 2 == 0:
            p = {'conv_w': lru_conv_w[li], 'conv_b': lru_conv_b[li],
                 'wa': lru_wa[li], 'ba': lru_ba[li], 'wx': lru_wx[li], 'bx': lru_bx[li], 'lam': lru_lambda[li],
                 'mu': rwkv_mu[li], 'w0': rwkv_w0[li], 'w2': rwkv_w2[li], 'a0': rwkv_a0[li], 'a2': rwkv_a2[li],
                 'g2': rwkv_g2[li], 'k_k': rwkv_k_k[li], 'k_a': rwkv_k_a[li], 'r_k': rwkv_r_k[li],
                 'ln_g': rwkv_ln_g[li], 'ln_b': rwkv_ln_b[li]}
            proj = norm_matmul(y, norm_mix[layer], _prep_cols(ab_w_in[li]))
            cat, (a0, a1, a2, a3), (b0, b1, b2, b3) = even_mixer(
                proj, p, B, SEQ, DB, state_lru_h[li], state_lru_conv[li], state_rwkv_shift[li], state_rwkv_wkv[li])
            lru_h_p.append(a0); lru_c_p.append(a1); sh_p.append(a2); wkv_p.append(a3)
            lru_h_s.append(b0); lru_c_s.append(b1); sh_s.append(b2); wkv_s.append(b3)
            w_out = ab_w_out[li]
        else:
            p = {'q_norm': nsa_q_norm[li], 'k_norm': nsa_k_norm[li],
                 'ck': (cmp_k_w1[li], cmp_k_b1[li], cmp_k_w2[li], cmp_k_b2[li]),
                 'cv': (cmp_v_w1[li], cmp_v_b1[li], cmp_v_w2[li], cmp_v_b2[li]),
                 'gn_g': ret_gn_g[li], 'gn_b': ret_gn_b[li]}
            proj = norm_matmul(y, norm_mix[layer], _prep_cols(cd_w_in[li]))
            proj_p = proj[:N_PROMPT].reshape(B, SEQ, -1)
            proj_s = proj[N_PROMPT:].reshape(DB, DEC_SEQ, -1)
            cp, a0, a1, a2 = odd_mixer_prompt(proj_p, p)
            cs, b0, b1, b2 = odd_mixer_sample(proj_s, p, cache_nsa_kv[li], page_table, cache_nsa_win[li],
                                              state_ret[li])
            kv_p.append(a0); win_p.append(a1); ret_p.append(a2)
            kv_s.append(b0); win_s.append(b1); ret_s.append(b2)
            w_out = cd_w_out[li]
            cat = jnp.concatenate([cp.reshape(N_PROMPT, D_MODEL), cs.reshape(DB * DEC_SEQ, D_MODEL)], axis=0)
        y = matmul_residual(cat, w_out.astype(BF16), y)
        y = ffn_block(y, norm_ffn2[layer], *_prep_ffn_weights(ffn2_w_in[layer], ffn2_w_out[layer]))
    yp = y[:N_PROMPT].reshape(B, SEQ, D_MODEL)
    ys = y[N_PROMPT:].reshape(DB, DEC_SEQ, D_MODEL)
    return (yp, ys,
            _stack(lru_h_p, dt), _stack(lru_h_s, dt), _stack(lru_c_p, dt), _stack(lru_c_s, dt),
            _stack(sh_p, dt), _stack(sh_s, dt), _stack(wkv_p, dt), _stack(wkv_s, dt),
            _stack(kv_p, dt), _stack(kv_s, dt), _stack(win_p, dt), _stack(win_s, dt),
            _stack(ret_p, dt), _stack(ret_s, dt))
```

```python
import functools

import jax
import jax.numpy as jnp
import numpy as np
from jax import lax
from jax.experimental import pallas as pl
from jax.experimental.pallas import tpu as pltpu

D_MODEL = 2048
BATCH = 4
SEQ = 2048
DEPTH = 2
DEC_BATCH = 128
DEC_SEQ = 1
PAST_LEN = 2048
PAGE_SIZE = 128
D_FF = 5504
LRU_W = D_MODEL // 2
LRU_BLOCKS = 16
LRU_BS = LRU_W // LRU_BLOCKS
CONV_W = 4
LRU_C = 8.0
RWKV_W = D_MODEL // 2
RWKV_HD = 64
RWKV_H = RWKV_W // RWKV_HD
W_LORA = 64
A_LORA = 64
G_LORA = 160
SHIFT_W = 3 * RWKV_W + W_LORA + A_LORA + G_LORA
AB_COLS = 2 * LRU_W + SHIFT_W
NSA_H = 16
NSA_G = 4
NSA_HPG = NSA_H // NSA_G
NSA_HD = 64
NSA_W = NSA_H * NSA_HD
ROPE_DIMS = NSA_HD // 4
ROPE_THETA = 500000.0
CMP_BLOCK = 32
CMP_STRIDE = 16
CMP_R = CMP_BLOCK // CMP_STRIDE
CMP_HID = 256
SEL_BLOCK = 64
SEL_TOP = 16
SEL_Q_BLOCK = 64
WINDOW = 512
WIN_BLOCK = 128
FORCE_SCORE = 1e4
KV_SLOTS = 4
RET_H = 8
RET_DK = 64
RET_DV = 128
RET_W = RET_H * RET_DV
RET_CHUNK = 128
RET_THETA = 10000.0
CD_COLS = NSA_W + 6 * NSA_G * NSA_HD + 3 * NSA_H + 2 * RET_H * RET_DK + 2 * RET_W

N_TOK = BATCH * SEQ + DEC_BATCH * DEC_SEQ
N_PROMPT = BATCH * SEQ

LANE = 128
VMEM_LIMIT_BYTES = 56 * 1024 * 1024
ROW_TILE = 640
FF_TILE = 512
D_FF_PAD = 5632
COL_TILE = 512

BF16 = jnp.bfloat16
F32 = jnp.float32


def _round_up(n, m):
    return -(-n // m) * m


def _rms_rows(x, g):
    ms = jnp.mean(x * x, axis=-1, keepdims=True)
    return x * lax.rsqrt(ms + 1e-6) * g


def _ffn_kernel(x_ref, g_ref, wg_ref, wu_ref, wo_ref, o_ref, xn_ref, acc_ref):
    k = pl.program_id(1)

    @pl.when(k == 0)
    def _():
        xn_ref[...] = _rms_rows(x_ref[...], g_ref[...]).astype(BF16)
        acc_ref[...] = jnp.zeros_like(acc_ref)

    xn = xn_ref[...]
    gate = jnp.dot(xn, wg_ref[...], preferred_element_type=F32)
    up = jnp.dot(xn, wu_ref[...], preferred_element_type=F32)
    act = gate * jax.nn.sigmoid(gate) * up
    acc_ref[...] += jnp.dot(act.astype(BF16), wo_ref[...], preferred_element_type=F32)

    @pl.when(k == pl.num_programs(1) - 1)
    def _():
        o_ref[...] = x_ref[...] + 0.5 * acc_ref[...]


def ffn_block(x, g, wg, wu, wo):
    m, d = x.shape
    return pl.pallas_call(
        _ffn_kernel,
        grid=(m // ROW_TILE, D_FF_PAD // FF_TILE),
        in_specs=[
            pl.BlockSpec((ROW_TILE, d), lambda i, k: (i, 0)),
            pl.BlockSpec((1, d), lambda i, k: (0, 0)),
            pl.BlockSpec((d, FF_TILE), lambda i, k: (0, k)),
            pl.BlockSpec((d, FF_TILE), lambda i, k: (0, k)),
            pl.BlockSpec((FF_TILE, d), lambda i, k: (k, 0)),
        ],
        out_specs=pl.BlockSpec((ROW_TILE, d), lambda i, k: (i, 0)),
        out_shape=jax.ShapeDtypeStruct((m, d), F32),
        scratch_shapes=[pltpu.VMEM((ROW_TILE, d), BF16), pltpu.VMEM((ROW_TILE, d), F32)],
        compiler_params=pltpu.CompilerParams(
            dimension_semantics=("parallel", "arbitrary"), vmem_limit_bytes=VMEM_LIMIT_BYTES),
        name="ffn_block",
    )(x, g.reshape(1, d), wg, wu, wo)


def _norm_matmul_kernel(x_ref, g_ref, w_ref, o_ref, xn_ref):
    @pl.when(pl.program_id(1) == 0)
    def _():
        xn_ref[...] = _rms_rows(x_ref[...], g_ref[...]).astype(BF16)

    o_ref[...] = jnp.dot(xn_ref[...], w_ref[...], preferred_element_type=F32)


def norm_matmul(x, g, w):
    m, k = x.shape
    n = w.shape[1]
    return pl.pallas_call(
        _norm_matmul_kernel,
        grid=(m // ROW_TILE, n // COL_TILE),
        in_specs=[
            pl.BlockSpec((ROW_TILE, k), lambda i, j: (i, 0)),
            pl.BlockSpec((1, k), lambda i, j: (0, 0)),
            pl.BlockSpec((k, COL_TILE), lambda i, j: (0, j)),
        ],
        out_specs=pl.BlockSpec((ROW_TILE, COL_TILE), lambda i, j: (i, j)),
        out_shape=jax.ShapeDtypeStruct((m, n), F32),
        scratch_shapes=[pltpu.VMEM((ROW_TILE, k), BF16)],
        compiler_params=pltpu.CompilerParams(
            dimension_semantics=("parallel", "arbitrary"), vmem_limit_bytes=VMEM_LIMIT_BYTES),
        name="norm_matmul",
    )(x, g.reshape(1, k), w)


def _matmul_residual_kernel(a_ref, w_ref, r_ref, o_ref):
    o_ref[...] = r_ref[...] + jnp.dot(a_ref[...].astype(BF16), w_ref[...], preferred_element_type=F32)


def matmul_residual(a, w, res):
    m, k = a.shape
    n = w.shape[1]
    return pl.pallas_call(
        _matmul_residual_kernel,
        grid=(m // ROW_TILE, n // COL_TILE),
        in_specs=[
            pl.BlockSpec((ROW_TILE, k), lambda i, j: (i, 0)),
            pl.BlockSpec((k, COL_TILE), lambda i, j: (0, j)),
            pl.BlockSpec((ROW_TILE, COL_TILE), lambda i, j: (i, j)),
        ],
        out_specs=pl.BlockSpec((ROW_TILE, COL_TILE), lambda i, j: (i, j)),
        out_shape=jax.ShapeDtypeStruct((m, n), F32),
        compiler_params=pltpu.CompilerParams(
            dimension_semantics=("parallel", "arbitrary"), vmem_limit_bytes=VMEM_LIMIT_BYTES),
        name="matmul_residual",
    )(a, w, res)


def _prep_ffn_weights(w_in, w_out):
    pad = D_FF_PAD - D_FF
    wg = jnp.pad(w_in[:, :D_FF], ((0, 0), (0, pad))).astype(BF16)
    wu = jnp.pad(w_in[:, D_FF:], ((0, 0), (0, pad))).astype(BF16)
    wo = jnp.pad(w_out, ((0, pad), (0, 0))).astype(BF16)
    return wg, wu, wo


def _prep_cols(w):
    n = w.shape[1]
    return jnp.pad(w, ((0, 0), (0, _round_up(n, COL_TILE) - n))).astype(BF16)


SCAN_TILE = 256


def _lru_scan_kernel(a_ref, b_ref, h0_ref, o_ref, carry_ref):
    @pl.when(pl.program_id(1) == 0)
    def _():
        carry_ref[...] = h0_ref[...]

    a = a_ref[...]
    b = b_ref[...]
    rows = lax.broadcasted_iota(jnp.int32, a.shape, 0)
    k = 1
    while k < a.shape[0]:
        keep = rows >= k
        b = jnp.where(keep, a * pltpu.roll(b, k, 0) + b, b)
        a = jnp.where(keep, a * pltpu.roll(a, k, 0), a)
        k *= 2
    h = a * carry_ref[...] + b
    o_ref[...] = h
    carry_ref[...] = h[a.shape[0] - 1:, :]


def lru_scan(a, b, h0):
    B, T, W = a.shape
    tt = min(SCAN_TILE, T)
    return pl.pallas_call(
        _lru_scan_kernel,
        grid=(B, T // tt),
        in_specs=[
            pl.BlockSpec((None, tt, W), lambda i, t: (i, t, 0)),
            pl.BlockSpec((None, tt, W), lambda i, t: (i, t, 0)),
            pl.BlockSpec((None, 1, W), lambda i, t: (i, 0, 0)),
        ],
        out_specs=pl.BlockSpec((None, tt, W), lambda i, t: (i, t, 0)),
        out_shape=jax.ShapeDtypeStruct((B, T, W), F32),
        scratch_shapes=[pltpu.VMEM((1, W), F32)],
        compiler_params=pltpu.CompilerParams(
            dimension_semantics=("parallel", "arbitrary"), vmem_limit_bytes=VMEM_LIMIT_BYTES),
        name="lru_scan",
    )(a, b, h0.reshape(B, 1, W))


GROUP_W = NSA_HPG * NSA_HD
ATT_Q_TILE = 128
ATT_K_TILE = 256
CMP_PAD = 128
NEG_BIG = -1e30


def _stack_heads(q):
    head = lax.broadcasted_iota(jnp.int32, q.shape, 1) // NSA_HD
    return jnp.concatenate([jnp.where(head == h, q, 0.0) for h in range(NSA_HPG)], axis=0)


def _unstack_heads(o, tq):
    head = lax.broadcasted_iota(jnp.int32, (tq, GROUP_W), 1) // NSA_HD
    out = jnp.zeros((tq, GROUP_W), F32)
    for h in range(NSA_HPG):
        out = out + jnp.where(head == h, o[h * tq:(h + 1) * tq], 0.0)
    return out


def _cmp_select_kernel(q_ref, k_ref, v_ref, ov_ref, o_ref, sel_ref, *, n_cmp, n_sel, q_pos0):
    tq = q_ref.shape[0]
    i = pl.program_id(2)
    qs = _stack_heads(q_ref[...] * (NSA_HD ** -0.5)).astype(BF16)
    s = lax.dot_general(qs, k_ref[...], (((1,), (1,)), ((), ())), preferred_element_type=F32)
    q_pos = q_pos0 + i * tq + lax.broadcasted_iota(jnp.int32, (tq, CMP_PAD), 0)
    c = lax.broadcasted_iota(jnp.int32, (tq, CMP_PAD), 1)
    mask1 = (c < n_cmp) & (c * CMP_STRIDE + (CMP_BLOCK - 1) <= q_pos)
    mask = jnp.concatenate([mask1] * NSA_HPG, axis=0)
    s = jnp.where(mask, s, NEG_BIG)
    m = jnp.max(s, axis=-1, keepdims=True)
    e = jnp.where(mask, jnp.exp(s - m), 0.0)
    den = jnp.sum(e, axis=-1, keepdims=True)
    prob = e / jnp.where(den > 0, den, 1.0)
    o = jnp.dot(prob.astype(BF16), v_ref[...], preferred_element_type=F32)
    o_ref[...] = _unstack_heads(o, tq)
    psum = prob[0:tq]
    for h in range(1, NSA_HPG):
        psum = psum + prob[h * tq:(h + 1) * tq]
    imp = jnp.dot(psum.astype(BF16), ov_ref[...], preferred_element_type=F32)
    qb = q_pos // SEL_BLOCK
    valid = (c <= qb) & (c < n_sel)
    forced = (c == 0) | (c == qb) | (c == qb - 1)
    score = jnp.where(valid, jnp.where(forced, FORCE_SCORE, imp), -jnp.inf)
    rank = jnp.zeros((tq, CMP_PAD), F32)
    for jp in range(n_sel):
        col = score[:, jp:jp + 1]
        beats = (col > score) | ((col == score) & (c > jp))
        rank = rank + jnp.where(beats, 1.0, 0.0)
    sel_ref[...] = jnp.where((rank < min(SEL_TOP, n_sel)) & (c < n_sel), 1.0, 0.0)


def nsa_cmp_select(qn, kc4, vc4, ovT, *, n_cmp, n_sel, q_pos0):
    B, T, _ = qn.shape
    tq = min(ATT_Q_TILE, T)
    return pl.pallas_call(
        functools.partial(_cmp_select_kernel, n_cmp=n_cmp, n_sel=n_sel, q_pos0=q_pos0),
        grid=(B, NSA_G, T // tq),
        in_specs=[
            pl.BlockSpec((None, tq, GROUP_W), lambda b, g, i: (b, i, g)),
            pl.BlockSpec((None, None, CMP_PAD, GROUP_W), lambda b, g, i: (b, g, 0, 0)),
            pl.BlockSpec((None, None, CMP_PAD, GROUP_W), lambda b, g, i: (b, g, 0, 0)),
            pl.BlockSpec((CMP_PAD, CMP_PAD), lambda b, g, i: (0, 0)),
        ],
        out_specs=[
            pl.BlockSpec((None, tq, GROUP_W), lambda b, g, i: (b, i, g)),
            pl.BlockSpec((None, None, tq, CMP_PAD), lambda b, g, i: (b, g, i, 0)),
        ],
        out_shape=[jax.ShapeDtypeStruct((B, T, NSA_W), F32),
                   jax.ShapeDtypeStruct((B, NSA_G, T, CMP_PAD), F32)],
        compiler_params=pltpu.CompilerParams(
            dimension_semantics=("parallel", "parallel", "parallel"), vmem_limit_bytes=VMEM_LIMIT_BYTES),
        name="nsa_cmp_select",
    )(qn, kc4, vc4, ovT)


def _flash_kernel(*refs, selected):
    if selected:
        q_ref, k_ref, v_ref, sel_ref, exp_ref, o_ref, m_ref, l_ref, acc_ref, s_a, s_b = refs
    else:
        q_ref, k_ref, v_ref, o_ref, m_ref, l_ref, acc_ref, s_a, s_b = refs
    tq = q_ref.shape[0]
    tk = ATT_K_TILE
    n_tiles = k_ref.shape[0] // tk
    i = pl.program_id(2)
    q = q_ref[...] * (NSA_HD ** -0.5)
    head = lax.broadcasted_iota(jnp.int32, q.shape, 1) // NSA_HD
    q4 = _stack_heads(q).astype(BF16)
    m_ref[...] = jnp.full(m_ref.shape, NEG_BIG, F32)
    l_ref[...] = jnp.zeros(l_ref.shape, F32)
    acc_ref[...] = jnp.zeros(acc_ref.shape, F32)
    q_pos = i * tq + lax.broadcasted_iota(jnp.int32, (tq, tk), 0)
    col = lax.broadcasted_iota(jnp.int32, (tq, tk), 1)
    if selected:
        sel = sel_ref[...].astype(BF16)
        lo = 0
    else:
        lo = jnp.maximum(i * tq - (WINDOW - 1), 0) // tk
    hi = (i * tq + tq - 1) // tk + 1

    def tile_start(j):
        return pl.multiple_of(jnp.minimum(j, n_tiles - 1) * tk, tk)

    def scores(j, s_ref):
        s_ref[...] = lax.dot_general(q4, k_ref[pl.ds(tile_start(j), tk), :], (((1,), (1,)), ((), ())),
                                     preferred_element_type=F32)

    def consume(j, s_ref):
        v = v_ref[pl.ds(tile_start(j), tk), :]
        k_pos = j * tk + col
        mask = k_pos <= q_pos
        if selected:
            mask = mask & (jnp.dot(sel, exp_ref[jnp.minimum(j, n_tiles - 1)], preferred_element_type=F32) > 0.5)
        else:
            mask = mask & (q_pos - k_pos < WINDOW)
        for h in range(NSA_HPG):
            s = jnp.where(mask, s_ref[h * tq:(h + 1) * tq, :], NEG_BIG)
            m_old = m_ref[h]
            m_new = jnp.maximum(m_old, jnp.max(s, axis=-1, keepdims=True))
            alpha = jnp.exp(m_old - m_new)
            p = jnp.where(mask, jnp.exp(s - pltpu.repeat(m_new, tk // LANE, axis=1)), 0.0)
            l_ref[h] = alpha * l_ref[h] + jnp.sum(p, axis=-1, keepdims=True)
            acc_ref[h] = (pltpu.repeat(alpha, GROUP_W // LANE, axis=1) * acc_ref[h]
                          + jnp.dot(p.astype(BF16), v, preferred_element_type=F32))
            m_ref[h] = m_new

    scores(lo, s_a)

    def body(t, carry):
        j = lo + 2 * t
        scores(j + 1, s_b)
        consume(j, s_a)
        scores(j + 2, s_a)
        consume(j + 1, s_b)
        return carry

    lax.fori_loop(0, (hi - lo + 1) // 2, body, 0)
    out = jnp.zeros((tq, GROUP_W), F32)
    for h in range(NSA_HPG):
        den = pltpu.repeat(l_ref[h], GROUP_W // LANE, axis=1)
        out = out + jnp.where(head == h, acc_ref[h] / jnp.where(den > 0, den, 1.0), 0.0)
    o_ref[...] = out


def nsa_flash(qr, k4, v4, sel=None, expand=None):
    B, T, _ = qr.shape
    tq = ATT_Q_TILE
    selected = sel is not None
    in_specs = [
        pl.BlockSpec((None, tq, GROUP_W), lambda b, g, i: (b, i, g)),
        pl.BlockSpec((None, T, GROUP_W), lambda b, g, i: (b, 0, g)),
        pl.BlockSpec((None, T, GROUP_W), lambda b, g, i: (b, 0, g)),
    ]
    args = [qr, k4, v4]
    if selected:
        in_specs += [
            pl.BlockSpec((None, None, tq, CMP_PAD), lambda b, g, i: (b, g, i, 0)),
            pl.BlockSpec(expand.shape, lambda b, g, i: (0, 0, 0)),
        ]
        args += [sel, expand]
    return pl.pallas_call(
        functools.partial(_flash_kernel, selected=selected),
        grid=(B, NSA_G, T // tq),
        in_specs=in_specs,
        out_specs=pl.BlockSpec((None, tq, GROUP_W), lambda b, g, i: (b, i, g)),
        out_shape=jax.ShapeDtypeStruct((B, T, NSA_W), F32),
        scratch_shapes=[pltpu.VMEM((NSA_HPG, tq, LANE), F32), pltpu.VMEM((NSA_HPG, tq, LANE), F32),
                        pltpu.VMEM((NSA_HPG, tq, GROUP_W), F32),
                        pltpu.VMEM((NSA_HPG * tq, ATT_K_TILE), F32), pltpu.VMEM((NSA_HPG * tq, ATT_K_TILE), F32)],
        compiler_params=pltpu.CompilerParams(
            dimension_semantics=("parallel", "parallel", "parallel"), vmem_limit_bytes=VMEM_LIMIT_BYTES),
        name="nsa_flash_sel" if selected else "nsa_flash_win",
    )(*args)


def _tile_groups(x):
    B, T = x.shape[:2]
    return jnp.broadcast_to(x[:, :, :, None, :], (B, T, NSA_G, NSA_HPG, NSA_HD)).reshape(B, T, NSA_W).astype(BF16)


def _tile_cmp(x):
    B, n = x.shape[:2]
    x = jnp.pad(jnp.moveaxis(x, 1, 2), ((0, 0), (0, 0), (0, CMP_PAD - n), (0, 0)))
    return jnp.tile(x, (1, 1, 1, NSA_HPG)).astype(BF16)


def _overlap_T(n_cmp, n_sel):
    ov = np.zeros((CMP_PAD, CMP_PAD), np.float32)
    cs = np.arange(n_cmp) * CMP_STRIDE
    ss = np.arange(n_sel) * SEL_BLOCK
    o = np.minimum(cs[None] + CMP_BLOCK, ss[:, None] + SEL_BLOCK) - np.maximum(cs[None], ss[:, None])
    ov[:n_cmp, :n_sel] = (np.clip(o, 0, None) / CMP_BLOCK).T
    return jnp.asarray(ov, dtype=BF16)


def _sel_expand(T):
    t = np.arange(T)
    e = (np.arange(CMP_PAD)[:, None] == (t // SEL_BLOCK)[None, :]).astype(np.float32)
    return jnp.asarray(e.reshape(CMP_PAD, T // ATT_K_TILE, ATT_K_TILE).transpose(1, 0, 2), dtype=BF16)


KV_ROWS = KV_SLOTS * NSA_G * NSA_HD
SLOT_ROWS = NSA_G * NSA_HD
N_PAGES = PAST_LEN // PAGE_SIZE
DEC_N_CHUNK = (PAST_LEN + DEC_SEQ) // CMP_STRIDE
DEC_N_CMP = DEC_N_CHUNK - CMP_R + 1
DEC_N_SEL = -(-(PAST_LEN + DEC_SEQ) // SEL_BLOCK)
WIN_BUF = min(WINDOW, PAST_LEN)


def _softmax_rows(s, mask, s_new=None):
    s = jnp.where(mask, s, NEG_BIG)
    m = jnp.max(s, axis=-1, keepdims=True)
    if s_new is not None:
        m = jnp.maximum(m, s_new)
    e = jnp.where(mask, jnp.exp(s - m), 0.0)
    den = jnp.sum(e, axis=-1, keepdims=True)
    if s_new is None:
        return e, den
    e_new = jnp.exp(s_new - m)
    return e, e_new, den + e_new


def _dec_nsa_kernel(pt_ref, *refs):
    pages = refs[:N_PAGES]
    (win_ref, qn_ref, qr_ref, new_ref, gate_ref, w1_ref, b1_ref, w2_ref, b2_ref, kn_ref,
     ov_ref, exp_ref, grp_ref, o_ref, xt_ref, acc_ref) = refs[N_PAGES:]
    del pt_ref
    f32 = F32
    half = 2 * NSA_HD
    n_chunk = DEC_N_CHUNK

    for p in range(N_PAGES):
        for sg in range(4):
            xt_ref[sg, p * PAGE_SIZE:(p + 1) * PAGE_SIZE, :] = pages[p][sg * half:(sg + 1) * half, :].T

    lane_lo = lax.broadcasted_iota(jnp.int32, (n_chunk, 2 * half), 1) % half < NSA_HD
    lane_grp = lax.broadcasted_iota(jnp.int32, (n_chunk, SLOT_ROWS), 1) // NSA_HD
    cmp_rows = []
    for slot in range(2):
        for gp in range(2):
            for rp in range(CMP_STRIDE // 2):
                xr = jnp.concatenate(
                    [xt_ref[slot * 2 + gp, pl.ds(2 * rp + j, n_chunk, stride=CMP_STRIDE), :] for j in range(2)],
                    axis=1)
                xs = jnp.concatenate([jnp.where(lane_lo, xr, 0.0), jnp.where(lane_lo, 0.0, xr)],
                                     axis=0).astype(BF16)
                part = jnp.dot(xs, w1_ref[slot, rp], preferred_element_type=f32)
                rows = pl.ds(gp * 2 * n_chunk, 2 * n_chunk)
                if rp == 0:
                    acc_ref[rows, :] = part
                else:
                    acc_ref[rows, :] += part
        acc = acc_ref[...]
        pre = b1_ref[slot] + acc[:, :CMP_HID] + pltpu.roll(acc[:, CMP_HID:], NSA_G * n_chunk - 1, 0)
        out = jnp.dot(jax.nn.gelu(pre).astype(BF16), w2_ref[slot], preferred_element_type=f32) + b2_ref[slot]
        if slot == 0:
            out = _rms_rows(out, kn_ref[...])
        sel_rows = jnp.zeros((n_chunk, SLOT_ROWS), f32)
        for g in range(NSA_G):
            sel_rows = sel_rows + jnp.where(lane_grp == g, out[g * n_chunk:(g + 1) * n_chunk], 0.0)
        cmp_rows.append(sel_rows.astype(BF16))
    kc, vc = cmp_rows

    qn = qn_ref[...].astype(BF16)
    qr = qr_ref[...].astype(BF16)
    nt = (((1,), (1,)), ((), ()))
    c = lax.broadcasted_iota(jnp.int32, (NSA_H, CMP_PAD), 1)
    s = lax.dot_general(qn, kc, nt, preferred_element_type=f32)
    e, den = _softmax_rows(s, c < DEC_N_CMP)
    prob = e / jnp.where(den > 0, den, 1.0)
    o_cmp = jnp.dot(prob.astype(BF16), vc, preferred_element_type=f32)
    p_hi, p_mid = _split_bf16(prob)
    p_lo = (prob - p_hi.astype(f32) - p_mid.astype(f32)).astype(BF16)
    grp = grp_ref[...]
    psum = (jnp.dot(grp, p_hi, preferred_element_type=f32) + jnp.dot(grp, p_mid, preferred_element_type=f32)
            + jnp.dot(grp, p_lo, preferred_element_type=f32))
    imp = jnp.dot(psum.astype(BF16), ov_ref[...], preferred_element_type=f32)
    qb = (PAST_LEN + DEC_SEQ - 1) // SEL_BLOCK
    valid = c <= qb
    forced = (c == 0) | (c == qb) | (c == qb - 1)
    score = jnp.where(valid, jnp.where(forced, FORCE_SCORE, imp), -jnp.inf)
    rank = jnp.zeros((NSA_H, CMP_PAD), f32)
    for jp in range(DEC_N_SEL):
        col = score[:, jp:jp + 1]
        rank = rank + jnp.where((col > score) | ((col == score) & (c > jp)), 1.0, 0.0)
    sel = jnp.where((rank < min(SEL_TOP, DEC_N_SEL)) & (c < DEC_N_SEL), 1.0, 0.0).astype(BF16)

    new = new_ref[...]
    new_b = new.astype(BF16).astype(f32)
    qr_f = qr.astype(f32)
    s_pages = [jnp.dot(qr, pages[p][2 * SLOT_ROWS:3 * SLOT_ROWS, :].astype(BF16), preferred_element_type=f32)
               for p in range(N_PAGES)]
    s = jnp.concatenate(s_pages, axis=1)
    mask = jnp.dot(sel, exp_ref[...], preferred_element_type=f32) > 0.5
    s_new = jnp.sum(qr_f * new_b[0:1], axis=-1, keepdims=True)
    e, e_new, den = _softmax_rows(s, mask, s_new)
    e = e.astype(BF16)
    o_slc = e_new.astype(BF16).astype(f32) * new_b[1:2]
    for p in range(N_PAGES):
        o_slc = o_slc + lax.dot_general(e[:, p * PAGE_SIZE:(p + 1) * PAGE_SIZE],
                                        pages[p][3 * SLOT_ROWS:4 * SLOT_ROWS, :].astype(BF16), nt,
                                        preferred_element_type=f32)
    o_slc = o_slc / den

    s = jnp.dot(qr, win_ref[0:SLOT_ROWS, :].astype(BF16), preferred_element_type=f32)
    i_buf = lax.broadcasted_iota(jnp.int32, (NSA_H, WIN_BUF), 1)
    s_new = jnp.sum(qr_f * new_b[2:3], axis=-1, keepdims=True)
    e, e_new, den = _softmax_rows(s, WIN_BUF - i_buf < WINDOW, s_new)
    o_win = e_new.astype(BF16).astype(f32) * new_b[3:4] + lax.dot_general(
        e.astype(BF16), win_ref[SLOT_ROWS:2 * SLOT_ROWS, :].astype(BF16), nt, preferred_element_type=f32)
    o_win = o_win / den

    gates = gate_ref[...]
    o_ref[...] = gates[:, 0:1] * o_cmp + gates[:, 1:2] * o_slc + gates[:, 2:3] * o_win


def dec_nsa(page_table, cache_t, win_t, qn16, qr16, new_rows, gates, w1t, b1, w2t, b2t, kn, ovT, expand, grp):
    DB = qn16.shape[0]
    const = lambda shape: pl.BlockSpec(shape, lambda b, pt: (0,) * len(shape))
    per_b = lambda shape: pl.BlockSpec((None,) + shape, lambda b, pt: (b,) + (0,) * len(shape))
    page_specs = [pl.BlockSpec((None, KV_ROWS, PAGE_SIZE), functools.partial(lambda b, pt, p: (pt[b, p], 0, 0), p=p))
                  for p in range(N_PAGES)]
    in_specs = page_specs + [
        per_b((2 * SLOT_ROWS, WIN_BUF)), per_b((NSA_H, SLOT_ROWS)), per_b((NSA_H, SLOT_ROWS)),
        per_b((4, SLOT_ROWS)), per_b((NSA_H, 3)),
        const(w1t.shape), const(b1.shape), const(w2t.shape), const(b2t.shape), const(kn.shape),
        const(ovT.shape), const(expand.shape), const(grp.shape),
    ]
    grid_spec = pltpu.PrefetchScalarGridSpec(
        num_scalar_prefetch=1, grid=(DB,), in_specs=in_specs,
        out_specs=pl.BlockSpec((None, NSA_H, SLOT_ROWS), lambda b, pt: (b, 0, 0)),
        scratch_shapes=[pltpu.VMEM((4, PAST_LEN, 2 * NSA_HD), F32),
                        pltpu.VMEM((NSA_G * DEC_N_CHUNK, CMP_R * CMP_HID), F32)])
    return pl.pallas_call(
        _dec_nsa_kernel,
        grid_spec=grid_spec,
        out_shape=jax.ShapeDtypeStruct((DB, NSA_H, SLOT_ROWS), F32),
        compiler_params=pltpu.CompilerParams(
            dimension_semantics=("arbitrary",), vmem_limit_bytes=VMEM_LIMIT_BYTES),
        name="dec_nsa",
    )(page_table, *([cache_t] * N_PAGES), win_t, qn16, qr16, new_rows, gates, w1t, b1, w2t, b2t, kn, ovT, expand, grp)


def _dec_cmp_weights(w1, b1, w2, b2):
    w = w1.reshape(CMP_R, CMP_STRIDE // 2, 2, 1, NSA_HD, CMP_HID)
    w = jnp.broadcast_to(w, (CMP_R, CMP_STRIDE // 2, 2, 2, NSA_HD, CMP_HID))
    w = jnp.moveaxis(w, 0, 4).reshape(CMP_STRIDE // 2, 4 * NSA_HD, CMP_R * CMP_HID)
    return (w.astype(BF16), b1.reshape(1, CMP_HID), jnp.tile(w2, (1, NSA_G)).astype(BF16),
            jnp.tile(b2, NSA_G).reshape(1, SLOT_ROWS))


def _place_heads(q):
    own = (jnp.arange(NSA_H)[:, None] // NSA_HPG) == jnp.arange(NSA_G)[None, :]
    return jnp.where(own[None, :, :, None], q[:, :, None, :], 0.0).reshape(q.shape[0], NSA_H, SLOT_ROWS)


def _take_heads(o):
    o = o.reshape(o.shape[0], NSA_H, NSA_G, NSA_HD)
    return o[:, jnp.arange(NSA_H), jnp.arange(NSA_H) // NSA_HPG, :].reshape(o.shape[0], NSA_W)


WKV_C = 64
WKV_PAIR = 2 * RWKV_HD
WKV_T_TILE = 512
WKV_PAIRS_PER_STEP = 4


def _split_bf16(x):
    hi = x.astype(BF16)
    return hi, (x - hi.astype(F32)).astype(BF16)


def _dot3(a, b):
    a_hi, a_lo = _split_bf16(a)
    b_hi, b_lo = _split_bf16(b)
    return (jnp.dot(a_hi, b_hi, preferred_element_type=F32) + jnp.dot(a_hi, b_lo, preferred_element_type=F32)
            + jnp.dot(a_lo, b_hi, preferred_element_type=F32))


def _wkv_kernel(r_ref, lw_ref, k_ref, v_ref, a_ref, b_ref, s0_ref, y_ref, sT_ref, s_scr):
    C = WKV_C
    P = WKV_PAIR
    n_chunks = r_ref.shape[0] // C

    @pl.when(pl.program_id(2) == 0)
    def _():
        s_scr[...] = s0_ref[...]

    lo_lane = lax.broadcasted_iota(jnp.int32, (C, P), 1) < RWKV_HD
    row = lax.broadcasted_iota(jnp.int32, (2 * C, 2 * C), 0)
    col = lax.broadcasted_iota(jnp.int32, (2 * C, 2 * C), 1)
    same_head = (row // C) == (col // C)
    strict = same_head & (row > col)
    lower = same_head & (row >= col)
    eye = jnp.where(row == col, 1.0, 0.0)
    tril = jnp.where(lax.broadcasted_iota(jnp.int32, (C, C), 0) >= lax.broadcasted_iota(jnp.int32, (C, C), 1),
                     1.0, 0.0).astype(BF16)

    def stack(x):
        return jnp.concatenate([jnp.where(lo_lane, x, 0.0), jnp.where(lo_lane, 0.0, x)], axis=0)

    def chunk(c, carry):
        stages = [pair_chunk(c, q) for q in range(WKV_PAIRS_PER_STEP)]
        while stages:
            stages = [g for g in stages if next(g, True) is None]
        return carry

    def pair_chunk(c, q):
        sl = pl.ds(pl.multiple_of(c * C, C), C)
        lanes = slice(q * P, (q + 1) * P)
        r, lw, k, v, a, b = (ref[sl, lanes] for ref in (r_ref, lw_ref, k_ref, v_ref, a_ref, b_ref))
        lw_hi, lw_mid = _split_bf16(lw)
        lw_lo = (lw - lw_hi.astype(F32) - lw_mid.astype(F32)).astype(BF16)
        cs = (jnp.dot(tril, lw_hi, preferred_element_type=F32) + jnp.dot(tril, lw_mid, preferred_element_type=F32)
              + jnp.dot(tril, lw_lo, preferred_element_type=F32))
        yield
        g_inv = jnp.exp(-cs)
        g_end = jnp.exp(cs[C - 1:C, :] - cs)
        a2 = stack(a * jnp.exp(cs - lw))
        r2 = stack(r * jnp.exp(cs))
        b2 = stack(b * g_inv)
        k2 = stack(k * g_inv)
        v2 = stack(v)
        s_old = s_scr[q]
        ar = jnp.concatenate([a2, r2], axis=0).astype(BF16)
        bk = jnp.concatenate([b2, k2], axis=0).astype(BF16)
        nt = (((1,), (1,)), ((), ()))
        pp = lax.dot_general(ar, bk, nt, preferred_element_type=F32)
        from_state = lax.dot_general(ar, s_old.astype(BF16), nt, preferred_element_type=F32)
        yield
        l_ab = jnp.where(strict, pp[:2 * C, :2 * C], 0.0)
        l_ak = jnp.where(strict, pp[:2 * C, 2 * C:], 0.0)
        m_rb = jnp.where(lower, pp[2 * C:, :2 * C], 0.0)
        m_rk = jnp.where(lower, pp[2 * C:, 2 * C:], 0.0)
        v2b = v2.astype(BF16)
        rhs = from_state[:2 * C] + jnp.dot(l_ak.astype(BF16), v2b, preferred_element_type=F32)
        yield
        n = l_ab
        x = eye + n
        span = 2
        while span < C:
            n = _dot3(n, n)
            yield
            x = x + _dot3(n, x)
            yield
            span *= 2
        u2 = _dot3(x, rhs)
        yield
        uv = jnp.concatenate([u2, v2], axis=0).astype(BF16)
        y2 = from_state[2 * C:] + jnp.dot(jnp.concatenate([m_rb, m_rk], axis=1).astype(BF16), uv,
                                          preferred_element_type=F32)
        yield
        y_ref[sl, lanes] = y2[:C] + y2[C:]
        bk_end = jnp.concatenate([stack(b * g_end), stack(k * g_end)], axis=0).astype(BF16)
        s_scr[q] = s_old * jnp.exp(cs[C - 1:C, :]) + lax.dot_general(
            uv, bk_end, (((0,), (0,)), ((), ())), preferred_element_type=F32)

    lax.fori_loop(0, n_chunks, chunk, 0)

    @pl.when(pl.program_id(2) == pl.num_programs(2) - 1)
    def _():
        sT_ref[...] = s_scr[...]


def wkv7_chunked(r, lw, k, v, a, b, s0):
    B, T, W = r.shape
    n_pair = W // WKV_PAIR
    tt = min(WKV_T_TILE, T)
    s0p = s0.astype(F32).reshape(B, n_pair, 2, RWKV_HD, RWKV_HD)
    zero = jnp.zeros_like(s0p[:, :, 0])
    s0_bd = jnp.concatenate([jnp.concatenate([s0p[:, :, 0], zero], axis=-1),
                             jnp.concatenate([zero, s0p[:, :, 1]], axis=-1)], axis=-2)
    pps = WKV_PAIRS_PER_STEP
    seq = pl.BlockSpec((None, tt, pps * WKV_PAIR), lambda i, p, t: (i, t, p))
    st = pl.BlockSpec((None, pps, WKV_PAIR, WKV_PAIR), lambda i, p, t: (i, p, 0, 0))
    y, s_bd = pl.pallas_call(
        _wkv_kernel,
        grid=(B, n_pair // pps, T // tt),
        in_specs=[seq] * 6 + [st],
        out_specs=[seq, st],
        out_shape=[jax.ShapeDtypeStruct((B, T, W), F32),
                   jax.ShapeDtypeStruct((B, n_pair, WKV_PAIR, WKV_PAIR), F32)],
        scratch_shapes=[pltpu.VMEM((pps, WKV_PAIR, WKV_PAIR), F32)],
        compiler_params=pltpu.CompilerParams(
            dimension_semantics=("parallel", "parallel", "arbitrary"), vmem_limit_bytes=VMEM_LIMIT_BYTES),
        name="wkv7_chunked",
    )(r, lw, k, v, a, b, s0_bd)
    s_fin = jnp.stack([s_bd[:, :, :RWKV_HD, :RWKV_HD], s_bd[:, :, RWKV_HD:, RWKV_HD:]], axis=2)
    return y, s_fin.reshape(B, W // RWKV_HD, RWKV_HD, RWKV_HD)


AB_PAD = _round_up(AB_COLS, COL_TILE)
SHIFT_PAD = _round_up(SHIFT_W, LANE)
LORA_PAD = SHIFT_PAD - 3 * RWKV_W
EVEN_ROWS = 128
N_EVEN_PRE_OUT = 10


def _split3(x):
    hi = x.astype(BF16)
    r1 = x - hi.astype(F32)
    mid = r1.astype(BF16)
    return hi, mid, (r1 - mid.astype(F32)).astype(BF16)


def _dot_01(x, m):
    return sum(jnp.dot(part, m, preferred_element_type=F32) for part in _split3(x))


def _head_sum(x, red_ref, exp_ref):
    return _dot_01(_dot_01(x, red_ref[...]), exp_ref[...])


def _expm1(x):
    u = jnp.exp(x)
    d = u - 1.0
    log_u = jnp.where((d == 0.0) | (d == -1.0), 1.0, jnp.log(u))
    return jnp.where(d == 0.0, x, jnp.where(d == -1.0, -1.0, d * x / log_u))


def _even_pre_math(x_ref, prev, taps, prm, outs):
    (cw_ref, cb_ref, wa_ref, ba_ref, wx_ref, bx_ref, lam_ref, mu_ref, w0_ref, a0_ref, wl_ref,
     kk_ref, ka_ref, red_ref, exp_ref) = prm
    a_o, u_o, gate_o, r_o, lw_o, k_o, v_o, na_o, nb_o, g_o = outs
    t1, t2, t3 = taps
    xb = x_ref[:, 0:LRU_W]
    xc = cb_ref[...] + cw_ref[0:1] * t3 + cw_ref[1:2] * t2 + cw_ref[2:3] * t1 + cw_ref[3:4] * xb
    xcb = xc.astype(BF16)
    gate_r = jax.nn.sigmoid(jnp.dot(xcb, wa_ref[...], preferred_element_type=F32) + ba_ref[...])
    gate_i = jax.nn.sigmoid(jnp.dot(xcb, wx_ref[...], preferred_element_type=F32) + bx_ref[...])
    log_a = -LRU_C * gate_r * lam_ref[...]
    a_o[...] = jnp.exp(log_a)
    u_o[...] = jnp.sqrt(-_expm1(2.0 * log_a)) * (gate_i * xc)
    gate_o[...] = jax.nn.gelu(x_ref[:, LRU_W:2 * LRU_W])
    rw = x_ref[:, 2 * LRU_W:2 * LRU_W + SHIFT_PAD]
    rs = rw + mu_ref[...] * (prev - rw)
    r_o[...] = rs[:, 0:RWKV_W]
    k = rs[:, RWKV_W:2 * RWKV_W]
    v_o[...] = rs[:, 2 * RWKV_W:3 * RWKV_W]
    tail = rs[:, 3 * RWKV_W:]
    lane = lax.broadcasted_iota(jnp.int32, tail.shape, 1)
    act = jnp.where(lane < W_LORA, jnp.tanh(tail), jnp.where(lane < W_LORA + A_LORA, tail, jax.nn.sigmoid(tail)))
    z = jnp.dot(act.astype(BF16), wl_ref[...], preferred_element_type=F32)
    w_log = -jax.nn.softplus(-(w0_ref[...] + z[:, 0:RWKV_W])) - 0.5
    lw_o[...] = -jnp.exp(w_log)
    a_icl = jax.nn.sigmoid(a0_ref[...] + z[:, RWKV_W:2 * RWKV_W])
    g_o[...] = z[:, 2 * RWKV_W:]
    kk = k * kk_ref[...]
    kk = kk / jnp.maximum(jnp.sqrt(_head_sum(kk * kk, red_ref, exp_ref)), 1e-12)
    k_o[...] = k * (1.0 + (a_icl - 1.0) * ka_ref[...])
    na_o[...] = -kk
    nb_o[...] = kk * a_icl


def _even_pre_seq_kernel(x_ref, conv0_ref, shift0_ref, *refs):
    prm = refs[:15]
    outs = refs[15:15 + N_EVEN_PRE_OUT]
    conv_c, shift_c = refs[15 + N_EVEN_PRE_OUT:]
    rows = x_ref.shape[0]

    @pl.when(pl.program_id(1) == 0)
    def _():
        conv_c[...] = conv0_ref[...]
        shift_c[...] = shift0_ref[...]

    xb = x_ref[:, 0:LRU_W]
    row = lax.broadcasted_iota(jnp.int32, xb.shape, 0)
    taps = []
    for j in (1, 2, 3):
        tap = pltpu.roll(xb, j, 0)
        for i in range(j):
            tap = jnp.where(row == i, conv_c[8 - j + i:9 - j + i, :], tap)
        taps.append(tap)
    rw = x_ref[:, 2 * LRU_W:2 * LRU_W + SHIFT_PAD]
    row_w = lax.broadcasted_iota(jnp.int32, rw.shape, 0)
    prev = jnp.where(row_w == 0, shift_c[7:8, :], pltpu.roll(rw, 1, 0))
    _even_pre_math(x_ref, prev, taps, prm, outs)
    conv_c[...] = x_ref[rows - 8:rows, 0:LRU_W]
    shift_c[...] = x_ref[rows - 8:rows, 2 * LRU_W:2 * LRU_W + SHIFT_PAD]


def _even_pre_step_kernel(x_ref, prev_ref, t1_ref, t2_ref, t3_ref, *refs):
    _even_pre_math(x_ref, prev_ref[...], (t1_ref[...], t2_ref[...], t3_ref[...]), refs[:15], refs[15:])


def _even_params(p):
    def bd(w):
        eye = jnp.eye(LRU_BLOCKS, dtype=w.dtype)
        return (eye[:, None, :, None] * w[:, :, None, :]).reshape(LRU_W, LRU_W).astype(BF16)
    row = lambda v: v.reshape(1, -1).astype(F32)
    wl = jnp.zeros((LORA_PAD, 3 * RWKV_W), F32)
    wl = wl.at[0:W_LORA, 0:RWKV_W].set(p['w2'])
    wl = wl.at[W_LORA:W_LORA + A_LORA, RWKV_W:2 * RWKV_W].set(p['a2'])
    wl = wl.at[W_LORA + A_LORA:W_LORA + A_LORA + G_LORA, 2 * RWKV_W:].set(p['g2'])
    head = np.arange(RWKV_W) // RWKV_HD
    red = jnp.asarray(head[:, None] == np.arange(LANE)[None, :], dtype=BF16)
    mu = jnp.pad(p['mu'], (0, SHIFT_PAD - SHIFT_W))
    return [p['conv_w'].astype(F32), row(p['conv_b']), bd(p['wa']), row(p['ba']), bd(p['wx']), row(p['bx']),
            row(jax.nn.softplus(-p['lam'].astype(F32))), row(mu), row(p['w0']), row(p['a0']), wl.astype(BF16),
            row(p['k_k']), row(p['k_a']), red, red.T]


def _const_spec(a, n_grid):
    return pl.BlockSpec(a.shape, lambda *_: (0,) * a.ndim)


def even_pre_seq(proj, conv0, shift0, prm, B, T):
    tr = EVEN_ROWS
    nt = T // tr
    conv_pad = jnp.pad(conv0.astype(F32), ((0, 0), (8 - (CONV_W - 1), 0), (0, 0)))
    shift_pad = jnp.pad(shift0.astype(F32)[:, None, :], ((0, 0), (7, 0), (0, SHIFT_PAD - SHIFT_W)))
    out_spec = pl.BlockSpec((tr, LRU_W), lambda b, t: (b * nt + t, 0))
    return pl.pallas_call(
        _even_pre_seq_kernel,
        grid=(B, nt),
        in_specs=[pl.BlockSpec((tr, AB_PAD), lambda b, t: (b * nt + t, 0)),
                  pl.BlockSpec((None, 8, LRU_W), lambda b, t: (b, 0, 0)),
                  pl.BlockSpec((None, 8, SHIFT_PAD), lambda b, t: (b, 0, 0))] + [_const_spec(a, 2) for a in prm],
        out_specs=[out_spec] * N_EVEN_PRE_OUT,
        out_shape=[jax.ShapeDtypeStruct((B * T, LRU_W), F32)] * N_EVEN_PRE_OUT,
        scratch_shapes=[pltpu.VMEM((8, LRU_W), F32), pltpu.VMEM((8, SHIFT_PAD), F32)],
        compiler_params=pltpu.CompilerParams(
            dimension_semantics=("parallel", "arbitrary"), vmem_limit_bytes=VMEM_LIMIT_BYTES),
        name="even_pre_seq",
    )(proj, conv_pad, shift_pad, *prm)


def even_pre_step(proj, row0, conv0, shift0, prm):
    n = conv0.shape[0]
    shift_pad = jnp.pad(shift0.astype(F32), ((0, 0), (0, SHIFT_PAD - SHIFT_W)))
    taps = [conv0[:, CONV_W - 1 - j].astype(F32) for j in (1, 2, 3)]
    full = lambda w: pl.BlockSpec((n, w), lambda i: (0, 0))
    return pl.pallas_call(
        _even_pre_step_kernel,
        grid=(1,),
        in_specs=[pl.BlockSpec((n, AB_PAD), lambda i: (row0 // n, 0)), full(SHIFT_PAD)] + [full(LRU_W)] * 3
        + [_const_spec(a, 1) for a in prm],
        out_specs=[full(LRU_W)] * N_EVEN_PRE_OUT,
        out_shape=[jax.ShapeDtypeStruct((n, LRU_W), F32)] * N_EVEN_PRE_OUT,
        compiler_params=pltpu.CompilerParams(
            dimension_semantics=("arbitrary",), vmem_limit_bytes=VMEM_LIMIT_BYTES),
        name="even_pre_step",
    )(proj, shift_pad, *taps, *prm)


def _even_post_kernel(hs_ref, gate_ref, y_ref, r_ref, k_ref, v_ref, g_ref, lng_ref, lnb_ref, rk_ref,
                      red_ref, exp_ref, *rest):
    o_ref = rest[-1]
    y = y_ref[...]
    mu = _head_sum(y, red_ref, exp_ref) * (1.0 / RWKV_HD)
    d = y - mu
    var = _head_sum(d * d, red_ref, exp_ref) * (1.0 / RWKV_HD)
    yn = d * lax.rsqrt(var + 64e-5) * lng_ref[...] + lnb_ref[...]
    bonus = _head_sum(r_ref[...] * k_ref[...] * rk_ref[...], red_ref, exp_ref) * v_ref[...]
    o_ref[:, 0:LRU_W] = (hs_ref[...] * gate_ref[...]).astype(o_ref.dtype)
    o_ref[:, LRU_W:] = ((yn + bonus) * g_ref[...]).astype(o_ref.dtype)


def even_post(hs, gate, y, r, k, v, g, p, red, n_total, row0, prior=None):
    n = hs.shape[0]
    tr = EVEN_ROWS
    row = lambda a: a.reshape(1, -1).astype(F32)
    consts = [row(p['ln_g']), row(p['ln_b']), row(p['r_k']), red, red.T]
    seq = pl.BlockSpec((tr, LRU_W), lambda i: (i, 0))
    args = [hs, gate, y, r, k, v, g] + consts
    in_specs = [seq] * 7 + [_const_spec(a, 1) for a in consts]
    aliases = {}
    if prior is not None:
        args.append(prior)
        in_specs.append(pl.BlockSpec(memory_space=pl.ANY))
        aliases = {len(args) - 1: 0}
    return pl.pallas_call(
        _even_post_kernel,
        grid=(n // tr,),
        in_specs=in_specs,
        out_specs=pl.BlockSpec((tr, D_MODEL), lambda i: (row0 // tr + i, 0)),
        out_shape=jax.ShapeDtypeStruct((n_total, D_MODEL), BF16),
        input_output_aliases=aliases,
        compiler_params=pltpu.CompilerParams(
            dimension_semantics=("parallel",), vmem_limit_bytes=VMEM_LIMIT_BYTES),
        name="even_post",
    )(*args)


def _retention_kernel(q_ref, k_ref, v_ref, dm_ref, rd_ref, kd_ref, sd_ref, o_ref, s_out_ref, s_scr):
    C = q_ref.shape[0]

    @pl.when(pl.program_id(2) == 0)
    def _():
        s_scr[...] = jnp.zeros(s_scr.shape, F32)

    lo = lax.broadcasted_iota(jnp.int32, (C, 2 * RET_DK), 1) < RET_DK

    def stack(x):
        return jnp.concatenate([jnp.where(lo, x, 0.0), jnp.where(lo, 0.0, x)], axis=0)

    q2 = stack(q_ref[...]).astype(BF16)
    k2 = stack(k_ref[...])
    v2 = jnp.concatenate([v_ref[:, 0:RET_DV], v_ref[:, RET_DV:]], axis=0).astype(BF16)
    s = lax.dot_general(q2, k2.astype(BF16), (((1,), (1,)), ((), ())), preferred_element_type=F32) * dm_ref[...]
    s_old = s_scr[...]
    o2 = jnp.dot(s.astype(BF16), v2, preferred_element_type=F32) + jnp.dot(
        q2, s_old.astype(BF16), preferred_element_type=F32) * rd_ref[...]
    o_ref[:, 0:RET_DV] = o2[:C]
    o_ref[:, RET_DV:] = o2[C:]
    s_new = s_old * sd_ref[...] + lax.dot_general((k2 * kd_ref[...]).astype(BF16), v2, (((0,), (0,)), ((), ())),
                                                  preferred_element_type=F32)
    s_scr[...] = s_new

    @pl.when(pl.program_id(2) == pl.num_programs(2) - 1)
    def _():
        s_out_ref[...] = s_new


def retention_prompt_pallas(rq, rk, rv, B, T, v_col0=0):
    C = RET_CHUNK
    nc = T // C
    vb0 = v_col0 // (2 * RET_DV)
    f32 = F32
    lg = jnp.log1p(-jnp.exp2(-5.0 - jnp.arange(RET_H, dtype=f32))).reshape(RET_H // 2, 2)
    i = jnp.arange(C, dtype=f32)
    diff = i[:, None] - i[None, :]
    causal = diff >= 0
    dmask = jnp.where(causal, jnp.exp(jnp.where(causal, diff, 0.0)[None, None] * lg[:, :, None, None]), 0.0)
    zero = jnp.zeros_like(dmask[:, 0])
    dm = jnp.concatenate([jnp.concatenate([dmask[:, 0], zero], axis=-1),
                          jnp.concatenate([zero, dmask[:, 1]], axis=-1)], axis=-2)
    rows = lambda x, w: jnp.broadcast_to(x[:, :, :, None], x.shape + (w,)).reshape(RET_H // 2, -1, w)
    rd = rows(jnp.exp((i[None, None, :] + 1.0) * lg[:, :, None]), RET_DV)
    kd = rows(jnp.exp((C - 1.0 - i)[None, None, :] * lg[:, :, None]), 2 * RET_DK)
    sd = rows(jnp.broadcast_to(jnp.exp(C * lg)[:, :, None], (RET_H // 2, 2, RET_DK)), RET_DV)
    qk_spec = pl.BlockSpec((C, 2 * RET_DK), lambda b, p, c: (b * nc + c, p))
    v_spec = pl.BlockSpec((C, 2 * RET_DV), lambda b, p, c: (b * nc + c, vb0 + p))
    o_spec = pl.BlockSpec((C, 2 * RET_DV), lambda b, p, c: (b * nc + c, p))
    const = lambda a: pl.BlockSpec((None,) + a.shape[1:], lambda b, p, c: (p, 0, 0))
    o, s = pl.pallas_call(
        _retention_kernel,
        grid=(B, RET_H // 2, nc),
        in_specs=[qk_spec, qk_spec, v_spec, const(dm), const(rd), const(kd), const(sd)],
        out_specs=[o_spec, pl.BlockSpec((None, None, 2 * RET_DK, RET_DV), lambda b, p, c: (b, p, 0, 0))],
        out_shape=[jax.ShapeDtypeStruct((B * T, RET_W), f32),
                   jax.ShapeDtypeStruct((B, RET_H // 2, 2 * RET_DK, RET_DV), f32)],
        scratch_shapes=[pltpu.VMEM((2 * RET_DK, RET_DV), f32)],
        compiler_params=pltpu.CompilerParams(
            dimension_semantics=("parallel", "parallel", "arbitrary"), vmem_limit_bytes=VMEM_LIMIT_BYTES),
        name="retention_prompt",
    )(rq, rk, rv, dm, rd, kd, sd)
    return s.reshape(B, RET_H, RET_DK, RET_DV), o


KV_W = NSA_G * NSA_HD
RET_QK_W = RET_H * RET_DK
OFF_Q = 0
OFF_KC = OFF_Q + NSA_W
OFF_VC = OFF_KC + KV_W
OFF_KS = OFF_VC + KV_W
OFF_VS = OFF_KS + KV_W
OFF_KW = OFF_VS + KV_W
OFF_VW = OFF_KW + KV_W
OFF_RQ = OFF_VW + KV_W
OFF_RK = OFF_RQ + RET_QK_W
OFF_RV = OFF_RK + RET_QK_W
OFF_RG = OFF_RV + RET_W
OFF_GT = OFF_RG + RET_W
CD_PAD = _round_up(OFF_GT + LANE, COL_TILE)
ODD_ROWS = 128
N_ODD_PRE_OUT = 11


def _odd_weight_cols(w):
    gt0 = NSA_W + 6 * KV_W
    body = jnp.concatenate([w[:, :gt0], w[:, gt0 + 3 * NSA_H:]], axis=1)
    gt = w[:, gt0:gt0 + 3 * NSA_H]
    out = jnp.concatenate([body, gt], axis=1)
    return jnp.pad(out, ((0, 0), (0, CD_PAD - out.shape[1]))).astype(BF16)


def _rope_tables(pos, n_rot, theta, head):
    half = n_rot // 2
    inv = jnp.exp(-jnp.log(jnp.float32(theta)) * jnp.arange(half, dtype=jnp.float32) / half)
    ang = pos.astype(jnp.float32)[:, None] * inv[None, :]
    cos, sin = jnp.cos(ang), jnp.sin(ang)
    d = np.arange(LANE) % head
    cos_d, sin_d = cos[:, d % half], sin[:, d % half]
    c = jnp.where(d < n_rot, cos_d, 1.0)
    s1 = jnp.where(d < half, -sin_d, 0.0)
    s2 = jnp.where((d >= half) & (d < n_rot), sin_d, 0.0)
    return jnp.stack([c, s1, s2])


def _rope_lanes(x, tab_ref, half):
    w = x.shape[1]
    rep = w // LANE
    c, s1, s2 = (pltpu.repeat(tab_ref[i], rep, axis=1) for i in range(3))
    return x * c + pltpu.roll(x, w - half, 1) * s1 + pltpu.roll(x, half, 1) * s2


def _rms_heads(x, g_ref, red_ref, exp_ref):
    ms = _head_sum(x * x, red_ref, exp_ref) * (1.0 / NSA_HD)
    return x * lax.rsqrt(ms + 1e-6) * g_ref[...]


def _odd_pre_kernel(x_ref, nsa_tab, ret_tab, qg_ref, ksg_ref, kwg_ref, redq_ref, expq_ref, redk_ref, expk_ref,
                    tile_ref, qn_o, qr_o, ks_o, kw_o, ks4_o, vs4_o, kw4_o, vw4_o, gate_o, rq_o, rk_o):
    nsa_half = ROPE_DIMS // 2
    qn = _rms_heads(x_ref[:, OFF_Q:OFF_Q + NSA_W], qg_ref, redq_ref, expq_ref)
    qn_o[...] = qn
    qr_o[...] = _rope_lanes(qn, nsa_tab, nsa_half)
    ks = _rope_lanes(_rms_heads(x_ref[:, OFF_KS:OFF_KS + KV_W], ksg_ref, redk_ref, expk_ref), nsa_tab, nsa_half)
    kw = _rope_lanes(_rms_heads(x_ref[:, OFF_KW:OFF_KW + KV_W], kwg_ref, redk_ref, expk_ref), nsa_tab, nsa_half)
    ks_o[...] = ks
    kw_o[...] = kw
    tile = tile_ref[...]
    for src, dst in ((ks, ks4_o), (x_ref[:, OFF_VS:OFF_VS + KV_W], vs4_o), (kw, kw4_o),
                     (x_ref[:, OFF_VW:OFF_VW + KV_W], vw4_o)):
        dst[...] = jnp.dot(src.astype(BF16), tile, preferred_element_type=F32).astype(BF16)
    gate_o[...] = jax.nn.sigmoid(x_ref[:, OFF_GT:OFF_GT + LANE])
    rq_o[...] = _rope_lanes(x_ref[:, OFF_RQ:OFF_RQ + RET_QK_W], ret_tab, RET_DK // 2)
    rk_o[...] = _rope_lanes(x_ref[:, OFF_RK:OFF_RK + RET_QK_W], ret_tab, RET_DK // 2) * (RET_DK ** -0.5)


def odd_pre(proj, pos, p, row0, n_rows, same_pos):
    tr = ODD_ROWS
    blk0 = row0 // tr
    n_tab = tr if same_pos else n_rows
    pos_rows = jnp.broadcast_to(pos, (n_tab,)) if same_pos else pos
    nsa_tab = _rope_tables(pos_rows, ROPE_DIMS, ROPE_THETA, NSA_HD)
    ret_tab = _rope_tables(pos_rows, RET_DK, RET_THETA, RET_DK)
    row = lambda v, rep: jnp.tile(v.astype(F32), rep).reshape(1, -1)
    lanes = np.arange(LANE)
    red_q = jnp.asarray((np.arange(NSA_W) // NSA_HD)[:, None] == lanes[None, :], dtype=BF16)
    red_k = jnp.asarray((np.arange(KV_W) // NSA_HD)[:, None] == lanes[None, :], dtype=BF16)
    src = np.arange(KV_W)
    dst = np.arange(NSA_W)
    tile = jnp.asarray((src[:, None] // NSA_HD == dst[None, :] // GROUP_W)
                       & (src[:, None] % NSA_HD == dst[None, :] % NSA_HD), dtype=BF16)
    consts = [row(p['q_norm'], NSA_H), row(p['k_norm'][1], NSA_G), row(p['k_norm'][2], NSA_G),
              red_q, red_q.T, red_k, red_k.T, tile]
    tab_spec = pl.BlockSpec((3, tr, LANE), (lambda i: (0, 0, 0)) if same_pos else (lambda i: (0, i, 0)))
    out = lambda w, dt: (pl.BlockSpec((tr, w), lambda i: (i, 0)), jax.ShapeDtypeStruct((n_rows, w), dt))
    outs = [out(NSA_W, F32), out(NSA_W, F32), out(KV_W, F32), out(KV_W, F32)] + [out(NSA_W, BF16)] * 4 + [
        out(LANE, F32), out(RET_QK_W, F32), out(RET_QK_W, F32)]
    return pl.pallas_call(
        _odd_pre_kernel,
        grid=(n_rows // tr,),
        in_specs=[pl.BlockSpec((tr, CD_PAD), lambda i: (blk0 + i, 0)), tab_spec, tab_spec]
        + [_const_spec(a, 1) for a in consts],
        out_specs=[o[0] for o in outs],
        out_shape=[o[1] for o in outs],
        compiler_params=pltpu.CompilerParams(
            dimension_semantics=("parallel",), vmem_limit_bytes=VMEM_LIMIT_BYTES),
        name="odd_pre",
    )(proj, nsa_tab, ret_tab, *consts)


def _odd_post_kernel(oc_ref, os_ref, ow_ref, gate_ref, ret_ref, rg0_ref, rg1_ref, gng_ref, gnb_ref, ge_ref, *rest,
                     gated):
    o_ref = rest[-1]
    if gated:
        nsa = oc_ref[...]
    else:
        gates = gate_ref[...]
        nsa = jnp.zeros(oc_ref.shape, F32)
        for j, branch in enumerate((oc_ref, os_ref, ow_ref)):
            nsa = nsa + _dot_01(gates, ge_ref[j]) * branch[...]
    o_ref[:, 0:NSA_W] = nsa.astype(o_ref.dtype)
    for h in range(RET_H):
        lanes = slice(h * RET_DV, (h + 1) * RET_DV)
        x = ret_ref[:, lanes]
        mu = jnp.mean(x, axis=-1, keepdims=True)
        d = x - mu
        var = jnp.mean(d * d, axis=-1, keepdims=True)
        yn = d * lax.rsqrt(var + 1e-5) * gng_ref[:, lanes] + gnb_ref[:, lanes]
        rg = (rg0_ref if h < RET_H // 2 else rg1_ref)[:, (h % (RET_H // 2)) * RET_DV:(h % (RET_H // 2) + 1) * RET_DV]
        o_ref[:, NSA_W + h * RET_DV:NSA_W + (h + 1) * RET_DV] = (yn * (rg * jax.nn.sigmoid(rg))).astype(o_ref.dtype)


def odd_post(o_cmp, o_slc, o_win, gates, o_ret, proj, p, n_total, row0, prior=None, gated=False):
    n = o_cmp.shape[0]
    tr = ODD_ROWS
    blk0 = row0 // tr
    h = np.arange(NSA_W) // NSA_HD
    ge = jnp.asarray(np.stack([(np.arange(LANE)[:, None] == (3 * h + j)[None, :]) for j in range(3)]), dtype=BF16)
    row = lambda a: a.reshape(1, -1).astype(F32)
    consts = [row(p['gn_g']), row(p['gn_b']), ge]
    seq = lambda w: pl.BlockSpec((tr, w), lambda i: (i, 0))
    half = RET_W // 2
    rg_spec = lambda k: pl.BlockSpec((tr, half), lambda i: (blk0 + i, OFF_RG // half + k))
    args = [o_cmp, o_slc, o_win, gates, o_ret, proj, proj] + consts
    in_specs = [seq(NSA_W)] * 3 + [seq(LANE), seq(RET_W), rg_spec(0), rg_spec(1)] + [_const_spec(a, 1) for a in consts]
    aliases = {}
    if prior is not None:
        args.append(prior)
        in_specs.append(pl.BlockSpec(memory_space=pl.ANY))
        aliases = {len(args) - 1: 0}
    return pl.pallas_call(
        functools.partial(_odd_post_kernel, gated=gated),
        grid=(n // tr,),
        in_specs=in_specs,
        out_specs=pl.BlockSpec((tr, D_MODEL), lambda i: (blk0 + i, 0)),
        out_shape=jax.ShapeDtypeStruct((n_total, D_MODEL), BF16),
        input_output_aliases=aliases,
        compiler_params=pltpu.CompilerParams(
            dimension_semantics=("parallel",), vmem_limit_bytes=VMEM_LIMIT_BYTES),
        name="odd_post",
    )(*args)


def rms_norm(x, g, eps=1e-6):
    xf = x.astype(jnp.float32)
    y = xf * lax.rsqrt(jnp.mean(xf * xf, axis=-1, keepdims=True) + eps)
    return (y * g.astype(jnp.float32)).astype(x.dtype)


def head_group_norm(y, g, b, eps):
    yf = y.astype(jnp.float32)
    mu = jnp.mean(yf, axis=-1, keepdims=True)
    var = jnp.mean(jnp.square(yf - mu), axis=-1, keepdims=True)
    yn = ((yf - mu) * lax.rsqrt(var + eps)).reshape(y.shape[:-2] + (-1,))
    return (yn * g.astype(jnp.float32) + b.astype(jnp.float32)).astype(y.dtype)


def masked_softmax(s, mask):
    s = jnp.where(mask, s.astype(jnp.float32), -jnp.inf)
    m = jnp.max(s, axis=-1, keepdims=True)
    e = jnp.exp(s - jnp.where(jnp.isfinite(m), m, 0.0))
    den = jnp.sum(e, axis=-1, keepdims=True)
    return e / jnp.where(den > 0, den, 1.0)


def rope(x, pos, n_rot, theta):
    half = n_rot // 2
    inv = jnp.exp(-jnp.log(jnp.float32(theta)) * jnp.arange(half, dtype=jnp.float32) / half)
    ang = pos.astype(jnp.float32)[:, None] * inv[None, :]
    cos = jnp.cos(ang)[None, :, None, :]
    sin = jnp.sin(ang)[None, :, None, :]
    xf = x.astype(jnp.float32)
    x1, x2 = xf[..., :half], xf[..., half:n_rot]
    out = jnp.concatenate([x1 * cos - x2 * sin, x2 * cos + x1 * sin, xf[..., n_rot:]], axis=-1)
    return out.astype(x.dtype)


def linear_scan(a, b, h0):
    b = b.at[:, 0].add(a[:, 0] * h0)

    def combine(left, right):
        return left[0] * right[0], right[0] * left[1] + right[1]

    return lax.associative_scan(combine, (a, b), axis=1)[1]


def wkv7_scan(r, w, k, v, a, b, s0):
    xs = tuple(jnp.moveaxis(z.astype(jnp.float32), 1, 0) for z in (r, w, k, v, a, b))

    def step(S, inp):
        r_t, w_t, k_t, v_t, a_t, b_t = inp
        sa = jnp.einsum('bhij,bhj->bhi', S, a_t)
        S = S * w_t[:, :, None, :] + sa[..., None] * b_t[:, :, None, :] + v_t[..., None] * k_t[:, :, None, :]
        return S, jnp.einsum('bhij,bhj->bhi', S, r_t)

    S, ys = lax.scan(step, s0.astype(jnp.float32), xs)
    return jnp.moveaxis(ys, 0, 1), S


def even_mixer_core(proj, p, lru_h0, lru_conv0, shift0, wkv0):
    B, T, _ = proj.shape
    f32 = jnp.float32
    dt = proj.dtype
    xb, gb, rw = jnp.split(proj, [LRU_W, 2 * LRU_W], axis=-1)
    xcat = jnp.concatenate([lru_conv0.astype(dt), xb], axis=1)
    xc = p['conv_b'] + sum(p['conv_w'][j] * xcat[:, j:j + T] for j in range(CONV_W))
    xbd = xc.reshape(B, T, LRU_BLOCKS, LRU_BS)
    gate_r = jax.nn.sigmoid(jnp.einsum('btnc,ncd->btnd', xbd, p['wa']).reshape(B, T, LRU_W) + p['ba'])
    gate_i = jax.nn.sigmoid(jnp.einsum('btnc,ncd->btnd', xbd, p['wx']).reshape(B, T, LRU_W) + p['bx'])
    log_a = -LRU_C * gate_r.astype(f32) * jax.nn.softplus(-p['lam'].astype(f32))
    u = jnp.sqrt(-jnp.expm1(2.0 * log_a)) * (gate_i * xc).astype(f32)
    hs = lru_scan(jnp.exp(log_a), u, lru_h0.astype(f32))
    y_lru = hs.astype(dt) * jax.nn.gelu(gb)
    prev = jnp.concatenate([shift0.astype(dt)[:, None], rw[:, :-1]], axis=1)
    rs = rw + p['mu'] * (prev - rw)
    r, k, v, xw, xa, xg = jnp.split(
        rs, [RWKV_W, 2 * RWKV_W, 3 * RWKV_W, 3 * RWKV_W + W_LORA, 3 * RWKV_W + W_LORA + A_LORA], axis=-1)
    w_log = -jax.nn.softplus(-(p['w0'] + jnp.tanh(xw) @ p['w2']).astype(f32)) - 0.5
    log_decay = -jnp.exp(w_log)
    decay = jnp.exp(log_decay)
    a_icl = jax.nn.sigmoid(p['a0'] + xa @ p['a2'])
    g = jax.nn.sigmoid(xg) @ p['g2']
    heads = (B, T, RWKV_H, RWKV_HD)
    kk = (k * p['k_k']).reshape(heads).astype(f32)
    kk = kk / jnp.maximum(jnp.sqrt(jnp.sum(kk * kk, axis=-1, keepdims=True)), 1e-12)
    k = k * (1.0 + (a_icl - 1.0) * p['k_a'])
    rh, kh, vh, ah = (z.reshape(heads) for z in (r, k, v, a_icl))
    if T % WKV_C == 0:
        y, wkv = wkv7_chunked(r.astype(f32), log_decay, k.astype(f32), v.astype(f32),
                              (-kk).reshape(B, T, RWKV_W), (kk * ah.astype(f32)).reshape(B, T, RWKV_W), wkv0)
        y = y.reshape(heads)
    else:
        y, wkv = wkv7_scan(rh, decay.reshape(heads), kh, vh, -kk, kk * ah.astype(f32), wkv0)
    y = head_group_norm(y, p['ln_g'], p['ln_b'], 64e-5).astype(dt)
    bonus = (jnp.sum(rh * kh * p['r_k'], axis=-1, keepdims=True) * vh).reshape(B, T, RWKV_W)
    y_rwkv = (y + bonus) * g
    cat = jnp.concatenate([y_lru, y_rwkv], axis=-1)
    return cat, hs[:, -1], xcat[:, T:], rw[:, -1], wkv


def even_mixer(proj, p, B, T, DB, lru_h0, lru_conv0, shift0, wkv0):
    f32 = F32
    prm = _even_params(p)
    red = prm[-2]
    n_p = B * T
    zeros = lambda *s: jnp.zeros(s, f32)
    a, u, gate, r, lw, k, v, na, nb, g = even_pre_seq(proj, zeros(B, CONV_W - 1, LRU_W), zeros(B, SHIFT_W), prm, B, T)
    seq = lambda z: z.reshape(B, T, LRU_W)
    hs = lru_scan(seq(a), seq(u), zeros(B, LRU_W))
    yw, wkv_p = wkv7_chunked(seq(r), seq(lw), seq(k), seq(v), seq(na), seq(nb), zeros(B, RWKV_H, RWKV_HD, RWKV_HD))
    cat = even_post(hs.reshape(n_p, LRU_W), gate, yw.reshape(n_p, RWKV_W), r, k, v, g, p, red, n_p + DB, 0)
    last = jnp.arange(B)[:, None] * T + (T - (CONV_W - 1) + jnp.arange(CONV_W - 1))[None, :]
    st_p = (hs[:, -1], proj[last, :LRU_W], proj[last[:, -1], 2 * LRU_W:AB_COLS], wkv_p)
    a, u, gate, r, lw, k, v, na, nb, g = even_pre_step(proj, n_p, lru_conv0, shift0, prm)
    hs_s = a * lru_h0.astype(f32) + u
    heads = (DB, 1, RWKV_H, RWKV_HD)
    yw, wkv_s = wkv7_scan(r.reshape(heads), jnp.exp(lw).reshape(heads), k.reshape(heads), v.reshape(heads),
                          na.reshape(heads), nb.reshape(heads), wkv0)
    cat = even_post(hs_s, gate, yw.reshape(DB, RWKV_W), r, k, v, g, p, red, n_p + DB, n_p, prior=cat)
    xb_s = proj[n_p:]
    conv_s = jnp.concatenate([lru_conv0[:, 1:].astype(f32), xb_s[:, None, :LRU_W]], axis=1)
    st_s = (hs_s, conv_s, xb_s[:, 2 * LRU_W:AB_COLS], wkv_s)
    return cat, st_p, st_s


def odd_project(proj, p, pos):
    B, T, _ = proj.shape
    sizes = [NSA_W] + [NSA_G * NSA_HD] * 6 + [3 * NSA_H, RET_H * RET_DK, RET_H * RET_DK, RET_W, RET_W]
    q, kc, vc, ks, vs, kw, vw, gt, rq, rk, rv, rg = jnp.split(
        proj, np.cumsum(sizes).tolist(), axis=-1)[:len(sizes)]
    kvs = (B, T, NSA_G, NSA_HD)
    q_n = rms_norm(q.reshape(B, T, NSA_H, NSA_HD), p['q_norm'])
    return {
        'q_n': q_n,
        'q_r': rope(q_n, pos, ROPE_DIMS, ROPE_THETA),
        'kc': kc.reshape(kvs), 'vc': vc.reshape(kvs),
        'ks': rope(rms_norm(ks.reshape(kvs), p['k_norm'][1]), pos, ROPE_DIMS, ROPE_THETA),
        'vs': vs.reshape(kvs),
        'kw': rope(rms_norm(kw.reshape(kvs), p['k_norm'][2]), pos, ROPE_DIMS, ROPE_THETA),
        'vw': vw.reshape(kvs),
        'gates': jax.nn.sigmoid(gt).reshape(B, T, NSA_H, 3),
        'rq': rope(rq.reshape(B, T, RET_H, RET_DK), pos, RET_DK, RET_THETA),
        'rk': rope(rk.reshape(B, T, RET_H, RET_DK), pos, RET_DK, RET_THETA) * (RET_DK ** -0.5),
        'rv': rv.reshape(B, T, RET_H, RET_DV),
        'rg': rg,
    }


def to_groups_q(q):
    B, T = q.shape[:2]
    return jnp.moveaxis(q.reshape(B, T, NSA_G, NSA_HPG, NSA_HD), 1, 3)


def to_groups_k(k):
    return jnp.moveaxis(k, 1, 2)


def nsa_compress(x, w1, b1, w2, b2):
    B, L = x.shape[:2]
    n_chunk = L // CMP_STRIDE
    n_cmp = n_chunk - CMP_R + 1
    ch = x[:, :n_chunk * CMP_STRIDE].reshape(B, n_chunk, CMP_STRIDE, NSA_G, NSA_HD)
    ch = jnp.moveaxis(ch, 3, 2).reshape(B, n_chunk, NSA_G, CMP_STRIDE * NSA_HD)
    part = jnp.einsum('bngc,rch->bngrh', ch, w1)
    pre = b1 + sum(part[:, m:m + n_cmp, :, m] for m in range(CMP_R))
    return jax.nn.gelu(pre) @ w2 + b2


def nsa_compressed_branch(qn, kc_raw, vc_raw, p, q_pos):
    kc = to_groups_k(rms_norm(nsa_compress(kc_raw, *p['ck']), p['k_norm'][0]))
    vc = to_groups_k(nsa_compress(vc_raw, *p['cv']))
    s = jnp.einsum('bghqd,bgcd->bghqc', qn, kc) * NSA_HD ** -0.5
    ends = jnp.arange(kc.shape[2]) * CMP_STRIDE + CMP_BLOCK - 1
    prob = masked_softmax(s, ends[None, :] <= q_pos[:, None])
    return jnp.einsum('bghqc,bgcd->bghqd', prob.astype(vc.dtype), vc), prob


def cmp_sel_overlap(n_cmp, n_sel):
    cs = np.arange(n_cmp) * CMP_STRIDE
    ss = np.arange(n_sel) * SEL_BLOCK
    ov = np.minimum(cs[None] + CMP_BLOCK, ss[:, None] + SEL_BLOCK) - np.maximum(cs[None], ss[:, None])
    return jnp.asarray(np.clip(ov, 0, None) / CMP_BLOCK, dtype=jnp.float32)


def nsa_select(p_cmp, q_pos, n_sel):
    imp = jnp.einsum('bgqc,sc->bgqs', p_cmp.sum(axis=2), cmp_sel_overlap(p_cmp.shape[-1], n_sel))
    j = jnp.arange(n_sel)[None, :]
    qb = (q_pos // SEL_BLOCK)[:, None]
    valid = j <= qb
    forced = (j == 0) | (j == qb) | (j == qb - 1)
    score = jnp.where(valid, jnp.where(forced, FORCE_SCORE, imp), -jnp.inf)
    _, idx = lax.top_k(score, min(SEL_TOP, n_sel))
    sel_ok = jnp.take_along_axis(jnp.broadcast_to(valid, score.shape), idx, axis=-1)
    return idx, sel_ok


def sel_blocks(x, n_sel):
    B, L = x.shape[:2]
    x = jnp.pad(x, ((0, 0), (0, n_sel * SEL_BLOCK - L), (0, 0), (0, 0)))
    return jnp.moveaxis(x.reshape(B, n_sel, SEL_BLOCK, NSA_G, NSA_HD), 3, 1)


def nsa_slc_attend(q, kb, vb, idx, sel_ok, q_pos):
    B, G = kb.shape[:2]
    bi = jnp.arange(B)[:, None, None, None]
    gi = jnp.arange(G)[None, :, None, None]
    kg = kb[bi, gi, idx]
    vg = vb[bi, gi, idx]
    s = jnp.einsum('bghqd,bgqnld->bghqnl', q, kg) * NSA_HD ** -0.5
    kpos = idx[..., None] * SEL_BLOCK + jnp.arange(SEL_BLOCK)
    mask = (kpos <= q_pos[None, None, :, None, None]) & sel_ok[..., None]
    sh = s.shape
    prob = masked_softmax(s.reshape(sh[:4] + (-1,)), mask.reshape(B, G, 1, sh[3], -1))
    return jnp.einsum('bghqnl,bgqnld->bghqd', prob.reshape(sh).astype(vg.dtype), vg)


def window_attend_banded(q, k, v):
    B, G, HPG, T, HD = q.shape
    nb = T // WIN_BLOCK
    npv = WINDOW // WIN_BLOCK
    pad = ((0, 0), (0, 0), (npv * WIN_BLOCK, 0), (0, 0))

    def band(z):
        zb = jnp.pad(z, pad).reshape(B, G, nb + npv, WIN_BLOCK, HD)
        return jnp.concatenate([zb[:, :, j:j + nb] for j in range(npv + 1)], axis=3)

    kb, vb = band(k), band(v)
    qb = q.reshape(B, G, HPG, nb, WIN_BLOCK, HD)
    s = jnp.einsum('bghiqd,bgikd->bghiqk', qb, kb) * NSA_HD ** -0.5
    blk = jnp.arange(nb)[:, None]
    q_pos = blk * WIN_BLOCK + jnp.arange(WIN_BLOCK)[None]
    k_pos = (blk - npv) * WIN_BLOCK + jnp.arange((npv + 1) * WIN_BLOCK)[None]
    diff = q_pos[:, :, None] - k_pos[:, None, :]
    mask = (diff >= 0) & (diff < WINDOW) & (k_pos[:, None, :] >= 0)
    prob = masked_softmax(s, mask)
    return jnp.einsum('bghiqk,bgikd->bghiqd', prob.astype(v.dtype), vb).reshape(B, G, HPG, T, HD)


def window_attend_cached(q, k, v, q_pos, k_pos):
    s = jnp.einsum('bghqd,blgd->bghql', q, k) * NSA_HD ** -0.5
    diff = q_pos[:, None] - k_pos[None, :]
    prob = masked_softmax(s, (diff >= 0) & (diff < WINDOW))
    return jnp.einsum('bghql,blgd->bghqd', prob.astype(v.dtype), v)


def retention_chunk(S, q, k, v):
    f32 = jnp.float32
    C = q.shape[1]
    lg = jnp.log1p(-jnp.exp2(-5.0 - jnp.arange(RET_H, dtype=f32)))
    i = jnp.arange(C, dtype=f32)
    diff = i[:, None] - i[None, :]
    causal = diff >= 0
    dmask = jnp.where(causal, jnp.exp(jnp.where(causal, diff, 0.0)[None] * lg[:, None, None]), 0.0)
    qf, kf, vf = q.astype(f32), k.astype(f32), v.astype(f32)
    s = jnp.einsum('bihd,bjhd->bhij', qf, kf) * dmask
    o = jnp.einsum('bhij,bjhe->bihe', s, vf)
    o = o + jnp.einsum('bihd,bhde->bihe', qf, S) * jnp.exp((i[:, None] + 1.0) * lg[None, :])[None, :, :, None]
    k_dec = kf * jnp.exp((C - 1.0 - i)[:, None] * lg[None, :])[None, :, :, None]
    S = S * jnp.exp(C * lg)[None, :, None, None] + jnp.einsum('bjhd,bjhe->bhde', k_dec, vf)
    return S, o


def retention_prompt(q, k, v):
    B, T = q.shape[:2]
    n = T // RET_CHUNK
    xs = tuple(jnp.moveaxis(z.reshape((B, n, RET_CHUNK) + z.shape[2:]), 1, 0) for z in (q, k, v))
    s0 = jnp.zeros((B, RET_H, RET_DK, RET_DV), jnp.float32)
    S, o = lax.scan(lambda S, c: retention_chunk(S, c[0], c[1], c[2]), s0, xs)
    return S, jnp.moveaxis(o, 0, 1).reshape(B, T, RET_H, RET_DV)


def odd_output(o_cmp, o_slc, o_win, o_ret, pr, p):
    gates = pr['gates']
    B, T = gates.shape[:2]
    gg = jnp.moveaxis(gates.reshape(B, T, NSA_G, NSA_HPG, 3), 1, 3)[..., None]
    o = gg[..., 0, :] * o_cmp + gg[..., 1, :] * o_slc + gg[..., 2, :] * o_win
    o_nsa = jnp.moveaxis(o, 3, 1).reshape(B, T, NSA_W)
    y_ret = head_group_norm(o_ret, p['gn_g'], p['gn_b'], 1e-5).astype(o_nsa.dtype) * jax.nn.silu(pr['rg'])
    return jnp.concatenate([o_nsa, y_ret], axis=-1)


def odd_mixer_prompt(proj, p):
    B, T, _ = proj.shape
    pos = jnp.arange(T)
    pr = odd_project(proj, p, pos)
    qn = pr['q_n'].reshape(B, T, NSA_W)
    qr = pr['q_r'].reshape(B, T, NSA_W)
    kc = rms_norm(nsa_compress(pr['kc'], *p['ck']), p['k_norm'][0])
    vc = nsa_compress(pr['vc'], *p['cv'])
    n_cmp = kc.shape[1]
    n_sel = -(-T // SEL_BLOCK)
    o_cmp, sel = nsa_cmp_select(qn, _tile_cmp(kc), _tile_cmp(vc), _overlap_T(n_cmp, n_sel),
                                n_cmp=n_cmp, n_sel=n_sel, q_pos0=0)
    o_slc = nsa_flash(qr, _tile_groups(pr['ks']), _tile_groups(pr['vs']), sel, _sel_expand(T))
    o_win = nsa_flash(qr, _tile_groups(pr['kw']), _tile_groups(pr['vw']))
    S, o_ret = retention_prompt_pallas(pr['rq'].reshape(B, T, -1), pr['rk'].reshape(B, T, -1),
                                       pr['rv'].reshape(B, T, -1))
    o_ret = o_ret.reshape(B, T, RET_H, RET_DV)
    gates = pr['gates']
    heads = (B, T, NSA_H, NSA_HD)
    o_nsa = (gates[..., 0:1] * o_cmp.reshape(heads) + gates[..., 1:2] * o_slc.reshape(heads)
             + gates[..., 2:3] * o_win.reshape(heads)).reshape(B, T, NSA_W)
    y_ret = head_group_norm(o_ret, p['gn_g'], p['gn_b'], 1e-5).astype(o_nsa.dtype) * jax.nn.silu(pr['rg'])
    out = jnp.concatenate([o_nsa, y_ret], axis=-1)
    kv_rows = jnp.stack([pr['kc'], pr['vc'], pr['ks'], pr['vs']], axis=2)
    win = jnp.stack([pr['kw'], pr['vw']], axis=2)[:, T - min(WINDOW, T):]
    return out, kv_rows, win, S


def odd_mixer_sample(proj, p, cache_layer, page_table, win_buf, ret_s0):
    B, T, _ = proj.shape
    assert T == DEC_SEQ == 1 and win_buf.shape[1] == WIN_BUF
    pos = PAST_LEN + jnp.arange(T)
    pr = odd_project(proj, p, pos)
    scale = NSA_HD ** -0.5
    new_rows = jnp.stack([pr['ks'], pr['vs'], pr['kw'], pr['vw']], axis=2)[:, 0].reshape(B, 4, SLOT_ROWS)
    cache_t = jnp.transpose(cache_layer, (0, 2, 3, 4, 1)).reshape(cache_layer.shape[0], KV_ROWS, PAGE_SIZE)
    win_t = jnp.transpose(win_buf, (0, 2, 3, 4, 1)).reshape(B, 2 * SLOT_ROWS, WIN_BUF)
    wk = _dec_cmp_weights(*p['ck'])
    wv = _dec_cmp_weights(*p['cv'])
    w1t, b1, w2t, b2t = (jnp.stack([a, b]) for a, b in zip(wk, wv))
    kn = jnp.tile(p['k_norm'][0], NSA_G).reshape(1, SLOT_ROWS)
    t = np.arange(PAST_LEN)
    expand = jnp.asarray(np.arange(CMP_PAD)[:, None] == (t // SEL_BLOCK)[None, :], dtype=BF16)
    h = np.arange(NSA_H)
    grp = jnp.asarray((h[:, None] // NSA_HPG) == (h[None, :] // NSA_HPG), dtype=BF16)
    o16 = dec_nsa(page_table, cache_t, win_t, _place_heads(pr['q_n'][:, 0] * scale),
                  _place_heads(pr['q_r'][:, 0] * scale), new_rows, pr['gates'][:, 0],
                  w1t, b1, w2t, b2t, kn, _overlap_T(DEC_N_CMP, DEC_N_SEL), expand, grp)
    o_nsa = _take_heads(o16)[:, None, :]
    S, o_ret = retention_chunk(ret_s0.astype(jnp.float32), pr['rq'], pr['rk'], pr['rv'])
    y_ret = head_group_norm(o_ret, p['gn_g'], p['gn_b'], 1e-5).astype(o_nsa.dtype) * jax.nn.silu(pr['rg'])
    out = jnp.concatenate([o_nsa, y_ret], axis=-1)
    rows = jnp.stack([pr['kc'], pr['vc'], pr['ks'], pr['vs']], axis=2).astype(cache_layer.dtype)
    new_col = jnp.stack([pr['kw'], pr['vw']], axis=2)[:, 0].reshape(B, 2 * SLOT_ROWS, 1).astype(win_buf.dtype)
    win_new = jnp.concatenate([win_t[:, :, T:], new_col], axis=2).reshape(B, 2, NSA_G, NSA_HD, WIN_BUF)
    return out, rows, jnp.transpose(win_new, (0, 4, 1, 2, 3)), S


def odd_mixer(proj, p, B, T, DB, cache_layer, page_table, win_buf, ret_s0):
    assert DEC_SEQ == 1 and win_buf.shape[1] == WIN_BUF
    n_p = B * T
    kv = (B, T, NSA_G, NSA_HD)
    cols = lambda rows, off, w: proj[rows, off:off + w]
    prompt = slice(0, n_p)
    dec = slice(n_p, n_p + DB)
    qn, qr, ks, kw, ks4, vs4, kw4, vw4, gates, rq, rk = odd_pre(proj, jnp.tile(jnp.arange(T), B), p, 0, n_p, False)
    seq = lambda z: z.reshape(B, T, -1)
    kc_raw = cols(prompt, OFF_KC, KV_W).reshape(kv)
    vc_raw = cols(prompt, OFF_VC, KV_W).reshape(kv)
    kc = rms_norm(nsa_compress(kc_raw, *p['ck']), p['k_norm'][0])
    vc = nsa_compress(vc_raw, *p['cv'])
    n_cmp = kc.shape[1]
    n_sel = -(-T // SEL_BLOCK)
    o_cmp, sel = nsa_cmp_select(seq(qn), _tile_cmp(kc), _tile_cmp(vc), _overlap_T(n_cmp, n_sel),
                                n_cmp=n_cmp, n_sel=n_sel, q_pos0=0)
    o_slc = nsa_flash(seq(qr), seq(ks4), seq(vs4), sel, _sel_expand(T))
    o_win = nsa_flash(seq(qr), seq(kw4), seq(vw4))
    ret_p, o_ret = retention_prompt_pallas(rq, rk, proj, B, T, v_col0=OFF_RV)
    flat = lambda z: z.reshape(n_p, -1)
    cat = odd_post(flat(o_cmp), flat(o_slc), flat(o_win), gates, o_ret, proj, p, n_p + DB, 0)
    kv_rows_p = jnp.stack([kc_raw, vc_raw, ks.reshape(kv), cols(prompt, OFF_VS, KV_W).reshape(kv)], axis=2)
    win_p = jnp.stack([kw.reshape(kv), cols(prompt, OFF_VW, KV_W).reshape(kv)], axis=2)[:, T - min(WINDOW, T):]
    qn, qr, ks, kw, _, _, _, _, gates, rq, rk = odd_pre(proj, jnp.asarray(PAST_LEN), p, n_p, DB, True)
    scale = NSA_HD ** -0.5
    heads = lambda z: z.reshape(DB, NSA_H, NSA_HD)
    vs, vw = cols(dec, OFF_VS, KV_W), cols(dec, OFF_VW, KV_W)
    new_rows = jnp.stack([ks, vs, kw, vw], axis=1)
    cache_t = jnp.transpose(cache_layer, (0, 2, 3, 4, 1)).reshape(cache_layer.shape[0], KV_ROWS, PAGE_SIZE)
    win_t = jnp.transpose(win_buf, (0, 2, 3, 4, 1)).reshape(DB, 2 * SLOT_ROWS, WIN_BUF)
    w1t, b1, w2t, b2t = (jnp.stack([a, b]) for a, b in zip(_dec_cmp_weights(*p['ck']), _dec_cmp_weights(*p['cv'])))
    kn = jnp.tile(p['k_norm'][0], NSA_G).reshape(1, SLOT_ROWS)
    t = np.arange(PAST_LEN)
    expand = jnp.asarray(np.arange(CMP_PAD)[:, None] == (t // SEL_BLOCK)[None, :], dtype=BF16)
    h = np.arange(NSA_H)
    grp = jnp.asarray((h[:, None] // NSA_HPG) == (h[None, :] // NSA_HPG), dtype=BF16)
    o16 = dec_nsa(page_table, cache_t, win_t, _place_heads(heads(qn) * scale), _place_heads(heads(qr) * scale),
                  new_rows, gates[:, :3 * NSA_H].reshape(DB, NSA_H, 3),
                  w1t, b1, w2t, b2t, kn, _overlap_T(DEC_N_CMP, DEC_N_SEL), expand, grp)
    o_nsa = _take_heads(o16)
    ret_s, o_ret = retention_chunk(ret_s0.astype(F32), rq.reshape(DB, 1, RET_H, RET_DK),
                                   rk.reshape(DB, 1, RET_H, RET_DK), cols(dec, OFF_RV, RET_W).reshape(DB, 1, RET_H, RET_DV))
    cat = odd_post(o_nsa, o_nsa, o_nsa, gates, o_ret.reshape(DB, RET_W), proj, p, n_p + DB, n_p, prior=cat, gated=True)
    kvs = (DB, 1, NSA_G, NSA_HD)
    rows_s = jnp.stack([cols(dec, OFF_KC, KV_W).reshape(kvs), cols(dec, OFF_VC, KV_W).reshape(kvs),
                        ks.reshape(kvs), vs.reshape(kvs)], axis=2).astype(cache_layer.dtype)
    new_col = jnp.concatenate([kw, vw], axis=1)[:, :, None].astype(win_buf.dtype)
    win_new = jnp.concatenate([win_t[:, :, DEC_SEQ:], new_col], axis=2).reshape(DB, 2, NSA_G, NSA_HD, WIN_BUF)
    win_s = jnp.transpose(win_new, (0, 4, 1, 2, 3))
    return cat, (kv_rows_p, win_p, ret_p), (rows_s, win_s, ret_s)


def _stack(xs, dt):
    return jnp.stack(xs).astype(dt)


def kernel(x_prompt, x_sample, state_lru_h, state_lru_conv, state_rwkv_shift, state_rwkv_wkv,
           cache_nsa_kv, cache_nsa_win, state_ret, page_table,
           norm_ffn1, ffn1_w_in, ffn1_w_out, norm_mix, norm_ffn2, ffn2_w_in, ffn2_w_out,
           ab_w_in, lru_conv_w, lru_conv_b, lru_wa, lru_ba, lru_wx, lru_bx, lru_lambda,
           rwkv_mu, rwkv_w0, rwkv_w2, rwkv_a0, rwkv_a2, rwkv_g2, rwkv_k_k, rwkv_k_a, rwkv_r_k,
           rwkv_ln_g, rwkv_ln_b, ab_w_out,
           cd_w_in, nsa_q_norm, nsa_k_norm, cmp_k_w1, cmp_k_b1, cmp_k_w2, cmp_k_b2,
           cmp_v_w1, cmp_v_b1, cmp_v_w2, cmp_v_b2, ret_gn_g, ret_gn_b, cd_w_out):
    dt = x_prompt.dtype
    B = x_prompt.shape[0]
    DB = x_sample.shape[0]
    y = jnp.concatenate([x_prompt.reshape(N_PROMPT, D_MODEL), x_sample.reshape(DB * DEC_SEQ, D_MODEL)], axis=0)
    lru_h_p, lru_h_s, lru_c_p, lru_c_s, sh_p, sh_s, wkv_p, wkv_s = [], [], [], [], [], [], [], []
    kv_p, kv_s, win_p, win_s, ret_p, ret_s = [], [], [], [], [], []
    for layer in range(DEPTH):
        li = layer // 2
        y = ffn_block(y, norm_ffn1[layer], *_prep_ffn_weights(ffn1_w_in[layer], ffn1_w_out[layer]))
        if layer % 2 == 0:
            p = {'conv_w': lru_conv_w[li], 'conv_b': lru_conv_b[li],
                 'wa': lru_wa[li], 'ba': lru_ba[li], 'wx': lru_wx[li], 'bx': lru_bx[li], 'lam': lru_lambda[li],
                 'mu': rwkv_mu[li], 'w0': rwkv_w0[li], 'w2': rwkv_w2[li], 'a0': rwkv_a0[li], 'a2': rwkv_a2[li],
                 'g2': rwkv_g2[li], 'k_k': rwkv_k_k[li], 'k_a': rwkv_k_a[li], 'r_k': rwkv_r_k[li],
                 'ln_g': rwkv_ln_g[li], 'ln_b': rwkv_ln_b[li]}
            proj = norm_matmul(y, norm_mix[layer], _prep_cols(ab_w_in[li]))
            cat, (a0, a1, a2, a3), (b0, b1, b2, b3) = even_mixer(
                proj, p, B, SEQ, DB, state_lru_h[li], state_lru_conv[li], state_rwkv_shift[li], state_rwkv_wkv[li])
            lru_h_p.append(a0); lru_c_p.append(a1); sh_p.append(a2); wkv_p.append(a3)
            lru_h_s.append(b0); lru_c_s.append(b1); sh_s.append(b2); wkv_s.append(b3)
            w_out = ab_w_out[li]
        else:
            p = {'q_norm': nsa_q_norm[li], 'k_norm': nsa_k_norm[li],
                 'ck': (cmp_k_w1[li], cmp_k_b1[li], cmp_k_w2[li], cmp_k_b2[li]),
                 'cv': (cmp_v_w1[li], cmp_v_b1[li], cmp_v_w2[li], cmp_v_b2[li]),
                 'gn_g': ret_gn_g[li], 'gn_b': ret_gn_b[li]}
            proj = norm_matmul(y, norm_mix[layer], _odd_weight_cols(cd_w_in[li]))
            cat, (a0, a1, a2), (b0, b1, b2) = odd_mixer(
                proj, p, B, SEQ, DB, cache_nsa_kv[li], page_table, cache_nsa_win[li], state_ret[li])
            kv_p.append(a0); win_p.append(a1); ret_p.append(a2)
            kv_s.append(b0); win_s.append(b1); ret_s.append(b2)
            w_out = cd_w_out[li]
        y = matmul_residual(cat, w_out.astype(BF16), y)
        y = ffn_block(y, norm_ffn2[layer], *_prep_ffn_weights(ffn2_w_in[layer], ffn2_w_out[layer]))
    yp = y[:N_PROMPT].reshape(B, SEQ, D_MODEL)
    ys = y[N_PROMPT:].reshape(DB, DEC_SEQ, D_MODEL)
    return (yp, ys,
            _stack(lru_h_p, dt), _stack(lru_h_s, dt), _stack(lru_c_p, dt), _stack(lru_c_s, dt),
            _stack(sh_p, dt), _stack(sh_s, dt), _stack(wkv_p, dt), _stack(wkv_s, dt),
            _stack(kv_p, dt), _stack(kv_s, dt), _stack(win_p, dt), _stack(win_s, dt),
            _stack(ret_p, dt), _stack(ret_s, dt))
```

```python
import functools

import jax
import jax.numpy as jnp
import numpy as np
from jax import lax
from jax.experimental import pallas as pl
from jax.experimental.pallas import tpu as pltpu

D_MODEL = 2048
BATCH = 4
SEQ = 2048
DEPTH = 2
DEC_BATCH = 128
DEC_SEQ = 1
PAST_LEN = 2048
PAGE_SIZE = 128
D_FF = 5504
LRU_W = D_MODEL // 2
LRU_BLOCKS = 16
LRU_BS = LRU_W // LRU_BLOCKS
CONV_W = 4
LRU_C = 8.0
RWKV_W = D_MODEL // 2
RWKV_HD = 64
RWKV_H = RWKV_W // RWKV_HD
W_LORA = 64
A_LORA = 64
G_LORA = 160
SHIFT_W = 3 * RWKV_W + W_LORA + A_LORA + G_LORA
AB_COLS = 2 * LRU_W + SHIFT_W
NSA_H = 16
NSA_G = 4
NSA_HPG = NSA_H // NSA_G
NSA_HD = 64
NSA_W = NSA_H * NSA_HD
ROPE_DIMS = NSA_HD // 4
ROPE_THETA = 500000.0
CMP_BLOCK = 32
CMP_STRIDE = 16
CMP_R = CMP_BLOCK // CMP_STRIDE
CMP_HID = 256
SEL_BLOCK = 64
SEL_TOP = 16
SEL_Q_BLOCK = 64
WINDOW = 512
WIN_BLOCK = 128
FORCE_SCORE = 1e4
KV_SLOTS = 4
RET_H = 8
RET_DK = 64
RET_DV = 128
RET_W = RET_H * RET_DV
RET_CHUNK = 128
RET_THETA = 10000.0
CD_COLS = NSA_W + 6 * NSA_G * NSA_HD + 3 * NSA_H + 2 * RET_H * RET_DK + 2 * RET_W

N_TOK = BATCH * SEQ + DEC_BATCH * DEC_SEQ
N_PROMPT = BATCH * SEQ

LANE = 128
VMEM_LIMIT_BYTES = 56 * 1024 * 1024
ROW_TILE = 640
FF_TILE = 512
D_FF_PAD = 5632
COL_TILE = 512

BF16 = jnp.bfloat16
F32 = jnp.float32


def _round_up(n, m):
    return -(-n // m) * m


def _rms_rows(x, g):
    ms = jnp.mean(x * x, axis=-1, keepdims=True)
    return x * lax.rsqrt(ms + 1e-6) * g


def _ffn_kernel(x_ref, g_ref, wg_ref, wu_ref, wo_ref, o_ref, xn_ref, acc_ref):
    k = pl.program_id(1)

    @pl.when(k == 0)
    def _():
        xn_ref[...] = _rms_rows(x_ref[...], g_ref[...]).astype(BF16)
        acc_ref[...] = jnp.zeros_like(acc_ref)

    xn = xn_ref[...]
    gate = jnp.dot(xn, wg_ref[...], preferred_element_type=F32)
    up = jnp.dot(xn, wu_ref[...], preferred_element_type=F32)
    act = gate * jax.nn.sigmoid(gate) * up
    acc_ref[...] += jnp.dot(act.astype(BF16), wo_ref[...], preferred_element_type=F32)

    @pl.when(k == pl.num_programs(1) - 1)
    def _():
        o_ref[...] = x_ref[...] + 0.5 * acc_ref[...]


def ffn_block(x, g, wg, wu, wo):
    m, d = x.shape
    return pl.pallas_call(
        _ffn_kernel,
        grid=(m // ROW_TILE, D_FF_PAD // FF_TILE),
        in_specs=[
            pl.BlockSpec((ROW_TILE, d), lambda i, k: (i, 0)),
            pl.BlockSpec((1, d), lambda i, k: (0, 0)),
            pl.BlockSpec((d, FF_TILE), lambda i, k: (0, k)),
            pl.BlockSpec((d, FF_TILE), lambda i, k: (0, k)),
            pl.BlockSpec((FF_TILE, d), lambda i, k: (k, 0)),
        ],
        out_specs=pl.BlockSpec((ROW_TILE, d), lambda i, k: (i, 0)),
        out_shape=jax.ShapeDtypeStruct((m, d), F32),
        scratch_shapes=[pltpu.VMEM((ROW_TILE, d), BF16), pltpu.VMEM((ROW_TILE, d), F32)],
        compiler_params=pltpu.CompilerParams(
            dimension_semantics=("parallel", "arbitrary"), vmem_limit_bytes=VMEM_LIMIT_BYTES),
        name="ffn_block",
    )(x, g.reshape(1, d), wg, wu, wo)


def _norm_matmul_kernel(x_ref, g_ref, w_ref, o_ref, xn_ref):
    @pl.when(pl.program_id(1) == 0)
    def _():
        xn_ref[...] = _rms_rows(x_ref[...], g_ref[...]).astype(BF16)

    o_ref[...] = jnp.dot(xn_ref[...], w_ref[...], preferred_element_type=F32)


def norm_matmul(x, g, w):
    m, k = x.shape
    n = w.shape[1]
    return pl.pallas_call(
        _norm_matmul_kernel,
        grid=(m // ROW_TILE, n // COL_TILE),
        in_specs=[
            pl.BlockSpec((ROW_TILE, k), lambda i, j: (i, 0)),
            pl.BlockSpec((1, k), lambda i, j: (0, 0)),
            pl.BlockSpec((k, COL_TILE), lambda i, j: (0, j)),
        ],
        out_specs=pl.BlockSpec((ROW_TILE, COL_TILE), lambda i, j: (i, j)),
        out_shape=jax.ShapeDtypeStruct((m, n), F32),
        scratch_shapes=[pltpu.VMEM((ROW_TILE, k), BF16)],
        compiler_params=pltpu.CompilerParams(
            dimension_semantics=("parallel", "arbitrary"), vmem_limit_bytes=VMEM_LIMIT_BYTES),
        name="norm_matmul",
    )(x, g.reshape(1, k), w)


def _matmul_residual_kernel(a_ref, w_ref, r_ref, o_ref):
    o_ref[...] = r_ref[...] + jnp.dot(a_ref[...].astype(BF16), w_ref[...], preferred_element_type=F32)


def matmul_residual(a, w, res):
    m, k = a.shape
    n = w.shape[1]
    return pl.pallas_call(
        _matmul_residual_kernel,
        grid=(m // ROW_TILE, n // COL_TILE),
        in_specs=[
            pl.BlockSpec((ROW_TILE, k), lambda i, j: (i, 0)),
            pl.BlockSpec((k, COL_TILE), lambda i, j: (0, j)),
            pl.BlockSpec((ROW_TILE, COL_TILE), lambda i, j: (i, j)),
        ],
        out_specs=pl.BlockSpec((ROW_TILE, COL_TILE), lambda i, j: (i, j)),
        out_shape=jax.ShapeDtypeStruct((m, n), F32),
        compiler_params=pltpu.CompilerParams(
            dimension_semantics=("parallel", "arbitrary"), vmem_limit_bytes=VMEM_LIMIT_BYTES),
        name="matmul_residual",
    )(a, w, res)


WCAST_ROWS = 256
WCAST_COLS = 512


def _cast_w_in_kernel(w_ref, wg_ref, wu_ref):
    pad = jnp.zeros((w_ref.shape[0], D_FF_PAD - D_FF), BF16)
    wg_ref[:, :D_FF] = w_ref[:, :D_FF].astype(BF16)
    wg_ref[:, D_FF:] = pad
    wu_ref[:, :D_FF] = w_ref[:, D_FF:].astype(BF16)
    wu_ref[:, D_FF:] = pad


def _cast_w_out_kernel(w_ref, wo_ref):
    wo_ref[:D_FF, :] = w_ref[...].astype(BF16)
    wo_ref[D_FF:, :] = jnp.zeros((D_FF_PAD - D_FF, w_ref.shape[1]), BF16)


def _prep_ffn_weights(w_in, w_out, layer):
    d = w_in.shape[1]
    wg, wu = pl.pallas_call(
        _cast_w_in_kernel,
        grid=(d // WCAST_ROWS,),
        in_specs=[pl.BlockSpec((None, WCAST_ROWS, 2 * D_FF), lambda i: (layer, i, 0))],
        out_specs=[pl.BlockSpec((WCAST_ROWS, D_FF_PAD), lambda i: (i, 0))] * 2,
        out_shape=[jax.ShapeDtypeStruct((d, D_FF_PAD), BF16)] * 2,
        compiler_params=pltpu.CompilerParams(dimension_semantics=("parallel",), vmem_limit_bytes=VMEM_LIMIT_BYTES),
        name="cast_w_in",
    )(w_in)
    wo = pl.pallas_call(
        _cast_w_out_kernel,
        grid=(d // WCAST_COLS,),
        in_specs=[pl.BlockSpec((None, D_FF, WCAST_COLS), lambda j: (layer, 0, j))],
        out_specs=pl.BlockSpec((D_FF_PAD, WCAST_COLS), lambda j: (0, j)),
        out_shape=jax.ShapeDtypeStruct((D_FF_PAD, d), BF16),
        compiler_params=pltpu.CompilerParams(dimension_semantics=("parallel",), vmem_limit_bytes=VMEM_LIMIT_BYTES),
        name="cast_w_out",
    )(w_out)
    return wg, wu, wo


def _prep_cols(w):
    n = w.shape[1]
    return jnp.pad(w, ((0, 0), (0, _round_up(n, COL_TILE) - n))).astype(BF16)


SCAN_TILE = 256


def _lru_scan_kernel(a_ref, b_ref, h0_ref, o_ref, carry_ref):
    @pl.when(pl.program_id(1) == 0)
    def _():
        carry_ref[...] = h0_ref[...]

    a = a_ref[...]
    b = b_ref[...]
    rows = lax.broadcasted_iota(jnp.int32, a.shape, 0)
    k = 1
    while k < a.shape[0]:
        keep = rows >= k
        b = jnp.where(keep, a * pltpu.roll(b, k, 0) + b, b)
        a = jnp.where(keep, a * pltpu.roll(a, k, 0), a)
        k *= 2
    h = a * carry_ref[...] + b
    o_ref[...] = h
    carry_ref[...] = h[a.shape[0] - 1:, :]


def lru_scan(a, b, h0):
    B, T, W = a.shape
    tt = min(SCAN_TILE, T)
    return pl.pallas_call(
        _lru_scan_kernel,
        grid=(B, T // tt),
        in_specs=[
            pl.BlockSpec((None, tt, W), lambda i, t: (i, t, 0)),
            pl.BlockSpec((None, tt, W), lambda i, t: (i, t, 0)),
            pl.BlockSpec((None, 1, W), lambda i, t: (i, 0, 0)),
        ],
        out_specs=pl.BlockSpec((None, tt, W), lambda i, t: (i, t, 0)),
        out_shape=jax.ShapeDtypeStruct((B, T, W), F32),
        scratch_shapes=[pltpu.VMEM((1, W), F32)],
        compiler_params=pltpu.CompilerParams(
            dimension_semantics=("parallel", "arbitrary"), vmem_limit_bytes=VMEM_LIMIT_BYTES),
        name="lru_scan",
    )(a, b, h0.reshape(B, 1, W))


GROUP_W = NSA_HPG * NSA_HD
ATT_Q_TILE = 128
ATT_K_TILE = 256
CMP_PAD = 128
NEG_BIG = -1e30


def _stack_heads(q):
    head = lax.broadcasted_iota(jnp.int32, q.shape, 1) // NSA_HD
    return jnp.concatenate([jnp.where(head == h, q, 0.0) for h in range(NSA_HPG)], axis=0)


def _unstack_heads(o, tq):
    head = lax.broadcasted_iota(jnp.int32, (tq, GROUP_W), 1) // NSA_HD
    out = jnp.zeros((tq, GROUP_W), F32)
    for h in range(NSA_HPG):
        out = out + jnp.where(head == h, o[h * tq:(h + 1) * tq], 0.0)
    return out


def _cmp_select_kernel(q_ref, k_ref, v_ref, ov_ref, o_ref, sel_ref, *, n_cmp, n_sel, q_pos0):
    tq = q_ref.shape[0]
    i = pl.program_id(2)
    qs = _stack_heads(q_ref[...] * (NSA_HD ** -0.5)).astype(BF16)
    s = lax.dot_general(qs, k_ref[...], (((1,), (1,)), ((), ())), preferred_element_type=F32)
    q_pos = q_pos0 + i * tq + lax.broadcasted_iota(jnp.int32, (tq, CMP_PAD), 0)
    c = lax.broadcasted_iota(jnp.int32, (tq, CMP_PAD), 1)
    mask1 = (c < n_cmp) & (c * CMP_STRIDE + (CMP_BLOCK - 1) <= q_pos)
    mask = jnp.concatenate([mask1] * NSA_HPG, axis=0)
    s = jnp.where(mask, s, NEG_BIG)
    m = jnp.max(s, axis=-1, keepdims=True)
    e = jnp.where(mask, jnp.exp(s - m), 0.0)
    den = jnp.sum(e, axis=-1, keepdims=True)
    prob = e / jnp.where(den > 0, den, 1.0)
    o = jnp.dot(prob.astype(BF16), v_ref[...], preferred_element_type=F32)
    o_ref[...] = _unstack_heads(o, tq)
    psum = prob[0:tq]
    for h in range(1, NSA_HPG):
        psum = psum + prob[h * tq:(h + 1) * tq]
    imp = jnp.dot(psum.astype(BF16), ov_ref[...], preferred_element_type=F32)
    qb = q_pos // SEL_BLOCK
    valid = (c <= qb) & (c < n_sel)
    forced = (c == 0) | (c == qb) | (c == qb - 1)
    score = jnp.where(valid, jnp.where(forced, FORCE_SCORE, imp), -jnp.inf)
    k_top = min(SEL_TOP, n_sel)
    few_blocks = (q_pos0 + (i + 1) * tq - 1) // SEL_BLOCK < k_top

    @pl.when(few_blocks)
    def _():
        sel_ref[...] = jnp.where(valid, 1.0, 0.0)

    @pl.when(jnp.logical_not(few_blocks))
    def _():
        rank = jnp.zeros((tq, CMP_PAD), F32)
        for jp in range(n_sel):
            col = score[:, jp:jp + 1]
            beats = (col > score) | ((col == score) & (c > jp))
            rank = rank + jnp.where(beats, 1.0, 0.0)
        sel_ref[...] = jnp.where((rank < k_top) & (c < n_sel), 1.0, 0.0)


def nsa_cmp_select(qn, kc4, vc4, ovT, *, n_cmp, n_sel, q_pos0):
    B, T, _ = qn.shape
    tq = min(ATT_Q_TILE, T)
    return pl.pallas_call(
        functools.partial(_cmp_select_kernel, n_cmp=n_cmp, n_sel=n_sel, q_pos0=q_pos0),
        grid=(B, NSA_G, T // tq),
        in_specs=[
            pl.BlockSpec((None, tq, GROUP_W), lambda b, g, i: (b, i, g)),
            pl.BlockSpec((None, None, CMP_PAD, GROUP_W), lambda b, g, i: (b, g, 0, 0)),
            pl.BlockSpec((None, None, CMP_PAD, GROUP_W), lambda b, g, i: (b, g, 0, 0)),
            pl.BlockSpec((CMP_PAD, CMP_PAD), lambda b, g, i: (0, 0)),
        ],
        out_specs=[
            pl.BlockSpec((None, tq, GROUP_W), lambda b, g, i: (b, i, g)),
            pl.BlockSpec((None, None, tq, CMP_PAD), lambda b, g, i: (b, g, i, 0)),
        ],
        out_shape=[jax.ShapeDtypeStruct((B, T, NSA_W), F32),
                   jax.ShapeDtypeStruct((B, NSA_G, T, CMP_PAD), F32)],
        compiler_params=pltpu.CompilerParams(
            dimension_semantics=("parallel", "parallel", "parallel"), vmem_limit_bytes=VMEM_LIMIT_BYTES),
        name="nsa_cmp_select",
    )(qn, kc4, vc4, ovT)


def _flash_kernel(*refs, selected):
    if selected:
        q_ref, k_ref, v_ref, sel_ref, exp_ref, o_ref, m_ref, l_ref, acc_ref, s_a, s_b = refs
    else:
        q_ref, k_ref, v_ref, o_ref, m_ref, l_ref, acc_ref, s_a, s_b = refs
    tq = q_ref.shape[0]
    tk = ATT_K_TILE
    n_tiles = k_ref.shape[0] // tk
    i = pl.program_id(2)
    q = q_ref[...] * (NSA_HD ** -0.5)
    head = lax.broadcasted_iota(jnp.int32, q.shape, 1) // NSA_HD
    q4 = _stack_heads(q).astype(BF16)
    m_ref[...] = jnp.full(m_ref.shape, NEG_BIG, F32)
    l_ref[...] = jnp.zeros(l_ref.shape, F32)
    acc_ref[...] = jnp.zeros(acc_ref.shape, F32)
    q_pos = i * tq + lax.broadcasted_iota(jnp.int32, (tq, tk), 0)
    col = lax.broadcasted_iota(jnp.int32, (tq, tk), 1)
    if selected:
        sel = sel_ref[...].astype(BF16)
        lo = 0
    else:
        lo = jnp.maximum(i * tq - (WINDOW - 1), 0) // tk
    hi = (i * tq + tq - 1) // tk + 1

    def tile_start(j):
        return pl.multiple_of(jnp.minimum(j, n_tiles - 1) * tk, tk)

    def scores(j, s_ref):
        s_ref[...] = lax.dot_general(q4, k_ref[pl.ds(tile_start(j), tk), :], (((1,), (1,)), ((), ())),
                                     preferred_element_type=F32)

    def consume(j, s_ref):
        v = v_ref[pl.ds(tile_start(j), tk), :]
        k_pos = j * tk + col
        mask = k_pos <= q_pos
        if selected:
            mask = mask & (jnp.dot(sel, exp_ref[jnp.minimum(j, n_tiles - 1)], preferred_element_type=F32) > 0.5)
        else:
            mask = mask & (q_pos - k_pos < WINDOW)
        for h in range(NSA_HPG):
            s = jnp.where(mask, s_ref[h * tq:(h + 1) * tq, :], NEG_BIG)
            m_old = m_ref[h]
            m_new = jnp.maximum(m_old, jnp.max(s, axis=-1, keepdims=True))
            alpha = jnp.exp(m_old - m_new)
            p = jnp.where(mask, jnp.exp(s - pltpu.repeat(m_new, tk // LANE, axis=1)), 0.0)
            l_ref[h] = alpha * l_ref[h] + jnp.sum(p, axis=-1, keepdims=True)
            acc_ref[h] = (pltpu.repeat(alpha, GROUP_W // LANE, axis=1) * acc_ref[h]
                          + jnp.dot(p.astype(BF16), v, preferred_element_type=F32))
            m_ref[h] = m_new

    scores(lo, s_a)

    def body(t, carry):
        j = lo + 2 * t
        scores(j + 1, s_b)
        consume(j, s_a)
        scores(j + 2, s_a)
        consume(j + 1, s_b)
        return carry

    lax.fori_loop(0, (hi - lo + 1) // 2, body, 0)
    out = jnp.zeros((tq, GROUP_W), F32)
    for h in range(NSA_HPG):
        den = pltpu.repeat(l_ref[h], GROUP_W // LANE, axis=1)
        out = out + jnp.where(head == h, acc_ref[h] / jnp.where(den > 0, den, 1.0), 0.0)
    o_ref[...] = out


def nsa_flash(qr, k4, v4, sel=None, expand=None):
    B, T, _ = qr.shape
    tq = ATT_Q_TILE
    selected = sel is not None
    in_specs = [
        pl.BlockSpec((None, tq, GROUP_W), lambda b, g, i: (b, i, g)),
        pl.BlockSpec((None, T, GROUP_W), lambda b, g, i: (b, 0, g)),
        pl.BlockSpec((None, T, GROUP_W), lambda b, g, i: (b, 0, g)),
    ]
    args = [qr, k4, v4]
    if selected:
        in_specs += [
            pl.BlockSpec((None, None, tq, CMP_PAD), lambda b, g, i: (b, g, i, 0)),
            pl.BlockSpec(expand.shape, lambda b, g, i: (0, 0, 0)),
        ]
        args += [sel, expand]
    return pl.pallas_call(
        functools.partial(_flash_kernel, selected=selected),
        grid=(B, NSA_G, T // tq),
        in_specs=in_specs,
        out_specs=pl.BlockSpec((None, tq, GROUP_W), lambda b, g, i: (b, i, g)),
        out_shape=jax.ShapeDtypeStruct((B, T, NSA_W), F32),
        scratch_shapes=[pltpu.VMEM((NSA_HPG, tq, LANE), F32), pltpu.VMEM((NSA_HPG, tq, LANE), F32),
                        pltpu.VMEM((NSA_HPG, tq, GROUP_W), F32),
                        pltpu.VMEM((NSA_HPG * tq, ATT_K_TILE), F32), pltpu.VMEM((NSA_HPG * tq, ATT_K_TILE), F32)],
        compiler_params=pltpu.CompilerParams(
            dimension_semantics=("parallel", "parallel", "parallel"), vmem_limit_bytes=VMEM_LIMIT_BYTES),
        name="nsa_flash_sel" if selected else "nsa_flash_win",
    )(*args)


def _tile_groups(x):
    B, T = x.shape[:2]
    return jnp.broadcast_to(x[:, :, :, None, :], (B, T, NSA_G, NSA_HPG, NSA_HD)).reshape(B, T, NSA_W).astype(BF16)


def _tile_cmp(x):
    B, n = x.shape[:2]
    x = jnp.pad(jnp.moveaxis(x, 1, 2), ((0, 0), (0, 0), (0, CMP_PAD - n), (0, 0)))
    return jnp.tile(x, (1, 1, 1, NSA_HPG)).astype(BF16)


def _overlap_T(n_cmp, n_sel):
    ov = np.zeros((CMP_PAD, CMP_PAD), np.float32)
    cs = np.arange(n_cmp) * CMP_STRIDE
    ss = np.arange(n_sel) * SEL_BLOCK
    o = np.minimum(cs[None] + CMP_BLOCK, ss[:, None] + SEL_BLOCK) - np.maximum(cs[None], ss[:, None])
    ov[:n_cmp, :n_sel] = (np.clip(o, 0, None) / CMP_BLOCK).T
    return jnp.asarray(ov, dtype=BF16)


def _sel_expand(T):
    t = np.arange(T)
    e = (np.arange(CMP_PAD)[:, None] == (t // SEL_BLOCK)[None, :]).astype(np.float32)
    return jnp.asarray(e.reshape(CMP_PAD, T // ATT_K_TILE, ATT_K_TILE).transpose(1, 0, 2), dtype=BF16)


KV_ROWS = KV_SLOTS * NSA_G * NSA_HD
SLOT_ROWS = NSA_G * NSA_HD
N_PAGES = PAST_LEN // PAGE_SIZE
DEC_N_CHUNK = (PAST_LEN + DEC_SEQ) // CMP_STRIDE
DEC_N_CMP = DEC_N_CHUNK - CMP_R + 1
DEC_N_SEL = -(-(PAST_LEN + DEC_SEQ) // SEL_BLOCK)
WIN_BUF = min(WINDOW, PAST_LEN)


def _softmax_rows(s, mask, s_new=None):
    s = jnp.where(mask, s, NEG_BIG)
    m = jnp.max(s, axis=-1, keepdims=True)
    if s_new is not None:
        m = jnp.maximum(m, s_new)
    e = jnp.where(mask, jnp.exp(s - m), 0.0)
    den = jnp.sum(e, axis=-1, keepdims=True)
    if s_new is None:
        return e, den
    e_new = jnp.exp(s_new - m)
    return e, e_new, den + e_new


def _dec_nsa_kernel(pt_ref, *refs):
    pages = refs[:N_PAGES]
    (win_ref, qn_ref, qr_ref, new_ref, gate_ref, w1_ref, b1_ref, w2_ref, b2_ref, kn_ref,
     ov_ref, exp_ref, grp_ref, o_ref, xt_ref, acc_ref) = refs[N_PAGES:]
    del pt_ref
    f32 = F32
    half = 2 * NSA_HD
    n_chunk = DEC_N_CHUNK

    for p in range(N_PAGES):
        for sg in range(4):
            xt_ref[sg, p * PAGE_SIZE:(p + 1) * PAGE_SIZE, :] = pages[p][sg * half:(sg + 1) * half, :].T

    lane_lo = lax.broadcasted_iota(jnp.int32, (n_chunk, 2 * half), 1) % half < NSA_HD
    lane_grp = lax.broadcasted_iota(jnp.int32, (n_chunk, SLOT_ROWS), 1) // NSA_HD
    cmp_rows = []
    for slot in range(2):
        for gp in range(2):
            for rp in range(CMP_STRIDE // 2):
                xr = jnp.concatenate(
                    [xt_ref[slot * 2 + gp, pl.ds(2 * rp + j, n_chunk, stride=CMP_STRIDE), :] for j in range(2)],
                    axis=1)
                xs = jnp.concatenate([jnp.where(lane_lo, xr, 0.0), jnp.where(lane_lo, 0.0, xr)],
                                     axis=0).astype(BF16)
                part = jnp.dot(xs, w1_ref[slot, rp], preferred_element_type=f32)
                rows = pl.ds(gp * 2 * n_chunk, 2 * n_chunk)
                if rp == 0:
                    acc_ref[rows, :] = part
                else:
                    acc_ref[rows, :] += part
        acc = acc_ref[...]
        pre = b1_ref[slot] + acc[:, :CMP_HID] + pltpu.roll(acc[:, CMP_HID:], NSA_G * n_chunk - 1, 0)
        out = jnp.dot(jax.nn.gelu(pre).astype(BF16), w2_ref[slot], preferred_element_type=f32) + b2_ref[slot]
        if slot == 0:
            out = _rms_rows(out, kn_ref[...])
        sel_rows = jnp.zeros((n_chunk, SLOT_ROWS), f32)
        for g in range(NSA_G):
            sel_rows = sel_rows + jnp.where(lane_grp == g, out[g * n_chunk:(g + 1) * n_chunk], 0.0)
        cmp_rows.append(sel_rows.astype(BF16))
    kc, vc = cmp_rows

    qn = qn_ref[...].astype(BF16)
    qr = qr_ref[...].astype(BF16)
    nt = (((1,), (1,)), ((), ()))
    c = lax.broadcasted_iota(jnp.int32, (NSA_H, CMP_PAD), 1)
    s = lax.dot_general(qn, kc, nt, preferred_element_type=f32)
    e, den = _softmax_rows(s, c < DEC_N_CMP)
    prob = e / jnp.where(den > 0, den, 1.0)
    o_cmp = jnp.dot(prob.astype(BF16), vc, preferred_element_type=f32)
    p_hi, p_mid = _split_bf16(prob)
    p_lo = (prob - p_hi.astype(f32) - p_mid.astype(f32)).astype(BF16)
    grp = grp_ref[...]
    psum = (jnp.dot(grp, p_hi, preferred_element_type=f32) + jnp.dot(grp, p_mid, preferred_element_type=f32)
            + jnp.dot(grp, p_lo, preferred_element_type=f32))
    imp = jnp.dot(psum.astype(BF16), ov_ref[...], preferred_element_type=f32)
    qb = (PAST_LEN + DEC_SEQ - 1) // SEL_BLOCK
    valid = c <= qb
    forced = (c == 0) | (c == qb) | (c == qb - 1)
    score = jnp.where(valid, jnp.where(forced, FORCE_SCORE, imp), -jnp.inf)
    rank = jnp.zeros((NSA_H, CMP_PAD), f32)
    for jp in range(DEC_N_SEL):
        col = score[:, jp:jp + 1]
        rank = rank + jnp.where((col > score) | ((col == score) & (c > jp)), 1.0, 0.0)
    sel = jnp.where((rank < min(SEL_TOP, DEC_N_SEL)) & (c < DEC_N_SEL), 1.0, 0.0).astype(BF16)

    new = new_ref[...]
    new_b = new.astype(BF16).astype(f32)
    qr_f = qr.astype(f32)
    s_pages = [jnp.dot(qr, pages[p][2 * SLOT_ROWS:3 * SLOT_ROWS, :].astype(BF16), preferred_element_type=f32)
               for p in range(N_PAGES)]
    s = jnp.concatenate(s_pages, axis=1)
    mask = jnp.dot(sel, exp_ref[...], preferred_element_type=f32) > 0.5
    s_new = jnp.sum(qr_f * new_b[0:1], axis=-1, keepdims=True)
    e, e_new, den = _softmax_rows(s, mask, s_new)
    e = e.astype(BF16)
    o_slc = e_new.astype(BF16).astype(f32) * new_b[1:2]
    for p in range(N_PAGES):
        o_slc = o_slc + lax.dot_general(e[:, p * PAGE_SIZE:(p + 1) * PAGE_SIZE],
                                        pages[p][3 * SLOT_ROWS:4 * SLOT_ROWS, :].astype(BF16), nt,
                                        preferred_element_type=f32)
    o_slc = o_slc / den

    s = jnp.dot(qr, win_ref[0:SLOT_ROWS, :].astype(BF16), preferred_element_type=f32)
    i_buf = lax.broadcasted_iota(jnp.int32, (NSA_H, WIN_BUF), 1)
    s_new = jnp.sum(qr_f * new_b[2:3], axis=-1, keepdims=True)
    e, e_new, den = _softmax_rows(s, WIN_BUF - i_buf < WINDOW, s_new)
    o_win = e_new.astype(BF16).astype(f32) * new_b[3:4] + lax.dot_general(
        e.astype(BF16), win_ref[SLOT_ROWS:2 * SLOT_ROWS, :].astype(BF16), nt, preferred_element_type=f32)
    o_win = o_win / den

    gates = gate_ref[...]
    o_ref[...] = gates[:, 0:1] * o_cmp + gates[:, 1:2] * o_slc + gates[:, 2:3] * o_win


def dec_nsa(page_table, cache_t, win_t, qn16, qr16, new_rows, gates, w1t, b1, w2t, b2t, kn, ovT, expand, grp):
    DB = qn16.shape[0]
    const = lambda shape: pl.BlockSpec(shape, lambda b, pt: (0,) * len(shape))
    per_b = lambda shape: pl.BlockSpec((None,) + shape, lambda b, pt: (b,) + (0,) * len(shape))
    page_specs = [pl.BlockSpec((None, KV_ROWS, PAGE_SIZE), functools.partial(lambda b, pt, p: (pt[b, p], 0, 0), p=p))
                  for p in range(N_PAGES)]
    in_specs = page_specs + [
        per_b((2 * SLOT_ROWS, WIN_BUF)), per_b((NSA_H, SLOT_ROWS)), per_b((NSA_H, SLOT_ROWS)),
        per_b((4, SLOT_ROWS)), per_b((NSA_H, 3)),
        const(w1t.shape), const(b1.shape), const(w2t.shape), const(b2t.shape), const(kn.shape),
        const(ovT.shape), const(expand.shape), const(grp.shape),
    ]
    grid_spec = pltpu.PrefetchScalarGridSpec(
        num_scalar_prefetch=1, grid=(DB,), in_specs=in_specs,
        out_specs=pl.BlockSpec((None, NSA_H, SLOT_ROWS), lambda b, pt: (b, 0, 0)),
        scratch_shapes=[pltpu.VMEM((4, PAST_LEN, 2 * NSA_HD), F32),
                        pltpu.VMEM((NSA_G * DEC_N_CHUNK, CMP_R * CMP_HID), F32)])
    return pl.pallas_call(
        _dec_nsa_kernel,
        grid_spec=grid_spec,
        out_shape=jax.ShapeDtypeStruct((DB, NSA_H, SLOT_ROWS), F32),
        compiler_params=pltpu.CompilerParams(
            dimension_semantics=("arbitrary",), vmem_limit_bytes=VMEM_LIMIT_BYTES),
        name="dec_nsa",
    )(page_table, *([cache_t] * N_PAGES), win_t, qn16, qr16, new_rows, gates, w1t, b1, w2t, b2t, kn, ovT, expand, grp)


def _win_shift_kernel(win_ref, new_ref, o_ref):
    w = win_ref[...]
    n = w.shape[1]
    row = lax.broadcasted_iota(jnp.int32, w.shape, 0)
    lane = lax.broadcasted_iota(jnp.int32, w.shape, 1)
    col = jnp.sum(jnp.where(row == lane, jnp.broadcast_to(new_ref[...], w.shape), 0.0), axis=1, keepdims=True)
    o_ref[...] = jnp.where(lane == n - 1, col, pltpu.roll(w, n - 1, 1))


def win_shift(win_t, new_row):
    DB, R, W = win_t.shape
    assert R == W
    return pl.pallas_call(
        _win_shift_kernel,
        grid=(DB,),
        in_specs=[pl.BlockSpec((None, R, W), lambda b: (b, 0, 0)), pl.BlockSpec((None, 1, R), lambda b: (b, 0, 0))],
        out_specs=pl.BlockSpec((None, R, W), lambda b: (b, 0, 0)),
        out_shape=jax.ShapeDtypeStruct((DB, R, W), win_t.dtype),
        compiler_params=pltpu.CompilerParams(dimension_semantics=("parallel",), vmem_limit_bytes=VMEM_LIMIT_BYTES),
        name="win_shift",
    )(win_t, new_row)


def _dec_cmp_weights(w1, b1, w2, b2):
    w = w1.reshape(CMP_R, CMP_STRIDE // 2, 2, 1, NSA_HD, CMP_HID)
    w = jnp.broadcast_to(w, (CMP_R, CMP_STRIDE // 2, 2, 2, NSA_HD, CMP_HID))
    w = jnp.moveaxis(w, 0, 4).reshape(CMP_STRIDE // 2, 4 * NSA_HD, CMP_R * CMP_HID)
    return (w.astype(BF16), b1.reshape(1, CMP_HID), jnp.tile(w2, (1, NSA_G)).astype(BF16),
            jnp.tile(b2, NSA_G).reshape(1, SLOT_ROWS))


def _place_heads(q):
    own = (jnp.arange(NSA_H)[:, None] // NSA_HPG) == jnp.arange(NSA_G)[None, :]
    return jnp.where(own[None, :, :, None], q[:, :, None, :], 0.0).reshape(q.shape[0], NSA_H, SLOT_ROWS)


def _take_heads(o):
    o = o.reshape(o.shape[0], NSA_H, NSA_G, NSA_HD)
    return o[:, jnp.arange(NSA_H), jnp.arange(NSA_H) // NSA_HPG, :].reshape(o.shape[0], NSA_W)


WKV_C = 64
WKV_PAIR = 2 * RWKV_HD
WKV_T_TILE = 512
WKV_PAIRS_PER_STEP = 4


def _split_bf16(x):
    hi = x.astype(BF16)
    return hi, (x - hi.astype(F32)).astype(BF16)


def _dot3(a, b):
    a_hi, a_lo = _split_bf16(a)
    b_hi, b_lo = _split_bf16(b)
    return (jnp.dot(a_hi, b_hi, preferred_element_type=F32) + jnp.dot(a_hi, b_lo, preferred_element_type=F32)
            + jnp.dot(a_lo, b_hi, preferred_element_type=F32))


def _wkv_kernel(r_ref, lw_ref, k_ref, v_ref, a_ref, b_ref, s0_ref, y_ref, sT_ref, s_scr):
    C = WKV_C
    P = WKV_PAIR
    n_chunks = r_ref.shape[0] // C

    @pl.when(pl.program_id(2) == 0)
    def _():
        s_scr[...] = s0_ref[...]

    lo_lane = lax.broadcasted_iota(jnp.int32, (C, P), 1) < RWKV_HD
    row = lax.broadcasted_iota(jnp.int32, (2 * C, 2 * C), 0)
    col = lax.broadcasted_iota(jnp.int32, (2 * C, 2 * C), 1)
    same_head = (row // C) == (col // C)
    strict = same_head & (row > col)
    lower = same_head & (row >= col)
    eye = jnp.where(row == col, 1.0, 0.0)
    tril = jnp.where(lax.broadcasted_iota(jnp.int32, (C, C), 0) >= lax.broadcasted_iota(jnp.int32, (C, C), 1),
                     1.0, 0.0).astype(BF16)

    def stack(x):
        return jnp.concatenate([jnp.where(lo_lane, x, 0.0), jnp.where(lo_lane, 0.0, x)], axis=0)

    def chunk(c, carry):
        stages = [pair_chunk(c, q) for q in range(WKV_PAIRS_PER_STEP)]
        while stages:
            stages = [g for g in stages if next(g, True) is None]
        return carry

    def pair_chunk(c, q):
        sl = pl.ds(pl.multiple_of(c * C, C), C)
        lanes = slice(q * P, (q + 1) * P)
        r, lw, k, v, a, b = (ref[sl, lanes] for ref in (r_ref, lw_ref, k_ref, v_ref, a_ref, b_ref))
        lw_hi, lw_mid = _split_bf16(lw)
        lw_lo = (lw - lw_hi.astype(F32) - lw_mid.astype(F32)).astype(BF16)
        cs = (jnp.dot(tril, lw_hi, preferred_element_type=F32) + jnp.dot(tril, lw_mid, preferred_element_type=F32)
              + jnp.dot(tril, lw_lo, preferred_element_type=F32))
        yield
        g_inv = jnp.exp(-cs)
        g_end = jnp.exp(cs[C - 1:C, :] - cs)
        a2 = stack(a * jnp.exp(cs - lw))
        r2 = stack(r * jnp.exp(cs))
        b2 = stack(b * g_inv)
        k2 = stack(k * g_inv)
        v2 = stack(v)
        s_old = s_scr[q]
        ar = jnp.concatenate([a2, r2], axis=0).astype(BF16)
        bk = jnp.concatenate([b2, k2], axis=0).astype(BF16)
        nt = (((1,), (1,)), ((), ()))
        pp = lax.dot_general(ar, bk, nt, preferred_element_type=F32)
        from_state = lax.dot_general(ar, s_old.astype(BF16), nt, preferred_element_type=F32)
        yield
        l_ab = jnp.where(strict, pp[:2 * C, :2 * C], 0.0)
        l_ak = jnp.where(strict, pp[:2 * C, 2 * C:], 0.0)
        m_rb = jnp.where(lower, pp[2 * C:, :2 * C], 0.0)
        m_rk = jnp.where(lower, pp[2 * C:, 2 * C:], 0.0)
        v2b = v2.astype(BF16)
        rhs = from_state[:2 * C] + jnp.dot(l_ak.astype(BF16), v2b, preferred_element_type=F32)
        yield
        n = l_ab
        x = eye + n
        span = 2
        while span < C:
            n = _dot3(n, n)
            yield
            x = x + _dot3(n, x)
            yield
            span *= 2
        u2 = _dot3(x, rhs)
        yield
        uv = jnp.concatenate([u2, v2], axis=0).astype(BF16)
        y2 = from_state[2 * C:] + jnp.dot(jnp.concatenate([m_rb, m_rk], axis=1).astype(BF16), uv,
                                          preferred_element_type=F32)
        yield
        y_ref[sl, lanes] = y2[:C] + y2[C:]
        bk_end = jnp.concatenate([stack(b * g_end), stack(k * g_end)], axis=0).astype(BF16)
        s_scr[q] = s_old * jnp.exp(cs[C - 1:C, :]) + lax.dot_general(
            uv, bk_end, (((0,), (0,)), ((), ())), preferred_element_type=F32)

    lax.fori_loop(0, n_chunks, chunk, 0)

    @pl.when(pl.program_id(2) == pl.num_programs(2) - 1)
    def _():
        sT_ref[...] = s_scr[...]


def wkv7_chunked(r, lw, k, v, a, b, s0):
    B, T, W = r.shape
    n_pair = W // WKV_PAIR
    tt = min(WKV_T_TILE, T)
    s0p = s0.astype(F32).reshape(B, n_pair, 2, RWKV_HD, RWKV_HD)
    zero = jnp.zeros_like(s0p[:, :, 0])
    s0_bd = jnp.concatenate([jnp.concatenate([s0p[:, :, 0], zero], axis=-1),
                             jnp.concatenate([zero, s0p[:, :, 1]], axis=-1)], axis=-2)
    pps = WKV_PAIRS_PER_STEP
    seq = pl.BlockSpec((None, tt, pps * WKV_PAIR), lambda i, p, t: (i, t, p))
    st = pl.BlockSpec((None, pps, WKV_PAIR, WKV_PAIR), lambda i, p, t: (i, p, 0, 0))
    y, s_bd = pl.pallas_call(
        _wkv_kernel,
        grid=(B, n_pair // pps, T // tt),
        in_specs=[seq] * 6 + [st],
        out_specs=[seq, st],
        out_shape=[jax.ShapeDtypeStruct((B, T, W), F32),
                   jax.ShapeDtypeStruct((B, n_pair, WKV_PAIR, WKV_PAIR), F32)],
        scratch_shapes=[pltpu.VMEM((pps, WKV_PAIR, WKV_PAIR), F32)],
        compiler_params=pltpu.CompilerParams(
            dimension_semantics=("parallel", "parallel", "arbitrary"), vmem_limit_bytes=VMEM_LIMIT_BYTES),
        name="wkv7_chunked",
    )(r, lw, k, v, a, b, s0_bd)
    s_fin = jnp.stack([s_bd[:, :, :RWKV_HD, :RWKV_HD], s_bd[:, :, RWKV_HD:, RWKV_HD:]], axis=2)
    return y, s_fin.reshape(B, W // RWKV_HD, RWKV_HD, RWKV_HD)


AB_PAD = _round_up(AB_COLS, COL_TILE)
SHIFT_PAD = _round_up(SHIFT_W, LANE)
LORA_PAD = SHIFT_PAD - 3 * RWKV_W
EVEN_ROWS = 128
N_EVEN_PRE_OUT = 10


def _split3(x):
    hi = x.astype(BF16)
    r1 = x - hi.astype(F32)
    mid = r1.astype(BF16)
    return hi, mid, (r1 - mid.astype(F32)).astype(BF16)


def _dot_01(x, m):
    return sum(jnp.dot(part, m, preferred_element_type=F32) for part in _split3(x))


def _head_sum(x, red_ref, exp_ref):
    return _dot_01(_dot_01(x, red_ref[...]), exp_ref[...])


def _expm1(x):
    u = jnp.exp(x)
    d = u - 1.0
    log_u = jnp.where((d == 0.0) | (d == -1.0), 1.0, jnp.log(u))
    return jnp.where(d == 0.0, x, jnp.where(d == -1.0, -1.0, d * x / log_u))


def _even_pre_math(x_ref, prev, taps, prm, outs):
    (cw_ref, cb_ref, wa_ref, ba_ref, wx_ref, bx_ref, lam_ref, mu_ref, w0_ref, a0_ref, wl_ref,
     kk_ref, ka_ref, red_ref, exp_ref) = prm
    a_o, u_o, gate_o, r_o, lw_o, k_o, v_o, na_o, nb_o, g_o = outs
    t1, t2, t3 = taps
    xb = x_ref[:, 0:LRU_W]
    xc = cb_ref[...] + cw_ref[0:1] * t3 + cw_ref[1:2] * t2 + cw_ref[2:3] * t1 + cw_ref[3:4] * xb
    xcb = xc.astype(BF16)
    gate_r = jax.nn.sigmoid(jnp.dot(xcb, wa_ref[...], preferred_element_type=F32) + ba_ref[...])
    gate_i = jax.nn.sigmoid(jnp.dot(xcb, wx_ref[...], preferred_element_type=F32) + bx_ref[...])
    log_a = -LRU_C * gate_r * lam_ref[...]
    a_o[...] = jnp.exp(log_a)
    u_o[...] = jnp.sqrt(-_expm1(2.0 * log_a)) * (gate_i * xc)
    gate_o[...] = jax.nn.gelu(x_ref[:, LRU_W:2 * LRU_W])
    rw = x_ref[:, 2 * LRU_W:2 * LRU_W + SHIFT_PAD]
    rs = rw + mu_ref[...] * (prev - rw)
    r_o[...] = rs[:, 0:RWKV_W]
    k = rs[:, RWKV_W:2 * RWKV_W]
    v_o[...] = rs[:, 2 * RWKV_W:3 * RWKV_W]
    tail = rs[:, 3 * RWKV_W:]
    lane = lax.broadcasted_iota(jnp.int32, tail.shape, 1)
    act = jnp.where(lane < W_LORA, jnp.tanh(tail), jnp.where(lane < W_LORA + A_LORA, tail, jax.nn.sigmoid(tail)))
    z = jnp.dot(act.astype(BF16), wl_ref[...], preferred_element_type=F32)
    w_log = -jax.nn.softplus(-(w0_ref[...] + z[:, 0:RWKV_W])) - 0.5
    lw_o[...] = -jnp.exp(w_log)
    a_icl = jax.nn.sigmoid(a0_ref[...] + z[:, RWKV_W:2 * RWKV_W])
    g_o[...] = z[:, 2 * RWKV_W:]
    kk = k * kk_ref[...]
    kk = kk / jnp.maximum(jnp.sqrt(_head_sum(kk * kk, red_ref, exp_ref)), 1e-12)
    k_o[...] = k * (1.0 + (a_icl - 1.0) * ka_ref[...])
    na_o[...] = -kk
    nb_o[...] = kk * a_icl


def _even_pre_seq_kernel(x_ref, conv0_ref, shift0_ref, *refs):
    prm = refs[:15]
    outs = refs[15:15 + N_EVEN_PRE_OUT]
    conv_c, shift_c = refs[15 + N_EVEN_PRE_OUT:]
    rows = x_ref.shape[0]

    @pl.when(pl.program_id(1) == 0)
    def _():
        conv_c[...] = conv0_ref[...]
        shift_c[...] = shift0_ref[...]

    xb = x_ref[:, 0:LRU_W]
    row = lax.broadcasted_iota(jnp.int32, xb.shape, 0)
    taps = []
    for j in (1, 2, 3):
        tap = pltpu.roll(xb, j, 0)
        for i in range(j):
            tap = jnp.where(row == i, conv_c[8 - j + i:9 - j + i, :], tap)
        taps.append(tap)
    rw = x_ref[:, 2 * LRU_W:2 * LRU_W + SHIFT_PAD]
    row_w = lax.broadcasted_iota(jnp.int32, rw.shape, 0)
    prev = jnp.where(row_w == 0, shift_c[7:8, :], pltpu.roll(rw, 1, 0))
    _even_pre_math(x_ref, prev, taps, prm, outs)
    conv_c[...] = x_ref[rows - 8:rows, 0:LRU_W]
    shift_c[...] = x_ref[rows - 8:rows, 2 * LRU_W:2 * LRU_W + SHIFT_PAD]


def _even_pre_step_kernel(x_ref, prev_ref, t1_ref, t2_ref, t3_ref, *refs):
    _even_pre_math(x_ref, prev_ref[...], (t1_ref[...], t2_ref[...], t3_ref[...]), refs[:15], refs[15:])


def _even_params(p):
    def bd(w):
        eye = jnp.eye(LRU_BLOCKS, dtype=w.dtype)
        return (eye[:, None, :, None] * w[:, :, None, :]).reshape(LRU_W, LRU_W).astype(BF16)
    row = lambda v: v.reshape(1, -1).astype(F32)
    wl = jnp.zeros((LORA_PAD, 3 * RWKV_W), F32)
    wl = wl.at[0:W_LORA, 0:RWKV_W].set(p['w2'])
    wl = wl.at[W_LORA:W_LORA + A_LORA, RWKV_W:2 * RWKV_W].set(p['a2'])
    wl = wl.at[W_LORA + A_LORA:W_LORA + A_LORA + G_LORA, 2 * RWKV_W:].set(p['g2'])
    head = np.arange(RWKV_W) // RWKV_HD
    red = jnp.asarray(head[:, None] == np.arange(LANE)[None, :], dtype=BF16)
    mu = jnp.pad(p['mu'], (0, SHIFT_PAD - SHIFT_W))
    return [p['conv_w'].astype(F32), row(p['conv_b']), bd(p['wa']), row(p['ba']), bd(p['wx']), row(p['bx']),
            row(jax.nn.softplus(-p['lam'].astype(F32))), row(mu), row(p['w0']), row(p['a0']), wl.astype(BF16),
            row(p['k_k']), row(p['k_a']), red, red.T]


def _const_spec(a, n_grid):
    return pl.BlockSpec(a.shape, lambda *_: (0,) * a.ndim)


def even_pre_seq(proj, conv0, shift0, prm, B, T):
    tr = EVEN_ROWS
    nt = T // tr
    conv_pad = jnp.pad(conv0.astype(F32), ((0, 0), (8 - (CONV_W - 1), 0), (0, 0)))
    shift_pad = jnp.pad(shift0.astype(F32)[:, None, :], ((0, 0), (7, 0), (0, SHIFT_PAD - SHIFT_W)))
    out_spec = pl.BlockSpec((tr, LRU_W), lambda b, t: (b * nt + t, 0))
    return pl.pallas_call(
        _even_pre_seq_kernel,
        grid=(B, nt),
        in_specs=[pl.BlockSpec((tr, AB_PAD), lambda b, t: (b * nt + t, 0)),
                  pl.BlockSpec((None, 8, LRU_W), lambda b, t: (b, 0, 0)),
                  pl.BlockSpec((None, 8, SHIFT_PAD), lambda b, t: (b, 0, 0))] + [_const_spec(a, 2) for a in prm],
        out_specs=[out_spec] * N_EVEN_PRE_OUT,
        out_shape=[jax.ShapeDtypeStruct((B * T, LRU_W), F32)] * N_EVEN_PRE_OUT,
        scratch_shapes=[pltpu.VMEM((8, LRU_W), F32), pltpu.VMEM((8, SHIFT_PAD), F32)],
        compiler_params=pltpu.CompilerParams(
            dimension_semantics=("parallel", "arbitrary"), vmem_limit_bytes=VMEM_LIMIT_BYTES),
        name="even_pre_seq",
    )(proj, conv_pad, shift_pad, *prm)


def even_pre_step(proj, row0, conv0, shift0, prm):
    n = conv0.shape[0]
    shift_pad = jnp.pad(shift0.astype(F32), ((0, 0), (0, SHIFT_PAD - SHIFT_W)))
    taps = [conv0[:, CONV_W - 1 - j].astype(F32) for j in (1, 2, 3)]
    full = lambda w: pl.BlockSpec((n, w), lambda i: (0, 0))
    return pl.pallas_call(
        _even_pre_step_kernel,
        grid=(1,),
        in_specs=[pl.BlockSpec((n, AB_PAD), lambda i: (row0 // n, 0)), full(SHIFT_PAD)] + [full(LRU_W)] * 3
        + [_const_spec(a, 1) for a in prm],
        out_specs=[full(LRU_W)] * N_EVEN_PRE_OUT,
        out_shape=[jax.ShapeDtypeStruct((n, LRU_W), F32)] * N_EVEN_PRE_OUT,
        compiler_params=pltpu.CompilerParams(
            dimension_semantics=("arbitrary",), vmem_limit_bytes=VMEM_LIMIT_BYTES),
        name="even_pre_step",
    )(proj, shift_pad, *taps, *prm)


def _even_post_kernel(hs_ref, gate_ref, y_ref, r_ref, k_ref, v_ref, g_ref, lng_ref, lnb_ref, rk_ref,
                      red_ref, exp_ref, *rest):
    o_ref = rest[-1]
    y = y_ref[...]
    mu = _head_sum(y, red_ref, exp_ref) * (1.0 / RWKV_HD)
    d = y - mu
    var = _head_sum(d * d, red_ref, exp_ref) * (1.0 / RWKV_HD)
    yn = d * lax.rsqrt(var + 64e-5) * lng_ref[...] + lnb_ref[...]
    bonus = _head_sum(r_ref[...] * k_ref[...] * rk_ref[...], red_ref, exp_ref) * v_ref[...]
    o_ref[:, 0:LRU_W] = (hs_ref[...] * gate_ref[...]).astype(o_ref.dtype)
    o_ref[:, LRU_W:] = ((yn + bonus) * g_ref[...]).astype(o_ref.dtype)


def even_post(hs, gate, y, r, k, v, g, p, red, n_total, row0, prior=None):
    n = hs.shape[0]
    tr = EVEN_ROWS
    row = lambda a: a.reshape(1, -1).astype(F32)
    consts = [row(p['ln_g']), row(p['ln_b']), row(p['r_k']), red, red.T]
    seq = pl.BlockSpec((tr, LRU_W), lambda i: (i, 0))
    args = [hs, gate, y, r, k, v, g] + consts
    in_specs = [seq] * 7 + [_const_spec(a, 1) for a in consts]
    aliases = {}
    if prior is not None:
        args.append(prior)
        in_specs.append(pl.BlockSpec(memory_space=pl.ANY))
        aliases = {len(args) - 1: 0}
    return pl.pallas_call(
        _even_post_kernel,
        grid=(n // tr,),
        in_specs=in_specs,
        out_specs=pl.BlockSpec((tr, D_MODEL), lambda i: (row0 // tr + i, 0)),
        out_shape=jax.ShapeDtypeStruct((n_total, D_MODEL), BF16),
        input_output_aliases=aliases,
        compiler_params=pltpu.CompilerParams(
            dimension_semantics=("parallel",), vmem_limit_bytes=VMEM_LIMIT_BYTES),
        name="even_post",
    )(*args)


def _retention_kernel(q_ref, k_ref, v_ref, dm_ref, rd_ref, kd_ref, sd_ref, o_ref, s_out_ref, s_scr):
    C = q_ref.shape[0]

    @pl.when(pl.program_id(2) == 0)
    def _():
        s_scr[...] = jnp.zeros(s_scr.shape, F32)

    lo = lax.broadcasted_iota(jnp.int32, (C, 2 * RET_DK), 1) < RET_DK

    def stack(x):
        return jnp.concatenate([jnp.where(lo, x, 0.0), jnp.where(lo, 0.0, x)], axis=0)

    q2 = stack(q_ref[...]).astype(BF16)
    k2 = stack(k_ref[...])
    v2 = jnp.concatenate([v_ref[:, 0:RET_DV], v_ref[:, RET_DV:]], axis=0).astype(BF16)
    s = lax.dot_general(q2, k2.astype(BF16), (((1,), (1,)), ((), ())), preferred_element_type=F32) * dm_ref[...]
    s_old = s_scr[...]
    o2 = jnp.dot(s.astype(BF16), v2, preferred_element_type=F32) + jnp.dot(
        q2, s_old.astype(BF16), preferred_element_type=F32) * rd_ref[...]
    o_ref[:, 0:RET_DV] = o2[:C]
    o_ref[:, RET_DV:] = o2[C:]
    s_new = s_old * sd_ref[...] + lax.dot_general((k2 * kd_ref[...]).astype(BF16), v2, (((0,), (0,)), ((), ())),
                                                  preferred_element_type=F32)
    s_scr[...] = s_new

    @pl.when(pl.program_id(2) == pl.num_programs(2) - 1)
    def _():
        s_out_ref[...] = s_new


def retention_prompt_pallas(rq, rk, rv, B, T, v_col0=0):
    C = RET_CHUNK
    nc = T // C
    vb0 = v_col0 // (2 * RET_DV)
    f32 = F32
    lg = jnp.log1p(-jnp.exp2(-5.0 - jnp.arange(RET_H, dtype=f32))).reshape(RET_H // 2, 2)
    i = jnp.arange(C, dtype=f32)
    diff = i[:, None] - i[None, :]
    causal = diff >= 0
    dmask = jnp.where(causal, jnp.exp(jnp.where(causal, diff, 0.0)[None, None] * lg[:, :, None, None]), 0.0)
    zero = jnp.zeros_like(dmask[:, 0])
    dm = jnp.concatenate([jnp.concatenate([dmask[:, 0], zero], axis=-1),
                          jnp.concatenate([zero, dmask[:, 1]], axis=-1)], axis=-2)
    rows = lambda x, w: jnp.broadcast_to(x[:, :, :, None], x.shape + (w,)).reshape(RET_H // 2, -1, w)
    rd = rows(jnp.exp((i[None, None, :] + 1.0) * lg[:, :, None]), RET_DV)
    kd = rows(jnp.exp((C - 1.0 - i)[None, None, :] * lg[:, :, None]), 2 * RET_DK)
    sd = rows(jnp.broadcast_to(jnp.exp(C * lg)[:, :, None], (RET_H // 2, 2, RET_DK)), RET_DV)
    qk_spec = pl.BlockSpec((C, 2 * RET_DK), lambda b, p, c: (b * nc + c, p))
    v_spec = pl.BlockSpec((C, 2 * RET_DV), lambda b, p, c: (b * nc + c, vb0 + p))
    o_spec = pl.BlockSpec((C, 2 * RET_DV), lambda b, p, c: (b * nc + c, p))
    const = lambda a: pl.BlockSpec((None,) + a.shape[1:], lambda b, p, c: (p, 0, 0))
    o, s = pl.pallas_call(
        _retention_kernel,
        grid=(B, RET_H // 2, nc),
        in_specs=[qk_spec, qk_spec, v_spec, const(dm), const(rd), const(kd), const(sd)],
        out_specs=[o_spec, pl.BlockSpec((None, None, 2 * RET_DK, RET_DV), lambda b, p, c: (b, p, 0, 0))],
        out_shape=[jax.ShapeDtypeStruct((B * T, RET_W), f32),
                   jax.ShapeDtypeStruct((B, RET_H // 2, 2 * RET_DK, RET_DV), f32)],
        scratch_shapes=[pltpu.VMEM((2 * RET_DK, RET_DV), f32)],
        compiler_params=pltpu.CompilerParams(
            dimension_semantics=("parallel", "parallel", "arbitrary"), vmem_limit_bytes=VMEM_LIMIT_BYTES),
        name="retention_prompt",
    )(rq, rk, rv, dm, rd, kd, sd)
    return s.reshape(B, RET_H, RET_DK, RET_DV), o


KV_W = NSA_G * NSA_HD
RET_QK_W = RET_H * RET_DK
OFF_Q = 0
OFF_KC = OFF_Q + NSA_W
OFF_VC = OFF_KC + KV_W
OFF_KS = OFF_VC + KV_W
OFF_VS = OFF_KS + KV_W
OFF_KW = OFF_VS + KV_W
OFF_VW = OFF_KW + KV_W
OFF_RQ = OFF_VW + KV_W
OFF_RK = OFF_RQ + RET_QK_W
OFF_RV = OFF_RK + RET_QK_W
OFF_RG = OFF_RV + RET_W
OFF_GT = OFF_RG + RET_W
CD_PAD = _round_up(OFF_GT + LANE, COL_TILE)
ODD_ROWS = 128
N_ODD_PRE_OUT = 11


def _odd_weight_cols(w):
    gt0 = NSA_W + 6 * KV_W
    body = jnp.concatenate([w[:, :gt0], w[:, gt0 + 3 * NSA_H:]], axis=1)
    gt = w[:, gt0:gt0 + 3 * NSA_H]
    out = jnp.concatenate([body, gt], axis=1)
    return jnp.pad(out, ((0, 0), (0, CD_PAD - out.shape[1]))).astype(BF16)


def _rope_tables(pos, n_rot, theta, head):
    half = n_rot // 2
    inv = jnp.exp(-jnp.log(jnp.float32(theta)) * jnp.arange(half, dtype=jnp.float32) / half)
    ang = pos.astype(jnp.float32)[:, None] * inv[None, :]
    cos, sin = jnp.cos(ang), jnp.sin(ang)
    d = np.arange(LANE) % head
    cos_d, sin_d = cos[:, d % half], sin[:, d % half]
    c = jnp.where(d < n_rot, cos_d, 1.0)
    s1 = jnp.where(d < half, -sin_d, 0.0)
    s2 = jnp.where((d >= half) & (d < n_rot), sin_d, 0.0)
    return jnp.stack([c, s1, s2])


def _rope_lanes(x, tab_ref, half):
    w = x.shape[1]
    rep = w // LANE
    c, s1, s2 = (pltpu.repeat(tab_ref[i], rep, axis=1) for i in range(3))
    return x * c + pltpu.roll(x, w - half, 1) * s1 + pltpu.roll(x, half, 1) * s2


def _rms_heads(x, g_ref, red_ref, exp_ref):
    ms = _head_sum(x * x, red_ref, exp_ref) * (1.0 / NSA_HD)
    return x * lax.rsqrt(ms + 1e-6) * g_ref[...]


def _odd_pre_kernel(x_ref, nsa_tab, ret_tab, qg_ref, ksg_ref, kwg_ref, redq_ref, expq_ref, redk_ref, expk_ref,
                    tile_ref, qn_o, qr_o, ks_o, kw_o, ks4_o, vs4_o, kw4_o, vw4_o, gate_o, rq_o, rk_o):
    nsa_half = ROPE_DIMS // 2
    qn = _rms_heads(x_ref[:, OFF_Q:OFF_Q + NSA_W], qg_ref, redq_ref, expq_ref)
    qn_o[...] = qn
    qr_o[...] = _rope_lanes(qn, nsa_tab, nsa_half)
    ks = _rope_lanes(_rms_heads(x_ref[:, OFF_KS:OFF_KS + KV_W], ksg_ref, redk_ref, expk_ref), nsa_tab, nsa_half)
    kw = _rope_lanes(_rms_heads(x_ref[:, OFF_KW:OFF_KW + KV_W], kwg_ref, redk_ref, expk_ref), nsa_tab, nsa_half)
    ks_o[...] = ks
    kw_o[...] = kw
    tile = tile_ref[...]
    for src, dst in ((ks, ks4_o), (x_ref[:, OFF_VS:OFF_VS + KV_W], vs4_o), (kw, kw4_o),
                     (x_ref[:, OFF_VW:OFF_VW + KV_W], vw4_o)):
        dst[...] = jnp.dot(src.astype(BF16), tile, preferred_element_type=F32).astype(BF16)
    gate_o[...] = jax.nn.sigmoid(x_ref[:, OFF_GT:OFF_GT + LANE])
    rq_o[...] = _rope_lanes(x_ref[:, OFF_RQ:OFF_RQ + RET_QK_W], ret_tab, RET_DK // 2)
    rk_o[...] = _rope_lanes(x_ref[:, OFF_RK:OFF_RK + RET_QK_W], ret_tab, RET_DK // 2) * (RET_DK ** -0.5)


def odd_pre(proj, pos, p, row0, n_rows, same_pos):
    tr = ODD_ROWS
    blk0 = row0 // tr
    n_tab = tr if same_pos else n_rows
    pos_rows = jnp.broadcast_to(pos, (n_tab,)) if same_pos else pos
    nsa_tab = _rope_tables(pos_rows, ROPE_DIMS, ROPE_THETA, NSA_HD)
    ret_tab = _rope_tables(pos_rows, RET_DK, RET_THETA, RET_DK)
    row = lambda v, rep: jnp.tile(v.astype(F32), rep).reshape(1, -1)
    lanes = np.arange(LANE)
    red_q = jnp.asarray((np.arange(NSA_W) // NSA_HD)[:, None] == lanes[None, :], dtype=BF16)
    red_k = jnp.asarray((np.arange(KV_W) // NSA_HD)[:, None] == lanes[None, :], dtype=BF16)
    src = np.arange(KV_W)
    dst = np.arange(NSA_W)
    tile = jnp.asarray((src[:, None] // NSA_HD == dst[None, :] // GROUP_W)
                       & (src[:, None] % NSA_HD == dst[None, :] % NSA_HD), dtype=BF16)
    consts = [row(p['q_norm'], NSA_H), row(p['k_norm'][1], NSA_G), row(p['k_norm'][2], NSA_G),
              red_q, red_q.T, red_k, red_k.T, tile]
    tab_spec = pl.BlockSpec((3, tr, LANE), (lambda i: (0, 0, 0)) if same_pos else (lambda i: (0, i, 0)))
    out = lambda w, dt: (pl.BlockSpec((tr, w), lambda i: (i, 0)), jax.ShapeDtypeStruct((n_rows, w), dt))
    outs = [out(NSA_W, F32), out(NSA_W, F32), out(KV_W, F32), out(KV_W, F32)] + [out(NSA_W, BF16)] * 4 + [
        out(LANE, F32), out(RET_QK_W, F32), out(RET_QK_W, F32)]
    return pl.pallas_call(
        _odd_pre_kernel,
        grid=(n_rows // tr,),
        in_specs=[pl.BlockSpec((tr, CD_PAD), lambda i: (blk0 + i, 0)), tab_spec, tab_spec]
        + [_const_spec(a, 1) for a in consts],
        out_specs=[o[0] for o in outs],
        out_shape=[o[1] for o in outs],
        compiler_params=pltpu.CompilerParams(
            dimension_semantics=("parallel",), vmem_limit_bytes=VMEM_LIMIT_BYTES),
        name="odd_pre",
    )(proj, nsa_tab, ret_tab, *consts)


def _odd_post_kernel(oc_ref, os_ref, ow_ref, gate_ref, ret_ref, rg0_ref, rg1_ref, gng_ref, gnb_ref, ge_ref, *rest,
                     gated):
    o_ref = rest[-1]
    if gated:
        nsa = oc_ref[...]
    else:
        gates = gate_ref[...]
        nsa = jnp.zeros(oc_ref.shape, F32)
        for j, branch in enumerate((oc_ref, os_ref, ow_ref)):
            nsa = nsa + _dot_01(gates, ge_ref[j]) * branch[...]
    o_ref[:, 0:NSA_W] = nsa.astype(o_ref.dtype)
    for h in range(RET_H):
        lanes = slice(h * RET_DV, (h + 1) * RET_DV)
        x = ret_ref[:, lanes]
        mu = jnp.mean(x, axis=-1, keepdims=True)
        d = x - mu
        var = jnp.mean(d * d, axis=-1, keepdims=True)
        yn = d * lax.rsqrt(var + 1e-5) * gng_ref[:, lanes] + gnb_ref[:, lanes]
        rg = (rg0_ref if h < RET_H // 2 else rg1_ref)[:, (h % (RET_H // 2)) * RET_DV:(h % (RET_H // 2) + 1) * RET_DV]
        o_ref[:, NSA_W + h * RET_DV:NSA_W + (h + 1) * RET_DV] = (yn * (rg * jax.nn.sigmoid(rg))).astype(o_ref.dtype)


def odd_post(o_cmp, o_slc, o_win, gates, o_ret, proj, p, n_total, row0, prior=None, gated=False):
    n = o_cmp.shape[0]
    tr = ODD_ROWS
    blk0 = row0 // tr
    h = np.arange(NSA_W) // NSA_HD
    ge = jnp.asarray(np.stack([(np.arange(LANE)[:, None] == (3 * h + j)[None, :]) for j in range(3)]), dtype=BF16)
    row = lambda a: a.reshape(1, -1).astype(F32)
    consts = [row(p['gn_g']), row(p['gn_b']), ge]
    seq = lambda w: pl.BlockSpec((tr, w), lambda i: (i, 0))
    half = RET_W // 2
    rg_spec = lambda k: pl.BlockSpec((tr, half), lambda i: (blk0 + i, OFF_RG // half + k))
    args = [o_cmp, o_slc, o_win, gates, o_ret, proj, proj] + consts
    in_specs = [seq(NSA_W)] * 3 + [seq(LANE), seq(RET_W), rg_spec(0), rg_spec(1)] + [_const_spec(a, 1) for a in consts]
    aliases = {}
    if prior is not None:
        args.append(prior)
        in_specs.append(pl.BlockSpec(memory_space=pl.ANY))
        aliases = {len(args) - 1: 0}
    return pl.pallas_call(
        functools.partial(_odd_post_kernel, gated=gated),
        grid=(n // tr,),
        in_specs=in_specs,
        out_specs=pl.BlockSpec((tr, D_MODEL), lambda i: (blk0 + i, 0)),
        out_shape=jax.ShapeDtypeStruct((n_total, D_MODEL), BF16),
        input_output_aliases=aliases,
        compiler_params=pltpu.CompilerParams(
            dimension_semantics=("parallel",), vmem_limit_bytes=VMEM_LIMIT_BYTES),
        name="odd_post",
    )(*args)


def rms_norm(x, g, eps=1e-6):
    xf = x.astype(jnp.float32)
    y = xf * lax.rsqrt(jnp.mean(xf * xf, axis=-1, keepdims=True) + eps)
    return (y * g.astype(jnp.float32)).astype(x.dtype)


def head_group_norm(y, g, b, eps):
    yf = y.astype(jnp.float32)
    mu = jnp.mean(yf, axis=-1, keepdims=True)
    var = jnp.mean(jnp.square(yf - mu), axis=-1, keepdims=True)
    yn = ((yf - mu) * lax.rsqrt(var + eps)).reshape(y.shape[:-2] + (-1,))
    return (yn * g.astype(jnp.float32) + b.astype(jnp.float32)).astype(y.dtype)


def masked_softmax(s, mask):
    s = jnp.where(mask, s.astype(jnp.float32), -jnp.inf)
    m = jnp.max(s, axis=-1, keepdims=True)
    e = jnp.exp(s - jnp.where(jnp.isfinite(m), m, 0.0))
    den = jnp.sum(e, axis=-1, keepdims=True)
    return e / jnp.where(den > 0, den, 1.0)


def rope(x, pos, n_rot, theta):
    half = n_rot // 2
    inv = jnp.exp(-jnp.log(jnp.float32(theta)) * jnp.arange(half, dtype=jnp.float32) / half)
    ang = pos.astype(jnp.float32)[:, None] * inv[None, :]
    cos = jnp.cos(ang)[None, :, None, :]
    sin = jnp.sin(ang)[None, :, None, :]
    xf = x.astype(jnp.float32)
    x1, x2 = xf[..., :half], xf[..., half:n_rot]
    out = jnp.concatenate([x1 * cos - x2 * sin, x2 * cos + x1 * sin, xf[..., n_rot:]], axis=-1)
    return out.astype(x.dtype)


def linear_scan(a, b, h0):
    b = b.at[:, 0].add(a[:, 0] * h0)

    def combine(left, right):
        return left[0] * right[0], right[0] * left[1] + right[1]

    return lax.associative_scan(combine, (a, b), axis=1)[1]


def wkv7_scan(r, w, k, v, a, b, s0):
    xs = tuple(jnp.moveaxis(z.astype(jnp.float32), 1, 0) for z in (r, w, k, v, a, b))

    def step(S, inp):
        r_t, w_t, k_t, v_t, a_t, b_t = inp
        sa = jnp.einsum('bhij,bhj->bhi', S, a_t)
        S = S * w_t[:, :, None, :] + sa[..., None] * b_t[:, :, None, :] + v_t[..., None] * k_t[:, :, None, :]
        return S, jnp.einsum('bhij,bhj->bhi', S, r_t)

    S, ys = lax.scan(step, s0.astype(jnp.float32), xs)
    return jnp.moveaxis(ys, 0, 1), S


def even_mixer_core(proj, p, lru_h0, lru_conv0, shift0, wkv0):
    B, T, _ = proj.shape
    f32 = jnp.float32
    dt = proj.dtype
    xb, gb, rw = jnp.split(proj, [LRU_W, 2 * LRU_W], axis=-1)
    xcat = jnp.concatenate([lru_conv0.astype(dt), xb], axis=1)
    xc = p['conv_b'] + sum(p['conv_w'][j] * xcat[:, j:j + T] for j in range(CONV_W))
    xbd = xc.reshape(B, T, LRU_BLOCKS, LRU_BS)
    gate_r = jax.nn.sigmoid(jnp.einsum('btnc,ncd->btnd', xbd, p['wa']).reshape(B, T, LRU_W) + p['ba'])
    gate_i = jax.nn.sigmoid(jnp.einsum('btnc,ncd->btnd', xbd, p['wx']).reshape(B, T, LRU_W) + p['bx'])
    log_a = -LRU_C * gate_r.astype(f32) * jax.nn.softplus(-p['lam'].astype(f32))
    u = jnp.sqrt(-jnp.expm1(2.0 * log_a)) * (gate_i * xc).astype(f32)
    hs = lru_scan(jnp.exp(log_a), u, lru_h0.astype(f32))
    y_lru = hs.astype(dt) * jax.nn.gelu(gb)
    prev = jnp.concatenate([shift0.astype(dt)[:, None], rw[:, :-1]], axis=1)
    rs = rw + p['mu'] * (prev - rw)
    r, k, v, xw, xa, xg = jnp.split(
        rs, [RWKV_W, 2 * RWKV_W, 3 * RWKV_W, 3 * RWKV_W + W_LORA, 3 * RWKV_W + W_LORA + A_LORA], axis=-1)
    w_log = -jax.nn.softplus(-(p['w0'] + jnp.tanh(xw) @ p['w2']).astype(f32)) - 0.5
    log_decay = -jnp.exp(w_log)
    decay = jnp.exp(log_decay)
    a_icl = jax.nn.sigmoid(p['a0'] + xa @ p['a2'])
    g = jax.nn.sigmoid(xg) @ p['g2']
    heads = (B, T, RWKV_H, RWKV_HD)
    kk = (k * p['k_k']).reshape(heads).astype(f32)
    kk = kk / jnp.maximum(jnp.sqrt(jnp.sum(kk * kk, axis=-1, keepdims=True)), 1e-12)
    k = k * (1.0 + (a_icl - 1.0) * p['k_a'])
    rh, kh, vh, ah = (z.reshape(heads) for z in (r, k, v, a_icl))
    if T % WKV_C == 0:
        y, wkv = wkv7_chunked(r.astype(f32), log_decay, k.astype(f32), v.astype(f32),
                              (-kk).reshape(B, T, RWKV_W), (kk * ah.astype(f32)).reshape(B, T, RWKV_W), wkv0)
        y = y.reshape(heads)
    else:
        y, wkv = wkv7_scan(rh, decay.reshape(heads), kh, vh, -kk, kk * ah.astype(f32), wkv0)
    y = head_group_norm(y, p['ln_g'], p['ln_b'], 64e-5).astype(dt)
    bonus = (jnp.sum(rh * kh * p['r_k'], axis=-1, keepdims=True) * vh).reshape(B, T, RWKV_W)
    y_rwkv = (y + bonus) * g
    cat = jnp.concatenate([y_lru, y_rwkv], axis=-1)
    return cat, hs[:, -1], xcat[:, T:], rw[:, -1], wkv


def even_mixer(proj, p, B, T, DB, lru_h0, lru_conv0, shift0, wkv0):
    f32 = F32
    prm = _even_params(p)
    red = prm[-2]
    n_p = B * T
    zeros = lambda *s: jnp.zeros(s, f32)
    a, u, gate, r, lw, k, v, na, nb, g = even_pre_seq(proj, zeros(B, CONV_W - 1, LRU_W), zeros(B, SHIFT_W), prm, B, T)
    seq = lambda z: z.reshape(B, T, LRU_W)
    hs = lru_scan(seq(a), seq(u), zeros(B, LRU_W))
    yw, wkv_p = wkv7_chunked(seq(r), seq(lw), seq(k), seq(v), seq(na), seq(nb), zeros(B, RWKV_H, RWKV_HD, RWKV_HD))
    cat = even_post(hs.reshape(n_p, LRU_W), gate, yw.reshape(n_p, RWKV_W), r, k, v, g, p, red, n_p + DB, 0)
    last = jnp.arange(B)[:, None] * T + (T - (CONV_W - 1) + jnp.arange(CONV_W - 1))[None, :]
    st_p = (hs[:, -1], proj[last, :LRU_W], proj[last[:, -1], 2 * LRU_W:AB_COLS], wkv_p)
    a, u, gate, r, lw, k, v, na, nb, g = even_pre_step(proj, n_p, lru_conv0, shift0, prm)
    hs_s = a * lru_h0.astype(f32) + u
    heads = (DB, 1, RWKV_H, RWKV_HD)
    yw, wkv_s = wkv7_scan(r.reshape(heads), jnp.exp(lw).reshape(heads), k.reshape(heads), v.reshape(heads),
                          na.reshape(heads), nb.reshape(heads), wkv0)
    cat = even_post(hs_s, gate, yw.reshape(DB, RWKV_W), r, k, v, g, p, red, n_p + DB, n_p, prior=cat)
    xb_s = proj[n_p:]
    conv_s = jnp.concatenate([lru_conv0[:, 1:].astype(f32), xb_s[:, None, :LRU_W]], axis=1)
    st_s = (hs_s, conv_s, xb_s[:, 2 * LRU_W:AB_COLS], wkv_s)
    return cat, st_p, st_s


def odd_project(proj, p, pos):
    B, T, _ = proj.shape
    sizes = [NSA_W] + [NSA_G * NSA_HD] * 6 + [3 * NSA_H, RET_H * RET_DK, RET_H * RET_DK, RET_W, RET_W]
    q, kc, vc, ks, vs, kw, vw, gt, rq, rk, rv, rg = jnp.split(
        proj, np.cumsum(sizes).tolist(), axis=-1)[:len(sizes)]
    kvs = (B, T, NSA_G, NSA_HD)
    q_n = rms_norm(q.reshape(B, T, NSA_H, NSA_HD), p['q_norm'])
    return {
        'q_n': q_n,
        'q_r': rope(q_n, pos, ROPE_DIMS, ROPE_THETA),
        'kc': kc.reshape(kvs), 'vc': vc.reshape(kvs),
        'ks': rope(rms_norm(ks.reshape(kvs), p['k_norm'][1]), pos, ROPE_DIMS, ROPE_THETA),
        'vs': vs.reshape(kvs),
        'kw': rope(rms_norm(kw.reshape(kvs), p['k_norm'][2]), pos, ROPE_DIMS, ROPE_THETA),
        'vw': vw.reshape(kvs),
        'gates': jax.nn.sigmoid(gt).reshape(B, T, NSA_H, 3),
        'rq': rope(rq.reshape(B, T, RET_H, RET_DK), pos, RET_DK, RET_THETA),
        'rk': rope(rk.reshape(B, T, RET_H, RET_DK), pos, RET_DK, RET_THETA) * (RET_DK ** -0.5),
        'rv': rv.reshape(B, T, RET_H, RET_DV),
        'rg': rg,
    }


def to_groups_q(q):
    B, T = q.shape[:2]
    return jnp.moveaxis(q.reshape(B, T, NSA_G, NSA_HPG, NSA_HD), 1, 3)


def to_groups_k(k):
    return jnp.moveaxis(k, 1, 2)


def nsa_compress(x, w1, b1, w2, b2):
    B, L = x.shape[:2]
    n_chunk = L // CMP_STRIDE
    n_cmp = n_chunk - CMP_R + 1
    ch = x[:, :n_chunk * CMP_STRIDE].reshape(B, n_chunk, CMP_STRIDE, NSA_G, NSA_HD)
    ch = jnp.moveaxis(ch, 3, 2).reshape(B, n_chunk, NSA_G, CMP_STRIDE * NSA_HD)
    part = jnp.einsum('bngc,rch->bngrh', ch, w1)
    pre = b1 + sum(part[:, m:m + n_cmp, :, m] for m in range(CMP_R))
    return jax.nn.gelu(pre) @ w2 + b2


def nsa_compressed_branch(qn, kc_raw, vc_raw, p, q_pos):
    kc = to_groups_k(rms_norm(nsa_compress(kc_raw, *p['ck']), p['k_norm'][0]))
    vc = to_groups_k(nsa_compress(vc_raw, *p['cv']))
    s = jnp.einsum('bghqd,bgcd->bghqc', qn, kc) * NSA_HD ** -0.5
    ends = jnp.arange(kc.shape[2]) * CMP_STRIDE + CMP_BLOCK - 1
    prob = masked_softmax(s, ends[None, :] <= q_pos[:, None])
    return jnp.einsum('bghqc,bgcd->bghqd', prob.astype(vc.dtype), vc), prob


def cmp_sel_overlap(n_cmp, n_sel):
    cs = np.arange(n_cmp) * CMP_STRIDE
    ss = np.arange(n_sel) * SEL_BLOCK
    ov = np.minimum(cs[None] + CMP_BLOCK, ss[:, None] + SEL_BLOCK) - np.maximum(cs[None], ss[:, None])
    return jnp.asarray(np.clip(ov, 0, None) / CMP_BLOCK, dtype=jnp.float32)


def nsa_select(p_cmp, q_pos, n_sel):
    imp = jnp.einsum('bgqc,sc->bgqs', p_cmp.sum(axis=2), cmp_sel_overlap(p_cmp.shape[-1], n_sel))
    j = jnp.arange(n_sel)[None, :]
    qb = (q_pos // SEL_BLOCK)[:, None]
    valid = j <= qb
    forced = (j == 0) | (j == qb) | (j == qb - 1)
    score = jnp.where(valid, jnp.where(forced, FORCE_SCORE, imp), -jnp.inf)
    _, idx = lax.top_k(score, min(SEL_TOP, n_sel))
    sel_ok = jnp.take_along_axis(jnp.broadcast_to(valid, score.shape), idx, axis=-1)
    return idx, sel_ok


def sel_blocks(x, n_sel):
    B, L = x.shape[:2]
    x = jnp.pad(x, ((0, 0), (0, n_sel * SEL_BLOCK - L), (0, 0), (0, 0)))
    return jnp.moveaxis(x.reshape(B, n_sel, SEL_BLOCK, NSA_G, NSA_HD), 3, 1)


def nsa_slc_attend(q, kb, vb, idx, sel_ok, q_pos):
    B, G = kb.shape[:2]
    bi = jnp.arange(B)[:, None, None, None]
    gi = jnp.arange(G)[None, :, None, None]
    kg = kb[bi, gi, idx]
    vg = vb[bi, gi, idx]
    s = jnp.einsum('bghqd,bgqnld->bghqnl', q, kg) * NSA_HD ** -0.5
    kpos = idx[..., None] * SEL_BLOCK + jnp.arange(SEL_BLOCK)
    mask = (kpos <= q_pos[None, None, :, None, None]) & sel_ok[..., None]
    sh = s.shape
    prob = masked_softmax(s.reshape(sh[:4] + (-1,)), mask.reshape(B, G, 1, sh[3], -1))
    return jnp.einsum('bghqnl,bgqnld->bghqd', prob.reshape(sh).astype(vg.dtype), vg)


def window_attend_banded(q, k, v):
    B, G, HPG, T, HD = q.shape
    nb = T // WIN_BLOCK
    npv = WINDOW // WIN_BLOCK
    pad = ((0, 0), (0, 0), (npv * WIN_BLOCK, 0), (0, 0))

    def band(z):
        zb = jnp.pad(z, pad).reshape(B, G, nb + npv, WIN_BLOCK, HD)
        return jnp.concatenate([zb[:, :, j:j + nb] for j in range(npv + 1)], axis=3)

    kb, vb = band(k), band(v)
    qb = q.reshape(B, G, HPG, nb, WIN_BLOCK, HD)
    s = jnp.einsum('bghiqd,bgikd->bghiqk', qb, kb) * NSA_HD ** -0.5
    blk = jnp.arange(nb)[:, None]
    q_pos = blk * WIN_BLOCK + jnp.arange(WIN_BLOCK)[None]
    k_pos = (blk - npv) * WIN_BLOCK + jnp.arange((npv + 1) * WIN_BLOCK)[None]
    diff = q_pos[:, :, None] - k_pos[:, None, :]
    mask = (diff >= 0) & (diff < WINDOW) & (k_pos[:, None, :] >= 0)
    prob = masked_softmax(s, mask)
    return jnp.einsum('bghiqk,bgikd->bghiqd', prob.astype(v.dtype), vb).reshape(B, G, HPG, T, HD)


def window_attend_cached(q, k, v, q_pos, k_pos):
    s = jnp.einsum('bghqd,blgd->bghql', q, k) * NSA_HD ** -0.5
    diff = q_pos[:, None] - k_pos[None, :]
    prob = masked_softmax(s, (diff >= 0) & (diff < WINDOW))
    return jnp.einsum('bghql,blgd->bghqd', prob.astype(v.dtype), v)


def retention_chunk(S, q, k, v):
    f32 = jnp.float32
    C = q.shape[1]
    lg = jnp.log1p(-jnp.exp2(-5.0 - jnp.arange(RET_H, dtype=f32)))
    i = jnp.arange(C, dtype=f32)
    diff = i[:, None] - i[None, :]
    causal = diff >= 0
    dmask = jnp.where(causal, jnp.exp(jnp.where(causal, diff, 0.0)[None] * lg[:, None, None]), 0.0)
    qf, kf, vf = q.astype(f32), k.astype(f32), v.astype(f32)
    s = jnp.einsum('bihd,bjhd->bhij', qf, kf) * dmask
    o = jnp.einsum('bhij,bjhe->bihe', s, vf)
    o = o + jnp.einsum('bihd,bhde->bihe', qf, S) * jnp.exp((i[:, None] + 1.0) * lg[None, :])[None, :, :, None]
    k_dec = kf * jnp.exp((C - 1.0 - i)[:, None] * lg[None, :])[None, :, :, None]
    S = S * jnp.exp(C * lg)[None, :, None, None] + jnp.einsum('bjhd,bjhe->bhde', k_dec, vf)
    return S, o


def retention_prompt(q, k, v):
    B, T = q.shape[:2]
    n = T // RET_CHUNK
    xs = tuple(jnp.moveaxis(z.reshape((B, n, RET_CHUNK) + z.shape[2:]), 1, 0) for z in (q, k, v))
    s0 = jnp.zeros((B, RET_H, RET_DK, RET_DV), jnp.float32)
    S, o = lax.scan(lambda S, c: retention_chunk(S, c[0], c[1], c[2]), s0, xs)
    return S, jnp.moveaxis(o, 0, 1).reshape(B, T, RET_H, RET_DV)


def odd_output(o_cmp, o_slc, o_win, o_ret, pr, p):
    gates = pr['gates']
    B, T = gates.shape[:2]
    gg = jnp.moveaxis(gates.reshape(B, T, NSA_G, NSA_HPG, 3), 1, 3)[..., None]
    o = gg[..., 0, :] * o_cmp + gg[..., 1, :] * o_slc + gg[..., 2, :] * o_win
    o_nsa = jnp.moveaxis(o, 3, 1).reshape(B, T, NSA_W)
    y_ret = head_group_norm(o_ret, p['gn_g'], p['gn_b'], 1e-5).astype(o_nsa.dtype) * jax.nn.silu(pr['rg'])
    return jnp.concatenate([o_nsa, y_ret], axis=-1)


def odd_mixer_prompt(proj, p):
    B, T, _ = proj.shape
    pos = jnp.arange(T)
    pr = odd_project(proj, p, pos)
    qn = pr['q_n'].reshape(B, T, NSA_W)
    qr = pr['q_r'].reshape(B, T, NSA_W)
    kc = rms_norm(nsa_compress(pr['kc'], *p['ck']), p['k_norm'][0])
    vc = nsa_compress(pr['vc'], *p['cv'])
    n_cmp = kc.shape[1]
    n_sel = -(-T // SEL_BLOCK)
    o_cmp, sel = nsa_cmp_select(qn, _tile_cmp(kc), _tile_cmp(vc), _overlap_T(n_cmp, n_sel),
                                n_cmp=n_cmp, n_sel=n_sel, q_pos0=0)
    o_slc = nsa_flash(qr, _tile_groups(pr['ks']), _tile_groups(pr['vs']), sel, _sel_expand(T))
    o_win = nsa_flash(qr, _tile_groups(pr['kw']), _tile_groups(pr['vw']))
    S, o_ret = retention_prompt_pallas(pr['rq'].reshape(B, T, -1), pr['rk'].reshape(B, T, -1),
                                       pr['rv'].reshape(B, T, -1))
    o_ret = o_ret.reshape(B, T, RET_H, RET_DV)
    gates = pr['gates']
    heads = (B, T, NSA_H, NSA_HD)
    o_nsa = (gates[..., 0:1] * o_cmp.reshape(heads) + gates[..., 1:2] * o_slc.reshape(heads)
             + gates[..., 2:3] * o_win.reshape(heads)).reshape(B, T, NSA_W)
    y_ret = head_group_norm(o_ret, p['gn_g'], p['gn_b'], 1e-5).astype(o_nsa.dtype) * jax.nn.silu(pr['rg'])
    out = jnp.concatenate([o_nsa, y_ret], axis=-1)
    kv_rows = jnp.stack([pr['kc'], pr['vc'], pr['ks'], pr['vs']], axis=2)
    win = jnp.stack([pr['kw'], pr['vw']], axis=2)[:, T - min(WINDOW, T):]
    return out, kv_rows, win, S


def odd_mixer_sample(proj, p, cache_layer, page_table, win_buf, ret_s0):
    B, T, _ = proj.shape
    assert T == DEC_SEQ == 1 and win_buf.shape[1] == WIN_BUF
    pos = PAST_LEN + jnp.arange(T)
    pr = odd_project(proj, p, pos)
    scale = NSA_HD ** -0.5
    new_rows = jnp.stack([pr['ks'], pr['vs'], pr['kw'], pr['vw']], axis=2)[:, 0].reshape(B, 4, SLOT_ROWS)
    cache_t = jnp.transpose(cache_layer, (0, 2, 3, 4, 1)).reshape(cache_layer.shape[0], KV_ROWS, PAGE_SIZE)
    win_t = jnp.transpose(win_buf, (0, 2, 3, 4, 1)).reshape(B, 2 * SLOT_ROWS, WIN_BUF)
    wk = _dec_cmp_weights(*p['ck'])
    wv = _dec_cmp_weights(*p['cv'])
    w1t, b1, w2t, b2t = (jnp.stack([a, b]) for a, b in zip(wk, wv))
    kn = jnp.tile(p['k_norm'][0], NSA_G).reshape(1, SLOT_ROWS)
    t = np.arange(PAST_LEN)
    expand = jnp.asarray(np.arange(CMP_PAD)[:, None] == (t // SEL_BLOCK)[None, :], dtype=BF16)
    h = np.arange(NSA_H)
    grp = jnp.asarray((h[:, None] // NSA_HPG) == (h[None, :] // NSA_HPG), dtype=BF16)
    o16 = dec_nsa(page_table, cache_t, win_t, _place_heads(pr['q_n'][:, 0] * scale),
                  _place_heads(pr['q_r'][:, 0] * scale), new_rows, pr['gates'][:, 0],
                  w1t, b1, w2t, b2t, kn, _overlap_T(DEC_N_CMP, DEC_N_SEL), expand, grp)
    o_nsa = _take_heads(o16)[:, None, :]
    S, o_ret = retention_chunk(ret_s0.astype(jnp.float32), pr['rq'], pr['rk'], pr['rv'])
    y_ret = head_group_norm(o_ret, p['gn_g'], p['gn_b'], 1e-5).astype(o_nsa.dtype) * jax.nn.silu(pr['rg'])
    out = jnp.concatenate([o_nsa, y_ret], axis=-1)
    rows = jnp.stack([pr['kc'], pr['vc'], pr['ks'], pr['vs']], axis=2).astype(cache_layer.dtype)
    new_col = jnp.stack([pr['kw'], pr['vw']], axis=2)[:, 0].reshape(B, 2 * SLOT_ROWS, 1).astype(win_buf.dtype)
    win_new = jnp.concatenate([win_t[:, :, T:], new_col], axis=2).reshape(B, 2, NSA_G, NSA_HD, WIN_BUF)
    return out, rows, jnp.transpose(win_new, (0, 4, 1, 2, 3)), S


def odd_mixer(proj, p, B, T, DB, cache_layer, page_table, win_buf, ret_s0):
    assert DEC_SEQ == 1 and win_buf.shape[1] == WIN_BUF
    n_p = B * T
    kv = (B, T, NSA_G, NSA_HD)
    cols = lambda rows, off, w: proj[rows, off:off + w]
    prompt = slice(0, n_p)
    dec = slice(n_p, n_p + DB)
    qn, qr, ks, kw, ks4, vs4, kw4, vw4, gates, rq, rk = odd_pre(proj, jnp.tile(jnp.arange(T), B), p, 0, n_p, False)
    seq = lambda z: z.reshape(B, T, -1)
    kc_raw = cols(prompt, OFF_KC, KV_W).reshape(kv)
    vc_raw = cols(prompt, OFF_VC, KV_W).reshape(kv)
    kc = rms_norm(nsa_compress(kc_raw, *p['ck']), p['k_norm'][0])
    vc = nsa_compress(vc_raw, *p['cv'])
    n_cmp = kc.shape[1]
    n_sel = -(-T // SEL_BLOCK)
    o_cmp, sel = nsa_cmp_select(seq(qn), _tile_cmp(kc), _tile_cmp(vc), _overlap_T(n_cmp, n_sel),
                                n_cmp=n_cmp, n_sel=n_sel, q_pos0=0)
    o_slc = nsa_flash(seq(qr), seq(ks4), seq(vs4), sel, _sel_expand(T))
    o_win = nsa_flash(seq(qr), seq(kw4), seq(vw4))
    ret_p, o_ret = retention_prompt_pallas(rq, rk, proj, B, T, v_col0=OFF_RV)
    flat = lambda z: z.reshape(n_p, -1)
    cat = odd_post(flat(o_cmp), flat(o_slc), flat(o_win), gates, o_ret, proj, p, n_p + DB, 0)
    kv_rows_p = jnp.stack([kc_raw, vc_raw, ks.reshape(kv), cols(prompt, OFF_VS, KV_W).reshape(kv)], axis=2)
    win_p = jnp.stack([kw.reshape(kv), cols(prompt, OFF_VW, KV_W).reshape(kv)], axis=2)[:, T - min(WINDOW, T):]
    qn, qr, ks, kw, _, _, _, _, gates, rq, rk = odd_pre(proj, jnp.asarray(PAST_LEN), p, n_p, DB, True)
    scale = NSA_HD ** -0.5
    heads = lambda z: z.reshape(DB, NSA_H, NSA_HD)
    vs, vw = cols(dec, OFF_VS, KV_W), cols(dec, OFF_VW, KV_W)
    new_rows = jnp.stack([ks, vs, kw, vw], axis=1)
    cache_t = jnp.transpose(cache_layer, (0, 2, 3, 4, 1)).reshape(cache_layer.shape[0], KV_ROWS, PAGE_SIZE)
    win_t = jnp.transpose(win_buf, (0, 2, 3, 4, 1)).reshape(DB, 2 * SLOT_ROWS, WIN_BUF)
    w1t, b1, w2t, b2t = (jnp.stack([a, b]) for a, b in zip(_dec_cmp_weights(*p['ck']), _dec_cmp_weights(*p['cv'])))
    kn = jnp.tile(p['k_norm'][0], NSA_G).reshape(1, SLOT_ROWS)
    t = np.arange(PAST_LEN)
    expand = jnp.asarray(np.arange(CMP_PAD)[:, None] == (t // SEL_BLOCK)[None, :], dtype=BF16)
    h = np.arange(NSA_H)
    grp = jnp.asarray((h[:, None] // NSA_HPG) == (h[None, :] // NSA_HPG), dtype=BF16)
    o16 = dec_nsa(page_table, cache_t, win_t, _place_heads(heads(qn) * scale), _place_heads(heads(qr) * scale),
                  new_rows, gates[:, :3 * NSA_H].reshape(DB, NSA_H, 3),
                  w1t, b1, w2t, b2t, kn, _overlap_T(DEC_N_CMP, DEC_N_SEL), expand, grp)
    o_nsa = _take_heads(o16)
    ret_s, o_ret = retention_chunk(ret_s0.astype(F32), rq.reshape(DB, 1, RET_H, RET_DK),
                                   rk.reshape(DB, 1, RET_H, RET_DK), cols(dec, OFF_RV, RET_W).reshape(DB, 1, RET_H, RET_DV))
    cat = odd_post(o_nsa, o_nsa, o_nsa, gates, o_ret.reshape(DB, RET_W), proj, p, n_p + DB, n_p, prior=cat, gated=True)
    kvs = (DB, 1, NSA_G, NSA_HD)
    rows_s = jnp.stack([cols(dec, OFF_KC, KV_W).reshape(kvs), cols(dec, OFF_VC, KV_W).reshape(kvs),
                        ks.reshape(kvs), vs.reshape(kvs)], axis=2).astype(cache_layer.dtype)
    new_row = jnp.concatenate([kw, vw], axis=1)[:, None, :].astype(win_buf.dtype)
    win_new = win_shift(win_t, new_row).reshape(DB, 2, NSA_G, NSA_HD, WIN_BUF)
    win_s = jnp.transpose(win_new, (0, 4, 1, 2, 3))
    return cat, (kv_rows_p, win_p, ret_p), (rows_s, win_s, ret_s)


def _stack(xs, dt):
    return jnp.stack(xs).astype(dt)


def kernel(x_prompt, x_sample, state_lru_h, state_lru_conv, state_rwkv_shift, state_rwkv_wkv,
           cache_nsa_kv, cache_nsa_win, state_ret, page_table,
           norm_ffn1, ffn1_w_in, ffn1_w_out, norm_mix, norm_ffn2, ffn2_w_in, ffn2_w_out,
           ab_w_in, lru_conv_w, lru_conv_b, lru_wa, lru_ba, lru_wx, lru_bx, lru_lambda,
           rwkv_mu, rwkv_w0, rwkv_w2, rwkv_a0, rwkv_a2, rwkv_g2, rwkv_k_k, rwkv_k_a, rwkv_r_k,
           rwkv_ln_g, rwkv_ln_b, ab_w_out,
           cd_w_in, nsa_q_norm, nsa_k_norm, cmp_k_w1, cmp_k_b1, cmp_k_w2, cmp_k_b2,
           cmp_v_w1, cmp_v_b1, cmp_v_w2, cmp_v_b2, ret_gn_g, ret_gn_b, cd_w_out):
    dt = x_prompt.dtype
    B = x_prompt.shape[0]
    DB = x_sample.shape[0]
    y = jnp.concatenate([x_prompt.reshape(N_PROMPT, D_MODEL), x_sample.reshape(DB * DEC_SEQ, D_MODEL)], axis=0)
    lru_h_p, lru_h_s, lru_c_p, lru_c_s, sh_p, sh_s, wkv_p, wkv_s = [], [], [], [], [], [], [], []
    kv_p, kv_s, win_p, win_s, ret_p, ret_s = [], [], [], [], [], []
    for layer in range(DEPTH):
        li = layer // 2
        y = ffn_block(y, norm_ffn1[layer], *_prep_ffn_weights(ffn1_w_in, ffn1_w_out, layer))
        if layer % 2 == 0:
            p = {'conv_w': lru_conv_w[li], 'conv_b': lru_conv_b[li],
                 'wa': lru_wa[li], 'ba': lru_ba[li], 'wx': lru_wx[li], 'bx': lru_bx[li], 'lam': lru_lambda[li],
                 'mu': rwkv_mu[li], 'w0': rwkv_w0[li], 'w2': rwkv_w2[li], 'a0': rwkv_a0[li], 'a2': rwkv_a2[li],
                 'g2': rwkv_g2[li], 'k_k': rwkv_k_k[li], 'k_a': rwkv_k_a[li], 'r_k': rwkv_r_k[li],
                 'ln_g': rwkv_ln_g[li], 'ln_b': rwkv_ln_b[li]}
            proj = norm_matmul(y, norm_mix[layer], _prep_cols(ab_w_in[li]))
            cat, (a0, a1, a2, a3), (b0, b1, b2, b3) = even_mixer(
                proj, p, B, SEQ, DB, state_lru_h[li], state_lru_conv[li], state_rwkv_shift[li], state_rwkv_wkv[li])
            lru_h_p.append(a0); lru_c_p.append(a1); sh_p.append(a2); wkv_p.append(a3)
            lru_h_s.append(b0); lru_c_s.append(b1); sh_s.append(b2); wkv_s.append(b3)
            w_out = ab_w_out[li]
        else:
            p = {'q_norm': nsa_q_norm[li], 'k_norm': nsa_k_norm[li],
                 'ck': (cmp_k_w1[li], cmp_k_b1[li], cmp_k_w2[li], cmp_k_b2[li]),
                 'cv': (cmp_v_w1[li], cmp_v_b1[li], cmp_v_w2[li], cmp_v_b2[li]),
                 'gn_g': ret_gn_g[li], 'gn_b': ret_gn_b[li]}
            proj = norm_matmul(y, norm_mix[layer], _odd_weight_cols(cd_w_in[li]))
            cat, (a0, a1, a2), (b0, b1, b2) = odd_mixer(
                proj, p, B, SEQ, DB, cache_nsa_kv[li], page_table, cache_nsa_win[li], state_ret[li])
            kv_p.append(a0); win_p.append(a1); ret_p.append(a2)
            kv_s.append(b0); win_s.append(b1); ret_s.append(b2)
            w_out = cd_w_out[li]
        y = matmul_residual(cat, w_out.astype(BF16), y)
        y = ffn_block(y, norm_ffn2[layer], *_prep_ffn_weights(ffn2_w_in, ffn2_w_out, layer))
    yp = y[:N_PROMPT].reshape(B, SEQ, D_MODEL)
    ys = y[N_PROMPT:].reshape(DB, DEC_SEQ, D_MODEL)
    return (yp, ys,
            _stack(lru_h_p, dt), _stack(lru_h_s, dt), _stack(lru_c_p, dt), _stack(lru_c_s, dt),
            _stack(sh_p, dt), _stack(sh_s, dt), _stack(wkv_p, dt), _stack(wkv_s, dt),
            _stack(kv_p, dt), _stack(kv_s, dt), _stack(win_p, dt), _stack(win_s, dt),
            _stack(ret_p, dt), _stack(ret_s, dt))
```

```python
import functools

import jax
import jax.numpy as jnp
import numpy as np
from jax import lax
from jax.experimental import pallas as pl
from jax.experimental.pallas import tpu as pltpu

D_MODEL = 2048
BATCH = 4
SEQ = 2048
DEPTH = 2
DEC_BATCH = 128
DEC_SEQ = 1
PAST_LEN = 2048
PAGE_SIZE = 128
D_FF = 5504
LRU_W = D_MODEL // 2
LRU_BLOCKS = 16
LRU_BS = LRU_W // LRU_BLOCKS
CONV_W = 4
LRU_C = 8.0
RWKV_W = D_MODEL // 2
RWKV_HD = 64
RWKV_H = RWKV_W // RWKV_HD
W_LORA = 64
A_LORA = 64
G_LORA = 160
SHIFT_W = 3 * RWKV_W + W_LORA + A_LORA + G_LORA
AB_COLS = 2 * LRU_W + SHIFT_W
NSA_H = 16
NSA_G = 4
NSA_HPG = NSA_H // NSA_G
NSA_HD = 64
NSA_W = NSA_H * NSA_HD
ROPE_DIMS = NSA_HD // 4
ROPE_THETA = 500000.0
CMP_BLOCK = 32
CMP_STRIDE = 16
CMP_R = CMP_BLOCK // CMP_STRIDE
CMP_HID = 256
SEL_BLOCK = 64
SEL_TOP = 16
SEL_Q_BLOCK = 64
WINDOW = 512
WIN_BLOCK = 128
FORCE_SCORE = 1e4
KV_SLOTS = 4
RET_H = 8
RET_DK = 64
RET_DV = 128
RET_W = RET_H * RET_DV
RET_CHUNK = 128
RET_THETA = 10000.0
CD_COLS = NSA_W + 6 * NSA_G * NSA_HD + 3 * NSA_H + 2 * RET_H * RET_DK + 2 * RET_W

N_TOK = BATCH * SEQ + DEC_BATCH * DEC_SEQ
N_PROMPT = BATCH * SEQ

LANE = 128
VMEM_LIMIT_BYTES = 56 * 1024 * 1024
ROW_TILE = 640
FF_TILE = 512
D_FF_PAD = 5632
COL_TILE = 512
WIDE_COL_TILE = 2048

BF16 = jnp.bfloat16
F32 = jnp.float32


def _round_up(n, m):
    return -(-n // m) * m


def _rms_rows(x, g):
    ms = jnp.mean(x * x, axis=-1, keepdims=True)
    return x * lax.rsqrt(ms + 1e-6) * g


def _ffn_kernel(x_ref, g_ref, wg_ref, wu_ref, wo_ref, o_ref, xn_ref, acc_ref):
    k = pl.program_id(1)

    @pl.when(k == 0)
    def _():
        xn_ref[...] = _rms_rows(x_ref[...], g_ref[...]).astype(BF16)
        acc_ref[...] = jnp.zeros_like(acc_ref)

    xn = xn_ref[...]
    gate = jnp.dot(xn, wg_ref[...], preferred_element_type=F32)
    up = jnp.dot(xn, wu_ref[...], preferred_element_type=F32)
    act = gate * jax.nn.sigmoid(gate) * up
    acc_ref[...] += jnp.dot(act.astype(BF16), wo_ref[...], preferred_element_type=F32)

    @pl.when(k == pl.num_programs(1) - 1)
    def _():
        o_ref[...] = x_ref[...] + 0.5 * acc_ref[...]


def ffn_block(x, g, wg, wu, wo):
    m, d = x.shape
    return pl.pallas_call(
        _ffn_kernel,
        grid=(m // ROW_TILE, D_FF_PAD // FF_TILE),
        in_specs=[
            pl.BlockSpec((ROW_TILE, d), lambda i, k: (i, 0)),
            pl.BlockSpec((1, d), lambda i, k: (0, 0)),
            pl.BlockSpec((d, FF_TILE), lambda i, k: (0, k)),
            pl.BlockSpec((d, FF_TILE), lambda i, k: (0, k)),
            pl.BlockSpec((FF_TILE, d), lambda i, k: (k, 0)),
        ],
        out_specs=pl.BlockSpec((ROW_TILE, d), lambda i, k: (i, 0)),
        out_shape=jax.ShapeDtypeStruct((m, d), F32),
        scratch_shapes=[pltpu.VMEM((ROW_TILE, d), BF16), pltpu.VMEM((ROW_TILE, d), F32)],
        compiler_params=pltpu.CompilerParams(
            dimension_semantics=("parallel", "arbitrary"), vmem_limit_bytes=VMEM_LIMIT_BYTES),
        name="ffn_block",
    )(x, g.reshape(1, d), wg, wu, wo)


def _norm_matmul_kernel(x_ref, g_ref, w_ref, o_ref, xn_ref):
    @pl.when(pl.program_id(1) == 0)
    def _():
        xn_ref[...] = _rms_rows(x_ref[...], g_ref[...]).astype(BF16)

    o_ref[...] = jnp.dot(xn_ref[...], w_ref[...], preferred_element_type=F32)


def norm_matmul(x, g, w, tn=COL_TILE):
    m, k = x.shape
    n = w.shape[1]
    return pl.pallas_call(
        _norm_matmul_kernel,
        grid=(m // ROW_TILE, n // tn),
        in_specs=[
            pl.BlockSpec((ROW_TILE, k), lambda i, j: (i, 0)),
            pl.BlockSpec((1, k), lambda i, j: (0, 0)),
            pl.BlockSpec((k, tn), lambda i, j: (0, j)),
        ],
        out_specs=pl.BlockSpec((ROW_TILE, tn), lambda i, j: (i, j)),
        out_shape=jax.ShapeDtypeStruct((m, n), F32),
        scratch_shapes=[pltpu.VMEM((ROW_TILE, k), BF16)],
        compiler_params=pltpu.CompilerParams(
            dimension_semantics=("parallel", "arbitrary"), vmem_limit_bytes=VMEM_LIMIT_BYTES),
        name="norm_matmul",
    )(x, g.reshape(1, k), w)


def _matmul_residual_kernel(a_ref, w_ref, r_ref, o_ref):
    o_ref[...] = r_ref[...] + jnp.dot(a_ref[...].astype(BF16), w_ref[...], preferred_element_type=F32)


def matmul_residual(a, w, res, tn=COL_TILE):
    m, k = a.shape
    n = w.shape[1]
    return pl.pallas_call(
        _matmul_residual_kernel,
        grid=(m // ROW_TILE, n // tn),
        in_specs=[
            pl.BlockSpec((ROW_TILE, k), lambda i, j: (i, 0)),
            pl.BlockSpec((k, tn), lambda i, j: (0, j)),
            pl.BlockSpec((ROW_TILE, tn), lambda i, j: (i, j)),
        ],
        out_specs=pl.BlockSpec((ROW_TILE, tn), lambda i, j: (i, j)),
        out_shape=jax.ShapeDtypeStruct((m, n), F32),
        compiler_params=pltpu.CompilerParams(
            dimension_semantics=("parallel", "arbitrary"), vmem_limit_bytes=VMEM_LIMIT_BYTES),
        name="matmul_residual",
    )(a, w, res)


WCAST_ROWS = 256
WCAST_COLS = 512


def _cast_w_in_kernel(w_ref, wg_ref, wu_ref):
    pad = jnp.zeros((w_ref.shape[0], D_FF_PAD - D_FF), BF16)
    wg_ref[:, :D_FF] = w_ref[:, :D_FF].astype(BF16)
    wg_ref[:, D_FF:] = pad
    wu_ref[:, :D_FF] = w_ref[:, D_FF:].astype(BF16)
    wu_ref[:, D_FF:] = pad


def _cast_w_out_kernel(w_ref, wo_ref):
    wo_ref[:D_FF, :] = w_ref[...].astype(BF16)
    wo_ref[D_FF:, :] = jnp.zeros((D_FF_PAD - D_FF, w_ref.shape[1]), BF16)


def _prep_ffn_weights(w_in, w_out, layer):
    d = w_in.shape[1]
    wg, wu = pl.pallas_call(
        _cast_w_in_kernel,
        grid=(d // WCAST_ROWS,),
        in_specs=[pl.BlockSpec((None, WCAST_ROWS, 2 * D_FF), lambda i: (layer, i, 0))],
        out_specs=[pl.BlockSpec((WCAST_ROWS, D_FF_PAD), lambda i: (i, 0))] * 2,
        out_shape=[jax.ShapeDtypeStruct((d, D_FF_PAD), BF16)] * 2,
        compiler_params=pltpu.CompilerParams(dimension_semantics=("parallel",), vmem_limit_bytes=VMEM_LIMIT_BYTES),
        name="cast_w_in",
    )(w_in)
    wo = pl.pallas_call(
        _cast_w_out_kernel,
        grid=(d // WCAST_COLS,),
        in_specs=[pl.BlockSpec((None, D_FF, WCAST_COLS), lambda j: (layer, 0, j))],
        out_specs=pl.BlockSpec((D_FF_PAD, WCAST_COLS), lambda j: (0, j)),
        out_shape=jax.ShapeDtypeStruct((D_FF_PAD, d), BF16),
        compiler_params=pltpu.CompilerParams(dimension_semantics=("parallel",), vmem_limit_bytes=VMEM_LIMIT_BYTES),
        name="cast_w_out",
    )(w_out)
    return wg, wu, wo


def _prep_cols(w):
    n = w.shape[1]
    return jnp.pad(w, ((0, 0), (0, _round_up(n, COL_TILE) - n))).astype(BF16)


SCAN_TILE = 256


def _lru_scan_kernel(a_ref, b_ref, h0_ref, o_ref, carry_ref):
    @pl.when(pl.program_id(1) == 0)
    def _():
        carry_ref[...] = h0_ref[...]

    a = a_ref[...]
    b = b_ref[...]
    rows = lax.broadcasted_iota(jnp.int32, a.shape, 0)
    k = 1
    while k < a.shape[0]:
        keep = rows >= k
        b = jnp.where(keep, a * pltpu.roll(b, k, 0) + b, b)
        a = jnp.where(keep, a * pltpu.roll(a, k, 0), a)
        k *= 2
    h = a * carry_ref[...] + b
    o_ref[...] = h
    carry_ref[...] = h[a.shape[0] - 1:, :]


def lru_scan(a, b, h0):
    B, T, W = a.shape
    tt = min(SCAN_TILE, T)
    return pl.pallas_call(
        _lru_scan_kernel,
        grid=(B, T // tt),
        in_specs=[
            pl.BlockSpec((None, tt, W), lambda i, t: (i, t, 0)),
            pl.BlockSpec((None, tt, W), lambda i, t: (i, t, 0)),
            pl.BlockSpec((None, 1, W), lambda i, t: (i, 0, 0)),
        ],
        out_specs=pl.BlockSpec((None, tt, W), lambda i, t: (i, t, 0)),
        out_shape=jax.ShapeDtypeStruct((B, T, W), F32),
        scratch_shapes=[pltpu.VMEM((1, W), F32)],
        compiler_params=pltpu.CompilerParams(
            dimension_semantics=("parallel", "arbitrary"), vmem_limit_bytes=VMEM_LIMIT_BYTES),
        name="lru_scan",
    )(a, b, h0.reshape(B, 1, W))


GROUP_W = NSA_HPG * NSA_HD
ATT_Q_TILE = 128
ATT_K_TILE = 256
CMP_PAD = 128
NEG_BIG = -1e30


def _stack_heads(q):
    head = lax.broadcasted_iota(jnp.int32, q.shape, 1) // NSA_HD
    return jnp.concatenate([jnp.where(head == h, q, 0.0) for h in range(NSA_HPG)], axis=0)


def _unstack_heads(o, tq):
    head = lax.broadcasted_iota(jnp.int32, (tq, GROUP_W), 1) // NSA_HD
    out = jnp.zeros((tq, GROUP_W), F32)
    for h in range(NSA_HPG):
        out = out + jnp.where(head == h, o[h * tq:(h + 1) * tq], 0.0)
    return out


def _cmp_select_kernel(q_ref, k_ref, v_ref, ov_ref, o_ref, sel_ref, *, n_cmp, n_sel, q_pos0):
    tq = q_ref.shape[0]
    i = pl.program_id(2)
    qs = _stack_heads(q_ref[...] * (NSA_HD ** -0.5)).astype(BF16)
    s = lax.dot_general(qs, k_ref[...], (((1,), (1,)), ((), ())), preferred_element_type=F32)
    q_pos = q_pos0 + i * tq + lax.broadcasted_iota(jnp.int32, (tq, CMP_PAD), 0)
    c = lax.broadcasted_iota(jnp.int32, (tq, CMP_PAD), 1)
    mask1 = (c < n_cmp) & (c * CMP_STRIDE + (CMP_BLOCK - 1) <= q_pos)
    mask = jnp.concatenate([mask1] * NSA_HPG, axis=0)
    s = jnp.where(mask, s, NEG_BIG)
    m = jnp.max(s, axis=-1, keepdims=True)
    e = jnp.where(mask, jnp.exp(s - m), 0.0)
    den = jnp.sum(e, axis=-1, keepdims=True)
    prob = e / jnp.where(den > 0, den, 1.0)
    o = jnp.dot(prob.astype(BF16), v_ref[...], preferred_element_type=F32)
    o_ref[...] = _unstack_heads(o, tq)
    psum = prob[0:tq]
    for h in range(1, NSA_HPG):
        psum = psum + prob[h * tq:(h + 1) * tq]
    imp = jnp.dot(psum.astype(BF16), ov_ref[...], preferred_element_type=F32)
    qb = q_pos // SEL_BLOCK
    valid = (c <= qb) & (c < n_sel)
    forced = (c == 0) | (c == qb) | (c == qb - 1)
    score = jnp.where(valid, jnp.where(forced, FORCE_SCORE, imp), -jnp.inf)
    k_top = min(SEL_TOP, n_sel)
    few_blocks = (q_pos0 + (i + 1) * tq - 1) // SEL_BLOCK < k_top

    @pl.when(few_blocks)
    def _():
        sel_ref[...] = jnp.where(valid, 1.0, 0.0)

    @pl.when(jnp.logical_not(few_blocks))
    def _():
        rank = jnp.zeros((tq, CMP_PAD), F32)
        for jp in range(n_sel):
            col = score[:, jp:jp + 1]
            beats = (col > score) | ((col == score) & (c > jp))
            rank = rank + jnp.where(beats, 1.0, 0.0)
        sel_ref[...] = jnp.where((rank < k_top) & (c < n_sel), 1.0, 0.0)


def nsa_cmp_select(qn, kc4, vc4, ovT, *, n_cmp, n_sel, q_pos0):
    B, T, _ = qn.shape
    tq = min(ATT_Q_TILE, T)
    return pl.pallas_call(
        functools.partial(_cmp_select_kernel, n_cmp=n_cmp, n_sel=n_sel, q_pos0=q_pos0),
        grid=(B, NSA_G, T // tq),
        in_specs=[
            pl.BlockSpec((None, tq, GROUP_W), lambda b, g, i: (b, i, g)),
            pl.BlockSpec((None, None, CMP_PAD, GROUP_W), lambda b, g, i: (b, g, 0, 0)),
            pl.BlockSpec((None, None, CMP_PAD, GROUP_W), lambda b, g, i: (b, g, 0, 0)),
            pl.BlockSpec((CMP_PAD, CMP_PAD), lambda b, g, i: (0, 0)),
        ],
        out_specs=[
            pl.BlockSpec((None, tq, GROUP_W), lambda b, g, i: (b, i, g)),
            pl.BlockSpec((None, None, tq, CMP_PAD), lambda b, g, i: (b, g, i, 0)),
        ],
        out_shape=[jax.ShapeDtypeStruct((B, T, NSA_W), F32),
                   jax.ShapeDtypeStruct((B, NSA_G, T, CMP_PAD), F32)],
        compiler_params=pltpu.CompilerParams(
            dimension_semantics=("parallel", "parallel", "parallel"), vmem_limit_bytes=VMEM_LIMIT_BYTES),
        name="nsa_cmp_select",
    )(qn, kc4, vc4, ovT)


def _flash_kernel(*refs, selected):
    if selected:
        q_ref, k_ref, v_ref, sel_ref, exp_ref, o_ref, m_ref, l_ref, acc_ref, s_a, s_b = refs
    else:
        q_ref, k_ref, v_ref, o_ref, m_ref, l_ref, acc_ref, s_a, s_b = refs
    tq = q_ref.shape[0]
    tk = ATT_K_TILE
    n_tiles = k_ref.shape[0] // tk
    i = pl.program_id(2)
    q = q_ref[...] * (NSA_HD ** -0.5)
    head = lax.broadcasted_iota(jnp.int32, q.shape, 1) // NSA_HD
    q4 = _stack_heads(q).astype(BF16)
    m_ref[...] = jnp.full(m_ref.shape, NEG_BIG, F32)
    l_ref[...] = jnp.zeros(l_ref.shape, F32)
    acc_ref[...] = jnp.zeros(acc_ref.shape, F32)
    q_pos = i * tq + lax.broadcasted_iota(jnp.int32, (tq, tk), 0)
    col = lax.broadcasted_iota(jnp.int32, (tq, tk), 1)
    if selected:
        sel = sel_ref[...].astype(BF16)
        lo = 0
    else:
        lo = jnp.maximum(i * tq - (WINDOW - 1), 0) // tk
    hi = (i * tq + tq - 1) // tk + 1

    def tile_start(j):
        return pl.multiple_of(jnp.minimum(j, n_tiles - 1) * tk, tk)

    def scores(j, s_ref):
        s_ref[...] = lax.dot_general(q4, k_ref[pl.ds(tile_start(j), tk), :], (((1,), (1,)), ((), ())),
                                     preferred_element_type=F32)

    def consume(j, s_ref):
        v = v_ref[pl.ds(tile_start(j), tk), :]
        k_pos = j * tk + col
        mask = k_pos <= q_pos
        if selected:
            mask = mask & (jnp.dot(sel, exp_ref[jnp.minimum(j, n_tiles - 1)], preferred_element_type=F32) > 0.5)
        else:
            mask = mask & (q_pos - k_pos < WINDOW)
        bias = jnp.where(mask, 0.0, 2.0 * NEG_BIG)
        for h in range(NSA_HPG):
            s = s_ref[h * tq:(h + 1) * tq, :] + bias
            m_old = m_ref[h]
            m_new = jnp.maximum(m_old, jnp.max(s, axis=-1, keepdims=True))
            alpha = jnp.exp(m_old - m_new)
            p = jnp.exp(s - pltpu.repeat(m_new, tk // LANE, axis=1))
            l_ref[h] = alpha * l_ref[h] + jnp.sum(p, axis=-1, keepdims=True)
            acc_ref[h] = (pltpu.repeat(alpha, GROUP_W // LANE, axis=1) * acc_ref[h]
                          + jnp.dot(p.astype(BF16), v, preferred_element_type=F32))
            m_ref[h] = m_new

    scores(lo, s_a)

    def body(t, carry):
        j = lo + 2 * t
        scores(j + 1, s_b)
        consume(j, s_a)
        scores(j + 2, s_a)
        consume(j + 1, s_b)
        return carry

    lax.fori_loop(0, (hi - lo + 1) // 2, body, 0)
    out = jnp.zeros((tq, GROUP_W), F32)
    for h in range(NSA_HPG):
        den = pltpu.repeat(l_ref[h], GROUP_W // LANE, axis=1)
        out = out + jnp.where(head == h, acc_ref[h] / jnp.where(den > 0, den, 1.0), 0.0)
    o_ref[...] = out


def nsa_flash(qr, k4, v4, sel=None, expand=None):
    B, T, _ = qr.shape
    tq = ATT_Q_TILE
    selected = sel is not None
    in_specs = [
        pl.BlockSpec((None, tq, GROUP_W), lambda b, g, i: (b, i, g)),
        pl.BlockSpec((None, T, GROUP_W), lambda b, g, i: (b, 0, g)),
        pl.BlockSpec((None, T, GROUP_W), lambda b, g, i: (b, 0, g)),
    ]
    args = [qr, k4, v4]
    if selected:
        in_specs += [
            pl.BlockSpec((None, None, tq, CMP_PAD), lambda b, g, i: (b, g, i, 0)),
            pl.BlockSpec(expand.shape, lambda b, g, i: (0, 0, 0)),
        ]
        args += [sel, expand]
    return pl.pallas_call(
        functools.partial(_flash_kernel, selected=selected),
        grid=(B, NSA_G, T // tq),
        in_specs=in_specs,
        out_specs=pl.BlockSpec((None, tq, GROUP_W), lambda b, g, i: (b, i, g)),
        out_shape=jax.ShapeDtypeStruct((B, T, NSA_W), F32),
        scratch_shapes=[pltpu.VMEM((NSA_HPG, tq, LANE), F32), pltpu.VMEM((NSA_HPG, tq, LANE), F32),
                        pltpu.VMEM((NSA_HPG, tq, GROUP_W), F32),
                        pltpu.VMEM((NSA_HPG * tq, ATT_K_TILE), F32), pltpu.VMEM((NSA_HPG * tq, ATT_K_TILE), F32)],
        compiler_params=pltpu.CompilerParams(
            dimension_semantics=("parallel", "parallel", "parallel"), vmem_limit_bytes=VMEM_LIMIT_BYTES),
        name="nsa_flash_sel" if selected else "nsa_flash_win",
    )(*args)


def _tile_groups(x):
    B, T = x.shape[:2]
    return jnp.broadcast_to(x[:, :, :, None, :], (B, T, NSA_G, NSA_HPG, NSA_HD)).reshape(B, T, NSA_W).astype(BF16)


def _tile_cmp(x):
    B, n = x.shape[:2]
    x = jnp.pad(jnp.moveaxis(x, 1, 2), ((0, 0), (0, 0), (0, CMP_PAD - n), (0, 0)))
    return jnp.tile(x, (1, 1, 1, NSA_HPG)).astype(BF16)


def _overlap_T(n_cmp, n_sel):
    ov = np.zeros((CMP_PAD, CMP_PAD), np.float32)
    cs = np.arange(n_cmp) * CMP_STRIDE
    ss = np.arange(n_sel) * SEL_BLOCK
    o = np.minimum(cs[None] + CMP_BLOCK, ss[:, None] + SEL_BLOCK) - np.maximum(cs[None], ss[:, None])
    ov[:n_cmp, :n_sel] = (np.clip(o, 0, None) / CMP_BLOCK).T
    return jnp.asarray(ov, dtype=BF16)


def _sel_expand(T):
    t = np.arange(T)
    e = (np.arange(CMP_PAD)[:, None] == (t // SEL_BLOCK)[None, :]).astype(np.float32)
    return jnp.asarray(e.reshape(CMP_PAD, T // ATT_K_TILE, ATT_K_TILE).transpose(1, 0, 2), dtype=BF16)


KV_ROWS = KV_SLOTS * NSA_G * NSA_HD
SLOT_ROWS = NSA_G * NSA_HD
N_PAGES = PAST_LEN // PAGE_SIZE
DEC_N_CHUNK = (PAST_LEN + DEC_SEQ) // CMP_STRIDE
DEC_N_CMP = DEC_N_CHUNK - CMP_R + 1
DEC_N_SEL = -(-(PAST_LEN + DEC_SEQ) // SEL_BLOCK)
WIN_BUF = min(WINDOW, PAST_LEN)


def _softmax_rows(s, mask, s_new=None):
    s = jnp.where(mask, s, NEG_BIG)
    m = jnp.max(s, axis=-1, keepdims=True)
    if s_new is not None:
        m = jnp.maximum(m, s_new)
    e = jnp.where(mask, jnp.exp(s - m), 0.0)
    den = jnp.sum(e, axis=-1, keepdims=True)
    if s_new is None:
        return e, den
    e_new = jnp.exp(s_new - m)
    return e, e_new, den + e_new


def _dec_nsa_kernel(pt_ref, *refs):
    pages = refs[:N_PAGES]
    (win_ref, qn_ref, qr_ref, new_ref, gate_ref, w1_ref, b1_ref, w2_ref, b2_ref, kn_ref,
     ov_ref, exp_ref, grp_ref, o_ref, xt_ref, acc_ref) = refs[N_PAGES:]
    del pt_ref
    f32 = F32
    half = 2 * NSA_HD
    n_chunk = DEC_N_CHUNK

    for p in range(N_PAGES):
        for sg in range(4):
            xt_ref[sg, p * PAGE_SIZE:(p + 1) * PAGE_SIZE, :] = pages[p][sg * half:(sg + 1) * half, :].T

    lane_lo = lax.broadcasted_iota(jnp.int32, (n_chunk, 2 * half), 1) % half < NSA_HD
    lane_grp = lax.broadcasted_iota(jnp.int32, (n_chunk, SLOT_ROWS), 1) // NSA_HD
    cmp_rows = []
    for slot in range(2):
        for gp in range(2):
            for rp in range(CMP_STRIDE // 2):
                xr = jnp.concatenate(
                    [xt_ref[slot * 2 + gp, pl.ds(2 * rp + j, n_chunk, stride=CMP_STRIDE), :] for j in range(2)],
                    axis=1)
                xs = jnp.concatenate([jnp.where(lane_lo, xr, 0.0), jnp.where(lane_lo, 0.0, xr)],
                                     axis=0).astype(BF16)
                part = jnp.dot(xs, w1_ref[slot, rp], preferred_element_type=f32)
                rows = pl.ds(gp * 2 * n_chunk, 2 * n_chunk)
                if rp == 0:
                    acc_ref[rows, :] = part
                else:
                    acc_ref[rows, :] += part
        acc = acc_ref[...]
        pre = b1_ref[slot] + acc[:, :CMP_HID] + pltpu.roll(acc[:, CMP_HID:], NSA_G * n_chunk - 1, 0)
        out = jnp.dot(jax.nn.gelu(pre).astype(BF16), w2_ref[slot], preferred_element_type=f32) + b2_ref[slot]
        if slot == 0:
            out = _rms_rows(out, kn_ref[...])
        sel_rows = jnp.zeros((n_chunk, SLOT_ROWS), f32)
        for g in range(NSA_G):
            sel_rows = sel_rows + jnp.where(lane_grp == g, out[g * n_chunk:(g + 1) * n_chunk], 0.0)
        cmp_rows.append(sel_rows.astype(BF16))
    kc, vc = cmp_rows

    qn = qn_ref[...].astype(BF16)
    qr = qr_ref[...].astype(BF16)
    nt = (((1,), (1,)), ((), ()))
    c = lax.broadcasted_iota(jnp.int32, (NSA_H, CMP_PAD), 1)
    s = lax.dot_general(qn, kc, nt, preferred_element_type=f32)
    e, den = _softmax_rows(s, c < DEC_N_CMP)
    prob = e / jnp.where(den > 0, den, 1.0)
    o_cmp = jnp.dot(prob.astype(BF16), vc, preferred_element_type=f32)
    p_hi, p_mid = _split_bf16(prob)
    p_lo = (prob - p_hi.astype(f32) - p_mid.astype(f32)).astype(BF16)
    grp = grp_ref[...]
    psum = (jnp.dot(grp, p_hi, preferred_element_type=f32) + jnp.dot(grp, p_mid, preferred_element_type=f32)
            + jnp.dot(grp, p_lo, preferred_element_type=f32))
    imp = jnp.dot(psum.astype(BF16), ov_ref[...], preferred_element_type=f32)
    qb = (PAST_LEN + DEC_SEQ - 1) // SEL_BLOCK
    valid = c <= qb
    forced = (c == 0) | (c == qb) | (c == qb - 1)
    score = jnp.where(valid, jnp.where(forced, FORCE_SCORE, imp), -jnp.inf)
    rank = jnp.zeros((NSA_H, CMP_PAD), f32)
    for jp in range(DEC_N_SEL):
        col = score[:, jp:jp + 1]
        rank = rank + jnp.where((col > score) | ((col == score) & (c > jp)), 1.0, 0.0)
    sel = jnp.where((rank < min(SEL_TOP, DEC_N_SEL)) & (c < DEC_N_SEL), 1.0, 0.0).astype(BF16)

    new = new_ref[...]
    new_b = new.astype(BF16).astype(f32)
    qr_f = qr.astype(f32)
    s_pages = [jnp.dot(qr, pages[p][2 * SLOT_ROWS:3 * SLOT_ROWS, :].astype(BF16), preferred_element_type=f32)
               for p in range(N_PAGES)]
    s = jnp.concatenate(s_pages, axis=1)
    mask = jnp.dot(sel, exp_ref[...], preferred_element_type=f32) > 0.5
    s_new = jnp.sum(qr_f * new_b[0:1], axis=-1, keepdims=True)
    e, e_new, den = _softmax_rows(s, mask, s_new)
    e = e.astype(BF16)
    o_slc = e_new.astype(BF16).astype(f32) * new_b[1:2]
    for p in range(N_PAGES):
        o_slc = o_slc + lax.dot_general(e[:, p * PAGE_SIZE:(p + 1) * PAGE_SIZE],
                                        pages[p][3 * SLOT_ROWS:4 * SLOT_ROWS, :].astype(BF16), nt,
                                        preferred_element_type=f32)
    o_slc = o_slc / den

    s = jnp.dot(qr, win_ref[0:SLOT_ROWS, :].astype(BF16), preferred_element_type=f32)
    i_buf = lax.broadcasted_iota(jnp.int32, (NSA_H, WIN_BUF), 1)
    s_new = jnp.sum(qr_f * new_b[2:3], axis=-1, keepdims=True)
    e, e_new, den = _softmax_rows(s, WIN_BUF - i_buf < WINDOW, s_new)
    o_win = e_new.astype(BF16).astype(f32) * new_b[3:4] + lax.dot_general(
        e.astype(BF16), win_ref[SLOT_ROWS:2 * SLOT_ROWS, :].astype(BF16), nt, preferred_element_type=f32)
    o_win = o_win / den

    gates = gate_ref[...]
    o_ref[...] = gates[:, 0:1] * o_cmp + gates[:, 1:2] * o_slc + gates[:, 2:3] * o_win


def dec_nsa(page_table, cache_t, win_t, qn16, qr16, new_rows, gates, w1t, b1, w2t, b2t, kn, ovT, expand, grp):
    DB = qn16.shape[0]
    const = lambda shape: pl.BlockSpec(shape, lambda b, pt: (0,) * len(shape))
    per_b = lambda shape: pl.BlockSpec((None,) + shape, lambda b, pt: (b,) + (0,) * len(shape))
    page_specs = [pl.BlockSpec((None, KV_ROWS, PAGE_SIZE), functools.partial(lambda b, pt, p: (pt[b, p], 0, 0), p=p))
                  for p in range(N_PAGES)]
    in_specs = page_specs + [
        per_b((2 * SLOT_ROWS, WIN_BUF)), per_b((NSA_H, SLOT_ROWS)), per_b((NSA_H, SLOT_ROWS)),
        per_b((4, SLOT_ROWS)), per_b((NSA_H, 3)),
        const(w1t.shape), const(b1.shape), const(w2t.shape), const(b2t.shape), const(kn.shape),
        const(ovT.shape), const(expand.shape), const(grp.shape),
    ]
    grid_spec = pltpu.PrefetchScalarGridSpec(
        num_scalar_prefetch=1, grid=(DB,), in_specs=in_specs,
        out_specs=pl.BlockSpec((None, NSA_H, SLOT_ROWS), lambda b, pt: (b, 0, 0)),
        scratch_shapes=[pltpu.VMEM((4, PAST_LEN, 2 * NSA_HD), F32),
                        pltpu.VMEM((NSA_G * DEC_N_CHUNK, CMP_R * CMP_HID), F32)])
    return pl.pallas_call(
        _dec_nsa_kernel,
        grid_spec=grid_spec,
        out_shape=jax.ShapeDtypeStruct((DB, NSA_H, SLOT_ROWS), F32),
        compiler_params=pltpu.CompilerParams(
            dimension_semantics=("arbitrary",), vmem_limit_bytes=VMEM_LIMIT_BYTES),
        name="dec_nsa",
    )(page_table, *([cache_t] * N_PAGES), win_t, qn16, qr16, new_rows, gates, w1t, b1, w2t, b2t, kn, ovT, expand, grp)


def _win_shift_kernel(win_ref, new_ref, o_ref):
    w = win_ref[...]
    n = w.shape[1]
    row = lax.broadcasted_iota(jnp.int32, w.shape, 0)
    lane = lax.broadcasted_iota(jnp.int32, w.shape, 1)
    col = jnp.sum(jnp.where(row == lane, jnp.broadcast_to(new_ref[...], w.shape), 0.0), axis=1, keepdims=True)
    o_ref[...] = jnp.where(lane == n - 1, col, pltpu.roll(w, n - 1, 1))


def win_shift(win_t, new_row):
    DB, R, W = win_t.shape
    assert R == W
    return pl.pallas_call(
        _win_shift_kernel,
        grid=(DB,),
        in_specs=[pl.BlockSpec((None, R, W), lambda b: (b, 0, 0)), pl.BlockSpec((None, 1, R), lambda b: (b, 0, 0))],
        out_specs=pl.BlockSpec((None, R, W), lambda b: (b, 0, 0)),
        out_shape=jax.ShapeDtypeStruct((DB, R, W), win_t.dtype),
        compiler_params=pltpu.CompilerParams(dimension_semantics=("parallel",), vmem_limit_bytes=VMEM_LIMIT_BYTES),
        name="win_shift",
    )(win_t, new_row)


def _dec_cmp_weights(w1, b1, w2, b2):
    w = w1.reshape(CMP_R, CMP_STRIDE // 2, 2, 1, NSA_HD, CMP_HID)
    w = jnp.broadcast_to(w, (CMP_R, CMP_STRIDE // 2, 2, 2, NSA_HD, CMP_HID))
    w = jnp.moveaxis(w, 0, 4).reshape(CMP_STRIDE // 2, 4 * NSA_HD, CMP_R * CMP_HID)
    return (w.astype(BF16), b1.reshape(1, CMP_HID), jnp.tile(w2, (1, NSA_G)).astype(BF16),
            jnp.tile(b2, NSA_G).reshape(1, SLOT_ROWS))


def _place_heads(q):
    own = (jnp.arange(NSA_H)[:, None] // NSA_HPG) == jnp.arange(NSA_G)[None, :]
    return jnp.where(own[None, :, :, None], q[:, :, None, :], 0.0).reshape(q.shape[0], NSA_H, SLOT_ROWS)


def _take_heads(o):
    o = o.reshape(o.shape[0], NSA_H, NSA_G, NSA_HD)
    return o[:, jnp.arange(NSA_H), jnp.arange(NSA_H) // NSA_HPG, :].reshape(o.shape[0], NSA_W)


WKV_C = 64
WKV_PAIR = 2 * RWKV_HD
WKV_T_TILE = 512
WKV_PAIRS_PER_STEP = 4


def _split_bf16(x):
    hi = x.astype(BF16)
    return hi, (x - hi.astype(F32)).astype(BF16)


def _dot3(a, b):
    a_hi, a_lo = _split_bf16(a)
    b_hi, b_lo = _split_bf16(b)
    return (jnp.dot(a_hi, b_hi, preferred_element_type=F32) + jnp.dot(a_hi, b_lo, preferred_element_type=F32)
            + jnp.dot(a_lo, b_hi, preferred_element_type=F32))


def _wkv_kernel(r_ref, lw_ref, k_ref, v_ref, a_ref, b_ref, s0_ref, y_ref, sT_ref, s_scr):
    C = WKV_C
    P = WKV_PAIR
    n_chunks = r_ref.shape[0] // C

    @pl.when(pl.program_id(2) == 0)
    def _():
        s_scr[...] = s0_ref[...]

    lo_lane = lax.broadcasted_iota(jnp.int32, (C, P), 1) < RWKV_HD
    row = lax.broadcasted_iota(jnp.int32, (2 * C, 2 * C), 0)
    col = lax.broadcasted_iota(jnp.int32, (2 * C, 2 * C), 1)
    same_head = (row // C) == (col // C)
    strict = same_head & (row > col)
    lower = same_head & (row >= col)
    eye = jnp.where(row == col, 1.0, 0.0)
    tril = jnp.where(lax.broadcasted_iota(jnp.int32, (C, C), 0) >= lax.broadcasted_iota(jnp.int32, (C, C), 1),
                     1.0, 0.0).astype(BF16)

    def stack(x):
        return jnp.concatenate([jnp.where(lo_lane, x, 0.0), jnp.where(lo_lane, 0.0, x)], axis=0)

    def chunk(c, carry):
        stages = [pair_chunk(c, q) for q in range(WKV_PAIRS_PER_STEP)]
        while stages:
            stages = [g for g in stages if next(g, True) is None]
        return carry

    def pair_chunk(c, q):
        sl = pl.ds(pl.multiple_of(c * C, C), C)
        lanes = slice(q * P, (q + 1) * P)
        r, lw, k, v, a, b = (ref[sl, lanes] for ref in (r_ref, lw_ref, k_ref, v_ref, a_ref, b_ref))
        lw_hi, lw_mid = _split_bf16(lw)
        lw_lo = (lw - lw_hi.astype(F32) - lw_mid.astype(F32)).astype(BF16)
        cs = (jnp.dot(tril, lw_hi, preferred_element_type=F32) + jnp.dot(tril, lw_mid, preferred_element_type=F32)
              + jnp.dot(tril, lw_lo, preferred_element_type=F32))
        yield
        g_inv = jnp.exp(-cs)
        g_end = jnp.exp(cs[C - 1:C, :] - cs)
        a2 = stack(a * jnp.exp(cs - lw))
        r2 = stack(r * jnp.exp(cs))
        b2 = stack(b * g_inv)
        k2 = stack(k * g_inv)
        v2 = stack(v)
        s_old = s_scr[q]
        ar = jnp.concatenate([a2, r2], axis=0).astype(BF16)
        bk = jnp.concatenate([b2, k2], axis=0).astype(BF16)
        nt = (((1,), (1,)), ((), ()))
        pp = lax.dot_general(ar, bk, nt, preferred_element_type=F32)
        from_state = lax.dot_general(ar, s_old.astype(BF16), nt, preferred_element_type=F32)
        yield
        l_ab = jnp.where(strict, pp[:2 * C, :2 * C], 0.0)
        l_ak = jnp.where(strict, pp[:2 * C, 2 * C:], 0.0)
        m_rb = jnp.where(lower, pp[2 * C:, :2 * C], 0.0)
        m_rk = jnp.where(lower, pp[2 * C:, 2 * C:], 0.0)
        v2b = v2.astype(BF16)
        rhs = from_state[:2 * C] + jnp.dot(l_ak.astype(BF16), v2b, preferred_element_type=F32)
        yield
        n = l_ab
        x = eye + n
        span = 2
        while span < C:
            n = _dot3(n, n)
            yield
            x = x + _dot3(n, x)
            yield
            span *= 2
        u2 = _dot3(x, rhs)
        yield
        uv = jnp.concatenate([u2, v2], axis=0).astype(BF16)
        y2 = from_state[2 * C:] + jnp.dot(jnp.concatenate([m_rb, m_rk], axis=1).astype(BF16), uv,
                                          preferred_element_type=F32)
        yield
        y_ref[sl, lanes] = y2[:C] + y2[C:]
        bk_end = jnp.concatenate([stack(b * g_end), stack(k * g_end)], axis=0).astype(BF16)
        s_scr[q] = s_old * jnp.exp(cs[C - 1:C, :]) + lax.dot_general(
            uv, bk_end, (((0,), (0,)), ((), ())), preferred_element_type=F32)

    lax.fori_loop(0, n_chunks, chunk, 0)

    @pl.when(pl.program_id(2) == pl.num_programs(2) - 1)
    def _():
        sT_ref[...] = s_scr[...]


def wkv7_chunked(r, lw, k, v, a, b, s0):
    B, T, W = r.shape
    n_pair = W // WKV_PAIR
    tt = min(WKV_T_TILE, T)
    s0p = s0.astype(F32).reshape(B, n_pair, 2, RWKV_HD, RWKV_HD)
    zero = jnp.zeros_like(s0p[:, :, 0])
    s0_bd = jnp.concatenate([jnp.concatenate([s0p[:, :, 0], zero], axis=-1),
                             jnp.concatenate([zero, s0p[:, :, 1]], axis=-1)], axis=-2)
    pps = WKV_PAIRS_PER_STEP
    seq = pl.BlockSpec((None, tt, pps * WKV_PAIR), lambda i, p, t: (i, t, p))
    st = pl.BlockSpec((None, pps, WKV_PAIR, WKV_PAIR), lambda i, p, t: (i, p, 0, 0))
    y, s_bd = pl.pallas_call(
        _wkv_kernel,
        grid=(B, n_pair // pps, T // tt),
        in_specs=[seq] * 6 + [st],
        out_specs=[seq, st],
        out_shape=[jax.ShapeDtypeStruct((B, T, W), F32),
                   jax.ShapeDtypeStruct((B, n_pair, WKV_PAIR, WKV_PAIR), F32)],
        scratch_shapes=[pltpu.VMEM((pps, WKV_PAIR, WKV_PAIR), F32)],
        compiler_params=pltpu.CompilerParams(
            dimension_semantics=("parallel", "parallel", "arbitrary"), vmem_limit_bytes=VMEM_LIMIT_BYTES),
        name="wkv7_chunked",
    )(r, lw, k, v, a, b, s0_bd)
    s_fin = jnp.stack([s_bd[:, :, :RWKV_HD, :RWKV_HD], s_bd[:, :, RWKV_HD:, RWKV_HD:]], axis=2)
    return y, s_fin.reshape(B, W // RWKV_HD, RWKV_HD, RWKV_HD)


AB_PAD = _round_up(AB_COLS, COL_TILE)
SHIFT_PAD = _round_up(SHIFT_W, LANE)
LORA_PAD = SHIFT_PAD - 3 * RWKV_W
EVEN_ROWS = 128
N_EVEN_PRE_OUT = 10


def _split3(x):
    hi = x.astype(BF16)
    r1 = x - hi.astype(F32)
    mid = r1.astype(BF16)
    return hi, mid, (r1 - mid.astype(F32)).astype(BF16)


def _dot_01(x, m):
    return sum(jnp.dot(part, m, preferred_element_type=F32) for part in _split3(x))


def _head_sum(x, red_ref, exp_ref):
    return _dot_01(_dot_01(x, red_ref[...]), exp_ref[...])


def _expm1(x):
    u = jnp.exp(x)
    d = u - 1.0
    log_u = jnp.where((d == 0.0) | (d == -1.0), 1.0, jnp.log(u))
    return jnp.where(d == 0.0, x, jnp.where(d == -1.0, -1.0, d * x / log_u))


def _even_pre_math(x_ref, prev, taps, prm, outs):
    (cw_ref, cb_ref, wa_ref, ba_ref, wx_ref, bx_ref, lam_ref, mu_ref, w0_ref, a0_ref, wl_ref,
     kk_ref, ka_ref, red_ref, exp_ref) = prm
    a_o, u_o, gate_o, r_o, lw_o, k_o, v_o, na_o, nb_o, g_o = outs
    t1, t2, t3 = taps
    xb = x_ref[:, 0:LRU_W]
    xc = cb_ref[...] + cw_ref[0:1] * t3 + cw_ref[1:2] * t2 + cw_ref[2:3] * t1 + cw_ref[3:4] * xb
    xcb = xc.astype(BF16)
    gate_r = jax.nn.sigmoid(jnp.dot(xcb, wa_ref[...], preferred_element_type=F32) + ba_ref[...])
    gate_i = jax.nn.sigmoid(jnp.dot(xcb, wx_ref[...], preferred_element_type=F32) + bx_ref[...])
    log_a = -LRU_C * gate_r * lam_ref[...]
    a_o[...] = jnp.exp(log_a)
    u_o[...] = jnp.sqrt(-_expm1(2.0 * log_a)) * (gate_i * xc)
    gate_o[...] = jax.nn.gelu(x_ref[:, LRU_W:2 * LRU_W])
    rw = x_ref[:, 2 * LRU_W:2 * LRU_W + SHIFT_PAD]
    rs = rw + mu_ref[...] * (prev - rw)
    r_o[...] = rs[:, 0:RWKV_W]
    k = rs[:, RWKV_W:2 * RWKV_W]
    v_o[...] = rs[:, 2 * RWKV_W:3 * RWKV_W]
    tail = rs[:, 3 * RWKV_W:]
    lane = lax.broadcasted_iota(jnp.int32, tail.shape, 1)
    act = jnp.where(lane < W_LORA, jnp.tanh(tail), jnp.where(lane < W_LORA + A_LORA, tail, jax.nn.sigmoid(tail)))
    z = jnp.dot(act.astype(BF16), wl_ref[...], preferred_element_type=F32)
    w_log = -jax.nn.softplus(-(w0_ref[...] + z[:, 0:RWKV_W])) - 0.5
    lw_o[...] = -jnp.exp(w_log)
    a_icl = jax.nn.sigmoid(a0_ref[...] + z[:, RWKV_W:2 * RWKV_W])
    g_o[...] = z[:, 2 * RWKV_W:]
    kk = k * kk_ref[...]
    kk = kk / jnp.maximum(jnp.sqrt(_head_sum(kk * kk, red_ref, exp_ref)), 1e-12)
    k_o[...] = k * (1.0 + (a_icl - 1.0) * ka_ref[...])
    na_o[...] = -kk
    nb_o[...] = kk * a_icl


def _even_pre_seq_kernel(x_ref, conv0_ref, shift0_ref, *refs):
    prm = refs[:15]
    outs = refs[15:15 + N_EVEN_PRE_OUT]
    conv_c, shift_c = refs[15 + N_EVEN_PRE_OUT:]
    rows = x_ref.shape[0]

    @pl.when(pl.program_id(1) == 0)
    def _():
        conv_c[...] = conv0_ref[...]
        shift_c[...] = shift0_ref[...]

    xb = x_ref[:, 0:LRU_W]
    row = lax.broadcasted_iota(jnp.int32, xb.shape, 0)
    taps = []
    for j in (1, 2, 3):
        tap = pltpu.roll(xb, j, 0)
        for i in range(j):
            tap = jnp.where(row == i, conv_c[8 - j + i:9 - j + i, :], tap)
        taps.append(tap)
    rw = x_ref[:, 2 * LRU_W:2 * LRU_W + SHIFT_PAD]
    row_w = lax.broadcasted_iota(jnp.int32, rw.shape, 0)
    prev = jnp.where(row_w == 0, shift_c[7:8, :], pltpu.roll(rw, 1, 0))
    _even_pre_math(x_ref, prev, taps, prm, outs)
    conv_c[...] = x_ref[rows - 8:rows, 0:LRU_W]
    shift_c[...] = x_ref[rows - 8:rows, 2 * LRU_W:2 * LRU_W + SHIFT_PAD]


def _even_pre_step_kernel(x_ref, prev_ref, t1_ref, t2_ref, t3_ref, *refs):
    _even_pre_math(x_ref, prev_ref[...], (t1_ref[...], t2_ref[...], t3_ref[...]), refs[:15], refs[15:])


def _even_params(p):
    def bd(w):
        eye = jnp.eye(LRU_BLOCKS, dtype=w.dtype)
        return (eye[:, None, :, None] * w[:, :, None, :]).reshape(LRU_W, LRU_W).astype(BF16)
    row = lambda v: v.reshape(1, -1).astype(F32)
    wl = jnp.zeros((LORA_PAD, 3 * RWKV_W), F32)
    wl = wl.at[0:W_LORA, 0:RWKV_W].set(p['w2'])
    wl = wl.at[W_LORA:W_LORA + A_LORA, RWKV_W:2 * RWKV_W].set(p['a2'])
    wl = wl.at[W_LORA + A_LORA:W_LORA + A_LORA + G_LORA, 2 * RWKV_W:].set(p['g2'])
    head = np.arange(RWKV_W) // RWKV_HD
    red = jnp.asarray(head[:, None] == np.arange(LANE)[None, :], dtype=BF16)
    mu = jnp.pad(p['mu'], (0, SHIFT_PAD - SHIFT_W))
    return [p['conv_w'].astype(F32), row(p['conv_b']), bd(p['wa']), row(p['ba']), bd(p['wx']), row(p['bx']),
            row(jax.nn.softplus(-p['lam'].astype(F32))), row(mu), row(p['w0']), row(p['a0']), wl.astype(BF16),
            row(p['k_k']), row(p['k_a']), red, red.T]


def _const_spec(a, n_grid):
    return pl.BlockSpec(a.shape, lambda *_: (0,) * a.ndim)


def even_pre_seq(proj, conv0, shift0, prm, B, T):
    tr = EVEN_ROWS
    nt = T // tr
    conv_pad = jnp.pad(conv0.astype(F32), ((0, 0), (8 - (CONV_W - 1), 0), (0, 0)))
    shift_pad = jnp.pad(shift0.astype(F32)[:, None, :], ((0, 0), (7, 0), (0, SHIFT_PAD - SHIFT_W)))
    out_spec = pl.BlockSpec((tr, LRU_W), lambda b, t: (b * nt + t, 0))
    return pl.pallas_call(
        _even_pre_seq_kernel,
        grid=(B, nt),
        in_specs=[pl.BlockSpec((tr, AB_PAD), lambda b, t: (b * nt + t, 0)),
                  pl.BlockSpec((None, 8, LRU_W), lambda b, t: (b, 0, 0)),
                  pl.BlockSpec((None, 8, SHIFT_PAD), lambda b, t: (b, 0, 0))] + [_const_spec(a, 2) for a in prm],
        out_specs=[out_spec] * N_EVEN_PRE_OUT,
        out_shape=[jax.ShapeDtypeStruct((B * T, LRU_W), F32)] * N_EVEN_PRE_OUT,
        scratch_shapes=[pltpu.VMEM((8, LRU_W), F32), pltpu.VMEM((8, SHIFT_PAD), F32)],
        compiler_params=pltpu.CompilerParams(
            dimension_semantics=("parallel", "arbitrary"), vmem_limit_bytes=VMEM_LIMIT_BYTES),
        name="even_pre_seq",
    )(proj, conv_pad, shift_pad, *prm)


def even_pre_step(proj, row0, conv0, shift0, prm):
    n = conv0.shape[0]
    shift_pad = jnp.pad(shift0.astype(F32), ((0, 0), (0, SHIFT_PAD - SHIFT_W)))
    taps = [conv0[:, CONV_W - 1 - j].astype(F32) for j in (1, 2, 3)]
    full = lambda w: pl.BlockSpec((n, w), lambda i: (0, 0))
    return pl.pallas_call(
        _even_pre_step_kernel,
        grid=(1,),
        in_specs=[pl.BlockSpec((n, AB_PAD), lambda i: (row0 // n, 0)), full(SHIFT_PAD)] + [full(LRU_W)] * 3
        + [_const_spec(a, 1) for a in prm],
        out_specs=[full(LRU_W)] * N_EVEN_PRE_OUT,
        out_shape=[jax.ShapeDtypeStruct((n, LRU_W), F32)] * N_EVEN_PRE_OUT,
        compiler_params=pltpu.CompilerParams(
            dimension_semantics=("arbitrary",), vmem_limit_bytes=VMEM_LIMIT_BYTES),
        name="even_pre_step",
    )(proj, shift_pad, *taps, *prm)


def _even_post_kernel(hs_ref, gate_ref, y_ref, r_ref, k_ref, v_ref, g_ref, lng_ref, lnb_ref, rk_ref,
                      red_ref, exp_ref, *rest):
    o_ref = rest[-1]
    y = y_ref[...]
    mu = _head_sum(y, red_ref, exp_ref) * (1.0 / RWKV_HD)
    d = y - mu
    var = _head_sum(d * d, red_ref, exp_ref) * (1.0 / RWKV_HD)
    yn = d * lax.rsqrt(var + 64e-5) * lng_ref[...] + lnb_ref[...]
    bonus = _head_sum(r_ref[...] * k_ref[...] * rk_ref[...], red_ref, exp_ref) * v_ref[...]
    o_ref[:, 0:LRU_W] = (hs_ref[...] * gate_ref[...]).astype(o_ref.dtype)
    o_ref[:, LRU_W:] = ((yn + bonus) * g_ref[...]).astype(o_ref.dtype)


def even_post(hs, gate, y, r, k, v, g, p, red, n_total, row0, prior=None):
    n = hs.shape[0]
    tr = EVEN_ROWS
    row = lambda a: a.reshape(1, -1).astype(F32)
    consts = [row(p['ln_g']), row(p['ln_b']), row(p['r_k']), red, red.T]
    seq = pl.BlockSpec((tr, LRU_W), lambda i: (i, 0))
    args = [hs, gate, y, r, k, v, g] + consts
    in_specs = [seq] * 7 + [_const_spec(a, 1) for a in consts]
    aliases = {}
    if prior is not None:
        args.append(prior)
        in_specs.append(pl.BlockSpec(memory_space=pl.ANY))
        aliases = {len(args) - 1: 0}
    return pl.pallas_call(
        _even_post_kernel,
        grid=(n // tr,),
        in_specs=in_specs,
        out_specs=pl.BlockSpec((tr, D_MODEL), lambda i: (row0 // tr + i, 0)),
        out_shape=jax.ShapeDtypeStruct((n_total, D_MODEL), BF16),
        input_output_aliases=aliases,
        compiler_params=pltpu.CompilerParams(
            dimension_semantics=("parallel",), vmem_limit_bytes=VMEM_LIMIT_BYTES),
        name="even_post",
    )(*args)


def _retention_kernel(q_ref, k_ref, va_ref, vb_ref, dm_ref, rd_ref, kd_ref, sd_ref, o_ref, s_out_ref, s_scr):
    C = q_ref.shape[0]
    n_pair = RET_H // 2

    @pl.when(pl.program_id(1) == 0)
    def _():
        s_scr[...] = jnp.zeros(s_scr.shape, F32)

    lo = lax.broadcasted_iota(jnp.int32, (C, 2 * RET_DK), 1) < RET_DK

    def stack(x):
        return jnp.concatenate([jnp.where(lo, x, 0.0), jnp.where(lo, 0.0, x)], axis=0)

    for p in range(n_pair):
        qk = slice(p * 2 * RET_DK, (p + 1) * 2 * RET_DK)
        q2 = stack(q_ref[:, qk]).astype(BF16)
        k2 = stack(k_ref[:, qk])
        v0 = p * 2 * RET_DV
        v_ref = va_ref if p < n_pair // 2 else vb_ref
        vl = v0 % (RET_W // 2)
        v2 = jnp.concatenate([v_ref[:, vl:vl + RET_DV], v_ref[:, vl + RET_DV:vl + 2 * RET_DV]],
                             axis=0).astype(BF16)
        s = lax.dot_general(q2, k2.astype(BF16), (((1,), (1,)), ((), ())), preferred_element_type=F32) * dm_ref[p]
        s_old = s_scr[p]
        o2 = jnp.dot(s.astype(BF16), v2, preferred_element_type=F32) + jnp.dot(
            q2, s_old.astype(BF16), preferred_element_type=F32) * rd_ref[p]
        o_ref[:, v0:v0 + RET_DV] = o2[:C]
        o_ref[:, v0 + RET_DV:v0 + 2 * RET_DV] = o2[C:]
        s_scr[p] = s_old * sd_ref[p] + lax.dot_general((k2 * kd_ref[p]).astype(BF16), v2, (((0,), (0,)), ((), ())),
                                                       preferred_element_type=F32)

    @pl.when(pl.program_id(1) == pl.num_programs(1) - 1)
    def _():
        s_out_ref[...] = s_scr[...]


def retention_prompt_pallas(rq, rk, rv, B, T, v_col0=0):
    C = RET_CHUNK
    nc = T // C
    f32 = F32
    lg = jnp.log1p(-jnp.exp2(-5.0 - jnp.arange(RET_H, dtype=f32))).reshape(RET_H // 2, 2)
    i = jnp.arange(C, dtype=f32)
    diff = i[:, None] - i[None, :]
    causal = diff >= 0
    dmask = jnp.where(causal, jnp.exp(jnp.where(causal, diff, 0.0)[None, None] * lg[:, :, None, None]), 0.0)
    zero = jnp.zeros_like(dmask[:, 0])
    dm = jnp.concatenate([jnp.concatenate([dmask[:, 0], zero], axis=-1),
                          jnp.concatenate([zero, dmask[:, 1]], axis=-1)], axis=-2)
    rows = lambda x, w: jnp.broadcast_to(x[:, :, :, None], x.shape + (w,)).reshape(RET_H // 2, -1, w)
    rd = rows(jnp.exp((i[None, None, :] + 1.0) * lg[:, :, None]), RET_DV)
    kd = rows(jnp.exp((C - 1.0 - i)[None, None, :] * lg[:, :, None]), 2 * RET_DK)
    sd = rows(jnp.broadcast_to(jnp.exp(C * lg)[:, :, None], (RET_H // 2, 2, RET_DK)), RET_DV)
    half_w = RET_W // 2
    qk_spec = pl.BlockSpec((C, RET_H * RET_DK), lambda b, c: (b * nc + c, 0))
    v_spec = lambda k: pl.BlockSpec((C, half_w), lambda b, c: (b * nc + c, v_col0 // half_w + k))
    const = lambda a: pl.BlockSpec(a.shape, lambda b, c: (0, 0, 0))
    o, s = pl.pallas_call(
        _retention_kernel,
        grid=(B, nc),
        in_specs=[qk_spec, qk_spec, v_spec(0), v_spec(1), const(dm), const(rd), const(kd), const(sd)],
        out_specs=[pl.BlockSpec((C, RET_W), lambda b, c: (b * nc + c, 0)),
                   pl.BlockSpec((None, RET_H // 2, 2 * RET_DK, RET_DV), lambda b, c: (b, 0, 0, 0))],
        out_shape=[jax.ShapeDtypeStruct((B * T, RET_W), f32),
                   jax.ShapeDtypeStruct((B, RET_H // 2, 2 * RET_DK, RET_DV), f32)],
        scratch_shapes=[pltpu.VMEM((RET_H // 2, 2 * RET_DK, RET_DV), f32)],
        compiler_params=pltpu.CompilerParams(
            dimension_semantics=("parallel", "arbitrary"), vmem_limit_bytes=VMEM_LIMIT_BYTES),
        name="retention_prompt",
    )(rq, rk, rv, rv, dm, rd, kd, sd)
    return s.reshape(B, RET_H, RET_DK, RET_DV), o


KV_W = NSA_G * NSA_HD
RET_QK_W = RET_H * RET_DK
OFF_Q = 0
OFF_KC = OFF_Q + NSA_W
OFF_VC = OFF_KC + KV_W
OFF_KS = OFF_VC + KV_W
OFF_VS = OFF_KS + KV_W
OFF_KW = OFF_VS + KV_W
OFF_VW = OFF_KW + KV_W
OFF_RQ = OFF_VW + KV_W
OFF_RK = OFF_RQ + RET_QK_W
OFF_RV = OFF_RK + RET_QK_W
OFF_RG = OFF_RV + RET_W
OFF_GT = OFF_RG + RET_W
CD_PAD = _round_up(OFF_GT + LANE, COL_TILE)
ODD_ROWS = 128
N_ODD_PRE_OUT = 11


def _odd_weight_cols(w):
    gt0 = NSA_W + 6 * KV_W
    body = jnp.concatenate([w[:, :gt0], w[:, gt0 + 3 * NSA_H:]], axis=1)
    gt = w[:, gt0:gt0 + 3 * NSA_H]
    out = jnp.concatenate([body, gt], axis=1)
    return jnp.pad(out, ((0, 0), (0, CD_PAD - out.shape[1]))).astype(BF16)


def _rope_tables(pos, n_rot, theta, head):
    half = n_rot // 2
    inv = jnp.exp(-jnp.log(jnp.float32(theta)) * jnp.arange(half, dtype=jnp.float32) / half)
    ang = pos.astype(jnp.float32)[:, None] * inv[None, :]
    cos, sin = jnp.cos(ang), jnp.sin(ang)
    d = np.arange(LANE) % head
    cos_d, sin_d = cos[:, d % half], sin[:, d % half]
    c = jnp.where(d < n_rot, cos_d, 1.0)
    s1 = jnp.where(d < half, -sin_d, 0.0)
    s2 = jnp.where((d >= half) & (d < n_rot), sin_d, 0.0)
    return jnp.stack([c, s1, s2])


def _rope_lanes(x, tab_ref, half):
    w = x.shape[1]
    rep = w // LANE
    c, s1, s2 = (pltpu.repeat(tab_ref[i], rep, axis=1) for i in range(3))
    return x * c + pltpu.roll(x, w - half, 1) * s1 + pltpu.roll(x, half, 1) * s2


def _rms_heads(x, g_ref, red_ref, exp_ref):
    ms = _head_sum(x * x, red_ref, exp_ref) * (1.0 / NSA_HD)
    return x * lax.rsqrt(ms + 1e-6) * g_ref[...]


def _odd_pre_kernel(x_ref, nsa_tab, ret_tab, qg_ref, ksg_ref, kwg_ref, redq_ref, expq_ref, redk_ref, expk_ref,
                    tile_ref, qn_o, qr_o, ks_o, kw_o, ks4_o, vs4_o, kw4_o, vw4_o, gate_o, rq_o, rk_o, kvt_o, wint_o):
    nsa_half = ROPE_DIMS // 2
    qn = _rms_heads(x_ref[:, OFF_Q:OFF_Q + NSA_W], qg_ref, redq_ref, expq_ref)
    qn_o[...] = qn
    qr_o[...] = _rope_lanes(qn, nsa_tab, nsa_half)
    ks = _rope_lanes(_rms_heads(x_ref[:, OFF_KS:OFF_KS + KV_W], ksg_ref, redk_ref, expk_ref), nsa_tab, nsa_half)
    kw = _rope_lanes(_rms_heads(x_ref[:, OFF_KW:OFF_KW + KV_W], kwg_ref, redk_ref, expk_ref), nsa_tab, nsa_half)
    ks_o[...] = ks
    kw_o[...] = kw
    tile = tile_ref[...]
    for src, dst in ((ks, ks4_o), (x_ref[:, OFF_VS:OFF_VS + KV_W], vs4_o), (kw, kw4_o),
                     (x_ref[:, OFF_VW:OFF_VW + KV_W], vw4_o)):
        dst[...] = jnp.dot(src.astype(BF16), tile, preferred_element_type=F32).astype(BF16)
    gate_o[...] = jax.nn.sigmoid(x_ref[:, OFF_GT:OFF_GT + LANE])
    rq_o[...] = _rope_lanes(x_ref[:, OFF_RQ:OFF_RQ + RET_QK_W], ret_tab, RET_DK // 2)
    rk_o[...] = _rope_lanes(x_ref[:, OFF_RK:OFF_RK + RET_QK_W], ret_tab, RET_DK // 2) * (RET_DK ** -0.5)
    kv_pieces = (x_ref[:, OFF_KC:OFF_KC + KV_W], x_ref[:, OFF_VC:OFF_VC + KV_W], ks, x_ref[:, OFF_VS:OFF_VS + KV_W])
    for dst, pieces in ((kvt_o, kv_pieces), (wint_o, (kw, x_ref[:, OFF_VW:OFF_VW + KV_W]))):
        for s, piece in enumerate(pieces):
            for c in range(KV_W // LANE):
                dst[s * KV_W + c * LANE:s * KV_W + (c + 1) * LANE, :] = piece[:, c * LANE:(c + 1) * LANE].T


def odd_pre(proj, pos, p, row0, n_rows, same_pos, seq_len):
    tr = ODD_ROWS
    blk0 = row0 // tr
    n_tab = tr if same_pos else n_rows
    pos_rows = jnp.broadcast_to(pos, (n_tab,)) if same_pos else pos
    nsa_tab = _rope_tables(pos_rows, ROPE_DIMS, ROPE_THETA, NSA_HD)
    ret_tab = _rope_tables(pos_rows, RET_DK, RET_THETA, RET_DK)
    row = lambda v, rep: jnp.tile(v.astype(F32), rep).reshape(1, -1)
    lanes = np.arange(LANE)
    red_q = jnp.asarray((np.arange(NSA_W) // NSA_HD)[:, None] == lanes[None, :], dtype=BF16)
    red_k = jnp.asarray((np.arange(KV_W) // NSA_HD)[:, None] == lanes[None, :], dtype=BF16)
    src = np.arange(KV_W)
    dst = np.arange(NSA_W)
    tile = jnp.asarray((src[:, None] // NSA_HD == dst[None, :] // GROUP_W)
                       & (src[:, None] % NSA_HD == dst[None, :] % NSA_HD), dtype=BF16)
    consts = [row(p['q_norm'], NSA_H), row(p['k_norm'][1], NSA_G), row(p['k_norm'][2], NSA_G),
              red_q, red_q.T, red_k, red_k.T, tile]
    tab_spec = pl.BlockSpec((3, tr, LANE), (lambda i: (0, 0, 0)) if same_pos else (lambda i: (0, i, 0)))
    out = lambda w, dt: (pl.BlockSpec((tr, w), lambda i: (i, 0)), jax.ShapeDtypeStruct((n_rows, w), dt))
    seq_tiles = seq_len // tr
    out_t = lambda r: (pl.BlockSpec((None, r, tr), lambda i: (i // seq_tiles, 0, i % seq_tiles)),
                       jax.ShapeDtypeStruct((n_rows // seq_len, r, seq_len), F32))
    outs = [out(NSA_W, F32), out(NSA_W, F32), out(KV_W, F32), out(KV_W, F32)] + [out(NSA_W, BF16)] * 4 + [
        out(LANE, F32), out(RET_QK_W, F32), out(RET_QK_W, F32), out_t(KV_SLOTS * KV_W), out_t(2 * KV_W)]
    return pl.pallas_call(
        _odd_pre_kernel,
        grid=(n_rows // tr,),
        in_specs=[pl.BlockSpec((tr, CD_PAD), lambda i: (blk0 + i, 0)), tab_spec, tab_spec]
        + [_const_spec(a, 1) for a in consts],
        out_specs=[o[0] for o in outs],
        out_shape=[o[1] for o in outs],
        compiler_params=pltpu.CompilerParams(
            dimension_semantics=("parallel",), vmem_limit_bytes=VMEM_LIMIT_BYTES),
        name="odd_pre",
    )(proj, nsa_tab, ret_tab, *consts)


def _odd_post_kernel(oc_ref, os_ref, ow_ref, gate_ref, ret_ref, rg0_ref, rg1_ref, gng_ref, gnb_ref, ge_ref, *rest,
                     gated):
    o_ref = rest[-1]
    if gated:
        nsa = oc_ref[...]
    else:
        gates = gate_ref[...]
        nsa = jnp.zeros(oc_ref.shape, F32)
        for j, branch in enumerate((oc_ref, os_ref, ow_ref)):
            nsa = nsa + _dot_01(gates, ge_ref[j]) * branch[...]
    o_ref[:, 0:NSA_W] = nsa.astype(o_ref.dtype)
    for h in range(RET_H):
        lanes = slice(h * RET_DV, (h + 1) * RET_DV)
        x = ret_ref[:, lanes]
        mu = jnp.mean(x, axis=-1, keepdims=True)
        d = x - mu
        var = jnp.mean(d * d, axis=-1, keepdims=True)
        yn = d * lax.rsqrt(var + 1e-5) * gng_ref[:, lanes] + gnb_ref[:, lanes]
        rg = (rg0_ref if h < RET_H // 2 else rg1_ref)[:, (h % (RET_H // 2)) * RET_DV:(h % (RET_H // 2) + 1) * RET_DV]
        o_ref[:, NSA_W + h * RET_DV:NSA_W + (h + 1) * RET_DV] = (yn * (rg * jax.nn.sigmoid(rg))).astype(o_ref.dtype)


def odd_post(o_cmp, o_slc, o_win, gates, o_ret, proj, p, n_total, row0, prior=None, gated=False):
    n = o_cmp.shape[0]
    tr = ODD_ROWS
    blk0 = row0 // tr
    h = np.arange(NSA_W) // NSA_HD
    ge = jnp.asarray(np.stack([(np.arange(LANE)[:, None] == (3 * h + j)[None, :]) for j in range(3)]), dtype=BF16)
    row = lambda a: a.reshape(1, -1).astype(F32)
    consts = [row(p['gn_g']), row(p['gn_b']), ge]
    seq = lambda w: pl.BlockSpec((tr, w), lambda i: (i, 0))
    half = RET_W // 2
    rg_spec = lambda k: pl.BlockSpec((tr, half), lambda i: (blk0 + i, OFF_RG // half + k))
    args = [o_cmp, o_slc, o_win, gates, o_ret, proj, proj] + consts
    in_specs = [seq(NSA_W)] * 3 + [seq(LANE), seq(RET_W), rg_spec(0), rg_spec(1)] + [_const_spec(a, 1) for a in consts]
    aliases = {}
    if prior is not None:
        args.append(prior)
        in_specs.append(pl.BlockSpec(memory_space=pl.ANY))
        aliases = {len(args) - 1: 0}
    return pl.pallas_call(
        functools.partial(_odd_post_kernel, gated=gated),
        grid=(n // tr,),
        in_specs=in_specs,
        out_specs=pl.BlockSpec((tr, D_MODEL), lambda i: (blk0 + i, 0)),
        out_shape=jax.ShapeDtypeStruct((n_total, D_MODEL), BF16),
        input_output_aliases=aliases,
        compiler_params=pltpu.CompilerParams(
            dimension_semantics=("parallel",), vmem_limit_bytes=VMEM_LIMIT_BYTES),
        name="odd_post",
    )(*args)


def rms_norm(x, g, eps=1e-6):
    xf = x.astype(jnp.float32)
    y = xf * lax.rsqrt(jnp.mean(xf * xf, axis=-1, keepdims=True) + eps)
    return (y * g.astype(jnp.float32)).astype(x.dtype)


def head_group_norm(y, g, b, eps):
    yf = y.astype(jnp.float32)
    mu = jnp.mean(yf, axis=-1, keepdims=True)
    var = jnp.mean(jnp.square(yf - mu), axis=-1, keepdims=True)
    yn = ((yf - mu) * lax.rsqrt(var + eps)).reshape(y.shape[:-2] + (-1,))
    return (yn * g.astype(jnp.float32) + b.astype(jnp.float32)).astype(y.dtype)


def masked_softmax(s, mask):
    s = jnp.where(mask, s.astype(jnp.float32), -jnp.inf)
    m = jnp.max(s, axis=-1, keepdims=True)
    e = jnp.exp(s - jnp.where(jnp.isfinite(m), m, 0.0))
    den = jnp.sum(e, axis=-1, keepdims=True)
    return e / jnp.where(den > 0, den, 1.0)


def rope(x, pos, n_rot, theta):
    half = n_rot // 2
    inv = jnp.exp(-jnp.log(jnp.float32(theta)) * jnp.arange(half, dtype=jnp.float32) / half)
    ang = pos.astype(jnp.float32)[:, None] * inv[None, :]
    cos = jnp.cos(ang)[None, :, None, :]
    sin = jnp.sin(ang)[None, :, None, :]
    xf = x.astype(jnp.float32)
    x1, x2 = xf[..., :half], xf[..., half:n_rot]
    out = jnp.concatenate([x1 * cos - x2 * sin, x2 * cos + x1 * sin, xf[..., n_rot:]], axis=-1)
    return out.astype(x.dtype)


def linear_scan(a, b, h0):
    b = b.at[:, 0].add(a[:, 0] * h0)

    def combine(left, right):
        return left[0] * right[0], right[0] * left[1] + right[1]

    return lax.associative_scan(combine, (a, b), axis=1)[1]


def wkv7_scan(r, w, k, v, a, b, s0):
    xs = tuple(jnp.moveaxis(z.astype(jnp.float32), 1, 0) for z in (r, w, k, v, a, b))

    def step(S, inp):
        r_t, w_t, k_t, v_t, a_t, b_t = inp
        sa = jnp.einsum('bhij,bhj->bhi', S, a_t)
        S = S * w_t[:, :, None, :] + sa[..., None] * b_t[:, :, None, :] + v_t[..., None] * k_t[:, :, None, :]
        return S, jnp.einsum('bhij,bhj->bhi', S, r_t)

    S, ys = lax.scan(step, s0.astype(jnp.float32), xs)
    return jnp.moveaxis(ys, 0, 1), S


def even_mixer_core(proj, p, lru_h0, lru_conv0, shift0, wkv0):
    B, T, _ = proj.shape
    f32 = jnp.float32
    dt = proj.dtype
    xb, gb, rw = jnp.split(proj, [LRU_W, 2 * LRU_W], axis=-1)
    xcat = jnp.concatenate([lru_conv0.astype(dt), xb], axis=1)
    xc = p['conv_b'] + sum(p['conv_w'][j] * xcat[:, j:j + T] for j in range(CONV_W))
    xbd = xc.reshape(B, T, LRU_BLOCKS, LRU_BS)
    gate_r = jax.nn.sigmoid(jnp.einsum('btnc,ncd->btnd', xbd, p['wa']).reshape(B, T, LRU_W) + p['ba'])
    gate_i = jax.nn.sigmoid(jnp.einsum('btnc,ncd->btnd', xbd, p['wx']).reshape(B, T, LRU_W) + p['bx'])
    log_a = -LRU_C * gate_r.astype(f32) * jax.nn.softplus(-p['lam'].astype(f32))
    u = jnp.sqrt(-jnp.expm1(2.0 * log_a)) * (gate_i * xc).astype(f32)
    hs = lru_scan(jnp.exp(log_a), u, lru_h0.astype(f32))
    y_lru = hs.astype(dt) * jax.nn.gelu(gb)
    prev = jnp.concatenate([shift0.astype(dt)[:, None], rw[:, :-1]], axis=1)
    rs = rw + p['mu'] * (prev - rw)
    r, k, v, xw, xa, xg = jnp.split(
        rs, [RWKV_W, 2 * RWKV_W, 3 * RWKV_W, 3 * RWKV_W + W_LORA, 3 * RWKV_W + W_LORA + A_LORA], axis=-1)
    w_log = -jax.nn.softplus(-(p['w0'] + jnp.tanh(xw) @ p['w2']).astype(f32)) - 0.5
    log_decay = -jnp.exp(w_log)
    decay = jnp.exp(log_decay)
    a_icl = jax.nn.sigmoid(p['a0'] + xa @ p['a2'])
    g = jax.nn.sigmoid(xg) @ p['g2']
    heads = (B, T, RWKV_H, RWKV_HD)
    kk = (k * p['k_k']).reshape(heads).astype(f32)
    kk = kk / jnp.maximum(jnp.sqrt(jnp.sum(kk * kk, axis=-1, keepdims=True)), 1e-12)
    k = k * (1.0 + (a_icl - 1.0) * p['k_a'])
    rh, kh, vh, ah = (z.reshape(heads) for z in (r, k, v, a_icl))
    if T % WKV_C == 0:
        y, wkv = wkv7_chunked(r.astype(f32), log_decay, k.astype(f32), v.astype(f32),
                              (-kk).reshape(B, T, RWKV_W), (kk * ah.astype(f32)).reshape(B, T, RWKV_W), wkv0)
        y = y.reshape(heads)
    else:
        y, wkv = wkv7_scan(rh, decay.reshape(heads), kh, vh, -kk, kk * ah.astype(f32), wkv0)
    y = head_group_norm(y, p['ln_g'], p['ln_b'], 64e-5).astype(dt)
    bonus = (jnp.sum(rh * kh * p['r_k'], axis=-1, keepdims=True) * vh).reshape(B, T, RWKV_W)
    y_rwkv = (y + bonus) * g
    cat = jnp.concatenate([y_lru, y_rwkv], axis=-1)
    return cat, hs[:, -1], xcat[:, T:], rw[:, -1], wkv


def even_mixer(proj, p, B, T, DB, lru_h0, lru_conv0, shift0, wkv0):
    f32 = F32
    prm = _even_params(p)
    red = prm[-2]
    n_p = B * T
    zeros = lambda *s: jnp.zeros(s, f32)
    a, u, gate, r, lw, k, v, na, nb, g = even_pre_seq(proj, zeros(B, CONV_W - 1, LRU_W), zeros(B, SHIFT_W), prm, B, T)
    seq = lambda z: z.reshape(B, T, LRU_W)
    hs = lru_scan(seq(a), seq(u), zeros(B, LRU_W))
    yw, wkv_p = wkv7_chunked(seq(r), seq(lw), seq(k), seq(v), seq(na), seq(nb), zeros(B, RWKV_H, RWKV_HD, RWKV_HD))
    cat = even_post(hs.reshape(n_p, LRU_W), gate, yw.reshape(n_p, RWKV_W), r, k, v, g, p, red, n_p + DB, 0)
    tail = lambda b, n, c0, c1: proj[(b + 1) * T - n:(b + 1) * T, c0:c1]
    st_p = (hs[:, -1], jnp.stack([tail(b, CONV_W - 1, 0, LRU_W) for b in range(B)]),
            jnp.concatenate([tail(b, 1, 2 * LRU_W, AB_COLS) for b in range(B)], axis=0), wkv_p)
    a, u, gate, r, lw, k, v, na, nb, g = even_pre_step(proj, n_p, lru_conv0, shift0, prm)
    hs_s = a * lru_h0.astype(f32) + u
    heads = (DB, 1, RWKV_H, RWKV_HD)
    yw, wkv_s = wkv7_scan(r.reshape(heads), jnp.exp(lw).reshape(heads), k.reshape(heads), v.reshape(heads),
                          na.reshape(heads), nb.reshape(heads), wkv0)
    cat = even_post(hs_s, gate, yw.reshape(DB, RWKV_W), r, k, v, g, p, red, n_p + DB, n_p, prior=cat)
    xb_s = proj[n_p:]
    conv_s = jnp.concatenate([lru_conv0[:, 1:].astype(f32), xb_s[:, None, :LRU_W]], axis=1)
    st_s = (hs_s, conv_s, xb_s[:, 2 * LRU_W:AB_COLS], wkv_s)
    return cat, st_p, st_s


def odd_project(proj, p, pos):
    B, T, _ = proj.shape
    sizes = [NSA_W] + [NSA_G * NSA_HD] * 6 + [3 * NSA_H, RET_H * RET_DK, RET_H * RET_DK, RET_W, RET_W]
    q, kc, vc, ks, vs, kw, vw, gt, rq, rk, rv, rg = jnp.split(
        proj, np.cumsum(sizes).tolist(), axis=-1)[:len(sizes)]
    kvs = (B, T, NSA_G, NSA_HD)
    q_n = rms_norm(q.reshape(B, T, NSA_H, NSA_HD), p['q_norm'])
    return {
        'q_n': q_n,
        'q_r': rope(q_n, pos, ROPE_DIMS, ROPE_THETA),
        'kc': kc.reshape(kvs), 'vc': vc.reshape(kvs),
        'ks': rope(rms_norm(ks.reshape(kvs), p['k_norm'][1]), pos, ROPE_DIMS, ROPE_THETA),
        'vs': vs.reshape(kvs),
        'kw': rope(rms_norm(kw.reshape(kvs), p['k_norm'][2]), pos, ROPE_DIMS, ROPE_THETA),
        'vw': vw.reshape(kvs),
        'gates': jax.nn.sigmoid(gt).reshape(B, T, NSA_H, 3),
        'rq': rope(rq.reshape(B, T, RET_H, RET_DK), pos, RET_DK, RET_THETA),
        'rk': rope(rk.reshape(B, T, RET_H, RET_DK), pos, RET_DK, RET_THETA) * (RET_DK ** -0.5),
        'rv': rv.reshape(B, T, RET_H, RET_DV),
        'rg': rg,
    }


def to_groups_q(q):
    B, T = q.shape[:2]
    return jnp.moveaxis(q.reshape(B, T, NSA_G, NSA_HPG, NSA_HD), 1, 3)


def to_groups_k(k):
    return jnp.moveaxis(k, 1, 2)


def nsa_compress(x, w1, b1, w2, b2):
    B, L = x.shape[:2]
    n_chunk = L // CMP_STRIDE
    n_cmp = n_chunk - CMP_R + 1
    ch = x[:, :n_chunk * CMP_STRIDE].reshape(B, n_chunk, CMP_STRIDE, NSA_G, NSA_HD)
    ch = jnp.moveaxis(ch, 3, 2).reshape(B, n_chunk, NSA_G, CMP_STRIDE * NSA_HD)
    part = jnp.einsum('bngc,rch->bngrh', ch, w1)
    pre = b1 + sum(part[:, m:m + n_cmp, :, m] for m in range(CMP_R))
    return jax.nn.gelu(pre) @ w2 + b2


def nsa_compressed_branch(qn, kc_raw, vc_raw, p, q_pos):
    kc = to_groups_k(rms_norm(nsa_compress(kc_raw, *p['ck']), p['k_norm'][0]))
    vc = to_groups_k(nsa_compress(vc_raw, *p['cv']))
    s = jnp.einsum('bghqd,bgcd->bghqc', qn, kc) * NSA_HD ** -0.5
    ends = jnp.arange(kc.shape[2]) * CMP_STRIDE + CMP_BLOCK - 1
    prob = masked_softmax(s, ends[None, :] <= q_pos[:, None])
    return jnp.einsum('bghqc,bgcd->bghqd', prob.astype(vc.dtype), vc), prob


def cmp_sel_overlap(n_cmp, n_sel):
    cs = np.arange(n_cmp) * CMP_STRIDE
    ss = np.arange(n_sel) * SEL_BLOCK
    ov = np.minimum(cs[None] + CMP_BLOCK, ss[:, None] + SEL_BLOCK) - np.maximum(cs[None], ss[:, None])
    return jnp.asarray(np.clip(ov, 0, None) / CMP_BLOCK, dtype=jnp.float32)


def nsa_select(p_cmp, q_pos, n_sel):
    imp = jnp.einsum('bgqc,sc->bgqs', p_cmp.sum(axis=2), cmp_sel_overlap(p_cmp.shape[-1], n_sel))
    j = jnp.arange(n_sel)[None, :]
    qb = (q_pos // SEL_BLOCK)[:, None]
    valid = j <= qb
    forced = (j == 0) | (j == qb) | (j == qb - 1)
    score = jnp.where(valid, jnp.where(forced, FORCE_SCORE, imp), -jnp.inf)
    _, idx = lax.top_k(score, min(SEL_TOP, n_sel))
    sel_ok = jnp.take_along_axis(jnp.broadcast_to(valid, score.shape), idx, axis=-1)
    return idx, sel_ok


def sel_blocks(x, n_sel):
    B, L = x.shape[:2]
    x = jnp.pad(x, ((0, 0), (0, n_sel * SEL_BLOCK - L), (0, 0), (0, 0)))
    return jnp.moveaxis(x.reshape(B, n_sel, SEL_BLOCK, NSA_G, NSA_HD), 3, 1)


def nsa_slc_attend(q, kb, vb, idx, sel_ok, q_pos):
    B, G = kb.shape[:2]
    bi = jnp.arange(B)[:, None, None, None]
    gi = jnp.arange(G)[None, :, None, None]
    kg = kb[bi, gi, idx]
    vg = vb[bi, gi, idx]
    s = jnp.einsum('bghqd,bgqnld->bghqnl', q, kg) * NSA_HD ** -0.5
    kpos = idx[..., None] * SEL_BLOCK + jnp.arange(SEL_BLOCK)
    mask = (kpos <= q_pos[None, None, :, None, None]) & sel_ok[..., None]
    sh = s.shape
    prob = masked_softmax(s.reshape(sh[:4] + (-1,)), mask.reshape(B, G, 1, sh[3], -1))
    return jnp.einsum('bghqnl,bgqnld->bghqd', prob.reshape(sh).astype(vg.dtype), vg)


def window_attend_banded(q, k, v):
    B, G, HPG, T, HD = q.shape
    nb = T // WIN_BLOCK
    npv = WINDOW // WIN_BLOCK
    pad = ((0, 0), (0, 0), (npv * WIN_BLOCK, 0), (0, 0))

    def band(z):
        zb = jnp.pad(z, pad).reshape(B, G, nb + npv, WIN_BLOCK, HD)
        return jnp.concatenate([zb[:, :, j:j + nb] for j in range(npv + 1)], axis=3)

    kb, vb = band(k), band(v)
    qb = q.reshape(B, G, HPG, nb, WIN_BLOCK, HD)
    s = jnp.einsum('bghiqd,bgikd->bghiqk', qb, kb) * NSA_HD ** -0.5
    blk = jnp.arange(nb)[:, None]
    q_pos = blk * WIN_BLOCK + jnp.arange(WIN_BLOCK)[None]
    k_pos = (blk - npv) * WIN_BLOCK + jnp.arange((npv + 1) * WIN_BLOCK)[None]
    diff = q_pos[:, :, None] - k_pos[:, None, :]
    mask = (diff >= 0) & (diff < WINDOW) & (k_pos[:, None, :] >= 0)
    prob = masked_softmax(s, mask)
    return jnp.einsum('bghiqk,bgikd->bghiqd', prob.astype(v.dtype), vb).reshape(B, G, HPG, T, HD)


def window_attend_cached(q, k, v, q_pos, k_pos):
    s = jnp.einsum('bghqd,blgd->bghql', q, k) * NSA_HD ** -0.5
    diff = q_pos[:, None] - k_pos[None, :]
    prob = masked_softmax(s, (diff >= 0) & (diff < WINDOW))
    return jnp.einsum('bghql,blgd->bghqd', prob.astype(v.dtype), v)


def retention_chunk(S, q, k, v):
    f32 = jnp.float32
    C = q.shape[1]
    lg = jnp.log1p(-jnp.exp2(-5.0 - jnp.arange(RET_H, dtype=f32)))
    i = jnp.arange(C, dtype=f32)
    diff = i[:, None] - i[None, :]
    causal = diff >= 0
    dmask = jnp.where(causal, jnp.exp(jnp.where(causal, diff, 0.0)[None] * lg[:, None, None]), 0.0)
    qf, kf, vf = q.astype(f32), k.astype(f32), v.astype(f32)
    s = jnp.einsum('bihd,bjhd->bhij', qf, kf) * dmask
    o = jnp.einsum('bhij,bjhe->bihe', s, vf)
    o = o + jnp.einsum('bihd,bhde->bihe', qf, S) * jnp.exp((i[:, None] + 1.0) * lg[None, :])[None, :, :, None]
    k_dec = kf * jnp.exp((C - 1.0 - i)[:, None] * lg[None, :])[None, :, :, None]
    S = S * jnp.exp(C * lg)[None, :, None, None] + jnp.einsum('bjhd,bjhe->bhde', k_dec, vf)
    return S, o


def retention_prompt(q, k, v):
    B, T = q.shape[:2]
    n = T // RET_CHUNK
    xs = tuple(jnp.moveaxis(z.reshape((B, n, RET_CHUNK) + z.shape[2:]), 1, 0) for z in (q, k, v))
    s0 = jnp.zeros((B, RET_H, RET_DK, RET_DV), jnp.float32)
    S, o = lax.scan(lambda S, c: retention_chunk(S, c[0], c[1], c[2]), s0, xs)
    return S, jnp.moveaxis(o, 0, 1).reshape(B, T, RET_H, RET_DV)


def odd_output(o_cmp, o_slc, o_win, o_ret, pr, p):
    gates = pr['gates']
    B, T = gates.shape[:2]
    gg = jnp.moveaxis(gates.reshape(B, T, NSA_G, NSA_HPG, 3), 1, 3)[..., None]
    o = gg[..., 0, :] * o_cmp + gg[..., 1, :] * o_slc + gg[..., 2, :] * o_win
    o_nsa = jnp.moveaxis(o, 3, 1).reshape(B, T, NSA_W)
    y_ret = head_group_norm(o_ret, p['gn_g'], p['gn_b'], 1e-5).astype(o_nsa.dtype) * jax.nn.silu(pr['rg'])
    return jnp.concatenate([o_nsa, y_ret], axis=-1)


def odd_mixer_prompt(proj, p):
    B, T, _ = proj.shape
    pos = jnp.arange(T)
    pr = odd_project(proj, p, pos)
    qn = pr['q_n'].reshape(B, T, NSA_W)
    qr = pr['q_r'].reshape(B, T, NSA_W)
    kc = rms_norm(nsa_compress(pr['kc'], *p['ck']), p['k_norm'][0])
    vc = nsa_compress(pr['vc'], *p['cv'])
    n_cmp = kc.shape[1]
    n_sel = -(-T // SEL_BLOCK)
    o_cmp, sel = nsa_cmp_select(qn, _tile_cmp(kc), _tile_cmp(vc), _overlap_T(n_cmp, n_sel),
                                n_cmp=n_cmp, n_sel=n_sel, q_pos0=0)
    o_slc = nsa_flash(qr, _tile_groups(pr['ks']), _tile_groups(pr['vs']), sel, _sel_expand(T))
    o_win = nsa_flash(qr, _tile_groups(pr['kw']), _tile_groups(pr['vw']))
    S, o_ret = retention_prompt_pallas(pr['rq'].reshape(B, T, -1), pr['rk'].reshape(B, T, -1),
                                       pr['rv'].reshape(B, T, -1))
    o_ret = o_ret.reshape(B, T, RET_H, RET_DV)
    gates = pr['gates']
    heads = (B, T, NSA_H, NSA_HD)
    o_nsa = (gates[..., 0:1] * o_cmp.reshape(heads) + gates[..., 1:2] * o_slc.reshape(heads)
             + gates[..., 2:3] * o_win.reshape(heads)).reshape(B, T, NSA_W)
    y_ret = head_group_norm(o_ret, p['gn_g'], p['gn_b'], 1e-5).astype(o_nsa.dtype) * jax.nn.silu(pr['rg'])
    out = jnp.concatenate([o_nsa, y_ret], axis=-1)
    kv_rows = jnp.stack([pr['kc'], pr['vc'], pr['ks'], pr['vs']], axis=2)
    win = jnp.stack([pr['kw'], pr['vw']], axis=2)[:, T - min(WINDOW, T):]
    return out, kv_rows, win, S


def odd_mixer_sample(proj, p, cache_layer, page_table, win_buf, ret_s0):
    B, T, _ = proj.shape
    assert T == DEC_SEQ == 1 and win_buf.shape[1] == WIN_BUF
    pos = PAST_LEN + jnp.arange(T)
    pr = odd_project(proj, p, pos)
    scale = NSA_HD ** -0.5
    new_rows = jnp.stack([pr['ks'], pr['vs'], pr['kw'], pr['vw']], axis=2)[:, 0].reshape(B, 4, SLOT_ROWS)
    cache_t = jnp.transpose(cache_layer, (0, 2, 3, 4, 1)).reshape(cache_layer.shape[0], KV_ROWS, PAGE_SIZE)
    win_t = jnp.transpose(win_buf, (0, 2, 3, 4, 1)).reshape(B, 2 * SLOT_ROWS, WIN_BUF)
    wk = _dec_cmp_weights(*p['ck'])
    wv = _dec_cmp_weights(*p['cv'])
    w1t, b1, w2t, b2t = (jnp.stack([a, b]) for a, b in zip(wk, wv))
    kn = jnp.tile(p['k_norm'][0], NSA_G).reshape(1, SLOT_ROWS)
    t = np.arange(PAST_LEN)
    expand = jnp.asarray(np.arange(CMP_PAD)[:, None] == (t // SEL_BLOCK)[None, :], dtype=BF16)
    h = np.arange(NSA_H)
    grp = jnp.asarray((h[:, None] // NSA_HPG) == (h[None, :] // NSA_HPG), dtype=BF16)
    o16 = dec_nsa(page_table, cache_t, win_t, _place_heads(pr['q_n'][:, 0] * scale),
                  _place_heads(pr['q_r'][:, 0] * scale), new_rows, pr['gates'][:, 0],
                  w1t, b1, w2t, b2t, kn, _overlap_T(DEC_N_CMP, DEC_N_SEL), expand, grp)
    o_nsa = _take_heads(o16)[:, None, :]
    S, o_ret = retention_chunk(ret_s0.astype(jnp.float32), pr['rq'], pr['rk'], pr['rv'])
    y_ret = head_group_norm(o_ret, p['gn_g'], p['gn_b'], 1e-5).astype(o_nsa.dtype) * jax.nn.silu(pr['rg'])
    out = jnp.concatenate([o_nsa, y_ret], axis=-1)
    rows = jnp.stack([pr['kc'], pr['vc'], pr['ks'], pr['vs']], axis=2).astype(cache_layer.dtype)
    new_col = jnp.stack([pr['kw'], pr['vw']], axis=2)[:, 0].reshape(B, 2 * SLOT_ROWS, 1).astype(win_buf.dtype)
    win_new = jnp.concatenate([win_t[:, :, T:], new_col], axis=2).reshape(B, 2, NSA_G, NSA_HD, WIN_BUF)
    return out, rows, jnp.transpose(win_new, (0, 4, 1, 2, 3)), S


def odd_mixer(proj, p, B, T, DB, cache_layer, page_table, win_buf, ret_s0):
    assert DEC_SEQ == 1 and win_buf.shape[1] == WIN_BUF
    n_p = B * T
    kv = (B, T, NSA_G, NSA_HD)
    cols = lambda rows, off, w: proj[rows, off:off + w]
    prompt = slice(0, n_p)
    dec = slice(n_p, n_p + DB)
    qn, qr, ks, kw, ks4, vs4, kw4, vw4, gates, rq, rk, kv_t, win_t_p = odd_pre(
        proj, jnp.tile(jnp.arange(T), B), p, 0, n_p, False, T)
    seq = lambda z: z.reshape(B, T, -1)
    kc_raw = cols(prompt, OFF_KC, KV_W).reshape(kv)
    vc_raw = cols(prompt, OFF_VC, KV_W).reshape(kv)
    kc = rms_norm(nsa_compress(kc_raw, *p['ck']), p['k_norm'][0])
    vc = nsa_compress(vc_raw, *p['cv'])
    n_cmp = kc.shape[1]
    n_sel = -(-T // SEL_BLOCK)
    o_cmp, sel = nsa_cmp_select(seq(qn), _tile_cmp(kc), _tile_cmp(vc), _overlap_T(n_cmp, n_sel),
                                n_cmp=n_cmp, n_sel=n_sel, q_pos0=0)
    o_slc = nsa_flash(seq(qr), seq(ks4), seq(vs4), sel, _sel_expand(T))
    o_win = nsa_flash(seq(qr), seq(kw4), seq(vw4))
    ret_p, o_ret = retention_prompt_pallas(rq, rk, proj, B, T, v_col0=OFF_RV)
    flat = lambda z: z.reshape(n_p, -1)
    cat = odd_post(flat(o_cmp), flat(o_slc), flat(o_win), gates, o_ret, proj, p, n_p + DB, 0)
    n_win = min(WINDOW, T)
    kv_rows_p = jnp.transpose(kv_t.reshape(B, KV_SLOTS, NSA_G, NSA_HD, T), (0, 4, 1, 2, 3))
    win_p = jnp.transpose(win_t_p[:, :, T - n_win:].reshape(B, 2, NSA_G, NSA_HD, n_win), (0, 4, 1, 2, 3))
    qn, qr, ks, kw, _, _, _, _, gates, rq, rk, kv_t, _ = odd_pre(proj, jnp.asarray(PAST_LEN), p, n_p, DB, True, DB)
    scale = NSA_HD ** -0.5
    heads = lambda z: z.reshape(DB, NSA_H, NSA_HD)
    vs, vw = cols(dec, OFF_VS, KV_W), cols(dec, OFF_VW, KV_W)
    new_rows = jnp.stack([ks, vs, kw, vw], axis=1)
    cache_t = jnp.transpose(cache_layer, (0, 2, 3, 4, 1)).reshape(cache_layer.shape[0], KV_ROWS, PAGE_SIZE)
    win_t = jnp.transpose(win_buf, (0, 2, 3, 4, 1)).reshape(DB, 2 * SLOT_ROWS, WIN_BUF)
    w1t, b1, w2t, b2t = (jnp.stack([a, b]) for a, b in zip(_dec_cmp_weights(*p['ck']), _dec_cmp_weights(*p['cv'])))
    kn = jnp.tile(p['k_norm'][0], NSA_G).reshape(1, SLOT_ROWS)
    t = np.arange(PAST_LEN)
    expand = jnp.asarray(np.arange(CMP_PAD)[:, None] == (t // SEL_BLOCK)[None, :], dtype=BF16)
    h = np.arange(NSA_H)
    grp = jnp.asarray((h[:, None] // NSA_HPG) == (h[None, :] // NSA_HPG), dtype=BF16)
    o16 = dec_nsa(page_table, cache_t, win_t, _place_heads(heads(qn) * scale), _place_heads(heads(qr) * scale),
                  new_rows, gates[:, :3 * NSA_H].reshape(DB, NSA_H, 3),
                  w1t, b1, w2t, b2t, kn, _overlap_T(DEC_N_CMP, DEC_N_SEL), expand, grp)
    o_nsa = _take_heads(o16)
    ret_s, o_ret = retention_chunk(ret_s0.astype(F32), rq.reshape(DB, 1, RET_H, RET_DK),
                                   rk.reshape(DB, 1, RET_H, RET_DK), cols(dec, OFF_RV, RET_W).reshape(DB, 1, RET_H, RET_DV))
    cat = odd_post(o_nsa, o_nsa, o_nsa, gates, o_ret.reshape(DB, RET_W), proj, p, n_p + DB, n_p, prior=cat, gated=True)
    rows_s = jnp.transpose(kv_t.reshape(DEC_SEQ, KV_SLOTS, NSA_G, NSA_HD, DB), (4, 0, 1, 2, 3)).astype(cache_layer.dtype)
    new_row = jnp.concatenate([kw, vw], axis=1)[:, None, :].astype(win_buf.dtype)
    win_new = win_shift(win_t, new_row).reshape(DB, 2, NSA_G, NSA_HD, WIN_BUF)
    win_s = jnp.transpose(win_new, (0, 4, 1, 2, 3))
    return cat, (kv_rows_p, win_p, ret_p), (rows_s, win_s, ret_s)


def _stack(xs, dt):
    return jnp.stack(xs).astype(dt)


def kernel(x_prompt, x_sample, state_lru_h, state_lru_conv, state_rwkv_shift, state_rwkv_wkv,
           cache_nsa_kv, cache_nsa_win, state_ret, page_table,
           norm_ffn1, ffn1_w_in, ffn1_w_out, norm_mix, norm_ffn2, ffn2_w_in, ffn2_w_out,
           ab_w_in, lru_conv_w, lru_conv_b, lru_wa, lru_ba, lru_wx, lru_bx, lru_lambda,
           rwkv_mu, rwkv_w0, rwkv_w2, rwkv_a0, rwkv_a2, rwkv_g2, rwkv_k_k, rwkv_k_a, rwkv_r_k,
           rwkv_ln_g, rwkv_ln_b, ab_w_out,
           cd_w_in, nsa_q_norm, nsa_k_norm, cmp_k_w1, cmp_k_b1, cmp_k_w2, cmp_k_b2,
           cmp_v_w1, cmp_v_b1, cmp_v_w2, cmp_v_b2, ret_gn_g, ret_gn_b, cd_w_out):
    dt = x_prompt.dtype
    B = x_prompt.shape[0]
    DB = x_sample.shape[0]
    y = jnp.concatenate([x_prompt.reshape(N_PROMPT, D_MODEL), x_sample.reshape(DB * DEC_SEQ, D_MODEL)], axis=0)
    lru_h_p, lru_h_s, lru_c_p, lru_c_s, sh_p, sh_s, wkv_p, wkv_s = [], [], [], [], [], [], [], []
    kv_p, kv_s, win_p, win_s, ret_p, ret_s = [], [], [], [], [], []
    for layer in range(DEPTH):
        li = layer // 2
        y = ffn_block(y, norm_ffn1[layer], *_prep_ffn_weights(ffn1_w_in, ffn1_w_out, layer))
        if layer % 2 == 0:
            p = {'conv_w': lru_conv_w[li], 'conv_b': lru_conv_b[li],
                 'wa': lru_wa[li], 'ba': lru_ba[li], 'wx': lru_wx[li], 'bx': lru_bx[li], 'lam': lru_lambda[li],
                 'mu': rwkv_mu[li], 'w0': rwkv_w0[li], 'w2': rwkv_w2[li], 'a0': rwkv_a0[li], 'a2': rwkv_a2[li],
                 'g2': rwkv_g2[li], 'k_k': rwkv_k_k[li], 'k_a': rwkv_k_a[li], 'r_k': rwkv_r_k[li],
                 'ln_g': rwkv_ln_g[li], 'ln_b': rwkv_ln_b[li]}
            proj = norm_matmul(y, norm_mix[layer], _prep_cols(ab_w_in[li]))
            cat, (a0, a1, a2, a3), (b0, b1, b2, b3) = even_mixer(
                proj, p, B, SEQ, DB, state_lru_h[li], state_lru_conv[li], state_rwkv_shift[li], state_rwkv_wkv[li])
            lru_h_p.append(a0); lru_c_p.append(a1); sh_p.append(a2); wkv_p.append(a3)
            lru_h_s.append(b0); lru_c_s.append(b1); sh_s.append(b2); wkv_s.append(b3)
            w_out = ab_w_out[li]
        else:
            p = {'q_norm': nsa_q_norm[li], 'k_norm': nsa_k_norm[li],
                 'ck': (cmp_k_w1[li], cmp_k_b1[li], cmp_k_w2[li], cmp_k_b2[li]),
                 'cv': (cmp_v_w1[li], cmp_v_b1[li], cmp_v_w2[li], cmp_v_b2[li]),
                 'gn_g': ret_gn_g[li], 'gn_b': ret_gn_b[li]}
            proj = norm_matmul(y, norm_mix[layer], _odd_weight_cols(cd_w_in[li]), tn=WIDE_COL_TILE)
            cat, (a0, a1, a2), (b0, b1, b2) = odd_mixer(
                proj, p, B, SEQ, DB, cache_nsa_kv[li], page_table, cache_nsa_win[li], state_ret[li])
            kv_p.append(a0); win_p.append(a1); ret_p.append(a2)
            kv_s.append(b0); win_s.append(b1); ret_s.append(b2)
            w_out = cd_w_out[li]
        y = matmul_residual(cat, w_out.astype(BF16), y, tn=WIDE_COL_TILE)
        y = ffn_block(y, norm_ffn2[layer], *_prep_ffn_weights(ffn2_w_in, ffn2_w_out, layer))
    yp = y[:N_PROMPT].reshape(B, SEQ, D_MODEL)
    ys = y[N_PROMPT:].reshape(DB, DEC_SEQ, D_MODEL)
    return (yp, ys,
            _stack(lru_h_p, dt), _stack(lru_h_s, dt), _stack(lru_c_p, dt), _stack(lru_c_s, dt),
            _stack(sh_p, dt), _stack(sh_s, dt), _stack(wkv_p, dt), _stack(wkv_s, dt),
            _stack(kv_p, dt), _stack(kv_s, dt), _stack(win_p, dt), _stack(win_s, dt),
            _stack(ret_p, dt), _stack(ret_s, dt))
```

```python
import functools

import jax
import jax.numpy as jnp
import numpy as np
from jax import lax
from jax.experimental import pallas as pl
from jax.experimental.pallas import tpu as pltpu

D_MODEL = 2048
BATCH = 4
SEQ = 2048
DEPTH = 2
DEC_BATCH = 128
DEC_SEQ = 1
PAST_LEN = 2048
PAGE_SIZE = 128
D_FF = 5504
LRU_W = D_MODEL // 2
LRU_BLOCKS = 16
LRU_BS = LRU_W // LRU_BLOCKS
CONV_W = 4
LRU_C = 8.0
RWKV_W = D_MODEL // 2
RWKV_HD = 64
RWKV_H = RWKV_W // RWKV_HD
W_LORA = 64
A_LORA = 64
G_LORA = 160
SHIFT_W = 3 * RWKV_W + W_LORA + A_LORA + G_LORA
AB_COLS = 2 * LRU_W + SHIFT_W
NSA_H = 16
NSA_G = 4
NSA_HPG = NSA_H // NSA_G
NSA_HD = 64
NSA_W = NSA_H * NSA_HD
ROPE_DIMS = NSA_HD // 4
ROPE_THETA = 500000.0
CMP_BLOCK = 32
CMP_STRIDE = 16
CMP_R = CMP_BLOCK // CMP_STRIDE
CMP_HID = 256
SEL_BLOCK = 64
SEL_TOP = 16
SEL_Q_BLOCK = 64
WINDOW = 512
WIN_BLOCK = 128
FORCE_SCORE = 1e4
KV_SLOTS = 4
RET_H = 8
RET_DK = 64
RET_DV = 128
RET_W = RET_H * RET_DV
RET_CHUNK = 128
RET_THETA = 10000.0
CD_COLS = NSA_W + 6 * NSA_G * NSA_HD + 3 * NSA_H + 2 * RET_H * RET_DK + 2 * RET_W

N_TOK = BATCH * SEQ + DEC_BATCH * DEC_SEQ
N_PROMPT = BATCH * SEQ

LANE = 128
VMEM_LIMIT_BYTES = 56 * 1024 * 1024
ROW_TILE = 640
FF_TILE = 512
D_FF_PAD = 5632
COL_TILE = 512
WIDE_COL_TILE = 2048

BF16 = jnp.bfloat16
F32 = jnp.float32


def _round_up(n, m):
    return -(-n // m) * m


def _rms_rows(x, g):
    ms = jnp.mean(x * x, axis=-1, keepdims=True)
    return x * lax.rsqrt(ms + 1e-6) * g


N_FF_TILES = D_FF_PAD // FF_TILE


def _ffn_kernel(x_ref, g_ref, wg_ref, wu_ref, wo_ref, res_ref, o_ref, xn_ref, h_ref):
    k = pl.program_id(1)

    @pl.when(k == 0)
    def _():
        xn_ref[...] = _rms_rows(x_ref[...], g_ref[...]).astype(BF16)

    @pl.when(k < N_FF_TILES)
    def _():
        xn = xn_ref[...]
        gate = jnp.dot(xn, wg_ref[...], preferred_element_type=F32)
        up = jnp.dot(xn, wu_ref[...], preferred_element_type=F32)
        h_ref[k] = (gate * jax.nn.sigmoid(gate) * up).astype(BF16)

    @pl.when(k >= N_FF_TILES)
    def _():
        acc = jnp.dot(h_ref[0], wo_ref[0:FF_TILE, :], preferred_element_type=F32)
        for c in range(1, N_FF_TILES):
            acc = acc + jnp.dot(h_ref[c], wo_ref[c * FF_TILE:(c + 1) * FF_TILE, :], preferred_element_type=F32)
        o_ref[...] = res_ref[...] + 0.5 * acc


def ffn_block(x, g, wg, wu, wo):
    m, d = x.shape
    n_out = d // COL_TILE
    out_col = lambda i, k: (i, jnp.maximum(k - N_FF_TILES, 0))
    return pl.pallas_call(
        _ffn_kernel,
        grid=(m // ROW_TILE, N_FF_TILES + n_out),
        in_specs=[
            pl.BlockSpec((ROW_TILE, d), lambda i, k: (i, 0)),
            pl.BlockSpec((1, d), lambda i, k: (0, 0)),
            pl.BlockSpec((d, FF_TILE), lambda i, k: (0, jnp.minimum(k, N_FF_TILES - 1))),
            pl.BlockSpec((d, FF_TILE), lambda i, k: (0, jnp.minimum(k, N_FF_TILES - 1))),
            pl.BlockSpec((D_FF_PAD, COL_TILE), lambda i, k: (0, jnp.maximum(k - N_FF_TILES, 0))),
            pl.BlockSpec((ROW_TILE, COL_TILE), out_col),
        ],
        out_specs=pl.BlockSpec((ROW_TILE, COL_TILE), out_col),
        out_shape=jax.ShapeDtypeStruct((m, d), F32),
        scratch_shapes=[pltpu.VMEM((ROW_TILE, d), BF16), pltpu.VMEM((N_FF_TILES, ROW_TILE, FF_TILE), BF16)],
        compiler_params=pltpu.CompilerParams(
            dimension_semantics=("parallel", "arbitrary"), vmem_limit_bytes=VMEM_LIMIT_BYTES),
        name="ffn_block",
    )(x, g.reshape(1, d), wg, wu, wo, x)


def _norm_matmul_kernel(x_ref, g_ref, w_ref, o_ref, xn_ref):
    @pl.when(pl.program_id(1) == 0)
    def _():
        xn_ref[...] = _rms_rows(x_ref[...], g_ref[...]).astype(BF16)

    o_ref[...] = jnp.dot(xn_ref[...], w_ref[...], preferred_element_type=F32)


def norm_matmul(x, g, w, tn=COL_TILE):
    m, k = x.shape
    n = w.shape[1]
    return pl.pallas_call(
        _norm_matmul_kernel,
        grid=(m // ROW_TILE, n // tn),
        in_specs=[
            pl.BlockSpec((ROW_TILE, k), lambda i, j: (i, 0)),
            pl.BlockSpec((1, k), lambda i, j: (0, 0)),
            pl.BlockSpec((k, tn), lambda i, j: (0, j)),
        ],
        out_specs=pl.BlockSpec((ROW_TILE, tn), lambda i, j: (i, j)),
        out_shape=jax.ShapeDtypeStruct((m, n), F32),
        scratch_shapes=[pltpu.VMEM((ROW_TILE, k), BF16)],
        compiler_params=pltpu.CompilerParams(
            dimension_semantics=("parallel", "arbitrary"), vmem_limit_bytes=VMEM_LIMIT_BYTES),
        name="norm_matmul",
    )(x, g.reshape(1, k), w)


def _matmul_residual_kernel(a_ref, w_ref, r_ref, o_ref):
    o_ref[...] = r_ref[...] + jnp.dot(a_ref[...].astype(BF16), w_ref[...], preferred_element_type=F32)


def matmul_residual(a, w, res, tn=COL_TILE):
    m, k = a.shape
    n = w.shape[1]
    return pl.pallas_call(
        _matmul_residual_kernel,
        grid=(m // ROW_TILE, n // tn),
        in_specs=[
            pl.BlockSpec((ROW_TILE, k), lambda i, j: (i, 0)),
            pl.BlockSpec((k, tn), lambda i, j: (0, j)),
            pl.BlockSpec((ROW_TILE, tn), lambda i, j: (i, j)),
        ],
        out_specs=pl.BlockSpec((ROW_TILE, tn), lambda i, j: (i, j)),
        out_shape=jax.ShapeDtypeStruct((m, n), F32),
        compiler_params=pltpu.CompilerParams(
            dimension_semantics=("parallel", "arbitrary"), vmem_limit_bytes=VMEM_LIMIT_BYTES),
        name="matmul_residual",
    )(a, w, res)


WCAST_ROWS = 256
WCAST_COLS = 512


def _cast_w_in_kernel(w_ref, wg_ref, wu_ref):
    pad = jnp.zeros((w_ref.shape[0], D_FF_PAD - D_FF), BF16)
    wg_ref[:, :D_FF] = w_ref[:, :D_FF].astype(BF16)
    wg_ref[:, D_FF:] = pad
    wu_ref[:, :D_FF] = w_ref[:, D_FF:].astype(BF16)
    wu_ref[:, D_FF:] = pad


def _cast_w_out_kernel(w_ref, wo_ref):
    wo_ref[:D_FF, :] = w_ref[...].astype(BF16)
    wo_ref[D_FF:, :] = jnp.zeros((D_FF_PAD - D_FF, w_ref.shape[1]), BF16)


def _prep_ffn_weights(w_in, w_out, layer):
    d = w_in.shape[1]
    wg, wu = pl.pallas_call(
        _cast_w_in_kernel,
        grid=(d // WCAST_ROWS,),
        in_specs=[pl.BlockSpec((None, WCAST_ROWS, 2 * D_FF), lambda i: (layer, i, 0))],
        out_specs=[pl.BlockSpec((WCAST_ROWS, D_FF_PAD), lambda i: (i, 0))] * 2,
        out_shape=[jax.ShapeDtypeStruct((d, D_FF_PAD), BF16)] * 2,
        compiler_params=pltpu.CompilerParams(dimension_semantics=("parallel",), vmem_limit_bytes=VMEM_LIMIT_BYTES),
        name="cast_w_in",
    )(w_in)
    wo = pl.pallas_call(
        _cast_w_out_kernel,
        grid=(d // WCAST_COLS,),
        in_specs=[pl.BlockSpec((None, D_FF, WCAST_COLS), lambda j: (layer, 0, j))],
        out_specs=pl.BlockSpec((D_FF_PAD, WCAST_COLS), lambda j: (0, j)),
        out_shape=jax.ShapeDtypeStruct((D_FF_PAD, d), BF16),
        compiler_params=pltpu.CompilerParams(dimension_semantics=("parallel",), vmem_limit_bytes=VMEM_LIMIT_BYTES),
        name="cast_w_out",
    )(w_out)
    return wg, wu, wo


def _prep_cols(w):
    n = w.shape[1]
    return jnp.pad(w, ((0, 0), (0, _round_up(n, COL_TILE) - n))).astype(BF16)


SCAN_TILE = 256


def _lru_scan_kernel(a_ref, b_ref, h0_ref, o_ref, carry_ref):
    @pl.when(pl.program_id(1) == 0)
    def _():
        carry_ref[...] = h0_ref[...]

    a = a_ref[...]
    b = b_ref[...]
    rows = lax.broadcasted_iota(jnp.int32, a.shape, 0)
    k = 1
    while k < a.shape[0]:
        keep = rows >= k
        b = jnp.where(keep, a * pltpu.roll(b, k, 0) + b, b)
        a = jnp.where(keep, a * pltpu.roll(a, k, 0), a)
        k *= 2
    h = a * carry_ref[...] + b
    o_ref[...] = h
    carry_ref[...] = h[a.shape[0] - 1:, :]


def lru_scan(a, b, h0):
    B, T, W = a.shape
    tt = min(SCAN_TILE, T)
    return pl.pallas_call(
        _lru_scan_kernel,
        grid=(B, T // tt),
        in_specs=[
            pl.BlockSpec((None, tt, W), lambda i, t: (i, t, 0)),
            pl.BlockSpec((None, tt, W), lambda i, t: (i, t, 0)),
            pl.BlockSpec((None, 1, W), lambda i, t: (i, 0, 0)),
        ],
        out_specs=pl.BlockSpec((None, tt, W), lambda i, t: (i, t, 0)),
        out_shape=jax.ShapeDtypeStruct((B, T, W), F32),
        scratch_shapes=[pltpu.VMEM((1, W), F32)],
        compiler_params=pltpu.CompilerParams(
            dimension_semantics=("parallel", "arbitrary"), vmem_limit_bytes=VMEM_LIMIT_BYTES),
        name="lru_scan",
    )(a, b, h0.reshape(B, 1, W))


GROUP_W = NSA_HPG * NSA_HD
ATT_Q_TILE = 128
ATT_K_TILE = 256
CMP_PAD = 128
NEG_BIG = -1e30


def _stack_heads(q):
    head = lax.broadcasted_iota(jnp.int32, q.shape, 1) // NSA_HD
    return jnp.concatenate([jnp.where(head == h, q, 0.0) for h in range(NSA_HPG)], axis=0)


def _unstack_heads(o, tq):
    head = lax.broadcasted_iota(jnp.int32, (tq, GROUP_W), 1) // NSA_HD
    out = jnp.zeros((tq, GROUP_W), F32)
    for h in range(NSA_HPG):
        out = out + jnp.where(head == h, o[h * tq:(h + 1) * tq], 0.0)
    return out


def _cmp_select_kernel(q_ref, k_ref, v_ref, ov_ref, o_ref, sel_ref, *, n_cmp, n_sel, q_pos0):
    tq = q_ref.shape[0]
    i = pl.program_id(2)
    qs = _stack_heads(q_ref[...] * (NSA_HD ** -0.5)).astype(BF16)
    s = lax.dot_general(qs, k_ref[...], (((1,), (1,)), ((), ())), preferred_element_type=F32)
    q_pos = q_pos0 + i * tq + lax.broadcasted_iota(jnp.int32, (tq, CMP_PAD), 0)
    c = lax.broadcasted_iota(jnp.int32, (tq, CMP_PAD), 1)
    mask1 = (c < n_cmp) & (c * CMP_STRIDE + (CMP_BLOCK - 1) <= q_pos)
    mask = jnp.concatenate([mask1] * NSA_HPG, axis=0)
    s = jnp.where(mask, s, NEG_BIG)
    m = jnp.max(s, axis=-1, keepdims=True)
    e = jnp.where(mask, jnp.exp(s - m), 0.0)
    den = jnp.sum(e, axis=-1, keepdims=True)
    prob = e / jnp.where(den > 0, den, 1.0)
    o = jnp.dot(prob.astype(BF16), v_ref[...], preferred_element_type=F32)
    o_ref[...] = _unstack_heads(o, tq)
    psum = prob[0:tq]
    for h in range(1, NSA_HPG):
        psum = psum + prob[h * tq:(h + 1) * tq]
    imp = jnp.dot(psum.astype(BF16), ov_ref[...], preferred_element_type=F32)
    qb = q_pos // SEL_BLOCK
    valid = (c <= qb) & (c < n_sel)
    forced = (c == 0) | (c == qb) | (c == qb - 1)
    score = jnp.where(valid, jnp.where(forced, FORCE_SCORE, imp), -jnp.inf)
    k_top = min(SEL_TOP, n_sel)
    few_blocks = (q_pos0 + (i + 1) * tq - 1) // SEL_BLOCK < k_top

    @pl.when(few_blocks)
    def _():
        sel_ref[...] = jnp.where(valid, 1.0, 0.0)

    @pl.when(jnp.logical_not(few_blocks))
    def _():
        rank = jnp.zeros((tq, CMP_PAD), F32)
        for jp in range(n_sel):
            col = score[:, jp:jp + 1]
            beats = (col > score) | ((col == score) & (c > jp))
            rank = rank + jnp.where(beats, 1.0, 0.0)
        sel_ref[...] = jnp.where((rank < k_top) & (c < n_sel), 1.0, 0.0)


def nsa_cmp_select(qn, kc4, vc4, ovT, *, n_cmp, n_sel, q_pos0):
    B, T, _ = qn.shape
    tq = min(ATT_Q_TILE, T)
    return pl.pallas_call(
        functools.partial(_cmp_select_kernel, n_cmp=n_cmp, n_sel=n_sel, q_pos0=q_pos0),
        grid=(B, NSA_G, T // tq),
        in_specs=[
            pl.BlockSpec((None, tq, GROUP_W), lambda b, g, i: (b, i, g)),
            pl.BlockSpec((None, None, CMP_PAD, GROUP_W), lambda b, g, i: (b, g, 0, 0)),
            pl.BlockSpec((None, None, CMP_PAD, GROUP_W), lambda b, g, i: (b, g, 0, 0)),
            pl.BlockSpec((CMP_PAD, CMP_PAD), lambda b, g, i: (0, 0)),
        ],
        out_specs=[
            pl.BlockSpec((None, tq, GROUP_W), lambda b, g, i: (b, i, g)),
            pl.BlockSpec((None, None, tq, CMP_PAD), lambda b, g, i: (b, g, i, 0)),
        ],
        out_shape=[jax.ShapeDtypeStruct((B, T, NSA_W), F32),
                   jax.ShapeDtypeStruct((B, NSA_G, T, CMP_PAD), F32)],
        compiler_params=pltpu.CompilerParams(
            dimension_semantics=("parallel", "parallel", "parallel"), vmem_limit_bytes=VMEM_LIMIT_BYTES),
        name="nsa_cmp_select",
    )(qn, kc4, vc4, ovT)


def _flash_kernel(*refs, selected):
    if selected:
        q_ref, k_ref, v_ref, sel_ref, exp_ref, o_ref, m_ref, l_ref, acc_ref, s_a, s_b = refs
    else:
        q_ref, k_ref, v_ref, o_ref, m_ref, l_ref, acc_ref, s_a, s_b = refs
    tq = q_ref.shape[0]
    tk = ATT_K_TILE
    n_tiles = k_ref.shape[0] // tk
    i = pl.program_id(2)
    q = q_ref[...] * (NSA_HD ** -0.5)
    head = lax.broadcasted_iota(jnp.int32, q.shape, 1) // NSA_HD
    q4 = _stack_heads(q).astype(BF16)
    m_ref[...] = jnp.full(m_ref.shape, NEG_BIG, F32)
    l_ref[...] = jnp.zeros(l_ref.shape, F32)
    acc_ref[...] = jnp.zeros(acc_ref.shape, F32)
    q_pos = i * tq + lax.broadcasted_iota(jnp.int32, (tq, tk), 0)
    col = lax.broadcasted_iota(jnp.int32, (tq, tk), 1)
    if selected:
        sel = sel_ref[...].astype(BF16)
        lo = 0
    else:
        lo = jnp.maximum(i * tq - (WINDOW - 1), 0) // tk
    hi = (i * tq + tq - 1) // tk + 1

    def tile_start(j):
        return pl.multiple_of(jnp.minimum(j, n_tiles - 1) * tk, tk)

    def scores(j, s_ref):
        s_ref[...] = lax.dot_general(q4, k_ref[pl.ds(tile_start(j), tk), :], (((1,), (1,)), ((), ())),
                                     preferred_element_type=F32)

    def consume(j, s_ref):
        v = v_ref[pl.ds(tile_start(j), tk), :]
        k_pos = j * tk + col
        mask = k_pos <= q_pos
        if selected:
            mask = mask & (jnp.dot(sel, exp_ref[jnp.minimum(j, n_tiles - 1)], preferred_element_type=F32) > 0.5)
        else:
            mask = mask & (q_pos - k_pos < WINDOW)
        if not selected:
            bias = jnp.where(mask, 0.0, 2.0 * NEG_BIG)
        for h in range(NSA_HPG):
            m_old = m_ref[h]
            if selected:
                s = jnp.where(mask, s_ref[h * tq:(h + 1) * tq, :], NEG_BIG)
                m_new = jnp.maximum(m_old, jnp.max(s, axis=-1, keepdims=True))
                p = jnp.where(mask, jnp.exp(s - pltpu.repeat(m_new, tk // LANE, axis=1)), 0.0)
            else:
                s = s_ref[h * tq:(h + 1) * tq, :] + bias
                m_new = jnp.maximum(m_old, jnp.max(s, axis=-1, keepdims=True))
                p = jnp.exp(s - pltpu.repeat(m_new, tk // LANE, axis=1))
            alpha = jnp.exp(m_old - m_new)
            l_ref[h] = alpha * l_ref[h] + jnp.sum(p, axis=-1, keepdims=True)
            acc_ref[h] = (pltpu.repeat(alpha, GROUP_W // LANE, axis=1) * acc_ref[h]
                          + jnp.dot(p.astype(BF16), v, preferred_element_type=F32))
            m_ref[h] = m_new

    scores(lo, s_a)

    def body(t, carry):
        j = lo + 2 * t
        scores(j + 1, s_b)
        consume(j, s_a)
        scores(j + 2, s_a)
        consume(j + 1, s_b)
        return carry

    lax.fori_loop(0, (hi - lo + 1) // 2, body, 0)
    out = jnp.zeros((tq, GROUP_W), F32)
    for h in range(NSA_HPG):
        den = pltpu.repeat(l_ref[h], GROUP_W // LANE, axis=1)
        out = out + jnp.where(head == h, acc_ref[h] / jnp.where(den > 0, den, 1.0), 0.0)
    o_ref[...] = out


def nsa_flash(qr, k4, v4, sel=None, expand=None):
    B, T, _ = qr.shape
    tq = ATT_Q_TILE
    selected = sel is not None
    in_specs = [
        pl.BlockSpec((None, tq, GROUP_W), lambda b, g, i: (b, i, g)),
        pl.BlockSpec((None, T, GROUP_W), lambda b, g, i: (b, 0, g)),
        pl.BlockSpec((None, T, GROUP_W), lambda b, g, i: (b, 0, g)),
    ]
    args = [qr, k4, v4]
    if selected:
        in_specs += [
            pl.BlockSpec((None, None, tq, CMP_PAD), lambda b, g, i: (b, g, i, 0)),
            pl.BlockSpec(expand.shape, lambda b, g, i: (0, 0, 0)),
        ]
        args += [sel, expand]
    return pl.pallas_call(
        functools.partial(_flash_kernel, selected=selected),
        grid=(B, NSA_G, T // tq),
        in_specs=in_specs,
        out_specs=pl.BlockSpec((None, tq, GROUP_W), lambda b, g, i: (b, i, g)),
        out_shape=jax.ShapeDtypeStruct((B, T, NSA_W), F32),
        scratch_shapes=[pltpu.VMEM((NSA_HPG, tq, LANE), F32), pltpu.VMEM((NSA_HPG, tq, LANE), F32),
                        pltpu.VMEM((NSA_HPG, tq, GROUP_W), F32),
                        pltpu.VMEM((NSA_HPG * tq, ATT_K_TILE), F32), pltpu.VMEM((NSA_HPG * tq, ATT_K_TILE), F32)],
        compiler_params=pltpu.CompilerParams(
            dimension_semantics=("parallel", "parallel", "parallel"), vmem_limit_bytes=VMEM_LIMIT_BYTES),
        name="nsa_flash_sel" if selected else "nsa_flash_win",
    )(*args)


def _tile_groups(x):
    B, T = x.shape[:2]
    return jnp.broadcast_to(x[:, :, :, None, :], (B, T, NSA_G, NSA_HPG, NSA_HD)).reshape(B, T, NSA_W).astype(BF16)


def _tile_cmp(x):
    B, n = x.shape[:2]
    x = jnp.pad(jnp.moveaxis(x, 1, 2), ((0, 0), (0, 0), (0, CMP_PAD - n), (0, 0)))
    return jnp.tile(x, (1, 1, 1, NSA_HPG)).astype(BF16)


def _overlap_T(n_cmp, n_sel):
    ov = np.zeros((CMP_PAD, CMP_PAD), np.float32)
    cs = np.arange(n_cmp) * CMP_STRIDE
    ss = np.arange(n_sel) * SEL_BLOCK
    o = np.minimum(cs[None] + CMP_BLOCK, ss[:, None] + SEL_BLOCK) - np.maximum(cs[None], ss[:, None])
    ov[:n_cmp, :n_sel] = (np.clip(o, 0, None) / CMP_BLOCK).T
    return jnp.asarray(ov, dtype=BF16)


def _sel_expand(T):
    t = np.arange(T)
    e = (np.arange(CMP_PAD)[:, None] == (t // SEL_BLOCK)[None, :]).astype(np.float32)
    return jnp.asarray(e.reshape(CMP_PAD, T // ATT_K_TILE, ATT_K_TILE).transpose(1, 0, 2), dtype=BF16)


KV_ROWS = KV_SLOTS * NSA_G * NSA_HD
SLOT_ROWS = NSA_G * NSA_HD
N_PAGES = PAST_LEN // PAGE_SIZE
DEC_N_CHUNK = (PAST_LEN + DEC_SEQ) // CMP_STRIDE
DEC_N_CMP = DEC_N_CHUNK - CMP_R + 1
DEC_N_SEL = -(-(PAST_LEN + DEC_SEQ) // SEL_BLOCK)
WIN_BUF = min(WINDOW, PAST_LEN)


def _softmax_rows(s, mask, s_new=None):
    s = jnp.where(mask, s, NEG_BIG)
    m = jnp.max(s, axis=-1, keepdims=True)
    if s_new is not None:
        m = jnp.maximum(m, s_new)
    e = jnp.where(mask, jnp.exp(s - m), 0.0)
    den = jnp.sum(e, axis=-1, keepdims=True)
    if s_new is None:
        return e, den
    e_new = jnp.exp(s_new - m)
    return e, e_new, den + e_new


def _dec_nsa_kernel(pt_ref, *refs):
    pages = refs[:N_PAGES]
    (win_ref, qn_ref, qr_ref, new_ref, gate_ref, w1_ref, b1_ref, w2_ref, b2_ref, kn_ref,
     ov_ref, exp_ref, grp_ref, o_ref, xt_ref, acc_ref) = refs[N_PAGES:]
    del pt_ref
    f32 = F32
    half = 2 * NSA_HD
    n_chunk = DEC_N_CHUNK

    for p in range(N_PAGES):
        for sg in range(4):
            xt_ref[sg, p * PAGE_SIZE:(p + 1) * PAGE_SIZE, :] = pages[p][sg * half:(sg + 1) * half, :].T

    lane_lo = lax.broadcasted_iota(jnp.int32, (n_chunk, 2 * half), 1) % half < NSA_HD
    lane_grp = lax.broadcasted_iota(jnp.int32, (n_chunk, SLOT_ROWS), 1) // NSA_HD
    cmp_rows = []
    for slot in range(2):
        for gp in range(2):
            pieces = []
            for rp in range(CMP_STRIDE // 2):
                xr = jnp.concatenate(
                    [xt_ref[slot * 2 + gp, pl.ds(2 * rp + j, n_chunk, stride=CMP_STRIDE), :] for j in range(2)],
                    axis=1)
                pieces.append(jnp.concatenate([jnp.where(lane_lo, xr, 0.0), jnp.where(lane_lo, 0.0, xr)],
                                              axis=0).astype(BF16))
            acc_ref[pl.ds(gp * 2 * n_chunk, 2 * n_chunk), :] = jnp.dot(
                jnp.concatenate(pieces, axis=1), w1_ref[slot], preferred_element_type=f32)
        acc = acc_ref[...]
        pre = b1_ref[slot] + acc[:, :CMP_HID] + pltpu.roll(acc[:, CMP_HID:], NSA_G * n_chunk - 1, 0)
        out = jnp.dot(jax.nn.gelu(pre).astype(BF16), w2_ref[slot], preferred_element_type=f32) + b2_ref[slot]
        if slot == 0:
            out = _rms_rows(out, kn_ref[...])
        sel_rows = jnp.zeros((n_chunk, SLOT_ROWS), f32)
        for g in range(NSA_G):
            sel_rows = sel_rows + jnp.where(lane_grp == g, out[g * n_chunk:(g + 1) * n_chunk], 0.0)
        cmp_rows.append(sel_rows.astype(BF16))
    kc, vc = cmp_rows

    qn = qn_ref[...].astype(BF16)
    qr = qr_ref[...].astype(BF16)
    nt = (((1,), (1,)), ((), ()))
    c = lax.broadcasted_iota(jnp.int32, (NSA_H, CMP_PAD), 1)
    s = lax.dot_general(qn, kc, nt, preferred_element_type=f32)
    e, den = _softmax_rows(s, c < DEC_N_CMP)
    prob = e / jnp.where(den > 0, den, 1.0)
    o_cmp = jnp.dot(prob.astype(BF16), vc, preferred_element_type=f32)
    p_hi, p_mid = _split_bf16(prob)
    p_lo = (prob - p_hi.astype(f32) - p_mid.astype(f32)).astype(BF16)
    grp = grp_ref[...]
    psum = (jnp.dot(grp, p_hi, preferred_element_type=f32) + jnp.dot(grp, p_mid, preferred_element_type=f32)
            + jnp.dot(grp, p_lo, preferred_element_type=f32))
    imp = jnp.dot(psum.astype(BF16), ov_ref[...], preferred_element_type=f32)
    qb = (PAST_LEN + DEC_SEQ - 1) // SEL_BLOCK
    valid = c <= qb
    forced = (c == 0) | (c == qb) | (c == qb - 1)
    score = jnp.where(valid, jnp.where(forced, FORCE_SCORE, imp), -jnp.inf)
    rank = jnp.zeros((NSA_H, CMP_PAD), f32)
    for jp in range(DEC_N_SEL):
        col = score[:, jp:jp + 1]
        rank = rank + jnp.where((col > score) | ((col == score) & (c > jp)), 1.0, 0.0)
    sel = jnp.where((rank < min(SEL_TOP, DEC_N_SEL)) & (c < DEC_N_SEL), 1.0, 0.0).astype(BF16)

    new = new_ref[...]
    new_b = new.astype(BF16).astype(f32)
    qr_f = qr.astype(f32)
    s_pages = [jnp.dot(qr, pages[p][2 * SLOT_ROWS:3 * SLOT_ROWS, :].astype(BF16), preferred_element_type=f32)
               for p in range(N_PAGES)]
    s = jnp.concatenate(s_pages, axis=1)
    mask = jnp.dot(sel, exp_ref[...], preferred_element_type=f32) > 0.5
    s_new = jnp.sum(qr_f * new_b[0:1], axis=-1, keepdims=True)
    e, e_new, den = _softmax_rows(s, mask, s_new)
    e = e.astype(BF16)
    o_slc = e_new.astype(BF16).astype(f32) * new_b[1:2]
    for p in range(N_PAGES):
        o_slc = o_slc + lax.dot_general(e[:, p * PAGE_SIZE:(p + 1) * PAGE_SIZE],
                                        pages[p][3 * SLOT_ROWS:4 * SLOT_ROWS, :].astype(BF16), nt,
                                        preferred_element_type=f32)
    o_slc = o_slc / den

    s = jnp.dot(qr, win_ref[0:SLOT_ROWS, :].astype(BF16), preferred_element_type=f32)
    i_buf = lax.broadcasted_iota(jnp.int32, (NSA_H, WIN_BUF), 1)
    s_new = jnp.sum(qr_f * new_b[2:3], axis=-1, keepdims=True)
    e, e_new, den = _softmax_rows(s, WIN_BUF - i_buf < WINDOW, s_new)
    o_win = e_new.astype(BF16).astype(f32) * new_b[3:4] + lax.dot_general(
        e.astype(BF16), win_ref[SLOT_ROWS:2 * SLOT_ROWS, :].astype(BF16), nt, preferred_element_type=f32)
    o_win = o_win / den

    gates = gate_ref[...]
    o_ref[...] = gates[:, 0:1] * o_cmp + gates[:, 1:2] * o_slc + gates[:, 2:3] * o_win


def dec_nsa(page_table, cache_t, win_t, qn16, qr16, new_rows, gates, w1t, b1, w2t, b2t, kn, ovT, expand, grp):
    DB = qn16.shape[0]
    const = lambda shape: pl.BlockSpec(shape, lambda b, pt: (0,) * len(shape))
    per_b = lambda shape: pl.BlockSpec((None,) + shape, lambda b, pt: (b,) + (0,) * len(shape))
    page_specs = [pl.BlockSpec((None, KV_ROWS, PAGE_SIZE), functools.partial(lambda b, pt, p: (pt[b, p], 0, 0), p=p))
                  for p in range(N_PAGES)]
    in_specs = page_specs + [
        per_b((2 * SLOT_ROWS, WIN_BUF)), per_b((NSA_H, SLOT_ROWS)), per_b((NSA_H, SLOT_ROWS)),
        per_b((4, SLOT_ROWS)), per_b((NSA_H, 3)),
        const(w1t.shape), const(b1.shape), const(w2t.shape), const(b2t.shape), const(kn.shape),
        const(ovT.shape), const(expand.shape), const(grp.shape),
    ]
    grid_spec = pltpu.PrefetchScalarGridSpec(
        num_scalar_prefetch=1, grid=(DB,), in_specs=in_specs,
        out_specs=pl.BlockSpec((None, NSA_H, SLOT_ROWS), lambda b, pt: (b, 0, 0)),
        scratch_shapes=[pltpu.VMEM((4, PAST_LEN, 2 * NSA_HD), F32),
                        pltpu.VMEM((NSA_G * DEC_N_CHUNK, CMP_R * CMP_HID), F32)])
    return pl.pallas_call(
        _dec_nsa_kernel,
        grid_spec=grid_spec,
        out_shape=jax.ShapeDtypeStruct((DB, NSA_H, SLOT_ROWS), F32),
        compiler_params=pltpu.CompilerParams(
            dimension_semantics=("arbitrary",), vmem_limit_bytes=VMEM_LIMIT_BYTES),
        name="dec_nsa",
    )(page_table, *([cache_t] * N_PAGES), win_t, qn16, qr16, new_rows, gates, w1t, b1, w2t, b2t, kn, ovT, expand, grp)


def _win_shift_kernel(win_ref, new_ref, o_ref):
    w = win_ref[...]
    n = w.shape[1]
    row = lax.broadcasted_iota(jnp.int32, w.shape, 0)
    lane = lax.broadcasted_iota(jnp.int32, w.shape, 1)
    col = jnp.sum(jnp.where(row == lane, jnp.broadcast_to(new_ref[...], w.shape), 0.0), axis=1, keepdims=True)
    o_ref[...] = jnp.where(lane == n - 1, col, pltpu.roll(w, n - 1, 1))


def win_shift(win_t, new_row):
    DB, R, W = win_t.shape
    assert R == W
    return pl.pallas_call(
        _win_shift_kernel,
        grid=(DB,),
        in_specs=[pl.BlockSpec((None, R, W), lambda b: (b, 0, 0)), pl.BlockSpec((None, 1, R), lambda b: (b, 0, 0))],
        out_specs=pl.BlockSpec((None, R, W), lambda b: (b, 0, 0)),
        out_shape=jax.ShapeDtypeStruct((DB, R, W), win_t.dtype),
        compiler_params=pltpu.CompilerParams(dimension_semantics=("parallel",), vmem_limit_bytes=VMEM_LIMIT_BYTES),
        name="win_shift",
    )(win_t, new_row)


def _dec_cmp_weights(w1, b1, w2, b2):
    w = w1.reshape(CMP_R, CMP_STRIDE // 2, 2, 1, NSA_HD, CMP_HID)
    w = jnp.broadcast_to(w, (CMP_R, CMP_STRIDE // 2, 2, 2, NSA_HD, CMP_HID))
    w = jnp.moveaxis(w, 0, 4).reshape(CMP_STRIDE * 2 * NSA_HD, CMP_R * CMP_HID)
    return (w.astype(BF16), b1.reshape(1, CMP_HID), jnp.tile(w2, (1, NSA_G)).astype(BF16),
            jnp.tile(b2, NSA_G).reshape(1, SLOT_ROWS))


def _place_heads(q):
    own = (jnp.arange(NSA_H)[:, None] // NSA_HPG) == jnp.arange(NSA_G)[None, :]
    return jnp.where(own[None, :, :, None], q[:, :, None, :], 0.0).reshape(q.shape[0], NSA_H, SLOT_ROWS)


def _take_heads(o):
    o = o.reshape(o.shape[0], NSA_H, NSA_G, NSA_HD)
    return o[:, jnp.arange(NSA_H), jnp.arange(NSA_H) // NSA_HPG, :].reshape(o.shape[0], NSA_W)


WKV_C = 64
WKV_PAIR = 2 * RWKV_HD
WKV_T_TILE = 512
WKV_PAIRS_PER_STEP = 4


def _split_bf16(x):
    hi = x.astype(BF16)
    return hi, (x - hi.astype(F32)).astype(BF16)


def _dot3(a, b):
    a_hi, a_lo = _split_bf16(a)
    b_hi, b_lo = _split_bf16(b)
    return (jnp.dot(a_hi, b_hi, preferred_element_type=F32) + jnp.dot(a_hi, b_lo, preferred_element_type=F32)
            + jnp.dot(a_lo, b_hi, preferred_element_type=F32))


def _wkv_kernel(r_ref, lw_ref, k_ref, v_ref, a_ref, b_ref, s0_ref, y_ref, sT_ref, s_scr):
    C = WKV_C
    P = WKV_PAIR
    n_chunks = r_ref.shape[0] // C

    @pl.when(pl.program_id(2) == 0)
    def _():
        s_scr[...] = s0_ref[...]

    lo_lane = lax.broadcasted_iota(jnp.int32, (C, P), 1) < RWKV_HD
    row = lax.broadcasted_iota(jnp.int32, (2 * C, 2 * C), 0)
    col = lax.broadcasted_iota(jnp.int32, (2 * C, 2 * C), 1)
    same_head = (row // C) == (col // C)
    strict = same_head & (row > col)
    lower = same_head & (row >= col)
    eye = jnp.where(row == col, 1.0, 0.0)
    tril = jnp.where(lax.broadcasted_iota(jnp.int32, (C, C), 0) >= lax.broadcasted_iota(jnp.int32, (C, C), 1),
                     1.0, 0.0).astype(BF16)

    def stack(x):
        return jnp.concatenate([jnp.where(lo_lane, x, 0.0), jnp.where(lo_lane, 0.0, x)], axis=0)

    def chunk(c, carry):
        stages = [pair_chunk(c, q) for q in range(WKV_PAIRS_PER_STEP)]
        while stages:
            stages = [g for g in stages if next(g, True) is None]
        return carry

    def pair_chunk(c, q):
        sl = pl.ds(pl.multiple_of(c * C, C), C)
        lanes = slice(q * P, (q + 1) * P)
        r, lw, k, v, a, b = (ref[sl, lanes] for ref in (r_ref, lw_ref, k_ref, v_ref, a_ref, b_ref))
        lw_hi, lw_mid = _split_bf16(lw)
        lw_lo = (lw - lw_hi.astype(F32) - lw_mid.astype(F32)).astype(BF16)
        cs = (jnp.dot(tril, lw_hi, preferred_element_type=F32) + jnp.dot(tril, lw_mid, preferred_element_type=F32)
              + jnp.dot(tril, lw_lo, preferred_element_type=F32))
        yield
        g_inv = jnp.exp(-cs)
        g_end = jnp.exp(cs[C - 1:C, :] - cs)
        a2 = stack(a * jnp.exp(cs - lw))
        r2 = stack(r * jnp.exp(cs))
        b2 = stack(b * g_inv)
        k2 = stack(k * g_inv)
        v2 = stack(v)
        s_old = s_scr[q]
        ar = jnp.concatenate([a2, r2], axis=0).astype(BF16)
        bk = jnp.concatenate([b2, k2], axis=0).astype(BF16)
        nt = (((1,), (1,)), ((), ()))
        pp = lax.dot_general(ar, bk, nt, preferred_element_type=F32)
        from_state = lax.dot_general(ar, s_old.astype(BF16), nt, preferred_element_type=F32)
        yield
        l_ab = jnp.where(strict, pp[:2 * C, :2 * C], 0.0)
        l_ak = jnp.where(strict, pp[:2 * C, 2 * C:], 0.0)
        m_rb = jnp.where(lower, pp[2 * C:, :2 * C], 0.0)
        m_rk = jnp.where(lower, pp[2 * C:, 2 * C:], 0.0)
        v2b = v2.astype(BF16)
        rhs = from_state[:2 * C] + jnp.dot(l_ak.astype(BF16), v2b, preferred_element_type=F32)
        yield
        n = l_ab
        x = eye + n
        span = 2
        while span < C:
            n = _dot3(n, n)
            yield
            x = x + _dot3(n, x)
            yield
            span *= 2
        u2 = _dot3(x, rhs)
        yield
        uv = jnp.concatenate([u2, v2], axis=0).astype(BF16)
        y2 = from_state[2 * C:] + jnp.dot(jnp.concatenate([m_rb, m_rk], axis=1).astype(BF16), uv,
                                          preferred_element_type=F32)
        yield
        y_ref[sl, lanes] = y2[:C] + y2[C:]
        bk_end = jnp.concatenate([stack(b * g_end), stack(k * g_end)], axis=0).astype(BF16)
        s_scr[q] = s_old * jnp.exp(cs[C - 1:C, :]) + lax.dot_general(
            uv, bk_end, (((0,), (0,)), ((), ())), preferred_element_type=F32)

    lax.fori_loop(0, n_chunks, chunk, 0)

    @pl.when(pl.program_id(2) == pl.num_programs(2) - 1)
    def _():
        sT_ref[...] = s_scr[...]


def wkv7_chunked(r, lw, k, v, a, b, s0):
    B, T, W = r.shape
    n_pair = W // WKV_PAIR
    tt = min(WKV_T_TILE, T)
    s0p = s0.astype(F32).reshape(B, n_pair, 2, RWKV_HD, RWKV_HD)
    zero = jnp.zeros_like(s0p[:, :, 0])
    s0_bd = jnp.concatenate([jnp.concatenate([s0p[:, :, 0], zero], axis=-1),
                             jnp.concatenate([zero, s0p[:, :, 1]], axis=-1)], axis=-2)
    pps = WKV_PAIRS_PER_STEP
    seq = pl.BlockSpec((None, tt, pps * WKV_PAIR), lambda i, p, t: (i, t, p))
    st = pl.BlockSpec((None, pps, WKV_PAIR, WKV_PAIR), lambda i, p, t: (i, p, 0, 0))
    y, s_bd = pl.pallas_call(
        _wkv_kernel,
        grid=(B, n_pair // pps, T // tt),
        in_specs=[seq] * 6 + [st],
        out_specs=[seq, st],
        out_shape=[jax.ShapeDtypeStruct((B, T, W), F32),
                   jax.ShapeDtypeStruct((B, n_pair, WKV_PAIR, WKV_PAIR), F32)],
        scratch_shapes=[pltpu.VMEM((pps, WKV_PAIR, WKV_PAIR), F32)],
        compiler_params=pltpu.CompilerParams(
            dimension_semantics=("parallel", "parallel", "arbitrary"), vmem_limit_bytes=VMEM_LIMIT_BYTES),
        name="wkv7_chunked",
    )(r, lw, k, v, a, b, s0_bd)
    s_fin = jnp.stack([s_bd[:, :, :RWKV_HD, :RWKV_HD], s_bd[:, :, RWKV_HD:, RWKV_HD:]], axis=2)
    return y, s_fin.reshape(B, W // RWKV_HD, RWKV_HD, RWKV_HD)


AB_PAD = _round_up(AB_COLS, COL_TILE)
SHIFT_PAD = _round_up(SHIFT_W, LANE)
LORA_PAD = SHIFT_PAD - 3 * RWKV_W
EVEN_ROWS = 128
N_EVEN_PRE_OUT = 10


def _split3(x):
    hi = x.astype(BF16)
    r1 = x - hi.astype(F32)
    mid = r1.astype(BF16)
    return hi, mid, (r1 - mid.astype(F32)).astype(BF16)


def _dot_01(x, m):
    return sum(jnp.dot(part, m, preferred_element_type=F32) for part in _split3(x))


def _head_sum(x, red_ref, exp_ref):
    return _dot_01(_dot_01(x, red_ref[...]), exp_ref[...])


def _expm1(x):
    u = jnp.exp(x)
    d = u - 1.0
    log_u = jnp.where((d == 0.0) | (d == -1.0), 1.0, jnp.log(u))
    return jnp.where(d == 0.0, x, jnp.where(d == -1.0, -1.0, d * x / log_u))


def _even_pre_math(x_ref, prev, taps, prm, outs):
    (cw_ref, cb_ref, wa_ref, ba_ref, wx_ref, bx_ref, lam_ref, mu_ref, w0_ref, a0_ref, wl_ref,
     kk_ref, ka_ref, red_ref, exp_ref) = prm
    a_o, u_o, gate_o, r_o, lw_o, k_o, v_o, na_o, nb_o, g_o = outs
    t1, t2, t3 = taps
    xb = x_ref[:, 0:LRU_W]
    xc = cb_ref[...] + cw_ref[0:1] * t3 + cw_ref[1:2] * t2 + cw_ref[2:3] * t1 + cw_ref[3:4] * xb
    xcb = xc.astype(BF16)
    gate_r = jax.nn.sigmoid(jnp.dot(xcb, wa_ref[...], preferred_element_type=F32) + ba_ref[...])
    gate_i = jax.nn.sigmoid(jnp.dot(xcb, wx_ref[...], preferred_element_type=F32) + bx_ref[...])
    log_a = -LRU_C * gate_r * lam_ref[...]
    a_o[...] = jnp.exp(log_a)
    u_o[...] = jnp.sqrt(-_expm1(2.0 * log_a)) * (gate_i * xc)
    gate_o[...] = jax.nn.gelu(x_ref[:, LRU_W:2 * LRU_W])
    rw = x_ref[:, 2 * LRU_W:2 * LRU_W + SHIFT_PAD]
    rs = rw + mu_ref[...] * (prev - rw)
    r_o[...] = rs[:, 0:RWKV_W]
    k = rs[:, RWKV_W:2 * RWKV_W]
    v_o[...] = rs[:, 2 * RWKV_W:3 * RWKV_W]
    tail = rs[:, 3 * RWKV_W:]
    lane = lax.broadcasted_iota(jnp.int32, tail.shape, 1)
    act = jnp.where(lane < W_LORA, jnp.tanh(tail), jnp.where(lane < W_LORA + A_LORA, tail, jax.nn.sigmoid(tail)))
    z = jnp.dot(act.astype(BF16), wl_ref[...], preferred_element_type=F32)
    w_log = -jax.nn.softplus(-(w0_ref[...] + z[:, 0:RWKV_W])) - 0.5
    lw_o[...] = -jnp.exp(w_log)
    a_icl = jax.nn.sigmoid(a0_ref[...] + z[:, RWKV_W:2 * RWKV_W])
    g_o[...] = z[:, 2 * RWKV_W:]
    kk = k * kk_ref[...]
    kk = kk / jnp.maximum(jnp.sqrt(_head_sum(kk * kk, red_ref, exp_ref)), 1e-12)
    k_o[...] = k * (1.0 + (a_icl - 1.0) * ka_ref[...])
    na_o[...] = -kk
    nb_o[...] = kk * a_icl


def _even_pre_seq_kernel(x_ref, conv0_ref, shift0_ref, *refs):
    prm = refs[:15]
    outs = refs[15:15 + N_EVEN_PRE_OUT]
    conv_c, shift_c = refs[15 + N_EVEN_PRE_OUT:]
    rows = x_ref.shape[0]

    @pl.when(pl.program_id(1) == 0)
    def _():
        conv_c[...] = conv0_ref[...]
        shift_c[...] = shift0_ref[...]

    xb = x_ref[:, 0:LRU_W]
    row = lax.broadcasted_iota(jnp.int32, xb.shape, 0)
    taps = []
    for j in (1, 2, 3):
        tap = pltpu.roll(xb, j, 0)
        for i in range(j):
            tap = jnp.where(row == i, conv_c[8 - j + i:9 - j + i, :], tap)
        taps.append(tap)
    rw = x_ref[:, 2 * LRU_W:2 * LRU_W + SHIFT_PAD]
    row_w = lax.broadcasted_iota(jnp.int32, rw.shape, 0)
    prev = jnp.where(row_w == 0, shift_c[7:8, :], pltpu.roll(rw, 1, 0))
    _even_pre_math(x_ref, prev, taps, prm, outs)
    conv_c[...] = x_ref[rows - 8:rows, 0:LRU_W]
    shift_c[...] = x_ref[rows - 8:rows, 2 * LRU_W:2 * LRU_W + SHIFT_PAD]


def _even_pre_step_kernel(x_ref, prev_ref, t1_ref, t2_ref, t3_ref, *refs):
    _even_pre_math(x_ref, prev_ref[...], (t1_ref[...], t2_ref[...], t3_ref[...]), refs[:15], refs[15:])


def _even_params(p):
    def bd(w):
        eye = jnp.eye(LRU_BLOCKS, dtype=w.dtype)
        return (eye[:, None, :, None] * w[:, :, None, :]).reshape(LRU_W, LRU_W).astype(BF16)
    row = lambda v: v.reshape(1, -1).astype(F32)
    wl = jnp.zeros((LORA_PAD, 3 * RWKV_W), F32)
    wl = wl.at[0:W_LORA, 0:RWKV_W].set(p['w2'])
    wl = wl.at[W_LORA:W_LORA + A_LORA, RWKV_W:2 * RWKV_W].set(p['a2'])
    wl = wl.at[W_LORA + A_LORA:W_LORA + A_LORA + G_LORA, 2 * RWKV_W:].set(p['g2'])
    head = np.arange(RWKV_W) // RWKV_HD
    red = jnp.asarray(head[:, None] == np.arange(LANE)[None, :], dtype=BF16)
    mu = jnp.pad(p['mu'], (0, SHIFT_PAD - SHIFT_W))
    return [p['conv_w'].astype(F32), row(p['conv_b']), bd(p['wa']), row(p['ba']), bd(p['wx']), row(p['bx']),
            row(jax.nn.softplus(-p['lam'].astype(F32))), row(mu), row(p['w0']), row(p['a0']), wl.astype(BF16),
            row(p['k_k']), row(p['k_a']), red, red.T]


def _const_spec(a, n_grid):
    return pl.BlockSpec(a.shape, lambda *_: (0,) * a.ndim)


def even_pre_seq(proj, conv0, shift0, prm, B, T):
    tr = EVEN_ROWS
    nt = T // tr
    conv_pad = jnp.pad(conv0.astype(F32), ((0, 0), (8 - (CONV_W - 1), 0), (0, 0)))
    shift_pad = jnp.pad(shift0.astype(F32)[:, None, :], ((0, 0), (7, 0), (0, SHIFT_PAD - SHIFT_W)))
    out_spec = pl.BlockSpec((tr, LRU_W), lambda b, t: (b * nt + t, 0))
    return pl.pallas_call(
        _even_pre_seq_kernel,
        grid=(B, nt),
        in_specs=[pl.BlockSpec((tr, AB_PAD), lambda b, t: (b * nt + t, 0)),
                  pl.BlockSpec((None, 8, LRU_W), lambda b, t: (b, 0, 0)),
                  pl.BlockSpec((None, 8, SHIFT_PAD), lambda b, t: (b, 0, 0))] + [_const_spec(a, 2) for a in prm],
        out_specs=[out_spec] * N_EVEN_PRE_OUT,
        out_shape=[jax.ShapeDtypeStruct((B * T, LRU_W), F32)] * N_EVEN_PRE_OUT,
        scratch_shapes=[pltpu.VMEM((8, LRU_W), F32), pltpu.VMEM((8, SHIFT_PAD), F32)],
        compiler_params=pltpu.CompilerParams(
            dimension_semantics=("parallel", "arbitrary"), vmem_limit_bytes=VMEM_LIMIT_BYTES),
        name="even_pre_seq",
    )(proj, conv_pad, shift_pad, *prm)


def even_pre_step(proj, row0, conv0, shift0, prm):
    n = conv0.shape[0]
    shift_pad = jnp.pad(shift0.astype(F32), ((0, 0), (0, SHIFT_PAD - SHIFT_W)))
    taps = [conv0[:, CONV_W - 1 - j].astype(F32) for j in (1, 2, 3)]
    full = lambda w: pl.BlockSpec((n, w), lambda i: (0, 0))
    return pl.pallas_call(
        _even_pre_step_kernel,
        grid=(1,),
        in_specs=[pl.BlockSpec((n, AB_PAD), lambda i: (row0 // n, 0)), full(SHIFT_PAD)] + [full(LRU_W)] * 3
        + [_const_spec(a, 1) for a in prm],
        out_specs=[full(LRU_W)] * N_EVEN_PRE_OUT,
        out_shape=[jax.ShapeDtypeStruct((n, LRU_W), F32)] * N_EVEN_PRE_OUT,
        compiler_params=pltpu.CompilerParams(
            dimension_semantics=("arbitrary",), vmem_limit_bytes=VMEM_LIMIT_BYTES),
        name="even_pre_step",
    )(proj, shift_pad, *taps, *prm)


def _even_post_kernel(hs_ref, gate_ref, y_ref, r_ref, k_ref, v_ref, g_ref, lng_ref, lnb_ref, rk_ref,
                      red_ref, exp_ref, *rest):
    o_ref = rest[-1]
    y = y_ref[...]
    mu = _head_sum(y, red_ref, exp_ref) * (1.0 / RWKV_HD)
    d = y - mu
    var = _head_sum(d * d, red_ref, exp_ref) * (1.0 / RWKV_HD)
    yn = d * lax.rsqrt(var + 64e-5) * lng_ref[...] + lnb_ref[...]
    bonus = _head_sum(r_ref[...] * k_ref[...] * rk_ref[...], red_ref, exp_ref) * v_ref[...]
    o_ref[:, 0:LRU_W] = (hs_ref[...] * gate_ref[...]).astype(o_ref.dtype)
    o_ref[:, LRU_W:] = ((yn + bonus) * g_ref[...]).astype(o_ref.dtype)


def even_post(hs, gate, y, r, k, v, g, p, red, n_total, row0, prior=None):
    n = hs.shape[0]
    tr = EVEN_ROWS
    row = lambda a: a.reshape(1, -1).astype(F32)
    consts = [row(p['ln_g']), row(p['ln_b']), row(p['r_k']), red, red.T]
    seq = pl.BlockSpec((tr, LRU_W), lambda i: (i, 0))
    args = [hs, gate, y, r, k, v, g] + consts
    in_specs = [seq] * 7 + [_const_spec(a, 1) for a in consts]
    aliases = {}
    if prior is not None:
        args.append(prior)
        in_specs.append(pl.BlockSpec(memory_space=pl.ANY))
        aliases = {len(args) - 1: 0}
    return pl.pallas_call(
        _even_post_kernel,
        grid=(n // tr,),
        in_specs=in_specs,
        out_specs=pl.BlockSpec((tr, D_MODEL), lambda i: (row0 // tr + i, 0)),
        out_shape=jax.ShapeDtypeStruct((n_total, D_MODEL), BF16),
        input_output_aliases=aliases,
        compiler_params=pltpu.CompilerParams(
            dimension_semantics=("parallel",), vmem_limit_bytes=VMEM_LIMIT_BYTES),
        name="even_post",
    )(*args)


def _retention_kernel(q_ref, k_ref, va_ref, vb_ref, dm_ref, rd_ref, kd_ref, sd_ref, o_ref, s_out_ref, s_scr):
    C = q_ref.shape[0]
    n_pair = RET_H // 2

    @pl.when(pl.program_id(1) == 0)
    def _():
        s_scr[...] = jnp.zeros(s_scr.shape, F32)

    lo = lax.broadcasted_iota(jnp.int32, (C, 2 * RET_DK), 1) < RET_DK

    def stack(x):
        return jnp.concatenate([jnp.where(lo, x, 0.0), jnp.where(lo, 0.0, x)], axis=0)

    for p in range(n_pair):
        qk = slice(p * 2 * RET_DK, (p + 1) * 2 * RET_DK)
        q2 = stack(q_ref[:, qk]).astype(BF16)
        k2 = stack(k_ref[:, qk])
        v0 = p * 2 * RET_DV
        v_ref = va_ref if p < n_pair // 2 else vb_ref
        vl = v0 % (RET_W // 2)
        v2 = jnp.concatenate([v_ref[:, vl:vl + RET_DV], v_ref[:, vl + RET_DV:vl + 2 * RET_DV]],
                             axis=0).astype(BF16)
        s = lax.dot_general(q2, k2.astype(BF16), (((1,), (1,)), ((), ())), preferred_element_type=F32) * dm_ref[p]
        s_old = s_scr[p]
        o2 = jnp.dot(s.astype(BF16), v2, preferred_element_type=F32) + jnp.dot(
            q2, s_old.astype(BF16), preferred_element_type=F32) * rd_ref[p]
        o_ref[:, v0:v0 + RET_DV] = o2[:C]
        o_ref[:, v0 + RET_DV:v0 + 2 * RET_DV] = o2[C:]
        s_scr[p] = s_old * sd_ref[p] + lax.dot_general((k2 * kd_ref[p]).astype(BF16), v2, (((0,), (0,)), ((), ())),
                                                       preferred_element_type=F32)

    @pl.when(pl.program_id(1) == pl.num_programs(1) - 1)
    def _():
        s_out_ref[...] = s_scr[...]


def retention_prompt_pallas(rq, rk, rv, B, T, v_col0=0):
    C = RET_CHUNK
    nc = T // C
    f32 = F32
    lg = jnp.log1p(-jnp.exp2(-5.0 - jnp.arange(RET_H, dtype=f32))).reshape(RET_H // 2, 2)
    i = jnp.arange(C, dtype=f32)
    diff = i[:, None] - i[None, :]
    causal = diff >= 0
    dmask = jnp.where(causal, jnp.exp(jnp.where(causal, diff, 0.0)[None, None] * lg[:, :, None, None]), 0.0)
    zero = jnp.zeros_like(dmask[:, 0])
    dm = jnp.concatenate([jnp.concatenate([dmask[:, 0], zero], axis=-1),
                          jnp.concatenate([zero, dmask[:, 1]], axis=-1)], axis=-2)
    rows = lambda x, w: jnp.broadcast_to(x[:, :, :, None], x.shape + (w,)).reshape(RET_H // 2, -1, w)
    rd = rows(jnp.exp((i[None, None, :] + 1.0) * lg[:, :, None]), RET_DV)
    kd = rows(jnp.exp((C - 1.0 - i)[None, None, :] * lg[:, :, None]), 2 * RET_DK)
    sd = rows(jnp.broadcast_to(jnp.exp(C * lg)[:, :, None], (RET_H // 2, 2, RET_DK)), RET_DV)
    half_w = RET_W // 2
    qk_spec = pl.BlockSpec((C, RET_H * RET_DK), lambda b, c: (b * nc + c, 0))
    v_spec = lambda k: pl.BlockSpec((C, half_w), lambda b, c: (b * nc + c, v_col0 // half_w + k))
    const = lambda a: pl.BlockSpec(a.shape, lambda b, c: (0, 0, 0))
    o, s = pl.pallas_call(
        _retention_kernel,
        grid=(B, nc),
        in_specs=[qk_spec, qk_spec, v_spec(0), v_spec(1), const(dm), const(rd), const(kd), const(sd)],
        out_specs=[pl.BlockSpec((C, RET_W), lambda b, c: (b * nc + c, 0)),
                   pl.BlockSpec((None, RET_H // 2, 2 * RET_DK, RET_DV), lambda b, c: (b, 0, 0, 0))],
        out_shape=[jax.ShapeDtypeStruct((B * T, RET_W), f32),
                   jax.ShapeDtypeStruct((B, RET_H // 2, 2 * RET_DK, RET_DV), f32)],
        scratch_shapes=[pltpu.VMEM((RET_H // 2, 2 * RET_DK, RET_DV), f32)],
        compiler_params=pltpu.CompilerParams(
            dimension_semantics=("parallel", "arbitrary"), vmem_limit_bytes=VMEM_LIMIT_BYTES),
        name="retention_prompt",
    )(rq, rk, rv, rv, dm, rd, kd, sd)
    return s.reshape(B, RET_H, RET_DK, RET_DV), o


KV_W = NSA_G * NSA_HD
RET_QK_W = RET_H * RET_DK
OFF_Q = 0
OFF_KC = OFF_Q + NSA_W
OFF_VC = OFF_KC + KV_W
OFF_KS = OFF_VC + KV_W
OFF_VS = OFF_KS + KV_W
OFF_KW = OFF_VS + KV_W
OFF_VW = OFF_KW + KV_W
OFF_RQ = OFF_VW + KV_W
OFF_RK = OFF_RQ + RET_QK_W
OFF_RV = OFF_RK + RET_QK_W
OFF_RG = OFF_RV + RET_W
OFF_GT = OFF_RG + RET_W
CD_PAD = _round_up(OFF_GT + LANE, COL_TILE)
ODD_ROWS = 128
N_ODD_PRE_OUT = 11


def _odd_weight_cols(w):
    gt0 = NSA_W + 6 * KV_W
    body = jnp.concatenate([w[:, :gt0], w[:, gt0 + 3 * NSA_H:]], axis=1)
    gt = w[:, gt0:gt0 + 3 * NSA_H]
    out = jnp.concatenate([body, gt], axis=1)
    return jnp.pad(out, ((0, 0), (0, CD_PAD - out.shape[1]))).astype(BF16)


def _rope_tables(pos, n_rot, theta, head):
    half = n_rot // 2
    inv = jnp.exp(-jnp.log(jnp.float32(theta)) * jnp.arange(half, dtype=jnp.float32) / half)
    ang = pos.astype(jnp.float32)[:, None] * inv[None, :]
    cos, sin = jnp.cos(ang), jnp.sin(ang)
    d = np.arange(LANE) % head
    cos_d, sin_d = cos[:, d % half], sin[:, d % half]
    c = jnp.where(d < n_rot, cos_d, 1.0)
    s1 = jnp.where(d < half, -sin_d, 0.0)
    s2 = jnp.where((d >= half) & (d < n_rot), sin_d, 0.0)
    return jnp.stack([c, s1, s2])


def _rope_lanes(x, tab_ref, half):
    w = x.shape[1]
    rep = w // LANE
    c, s1, s2 = (pltpu.repeat(tab_ref[i], rep, axis=1) for i in range(3))
    return x * c + pltpu.roll(x, w - half, 1) * s1 + pltpu.roll(x, half, 1) * s2


def _rms_heads(x, g_ref, red_ref, exp_ref):
    ms = _head_sum(x * x, red_ref, exp_ref) * (1.0 / NSA_HD)
    return x * lax.rsqrt(ms + 1e-6) * g_ref[...]


def _odd_pre_kernel(x_ref, nsa_tab, ret_tab, qg_ref, ksg_ref, kwg_ref, redq_ref, expq_ref, redk_ref, expk_ref,
                    tile_ref, qn_o, qr_o, ks_o, kw_o, ks4_o, vs4_o, kw4_o, vw4_o, gate_o, rq_o, rk_o, kvt_o, wint_o):
    nsa_half = ROPE_DIMS // 2
    qn = _rms_heads(x_ref[:, OFF_Q:OFF_Q + NSA_W], qg_ref, redq_ref, expq_ref)
    qn_o[...] = qn
    qr_o[...] = _rope_lanes(qn, nsa_tab, nsa_half)
    ks = _rope_lanes(_rms_heads(x_ref[:, OFF_KS:OFF_KS + KV_W], ksg_ref, redk_ref, expk_ref), nsa_tab, nsa_half)
    kw = _rope_lanes(_rms_heads(x_ref[:, OFF_KW:OFF_KW + KV_W], kwg_ref, redk_ref, expk_ref), nsa_tab, nsa_half)
    ks_o[...] = ks
    kw_o[...] = kw
    tile = tile_ref[...]
    for src, dst in ((ks, ks4_o), (x_ref[:, OFF_VS:OFF_VS + KV_W], vs4_o), (kw, kw4_o),
                     (x_ref[:, OFF_VW:OFF_VW + KV_W], vw4_o)):
        dst[...] = jnp.dot(src.astype(BF16), tile, preferred_element_type=F32).astype(BF16)
    gate_o[...] = jax.nn.sigmoid(x_ref[:, OFF_GT:OFF_GT + LANE])
    rq_o[...] = _rope_lanes(x_ref[:, OFF_RQ:OFF_RQ + RET_QK_W], ret_tab, RET_DK // 2)
    rk_o[...] = _rope_lanes(x_ref[:, OFF_RK:OFF_RK + RET_QK_W], ret_tab, RET_DK // 2) * (RET_DK ** -0.5)
    kv_pieces = (x_ref[:, OFF_KC:OFF_KC + KV_W], x_ref[:, OFF_VC:OFF_VC + KV_W], ks, x_ref[:, OFF_VS:OFF_VS + KV_W])
    for dst, pieces in ((kvt_o, kv_pieces), (wint_o, (kw, x_ref[:, OFF_VW:OFF_VW + KV_W]))):
        for s, piece in enumerate(pieces):
            for c in range(KV_W // LANE):
                dst[s * KV_W + c * LANE:s * KV_W + (c + 1) * LANE, :] = piece[:, c * LANE:(c + 1) * LANE].T


def odd_pre(proj, pos, p, row0, n_rows, same_pos, seq_len):
    tr = ODD_ROWS
    blk0 = row0 // tr
    n_tab = tr if same_pos else n_rows
    pos_rows = jnp.broadcast_to(pos, (n_tab,)) if same_pos else pos
    nsa_tab = _rope_tables(pos_rows, ROPE_DIMS, ROPE_THETA, NSA_HD)
    ret_tab = _rope_tables(pos_rows, RET_DK, RET_THETA, RET_DK)
    row = lambda v, rep: jnp.tile(v.astype(F32), rep).reshape(1, -1)
    lanes = np.arange(LANE)
    red_q = jnp.asarray((np.arange(NSA_W) // NSA_HD)[:, None] == lanes[None, :], dtype=BF16)
    red_k = jnp.asarray((np.arange(KV_W) // NSA_HD)[:, None] == lanes[None, :], dtype=BF16)
    src = np.arange(KV_W)
    dst = np.arange(NSA_W)
    tile = jnp.asarray((src[:, None] // NSA_HD == dst[None, :] // GROUP_W)
                       & (src[:, None] % NSA_HD == dst[None, :] % NSA_HD), dtype=BF16)
    consts = [row(p['q_norm'], NSA_H), row(p['k_norm'][1], NSA_G), row(p['k_norm'][2], NSA_G),
              red_q, red_q.T, red_k, red_k.T, tile]
    tab_spec = pl.BlockSpec((3, tr, LANE), (lambda i: (0, 0, 0)) if same_pos else (lambda i: (0, i, 0)))
    out = lambda w, dt: (pl.BlockSpec((tr, w), lambda i: (i, 0)), jax.ShapeDtypeStruct((n_rows, w), dt))
    seq_tiles = seq_len // tr
    out_t = lambda r: (pl.BlockSpec((None, r, tr), lambda i: (i // seq_tiles, 0, i % seq_tiles)),
                       jax.ShapeDtypeStruct((n_rows // seq_len, r, seq_len), F32))
    outs = [out(NSA_W, F32), out(NSA_W, F32), out(KV_W, F32), out(KV_W, F32)] + [out(NSA_W, BF16)] * 4 + [
        out(LANE, F32), out(RET_QK_W, F32), out(RET_QK_W, F32), out_t(KV_SLOTS * KV_W), out_t(2 * KV_W)]
    return pl.pallas_call(
        _odd_pre_kernel,
        grid=(n_rows // tr,),
        in_specs=[pl.BlockSpec((tr, CD_PAD), lambda i: (blk0 + i, 0)), tab_spec, tab_spec]
        + [_const_spec(a, 1) for a in consts],
        out_specs=[o[0] for o in outs],
        out_shape=[o[1] for o in outs],
        compiler_params=pltpu.CompilerParams(
            dimension_semantics=("parallel",), vmem_limit_bytes=VMEM_LIMIT_BYTES),
        name="odd_pre",
    )(proj, nsa_tab, ret_tab, *consts)


def _odd_post_kernel(oc_ref, os_ref, ow_ref, gate_ref, ret_ref, rg0_ref, rg1_ref, gng_ref, gnb_ref, ge_ref, *rest,
                     gated):
    o_ref = rest[-1]
    if gated:
        nsa = oc_ref[...]
    else:
        gates = gate_ref[...]
        nsa = jnp.zeros(oc_ref.shape, F32)
        for j, branch in enumerate((oc_ref, os_ref, ow_ref)):
            nsa = nsa + _dot_01(gates, ge_ref[j]) * branch[...]
    o_ref[:, 0:NSA_W] = nsa.astype(o_ref.dtype)
    for h in range(RET_H):
        lanes = slice(h * RET_DV, (h + 1) * RET_DV)
        x = ret_ref[:, lanes]
        mu = jnp.mean(x, axis=-1, keepdims=True)
        d = x - mu
        var = jnp.mean(d * d, axis=-1, keepdims=True)
        yn = d * lax.rsqrt(var + 1e-5) * gng_ref[:, lanes] + gnb_ref[:, lanes]
        rg = (rg0_ref if h < RET_H // 2 else rg1_ref)[:, (h % (RET_H // 2)) * RET_DV:(h % (RET_H // 2) + 1) * RET_DV]
        o_ref[:, NSA_W + h * RET_DV:NSA_W + (h + 1) * RET_DV] = (yn * (rg * jax.nn.sigmoid(rg))).astype(o_ref.dtype)


def odd_post(o_cmp, o_slc, o_win, gates, o_ret, proj, p, n_total, row0, prior=None, gated=False):
    n = o_cmp.shape[0]
    tr = ODD_ROWS
    blk0 = row0 // tr
    h = np.arange(NSA_W) // NSA_HD
    ge = jnp.asarray(np.stack([(np.arange(LANE)[:, None] == (3 * h + j)[None, :]) for j in range(3)]), dtype=BF16)
    row = lambda a: a.reshape(1, -1).astype(F32)
    consts = [row(p['gn_g']), row(p['gn_b']), ge]
    seq = lambda w: pl.BlockSpec((tr, w), lambda i: (i, 0))
    half = RET_W // 2
    rg_spec = lambda k: pl.BlockSpec((tr, half), lambda i: (blk0 + i, OFF_RG // half + k))
    args = [o_cmp, o_slc, o_win, gates, o_ret, proj, proj] + consts
    in_specs = [seq(NSA_W)] * 3 + [seq(LANE), seq(RET_W), rg_spec(0), rg_spec(1)] + [_const_spec(a, 1) for a in consts]
    aliases = {}
    if prior is not None:
        args.append(prior)
        in_specs.append(pl.BlockSpec(memory_space=pl.ANY))
        aliases = {len(args) - 1: 0}
    return pl.pallas_call(
        functools.partial(_odd_post_kernel, gated=gated),
        grid=(n // tr,),
        in_specs=in_specs,
        out_specs=pl.BlockSpec((tr, D_MODEL), lambda i: (blk0 + i, 0)),
        out_shape=jax.ShapeDtypeStruct((n_total, D_MODEL), BF16),
        input_output_aliases=aliases,
        compiler_params=pltpu.CompilerParams(
            dimension_semantics=("parallel",), vmem_limit_bytes=VMEM_LIMIT_BYTES),
        name="odd_post",
    )(*args)


def rms_norm(x, g, eps=1e-6):
    xf = x.astype(jnp.float32)
    y = xf * lax.rsqrt(jnp.mean(xf * xf, axis=-1, keepdims=True) + eps)
    return (y * g.astype(jnp.float32)).astype(x.dtype)


def head_group_norm(y, g, b, eps):
    yf = y.astype(jnp.float32)
    mu = jnp.mean(yf, axis=-1, keepdims=True)
    var = jnp.mean(jnp.square(yf - mu), axis=-1, keepdims=True)
    yn = ((yf - mu) * lax.rsqrt(var + eps)).reshape(y.shape[:-2] + (-1,))
    return (yn * g.astype(jnp.float32) + b.astype(jnp.float32)).astype(y.dtype)


def masked_softmax(s, mask):
    s = jnp.where(mask, s.astype(jnp.float32), -jnp.inf)
    m = jnp.max(s, axis=-1, keepdims=True)
    e = jnp.exp(s - jnp.where(jnp.isfinite(m), m, 0.0))
    den = jnp.sum(e, axis=-1, keepdims=True)
    return e / jnp.where(den > 0, den, 1.0)


def rope(x, pos, n_rot, theta):
    half = n_rot // 2
    inv = jnp.exp(-jnp.log(jnp.float32(theta)) * jnp.arange(half, dtype=jnp.float32) / half)
    ang = pos.astype(jnp.float32)[:, None] * inv[None, :]
    cos = jnp.cos(ang)[None, :, None, :]
    sin = jnp.sin(ang)[None, :, None, :]
    xf = x.astype(jnp.float32)
    x1, x2 = xf[..., :half], xf[..., half:n_rot]
    out = jnp.concatenate([x1 * cos - x2 * sin, x2 * cos + x1 * sin, xf[..., n_rot:]], axis=-1)
    return out.astype(x.dtype)


def linear_scan(a, b, h0):
    b = b.at[:, 0].add(a[:, 0] * h0)

    def combine(left, right):
        return left[0] * right[0], right[0] * left[1] + right[1]

    return lax.associative_scan(combine, (a, b), axis=1)[1]


def wkv7_scan(r, w, k, v, a, b, s0):
    xs = tuple(jnp.moveaxis(z.astype(jnp.float32), 1, 0) for z in (r, w, k, v, a, b))

    def step(S, inp):
        r_t, w_t, k_t, v_t, a_t, b_t = inp
        sa = jnp.einsum('bhij,bhj->bhi', S, a_t)
        S = S * w_t[:, :, None, :] + sa[..., None] * b_t[:, :, None, :] + v_t[..., None] * k_t[:, :, None, :]
        return S, jnp.einsum('bhij,bhj->bhi', S, r_t)

    S, ys = lax.scan(step, s0.astype(jnp.float32), xs)
    return jnp.moveaxis(ys, 0, 1), S


def even_mixer_core(proj, p, lru_h0, lru_conv0, shift0, wkv0):
    B, T, _ = proj.shape
    f32 = jnp.float32
    dt = proj.dtype
    xb, gb, rw = jnp.split(proj, [LRU_W, 2 * LRU_W], axis=-1)
    xcat = jnp.concatenate([lru_conv0.astype(dt), xb], axis=1)
    xc = p['conv_b'] + sum(p['conv_w'][j] * xcat[:, j:j + T] for j in range(CONV_W))
    xbd = xc.reshape(B, T, LRU_BLOCKS, LRU_BS)
    gate_r = jax.nn.sigmoid(jnp.einsum('btnc,ncd->btnd', xbd, p['wa']).reshape(B, T, LRU_W) + p['ba'])
    gate_i = jax.nn.sigmoid(jnp.einsum('btnc,ncd->btnd', xbd, p['wx']).reshape(B, T, LRU_W) + p['bx'])
    log_a = -LRU_C * gate_r.astype(f32) * jax.nn.softplus(-p['lam'].astype(f32))
    u = jnp.sqrt(-jnp.expm1(2.0 * log_a)) * (gate_i * xc).astype(f32)
    hs = lru_scan(jnp.exp(log_a), u, lru_h0.astype(f32))
    y_lru = hs.astype(dt) * jax.nn.gelu(gb)
    prev = jnp.concatenate([shift0.astype(dt)[:, None], rw[:, :-1]], axis=1)
    rs = rw + p['mu'] * (prev - rw)
    r, k, v, xw, xa, xg = jnp.split(
        rs, [RWKV_W, 2 * RWKV_W, 3 * RWKV_W, 3 * RWKV_W + W_LORA, 3 * RWKV_W + W_LORA + A_LORA], axis=-1)
    w_log = -jax.nn.softplus(-(p['w0'] + jnp.tanh(xw) @ p['w2']).astype(f32)) - 0.5
    log_decay = -jnp.exp(w_log)
    decay = jnp.exp(log_decay)
    a_icl = jax.nn.sigmoid(p['a0'] + xa @ p['a2'])
    g = jax.nn.sigmoid(xg) @ p['g2']
    heads = (B, T, RWKV_H, RWKV_HD)
    kk = (k * p['k_k']).reshape(heads).astype(f32)
    kk = kk / jnp.maximum(jnp.sqrt(jnp.sum(kk * kk, axis=-1, keepdims=True)), 1e-12)
    k = k * (1.0 + (a_icl - 1.0) * p['k_a'])
    rh, kh, vh, ah = (z.reshape(heads) for z in (r, k, v, a_icl))
    if T % WKV_C == 0:
        y, wkv = wkv7_chunked(r.astype(f32), log_decay, k.astype(f32), v.astype(f32),
                              (-kk).reshape(B, T, RWKV_W), (kk * ah.astype(f32)).reshape(B, T, RWKV_W), wkv0)
        y = y.reshape(heads)
    else:
        y, wkv = wkv7_scan(rh, decay.reshape(heads), kh, vh, -kk, kk * ah.astype(f32), wkv0)
    y = head_group_norm(y, p['ln_g'], p['ln_b'], 64e-5).astype(dt)
    bonus = (jnp.sum(rh * kh * p['r_k'], axis=-1, keepdims=True) * vh).reshape(B, T, RWKV_W)
    y_rwkv = (y + bonus) * g
    cat = jnp.concatenate([y_lru, y_rwkv], axis=-1)
    return cat, hs[:, -1], xcat[:, T:], rw[:, -1], wkv


def even_mixer(proj, p, B, T, DB, lru_h0, lru_conv0, shift0, wkv0):
    f32 = F32
    prm = _even_params(p)
    red = prm[-2]
    n_p = B * T
    zeros = lambda *s: jnp.zeros(s, f32)
    a, u, gate, r, lw, k, v, na, nb, g = even_pre_seq(proj, zeros(B, CONV_W - 1, LRU_W), zeros(B, SHIFT_W), prm, B, T)
    seq = lambda z: z.reshape(B, T, LRU_W)
    hs = lru_scan(seq(a), seq(u), zeros(B, LRU_W))
    yw, wkv_p = wkv7_chunked(seq(r), seq(lw), seq(k), seq(v), seq(na), seq(nb), zeros(B, RWKV_H, RWKV_HD, RWKV_HD))
    cat = even_post(hs.reshape(n_p, LRU_W), gate, yw.reshape(n_p, RWKV_W), r, k, v, g, p, red, n_p + DB, 0)
    tail = lambda b, n, c0, c1: proj[(b + 1) * T - n:(b + 1) * T, c0:c1]
    st_p = (hs[:, -1], jnp.stack([tail(b, CONV_W - 1, 0, LRU_W) for b in range(B)]),
            jnp.concatenate([tail(b, 1, 2 * LRU_W, AB_COLS) for b in range(B)], axis=0), wkv_p)
    a, u, gate, r, lw, k, v, na, nb, g = even_pre_step(proj, n_p, lru_conv0, shift0, prm)
    hs_s = a * lru_h0.astype(f32) + u
    heads = (DB, 1, RWKV_H, RWKV_HD)
    yw, wkv_s = wkv7_scan(r.reshape(heads), jnp.exp(lw).reshape(heads), k.reshape(heads), v.reshape(heads),
                          na.reshape(heads), nb.reshape(heads), wkv0)
    cat = even_post(hs_s, gate, yw.reshape(DB, RWKV_W), r, k, v, g, p, red, n_p + DB, n_p, prior=cat)
    xb_s = proj[n_p:]
    conv_s = jnp.concatenate([lru_conv0[:, 1:].astype(f32), xb_s[:, None, :LRU_W]], axis=1)
    st_s = (hs_s, conv_s, xb_s[:, 2 * LRU_W:AB_COLS], wkv_s)
    return cat, st_p, st_s


def odd_project(proj, p, pos):
    B, T, _ = proj.shape
    sizes = [NSA_W] + [NSA_G * NSA_HD] * 6 + [3 * NSA_H, RET_H * RET_DK, RET_H * RET_DK, RET_W, RET_W]
    q, kc, vc, ks, vs, kw, vw, gt, rq, rk, rv, rg = jnp.split(
        proj, np.cumsum(sizes).tolist(), axis=-1)[:len(sizes)]
    kvs = (B, T, NSA_G, NSA_HD)
    q_n = rms_norm(q.reshape(B, T, NSA_H, NSA_HD), p['q_norm'])
    return {
        'q_n': q_n,
        'q_r': rope(q_n, pos, ROPE_DIMS, ROPE_THETA),
        'kc': kc.reshape(kvs), 'vc': vc.reshape(kvs),
        'ks': rope(rms_norm(ks.reshape(kvs), p['k_norm'][1]), pos, ROPE_DIMS, ROPE_THETA),
        'vs': vs.reshape(kvs),
        'kw': rope(rms_norm(kw.reshape(kvs), p['k_norm'][2]), pos, ROPE_DIMS, ROPE_THETA),
        'vw': vw.reshape(kvs),
        'gates': jax.nn.sigmoid(gt).reshape(B, T, NSA_H, 3),
        'rq': rope(rq.reshape(B, T, RET_H, RET_DK), pos, RET_DK, RET_THETA),
        'rk': rope(rk.reshape(B, T, RET_H, RET_DK), pos, RET_DK, RET_THETA) * (RET_DK ** -0.5),
        'rv': rv.reshape(B, T, RET_H, RET_DV),
        'rg': rg,
    }


def to_groups_q(q):
    B, T = q.shape[:2]
    return jnp.moveaxis(q.reshape(B, T, NSA_G, NSA_HPG, NSA_HD), 1, 3)


def to_groups_k(k):
    return jnp.moveaxis(k, 1, 2)


def nsa_compress(x, w1, b1, w2, b2):
    B, L = x.shape[:2]
    n_chunk = L // CMP_STRIDE
    n_cmp = n_chunk - CMP_R + 1
    ch = x[:, :n_chunk * CMP_STRIDE].reshape(B, n_chunk, CMP_STRIDE, NSA_G, NSA_HD)
    ch = jnp.moveaxis(ch, 3, 2).reshape(B, n_chunk, NSA_G, CMP_STRIDE * NSA_HD)
    part = jnp.einsum('bngc,rch->bngrh', ch, w1)
    pre = b1 + sum(part[:, m:m + n_cmp, :, m] for m in range(CMP_R))
    return jax.nn.gelu(pre) @ w2 + b2


def nsa_compressed_branch(qn, kc_raw, vc_raw, p, q_pos):
    kc = to_groups_k(rms_norm(nsa_compress(kc_raw, *p['ck']), p['k_norm'][0]))
    vc = to_groups_k(nsa_compress(vc_raw, *p['cv']))
    s = jnp.einsum('bghqd,bgcd->bghqc', qn, kc) * NSA_HD ** -0.5
    ends = jnp.arange(kc.shape[2]) * CMP_STRIDE + CMP_BLOCK - 1
    prob = masked_softmax(s, ends[None, :] <= q_pos[:, None])
    return jnp.einsum('bghqc,bgcd->bghqd', prob.astype(vc.dtype), vc), prob


def cmp_sel_overlap(n_cmp, n_sel):
    cs = np.arange(n_cmp) * CMP_STRIDE
    ss = np.arange(n_sel) * SEL_BLOCK
    ov = np.minimum(cs[None] + CMP_BLOCK, ss[:, None] + SEL_BLOCK) - np.maximum(cs[None], ss[:, None])
    return jnp.asarray(np.clip(ov, 0, None) / CMP_BLOCK, dtype=jnp.float32)


def nsa_select(p_cmp, q_pos, n_sel):
    imp = jnp.einsum('bgqc,sc->bgqs', p_cmp.sum(axis=2), cmp_sel_overlap(p_cmp.shape[-1], n_sel))
    j = jnp.arange(n_sel)[None, :]
    qb = (q_pos // SEL_BLOCK)[:, None]
    valid = j <= qb
    forced = (j == 0) | (j == qb) | (j == qb - 1)
    score = jnp.where(valid, jnp.where(forced, FORCE_SCORE, imp), -jnp.inf)
    _, idx = lax.top_k(score, min(SEL_TOP, n_sel))
    sel_ok = jnp.take_along_axis(jnp.broadcast_to(valid, score.shape), idx, axis=-1)
    return idx, sel_ok


def sel_blocks(x, n_sel):
    B, L = x.shape[:2]
    x = jnp.pad(x, ((0, 0), (0, n_sel * SEL_BLOCK - L), (0, 0), (0, 0)))
    return jnp.moveaxis(x.reshape(B, n_sel, SEL_BLOCK, NSA_G, NSA_HD), 3, 1)


def nsa_slc_attend(q, kb, vb, idx, sel_ok, q_pos):
    B, G = kb.shape[:2]
    bi = jnp.arange(B)[:, None, None, None]
    gi = jnp.arange(G)[None, :, None, None]
    kg = kb[bi, gi, idx]
    vg = vb[bi, gi, idx]
    s = jnp.einsum('bghqd,bgqnld->bghqnl', q, kg) * NSA_HD ** -0.5
    kpos = idx[..., None] * SEL_BLOCK + jnp.arange(SEL_BLOCK)
    mask = (kpos <= q_pos[None, None, :, None, None]) & sel_ok[..., None]
    sh = s.shape
    prob = masked_softmax(s.reshape(sh[:4] + (-1,)), mask.reshape(B, G, 1, sh[3], -1))
    return jnp.einsum('bghqnl,bgqnld->bghqd', prob.reshape(sh).astype(vg.dtype), vg)


def window_attend_banded(q, k, v):
    B, G, HPG, T, HD = q.shape
    nb = T // WIN_BLOCK
    npv = WINDOW // WIN_BLOCK
    pad = ((0, 0), (0, 0), (npv * WIN_BLOCK, 0), (0, 0))

    def band(z):
        zb = jnp.pad(z, pad).reshape(B, G, nb + npv, WIN_BLOCK, HD)
        return jnp.concatenate([zb[:, :, j:j + nb] for j in range(npv + 1)], axis=3)

    kb, vb = band(k), band(v)
    qb = q.reshape(B, G, HPG, nb, WIN_BLOCK, HD)
    s = jnp.einsum('bghiqd,bgikd->bghiqk', qb, kb) * NSA_HD ** -0.5
    blk = jnp.arange(nb)[:, None]
    q_pos = blk * WIN_BLOCK + jnp.arange(WIN_BLOCK)[None]
    k_pos = (blk - npv) * WIN_BLOCK + jnp.arange((npv + 1) * WIN_BLOCK)[None]
    diff = q_pos[:, :, None] - k_pos[:, None, :]
    mask = (diff >= 0) & (diff < WINDOW) & (k_pos[:, None, :] >= 0)
    prob = masked_softmax(s, mask)
    return jnp.einsum('bghiqk,bgikd->bghiqd', prob.astype(v.dtype), vb).reshape(B, G, HPG, T, HD)


def window_attend_cached(q, k, v, q_pos, k_pos):
    s = jnp.einsum('bghqd,blgd->bghql', q, k) * NSA_HD ** -0.5
    diff = q_pos[:, None] - k_pos[None, :]
    prob = masked_softmax(s, (diff >= 0) & (diff < WINDOW))
    return jnp.einsum('bghql,blgd->bghqd', prob.astype(v.dtype), v)


def retention_chunk(S, q, k, v):
    f32 = jnp.float32
    C = q.shape[1]
    lg = jnp.log1p(-jnp.exp2(-5.0 - jnp.arange(RET_H, dtype=f32)))
    i = jnp.arange(C, dtype=f32)
    diff = i[:, None] - i[None, :]
    causal = diff >= 0
    dmask = jnp.where(causal, jnp.exp(jnp.where(causal, diff, 0.0)[None] * lg[:, None, None]), 0.0)
    qf, kf, vf = q.astype(f32), k.astype(f32), v.astype(f32)
    s = jnp.einsum('bihd,bjhd->bhij', qf, kf) * dmask
    o = jnp.einsum('bhij,bjhe->bihe', s, vf)
    o = o + jnp.einsum('bihd,bhde->bihe', qf, S) * jnp.exp((i[:, None] + 1.0) * lg[None, :])[None, :, :, None]
    k_dec = kf * jnp.exp((C - 1.0 - i)[:, None] * lg[None, :])[None, :, :, None]
    S = S * jnp.exp(C * lg)[None, :, None, None] + jnp.einsum('bjhd,bjhe->bhde', k_dec, vf)
    return S, o


def retention_prompt(q, k, v):
    B, T = q.shape[:2]
    n = T // RET_CHUNK
    xs = tuple(jnp.moveaxis(z.reshape((B, n, RET_CHUNK) + z.shape[2:]), 1, 0) for z in (q, k, v))
    s0 = jnp.zeros((B, RET_H, RET_DK, RET_DV), jnp.float32)
    S, o = lax.scan(lambda S, c: retention_chunk(S, c[0], c[1], c[2]), s0, xs)
    return S, jnp.moveaxis(o, 0, 1).reshape(B, T, RET_H, RET_DV)


def odd_output(o_cmp, o_slc, o_win, o_ret, pr, p):
    gates = pr['gates']
    B, T = gates.shape[:2]
    gg = jnp.moveaxis(gates.reshape(B, T, NSA_G, NSA_HPG, 3), 1, 3)[..., None]
    o = gg[..., 0, :] * o_cmp + gg[..., 1, :] * o_slc + gg[..., 2, :] * o_win
    o_nsa = jnp.moveaxis(o, 3, 1).reshape(B, T, NSA_W)
    y_ret = head_group_norm(o_ret, p['gn_g'], p['gn_b'], 1e-5).astype(o_nsa.dtype) * jax.nn.silu(pr['rg'])
    return jnp.concatenate([o_nsa, y_ret], axis=-1)


def odd_mixer_prompt(proj, p):
    B, T, _ = proj.shape
    pos = jnp.arange(T)
    pr = odd_project(proj, p, pos)
    qn = pr['q_n'].reshape(B, T, NSA_W)
    qr = pr['q_r'].reshape(B, T, NSA_W)
    kc = rms_norm(nsa_compress(pr['kc'], *p['ck']), p['k_norm'][0])
    vc = nsa_compress(pr['vc'], *p['cv'])
    n_cmp = kc.shape[1]
    n_sel = -(-T // SEL_BLOCK)
    o_cmp, sel = nsa_cmp_select(qn, _tile_cmp(kc), _tile_cmp(vc), _overlap_T(n_cmp, n_sel),
                                n_cmp=n_cmp, n_sel=n_sel, q_pos0=0)
    o_slc = nsa_flash(qr, _tile_groups(pr['ks']), _tile_groups(pr['vs']), sel, _sel_expand(T))
    o_win = nsa_flash(qr, _tile_groups(pr['kw']), _tile_groups(pr['vw']))
    S, o_ret = retention_prompt_pallas(pr['rq'].reshape(B, T, -1), pr['rk'].reshape(B, T, -1),
                                       pr['rv'].reshape(B, T, -1))
    o_ret = o_ret.reshape(B, T, RET_H, RET_DV)
    gates = pr['gates']
    heads = (B, T, NSA_H, NSA_HD)
    o_nsa = (gates[..., 0:1] * o_cmp.reshape(heads) + gates[..., 1:2] * o_slc.reshape(heads)
             + gates[..., 2:3] * o_win.reshape(heads)).reshape(B, T, NSA_W)
    y_ret = head_group_norm(o_ret, p['gn_g'], p['gn_b'], 1e-5).astype(o_nsa.dtype) * jax.nn.silu(pr['rg'])
    out = jnp.concatenate([o_nsa, y_ret], axis=-1)
    kv_rows = jnp.stack([pr['kc'], pr['vc'], pr['ks'], pr['vs']], axis=2)
    win = jnp.stack([pr['kw'], pr['vw']], axis=2)[:, T - min(WINDOW, T):]
    return out, kv_rows, win, S


def odd_mixer_sample(proj, p, cache_layer, page_table, win_buf, ret_s0):
    B, T, _ = proj.shape
    assert T == DEC_SEQ == 1 and win_buf.shape[1] == WIN_BUF
    pos = PAST_LEN + jnp.arange(T)
    pr = odd_project(proj, p, pos)
    scale = NSA_HD ** -0.5
    new_rows = jnp.stack([pr['ks'], pr['vs'], pr['kw'], pr['vw']], axis=2)[:, 0].reshape(B, 4, SLOT_ROWS)
    cache_t = jnp.transpose(cache_layer, (0, 2, 3, 4, 1)).reshape(cache_layer.shape[0], KV_ROWS, PAGE_SIZE)
    win_t = jnp.transpose(win_buf, (0, 2, 3, 4, 1)).reshape(B, 2 * SLOT_ROWS, WIN_BUF)
    wk = _dec_cmp_weights(*p['ck'])
    wv = _dec_cmp_weights(*p['cv'])
    w1t, b1, w2t, b2t = (jnp.stack([a, b]) for a, b in zip(wk, wv))
    kn = jnp.tile(p['k_norm'][0], NSA_G).reshape(1, SLOT_ROWS)
    t = np.arange(PAST_LEN)
    expand = jnp.asarray(np.arange(CMP_PAD)[:, None] == (t // SEL_BLOCK)[None, :], dtype=BF16)
    h = np.arange(NSA_H)
    grp = jnp.asarray((h[:, None] // NSA_HPG) == (h[None, :] // NSA_HPG), dtype=BF16)
    o16 = dec_nsa(page_table, cache_t, win_t, _place_heads(pr['q_n'][:, 0] * scale),
                  _place_heads(pr['q_r'][:, 0] * scale), new_rows, pr['gates'][:, 0],
                  w1t, b1, w2t, b2t, kn, _overlap_T(DEC_N_CMP, DEC_N_SEL), expand, grp)
    o_nsa = _take_heads(o16)[:, None, :]
    S, o_ret = retention_chunk(ret_s0.astype(jnp.float32), pr['rq'], pr['rk'], pr['rv'])
    y_ret = head_group_norm(o_ret, p['gn_g'], p['gn_b'], 1e-5).astype(o_nsa.dtype) * jax.nn.silu(pr['rg'])
    out = jnp.concatenate([o_nsa, y_ret], axis=-1)
    rows = jnp.stack([pr['kc'], pr['vc'], pr['ks'], pr['vs']], axis=2).astype(cache_layer.dtype)
    new_col = jnp.stack([pr['kw'], pr['vw']], axis=2)[:, 0].reshape(B, 2 * SLOT_ROWS, 1).astype(win_buf.dtype)
    win_new = jnp.concatenate([win_t[:, :, T:], new_col], axis=2).reshape(B, 2, NSA_G, NSA_HD, WIN_BUF)
    return out, rows, jnp.transpose(win_new, (0, 4, 1, 2, 3)), S


def odd_mixer(proj, p, B, T, DB, cache_layer, page_table, win_buf, ret_s0):
    assert DEC_SEQ == 1 and win_buf.shape[1] == WIN_BUF
    n_p = B * T
    kv = (B, T, NSA_G, NSA_HD)
    cols = lambda rows, off, w: proj[rows, off:off + w]
    prompt = slice(0, n_p)
    dec = slice(n_p, n_p + DB)
    qn, qr, ks, kw, ks4, vs4, kw4, vw4, gates, rq, rk, kv_t, win_t_p = odd_pre(
        proj, jnp.tile(jnp.arange(T), B), p, 0, n_p, False, T)
    seq = lambda z: z.reshape(B, T, -1)
    kc_raw = cols(prompt, OFF_KC, KV_W).reshape(kv)
    vc_raw = cols(prompt, OFF_VC, KV_W).reshape(kv)
    kc = rms_norm(nsa_compress(kc_raw, *p['ck']), p['k_norm'][0])
    vc = nsa_compress(vc_raw, *p['cv'])
    n_cmp = kc.shape[1]
    n_sel = -(-T // SEL_BLOCK)
    o_cmp, sel = nsa_cmp_select(seq(qn), _tile_cmp(kc), _tile_cmp(vc), _overlap_T(n_cmp, n_sel),
                                n_cmp=n_cmp, n_sel=n_sel, q_pos0=0)
    o_slc = nsa_flash(seq(qr), seq(ks4), seq(vs4), sel, _sel_expand(T))
    o_win = nsa_flash(seq(qr), seq(kw4), seq(vw4))
    ret_p, o_ret = retention_prompt_pallas(rq, rk, proj, B, T, v_col0=OFF_RV)
    flat = lambda z: z.reshape(n_p, -1)
    cat = odd_post(flat(o_cmp), flat(o_slc), flat(o_win), gates, o_ret, proj, p, n_p + DB, 0)
    n_win = min(WINDOW, T)
    kv_rows_p = jnp.transpose(kv_t.reshape(B, KV_SLOTS, NSA_G, NSA_HD, T), (0, 4, 1, 2, 3))
    win_p = jnp.transpose(win_t_p[:, :, T - n_win:].reshape(B, 2, NSA_G, NSA_HD, n_win), (0, 4, 1, 2, 3))
    qn, qr, ks, kw, _, _, _, _, gates, rq, rk, kv_t, _ = odd_pre(proj, jnp.asarray(PAST_LEN), p, n_p, DB, True, DB)
    scale = NSA_HD ** -0.5
    heads = lambda z: z.reshape(DB, NSA_H, NSA_HD)
    vs, vw = cols(dec, OFF_VS, KV_W), cols(dec, OFF_VW, KV_W)
    new_rows = jnp.stack([ks, vs, kw, vw], axis=1)
    cache_t = jnp.transpose(cache_layer, (0, 2, 3, 4, 1)).reshape(cache_layer.shape[0], KV_ROWS, PAGE_SIZE)
    win_t = jnp.transpose(win_buf, (0, 2, 3, 4, 1)).reshape(DB, 2 * SLOT_ROWS, WIN_BUF)
    w1t, b1, w2t, b2t = (jnp.stack([a, b]) for a, b in zip(_dec_cmp_weights(*p['ck']), _dec_cmp_weights(*p['cv'])))
    kn = jnp.tile(p['k_norm'][0], NSA_G).reshape(1, SLOT_ROWS)
    t = np.arange(PAST_LEN)
    expand = jnp.asarray(np.arange(CMP_PAD)[:, None] == (t // SEL_BLOCK)[None, :], dtype=BF16)
    h = np.arange(NSA_H)
    grp = jnp.asarray((h[:, None] // NSA_HPG) == (h[None, :] // NSA_HPG), dtype=BF16)
    o16 = dec_nsa(page_table, cache_t, win_t, _place_heads(heads(qn) * scale), _place_heads(heads(qr) * scale),
                  new_rows, gates[:, :3 * NSA_H].reshape(DB, NSA_H, 3),
                  w1t, b1, w2t, b2t, kn, _overlap_T(DEC_N_CMP, DEC_N_SEL), expand, grp)
    o_nsa = _take_heads(o16)
    ret_s, o_ret = retention_chunk(ret_s0.astype(F32), rq.reshape(DB, 1, RET_H, RET_DK),
                                   rk.reshape(DB, 1, RET_H, RET_DK), cols(dec, OFF_RV, RET_W).reshape(DB, 1, RET_H, RET_DV))
    cat = odd_post(o_nsa, o_nsa, o_nsa, gates, o_ret.reshape(DB, RET_W), proj, p, n_p + DB, n_p, prior=cat, gated=True)
    rows_s = jnp.transpose(kv_t.reshape(DEC_SEQ, KV_SLOTS, NSA_G, NSA_HD, DB), (4, 0, 1, 2, 3)).astype(cache_layer.dtype)
    new_row = jnp.concatenate([kw, vw], axis=1)[:, None, :].astype(win_buf.dtype)
    win_new = win_shift(win_t, new_row).reshape(DB, 2, NSA_G, NSA_HD, WIN_BUF)
    win_s = jnp.transpose(win_new, (0, 4, 1, 2, 3))
    return cat, (kv_rows_p, win_p, ret_p), (rows_s, win_s, ret_s)


def _stack(xs, dt):
    return jnp.stack(xs).astype(dt)


def kernel(x_prompt, x_sample, state_lru_h, state_lru_conv, state_rwkv_shift, state_rwkv_wkv,
           cache_nsa_kv, cache_nsa_win, state_ret, page_table,
           norm_ffn1, ffn1_w_in, ffn1_w_out, norm_mix, norm_ffn2, ffn2_w_in, ffn2_w_out,
           ab_w_in, lru_conv_w, lru_conv_b, lru_wa, lru_ba, lru_wx, lru_bx, lru_lambda,
           rwkv_mu, rwkv_w0, rwkv_w2, rwkv_a0, rwkv_a2, rwkv_g2, rwkv_k_k, rwkv_k_a, rwkv_r_k,
           rwkv_ln_g, rwkv_ln_b, ab_w_out,
           cd_w_in, nsa_q_norm, nsa_k_norm, cmp_k_w1, cmp_k_b1, cmp_k_w2, cmp_k_b2,
           cmp_v_w1, cmp_v_b1, cmp_v_w2, cmp_v_b2, ret_gn_g, ret_gn_b, cd_w_out):
    dt = x_prompt.dtype
    B = x_prompt.shape[0]
    DB = x_sample.shape[0]
    y = jnp.concatenate([x_prompt.reshape(N_PROMPT, D_MODEL), x_sample.reshape(DB * DEC_SEQ, D_MODEL)], axis=0)
    lru_h_p, lru_h_s, lru_c_p, lru_c_s, sh_p, sh_s, wkv_p, wkv_s = [], [], [], [], [], [], [], []
    kv_p, kv_s, win_p, win_s, ret_p, ret_s = [], [], [], [], [], []
    for layer in range(DEPTH):
        li = layer // 2
        y = ffn_block(y, norm_ffn1[layer], *_prep_ffn_weights(ffn1_w_in, ffn1_w_out, layer))
        if layer % 2 == 0:
            p = {'conv_w': lru_conv_w[li], 'conv_b': lru_conv_b[li],
                 'wa': lru_wa[li], 'ba': lru_ba[li], 'wx': lru_wx[li], 'bx': lru_bx[li], 'lam': lru_lambda[li],
                 'mu': rwkv_mu[li], 'w0': rwkv_w0[li], 'w2': rwkv_w2[li], 'a0': rwkv_a0[li], 'a2': rwkv_a2[li],
                 'g2': rwkv_g2[li], 'k_k': rwkv_k_k[li], 'k_a': rwkv_k_a[li], 'r_k': rwkv_r_k[li],
                 'ln_g': rwkv_ln_g[li], 'ln_b': rwkv_ln_b[li]}
            proj = norm_matmul(y, norm_mix[layer], _prep_cols(ab_w_in[li]))
            cat, (a0, a1, a2, a3), (b0, b1, b2, b3) = even_mixer(
                proj, p, B, SEQ, DB, state_lru_h[li], state_lru_conv[li], state_rwkv_shift[li], state_rwkv_wkv[li])
            lru_h_p.append(a0); lru_c_p.append(a1); sh_p.append(a2); wkv_p.append(a3)
            lru_h_s.append(b0); lru_c_s.append(b1); sh_s.append(b2); wkv_s.append(b3)
            w_out = ab_w_out[li]
        else:
            p = {'q_norm': nsa_q_norm[li], 'k_norm': nsa_k_norm[li],
                 'ck': (cmp_k_w1[li], cmp_k_b1[li], cmp_k_w2[li], cmp_k_b2[li]),
                 'cv': (cmp_v_w1[li], cmp_v_b1[li], cmp_v_w2[li], cmp_v_b2[li]),
                 'gn_g': ret_gn_g[li], 'gn_b': ret_gn_b[li]}
            proj = norm_matmul(y, norm_mix[layer], _odd_weight_cols(cd_w_in[li]), tn=WIDE_COL_TILE)
            cat, (a0, a1, a2), (b0, b1, b2) = odd_mixer(
                proj, p, B, SEQ, DB, cache_nsa_kv[li], page_table, cache_nsa_win[li], state_ret[li])
            kv_p.append(a0); win_p.append(a1); ret_p.append(a2)
            kv_s.append(b0); win_s.append(b1); ret_s.append(b2)
            w_out = cd_w_out[li]
        y = matmul_residual(cat, w_out.astype(BF16), y, tn=WIDE_COL_TILE)
        y = ffn_block(y, norm_ffn2[layer], *_prep_ffn_weights(ffn2_w_in, ffn2_w_out, layer))
    yp = y[:N_PROMPT].reshape(B, SEQ, D_MODEL)
    ys = y[N_PROMPT:].reshape(DB, DEC_SEQ, D_MODEL)
    return (yp, ys,
            _stack(lru_h_p, dt), _stack(lru_h_s, dt), _stack(lru_c_p, dt), _stack(lru_c_s, dt),
            _stack(sh_p, dt), _stack(sh_s, dt), _stack(wkv_p, dt), _stack(wkv_s, dt),
            _stack(kv_p, dt), _stack(kv_s, dt), _stack(win_p, dt), _stack(win_s, dt),
            _stack(ret_p, dt), _stack(ret_s, dt))
```

```python
import functools

import jax
import jax.numpy as jnp
import numpy as np
from jax import lax
from jax.experimental import pallas as pl
from jax.experimental.pallas import tpu as pltpu

D_MODEL = 2048
BATCH = 4
SEQ = 2048
DEPTH = 2
DEC_BATCH = 128
DEC_SEQ = 1
PAST_LEN = 2048
PAGE_SIZE = 128
D_FF = 5504
LRU_W = D_MODEL // 2
LRU_BLOCKS = 16
LRU_BS = LRU_W // LRU_BLOCKS
CONV_W = 4
LRU_C = 8.0
RWKV_W = D_MODEL // 2
RWKV_HD = 64
RWKV_H = RWKV_W // RWKV_HD
W_LORA = 64
A_LORA = 64
G_LORA = 160
SHIFT_W = 3 * RWKV_W + W_LORA + A_LORA + G_LORA
AB_COLS = 2 * LRU_W + SHIFT_W
NSA_H = 16
NSA_G = 4
NSA_HPG = NSA_H // NSA_G
NSA_HD = 64
NSA_W = NSA_H * NSA_HD
ROPE_DIMS = NSA_HD // 4
ROPE_THETA = 500000.0
CMP_BLOCK = 32
CMP_STRIDE = 16
CMP_R = CMP_BLOCK // CMP_STRIDE
CMP_HID = 256
SEL_BLOCK = 64
SEL_TOP = 16
SEL_Q_BLOCK = 64
WINDOW = 512
WIN_BLOCK = 128
FORCE_SCORE = 1e4
KV_SLOTS = 4
RET_H = 8
RET_DK = 64
RET_DV = 128
RET_W = RET_H * RET_DV
RET_CHUNK = 128
RET_THETA = 10000.0
CD_COLS = NSA_W + 6 * NSA_G * NSA_HD + 3 * NSA_H + 2 * RET_H * RET_DK + 2 * RET_W

N_TOK = BATCH * SEQ + DEC_BATCH * DEC_SEQ
N_PROMPT = BATCH * SEQ

LANE = 128
VMEM_LIMIT_BYTES = 56 * 1024 * 1024
ROW_TILE = 640
FF_TILE = 512
D_FF_PAD = 5632
COL_TILE = 512
WIDE_COL_TILE = 2048

BF16 = jnp.bfloat16
F32 = jnp.float32


def _round_up(n, m):
    return -(-n // m) * m


def _rms_rows(x, g):
    ms = jnp.mean(x * x, axis=-1, keepdims=True)
    return x * lax.rsqrt(ms + 1e-6) * g


def _ffn_kernel(x_ref, g_ref, wg_ref, wu_ref, wo_ref, o_ref, xn_ref, acc_ref):
    k = pl.program_id(1)

    @pl.when(k == 0)
    def _():
        xn_ref[...] = _rms_rows(x_ref[...], g_ref[...]).astype(BF16)
        acc_ref[...] = jnp.zeros_like(acc_ref)

    xn = xn_ref[...]
    gate = jnp.dot(xn, wg_ref[...], preferred_element_type=F32)
    up = jnp.dot(xn, wu_ref[...], preferred_element_type=F32)
    act = gate * jax.nn.sigmoid(gate) * up
    acc_ref[...] += jnp.dot(act.astype(BF16), wo_ref[...], preferred_element_type=F32)

    @pl.when(k == pl.num_programs(1) - 1)
    def _():
        o_ref[...] = x_ref[...] + 0.5 * acc_ref[...]


def ffn_block(x, g, wg, wu, wo):
    m, d = x.shape
    return pl.pallas_call(
        _ffn_kernel,
        grid=(m // ROW_TILE, D_FF_PAD // FF_TILE),
        in_specs=[
            pl.BlockSpec((ROW_TILE, d), lambda i, k: (i, 0)),
            pl.BlockSpec((1, d), lambda i, k: (0, 0)),
            pl.BlockSpec((d, FF_TILE), lambda i, k: (0, k)),
            pl.BlockSpec((d, FF_TILE), lambda i, k: (0, k)),
            pl.BlockSpec((FF_TILE, d), lambda i, k: (k, 0)),
        ],
        out_specs=pl.BlockSpec((ROW_TILE, d), lambda i, k: (i, 0)),
        out_shape=jax.ShapeDtypeStruct((m, d), F32),
        scratch_shapes=[pltpu.VMEM((ROW_TILE, d), BF16), pltpu.VMEM((ROW_TILE, d), F32)],
        compiler_params=pltpu.CompilerParams(
            dimension_semantics=("parallel", "arbitrary"), vmem_limit_bytes=VMEM_LIMIT_BYTES),
        name="ffn_block",
    )(x, g.reshape(1, d), wg, wu, wo)


def _norm_matmul_kernel(x_ref, g_ref, w_ref, o_ref, xn_ref):
    @pl.when(pl.program_id(1) == 0)
    def _():
        xn_ref[...] = _rms_rows(x_ref[...], g_ref[...]).astype(BF16)

    o_ref[...] = jnp.dot(xn_ref[...], w_ref[...], preferred_element_type=F32)


def norm_matmul(x, g, w, tn=COL_TILE):
    m, k = x.shape
    n = w.shape[1]
    return pl.pallas_call(
        _norm_matmul_kernel,
        grid=(m // ROW_TILE, n // tn),
        in_specs=[
            pl.BlockSpec((ROW_TILE, k), lambda i, j: (i, 0)),
            pl.BlockSpec((1, k), lambda i, j: (0, 0)),
            pl.BlockSpec((k, tn), lambda i, j: (0, j)),
        ],
        out_specs=pl.BlockSpec((ROW_TILE, tn), lambda i, j: (i, j)),
        out_shape=jax.ShapeDtypeStruct((m, n), F32),
        scratch_shapes=[pltpu.VMEM((ROW_TILE, k), BF16)],
        compiler_params=pltpu.CompilerParams(
            dimension_semantics=("parallel", "arbitrary"), vmem_limit_bytes=VMEM_LIMIT_BYTES),
        name="norm_matmul",
    )(x, g.reshape(1, k), w)


def _matmul_residual_kernel(a_ref, w_ref, r_ref, o_ref):
    o_ref[...] = r_ref[...] + jnp.dot(a_ref[...].astype(BF16), w_ref[...], preferred_element_type=F32)


def matmul_residual(a, w, res, tn=COL_TILE):
    m, k = a.shape
    n = w.shape[1]
    return pl.pallas_call(
        _matmul_residual_kernel,
        grid=(m // ROW_TILE, n // tn),
        in_specs=[
            pl.BlockSpec((ROW_TILE, k), lambda i, j: (i, 0)),
            pl.BlockSpec((k, tn), lambda i, j: (0, j)),
            pl.BlockSpec((ROW_TILE, tn), lambda i, j: (i, j)),
        ],
        out_specs=pl.BlockSpec((ROW_TILE, tn), lambda i, j: (i, j)),
        out_shape=jax.ShapeDtypeStruct((m, n), F32),
        compiler_params=pltpu.CompilerParams(
            dimension_semantics=("parallel", "arbitrary"), vmem_limit_bytes=VMEM_LIMIT_BYTES),
        name="matmul_residual",
    )(a, w, res)


WCAST_ROWS = 256
WCAST_COLS = 512


def _cast_w_in_kernel(w_ref, wg_ref, wu_ref):
    pad = jnp.zeros((w_ref.shape[0], D_FF_PAD - D_FF), BF16)
    wg_ref[:, :D_FF] = w_ref[:, :D_FF].astype(BF16)
    wg_ref[:, D_FF:] = pad
    wu_ref[:, :D_FF] = w_ref[:, D_FF:].astype(BF16)
    wu_ref[:, D_FF:] = pad


def _cast_w_out_kernel(w_ref, wo_ref):
    wo_ref[:D_FF, :] = w_ref[...].astype(BF16)
    wo_ref[D_FF:, :] = jnp.zeros((D_FF_PAD - D_FF, w_ref.shape[1]), BF16)


def _prep_ffn_weights(w_in, w_out, layer):
    d = w_in.shape[1]
    wg, wu = pl.pallas_call(
        _cast_w_in_kernel,
        grid=(d // WCAST_ROWS,),
        in_specs=[pl.BlockSpec((None, WCAST_ROWS, 2 * D_FF), lambda i: (layer, i, 0))],
        out_specs=[pl.BlockSpec((WCAST_ROWS, D_FF_PAD), lambda i: (i, 0))] * 2,
        out_shape=[jax.ShapeDtypeStruct((d, D_FF_PAD), BF16)] * 2,
        compiler_params=pltpu.CompilerParams(dimension_semantics=("parallel",), vmem_limit_bytes=VMEM_LIMIT_BYTES),
        name="cast_w_in",
    )(w_in)
    wo = pl.pallas_call(
        _cast_w_out_kernel,
        grid=(d // WCAST_COLS,),
        in_specs=[pl.BlockSpec((None, D_FF, WCAST_COLS), lambda j: (layer, 0, j))],
        out_specs=pl.BlockSpec((D_FF_PAD, WCAST_COLS), lambda j: (0, j)),
        out_shape=jax.ShapeDtypeStruct((D_FF_PAD, d), BF16),
        compiler_params=pltpu.CompilerParams(dimension_semantics=("parallel",), vmem_limit_bytes=VMEM_LIMIT_BYTES),
        name="cast_w_out",
    )(w_out)
    return wg, wu, wo


def _prep_cols(w, tile):
    n = w.shape[1]
    return jnp.pad(w, ((0, 0), (0, _round_up(n, tile) - n))).astype(BF16)


SCAN_TILE = 256


def _lru_scan_kernel(a_ref, b_ref, h0_ref, o_ref, carry_ref):
    @pl.when(pl.program_id(1) == 0)
    def _():
        carry_ref[...] = h0_ref[...]

    a = a_ref[...]
    b = b_ref[...]
    rows = lax.broadcasted_iota(jnp.int32, a.shape, 0)
    k = 1
    while k < a.shape[0]:
        keep = rows >= k
        b = jnp.where(keep, a * pltpu.roll(b, k, 0) + b, b)
        a = jnp.where(keep, a * pltpu.roll(a, k, 0), a)
        k *= 2
    h = a * carry_ref[...] + b
    o_ref[...] = h
    carry_ref[...] = h[a.shape[0] - 1:, :]


def lru_scan(a, b, h0):
    B, T, W = a.shape
    tt = min(SCAN_TILE, T)
    return pl.pallas_call(
        _lru_scan_kernel,
        grid=(B, T // tt),
        in_specs=[
            pl.BlockSpec((None, tt, W), lambda i, t: (i, t, 0)),
            pl.BlockSpec((None, tt, W), lambda i, t: (i, t, 0)),
            pl.BlockSpec((None, 1, W), lambda i, t: (i, 0, 0)),
        ],
        out_specs=pl.BlockSpec((None, tt, W), lambda i, t: (i, t, 0)),
        out_shape=jax.ShapeDtypeStruct((B, T, W), F32),
        scratch_shapes=[pltpu.VMEM((1, W), F32)],
        compiler_params=pltpu.CompilerParams(
            dimension_semantics=("parallel", "arbitrary"), vmem_limit_bytes=VMEM_LIMIT_BYTES),
        name="lru_scan",
    )(a, b, h0.reshape(B, 1, W))


GROUP_W = NSA_HPG * NSA_HD
ATT_Q_TILE = 128
ATT_K_TILE = 256
CMP_PAD = 128
NEG_BIG = -1e30


def _stack_heads(q):
    head = lax.broadcasted_iota(jnp.int32, q.shape, 1) // NSA_HD
    return jnp.concatenate([jnp.where(head == h, q, 0.0) for h in range(NSA_HPG)], axis=0)


def _unstack_heads(o, tq):
    head = lax.broadcasted_iota(jnp.int32, (tq, GROUP_W), 1) // NSA_HD
    out = jnp.zeros((tq, GROUP_W), F32)
    for h in range(NSA_HPG):
        out = out + jnp.where(head == h, o[h * tq:(h + 1) * tq], 0.0)
    return out


def _cmp_select_kernel(q_ref, k_ref, v_ref, ov_ref, o_ref, sel_ref, *, n_cmp, n_sel, q_pos0):
    tq = q_ref.shape[0]
    i = pl.program_id(2)
    qs = _stack_heads(q_ref[...] * (NSA_HD ** -0.5)).astype(BF16)
    s = lax.dot_general(qs, k_ref[...], (((1,), (1,)), ((), ())), preferred_element_type=F32)
    q_pos = q_pos0 + i * tq + lax.broadcasted_iota(jnp.int32, (tq, CMP_PAD), 0)
    c = lax.broadcasted_iota(jnp.int32, (tq, CMP_PAD), 1)
    mask1 = (c < n_cmp) & (c * CMP_STRIDE + (CMP_BLOCK - 1) <= q_pos)
    mask = jnp.concatenate([mask1] * NSA_HPG, axis=0)
    s = jnp.where(mask, s, NEG_BIG)
    m = jnp.max(s, axis=-1, keepdims=True)
    e = jnp.where(mask, jnp.exp(s - m), 0.0)
    den = jnp.sum(e, axis=-1, keepdims=True)
    prob = e / jnp.where(den > 0, den, 1.0)
    o = jnp.dot(prob.astype(BF16), v_ref[...], preferred_element_type=F32)
    o_ref[...] = _unstack_heads(o, tq)
    psum = prob[0:tq]
    for h in range(1, NSA_HPG):
        psum = psum + prob[h * tq:(h + 1) * tq]
    imp = jnp.dot(psum.astype(BF16), ov_ref[...], preferred_element_type=F32)
    qb = q_pos // SEL_BLOCK
    valid = (c <= qb) & (c < n_sel)
    forced = (c == 0) | (c == qb) | (c == qb - 1)
    score = jnp.where(valid, jnp.where(forced, FORCE_SCORE, imp), -jnp.inf)
    k_top = min(SEL_TOP, n_sel)
    few_blocks = (q_pos0 + (i + 1) * tq - 1) // SEL_BLOCK < k_top

    @pl.when(few_blocks)
    def _():
        sel_ref[...] = jnp.where(valid, 1.0, 0.0)

    @pl.when(jnp.logical_not(few_blocks))
    def _():
        rank = jnp.zeros((tq, CMP_PAD), F32)
        for jp in range(n_sel):
            col = score[:, jp:jp + 1]
            beats = (col > score) | ((col == score) & (c > jp))
            rank = rank + jnp.where(beats, 1.0, 0.0)
        sel_ref[...] = jnp.where((rank < k_top) & (c < n_sel), 1.0, 0.0)


def nsa_cmp_select(qn, kc4, vc4, ovT, *, n_cmp, n_sel, q_pos0):
    B, T, _ = qn.shape
    tq = min(ATT_Q_TILE, T)
    return pl.pallas_call(
        functools.partial(_cmp_select_kernel, n_cmp=n_cmp, n_sel=n_sel, q_pos0=q_pos0),
        grid=(B, NSA_G, T // tq),
        in_specs=[
            pl.BlockSpec((None, tq, GROUP_W), lambda b, g, i: (b, i, g)),
            pl.BlockSpec((None, None, CMP_PAD, GROUP_W), lambda b, g, i: (b, g, 0, 0)),
            pl.BlockSpec((None, None, CMP_PAD, GROUP_W), lambda b, g, i: (b, g, 0, 0)),
            pl.BlockSpec((CMP_PAD, CMP_PAD), lambda b, g, i: (0, 0)),
        ],
        out_specs=[
            pl.BlockSpec((None, tq, GROUP_W), lambda b, g, i: (b, i, g)),
            pl.BlockSpec((None, None, tq, CMP_PAD), lambda b, g, i: (b, g, i, 0)),
        ],
        out_shape=[jax.ShapeDtypeStruct((B, T, NSA_W), F32),
                   jax.ShapeDtypeStruct((B, NSA_G, T, CMP_PAD), F32)],
        compiler_params=pltpu.CompilerParams(
            dimension_semantics=("parallel", "parallel", "parallel"), vmem_limit_bytes=VMEM_LIMIT_BYTES),
        name="nsa_cmp_select",
    )(qn, kc4, vc4, ovT)


def _flash_kernel(*refs, selected):
    if selected:
        q_ref, k_ref, v_ref, sel_ref, exp_ref, o_ref, m_ref, l_ref, acc_ref, s_a, s_b = refs
    else:
        q_ref, k_ref, v_ref, o_ref, m_ref, l_ref, acc_ref, s_a, s_b = refs
    tq = q_ref.shape[0]
    tk = ATT_K_TILE
    n_tiles = k_ref.shape[0] // tk
    i = pl.program_id(2)
    q = q_ref[...] * (NSA_HD ** -0.5)
    head = lax.broadcasted_iota(jnp.int32, q.shape, 1) // NSA_HD
    q4 = _stack_heads(q).astype(BF16)
    m_ref[...] = jnp.full(m_ref.shape, NEG_BIG, F32)
    l_ref[...] = jnp.zeros(l_ref.shape, F32)
    acc_ref[...] = jnp.zeros(acc_ref.shape, F32)
    q_pos = i * tq + lax.broadcasted_iota(jnp.int32, (tq, tk), 0)
    col = lax.broadcasted_iota(jnp.int32, (tq, tk), 1)
    if selected:
        sel = sel_ref[...].astype(BF16)
        lo = 0
    else:
        lo = jnp.maximum(i * tq - (WINDOW - 1), 0) // tk
    hi = (i * tq + tq - 1) // tk + 1

    def tile_start(j):
        return pl.multiple_of(jnp.minimum(j, n_tiles - 1) * tk, tk)

    def scores(j, s_ref):
        s_ref[...] = lax.dot_general(q4, k_ref[pl.ds(tile_start(j), tk), :], (((1,), (1,)), ((), ())),
                                     preferred_element_type=F32)

    def consume(j, s_ref):
        v = v_ref[pl.ds(tile_start(j), tk), :]
        k_pos = j * tk + col
        mask = k_pos <= q_pos
        if selected:
            mask = mask & (jnp.dot(sel, exp_ref[jnp.minimum(j, n_tiles - 1)], preferred_element_type=F32) > 0.5)
        else:
            mask = mask & (q_pos - k_pos < WINDOW)
        if not selected:
            bias = jnp.where(mask, 0.0, 2.0 * NEG_BIG)
        for h in range(NSA_HPG):
            m_old = m_ref[h]
            if selected:
                s = jnp.where(mask, s_ref[h * tq:(h + 1) * tq, :], NEG_BIG)
                m_new = jnp.maximum(m_old, jnp.max(s, axis=-1, keepdims=True))
                p = jnp.where(mask, jnp.exp(s - pltpu.repeat(m_new, tk // LANE, axis=1)), 0.0)
            else:
                s = s_ref[h * tq:(h + 1) * tq, :] + bias
                m_new = jnp.maximum(m_old, jnp.max(s, axis=-1, keepdims=True))
                p = jnp.exp(s - pltpu.repeat(m_new, tk // LANE, axis=1))
            alpha = jnp.exp(m_old - m_new)
            l_ref[h] = alpha * l_ref[h] + jnp.sum(p, axis=-1, keepdims=True)
            acc_ref[h] = (pltpu.repeat(alpha, GROUP_W // LANE, axis=1) * acc_ref[h]
                          + jnp.dot(p.astype(BF16), v, preferred_element_type=F32))
            m_ref[h] = m_new

    scores(lo, s_a)

    def body(t, carry):
        j = lo + 2 * t
        scores(j + 1, s_b)
        consume(j, s_a)
        scores(j + 2, s_a)
        consume(j + 1, s_b)
        return carry

    lax.fori_loop(0, (hi - lo + 1) // 2, body, 0)
    out = jnp.zeros((tq, GROUP_W), F32)
    for h in range(NSA_HPG):
        den = pltpu.repeat(l_ref[h], GROUP_W // LANE, axis=1)
        out = out + jnp.where(head == h, acc_ref[h] / jnp.where(den > 0, den, 1.0), 0.0)
    o_ref[...] = out


def nsa_flash(qr, k4, v4, sel=None, expand=None):
    B, T, _ = qr.shape
    tq = ATT_Q_TILE
    selected = sel is not None
    in_specs = [
        pl.BlockSpec((None, tq, GROUP_W), lambda b, g, i: (b, i, g)),
        pl.BlockSpec((None, T, GROUP_W), lambda b, g, i: (b, 0, g)),
        pl.BlockSpec((None, T, GROUP_W), lambda b, g, i: (b, 0, g)),
    ]
    args = [qr, k4, v4]
    if selected:
        in_specs += [
            pl.BlockSpec((None, None, tq, CMP_PAD), lambda b, g, i: (b, g, i, 0)),
            pl.BlockSpec(expand.shape, lambda b, g, i: (0, 0, 0)),
        ]
        args += [sel, expand]
    return pl.pallas_call(
        functools.partial(_flash_kernel, selected=selected),
        grid=(B, NSA_G, T // tq),
        in_specs=in_specs,
        out_specs=pl.BlockSpec((None, tq, GROUP_W), lambda b, g, i: (b, i, g)),
        out_shape=jax.ShapeDtypeStruct((B, T, NSA_W), F32),
        scratch_shapes=[pltpu.VMEM((NSA_HPG, tq, LANE), F32), pltpu.VMEM((NSA_HPG, tq, LANE), F32),
                        pltpu.VMEM((NSA_HPG, tq, GROUP_W), F32),
                        pltpu.VMEM((NSA_HPG * tq, ATT_K_TILE), F32), pltpu.VMEM((NSA_HPG * tq, ATT_K_TILE), F32)],
        compiler_params=pltpu.CompilerParams(
            dimension_semantics=("parallel", "parallel", "parallel"), vmem_limit_bytes=VMEM_LIMIT_BYTES),
        name="nsa_flash_sel" if selected else "nsa_flash_win",
    )(*args)


def _tile_groups(x):
    B, T = x.shape[:2]
    return jnp.broadcast_to(x[:, :, :, None, :], (B, T, NSA_G, NSA_HPG, NSA_HD)).reshape(B, T, NSA_W).astype(BF16)


def _tile_cmp(x):
    B, n = x.shape[:2]
    x = jnp.pad(jnp.moveaxis(x, 1, 2), ((0, 0), (0, 0), (0, CMP_PAD - n), (0, 0)))
    return jnp.tile(x, (1, 1, 1, NSA_HPG)).astype(BF16)


def _overlap_T(n_cmp, n_sel):
    ov = np.zeros((CMP_PAD, CMP_PAD), np.float32)
    cs = np.arange(n_cmp) * CMP_STRIDE
    ss = np.arange(n_sel) * SEL_BLOCK
    o = np.minimum(cs[None] + CMP_BLOCK, ss[:, None] + SEL_BLOCK) - np.maximum(cs[None], ss[:, None])
    ov[:n_cmp, :n_sel] = (np.clip(o, 0, None) / CMP_BLOCK).T
    return jnp.asarray(ov, dtype=BF16)


def _sel_expand(T):
    t = np.arange(T)
    e = (np.arange(CMP_PAD)[:, None] == (t // SEL_BLOCK)[None, :]).astype(np.float32)
    return jnp.asarray(e.reshape(CMP_PAD, T // ATT_K_TILE, ATT_K_TILE).transpose(1, 0, 2), dtype=BF16)


KV_ROWS = KV_SLOTS * NSA_G * NSA_HD
SLOT_ROWS = NSA_G * NSA_HD
N_PAGES = PAST_LEN // PAGE_SIZE
DEC_N_CHUNK = (PAST_LEN + DEC_SEQ) // CMP_STRIDE
DEC_N_CMP = DEC_N_CHUNK - CMP_R + 1
DEC_N_SEL = -(-(PAST_LEN + DEC_SEQ) // SEL_BLOCK)
WIN_BUF = min(WINDOW, PAST_LEN)


def _softmax_rows(s, mask, s_new=None):
    s = jnp.where(mask, s, NEG_BIG)
    m = jnp.max(s, axis=-1, keepdims=True)
    if s_new is not None:
        m = jnp.maximum(m, s_new)
    e = jnp.where(mask, jnp.exp(s - m), 0.0)
    den = jnp.sum(e, axis=-1, keepdims=True)
    if s_new is None:
        return e, den
    e_new = jnp.exp(s_new - m)
    return e, e_new, den + e_new


def _dec_nsa_kernel(pt_ref, *refs):
    pages = refs[:N_PAGES]
    (win_ref, qn_ref, qr_ref, new_ref, gate_ref, w1_ref, b1_ref, w2_ref, b2_ref, kn_ref,
     ov_ref, exp_ref, grp_ref, perm_ref, o_ref, xt_ref, acc_ref) = refs[N_PAGES:]
    del pt_ref
    f32 = F32
    half = 2 * NSA_HD
    n_chunk = DEC_N_CHUNK

    perm = perm_ref[...]
    per_page = PAGE_SIZE // CMP_STRIDE
    for p in range(N_PAGES):
        for sg in range(4):
            tile = pages[p][sg * half:(sg + 1) * half, :].astype(BF16)
            xt = lax.dot_general(perm, tile, (((1,), (1,)), ((), ())), preferred_element_type=f32)
            for r in range(CMP_STRIDE):
                xt_ref[sg, r, p * per_page:(p + 1) * per_page, :] = xt[r * per_page:(r + 1) * per_page, :]

    lane_lo = lax.broadcasted_iota(jnp.int32, (n_chunk, half), 1) < NSA_HD
    lane_grp = lax.broadcasted_iota(jnp.int32, (n_chunk, SLOT_ROWS), 1) // NSA_HD
    cmp_rows = []
    for slot in range(2):
        for gp in range(2):
            los, his = [], []
            for rp in range(CMP_STRIDE // 2):
                x0, x1 = (xt_ref[slot * 2 + gp, 2 * rp + j] for j in range(2))
                los.append(jnp.where(lane_lo, x0, pltpu.roll(x1, NSA_HD, 1)))
                his.append(jnp.where(lane_lo, pltpu.roll(x0, NSA_HD, 1), x1))
            lhs = jnp.concatenate([jnp.concatenate(los, axis=1), jnp.concatenate(his, axis=1)], axis=0).astype(BF16)
            acc_ref[pl.ds(gp * 2 * n_chunk, 2 * n_chunk), :] = jnp.dot(lhs, w1_ref[slot], preferred_element_type=f32)
        acc = acc_ref[...]
        pre = b1_ref[slot] + acc[:, :CMP_HID] + pltpu.roll(acc[:, CMP_HID:], NSA_G * n_chunk - 1, 0)
        out = jnp.dot(jax.nn.gelu(pre).astype(BF16), w2_ref[slot], preferred_element_type=f32) + b2_ref[slot]
        if slot == 0:
            out = _rms_rows(out, kn_ref[...])
        sel_rows = jnp.zeros((n_chunk, SLOT_ROWS), f32)
        for g in range(NSA_G):
            sel_rows = sel_rows + jnp.where(lane_grp == g, out[g * n_chunk:(g + 1) * n_chunk], 0.0)
        cmp_rows.append(sel_rows.astype(BF16))
    kc, vc = cmp_rows

    qn = qn_ref[...].astype(BF16)
    qr = qr_ref[...].astype(BF16)
    nt = (((1,), (1,)), ((), ()))
    c = lax.broadcasted_iota(jnp.int32, (NSA_H, CMP_PAD), 1)
    s = lax.dot_general(qn, kc, nt, preferred_element_type=f32)
    e, den = _softmax_rows(s, c < DEC_N_CMP)
    prob = e / jnp.where(den > 0, den, 1.0)
    o_cmp = jnp.dot(prob.astype(BF16), vc, preferred_element_type=f32)
    p_hi, p_mid = _split_bf16(prob)
    p_lo = (prob - p_hi.astype(f32) - p_mid.astype(f32)).astype(BF16)
    grp = grp_ref[...]
    psum = (jnp.dot(grp, p_hi, preferred_element_type=f32) + jnp.dot(grp, p_mid, preferred_element_type=f32)
            + jnp.dot(grp, p_lo, preferred_element_type=f32))
    imp = jnp.dot(psum.astype(BF16), ov_ref[...], preferred_element_type=f32)
    qb = (PAST_LEN + DEC_SEQ - 1) // SEL_BLOCK
    valid = c <= qb
    forced = (c == 0) | (c == qb) | (c == qb - 1)
    score = jnp.where(valid, jnp.where(forced, FORCE_SCORE, imp), -jnp.inf)
    rank = jnp.zeros((NSA_H, CMP_PAD), f32)
    for jp in range(DEC_N_SEL):
        col = score[:, jp:jp + 1]
        rank = rank + jnp.where((col > score) | ((col == score) & (c > jp)), 1.0, 0.0)
    sel = jnp.where((rank < min(SEL_TOP, DEC_N_SEL)) & (c < DEC_N_SEL), 1.0, 0.0).astype(BF16)

    new = new_ref[...]
    new_b = new.astype(BF16).astype(f32)
    qr_f = qr.astype(f32)
    s_pages = [jnp.dot(qr, pages[p][2 * SLOT_ROWS:3 * SLOT_ROWS, :].astype(BF16), preferred_element_type=f32)
               for p in range(N_PAGES)]
    s = jnp.concatenate(s_pages, axis=1)
    mask = jnp.dot(sel, exp_ref[...], preferred_element_type=f32) > 0.5
    s_new = jnp.sum(qr_f * new_b[0:1], axis=-1, keepdims=True)
    e, e_new, den = _softmax_rows(s, mask, s_new)
    e = e.astype(BF16)
    o_slc = e_new.astype(BF16).astype(f32) * new_b[1:2]
    for p in range(N_PAGES):
        o_slc = o_slc + lax.dot_general(e[:, p * PAGE_SIZE:(p + 1) * PAGE_SIZE],
                                        pages[p][3 * SLOT_ROWS:4 * SLOT_ROWS, :].astype(BF16), nt,
                                        preferred_element_type=f32)
    o_slc = o_slc / den

    s = jnp.dot(qr, win_ref[0:SLOT_ROWS, :].astype(BF16), preferred_element_type=f32)
    i_buf = lax.broadcasted_iota(jnp.int32, (NSA_H, WIN_BUF), 1)
    s_new = jnp.sum(qr_f * new_b[2:3], axis=-1, keepdims=True)
    e, e_new, den = _softmax_rows(s, WIN_BUF - i_buf < WINDOW, s_new)
    o_win = e_new.astype(BF16).astype(f32) * new_b[3:4] + lax.dot_general(
        e.astype(BF16), win_ref[SLOT_ROWS:2 * SLOT_ROWS, :].astype(BF16), nt, preferred_element_type=f32)
    o_win = o_win / den

    gates = gate_ref[...]
    o_ref[...] = gates[:, 0:1] * o_cmp + gates[:, 1:2] * o_slc + gates[:, 2:3] * o_win


def dec_nsa(page_table, cache_t, win_t, qn16, qr16, new_rows, gates, w1t, b1, w2t, b2t, kn, ovT, expand, grp):
    DB = qn16.shape[0]
    per_page = PAGE_SIZE // CMP_STRIDE
    tok = np.arange(PAGE_SIZE)
    perm = jnp.asarray((tok[:, None] // per_page == tok[None, :] % CMP_STRIDE)
                       & (tok[:, None] % per_page == tok[None, :] // CMP_STRIDE), dtype=BF16)
    const = lambda shape: pl.BlockSpec(shape, lambda b, pt: (0,) * len(shape))
    per_b = lambda shape: pl.BlockSpec((None,) + shape, lambda b, pt: (b,) + (0,) * len(shape))
    page_specs = [pl.BlockSpec((None, KV_ROWS, PAGE_SIZE), functools.partial(lambda b, pt, p: (pt[b, p], 0, 0), p=p))
                  for p in range(N_PAGES)]
    in_specs = page_specs + [
        per_b((2 * SLOT_ROWS, WIN_BUF)), per_b((NSA_H, SLOT_ROWS)), per_b((NSA_H, SLOT_ROWS)),
        per_b((4, SLOT_ROWS)), per_b((NSA_H, 3)),
        const(w1t.shape), const(b1.shape), const(w2t.shape), const(b2t.shape), const(kn.shape),
        const(ovT.shape), const(expand.shape), const(grp.shape), const(perm.shape),
    ]
    grid_spec = pltpu.PrefetchScalarGridSpec(
        num_scalar_prefetch=1, grid=(DB,), in_specs=in_specs,
        out_specs=pl.BlockSpec((None, NSA_H, SLOT_ROWS), lambda b, pt: (b, 0, 0)),
        scratch_shapes=[pltpu.VMEM((4, CMP_STRIDE, DEC_N_CHUNK, 2 * NSA_HD), F32),
                        pltpu.VMEM((NSA_G * DEC_N_CHUNK, CMP_R * CMP_HID), F32)])
    return pl.pallas_call(
        _dec_nsa_kernel,
        grid_spec=grid_spec,
        out_shape=jax.ShapeDtypeStruct((DB, NSA_H, SLOT_ROWS), F32),
        compiler_params=pltpu.CompilerParams(
            dimension_semantics=("arbitrary",), vmem_limit_bytes=VMEM_LIMIT_BYTES),
        name="dec_nsa",
    )(page_table, *([cache_t] * N_PAGES), win_t, qn16, qr16, new_rows, gates, w1t, b1, w2t, b2t, kn, ovT, expand, grp,
      perm)


def _win_shift_kernel(win_ref, new_ref, o_ref):
    w = win_ref[...]
    n = w.shape[1]
    row = lax.broadcasted_iota(jnp.int32, w.shape, 0)
    lane = lax.broadcasted_iota(jnp.int32, w.shape, 1)
    col = jnp.sum(jnp.where(row == lane, jnp.broadcast_to(new_ref[...], w.shape), 0.0), axis=1, keepdims=True)
    o_ref[...] = jnp.where(lane == n - 1, col, pltpu.roll(w, n - 1, 1))


def win_shift(win_t, new_row):
    DB, R, W = win_t.shape
    assert R == W
    return pl.pallas_call(
        _win_shift_kernel,
        grid=(DB,),
        in_specs=[pl.BlockSpec((None, R, W), lambda b: (b, 0, 0)), pl.BlockSpec((None, 1, R), lambda b: (b, 0, 0))],
        out_specs=pl.BlockSpec((None, R, W), lambda b: (b, 0, 0)),
        out_shape=jax.ShapeDtypeStruct((DB, R, W), win_t.dtype),
        compiler_params=pltpu.CompilerParams(dimension_semantics=("parallel",), vmem_limit_bytes=VMEM_LIMIT_BYTES),
        name="win_shift",
    )(win_t, new_row)


def _dec_cmp_weights(w1, b1, w2, b2):
    w = jnp.moveaxis(w1, 0, 1).reshape(CMP_STRIDE * NSA_HD, CMP_R * CMP_HID)
    return (w.astype(BF16), b1.reshape(1, CMP_HID), jnp.tile(w2, (1, NSA_G)).astype(BF16),
            jnp.tile(b2, NSA_G).reshape(1, SLOT_ROWS))


def _place_heads(q):
    own = (jnp.arange(NSA_H)[:, None] // NSA_HPG) == jnp.arange(NSA_G)[None, :]
    return jnp.where(own[None, :, :, None], q[:, :, None, :], 0.0).reshape(q.shape[0], NSA_H, SLOT_ROWS)


def _take_heads(o):
    o = o.reshape(o.shape[0], NSA_H, NSA_G, NSA_HD)
    return o[:, jnp.arange(NSA_H), jnp.arange(NSA_H) // NSA_HPG, :].reshape(o.shape[0], NSA_W)


WKV_C = 64
WKV_PAIR = 2 * RWKV_HD
WKV_T_TILE = 512
WKV_PAIRS_PER_STEP = 4


def _split_bf16(x):
    hi = x.astype(BF16)
    return hi, (x - hi.astype(F32)).astype(BF16)


def _dot3(a, b):
    a_hi, a_lo = _split_bf16(a)
    b_hi, b_lo = _split_bf16(b)
    return (jnp.dot(a_hi, b_hi, preferred_element_type=F32) + jnp.dot(a_hi, b_lo, preferred_element_type=F32)
            + jnp.dot(a_lo, b_hi, preferred_element_type=F32))


def _wkv_kernel(r_ref, lw_ref, k_ref, v_ref, a_ref, b_ref, s0_ref, y_ref, sT_ref, s_scr):
    C = WKV_C
    P = WKV_PAIR
    n_chunks = r_ref.shape[0] // C

    @pl.when(pl.program_id(2) == 0)
    def _():
        s_scr[...] = s0_ref[...]

    lo_lane = lax.broadcasted_iota(jnp.int32, (C, P), 1) < RWKV_HD
    row = lax.broadcasted_iota(jnp.int32, (2 * C, 2 * C), 0)
    col = lax.broadcasted_iota(jnp.int32, (2 * C, 2 * C), 1)
    same_head = (row // C) == (col // C)
    strict = same_head & (row > col)
    lower = same_head & (row >= col)
    eye = jnp.where(row == col, 1.0, 0.0)
    tril = jnp.where(lax.broadcasted_iota(jnp.int32, (C, C), 0) >= lax.broadcasted_iota(jnp.int32, (C, C), 1),
                     1.0, 0.0).astype(BF16)

    def stack(x):
        return jnp.concatenate([jnp.where(lo_lane, x, 0.0), jnp.where(lo_lane, 0.0, x)], axis=0)

    def chunk(c, carry):
        stages = [pair_chunk(c, q) for q in range(WKV_PAIRS_PER_STEP)]
        while stages:
            stages = [g for g in stages if next(g, True) is None]
        return carry

    def pair_chunk(c, q):
        sl = pl.ds(pl.multiple_of(c * C, C), C)
        lanes = slice(q * P, (q + 1) * P)
        r, lw, k, v, a, b = (ref[sl, lanes] for ref in (r_ref, lw_ref, k_ref, v_ref, a_ref, b_ref))
        lw_hi, lw_mid = _split_bf16(lw)
        lw_lo = (lw - lw_hi.astype(F32) - lw_mid.astype(F32)).astype(BF16)
        cs = (jnp.dot(tril, lw_hi, preferred_element_type=F32) + jnp.dot(tril, lw_mid, preferred_element_type=F32)
              + jnp.dot(tril, lw_lo, preferred_element_type=F32))
        yield
        g_inv = jnp.exp(-cs)
        g_end = jnp.exp(cs[C - 1:C, :] - cs)
        a2 = stack(a * jnp.exp(cs - lw))
        r2 = stack(r * jnp.exp(cs))
        b2 = stack(b * g_inv)
        k2 = stack(k * g_inv)
        v2 = stack(v)
        s_old = s_scr[q]
        ar = jnp.concatenate([a2, r2], axis=0).astype(BF16)
        bk = jnp.concatenate([b2, k2], axis=0).astype(BF16)
        nt = (((1,), (1,)), ((), ()))
        pp = lax.dot_general(ar, bk, nt, preferred_element_type=F32)
        from_state = lax.dot_general(ar, s_old.astype(BF16), nt, preferred_element_type=F32)
        yield
        l_ab = jnp.where(strict, pp[:2 * C, :2 * C], 0.0)
        l_ak = jnp.where(strict, pp[:2 * C, 2 * C:], 0.0)
        m_rb = jnp.where(lower, pp[2 * C:, :2 * C], 0.0)
        m_rk = jnp.where(lower, pp[2 * C:, 2 * C:], 0.0)
        v2b = v2.astype(BF16)
        rhs = from_state[:2 * C] + jnp.dot(l_ak.astype(BF16), v2b, preferred_element_type=F32)
        yield
        n = l_ab
        x = eye + n
        span = 2
        while span < C:
            n = _dot3(n, n)
            yield
            x = x + _dot3(n, x)
            yield
            span *= 2
        u2 = _dot3(x, rhs)
        yield
        uv = jnp.concatenate([u2, v2], axis=0).astype(BF16)
        y2 = from_state[2 * C:] + jnp.dot(jnp.concatenate([m_rb, m_rk], axis=1).astype(BF16), uv,
                                          preferred_element_type=F32)
        yield
        y_ref[sl, lanes] = y2[:C] + y2[C:]
        bk_end = jnp.concatenate([stack(b * g_end), stack(k * g_end)], axis=0).astype(BF16)
        s_scr[q] = s_old * jnp.exp(cs[C - 1:C, :]) + lax.dot_general(
            uv, bk_end, (((0,), (0,)), ((), ())), preferred_element_type=F32)

    lax.fori_loop(0, n_chunks, chunk, 0)

    @pl.when(pl.program_id(2) == pl.num_programs(2) - 1)
    def _():
        sT_ref[...] = s_scr[...]


def wkv7_chunked(r, lw, k, v, a, b, s0):
    B, T, W = r.shape
    n_pair = W // WKV_PAIR
    tt = min(WKV_T_TILE, T)
    s0p = s0.astype(F32).reshape(B, n_pair, 2, RWKV_HD, RWKV_HD)
    zero = jnp.zeros_like(s0p[:, :, 0])
    s0_bd = jnp.concatenate([jnp.concatenate([s0p[:, :, 0], zero], axis=-1),
                             jnp.concatenate([zero, s0p[:, :, 1]], axis=-1)], axis=-2)
    pps = WKV_PAIRS_PER_STEP
    seq = pl.BlockSpec((None, tt, pps * WKV_PAIR), lambda i, p, t: (i, t, p))
    st = pl.BlockSpec((None, pps, WKV_PAIR, WKV_PAIR), lambda i, p, t: (i, p, 0, 0))
    y, s_bd = pl.pallas_call(
        _wkv_kernel,
        grid=(B, n_pair // pps, T // tt),
        in_specs=[seq] * 6 + [st],
        out_specs=[seq, st],
        out_shape=[jax.ShapeDtypeStruct((B, T, W), F32),
                   jax.ShapeDtypeStruct((B, n_pair, WKV_PAIR, WKV_PAIR), F32)],
        scratch_shapes=[pltpu.VMEM((pps, WKV_PAIR, WKV_PAIR), F32)],
        compiler_params=pltpu.CompilerParams(
            dimension_semantics=("parallel", "parallel", "arbitrary"), vmem_limit_bytes=VMEM_LIMIT_BYTES),
        name="wkv7_chunked",
    )(r, lw, k, v, a, b, s0_bd)
    s_fin = jnp.stack([s_bd[:, :, :RWKV_HD, :RWKV_HD], s_bd[:, :, RWKV_HD:, RWKV_HD:]], axis=2)
    return y, s_fin.reshape(B, W // RWKV_HD, RWKV_HD, RWKV_HD)


AB_PAD = _round_up(AB_COLS, WIDE_COL_TILE)
SHIFT_PAD = _round_up(SHIFT_W, LANE)
LORA_PAD = SHIFT_PAD - 3 * RWKV_W
EVEN_ROWS = 128
N_EVEN_PRE_OUT = 10


def _split3(x):
    hi = x.astype(BF16)
    r1 = x - hi.astype(F32)
    mid = r1.astype(BF16)
    return hi, mid, (r1 - mid.astype(F32)).astype(BF16)


def _dot_01(x, m):
    return sum(jnp.dot(part, m, preferred_element_type=F32) for part in _split3(x))


def _head_sum(x, red_ref, exp_ref):
    return _dot_01(_dot_01(x, red_ref[...]), exp_ref[...])


def _expm1(x):
    u = jnp.exp(x)
    d = u - 1.0
    log_u = jnp.where((d == 0.0) | (d == -1.0), 1.0, jnp.log(u))
    return jnp.where(d == 0.0, x, jnp.where(d == -1.0, -1.0, d * x / log_u))


def _even_pre_math(x_ref, prev, taps, prm, outs):
    (cw_ref, cb_ref, wa_ref, ba_ref, wx_ref, bx_ref, lam_ref, mu_ref, w0_ref, a0_ref, wl_ref,
     kk_ref, ka_ref, red_ref, exp_ref) = prm
    a_o, u_o, gate_o, r_o, lw_o, k_o, v_o, na_o, nb_o, g_o = outs
    t1, t2, t3 = taps
    xb = x_ref[:, 0:LRU_W]
    xc = cb_ref[...] + cw_ref[0:1] * t3 + cw_ref[1:2] * t2 + cw_ref[2:3] * t1 + cw_ref[3:4] * xb
    xcb = xc.astype(BF16)
    gate_r = jax.nn.sigmoid(jnp.dot(xcb, wa_ref[...], preferred_element_type=F32) + ba_ref[...])
    gate_i = jax.nn.sigmoid(jnp.dot(xcb, wx_ref[...], preferred_element_type=F32) + bx_ref[...])
    log_a = -LRU_C * gate_r * lam_ref[...]
    a_o[...] = jnp.exp(log_a)
    u_o[...] = jnp.sqrt(-_expm1(2.0 * log_a)) * (gate_i * xc)
    gate_o[...] = jax.nn.gelu(x_ref[:, LRU_W:2 * LRU_W])
    rw = x_ref[:, 2 * LRU_W:2 * LRU_W + SHIFT_PAD]
    rs = rw + mu_ref[...] * (prev - rw)
    r_o[...] = rs[:, 0:RWKV_W]
    k = rs[:, RWKV_W:2 * RWKV_W]
    v_o[...] = rs[:, 2 * RWKV_W:3 * RWKV_W]
    tail = rs[:, 3 * RWKV_W:]
    lane = lax.broadcasted_iota(jnp.int32, tail.shape, 1)
    act = jnp.where(lane < W_LORA, jnp.tanh(tail), jnp.where(lane < W_LORA + A_LORA, tail, jax.nn.sigmoid(tail)))
    z = jnp.dot(act.astype(BF16), wl_ref[...], preferred_element_type=F32)
    w_log = -jax.nn.softplus(-(w0_ref[...] + z[:, 0:RWKV_W])) - 0.5
    lw_o[...] = -jnp.exp(w_log)
    a_icl = jax.nn.sigmoid(a0_ref[...] + z[:, RWKV_W:2 * RWKV_W])
    g_o[...] = z[:, 2 * RWKV_W:]
    kk = k * kk_ref[...]
    kk = kk / jnp.maximum(jnp.sqrt(_head_sum(kk * kk, red_ref, exp_ref)), 1e-12)
    k_o[...] = k * (1.0 + (a_icl - 1.0) * ka_ref[...])
    na_o[...] = -kk
    nb_o[...] = kk * a_icl


def _even_pre_seq_kernel(x_ref, conv0_ref, shift0_ref, *refs):
    prm = refs[:15]
    outs = refs[15:15 + N_EVEN_PRE_OUT]
    conv_c, shift_c = refs[15 + N_EVEN_PRE_OUT:]
    rows = x_ref.shape[0]

    @pl.when(pl.program_id(1) == 0)
    def _():
        conv_c[...] = conv0_ref[...]
        shift_c[...] = shift0_ref[...]

    xb = x_ref[:, 0:LRU_W]
    row = lax.broadcasted_iota(jnp.int32, xb.shape, 0)
    taps = []
    for j in (1, 2, 3):
        tap = pltpu.roll(xb, j, 0)
        for i in range(j):
            tap = jnp.where(row == i, conv_c[8 - j + i:9 - j + i, :], tap)
        taps.append(tap)
    rw = x_ref[:, 2 * LRU_W:2 * LRU_W + SHIFT_PAD]
    row_w = lax.broadcasted_iota(jnp.int32, rw.shape, 0)
    prev = jnp.where(row_w == 0, shift_c[7:8, :], pltpu.roll(rw, 1, 0))
    _even_pre_math(x_ref, prev, taps, prm, outs)
    conv_c[...] = x_ref[rows - 8:rows, 0:LRU_W]
    shift_c[...] = x_ref[rows - 8:rows, 2 * LRU_W:2 * LRU_W + SHIFT_PAD]


def _even_pre_step_kernel(x_ref, prev_ref, t1_ref, t2_ref, t3_ref, *refs):
    _even_pre_math(x_ref, prev_ref[...], (t1_ref[...], t2_ref[...], t3_ref[...]), refs[:15], refs[15:])


def _even_params(p):
    def bd(w):
        eye = jnp.eye(LRU_BLOCKS, dtype=w.dtype)
        return (eye[:, None, :, None] * w[:, :, None, :]).reshape(LRU_W, LRU_W).astype(BF16)
    row = lambda v: v.reshape(1, -1).astype(F32)
    wl = jnp.zeros((LORA_PAD, 3 * RWKV_W), F32)
    wl = wl.at[0:W_LORA, 0:RWKV_W].set(p['w2'])
    wl = wl.at[W_LORA:W_LORA + A_LORA, RWKV_W:2 * RWKV_W].set(p['a2'])
    wl = wl.at[W_LORA + A_LORA:W_LORA + A_LORA + G_LORA, 2 * RWKV_W:].set(p['g2'])
    head = np.arange(RWKV_W) // RWKV_HD
    red = jnp.asarray(head[:, None] == np.arange(LANE)[None, :], dtype=BF16)
    mu = jnp.pad(p['mu'], (0, SHIFT_PAD - SHIFT_W))
    return [p['conv_w'].astype(F32), row(p['conv_b']), bd(p['wa']), row(p['ba']), bd(p['wx']), row(p['bx']),
            row(jax.nn.softplus(-p['lam'].astype(F32))), row(mu), row(p['w0']), row(p['a0']), wl.astype(BF16),
            row(p['k_k']), row(p['k_a']), red, red.T]


def _const_spec(a, n_grid):
    return pl.BlockSpec(a.shape, lambda *_: (0,) * a.ndim)


def even_pre_seq(proj, conv0, shift0, prm, B, T):
    tr = EVEN_ROWS
    nt = T // tr
    conv_pad = jnp.pad(conv0.astype(F32), ((0, 0), (8 - (CONV_W - 1), 0), (0, 0)))
    shift_pad = jnp.pad(shift0.astype(F32)[:, None, :], ((0, 0), (7, 0), (0, SHIFT_PAD - SHIFT_W)))
    out_spec = pl.BlockSpec((tr, LRU_W), lambda b, t: (b * nt + t, 0))
    return pl.pallas_call(
        _even_pre_seq_kernel,
        grid=(B, nt),
        in_specs=[pl.BlockSpec((tr, AB_PAD), lambda b, t: (b * nt + t, 0)),
                  pl.BlockSpec((None, 8, LRU_W), lambda b, t: (b, 0, 0)),
                  pl.BlockSpec((None, 8, SHIFT_PAD), lambda b, t: (b, 0, 0))] + [_const_spec(a, 2) for a in prm],
        out_specs=[out_spec] * N_EVEN_PRE_OUT,
        out_shape=[jax.ShapeDtypeStruct((B * T, LRU_W), F32)] * N_EVEN_PRE_OUT,
        scratch_shapes=[pltpu.VMEM((8, LRU_W), F32), pltpu.VMEM((8, SHIFT_PAD), F32)],
        compiler_params=pltpu.CompilerParams(
            dimension_semantics=("parallel", "arbitrary"), vmem_limit_bytes=VMEM_LIMIT_BYTES),
        name="even_pre_seq",
    )(proj, conv_pad, shift_pad, *prm)


def even_pre_step(proj, row0, conv0, shift0, prm):
    n = conv0.shape[0]
    shift_pad = jnp.pad(shift0.astype(F32), ((0, 0), (0, SHIFT_PAD - SHIFT_W)))
    taps = [conv0[:, CONV_W - 1 - j].astype(F32) for j in (1, 2, 3)]
    full = lambda w: pl.BlockSpec((n, w), lambda i: (0, 0))
    return pl.pallas_call(
        _even_pre_step_kernel,
        grid=(1,),
        in_specs=[pl.BlockSpec((n, AB_PAD), lambda i: (row0 // n, 0)), full(SHIFT_PAD)] + [full(LRU_W)] * 3
        + [_const_spec(a, 1) for a in prm],
        out_specs=[full(LRU_W)] * N_EVEN_PRE_OUT,
        out_shape=[jax.ShapeDtypeStruct((n, LRU_W), F32)] * N_EVEN_PRE_OUT,
        compiler_params=pltpu.CompilerParams(
            dimension_semantics=("arbitrary",), vmem_limit_bytes=VMEM_LIMIT_BYTES),
        name="even_pre_step",
    )(proj, shift_pad, *taps, *prm)


def _even_post_kernel(hs_ref, gate_ref, y_ref, r_ref, k_ref, v_ref, g_ref, lng_ref, lnb_ref, rk_ref,
                      red_ref, exp_ref, *rest):
    o_ref = rest[-1]
    y = y_ref[...]
    mu = _head_sum(y, red_ref, exp_ref) * (1.0 / RWKV_HD)
    d = y - mu
    var = _head_sum(d * d, red_ref, exp_ref) * (1.0 / RWKV_HD)
    yn = d * lax.rsqrt(var + 64e-5) * lng_ref[...] + lnb_ref[...]
    bonus = _head_sum(r_ref[...] * k_ref[...] * rk_ref[...], red_ref, exp_ref) * v_ref[...]
    o_ref[:, 0:LRU_W] = (hs_ref[...] * gate_ref[...]).astype(o_ref.dtype)
    o_ref[:, LRU_W:] = ((yn + bonus) * g_ref[...]).astype(o_ref.dtype)


def even_post(hs, gate, y, r, k, v, g, p, red, n_total, row0, prior=None):
    n = hs.shape[0]
    tr = EVEN_ROWS
    row = lambda a: a.reshape(1, -1).astype(F32)
    consts = [row(p['ln_g']), row(p['ln_b']), row(p['r_k']), red, red.T]
    seq = pl.BlockSpec((tr, LRU_W), lambda i: (i, 0))
    args = [hs, gate, y, r, k, v, g] + consts
    in_specs = [seq] * 7 + [_const_spec(a, 1) for a in consts]
    aliases = {}
    if prior is not None:
        args.append(prior)
        in_specs.append(pl.BlockSpec(memory_space=pl.ANY))
        aliases = {len(args) - 1: 0}
    return pl.pallas_call(
        _even_post_kernel,
        grid=(n // tr,),
        in_specs=in_specs,
        out_specs=pl.BlockSpec((tr, D_MODEL), lambda i: (row0 // tr + i, 0)),
        out_shape=jax.ShapeDtypeStruct((n_total, D_MODEL), BF16),
        input_output_aliases=aliases,
        compiler_params=pltpu.CompilerParams(
            dimension_semantics=("parallel",), vmem_limit_bytes=VMEM_LIMIT_BYTES),
        name="even_post",
    )(*args)


def _retention_kernel(q_ref, k_ref, va_ref, vb_ref, dm_ref, rd_ref, kd_ref, sd_ref, o_ref, s_out_ref, s_scr):
    C = q_ref.shape[0]
    n_pair = RET_H // 2

    @pl.when(pl.program_id(1) == 0)
    def _():
        s_scr[...] = jnp.zeros(s_scr.shape, F32)

    lo = lax.broadcasted_iota(jnp.int32, (C, 2 * RET_DK), 1) < RET_DK

    def stack(x):
        return jnp.concatenate([jnp.where(lo, x, 0.0), jnp.where(lo, 0.0, x)], axis=0)

    for p in range(n_pair):
        qk = slice(p * 2 * RET_DK, (p + 1) * 2 * RET_DK)
        q2 = stack(q_ref[:, qk]).astype(BF16)
        k2 = stack(k_ref[:, qk])
        v0 = p * 2 * RET_DV
        v_ref = va_ref if p < n_pair // 2 else vb_ref
        vl = v0 % (RET_W // 2)
        v2 = jnp.concatenate([v_ref[:, vl:vl + RET_DV], v_ref[:, vl + RET_DV:vl + 2 * RET_DV]],
                             axis=0).astype(BF16)
        s = lax.dot_general(q2, k2.astype(BF16), (((1,), (1,)), ((), ())), preferred_element_type=F32) * dm_ref[p]
        s_old = s_scr[p]
        o2 = jnp.dot(s.astype(BF16), v2, preferred_element_type=F32) + jnp.dot(
            q2, s_old.astype(BF16), preferred_element_type=F32) * rd_ref[p]
        o_ref[:, v0:v0 + RET_DV] = o2[:C]
        o_ref[:, v0 + RET_DV:v0 + 2 * RET_DV] = o2[C:]
        s_scr[p] = s_old * sd_ref[p] + lax.dot_general((k2 * kd_ref[p]).astype(BF16), v2, (((0,), (0,)), ((), ())),
                                                       preferred_element_type=F32)

    @pl.when(pl.program_id(1) == pl.num_programs(1) - 1)
    def _():
        s_out_ref[...] = s_scr[...]


def retention_prompt_pallas(rq, rk, rv, B, T, v_col0=0):
    C = RET_CHUNK
    nc = T // C
    f32 = F32
    lg = jnp.log1p(-jnp.exp2(-5.0 - jnp.arange(RET_H, dtype=f32))).reshape(RET_H // 2, 2)
    i = jnp.arange(C, dtype=f32)
    diff = i[:, None] - i[None, :]
    causal = diff >= 0
    dmask = jnp.where(causal, jnp.exp(jnp.where(causal, diff, 0.0)[None, None] * lg[:, :, None, None]), 0.0)
    zero = jnp.zeros_like(dmask[:, 0])
    dm = jnp.concatenate([jnp.concatenate([dmask[:, 0], zero], axis=-1),
                          jnp.concatenate([zero, dmask[:, 1]], axis=-1)], axis=-2)
    rows = lambda x, w: jnp.broadcast_to(x[:, :, :, None], x.shape + (w,)).reshape(RET_H // 2, -1, w)
    rd = rows(jnp.exp((i[None, None, :] + 1.0) * lg[:, :, None]), RET_DV)
    kd = rows(jnp.exp((C - 1.0 - i)[None, None, :] * lg[:, :, None]), 2 * RET_DK)
    sd = rows(jnp.broadcast_to(jnp.exp(C * lg)[:, :, None], (RET_H // 2, 2, RET_DK)), RET_DV)
    half_w = RET_W // 2
    qk_spec = pl.BlockSpec((C, RET_H * RET_DK), lambda b, c: (b * nc + c, 0))
    v_spec = lambda k: pl.BlockSpec((C, half_w), lambda b, c: (b * nc + c, v_col0 // half_w + k))
    const = lambda a: pl.BlockSpec(a.shape, lambda b, c: (0, 0, 0))
    o, s = pl.pallas_call(
        _retention_kernel,
        grid=(B, nc),
        in_specs=[qk_spec, qk_spec, v_spec(0), v_spec(1), const(dm), const(rd), const(kd), const(sd)],
        out_specs=[pl.BlockSpec((C, RET_W), lambda b, c: (b * nc + c, 0)),
                   pl.BlockSpec((None, RET_H // 2, 2 * RET_DK, RET_DV), lambda b, c: (b, 0, 0, 0))],
        out_shape=[jax.ShapeDtypeStruct((B * T, RET_W), f32),
                   jax.ShapeDtypeStruct((B, RET_H // 2, 2 * RET_DK, RET_DV), f32)],
        scratch_shapes=[pltpu.VMEM((RET_H // 2, 2 * RET_DK, RET_DV), f32)],
        compiler_params=pltpu.CompilerParams(
            dimension_semantics=("parallel", "arbitrary"), vmem_limit_bytes=VMEM_LIMIT_BYTES),
        name="retention_prompt",
    )(rq, rk, rv, rv, dm, rd, kd, sd)
    return s.reshape(B, RET_H, RET_DK, RET_DV), o


KV_W = NSA_G * NSA_HD
RET_QK_W = RET_H * RET_DK
OFF_Q = 0
OFF_KC = OFF_Q + NSA_W
OFF_VC = OFF_KC + KV_W
OFF_KS = OFF_VC + KV_W
OFF_VS = OFF_KS + KV_W
OFF_KW = OFF_VS + KV_W
OFF_VW = OFF_KW + KV_W
OFF_RQ = OFF_VW + KV_W
OFF_RK = OFF_RQ + RET_QK_W
OFF_RV = OFF_RK + RET_QK_W
OFF_RG = OFF_RV + RET_W
OFF_GT = OFF_RG + RET_W
CD_PAD = _round_up(OFF_GT + LANE, COL_TILE)
ODD_ROWS = 128
N_ODD_PRE_OUT = 11


def _odd_weight_cols(w):
    gt0 = NSA_W + 6 * KV_W
    body = jnp.concatenate([w[:, :gt0], w[:, gt0 + 3 * NSA_H:]], axis=1)
    gt = w[:, gt0:gt0 + 3 * NSA_H]
    out = jnp.concatenate([body, gt], axis=1)
    return jnp.pad(out, ((0, 0), (0, CD_PAD - out.shape[1]))).astype(BF16)


def _rope_tables(pos, n_rot, theta, head):
    half = n_rot // 2
    inv = jnp.exp(-jnp.log(jnp.float32(theta)) * jnp.arange(half, dtype=jnp.float32) / half)
    ang = pos.astype(jnp.float32)[:, None] * inv[None, :]
    cos, sin = jnp.cos(ang), jnp.sin(ang)
    d = np.arange(LANE) % head
    cos_d, sin_d = cos[:, d % half], sin[:, d % half]
    c = jnp.where(d < n_rot, cos_d, 1.0)
    s1 = jnp.where(d < half, -sin_d, 0.0)
    s2 = jnp.where((d >= half) & (d < n_rot), sin_d, 0.0)
    return jnp.stack([c, s1, s2])


def _rope_lanes(x, tab_ref, half):
    w = x.shape[1]
    rep = w // LANE
    c, s1, s2 = (pltpu.repeat(tab_ref[i], rep, axis=1) for i in range(3))
    return x * c + pltpu.roll(x, w - half, 1) * s1 + pltpu.roll(x, half, 1) * s2


def _rms_heads(x, g_ref, red_ref, exp_ref):
    ms = _head_sum(x * x, red_ref, exp_ref) * (1.0 / NSA_HD)
    return x * lax.rsqrt(ms + 1e-6) * g_ref[...]


def _odd_pre_kernel(x_ref, nsa_tab, ret_tab, qg_ref, ksg_ref, kwg_ref, redq_ref, expq_ref, redk_ref, expk_ref,
                    tile_ref, qn_o, qr_o, ks_o, kw_o, ks4_o, vs4_o, kw4_o, vw4_o, gate_o, rq_o, rk_o, kvt_o, wint_o):
    nsa_half = ROPE_DIMS // 2
    qn = _rms_heads(x_ref[:, OFF_Q:OFF_Q + NSA_W], qg_ref, redq_ref, expq_ref)
    qn_o[...] = qn
    qr_o[...] = _rope_lanes(qn, nsa_tab, nsa_half)
    ks = _rope_lanes(_rms_heads(x_ref[:, OFF_KS:OFF_KS + KV_W], ksg_ref, redk_ref, expk_ref), nsa_tab, nsa_half)
    kw = _rope_lanes(_rms_heads(x_ref[:, OFF_KW:OFF_KW + KV_W], kwg_ref, redk_ref, expk_ref), nsa_tab, nsa_half)
    ks_o[...] = ks
    kw_o[...] = kw
    tile = tile_ref[...]
    for src, dst in ((ks, ks4_o), (x_ref[:, OFF_VS:OFF_VS + KV_W], vs4_o), (kw, kw4_o),
                     (x_ref[:, OFF_VW:OFF_VW + KV_W], vw4_o)):
        dst[...] = jnp.dot(src.astype(BF16), tile, preferred_element_type=F32).astype(BF16)
    gate_o[...] = jax.nn.sigmoid(x_ref[:, OFF_GT:OFF_GT + LANE])
    rq_o[...] = _rope_lanes(x_ref[:, OFF_RQ:OFF_RQ + RET_QK_W], ret_tab, RET_DK // 2)
    rk_o[...] = _rope_lanes(x_ref[:, OFF_RK:OFF_RK + RET_QK_W], ret_tab, RET_DK // 2) * (RET_DK ** -0.5)
    kv_pieces = (x_ref[:, OFF_KC:OFF_KC + KV_W], x_ref[:, OFF_VC:OFF_VC + KV_W], ks, x_ref[:, OFF_VS:OFF_VS + KV_W])
    for dst, pieces in ((kvt_o, kv_pieces), (wint_o, (kw, x_ref[:, OFF_VW:OFF_VW + KV_W]))):
        for s, piece in enumerate(pieces):
            for c in range(KV_W // LANE):
                dst[s * KV_W + c * LANE:s * KV_W + (c + 1) * LANE, :] = piece[:, c * LANE:(c + 1) * LANE].T


def odd_pre(proj, pos, p, row0, n_rows, same_pos, seq_len):
    tr = ODD_ROWS
    blk0 = row0 // tr
    n_tab = tr if same_pos else n_rows
    pos_rows = jnp.broadcast_to(pos, (n_tab,)) if same_pos else pos
    nsa_tab = _rope_tables(pos_rows, ROPE_DIMS, ROPE_THETA, NSA_HD)
    ret_tab = _rope_tables(pos_rows, RET_DK, RET_THETA, RET_DK)
    row = lambda v, rep: jnp.tile(v.astype(F32), rep).reshape(1, -1)
    lanes = np.arange(LANE)
    red_q = jnp.asarray((np.arange(NSA_W) // NSA_HD)[:, None] == lanes[None, :], dtype=BF16)
    red_k = jnp.asarray((np.arange(KV_W) // NSA_HD)[:, None] == lanes[None, :], dtype=BF16)
    src = np.arange(KV_W)
    dst = np.arange(NSA_W)
    tile = jnp.asarray((src[:, None] // NSA_HD == dst[None, :] // GROUP_W)
                       & (src[:, None] % NSA_HD == dst[None, :] % NSA_HD), dtype=BF16)
    consts = [row(p['q_norm'], NSA_H), row(p['k_norm'][1], NSA_G), row(p['k_norm'][2], NSA_G),
              red_q, red_q.T, red_k, red_k.T, tile]
    tab_spec = pl.BlockSpec((3, tr, LANE), (lambda i: (0, 0, 0)) if same_pos else (lambda i: (0, i, 0)))
    out = lambda w, dt: (pl.BlockSpec((tr, w), lambda i: (i, 0)), jax.ShapeDtypeStruct((n_rows, w), dt))
    seq_tiles = seq_len // tr
    out_t = lambda r: (pl.BlockSpec((None, r, tr), lambda i: (i // seq_tiles, 0, i % seq_tiles)),
                       jax.ShapeDtypeStruct((n_rows // seq_len, r, seq_len), F32))
    outs = [out(NSA_W, F32), out(NSA_W, F32), out(KV_W, F32), out(KV_W, F32)] + [out(NSA_W, BF16)] * 4 + [
        out(LANE, F32), out(RET_QK_W, F32), out(RET_QK_W, F32), out_t(KV_SLOTS * KV_W), out_t(2 * KV_W)]
    return pl.pallas_call(
        _odd_pre_kernel,
        grid=(n_rows // tr,),
        in_specs=[pl.BlockSpec((tr, CD_PAD), lambda i: (blk0 + i, 0)), tab_spec, tab_spec]
        + [_const_spec(a, 1) for a in consts],
        out_specs=[o[0] for o in outs],
        out_shape=[o[1] for o in outs],
        compiler_params=pltpu.CompilerParams(
            dimension_semantics=("parallel",), vmem_limit_bytes=VMEM_LIMIT_BYTES),
        name="odd_pre",
    )(proj, nsa_tab, ret_tab, *consts)


def _odd_post_kernel(oc_ref, os_ref, ow_ref, gate_ref, ret_ref, rg0_ref, rg1_ref, gng_ref, gnb_ref, ge_ref, *rest,
                     gated):
    o_ref = rest[-1]
    if gated:
        nsa = oc_ref[...]
    else:
        gates = gate_ref[...]
        nsa = jnp.zeros(oc_ref.shape, F32)
        for j, branch in enumerate((oc_ref, os_ref, ow_ref)):
            nsa = nsa + _dot_01(gates, ge_ref[j]) * branch[...]
    o_ref[:, 0:NSA_W] = nsa.astype(o_ref.dtype)
    for h in range(RET_H):
        lanes = slice(h * RET_DV, (h + 1) * RET_DV)
        x = ret_ref[:, lanes]
        mu = jnp.mean(x, axis=-1, keepdims=True)
        d = x - mu
        var = jnp.mean(d * d, axis=-1, keepdims=True)
        yn = d * lax.rsqrt(var + 1e-5) * gng_ref[:, lanes] + gnb_ref[:, lanes]
        rg = (rg0_ref if h < RET_H // 2 else rg1_ref)[:, (h % (RET_H // 2)) * RET_DV:(h % (RET_H // 2) + 1) * RET_DV]
        o_ref[:, NSA_W + h * RET_DV:NSA_W + (h + 1) * RET_DV] = (yn * (rg * jax.nn.sigmoid(rg))).astype(o_ref.dtype)


def odd_post(o_cmp, o_slc, o_win, gates, o_ret, proj, p, n_total, row0, prior=None, gated=False):
    n = o_cmp.shape[0]
    tr = ODD_ROWS
    blk0 = row0 // tr
    h = np.arange(NSA_W) // NSA_HD
    ge = jnp.asarray(np.stack([(np.arange(LANE)[:, None] == (3 * h + j)[None, :]) for j in range(3)]), dtype=BF16)
    row = lambda a: a.reshape(1, -1).astype(F32)
    consts = [row(p['gn_g']), row(p['gn_b']), ge]
    seq = lambda w: pl.BlockSpec((tr, w), lambda i: (i, 0))
    half = RET_W // 2
    rg_spec = lambda k: pl.BlockSpec((tr, half), lambda i: (blk0 + i, OFF_RG // half + k))
    args = [o_cmp, o_slc, o_win, gates, o_ret, proj, proj] + consts
    in_specs = [seq(NSA_W)] * 3 + [seq(LANE), seq(RET_W), rg_spec(0), rg_spec(1)] + [_const_spec(a, 1) for a in consts]
    aliases = {}
    if prior is not None:
        args.append(prior)
        in_specs.append(pl.BlockSpec(memory_space=pl.ANY))
        aliases = {len(args) - 1: 0}
    return pl.pallas_call(
        functools.partial(_odd_post_kernel, gated=gated),
        grid=(n // tr,),
        in_specs=in_specs,
        out_specs=pl.BlockSpec((tr, D_MODEL), lambda i: (blk0 + i, 0)),
        out_shape=jax.ShapeDtypeStruct((n_total, D_MODEL), BF16),
        input_output_aliases=aliases,
        compiler_params=pltpu.CompilerParams(
            dimension_semantics=("parallel",), vmem_limit_bytes=VMEM_LIMIT_BYTES),
        name="odd_post",
    )(*args)


def rms_norm(x, g, eps=1e-6):
    xf = x.astype(jnp.float32)
    y = xf * lax.rsqrt(jnp.mean(xf * xf, axis=-1, keepdims=True) + eps)
    return (y * g.astype(jnp.float32)).astype(x.dtype)


def head_group_norm(y, g, b, eps):
    yf = y.astype(jnp.float32)
    mu = jnp.mean(yf, axis=-1, keepdims=True)
    var = jnp.mean(jnp.square(yf - mu), axis=-1, keepdims=True)
    yn = ((yf - mu) * lax.rsqrt(var + eps)).reshape(y.shape[:-2] + (-1,))
    return (yn * g.astype(jnp.float32) + b.astype(jnp.float32)).astype(y.dtype)


def masked_softmax(s, mask):
    s = jnp.where(mask, s.astype(jnp.float32), -jnp.inf)
    m = jnp.max(s, axis=-1, keepdims=True)
    e = jnp.exp(s - jnp.where(jnp.isfinite(m), m, 0.0))
    den = jnp.sum(e, axis=-1, keepdims=True)
    return e / jnp.where(den > 0, den, 1.0)


def rope(x, pos, n_rot, theta):
    half = n_rot // 2
    inv = jnp.exp(-jnp.log(jnp.float32(theta)) * jnp.arange(half, dtype=jnp.float32) / half)
    ang = pos.astype(jnp.float32)[:, None] * inv[None, :]
    cos = jnp.cos(ang)[None, :, None, :]
    sin = jnp.sin(ang)[None, :, None, :]
    xf = x.astype(jnp.float32)
    x1, x2 = xf[..., :half], xf[..., half:n_rot]
    out = jnp.concatenate([x1 * cos - x2 * sin, x2 * cos + x1 * sin, xf[..., n_rot:]], axis=-1)
    return out.astype(x.dtype)


def linear_scan(a, b, h0):
    b = b.at[:, 0].add(a[:, 0] * h0)

    def combine(left, right):
        return left[0] * right[0], right[0] * left[1] + right[1]

    return lax.associative_scan(combine, (a, b), axis=1)[1]


def wkv7_scan(r, w, k, v, a, b, s0):
    xs = tuple(jnp.moveaxis(z.astype(jnp.float32), 1, 0) for z in (r, w, k, v, a, b))

    def step(S, inp):
        r_t, w_t, k_t, v_t, a_t, b_t = inp
        sa = jnp.einsum('bhij,bhj->bhi', S, a_t)
        S = S * w_t[:, :, None, :] + sa[..., None] * b_t[:, :, None, :] + v_t[..., None] * k_t[:, :, None, :]
        return S, jnp.einsum('bhij,bhj->bhi', S, r_t)

    S, ys = lax.scan(step, s0.astype(jnp.float32), xs)
    return jnp.moveaxis(ys, 0, 1), S


def even_mixer_core(proj, p, lru_h0, lru_conv0, shift0, wkv0):
    B, T, _ = proj.shape
    f32 = jnp.float32
    dt = proj.dtype
    xb, gb, rw = jnp.split(proj, [LRU_W, 2 * LRU_W], axis=-1)
    xcat = jnp.concatenate([lru_conv0.astype(dt), xb], axis=1)
    xc = p['conv_b'] + sum(p['conv_w'][j] * xcat[:, j:j + T] for j in range(CONV_W))
    xbd = xc.reshape(B, T, LRU_BLOCKS, LRU_BS)
    gate_r = jax.nn.sigmoid(jnp.einsum('btnc,ncd->btnd', xbd, p['wa']).reshape(B, T, LRU_W) + p['ba'])
    gate_i = jax.nn.sigmoid(jnp.einsum('btnc,ncd->btnd', xbd, p['wx']).reshape(B, T, LRU_W) + p['bx'])
    log_a = -LRU_C * gate_r.astype(f32) * jax.nn.softplus(-p['lam'].astype(f32))
    u = jnp.sqrt(-jnp.expm1(2.0 * log_a)) * (gate_i * xc).astype(f32)
    hs = lru_scan(jnp.exp(log_a), u, lru_h0.astype(f32))
    y_lru = hs.astype(dt) * jax.nn.gelu(gb)
    prev = jnp.concatenate([shift0.astype(dt)[:, None], rw[:, :-1]], axis=1)
    rs = rw + p['mu'] * (prev - rw)
    r, k, v, xw, xa, xg = jnp.split(
        rs, [RWKV_W, 2 * RWKV_W, 3 * RWKV_W, 3 * RWKV_W + W_LORA, 3 * RWKV_W + W_LORA + A_LORA], axis=-1)
    w_log = -jax.nn.softplus(-(p['w0'] + jnp.tanh(xw) @ p['w2']).astype(f32)) - 0.5
    log_decay = -jnp.exp(w_log)
    decay = jnp.exp(log_decay)
    a_icl = jax.nn.sigmoid(p['a0'] + xa @ p['a2'])
    g = jax.nn.sigmoid(xg) @ p['g2']
    heads = (B, T, RWKV_H, RWKV_HD)
    kk = (k * p['k_k']).reshape(heads).astype(f32)
    kk = kk / jnp.maximum(jnp.sqrt(jnp.sum(kk * kk, axis=-1, keepdims=True)), 1e-12)
    k = k * (1.0 + (a_icl - 1.0) * p['k_a'])
    rh, kh, vh, ah = (z.reshape(heads) for z in (r, k, v, a_icl))
    if T % WKV_C == 0:
        y, wkv = wkv7_chunked(r.astype(f32), log_decay, k.astype(f32), v.astype(f32),
                              (-kk).reshape(B, T, RWKV_W), (kk * ah.astype(f32)).reshape(B, T, RWKV_W), wkv0)
        y = y.reshape(heads)
    else:
        y, wkv = wkv7_scan(rh, decay.reshape(heads), kh, vh, -kk, kk * ah.astype(f32), wkv0)
    y = head_group_norm(y, p['ln_g'], p['ln_b'], 64e-5).astype(dt)
    bonus = (jnp.sum(rh * kh * p['r_k'], axis=-1, keepdims=True) * vh).reshape(B, T, RWKV_W)
    y_rwkv = (y + bonus) * g
    cat = jnp.concatenate([y_lru, y_rwkv], axis=-1)
    return cat, hs[:, -1], xcat[:, T:], rw[:, -1], wkv


def even_mixer(proj, p, B, T, DB, lru_h0, lru_conv0, shift0, wkv0):
    f32 = F32
    prm = _even_params(p)
    red = prm[-2]
    n_p = B * T
    zeros = lambda *s: jnp.zeros(s, f32)
    a, u, gate, r, lw, k, v, na, nb, g = even_pre_seq(proj, zeros(B, CONV_W - 1, LRU_W), zeros(B, SHIFT_W), prm, B, T)
    seq = lambda z: z.reshape(B, T, LRU_W)
    hs = lru_scan(seq(a), seq(u), zeros(B, LRU_W))
    yw, wkv_p = wkv7_chunked(seq(r), seq(lw), seq(k), seq(v), seq(na), seq(nb), zeros(B, RWKV_H, RWKV_HD, RWKV_HD))
    cat = even_post(hs.reshape(n_p, LRU_W), gate, yw.reshape(n_p, RWKV_W), r, k, v, g, p, red, n_p + DB, 0)
    tail = lambda b, n, c0, c1: proj[(b + 1) * T - n:(b + 1) * T, c0:c1]
    st_p = (hs[:, -1], jnp.stack([tail(b, CONV_W - 1, 0, LRU_W) for b in range(B)]),
            jnp.concatenate([tail(b, 1, 2 * LRU_W, AB_COLS) for b in range(B)], axis=0), wkv_p)
    a, u, gate, r, lw, k, v, na, nb, g = even_pre_step(proj, n_p, lru_conv0, shift0, prm)
    hs_s = a * lru_h0.astype(f32) + u
    heads = (DB, 1, RWKV_H, RWKV_HD)
    yw, wkv_s = wkv7_scan(r.reshape(heads), jnp.exp(lw).reshape(heads), k.reshape(heads), v.reshape(heads),
                          na.reshape(heads), nb.reshape(heads), wkv0)
    cat = even_post(hs_s, gate, yw.reshape(DB, RWKV_W), r, k, v, g, p, red, n_p + DB, n_p, prior=cat)
    xb_s = proj[n_p:]
    conv_s = jnp.concatenate([lru_conv0[:, 1:].astype(f32), xb_s[:, None, :LRU_W]], axis=1)
    st_s = (hs_s, conv_s, xb_s[:, 2 * LRU_W:AB_COLS], wkv_s)
    return cat, st_p, st_s


def odd_project(proj, p, pos):
    B, T, _ = proj.shape
    sizes = [NSA_W] + [NSA_G * NSA_HD] * 6 + [3 * NSA_H, RET_H * RET_DK, RET_H * RET_DK, RET_W, RET_W]
    q, kc, vc, ks, vs, kw, vw, gt, rq, rk, rv, rg = jnp.split(
        proj, np.cumsum(sizes).tolist(), axis=-1)[:len(sizes)]
    kvs = (B, T, NSA_G, NSA_HD)
    q_n = rms_norm(q.reshape(B, T, NSA_H, NSA_HD), p['q_norm'])
    return {
        'q_n': q_n,
        'q_r': rope(q_n, pos, ROPE_DIMS, ROPE_THETA),
        'kc': kc.reshape(kvs), 'vc': vc.reshape(kvs),
        'ks': rope(rms_norm(ks.reshape(kvs), p['k_norm'][1]), pos, ROPE_DIMS, ROPE_THETA),
        'vs': vs.reshape(kvs),
        'kw': rope(rms_norm(kw.reshape(kvs), p['k_norm'][2]), pos, ROPE_DIMS, ROPE_THETA),
        'vw': vw.reshape(kvs),
        'gates': jax.nn.sigmoid(gt).reshape(B, T, NSA_H, 3),
        'rq': rope(rq.reshape(B, T, RET_H, RET_DK), pos, RET_DK, RET_THETA),
        'rk': rope(rk.reshape(B, T, RET_H, RET_DK), pos, RET_DK, RET_THETA) * (RET_DK ** -0.5),
        'rv': rv.reshape(B, T, RET_H, RET_DV),
        'rg': rg,
    }


def to_groups_q(q):
    B, T = q.shape[:2]
    return jnp.moveaxis(q.reshape(B, T, NSA_G, NSA_HPG, NSA_HD), 1, 3)


def to_groups_k(k):
    return jnp.moveaxis(k, 1, 2)


def nsa_compress(x, w1, b1, w2, b2):
    B, L = x.shape[:2]
    n_chunk = L // CMP_STRIDE
    n_cmp = n_chunk - CMP_R + 1
    ch = x[:, :n_chunk * CMP_STRIDE].reshape(B, n_chunk, CMP_STRIDE, NSA_G, NSA_HD)
    ch = jnp.moveaxis(ch, 3, 2).reshape(B, n_chunk, NSA_G, CMP_STRIDE * NSA_HD)
    part = jnp.einsum('bngc,rch->bngrh', ch, w1)
    pre = b1 + sum(part[:, m:m + n_cmp, :, m] for m in range(CMP_R))
    return jax.nn.gelu(pre) @ w2 + b2


def nsa_compressed_branch(qn, kc_raw, vc_raw, p, q_pos):
    kc = to_groups_k(rms_norm(nsa_compress(kc_raw, *p['ck']), p['k_norm'][0]))
    vc = to_groups_k(nsa_compress(vc_raw, *p['cv']))
    s = jnp.einsum('bghqd,bgcd->bghqc', qn, kc) * NSA_HD ** -0.5
    ends = jnp.arange(kc.shape[2]) * CMP_STRIDE + CMP_BLOCK - 1
    prob = masked_softmax(s, ends[None, :] <= q_pos[:, None])
    return jnp.einsum('bghqc,bgcd->bghqd', prob.astype(vc.dtype), vc), prob


def cmp_sel_overlap(n_cmp, n_sel):
    cs = np.arange(n_cmp) * CMP_STRIDE
    ss = np.arange(n_sel) * SEL_BLOCK
    ov = np.minimum(cs[None] + CMP_BLOCK, ss[:, None] + SEL_BLOCK) - np.maximum(cs[None], ss[:, None])
    return jnp.asarray(np.clip(ov, 0, None) / CMP_BLOCK, dtype=jnp.float32)


def nsa_select(p_cmp, q_pos, n_sel):
    imp = jnp.einsum('bgqc,sc->bgqs', p_cmp.sum(axis=2), cmp_sel_overlap(p_cmp.shape[-1], n_sel))
    j = jnp.arange(n_sel)[None, :]
    qb = (q_pos // SEL_BLOCK)[:, None]
    valid = j <= qb
    forced = (j == 0) | (j == qb) | (j == qb - 1)
    score = jnp.where(valid, jnp.where(forced, FORCE_SCORE, imp), -jnp.inf)
    _, idx = lax.top_k(score, min(SEL_TOP, n_sel))
    sel_ok = jnp.take_along_axis(jnp.broadcast_to(valid, score.shape), idx, axis=-1)
    return idx, sel_ok


def sel_blocks(x, n_sel):
    B, L = x.shape[:2]
    x = jnp.pad(x, ((0, 0), (0, n_sel * SEL_BLOCK - L), (0, 0), (0, 0)))
    return jnp.moveaxis(x.reshape(B, n_sel, SEL_BLOCK, NSA_G, NSA_HD), 3, 1)


def nsa_slc_attend(q, kb, vb, idx, sel_ok, q_pos):
    B, G = kb.shape[:2]
    bi = jnp.arange(B)[:, None, None, None]
    gi = jnp.arange(G)[None, :, None, None]
    kg = kb[bi, gi, idx]
    vg = vb[bi, gi, idx]
    s = jnp.einsum('bghqd,bgqnld->bghqnl', q, kg) * NSA_HD ** -0.5
    kpos = idx[..., None] * SEL_BLOCK + jnp.arange(SEL_BLOCK)
    mask = (kpos <= q_pos[None, None, :, None, None]) & sel_ok[..., None]
    sh = s.shape
    prob = masked_softmax(s.reshape(sh[:4] + (-1,)), mask.reshape(B, G, 1, sh[3], -1))
    return jnp.einsum('bghqnl,bgqnld->bghqd', prob.reshape(sh).astype(vg.dtype), vg)


def window_attend_banded(q, k, v):
    B, G, HPG, T, HD = q.shape
    nb = T // WIN_BLOCK
    npv = WINDOW // WIN_BLOCK
    pad = ((0, 0), (0, 0), (npv * WIN_BLOCK, 0), (0, 0))

    def band(z):
        zb = jnp.pad(z, pad).reshape(B, G, nb + npv, WIN_BLOCK, HD)
        return jnp.concatenate([zb[:, :, j:j + nb] for j in range(npv + 1)], axis=3)

    kb, vb = band(k), band(v)
    qb = q.reshape(B, G, HPG, nb, WIN_BLOCK, HD)
    s = jnp.einsum('bghiqd,bgikd->bghiqk', qb, kb) * NSA_HD ** -0.5
    blk = jnp.arange(nb)[:, None]
    q_pos = blk * WIN_BLOCK + jnp.arange(WIN_BLOCK)[None]
    k_pos = (blk - npv) * WIN_BLOCK + jnp.arange((npv + 1) * WIN_BLOCK)[None]
    diff = q_pos[:, :, None] - k_pos[:, None, :]
    mask = (diff >= 0) & (diff < WINDOW) & (k_pos[:, None, :] >= 0)
    prob = masked_softmax(s, mask)
    return jnp.einsum('bghiqk,bgikd->bghiqd', prob.astype(v.dtype), vb).reshape(B, G, HPG, T, HD)


def window_attend_cached(q, k, v, q_pos, k_pos):
    s = jnp.einsum('bghqd,blgd->bghql', q, k) * NSA_HD ** -0.5
    diff = q_pos[:, None] - k_pos[None, :]
    prob = masked_softmax(s, (diff >= 0) & (diff < WINDOW))
    return jnp.einsum('bghql,blgd->bghqd', prob.astype(v.dtype), v)


def retention_chunk(S, q, k, v):
    f32 = jnp.float32
    C = q.shape[1]
    lg = jnp.log1p(-jnp.exp2(-5.0 - jnp.arange(RET_H, dtype=f32)))
    i = jnp.arange(C, dtype=f32)
    diff = i[:, None] - i[None, :]
    causal = diff >= 0
    dmask = jnp.where(causal, jnp.exp(jnp.where(causal, diff, 0.0)[None] * lg[:, None, None]), 0.0)
    qf, kf, vf = q.astype(f32), k.astype(f32), v.astype(f32)
    s = jnp.einsum('bihd,bjhd->bhij', qf, kf) * dmask
    o = jnp.einsum('bhij,bjhe->bihe', s, vf)
    o = o + jnp.einsum('bihd,bhde->bihe', qf, S) * jnp.exp((i[:, None] + 1.0) * lg[None, :])[None, :, :, None]
    k_dec = kf * jnp.exp((C - 1.0 - i)[:, None] * lg[None, :])[None, :, :, None]
    S = S * jnp.exp(C * lg)[None, :, None, None] + jnp.einsum('bjhd,bjhe->bhde', k_dec, vf)
    return S, o


def retention_prompt(q, k, v):
    B, T = q.shape[:2]
    n = T // RET_CHUNK
    xs = tuple(jnp.moveaxis(z.reshape((B, n, RET_CHUNK) + z.shape[2:]), 1, 0) for z in (q, k, v))
    s0 = jnp.zeros((B, RET_H, RET_DK, RET_DV), jnp.float32)
    S, o = lax.scan(lambda S, c: retention_chunk(S, c[0], c[1], c[2]), s0, xs)
    return S, jnp.moveaxis(o, 0, 1).reshape(B, T, RET_H, RET_DV)


def odd_output(o_cmp, o_slc, o_win, o_ret, pr, p):
    gates = pr['gates']
    B, T = gates.shape[:2]
    gg = jnp.moveaxis(gates.reshape(B, T, NSA_G, NSA_HPG, 3), 1, 3)[..., None]
    o = gg[..., 0, :] * o_cmp + gg[..., 1, :] * o_slc + gg[..., 2, :] * o_win
    o_nsa = jnp.moveaxis(o, 3, 1).reshape(B, T, NSA_W)
    y_ret = head_group_norm(o_ret, p['gn_g'], p['gn_b'], 1e-5).astype(o_nsa.dtype) * jax.nn.silu(pr['rg'])
    return jnp.concatenate([o_nsa, y_ret], axis=-1)


def odd_mixer_prompt(proj, p):
    B, T, _ = proj.shape
    pos = jnp.arange(T)
    pr = odd_project(proj, p, pos)
    qn = pr['q_n'].reshape(B, T, NSA_W)
    qr = pr['q_r'].reshape(B, T, NSA_W)
    kc = rms_norm(nsa_compress(pr['kc'], *p['ck']), p['k_norm'][0])
    vc = nsa_compress(pr['vc'], *p['cv'])
    n_cmp = kc.shape[1]
    n_sel = -(-T // SEL_BLOCK)
    o_cmp, sel = nsa_cmp_select(qn, _tile_cmp(kc), _tile_cmp(vc), _overlap_T(n_cmp, n_sel),
                                n_cmp=n_cmp, n_sel=n_sel, q_pos0=0)
    o_slc = nsa_flash(qr, _tile_groups(pr['ks']), _tile_groups(pr['vs']), sel, _sel_expand(T))
    o_win = nsa_flash(qr, _tile_groups(pr['kw']), _tile_groups(pr['vw']))
    S, o_ret = retention_prompt_pallas(pr['rq'].reshape(B, T, -1), pr['rk'].reshape(B, T, -1),
                                       pr['rv'].reshape(B, T, -1))
    o_ret = o_ret.reshape(B, T, RET_H, RET_DV)
    gates = pr['gates']
    heads = (B, T, NSA_H, NSA_HD)
    o_nsa = (gates[..., 0:1] * o_cmp.reshape(heads) + gates[..., 1:2] * o_slc.reshape(heads)
             + gates[..., 2:3] * o_win.reshape(heads)).reshape(B, T, NSA_W)
    y_ret = head_group_norm(o_ret, p['gn_g'], p['gn_b'], 1e-5).astype(o_nsa.dtype) * jax.nn.silu(pr['rg'])
    out = jnp.concatenate([o_nsa, y_ret], axis=-1)
    kv_rows = jnp.stack([pr['kc'], pr['vc'], pr['ks'], pr['vs']], axis=2)
    win = jnp.stack([pr['kw'], pr['vw']], axis=2)[:, T - min(WINDOW, T):]
    return out, kv_rows, win, S


def odd_mixer_sample(proj, p, cache_layer, page_table, win_buf, ret_s0):
    B, T, _ = proj.shape
    assert T == DEC_SEQ == 1 and win_buf.shape[1] == WIN_BUF
    pos = PAST_LEN + jnp.arange(T)
    pr = odd_project(proj, p, pos)
    scale = NSA_HD ** -0.5
    new_rows = jnp.stack([pr['ks'], pr['vs'], pr['kw'], pr['vw']], axis=2)[:, 0].reshape(B, 4, SLOT_ROWS)
    cache_t = jnp.transpose(cache_layer, (0, 2, 3, 4, 1)).reshape(cache_layer.shape[0], KV_ROWS, PAGE_SIZE)
    win_t = jnp.transpose(win_buf, (0, 2, 3, 4, 1)).reshape(B, 2 * SLOT_ROWS, WIN_BUF)
    wk = _dec_cmp_weights(*p['ck'])
    wv = _dec_cmp_weights(*p['cv'])
    w1t, b1, w2t, b2t = (jnp.stack([a, b]) for a, b in zip(wk, wv))
    kn = jnp.tile(p['k_norm'][0], NSA_G).reshape(1, SLOT_ROWS)
    t = np.arange(PAST_LEN)
    expand = jnp.asarray(np.arange(CMP_PAD)[:, None] == (t // SEL_BLOCK)[None, :], dtype=BF16)
    h = np.arange(NSA_H)
    grp = jnp.asarray((h[:, None] // NSA_HPG) == (h[None, :] // NSA_HPG), dtype=BF16)
    o16 = dec_nsa(page_table, cache_t, win_t, _place_heads(pr['q_n'][:, 0] * scale),
                  _place_heads(pr['q_r'][:, 0] * scale), new_rows, pr['gates'][:, 0],
                  w1t, b1, w2t, b2t, kn, _overlap_T(DEC_N_CMP, DEC_N_SEL), expand, grp)
    o_nsa = _take_heads(o16)[:, None, :]
    S, o_ret = retention_chunk(ret_s0.astype(jnp.float32), pr['rq'], pr['rk'], pr['rv'])
    y_ret = head_group_norm(o_ret, p['gn_g'], p['gn_b'], 1e-5).astype(o_nsa.dtype) * jax.nn.silu(pr['rg'])
    out = jnp.concatenate([o_nsa, y_ret], axis=-1)
    rows = jnp.stack([pr['kc'], pr['vc'], pr['ks'], pr['vs']], axis=2).astype(cache_layer.dtype)
    new_col = jnp.stack([pr['kw'], pr['vw']], axis=2)[:, 0].reshape(B, 2 * SLOT_ROWS, 1).astype(win_buf.dtype)
    win_new = jnp.concatenate([win_t[:, :, T:], new_col], axis=2).reshape(B, 2, NSA_G, NSA_HD, WIN_BUF)
    return out, rows, jnp.transpose(win_new, (0, 4, 1, 2, 3)), S


def odd_mixer(proj, p, B, T, DB, cache_layer, page_table, win_buf, ret_s0):
    assert DEC_SEQ == 1 and win_buf.shape[1] == WIN_BUF
    n_p = B * T
    kv = (B, T, NSA_G, NSA_HD)
    cols = lambda rows, off, w: proj[rows, off:off + w]
    prompt = slice(0, n_p)
    dec = slice(n_p, n_p + DB)
    qn, qr, ks, kw, ks4, vs4, kw4, vw4, gates, rq, rk, kv_t, win_t_p = odd_pre(
        proj, jnp.tile(jnp.arange(T), B), p, 0, n_p, False, T)
    seq = lambda z: z.reshape(B, T, -1)
    kc_raw = cols(prompt, OFF_KC, KV_W).reshape(kv)
    vc_raw = cols(prompt, OFF_VC, KV_W).reshape(kv)
    kc = rms_norm(nsa_compress(kc_raw, *p['ck']), p['k_norm'][0])
    vc = nsa_compress(vc_raw, *p['cv'])
    n_cmp = kc.shape[1]
    n_sel = -(-T // SEL_BLOCK)
    o_cmp, sel = nsa_cmp_select(seq(qn), _tile_cmp(kc), _tile_cmp(vc), _overlap_T(n_cmp, n_sel),
                                n_cmp=n_cmp, n_sel=n_sel, q_pos0=0)
    o_slc = nsa_flash(seq(qr), seq(ks4), seq(vs4), sel, _sel_expand(T))
    o_win = nsa_flash(seq(qr), seq(kw4), seq(vw4))
    ret_p, o_ret = retention_prompt_pallas(rq, rk, proj, B, T, v_col0=OFF_RV)
    flat = lambda z: z.reshape(n_p, -1)
    cat = odd_post(flat(o_cmp), flat(o_slc), flat(o_win), gates, o_ret, proj, p, n_p + DB, 0)
    n_win = min(WINDOW, T)
    kv_rows_p = jnp.transpose(kv_t.reshape(B, KV_SLOTS, NSA_G, NSA_HD, T), (0, 4, 1, 2, 3))
    win_p = jnp.transpose(win_t_p[:, :, T - n_win:].reshape(B, 2, NSA_G, NSA_HD, n_win), (0, 4, 1, 2, 3))
    qn, qr, ks, kw, _, _, _, _, gates, rq, rk, kv_t, _ = odd_pre(proj, jnp.asarray(PAST_LEN), p, n_p, DB, True, DB)
    scale = NSA_HD ** -0.5
    heads = lambda z: z.reshape(DB, NSA_H, NSA_HD)
    vs, vw = cols(dec, OFF_VS, KV_W), cols(dec, OFF_VW, KV_W)
    new_rows = jnp.stack([ks, vs, kw, vw], axis=1)
    cache_t = jnp.transpose(cache_layer, (0, 2, 3, 4, 1)).reshape(cache_layer.shape[0], KV_ROWS, PAGE_SIZE)
    win_t = jnp.transpose(win_buf, (0, 2, 3, 4, 1)).reshape(DB, 2 * SLOT_ROWS, WIN_BUF)
    w1t, b1, w2t, b2t = (jnp.stack([a, b]) for a, b in zip(_dec_cmp_weights(*p['ck']), _dec_cmp_weights(*p['cv'])))
    kn = jnp.tile(p['k_norm'][0], NSA_G).reshape(1, SLOT_ROWS)
    t = np.arange(PAST_LEN)
    expand = jnp.asarray(np.arange(CMP_PAD)[:, None] == (t // SEL_BLOCK)[None, :], dtype=BF16)
    h = np.arange(NSA_H)
    grp = jnp.asarray((h[:, None] // NSA_HPG) == (h[None, :] // NSA_HPG), dtype=BF16)
    o16 = dec_nsa(page_table, cache_t, win_t, _place_heads(heads(qn) * scale), _place_heads(heads(qr) * scale),
                  new_rows, gates[:, :3 * NSA_H].reshape(DB, NSA_H, 3),
                  w1t, b1, w2t, b2t, kn, _overlap_T(DEC_N_CMP, DEC_N_SEL), expand, grp)
    o_nsa = _take_heads(o16)
    ret_s, o_ret = retention_chunk(ret_s0.astype(F32), rq.reshape(DB, 1, RET_H, RET_DK),
                                   rk.reshape(DB, 1, RET_H, RET_DK), cols(dec, OFF_RV, RET_W).reshape(DB, 1, RET_H, RET_DV))
    cat = odd_post(o_nsa, o_nsa, o_nsa, gates, o_ret.reshape(DB, RET_W), proj, p, n_p + DB, n_p, prior=cat, gated=True)
    rows_s = jnp.transpose(kv_t.reshape(DEC_SEQ, KV_SLOTS, NSA_G, NSA_HD, DB), (4, 0, 1, 2, 3)).astype(cache_layer.dtype)
    new_row = jnp.concatenate([kw, vw], axis=1)[:, None, :].astype(win_buf.dtype)
    win_new = win_shift(win_t, new_row).reshape(DB, 2, NSA_G, NSA_HD, WIN_BUF)
    win_s = jnp.transpose(win_new, (0, 4, 1, 2, 3))
    return cat, (kv_rows_p, win_p, ret_p), (rows_s, win_s, ret_s)


def _stack(xs, dt):
    return jnp.stack(xs).astype(dt)


def kernel(x_prompt, x_sample, state_lru_h, state_lru_conv, state_rwkv_shift, state_rwkv_wkv,
           cache_nsa_kv, cache_nsa_win, state_ret, page_table,
           norm_ffn1, ffn1_w_in, ffn1_w_out, norm_mix, norm_ffn2, ffn2_w_in, ffn2_w_out,
           ab_w_in, lru_conv_w, lru_conv_b, lru_wa, lru_ba, lru_wx, lru_bx, lru_lambda,
           rwkv_mu, rwkv_w0, rwkv_w2, rwkv_a0, rwkv_a2, rwkv_g2, rwkv_k_k, rwkv_k_a, rwkv_r_k,
           rwkv_ln_g, rwkv_ln_b, ab_w_out,
           cd_w_in, nsa_q_norm, nsa_k_norm, cmp_k_w1, cmp_k_b1, cmp_k_w2, cmp_k_b2,
           cmp_v_w1, cmp_v_b1, cmp_v_w2, cmp_v_b2, ret_gn_g, ret_gn_b, cd_w_out):
    dt = x_prompt.dtype
    B = x_prompt.shape[0]
    DB = x_sample.shape[0]
    y = jnp.concatenate([x_prompt.reshape(N_PROMPT, D_MODEL), x_sample.reshape(DB * DEC_SEQ, D_MODEL)], axis=0)
    lru_h_p, lru_h_s, lru_c_p, lru_c_s, sh_p, sh_s, wkv_p, wkv_s = [], [], [], [], [], [], [], []
    kv_p, kv_s, win_p, win_s, ret_p, ret_s = [], [], [], [], [], []
    for layer in range(DEPTH):
        li = layer // 2
        y = ffn_block(y, norm_ffn1[layer], *_prep_ffn_weights(ffn1_w_in, ffn1_w_out, layer))
        if layer % 2 == 0:
            p = {'conv_w': lru_conv_w[li], 'conv_b': lru_conv_b[li],
                 'wa': lru_wa[li], 'ba': lru_ba[li], 'wx': lru_wx[li], 'bx': lru_bx[li], 'lam': lru_lambda[li],
                 'mu': rwkv_mu[li], 'w0': rwkv_w0[li], 'w2': rwkv_w2[li], 'a0': rwkv_a0[li], 'a2': rwkv_a2[li],
                 'g2': rwkv_g2[li], 'k_k': rwkv_k_k[li], 'k_a': rwkv_k_a[li], 'r_k': rwkv_r_k[li],
                 'ln_g': rwkv_ln_g[li], 'ln_b': rwkv_ln_b[li]}
            proj = norm_matmul(y, norm_mix[layer], _prep_cols(ab_w_in[li], WIDE_COL_TILE), tn=WIDE_COL_TILE)
            cat, (a0, a1, a2, a3), (b0, b1, b2, b3) = even_mixer(
                proj, p, B, SEQ, DB, state_lru_h[li], state_lru_conv[li], state_rwkv_shift[li], state_rwkv_wkv[li])
            lru_h_p.append(a0); lru_c_p.append(a1); sh_p.append(a2); wkv_p.append(a3)
            lru_h_s.append(b0); lru_c_s.append(b1); sh_s.append(b2); wkv_s.append(b3)
            w_out = ab_w_out[li]
        else:
            p = {'q_norm': nsa_q_norm[li], 'k_norm': nsa_k_norm[li],
                 'ck': (cmp_k_w1[li], cmp_k_b1[li], cmp_k_w2[li], cmp_k_b2[li]),
                 'cv': (cmp_v_w1[li], cmp_v_b1[li], cmp_v_w2[li], cmp_v_b2[li]),
                 'gn_g': ret_gn_g[li], 'gn_b': ret_gn_b[li]}
            proj = norm_matmul(y, norm_mix[layer], _odd_weight_cols(cd_w_in[li]), tn=WIDE_COL_TILE)
            cat, (a0, a1, a2), (b0, b1, b2) = odd_mixer(
                proj, p, B, SEQ, DB, cache_nsa_kv[li], page_table, cache_nsa_win[li], state_ret[li])
            kv_p.append(a0); win_p.append(a1); ret_p.append(a2)
            kv_s.append(b0); win_s.append(b1); ret_s.append(b2)
            w_out = cd_w_out[li]
        y = matmul_residual(cat, w_out.astype(BF16), y, tn=WIDE_COL_TILE)
        y = ffn_block(y, norm_ffn2[layer], *_prep_ffn_weights(ffn2_w_in, ffn2_w_out, layer))
    yp = y[:N_PROMPT].reshape(B, SEQ, D_MODEL)
    ys = y[N_PROMPT:].reshape(DB, DEC_SEQ, D_MODEL)
    return (yp, ys,
            _stack(lru_h_p, dt), _stack(lru_h_s, dt), _stack(lru_c_p, dt), _stack(lru_c_s, dt),
            _stack(sh_p, dt), _stack(sh_s, dt), _stack(wkv_p, dt), _stack(wkv_s, dt),
            _stack(kv_p, dt), _stack(kv_s, dt), _stack(win_p, dt), _stack(win_s, dt),
            _stack(ret_p, dt), _stack(ret_s, dt))
```

```python
import functools

import jax
import jax.numpy as jnp
import numpy as np
from jax import lax
from jax.experimental import pallas as pl
from jax.experimental.pallas import tpu as pltpu

D_MODEL = 2048
BATCH = 4
SEQ = 2048
DEPTH = 2
DEC_BATCH = 128
DEC_SEQ = 1
PAST_LEN = 2048
PAGE_SIZE = 128
D_FF = 5504
LRU_W = D_MODEL // 2
LRU_BLOCKS = 16
LRU_BS = LRU_W // LRU_BLOCKS
CONV_W = 4
LRU_C = 8.0
RWKV_W = D_MODEL // 2
RWKV_HD = 64
RWKV_H = RWKV_W // RWKV_HD
W_LORA = 64
A_LORA = 64
G_LORA = 160
SHIFT_W = 3 * RWKV_W + W_LORA + A_LORA + G_LORA
AB_COLS = 2 * LRU_W + SHIFT_W
NSA_H = 16
NSA_G = 4
NSA_HPG = NSA_H // NSA_G
NSA_HD = 64
NSA_W = NSA_H * NSA_HD
ROPE_DIMS = NSA_HD // 4
ROPE_THETA = 500000.0
CMP_BLOCK = 32
CMP_STRIDE = 16
CMP_R = CMP_BLOCK // CMP_STRIDE
CMP_HID = 256
SEL_BLOCK = 64
SEL_TOP = 16
SEL_Q_BLOCK = 64
WINDOW = 512
WIN_BLOCK = 128
FORCE_SCORE = 1e4
KV_SLOTS = 4
RET_H = 8
RET_DK = 64
RET_DV = 128
RET_W = RET_H * RET_DV
RET_CHUNK = 128
RET_THETA = 10000.0
CD_COLS = NSA_W + 6 * NSA_G * NSA_HD + 3 * NSA_H + 2 * RET_H * RET_DK + 2 * RET_W

N_TOK = BATCH * SEQ + DEC_BATCH * DEC_SEQ
N_PROMPT = BATCH * SEQ

LANE = 128
VMEM_LIMIT_BYTES = 56 * 1024 * 1024
ROW_TILE = 640
FF_TILE = 512
D_FF_PAD = 5632
COL_TILE = 512
WIDE_COL_TILE = 2048

BF16 = jnp.bfloat16
F32 = jnp.float32


def _round_up(n, m):
    return -(-n // m) * m


def _rms_rows(x, g):
    ms = jnp.mean(x * x, axis=-1, keepdims=True)
    return x * lax.rsqrt(ms + 1e-6) * g


def _ffn_kernel(x_ref, g_ref, wg_ref, wu_ref, wo_ref, o_ref, xn_ref, acc_ref):
    k = pl.program_id(1)

    @pl.when(k == 0)
    def _():
        xn_ref[...] = _rms_rows(x_ref[...], g_ref[...]).astype(BF16)
        acc_ref[...] = jnp.zeros_like(acc_ref)

    xn = xn_ref[...]
    gate = jnp.dot(xn, wg_ref[...], preferred_element_type=F32)
    up = jnp.dot(xn, wu_ref[...], preferred_element_type=F32)
    act = gate * jax.nn.sigmoid(gate) * up
    acc_ref[...] += jnp.dot(act.astype(BF16), wo_ref[...], preferred_element_type=F32)

    @pl.when(k == pl.num_programs(1) - 1)
    def _():
        o_ref[...] = x_ref[...] + 0.5 * acc_ref[...]


def ffn_block(x, g, wg, wu, wo):
    m, d = x.shape
    return pl.pallas_call(
        _ffn_kernel,
        grid=(m // ROW_TILE, D_FF_PAD // FF_TILE),
        in_specs=[
            pl.BlockSpec((ROW_TILE, d), lambda i, k: (i, 0)),
            pl.BlockSpec((1, d), lambda i, k: (0, 0)),
            pl.BlockSpec((d, FF_TILE), lambda i, k: (0, k)),
            pl.BlockSpec((d, FF_TILE), lambda i, k: (0, k)),
            pl.BlockSpec((FF_TILE, d), lambda i, k: (k, 0)),
        ],
        out_specs=pl.BlockSpec((ROW_TILE, d), lambda i, k: (i, 0)),
        out_shape=jax.ShapeDtypeStruct((m, d), F32),
        scratch_shapes=[pltpu.VMEM((ROW_TILE, d), BF16), pltpu.VMEM((ROW_TILE, d), F32)],
        compiler_params=pltpu.CompilerParams(
            dimension_semantics=("parallel", "arbitrary"), vmem_limit_bytes=VMEM_LIMIT_BYTES),
        name="ffn_block",
    )(x, g.reshape(1, d), wg, wu, wo)


def _norm_matmul_kernel(x_ref, g_ref, w_ref, o_ref, xn_ref):
    @pl.when(pl.program_id(1) == 0)
    def _():
        xn_ref[...] = _rms_rows(x_ref[...], g_ref[...]).astype(BF16)

    o_ref[...] = jnp.dot(xn_ref[...], w_ref[...], preferred_element_type=F32)


def norm_matmul(x, g, w, tn=COL_TILE):
    m, k = x.shape
    n = w.shape[1]
    return pl.pallas_call(
        _norm_matmul_kernel,
        grid=(m // ROW_TILE, n // tn),
        in_specs=[
            pl.BlockSpec((ROW_TILE, k), lambda i, j: (i, 0)),
            pl.BlockSpec((1, k), lambda i, j: (0, 0)),
            pl.BlockSpec((k, tn), lambda i, j: (0, j)),
        ],
        out_specs=pl.BlockSpec((ROW_TILE, tn), lambda i, j: (i, j)),
        out_shape=jax.ShapeDtypeStruct((m, n), F32),
        scratch_shapes=[pltpu.VMEM((ROW_TILE, k), BF16)],
        compiler_params=pltpu.CompilerParams(
            dimension_semantics=("parallel", "arbitrary"), vmem_limit_bytes=VMEM_LIMIT_BYTES),
        name="norm_matmul",
    )(x, g.reshape(1, k), w)


def _matmul_residual_kernel(a_ref, w_ref, r_ref, o_ref):
    o_ref[...] = r_ref[...] + jnp.dot(a_ref[...].astype(BF16), w_ref[...], preferred_element_type=F32)


def matmul_residual(a, w, res, tn=COL_TILE):
    m, k = a.shape
    n = w.shape[1]
    return pl.pallas_call(
        _matmul_residual_kernel,
        grid=(m // ROW_TILE, n // tn),
        in_specs=[
            pl.BlockSpec((ROW_TILE, k), lambda i, j: (i, 0)),
            pl.BlockSpec((k, tn), lambda i, j: (0, j)),
            pl.BlockSpec((ROW_TILE, tn), lambda i, j: (i, j)),
        ],
        out_specs=pl.BlockSpec((ROW_TILE, tn), lambda i, j: (i, j)),
        out_shape=jax.ShapeDtypeStruct((m, n), F32),
        compiler_params=pltpu.CompilerParams(
            dimension_semantics=("parallel", "arbitrary"), vmem_limit_bytes=VMEM_LIMIT_BYTES),
        name="matmul_residual",
    )(a, w, res)


WCAST_ROWS = 256
WCAST_COLS = 512


def _cast_w_in_kernel(w_ref, wg_ref, wu_ref):
    pad = jnp.zeros((w_ref.shape[0], D_FF_PAD - D_FF), BF16)
    wg_ref[:, :D_FF] = w_ref[:, :D_FF].astype(BF16)
    wg_ref[:, D_FF:] = pad
    wu_ref[:, :D_FF] = w_ref[:, D_FF:].astype(BF16)
    wu_ref[:, D_FF:] = pad


def _cast_w_out_kernel(w_ref, wo_ref):
    wo_ref[:D_FF, :] = w_ref[...].astype(BF16)
    wo_ref[D_FF:, :] = jnp.zeros((D_FF_PAD - D_FF, w_ref.shape[1]), BF16)


def _prep_ffn_weights(w_in, w_out, layer):
    d = w_in.shape[1]
    wg, wu = pl.pallas_call(
        _cast_w_in_kernel,
        grid=(d // WCAST_ROWS,),
        in_specs=[pl.BlockSpec((None, WCAST_ROWS, 2 * D_FF), lambda i: (layer, i, 0))],
        out_specs=[pl.BlockSpec((WCAST_ROWS, D_FF_PAD), lambda i: (i, 0))] * 2,
        out_shape=[jax.ShapeDtypeStruct((d, D_FF_PAD), BF16)] * 2,
        compiler_params=pltpu.CompilerParams(dimension_semantics=("parallel",), vmem_limit_bytes=VMEM_LIMIT_BYTES),
        name="cast_w_in",
    )(w_in)
    wo = pl.pallas_call(
        _cast_w_out_kernel,
        grid=(d // WCAST_COLS,),
        in_specs=[pl.BlockSpec((None, D_FF, WCAST_COLS), lambda j: (layer, 0, j))],
        out_specs=pl.BlockSpec((D_FF_PAD, WCAST_COLS), lambda j: (0, j)),
        out_shape=jax.ShapeDtypeStruct((D_FF_PAD, d), BF16),
        compiler_params=pltpu.CompilerParams(dimension_semantics=("parallel",), vmem_limit_bytes=VMEM_LIMIT_BYTES),
        name="cast_w_out",
    )(w_out)
    return wg, wu, wo


def _prep_cols(w, tile):
    n = w.shape[1]
    return jnp.pad(w, ((0, 0), (0, _round_up(n, tile) - n))).astype(BF16)


SCAN_TILE = 256


def _lru_scan_kernel(a_ref, b_ref, h0_ref, o_ref, carry_ref):
    @pl.when(pl.program_id(1) == 0)
    def _():
        carry_ref[...] = h0_ref[...]

    a = a_ref[...]
    b = b_ref[...]
    rows = lax.broadcasted_iota(jnp.int32, a.shape, 0)
    k = 1
    while k < a.shape[0]:
        keep = rows >= k
        b = jnp.where(keep, a * pltpu.roll(b, k, 0) + b, b)
        a = jnp.where(keep, a * pltpu.roll(a, k, 0), a)
        k *= 2
    h = a * carry_ref[...] + b
    o_ref[...] = h
    carry_ref[...] = h[a.shape[0] - 1:, :]


def lru_scan(a, b, h0):
    B, T, W = a.shape
    tt = min(SCAN_TILE, T)
    return pl.pallas_call(
        _lru_scan_kernel,
        grid=(B, T // tt),
        in_specs=[
            pl.BlockSpec((None, tt, W), lambda i, t: (i, t, 0)),
            pl.BlockSpec((None, tt, W), lambda i, t: (i, t, 0)),
            pl.BlockSpec((None, 1, W), lambda i, t: (i, 0, 0)),
        ],
        out_specs=pl.BlockSpec((None, tt, W), lambda i, t: (i, t, 0)),
        out_shape=jax.ShapeDtypeStruct((B, T, W), F32),
        scratch_shapes=[pltpu.VMEM((1, W), F32)],
        compiler_params=pltpu.CompilerParams(
            dimension_semantics=("parallel", "arbitrary"), vmem_limit_bytes=VMEM_LIMIT_BYTES),
        name="lru_scan",
    )(a, b, h0.reshape(B, 1, W))


GROUP_W = NSA_HPG * NSA_HD
ATT_Q_TILE = 256
ATT_Q_TILE_WIN = 128
ATT_K_TILE = 256
CMP_PAD = 128
NEG_BIG = -1e30


def _stack_heads(q):
    head = lax.broadcasted_iota(jnp.int32, q.shape, 1) // NSA_HD
    return jnp.concatenate([jnp.where(head == h, q, 0.0) for h in range(NSA_HPG)], axis=0)


def _unstack_heads(o, tq):
    head = lax.broadcasted_iota(jnp.int32, (tq, GROUP_W), 1) // NSA_HD
    out = jnp.zeros((tq, GROUP_W), F32)
    for h in range(NSA_HPG):
        out = out + jnp.where(head == h, o[h * tq:(h + 1) * tq], 0.0)
    return out


def _cmp_select_kernel(q_ref, k_ref, v_ref, ov_ref, o_ref, sel_ref, *, n_cmp, n_sel, q_pos0):
    tq = q_ref.shape[0]
    i = pl.program_id(2)
    qs = _stack_heads(q_ref[...] * (NSA_HD ** -0.5)).astype(BF16)
    s = lax.dot_general(qs, k_ref[...], (((1,), (1,)), ((), ())), preferred_element_type=F32)
    q_pos = q_pos0 + i * tq + lax.broadcasted_iota(jnp.int32, (tq, CMP_PAD), 0)
    c = lax.broadcasted_iota(jnp.int32, (tq, CMP_PAD), 1)
    mask1 = (c < n_cmp) & (c * CMP_STRIDE + (CMP_BLOCK - 1) <= q_pos)
    mask = jnp.concatenate([mask1] * NSA_HPG, axis=0)
    s = jnp.where(mask, s, NEG_BIG)
    m = jnp.max(s, axis=-1, keepdims=True)
    e = jnp.where(mask, jnp.exp(s - m), 0.0)
    den = jnp.sum(e, axis=-1, keepdims=True)
    prob = e / jnp.where(den > 0, den, 1.0)
    o = jnp.dot(prob.astype(BF16), v_ref[...], preferred_element_type=F32)
    o_ref[...] = _unstack_heads(o, tq)
    psum = prob[0:tq]
    for h in range(1, NSA_HPG):
        psum = psum + prob[h * tq:(h + 1) * tq]
    imp = jnp.dot(psum.astype(BF16), ov_ref[...], preferred_element_type=F32)
    qb = q_pos // SEL_BLOCK
    valid = (c <= qb) & (c < n_sel)
    forced = (c == 0) | (c == qb) | (c == qb - 1)
    score = jnp.where(valid, jnp.where(forced, FORCE_SCORE, imp), -jnp.inf)
    k_top = min(SEL_TOP, n_sel)
    few_blocks = (q_pos0 + (i + 1) * tq - 1) // SEL_BLOCK < k_top

    @pl.when(few_blocks)
    def _():
        sel_ref[...] = jnp.where(valid, 1.0, 0.0)

    @pl.when(jnp.logical_not(few_blocks))
    def _():
        rank = jnp.zeros((tq, CMP_PAD), F32)
        for jp in range(n_sel):
            col = score[:, jp:jp + 1]
            beats = (col > score) | ((col == score) & (c > jp))
            rank = rank + jnp.where(beats, 1.0, 0.0)
        sel_ref[...] = jnp.where((rank < k_top) & (c < n_sel), 1.0, 0.0)


def nsa_cmp_select(qn, kc4, vc4, ovT, *, n_cmp, n_sel, q_pos0):
    B, T, _ = qn.shape
    tq = min(ATT_Q_TILE, T)
    return pl.pallas_call(
        functools.partial(_cmp_select_kernel, n_cmp=n_cmp, n_sel=n_sel, q_pos0=q_pos0),
        grid=(B, NSA_G, T // tq),
        in_specs=[
            pl.BlockSpec((None, tq, GROUP_W), lambda b, g, i: (b, i, g)),
            pl.BlockSpec((None, None, CMP_PAD, GROUP_W), lambda b, g, i: (b, g, 0, 0)),
            pl.BlockSpec((None, None, CMP_PAD, GROUP_W), lambda b, g, i: (b, g, 0, 0)),
            pl.BlockSpec((CMP_PAD, CMP_PAD), lambda b, g, i: (0, 0)),
        ],
        out_specs=[
            pl.BlockSpec((None, tq, GROUP_W), lambda b, g, i: (b, i, g)),
            pl.BlockSpec((None, None, tq, CMP_PAD), lambda b, g, i: (b, g, i, 0)),
        ],
        out_shape=[jax.ShapeDtypeStruct((B, T, NSA_W), F32),
                   jax.ShapeDtypeStruct((B, NSA_G, T, CMP_PAD), F32)],
        compiler_params=pltpu.CompilerParams(
            dimension_semantics=("parallel", "parallel", "parallel"), vmem_limit_bytes=VMEM_LIMIT_BYTES),
        name="nsa_cmp_select",
    )(qn, kc4, vc4, ovT)


def _flash_kernel(*refs, selected):
    if selected:
        q_ref, k_ref, v_ref, sel_ref, exp_ref, o_ref, m_ref, l_ref, acc_ref, s_a, s_b = refs
    else:
        q_ref, k_ref, v_ref, o_ref, m_ref, l_ref, acc_ref, s_a, s_b = refs
    tq = q_ref.shape[0]
    tk = ATT_K_TILE
    n_tiles = k_ref.shape[0] // tk
    i = pl.program_id(2)
    q = q_ref[...] * (NSA_HD ** -0.5)
    head = lax.broadcasted_iota(jnp.int32, q.shape, 1) // NSA_HD
    q4 = _stack_heads(q).astype(BF16)
    m_ref[...] = jnp.full(m_ref.shape, NEG_BIG, F32)
    l_ref[...] = jnp.zeros(l_ref.shape, F32)
    acc_ref[...] = jnp.zeros(acc_ref.shape, F32)
    q_pos = i * tq + lax.broadcasted_iota(jnp.int32, (tq, tk), 0)
    col = lax.broadcasted_iota(jnp.int32, (tq, tk), 1)
    if selected:
        sel = sel_ref[...].astype(BF16)
        lo = 0
    else:
        lo = jnp.maximum(i * tq - (WINDOW - 1), 0) // tk
    hi = (i * tq + tq - 1) // tk + 1

    def tile_start(j):
        return pl.multiple_of(jnp.minimum(j, n_tiles - 1) * tk, tk)

    def scores(j, s_ref):
        s_ref[...] = lax.dot_general(q4, k_ref[pl.ds(tile_start(j), tk), :], (((1,), (1,)), ((), ())),
                                     preferred_element_type=F32)

    def consume(j, s_ref):
        v = v_ref[pl.ds(tile_start(j), tk), :]
        k_pos = j * tk + col
        mask = k_pos <= q_pos
        if selected:
            mask = mask & (jnp.dot(sel, exp_ref[jnp.minimum(j, n_tiles - 1)], preferred_element_type=F32) > 0.5)
        else:
            mask = mask & (q_pos - k_pos < WINDOW)
        if not selected:
            bias = jnp.where(mask, 0.0, 2.0 * NEG_BIG)
        for h in range(NSA_HPG):
            m_old = m_ref[h]
            if selected:
                s = jnp.where(mask, s_ref[h * tq:(h + 1) * tq, :], NEG_BIG)
                m_new = jnp.maximum(m_old, jnp.max(s, axis=-1, keepdims=True))
                p = jnp.where(mask, jnp.exp(s - pltpu.repeat(m_new, tk // LANE, axis=1)), 0.0)
            else:
                s = s_ref[h * tq:(h + 1) * tq, :] + bias
                m_new = jnp.maximum(m_old, jnp.max(s, axis=-1, keepdims=True))
                p = jnp.exp(s - pltpu.repeat(m_new, tk // LANE, axis=1))
            alpha = jnp.exp(m_old - m_new)
            l_ref[h] = alpha * l_ref[h] + jnp.sum(p, axis=-1, keepdims=True)
            acc_ref[h] = (pltpu.repeat(alpha, GROUP_W // LANE, axis=1) * acc_ref[h]
                          + jnp.dot(p.astype(BF16), v, preferred_element_type=F32))
            m_ref[h] = m_new

    scores(lo, s_a)

    def body(t, carry):
        j = lo + 2 * t
        scores(j + 1, s_b)
        consume(j, s_a)
        scores(j + 2, s_a)
        consume(j + 1, s_b)
        return carry

    lax.fori_loop(0, (hi - lo + 1) // 2, body, 0)
    out = jnp.zeros((tq, GROUP_W), F32)
    for h in range(NSA_HPG):
        den = pltpu.repeat(l_ref[h], GROUP_W // LANE, axis=1)
        out = out + jnp.where(head == h, acc_ref[h] / jnp.where(den > 0, den, 1.0), 0.0)
    o_ref[...] = out


def nsa_flash(qr, k4, v4, sel=None, expand=None):
    B, T, _ = qr.shape
    selected = sel is not None
    tq = ATT_Q_TILE if selected else ATT_Q_TILE_WIN
    in_specs = [
        pl.BlockSpec((None, tq, GROUP_W), lambda b, g, i: (b, i, g)),
        pl.BlockSpec((None, T, GROUP_W), lambda b, g, i: (b, 0, g)),
        pl.BlockSpec((None, T, GROUP_W), lambda b, g, i: (b, 0, g)),
    ]
    args = [qr, k4, v4]
    if selected:
        in_specs += [
            pl.BlockSpec((None, None, tq, CMP_PAD), lambda b, g, i: (b, g, i, 0)),
            pl.BlockSpec(expand.shape, lambda b, g, i: (0, 0, 0)),
        ]
        args += [sel, expand]
    return pl.pallas_call(
        functools.partial(_flash_kernel, selected=selected),
        grid=(B, NSA_G, T // tq),
        in_specs=in_specs,
        out_specs=pl.BlockSpec((None, tq, GROUP_W), lambda b, g, i: (b, i, g)),
        out_shape=jax.ShapeDtypeStruct((B, T, NSA_W), F32),
        scratch_shapes=[pltpu.VMEM((NSA_HPG, tq, LANE), F32), pltpu.VMEM((NSA_HPG, tq, LANE), F32),
                        pltpu.VMEM((NSA_HPG, tq, GROUP_W), F32),
                        pltpu.VMEM((NSA_HPG * tq, ATT_K_TILE), F32), pltpu.VMEM((NSA_HPG * tq, ATT_K_TILE), F32)],
        compiler_params=pltpu.CompilerParams(
            dimension_semantics=("parallel", "parallel", "parallel"), vmem_limit_bytes=VMEM_LIMIT_BYTES),
        name="nsa_flash_sel" if selected else "nsa_flash_win",
    )(*args)


def _tile_groups(x):
    B, T = x.shape[:2]
    return jnp.broadcast_to(x[:, :, :, None, :], (B, T, NSA_G, NSA_HPG, NSA_HD)).reshape(B, T, NSA_W).astype(BF16)


def _tile_cmp(x):
    B, n = x.shape[:2]
    x = jnp.pad(jnp.moveaxis(x, 1, 2), ((0, 0), (0, 0), (0, CMP_PAD - n), (0, 0)))
    return jnp.tile(x, (1, 1, 1, NSA_HPG)).astype(BF16)


def _overlap_T(n_cmp, n_sel):
    ov = np.zeros((CMP_PAD, CMP_PAD), np.float32)
    cs = np.arange(n_cmp) * CMP_STRIDE
    ss = np.arange(n_sel) * SEL_BLOCK
    o = np.minimum(cs[None] + CMP_BLOCK, ss[:, None] + SEL_BLOCK) - np.maximum(cs[None], ss[:, None])
    ov[:n_cmp, :n_sel] = (np.clip(o, 0, None) / CMP_BLOCK).T
    return jnp.asarray(ov, dtype=BF16)


def _sel_expand(T):
    t = np.arange(T)
    e = (np.arange(CMP_PAD)[:, None] == (t // SEL_BLOCK)[None, :]).astype(np.float32)
    return jnp.asarray(e.reshape(CMP_PAD, T // ATT_K_TILE, ATT_K_TILE).transpose(1, 0, 2), dtype=BF16)


KV_ROWS = KV_SLOTS * NSA_G * NSA_HD
SLOT_ROWS = NSA_G * NSA_HD
N_PAGES = PAST_LEN // PAGE_SIZE
DEC_N_CHUNK = (PAST_LEN + DEC_SEQ) // CMP_STRIDE
DEC_N_CMP = DEC_N_CHUNK - CMP_R + 1
DEC_N_SEL = -(-(PAST_LEN + DEC_SEQ) // SEL_BLOCK)
WIN_BUF = min(WINDOW, PAST_LEN)


def _softmax_rows(s, mask, s_new=None):
    s = jnp.where(mask, s, NEG_BIG)
    m = jnp.max(s, axis=-1, keepdims=True)
    if s_new is not None:
        m = jnp.maximum(m, s_new)
    e = jnp.where(mask, jnp.exp(s - m), 0.0)
    den = jnp.sum(e, axis=-1, keepdims=True)
    if s_new is None:
        return e, den
    e_new = jnp.exp(s_new - m)
    return e, e_new, den + e_new


def _dec_nsa_kernel(pt_ref, *refs):
    pages = refs[:N_PAGES]
    (win_ref, qn_ref, qr_ref, new_ref, gate_ref, w1_ref, b1_ref, w2_ref, b2_ref, kn_ref,
     ov_ref, exp_ref, grp_ref, perm_ref, o_ref, xt_ref, acc_ref) = refs[N_PAGES:]
    del pt_ref
    f32 = F32
    half = 2 * NSA_HD
    n_chunk = DEC_N_CHUNK

    perm = perm_ref[...]
    per_page = PAGE_SIZE // CMP_STRIDE
    for p in range(N_PAGES):
        for sg in range(4):
            tile = pages[p][sg * half:(sg + 1) * half, :].astype(BF16)
            xt = lax.dot_general(perm, tile, (((1,), (1,)), ((), ())), preferred_element_type=f32)
            for r in range(CMP_STRIDE):
                xt_ref[sg, r, p * per_page:(p + 1) * per_page, :] = xt[r * per_page:(r + 1) * per_page, :]

    lane_lo = lax.broadcasted_iota(jnp.int32, (n_chunk, half), 1) < NSA_HD
    lane_grp = lax.broadcasted_iota(jnp.int32, (n_chunk, SLOT_ROWS), 1) // NSA_HD
    cmp_rows = []
    for slot in range(2):
        for gp in range(2):
            los, his = [], []
            for rp in range(CMP_STRIDE // 2):
                x0, x1 = (xt_ref[slot * 2 + gp, 2 * rp + j] for j in range(2))
                los.append(jnp.where(lane_lo, x0, pltpu.roll(x1, NSA_HD, 1)))
                his.append(jnp.where(lane_lo, pltpu.roll(x0, NSA_HD, 1), x1))
            lhs = jnp.concatenate([jnp.concatenate(los, axis=1), jnp.concatenate(his, axis=1)], axis=0).astype(BF16)
            acc_ref[pl.ds(gp * 2 * n_chunk, 2 * n_chunk), :] = jnp.dot(lhs, w1_ref[slot], preferred_element_type=f32)
        acc = acc_ref[...]
        pre = b1_ref[slot] + acc[:, :CMP_HID] + pltpu.roll(acc[:, CMP_HID:], NSA_G * n_chunk - 1, 0)
        out = jnp.dot(jax.nn.gelu(pre).astype(BF16), w2_ref[slot], preferred_element_type=f32) + b2_ref[slot]
        if slot == 0:
            out = _rms_rows(out, kn_ref[...])
        sel_rows = jnp.zeros((n_chunk, SLOT_ROWS), f32)
        for g in range(NSA_G):
            sel_rows = sel_rows + jnp.where(lane_grp == g, out[g * n_chunk:(g + 1) * n_chunk], 0.0)
        cmp_rows.append(sel_rows.astype(BF16))
    kc, vc = cmp_rows

    qn = qn_ref[...].astype(BF16)
    qr = qr_ref[...].astype(BF16)
    nt = (((1,), (1,)), ((), ()))
    c = lax.broadcasted_iota(jnp.int32, (NSA_H, CMP_PAD), 1)
    s = lax.dot_general(qn, kc, nt, preferred_element_type=f32)
    e, den = _softmax_rows(s, c < DEC_N_CMP)
    prob = e / jnp.where(den > 0, den, 1.0)
    o_cmp = jnp.dot(prob.astype(BF16), vc, preferred_element_type=f32)
    p_hi, p_mid = _split_bf16(prob)
    p_lo = (prob - p_hi.astype(f32) - p_mid.astype(f32)).astype(BF16)
    grp = grp_ref[...]
    psum = (jnp.dot(grp, p_hi, preferred_element_type=f32) + jnp.dot(grp, p_mid, preferred_element_type=f32)
            + jnp.dot(grp, p_lo, preferred_element_type=f32))
    imp = jnp.dot(psum.astype(BF16), ov_ref[...], preferred_element_type=f32)
    qb = (PAST_LEN + DEC_SEQ - 1) // SEL_BLOCK
    valid = c <= qb
    forced = (c == 0) | (c == qb) | (c == qb - 1)
    score = jnp.where(valid, jnp.where(forced, FORCE_SCORE, imp), -jnp.inf)
    rank = jnp.zeros((NSA_H, CMP_PAD), f32)
    for jp in range(DEC_N_SEL):
        col = score[:, jp:jp + 1]
        rank = rank + jnp.where((col > score) | ((col == score) & (c > jp)), 1.0, 0.0)
    sel = jnp.where((rank < min(SEL_TOP, DEC_N_SEL)) & (c < DEC_N_SEL), 1.0, 0.0).astype(BF16)

    new = new_ref[...]
    new_b = new.astype(BF16).astype(f32)
    qr_f = qr.astype(f32)
    s_pages = [jnp.dot(qr, pages[p][2 * SLOT_ROWS:3 * SLOT_ROWS, :].astype(BF16), preferred_element_type=f32)
               for p in range(N_PAGES)]
    s = jnp.concatenate(s_pages, axis=1)
    mask = jnp.dot(sel, exp_ref[...], preferred_element_type=f32) > 0.5
    s_new = jnp.sum(qr_f * new_b[0:1], axis=-1, keepdims=True)
    e, e_new, den = _softmax_rows(s, mask, s_new)
    e = e.astype(BF16)
    o_slc = e_new.astype(BF16).astype(f32) * new_b[1:2]
    for p in range(N_PAGES):
        o_slc = o_slc + lax.dot_general(e[:, p * PAGE_SIZE:(p + 1) * PAGE_SIZE],
                                        pages[p][3 * SLOT_ROWS:4 * SLOT_ROWS, :].astype(BF16), nt,
                                        preferred_element_type=f32)
    o_slc = o_slc / den

    s = jnp.dot(qr, win_ref[0:SLOT_ROWS, :].astype(BF16), preferred_element_type=f32)
    i_buf = lax.broadcasted_iota(jnp.int32, (NSA_H, WIN_BUF), 1)
    s_new = jnp.sum(qr_f * new_b[2:3], axis=-1, keepdims=True)
    e, e_new, den = _softmax_rows(s, WIN_BUF - i_buf < WINDOW, s_new)
    o_win = e_new.astype(BF16).astype(f32) * new_b[3:4] + lax.dot_general(
        e.astype(BF16), win_ref[SLOT_ROWS:2 * SLOT_ROWS, :].astype(BF16), nt, preferred_element_type=f32)
    o_win = o_win / den

    gates = gate_ref[...]
    o_ref[...] = gates[:, 0:1] * o_cmp + gates[:, 1:2] * o_slc + gates[:, 2:3] * o_win


def dec_nsa(page_table, cache_t, win_t, qn16, qr16, new_rows, gates, w1t, b1, w2t, b2t, kn, ovT, expand, grp):
    DB = qn16.shape[0]
    per_page = PAGE_SIZE // CMP_STRIDE
    tok = np.arange(PAGE_SIZE)
    perm = jnp.asarray((tok[:, None] // per_page == tok[None, :] % CMP_STRIDE)
                       & (tok[:, None] % per_page == tok[None, :] // CMP_STRIDE), dtype=BF16)
    const = lambda shape: pl.BlockSpec(shape, lambda b, pt: (0,) * len(shape))
    per_b = lambda shape: pl.BlockSpec((None,) + shape, lambda b, pt: (b,) + (0,) * len(shape))
    page_specs = [pl.BlockSpec((None, KV_ROWS, PAGE_SIZE), functools.partial(lambda b, pt, p: (pt[b, p], 0, 0), p=p))
                  for p in range(N_PAGES)]
    in_specs = page_specs + [
        per_b((2 * SLOT_ROWS, WIN_BUF)), per_b((NSA_H, SLOT_ROWS)), per_b((NSA_H, SLOT_ROWS)),
        per_b((4, SLOT_ROWS)), per_b((NSA_H, 3)),
        const(w1t.shape), const(b1.shape), const(w2t.shape), const(b2t.shape), const(kn.shape),
        const(ovT.shape), const(expand.shape), const(grp.shape), const(perm.shape),
    ]
    grid_spec = pltpu.PrefetchScalarGridSpec(
        num_scalar_prefetch=1, grid=(DB,), in_specs=in_specs,
        out_specs=pl.BlockSpec((None, NSA_H, SLOT_ROWS), lambda b, pt: (b, 0, 0)),
        scratch_shapes=[pltpu.VMEM((4, CMP_STRIDE, DEC_N_CHUNK, 2 * NSA_HD), F32),
                        pltpu.VMEM((NSA_G * DEC_N_CHUNK, CMP_R * CMP_HID), F32)])
    return pl.pallas_call(
        _dec_nsa_kernel,
        grid_spec=grid_spec,
        out_shape=jax.ShapeDtypeStruct((DB, NSA_H, SLOT_ROWS), F32),
        compiler_params=pltpu.CompilerParams(
            dimension_semantics=("arbitrary",), vmem_limit_bytes=VMEM_LIMIT_BYTES),
        name="dec_nsa",
    )(page_table, *([cache_t] * N_PAGES), win_t, qn16, qr16, new_rows, gates, w1t, b1, w2t, b2t, kn, ovT, expand, grp,
      perm)


WIN_SEQS_PER_STEP = 4


def _win_shift_kernel(win_ref, new_ref, o_ref):
    shape = win_ref.shape[1:]
    n = shape[1]
    row = lax.broadcasted_iota(jnp.int32, shape, 0)
    lane = lax.broadcasted_iota(jnp.int32, shape, 1)
    for s in range(win_ref.shape[0]):
        w = win_ref[s]
        col = jnp.sum(jnp.where(row == lane, jnp.broadcast_to(new_ref[s], shape), 0.0), axis=1, keepdims=True)
        o_ref[s] = jnp.where(lane == n - 1, col, pltpu.roll(w, n - 1, 1))


def win_shift(win_t, new_row):
    DB, R, W = win_t.shape
    ns = WIN_SEQS_PER_STEP
    assert R == W and DB % ns == 0
    return pl.pallas_call(
        _win_shift_kernel,
        grid=(DB // ns,),
        in_specs=[pl.BlockSpec((ns, R, W), lambda b: (b, 0, 0)), pl.BlockSpec((ns, 1, R), lambda b: (b, 0, 0))],
        out_specs=pl.BlockSpec((ns, R, W), lambda b: (b, 0, 0)),
        out_shape=jax.ShapeDtypeStruct((DB, R, W), win_t.dtype),
        compiler_params=pltpu.CompilerParams(dimension_semantics=("parallel",), vmem_limit_bytes=VMEM_LIMIT_BYTES),
        name="win_shift",
    )(win_t, new_row)


def _dec_cmp_weights(w1, b1, w2, b2):
    w = jnp.moveaxis(w1, 0, 1).reshape(CMP_STRIDE * NSA_HD, CMP_R * CMP_HID)
    return (w.astype(BF16), b1.reshape(1, CMP_HID), jnp.tile(w2, (1, NSA_G)).astype(BF16),
            jnp.tile(b2, NSA_G).reshape(1, SLOT_ROWS))


def _place_heads(q):
    own = (jnp.arange(NSA_H)[:, None] // NSA_HPG) == jnp.arange(NSA_G)[None, :]
    return jnp.where(own[None, :, :, None], q[:, :, None, :], 0.0).reshape(q.shape[0], NSA_H, SLOT_ROWS)


def _take_heads(o):
    o = o.reshape(o.shape[0], NSA_H, NSA_G, NSA_HD)
    return o[:, jnp.arange(NSA_H), jnp.arange(NSA_H) // NSA_HPG, :].reshape(o.shape[0], NSA_W)


WKV_C = 64
WKV_PAIR = 2 * RWKV_HD
WKV_T_TILE = 512
WKV_PAIRS_PER_STEP = 4


def _split_bf16(x):
    hi = x.astype(BF16)
    return hi, (x - hi.astype(F32)).astype(BF16)


def _dot3(a, b):
    a_hi, a_lo = _split_bf16(a)
    b_hi, b_lo = _split_bf16(b)
    return (jnp.dot(a_hi, b_hi, preferred_element_type=F32) + jnp.dot(a_hi, b_lo, preferred_element_type=F32)
            + jnp.dot(a_lo, b_hi, preferred_element_type=F32))


def _wkv_kernel(r_ref, lw_ref, k_ref, v_ref, a_ref, b_ref, s0_ref, y_ref, sT_ref, s_scr):
    C = WKV_C
    P = WKV_PAIR
    n_chunks = r_ref.shape[0] // C

    @pl.when(pl.program_id(2) == 0)
    def _():
        s_scr[...] = s0_ref[...]

    lo_lane = lax.broadcasted_iota(jnp.int32, (C, P), 1) < RWKV_HD
    row = lax.broadcasted_iota(jnp.int32, (2 * C, 2 * C), 0)
    col = lax.broadcasted_iota(jnp.int32, (2 * C, 2 * C), 1)
    same_head = (row // C) == (col // C)
    strict = same_head & (row > col)
    lower = same_head & (row >= col)
    eye = jnp.where(row == col, 1.0, 0.0)
    tril = jnp.where(lax.broadcasted_iota(jnp.int32, (C, C), 0) >= lax.broadcasted_iota(jnp.int32, (C, C), 1),
                     1.0, 0.0).astype(BF16)

    def stack(x):
        return jnp.concatenate([jnp.where(lo_lane, x, 0.0), jnp.where(lo_lane, 0.0, x)], axis=0)

    def chunk(c, carry):
        stages = [pair_chunk(c, q) for q in range(WKV_PAIRS_PER_STEP)]
        while stages:
            stages = [g for g in stages if next(g, True) is None]
        return carry

    def pair_chunk(c, q):
        sl = pl.ds(pl.multiple_of(c * C, C), C)
        lanes = slice(q * P, (q + 1) * P)
        r, lw, k, v, a, b = (ref[sl, lanes] for ref in (r_ref, lw_ref, k_ref, v_ref, a_ref, b_ref))
        lw_hi, lw_mid = _split_bf16(lw)
        lw_lo = (lw - lw_hi.astype(F32) - lw_mid.astype(F32)).astype(BF16)
        cs = (jnp.dot(tril, lw_hi, preferred_element_type=F32) + jnp.dot(tril, lw_mid, preferred_element_type=F32)
              + jnp.dot(tril, lw_lo, preferred_element_type=F32))
        yield
        g_inv = jnp.exp(-cs)
        g_end = jnp.exp(cs[C - 1:C, :] - cs)
        a2 = stack(a * jnp.exp(cs - lw))
        r2 = stack(r * jnp.exp(cs))
        b2 = stack(b * g_inv)
        k2 = stack(k * g_inv)
        v2 = stack(v)
        s_old = s_scr[q]
        ar = jnp.concatenate([a2, r2], axis=0).astype(BF16)
        bk = jnp.concatenate([b2, k2], axis=0).astype(BF16)
        nt = (((1,), (1,)), ((), ()))
        pp = lax.dot_general(ar, bk, nt, preferred_element_type=F32)
        from_state = lax.dot_general(ar, s_old.astype(BF16), nt, preferred_element_type=F32)
        yield
        l_ab = jnp.where(strict, pp[:2 * C, :2 * C], 0.0)
        l_ak = jnp.where(strict, pp[:2 * C, 2 * C:], 0.0)
        m_rb = jnp.where(lower, pp[2 * C:, :2 * C], 0.0)
        m_rk = jnp.where(lower, pp[2 * C:, 2 * C:], 0.0)
        v2b = v2.astype(BF16)
        rhs = from_state[:2 * C] + jnp.dot(l_ak.astype(BF16), v2b, preferred_element_type=F32)
        yield
        n = l_ab
        x = eye + n
        span = 2
        while span < C:
            n = _dot3(n, n)
            yield
            x = x + _dot3(n, x)
            yield
            span *= 2
        u2 = _dot3(x, rhs)
        yield
        uv = jnp.concatenate([u2, v2], axis=0).astype(BF16)
        y2 = from_state[2 * C:] + jnp.dot(jnp.concatenate([m_rb, m_rk], axis=1).astype(BF16), uv,
                                          preferred_element_type=F32)
        yield
        y_ref[sl, lanes] = y2[:C] + y2[C:]
        bk_end = jnp.concatenate([stack(b * g_end), stack(k * g_end)], axis=0).astype(BF16)
        s_scr[q] = s_old * jnp.exp(cs[C - 1:C, :]) + lax.dot_general(
            uv, bk_end, (((0,), (0,)), ((), ())), preferred_element_type=F32)

    lax.fori_loop(0, n_chunks, chunk, 0)

    @pl.when(pl.program_id(2) == pl.num_programs(2) - 1)
    def _():
        sT_ref[...] = s_scr[...]


def wkv7_chunked(r, lw, k, v, a, b, s0):
    B, T, W = r.shape
    n_pair = W // WKV_PAIR
    tt = min(WKV_T_TILE, T)
    s0p = s0.astype(F32).reshape(B, n_pair, 2, RWKV_HD, RWKV_HD)
    zero = jnp.zeros_like(s0p[:, :, 0])
    s0_bd = jnp.concatenate([jnp.concatenate([s0p[:, :, 0], zero], axis=-1),
                             jnp.concatenate([zero, s0p[:, :, 1]], axis=-1)], axis=-2)
    pps = WKV_PAIRS_PER_STEP
    seq = pl.BlockSpec((None, tt, pps * WKV_PAIR), lambda i, p, t: (i, t, p))
    st = pl.BlockSpec((None, pps, WKV_PAIR, WKV_PAIR), lambda i, p, t: (i, p, 0, 0))
    y, s_bd = pl.pallas_call(
        _wkv_kernel,
        grid=(B, n_pair // pps, T // tt),
        in_specs=[seq] * 6 + [st],
        out_specs=[seq, st],
        out_shape=[jax.ShapeDtypeStruct((B, T, W), F32),
                   jax.ShapeDtypeStruct((B, n_pair, WKV_PAIR, WKV_PAIR), F32)],
        scratch_shapes=[pltpu.VMEM((pps, WKV_PAIR, WKV_PAIR), F32)],
        compiler_params=pltpu.CompilerParams(
            dimension_semantics=("parallel", "parallel", "arbitrary"), vmem_limit_bytes=VMEM_LIMIT_BYTES),
        name="wkv7_chunked",
    )(r, lw, k, v, a, b, s0_bd)
    s_fin = jnp.stack([s_bd[:, :, :RWKV_HD, :RWKV_HD], s_bd[:, :, RWKV_HD:, RWKV_HD:]], axis=2)
    return y, s_fin.reshape(B, W // RWKV_HD, RWKV_HD, RWKV_HD)


AB_PAD = _round_up(AB_COLS, WIDE_COL_TILE)
SHIFT_PAD = _round_up(SHIFT_W, LANE)
LORA_PAD = SHIFT_PAD - 3 * RWKV_W
EVEN_ROWS = 128
POST_ROWS = 256
N_EVEN_PRE_OUT = 10


def _split3(x):
    hi = x.astype(BF16)
    r1 = x - hi.astype(F32)
    mid = r1.astype(BF16)
    return hi, mid, (r1 - mid.astype(F32)).astype(BF16)


def _dot_01(x, m):
    return sum(jnp.dot(part, m, preferred_element_type=F32) for part in _split3(x))


def _head_sum(x, red_ref, exp_ref):
    return _dot_01(_dot_01(x, red_ref[...]), exp_ref[...])


def _expm1(x):
    u = jnp.exp(x)
    d = u - 1.0
    log_u = jnp.where((d == 0.0) | (d == -1.0), 1.0, jnp.log(u))
    return jnp.where(d == 0.0, x, jnp.where(d == -1.0, -1.0, d * x / log_u))


def _even_pre_math(x_ref, prev, taps, prm, outs):
    (cw_ref, cb_ref, wa_ref, ba_ref, wx_ref, bx_ref, lam_ref, mu_ref, w0_ref, a0_ref, wl_ref,
     kk_ref, ka_ref, red_ref, exp_ref) = prm
    a_o, u_o, gate_o, r_o, lw_o, k_o, v_o, na_o, nb_o, g_o = outs
    t1, t2, t3 = taps
    xb = x_ref[:, 0:LRU_W]
    xc = cb_ref[...] + cw_ref[0:1] * t3 + cw_ref[1:2] * t2 + cw_ref[2:3] * t1 + cw_ref[3:4] * xb
    xcb = xc.astype(BF16)
    gate_r = jax.nn.sigmoid(jnp.dot(xcb, wa_ref[...], preferred_element_type=F32) + ba_ref[...])
    gate_i = jax.nn.sigmoid(jnp.dot(xcb, wx_ref[...], preferred_element_type=F32) + bx_ref[...])
    log_a = -LRU_C * gate_r * lam_ref[...]
    a_o[...] = jnp.exp(log_a)
    u_o[...] = jnp.sqrt(-_expm1(2.0 * log_a)) * (gate_i * xc)
    gate_o[...] = jax.nn.gelu(x_ref[:, LRU_W:2 * LRU_W])
    rw = x_ref[:, 2 * LRU_W:2 * LRU_W + SHIFT_PAD]
    rs = rw + mu_ref[...] * (prev - rw)
    r_o[...] = rs[:, 0:RWKV_W]
    k = rs[:, RWKV_W:2 * RWKV_W]
    v_o[...] = rs[:, 2 * RWKV_W:3 * RWKV_W]
    tail = rs[:, 3 * RWKV_W:]
    lane = lax.broadcasted_iota(jnp.int32, tail.shape, 1)
    act = jnp.where(lane < W_LORA, jnp.tanh(tail), jnp.where(lane < W_LORA + A_LORA, tail, jax.nn.sigmoid(tail)))
    z = jnp.dot(act.astype(BF16), wl_ref[...], preferred_element_type=F32)
    w_log = -jax.nn.softplus(-(w0_ref[...] + z[:, 0:RWKV_W])) - 0.5
    lw_o[...] = -jnp.exp(w_log)
    a_icl = jax.nn.sigmoid(a0_ref[...] + z[:, RWKV_W:2 * RWKV_W])
    g_o[...] = z[:, 2 * RWKV_W:]
    kk = k * kk_ref[...]
    kk = kk / jnp.maximum(jnp.sqrt(_head_sum(kk * kk, red_ref, exp_ref)), 1e-12)
    k_o[...] = k * (1.0 + (a_icl - 1.0) * ka_ref[...])
    na_o[...] = -kk
    nb_o[...] = kk * a_icl


def _even_pre_seq_kernel(x_ref, conv0_ref, shift0_ref, *refs):
    prm = refs[:15]
    outs = refs[15:15 + N_EVEN_PRE_OUT]
    conv_c, shift_c = refs[15 + N_EVEN_PRE_OUT:]
    rows = x_ref.shape[0]

    @pl.when(pl.program_id(1) == 0)
    def _():
        conv_c[...] = conv0_ref[...]
        shift_c[...] = shift0_ref[...]

    xb = x_ref[:, 0:LRU_W]
    row = lax.broadcasted_iota(jnp.int32, xb.shape, 0)
    taps = []
    for j in (1, 2, 3):
        tap = pltpu.roll(xb, j, 0)
        for i in range(j):
            tap = jnp.where(row == i, conv_c[8 - j + i:9 - j + i, :], tap)
        taps.append(tap)
    rw = x_ref[:, 2 * LRU_W:2 * LRU_W + SHIFT_PAD]
    row_w = lax.broadcasted_iota(jnp.int32, rw.shape, 0)
    prev = jnp.where(row_w == 0, shift_c[7:8, :], pltpu.roll(rw, 1, 0))
    _even_pre_math(x_ref, prev, taps, prm, outs)
    conv_c[...] = x_ref[rows - 8:rows, 0:LRU_W]
    shift_c[...] = x_ref[rows - 8:rows, 2 * LRU_W:2 * LRU_W + SHIFT_PAD]


def _even_pre_step_kernel(x_ref, prev_ref, t1_ref, t2_ref, t3_ref, *refs):
    _even_pre_math(x_ref, prev_ref[...], (t1_ref[...], t2_ref[...], t3_ref[...]), refs[:15], refs[15:])


def _even_params(p):
    def bd(w):
        eye = jnp.eye(LRU_BLOCKS, dtype=w.dtype)
        return (eye[:, None, :, None] * w[:, :, None, :]).reshape(LRU_W, LRU_W).astype(BF16)
    row = lambda v: v.reshape(1, -1).astype(F32)
    wl = jnp.zeros((LORA_PAD, 3 * RWKV_W), F32)
    wl = wl.at[0:W_LORA, 0:RWKV_W].set(p['w2'])
    wl = wl.at[W_LORA:W_LORA + A_LORA, RWKV_W:2 * RWKV_W].set(p['a2'])
    wl = wl.at[W_LORA + A_LORA:W_LORA + A_LORA + G_LORA, 2 * RWKV_W:].set(p['g2'])
    head = np.arange(RWKV_W) // RWKV_HD
    red = jnp.asarray(head[:, None] == np.arange(LANE)[None, :], dtype=BF16)
    mu = jnp.pad(p['mu'], (0, SHIFT_PAD - SHIFT_W))
    return [p['conv_w'].astype(F32), row(p['conv_b']), bd(p['wa']), row(p['ba']), bd(p['wx']), row(p['bx']),
            row(jax.nn.softplus(-p['lam'].astype(F32))), row(mu), row(p['w0']), row(p['a0']), wl.astype(BF16),
            row(p['k_k']), row(p['k_a']), red, red.T]


def _const_spec(a, n_grid):
    return pl.BlockSpec(a.shape, lambda *_: (0,) * a.ndim)


def even_pre_seq(proj, conv0, shift0, prm, B, T):
    tr = EVEN_ROWS
    nt = T // tr
    conv_pad = jnp.pad(conv0.astype(F32), ((0, 0), (8 - (CONV_W - 1), 0), (0, 0)))
    shift_pad = jnp.pad(shift0.astype(F32)[:, None, :], ((0, 0), (7, 0), (0, SHIFT_PAD - SHIFT_W)))
    out_spec = pl.BlockSpec((tr, LRU_W), lambda b, t: (b * nt + t, 0))
    return pl.pallas_call(
        _even_pre_seq_kernel,
        grid=(B, nt),
        in_specs=[pl.BlockSpec((tr, AB_PAD), lambda b, t: (b * nt + t, 0)),
                  pl.BlockSpec((None, 8, LRU_W), lambda b, t: (b, 0, 0)),
                  pl.BlockSpec((None, 8, SHIFT_PAD), lambda b, t: (b, 0, 0))] + [_const_spec(a, 2) for a in prm],
        out_specs=[out_spec] * N_EVEN_PRE_OUT,
        out_shape=[jax.ShapeDtypeStruct((B * T, LRU_W), F32)] * N_EVEN_PRE_OUT,
        scratch_shapes=[pltpu.VMEM((8, LRU_W), F32), pltpu.VMEM((8, SHIFT_PAD), F32)],
        compiler_params=pltpu.CompilerParams(
            dimension_semantics=("parallel", "arbitrary"), vmem_limit_bytes=VMEM_LIMIT_BYTES),
        name="even_pre_seq",
    )(proj, conv_pad, shift_pad, *prm)


def even_pre_step(proj, row0, conv0, shift0, prm):
    n = conv0.shape[0]
    shift_pad = jnp.pad(shift0.astype(F32), ((0, 0), (0, SHIFT_PAD - SHIFT_W)))
    taps = [conv0[:, CONV_W - 1 - j].astype(F32) for j in (1, 2, 3)]
    full = lambda w: pl.BlockSpec((n, w), lambda i: (0, 0))
    return pl.pallas_call(
        _even_pre_step_kernel,
        grid=(1,),
        in_specs=[pl.BlockSpec((n, AB_PAD), lambda i: (row0 // n, 0)), full(SHIFT_PAD)] + [full(LRU_W)] * 3
        + [_const_spec(a, 1) for a in prm],
        out_specs=[full(LRU_W)] * N_EVEN_PRE_OUT,
        out_shape=[jax.ShapeDtypeStruct((n, LRU_W), F32)] * N_EVEN_PRE_OUT,
        compiler_params=pltpu.CompilerParams(
            dimension_semantics=("arbitrary",), vmem_limit_bytes=VMEM_LIMIT_BYTES),
        name="even_pre_step",
    )(proj, shift_pad, *taps, *prm)


def _even_post_kernel(hs_ref, gate_ref, y_ref, r_ref, k_ref, v_ref, g_ref, lng_ref, lnb_ref, rk_ref,
                      red_ref, exp_ref, *rest):
    o_ref = rest[-1]
    y = y_ref[...]
    mu = _head_sum(y, red_ref, exp_ref) * (1.0 / RWKV_HD)
    d = y - mu
    var = _head_sum(d * d, red_ref, exp_ref) * (1.0 / RWKV_HD)
    yn = d * lax.rsqrt(var + 64e-5) * lng_ref[...] + lnb_ref[...]
    bonus = _head_sum(r_ref[...] * k_ref[...] * rk_ref[...], red_ref, exp_ref) * v_ref[...]
    o_ref[:, 0:LRU_W] = (hs_ref[...] * gate_ref[...]).astype(o_ref.dtype)
    o_ref[:, LRU_W:] = ((yn + bonus) * g_ref[...]).astype(o_ref.dtype)


def even_post(hs, gate, y, r, k, v, g, p, red, n_total, row0, prior=None):
    n = hs.shape[0]
    tr = min(POST_ROWS, n)
    row = lambda a: a.reshape(1, -1).astype(F32)
    consts = [row(p['ln_g']), row(p['ln_b']), row(p['r_k']), red, red.T]
    seq = pl.BlockSpec((tr, LRU_W), lambda i: (i, 0))
    args = [hs, gate, y, r, k, v, g] + consts
    in_specs = [seq] * 7 + [_const_spec(a, 1) for a in consts]
    aliases = {}
    if prior is not None:
        args.append(prior)
        in_specs.append(pl.BlockSpec(memory_space=pl.ANY))
        aliases = {len(args) - 1: 0}
    return pl.pallas_call(
        _even_post_kernel,
        grid=(n // tr,),
        in_specs=in_specs,
        out_specs=pl.BlockSpec((tr, D_MODEL), lambda i: (row0 // tr + i, 0)),
        out_shape=jax.ShapeDtypeStruct((n_total, D_MODEL), BF16),
        input_output_aliases=aliases,
        compiler_params=pltpu.CompilerParams(
            dimension_semantics=("parallel",), vmem_limit_bytes=VMEM_LIMIT_BYTES),
        name="even_post",
    )(*args)


def _retention_kernel(q_ref, k_ref, va_ref, vb_ref, dm_ref, rd_ref, kd_ref, sd_ref, o_ref, s_out_ref, s_scr):
    C = q_ref.shape[0]
    n_pair = RET_H // 2

    @pl.when(pl.program_id(1) == 0)
    def _():
        s_scr[...] = jnp.zeros(s_scr.shape, F32)

    lo = lax.broadcasted_iota(jnp.int32, (C, 2 * RET_DK), 1) < RET_DK

    def stack(x):
        return jnp.concatenate([jnp.where(lo, x, 0.0), jnp.where(lo, 0.0, x)], axis=0)

    for p in range(n_pair):
        qk = slice(p * 2 * RET_DK, (p + 1) * 2 * RET_DK)
        q2 = stack(q_ref[:, qk]).astype(BF16)
        k2 = stack(k_ref[:, qk])
        v0 = p * 2 * RET_DV
        v_ref = va_ref if p < n_pair // 2 else vb_ref
        vl = v0 % (RET_W // 2)
        v2 = jnp.concatenate([v_ref[:, vl:vl + RET_DV], v_ref[:, vl + RET_DV:vl + 2 * RET_DV]],
                             axis=0).astype(BF16)
        s = lax.dot_general(q2, k2.astype(BF16), (((1,), (1,)), ((), ())), preferred_element_type=F32) * dm_ref[p]
        s_old = s_scr[p]
        o2 = jnp.dot(s.astype(BF16), v2, preferred_element_type=F32) + jnp.dot(
            q2, s_old.astype(BF16), preferred_element_type=F32) * rd_ref[p]
        o_ref[:, v0:v0 + RET_DV] = o2[:C]
        o_ref[:, v0 + RET_DV:v0 + 2 * RET_DV] = o2[C:]
        s_scr[p] = s_old * sd_ref[p] + lax.dot_general((k2 * kd_ref[p]).astype(BF16), v2, (((0,), (0,)), ((), ())),
                                                       preferred_element_type=F32)

    @pl.when(pl.program_id(1) == pl.num_programs(1) - 1)
    def _():
        s_out_ref[...] = s_scr[...]


def retention_prompt_pallas(rq, rk, rv, B, T, v_col0=0):
    C = RET_CHUNK
    nc = T // C
    f32 = F32
    lg = jnp.log1p(-jnp.exp2(-5.0 - jnp.arange(RET_H, dtype=f32))).reshape(RET_H // 2, 2)
    i = jnp.arange(C, dtype=f32)
    diff = i[:, None] - i[None, :]
    causal = diff >= 0
    dmask = jnp.where(causal, jnp.exp(jnp.where(causal, diff, 0.0)[None, None] * lg[:, :, None, None]), 0.0)
    zero = jnp.zeros_like(dmask[:, 0])
    dm = jnp.concatenate([jnp.concatenate([dmask[:, 0], zero], axis=-1),
                          jnp.concatenate([zero, dmask[:, 1]], axis=-1)], axis=-2)
    rows = lambda x, w: jnp.broadcast_to(x[:, :, :, None], x.shape + (w,)).reshape(RET_H // 2, -1, w)
    rd = rows(jnp.exp((i[None, None, :] + 1.0) * lg[:, :, None]), RET_DV)
    kd = rows(jnp.exp((C - 1.0 - i)[None, None, :] * lg[:, :, None]), 2 * RET_DK)
    sd = rows(jnp.broadcast_to(jnp.exp(C * lg)[:, :, None], (RET_H // 2, 2, RET_DK)), RET_DV)
    half_w = RET_W // 2
    qk_spec = pl.BlockSpec((C, RET_H * RET_DK), lambda b, c: (b * nc + c, 0))
    v_spec = lambda k: pl.BlockSpec((C, half_w), lambda b, c: (b * nc + c, v_col0 // half_w + k))
    const = lambda a: pl.BlockSpec(a.shape, lambda b, c: (0, 0, 0))
    o, s = pl.pallas_call(
        _retention_kernel,
        grid=(B, nc),
        in_specs=[qk_spec, qk_spec, v_spec(0), v_spec(1), const(dm), const(rd), const(kd), const(sd)],
        out_specs=[pl.BlockSpec((C, RET_W), lambda b, c: (b * nc + c, 0)),
                   pl.BlockSpec((None, RET_H // 2, 2 * RET_DK, RET_DV), lambda b, c: (b, 0, 0, 0))],
        out_shape=[jax.ShapeDtypeStruct((B * T, RET_W), f32),
                   jax.ShapeDtypeStruct((B, RET_H // 2, 2 * RET_DK, RET_DV), f32)],
        scratch_shapes=[pltpu.VMEM((RET_H // 2, 2 * RET_DK, RET_DV), f32)],
        compiler_params=pltpu.CompilerParams(
            dimension_semantics=("parallel", "arbitrary"), vmem_limit_bytes=VMEM_LIMIT_BYTES),
        name="retention_prompt",
    )(rq, rk, rv, rv, dm, rd, kd, sd)
    return s.reshape(B, RET_H, RET_DK, RET_DV), o


KV_W = NSA_G * NSA_HD
RET_QK_W = RET_H * RET_DK
OFF_Q = 0
OFF_KC = OFF_Q + NSA_W
OFF_VC = OFF_KC + KV_W
OFF_KS = OFF_VC + KV_W
OFF_VS = OFF_KS + KV_W
OFF_KW = OFF_VS + KV_W
OFF_VW = OFF_KW + KV_W
OFF_RQ = OFF_VW + KV_W
OFF_RK = OFF_RQ + RET_QK_W
OFF_RV = OFF_RK + RET_QK_W
OFF_RG = OFF_RV + RET_W
OFF_GT = OFF_RG + RET_W
CD_PAD = _round_up(OFF_GT + LANE, COL_TILE)
ODD_ROWS = 128
N_ODD_PRE_OUT = 11


def _odd_weight_cols(w):
    gt0 = NSA_W + 6 * KV_W
    body = jnp.concatenate([w[:, :gt0], w[:, gt0 + 3 * NSA_H:]], axis=1)
    gt = w[:, gt0:gt0 + 3 * NSA_H]
    out = jnp.concatenate([body, gt], axis=1)
    return jnp.pad(out, ((0, 0), (0, CD_PAD - out.shape[1]))).astype(BF16)


def _rope_tables(pos, n_rot, theta, head):
    half = n_rot // 2
    inv = jnp.exp(-jnp.log(jnp.float32(theta)) * jnp.arange(half, dtype=jnp.float32) / half)
    ang = pos.astype(jnp.float32)[:, None] * inv[None, :]
    cos, sin = jnp.cos(ang), jnp.sin(ang)
    d = np.arange(LANE) % head
    cos_d, sin_d = cos[:, d % half], sin[:, d % half]
    c = jnp.where(d < n_rot, cos_d, 1.0)
    s1 = jnp.where(d < half, -sin_d, 0.0)
    s2 = jnp.where((d >= half) & (d < n_rot), sin_d, 0.0)
    return jnp.stack([c, s1, s2])


def _rope_lanes(x, tab_ref, half):
    w = x.shape[1]
    rep = w // LANE
    c, s1, s2 = (pltpu.repeat(tab_ref[i], rep, axis=1) for i in range(3))
    return x * c + pltpu.roll(x, w - half, 1) * s1 + pltpu.roll(x, half, 1) * s2


def _rms_heads(x, g_ref, red_ref, exp_ref):
    ms = _head_sum(x * x, red_ref, exp_ref) * (1.0 / NSA_HD)
    return x * lax.rsqrt(ms + 1e-6) * g_ref[...]


def _odd_pre_kernel(x_ref, nsa_tab, ret_tab, qg_ref, ksg_ref, kwg_ref, redq_ref, expq_ref, redk_ref, expk_ref,
                    tile_ref, qn_o, qr_o, ks_o, kw_o, ks4_o, vs4_o, kw4_o, vw4_o, gate_o, rq_o, rk_o, kvt_o, wint_o):
    nsa_half = ROPE_DIMS // 2
    qn = _rms_heads(x_ref[:, OFF_Q:OFF_Q + NSA_W], qg_ref, redq_ref, expq_ref)
    qn_o[...] = qn
    qr_o[...] = _rope_lanes(qn, nsa_tab, nsa_half)
    ks = _rope_lanes(_rms_heads(x_ref[:, OFF_KS:OFF_KS + KV_W], ksg_ref, redk_ref, expk_ref), nsa_tab, nsa_half)
    kw = _rope_lanes(_rms_heads(x_ref[:, OFF_KW:OFF_KW + KV_W], kwg_ref, redk_ref, expk_ref), nsa_tab, nsa_half)
    ks_o[...] = ks
    kw_o[...] = kw
    tile = tile_ref[...]
    for src, dst in ((ks, ks4_o), (x_ref[:, OFF_VS:OFF_VS + KV_W], vs4_o), (kw, kw4_o),
                     (x_ref[:, OFF_VW:OFF_VW + KV_W], vw4_o)):
        dst[...] = jnp.dot(src.astype(BF16), tile, preferred_element_type=F32).astype(BF16)
    gate_o[...] = jax.nn.sigmoid(x_ref[:, OFF_GT:OFF_GT + LANE])
    rq_o[...] = _rope_lanes(x_ref[:, OFF_RQ:OFF_RQ + RET_QK_W], ret_tab, RET_DK // 2)
    rk_o[...] = _rope_lanes(x_ref[:, OFF_RK:OFF_RK + RET_QK_W], ret_tab, RET_DK // 2) * (RET_DK ** -0.5)
    kv_pieces = (x_ref[:, OFF_KC:OFF_KC + KV_W], x_ref[:, OFF_VC:OFF_VC + KV_W], ks, x_ref[:, OFF_VS:OFF_VS + KV_W])
    for dst, pieces in ((kvt_o, kv_pieces), (wint_o, (kw, x_ref[:, OFF_VW:OFF_VW + KV_W]))):
        for s, piece in enumerate(pieces):
            for c in range(KV_W // LANE):
                dst[s * KV_W + c * LANE:s * KV_W + (c + 1) * LANE, :] = piece[:, c * LANE:(c + 1) * LANE].T


def odd_pre(proj, pos, p, row0, n_rows, same_pos, seq_len):
    tr = ODD_ROWS
    blk0 = row0 // tr
    n_tab = tr if same_pos else n_rows
    pos_rows = jnp.broadcast_to(pos, (n_tab,)) if same_pos else pos
    nsa_tab = _rope_tables(pos_rows, ROPE_DIMS, ROPE_THETA, NSA_HD)
    ret_tab = _rope_tables(pos_rows, RET_DK, RET_THETA, RET_DK)
    row = lambda v, rep: jnp.tile(v.astype(F32), rep).reshape(1, -1)
    lanes = np.arange(LANE)
    red_q = jnp.asarray((np.arange(NSA_W) // NSA_HD)[:, None] == lanes[None, :], dtype=BF16)
    red_k = jnp.asarray((np.arange(KV_W) // NSA_HD)[:, None] == lanes[None, :], dtype=BF16)
    src = np.arange(KV_W)
    dst = np.arange(NSA_W)
    tile = jnp.asarray((src[:, None] // NSA_HD == dst[None, :] // GROUP_W)
                       & (src[:, None] % NSA_HD == dst[None, :] % NSA_HD), dtype=BF16)
    consts = [row(p['q_norm'], NSA_H), row(p['k_norm'][1], NSA_G), row(p['k_norm'][2], NSA_G),
              red_q, red_q.T, red_k, red_k.T, tile]
    tab_spec = pl.BlockSpec((3, tr, LANE), (lambda i: (0, 0, 0)) if same_pos else (lambda i: (0, i, 0)))
    out = lambda w, dt: (pl.BlockSpec((tr, w), lambda i: (i, 0)), jax.ShapeDtypeStruct((n_rows, w), dt))
    seq_tiles = seq_len // tr
    out_t = lambda r: (pl.BlockSpec((None, r, tr), lambda i: (i // seq_tiles, 0, i % seq_tiles)),
                       jax.ShapeDtypeStruct((n_rows // seq_len, r, seq_len), F32))
    outs = [out(NSA_W, F32), out(NSA_W, F32), out(KV_W, F32), out(KV_W, F32)] + [out(NSA_W, BF16)] * 4 + [
        out(LANE, F32), out(RET_QK_W, F32), out(RET_QK_W, F32), out_t(KV_SLOTS * KV_W), out_t(2 * KV_W)]
    return pl.pallas_call(
        _odd_pre_kernel,
        grid=(n_rows // tr,),
        in_specs=[pl.BlockSpec((tr, CD_PAD), lambda i: (blk0 + i, 0)), tab_spec, tab_spec]
        + [_const_spec(a, 1) for a in consts],
        out_specs=[o[0] for o in outs],
        out_shape=[o[1] for o in outs],
        compiler_params=pltpu.CompilerParams(
            dimension_semantics=("parallel",), vmem_limit_bytes=VMEM_LIMIT_BYTES),
        name="odd_pre",
    )(proj, nsa_tab, ret_tab, *consts)


def _odd_post_kernel(oc_ref, os_ref, ow_ref, gate_ref, ret_ref, rg0_ref, rg1_ref, gng_ref, gnb_ref, ge_ref, *rest,
                     gated):
    o_ref = rest[-1]
    if gated:
        nsa = oc_ref[...]
    else:
        gates = gate_ref[...]
        nsa = jnp.zeros(oc_ref.shape, F32)
        for j, branch in enumerate((oc_ref, os_ref, ow_ref)):
            nsa = nsa + _dot_01(gates, ge_ref[j]) * branch[...]
    o_ref[:, 0:NSA_W] = nsa.astype(o_ref.dtype)
    for h in range(RET_H):
        lanes = slice(h * RET_DV, (h + 1) * RET_DV)
        x = ret_ref[:, lanes]
        mu = jnp.mean(x, axis=-1, keepdims=True)
        d = x - mu
        var = jnp.mean(d * d, axis=-1, keepdims=True)
        yn = d * lax.rsqrt(var + 1e-5) * gng_ref[:, lanes] + gnb_ref[:, lanes]
        rg = (rg0_ref if h < RET_H // 2 else rg1_ref)[:, (h % (RET_H // 2)) * RET_DV:(h % (RET_H // 2) + 1) * RET_DV]
        o_ref[:, NSA_W + h * RET_DV:NSA_W + (h + 1) * RET_DV] = (yn * (rg * jax.nn.sigmoid(rg))).astype(o_ref.dtype)


def odd_post(o_cmp, o_slc, o_win, gates, o_ret, proj, p, n_total, row0, prior=None, gated=False):
    n = o_cmp.shape[0]
    tr = min(POST_ROWS, n)
    blk0 = row0 // tr
    h = np.arange(NSA_W) // NSA_HD
    ge = jnp.asarray(np.stack([(np.arange(LANE)[:, None] == (3 * h + j)[None, :]) for j in range(3)]), dtype=BF16)
    row = lambda a: a.reshape(1, -1).astype(F32)
    consts = [row(p['gn_g']), row(p['gn_b']), ge]
    seq = lambda w: pl.BlockSpec((tr, w), lambda i: (i, 0))
    half = RET_W // 2
    rg_spec = lambda k: pl.BlockSpec((tr, half), lambda i: (blk0 + i, OFF_RG // half + k))
    args = [o_cmp, o_slc, o_win, gates, o_ret, proj, proj] + consts
    in_specs = [seq(NSA_W)] * 3 + [seq(LANE), seq(RET_W), rg_spec(0), rg_spec(1)] + [_const_spec(a, 1) for a in consts]
    aliases = {}
    if prior is not None:
        args.append(prior)
        in_specs.append(pl.BlockSpec(memory_space=pl.ANY))
        aliases = {len(args) - 1: 0}
    return pl.pallas_call(
        functools.partial(_odd_post_kernel, gated=gated),
        grid=(n // tr,),
        in_specs=in_specs,
        out_specs=pl.BlockSpec((tr, D_MODEL), lambda i: (blk0 + i, 0)),
        out_shape=jax.ShapeDtypeStruct((n_total, D_MODEL), BF16),
        input_output_aliases=aliases,
        compiler_params=pltpu.CompilerParams(
            dimension_semantics=("parallel",), vmem_limit_bytes=VMEM_LIMIT_BYTES),
        name="odd_post",
    )(*args)


def rms_norm(x, g, eps=1e-6):
    xf = x.astype(jnp.float32)
    y = xf * lax.rsqrt(jnp.mean(xf * xf, axis=-1, keepdims=True) + eps)
    return (y * g.astype(jnp.float32)).astype(x.dtype)


def head_group_norm(y, g, b, eps):
    yf = y.astype(jnp.float32)
    mu = jnp.mean(yf, axis=-1, keepdims=True)
    var = jnp.mean(jnp.square(yf - mu), axis=-1, keepdims=True)
    yn = ((yf - mu) * lax.rsqrt(var + eps)).reshape(y.shape[:-2] + (-1,))
    return (yn * g.astype(jnp.float32) + b.astype(jnp.float32)).astype(y.dtype)


def masked_softmax(s, mask):
    s = jnp.where(mask, s.astype(jnp.float32), -jnp.inf)
    m = jnp.max(s, axis=-1, keepdims=True)
    e = jnp.exp(s - jnp.where(jnp.isfinite(m), m, 0.0))
    den = jnp.sum(e, axis=-1, keepdims=True)
    return e / jnp.where(den > 0, den, 1.0)


def rope(x, pos, n_rot, theta):
    half = n_rot // 2
    inv = jnp.exp(-jnp.log(jnp.float32(theta)) * jnp.arange(half, dtype=jnp.float32) / half)
    ang = pos.astype(jnp.float32)[:, None] * inv[None, :]
    cos = jnp.cos(ang)[None, :, None, :]
    sin = jnp.sin(ang)[None, :, None, :]
    xf = x.astype(jnp.float32)
    x1, x2 = xf[..., :half], xf[..., half:n_rot]
    out = jnp.concatenate([x1 * cos - x2 * sin, x2 * cos + x1 * sin, xf[..., n_rot:]], axis=-1)
    return out.astype(x.dtype)


def linear_scan(a, b, h0):
    b = b.at[:, 0].add(a[:, 0] * h0)

    def combine(left, right):
        return left[0] * right[0], right[0] * left[1] + right[1]

    return lax.associative_scan(combine, (a, b), axis=1)[1]


def wkv7_scan(r, w, k, v, a, b, s0):
    xs = tuple(jnp.moveaxis(z.astype(jnp.float32), 1, 0) for z in (r, w, k, v, a, b))

    def step(S, inp):
        r_t, w_t, k_t, v_t, a_t, b_t = inp
        sa = jnp.einsum('bhij,bhj->bhi', S, a_t)
        S = S * w_t[:, :, None, :] + sa[..., None] * b_t[:, :, None, :] + v_t[..., None] * k_t[:, :, None, :]
        return S, jnp.einsum('bhij,bhj->bhi', S, r_t)

    S, ys = lax.scan(step, s0.astype(jnp.float32), xs)
    return jnp.moveaxis(ys, 0, 1), S


def even_mixer_core(proj, p, lru_h0, lru_conv0, shift0, wkv0):
    B, T, _ = proj.shape
    f32 = jnp.float32
    dt = proj.dtype
    xb, gb, rw = jnp.split(proj, [LRU_W, 2 * LRU_W], axis=-1)
    xcat = jnp.concatenate([lru_conv0.astype(dt), xb], axis=1)
    xc = p['conv_b'] + sum(p['conv_w'][j] * xcat[:, j:j + T] for j in range(CONV_W))
    xbd = xc.reshape(B, T, LRU_BLOCKS, LRU_BS)
    gate_r = jax.nn.sigmoid(jnp.einsum('btnc,ncd->btnd', xbd, p['wa']).reshape(B, T, LRU_W) + p['ba'])
    gate_i = jax.nn.sigmoid(jnp.einsum('btnc,ncd->btnd', xbd, p['wx']).reshape(B, T, LRU_W) + p['bx'])
    log_a = -LRU_C * gate_r.astype(f32) * jax.nn.softplus(-p['lam'].astype(f32))
    u = jnp.sqrt(-jnp.expm1(2.0 * log_a)) * (gate_i * xc).astype(f32)
    hs = lru_scan(jnp.exp(log_a), u, lru_h0.astype(f32))
    y_lru = hs.astype(dt) * jax.nn.gelu(gb)
    prev = jnp.concatenate([shift0.astype(dt)[:, None], rw[:, :-1]], axis=1)
    rs = rw + p['mu'] * (prev - rw)
    r, k, v, xw, xa, xg = jnp.split(
        rs, [RWKV_W, 2 * RWKV_W, 3 * RWKV_W, 3 * RWKV_W + W_LORA, 3 * RWKV_W + W_LORA + A_LORA], axis=-1)
    w_log = -jax.nn.softplus(-(p['w0'] + jnp.tanh(xw) @ p['w2']).astype(f32)) - 0.5
    log_decay = -jnp.exp(w_log)
    decay = jnp.exp(log_decay)
    a_icl = jax.nn.sigmoid(p['a0'] + xa @ p['a2'])
    g = jax.nn.sigmoid(xg) @ p['g2']
    heads = (B, T, RWKV_H, RWKV_HD)
    kk = (k * p['k_k']).reshape(heads).astype(f32)
    kk = kk / jnp.maximum(jnp.sqrt(jnp.sum(kk * kk, axis=-1, keepdims=True)), 1e-12)
    k = k * (1.0 + (a_icl - 1.0) * p['k_a'])
    rh, kh, vh, ah = (z.reshape(heads) for z in (r, k, v, a_icl))
    if T % WKV_C == 0:
        y, wkv = wkv7_chunked(r.astype(f32), log_decay, k.astype(f32), v.astype(f32),
                              (-kk).reshape(B, T, RWKV_W), (kk * ah.astype(f32)).reshape(B, T, RWKV_W), wkv0)
        y = y.reshape(heads)
    else:
        y, wkv = wkv7_scan(rh, decay.reshape(heads), kh, vh, -kk, kk * ah.astype(f32), wkv0)
    y = head_group_norm(y, p['ln_g'], p['ln_b'], 64e-5).astype(dt)
    bonus = (jnp.sum(rh * kh * p['r_k'], axis=-1, keepdims=True) * vh).reshape(B, T, RWKV_W)
    y_rwkv = (y + bonus) * g
    cat = jnp.concatenate([y_lru, y_rwkv], axis=-1)
    return cat, hs[:, -1], xcat[:, T:], rw[:, -1], wkv


def even_mixer(proj, p, B, T, DB, lru_h0, lru_conv0, shift0, wkv0):
    f32 = F32
    prm = _even_params(p)
    red = prm[-2]
    n_p = B * T
    zeros = lambda *s: jnp.zeros(s, f32)
    a, u, gate, r, lw, k, v, na, nb, g = even_pre_seq(proj, zeros(B, CONV_W - 1, LRU_W), zeros(B, SHIFT_W), prm, B, T)
    seq = lambda z: z.reshape(B, T, LRU_W)
    hs = lru_scan(seq(a), seq(u), zeros(B, LRU_W))
    yw, wkv_p = wkv7_chunked(seq(r), seq(lw), seq(k), seq(v), seq(na), seq(nb), zeros(B, RWKV_H, RWKV_HD, RWKV_HD))
    cat = even_post(hs.reshape(n_p, LRU_W), gate, yw.reshape(n_p, RWKV_W), r, k, v, g, p, red, n_p + DB, 0)
    tail = lambda b, n, c0, c1: proj[(b + 1) * T - n:(b + 1) * T, c0:c1]
    st_p = (hs[:, -1], jnp.stack([tail(b, CONV_W - 1, 0, LRU_W) for b in range(B)]),
            jnp.concatenate([tail(b, 1, 2 * LRU_W, AB_COLS) for b in range(B)], axis=0), wkv_p)
    a, u, gate, r, lw, k, v, na, nb, g = even_pre_step(proj, n_p, lru_conv0, shift0, prm)
    hs_s = a * lru_h0.astype(f32) + u
    heads = (DB, 1, RWKV_H, RWKV_HD)
    yw, wkv_s = wkv7_scan(r.reshape(heads), jnp.exp(lw).reshape(heads), k.reshape(heads), v.reshape(heads),
                          na.reshape(heads), nb.reshape(heads), wkv0)
    cat = even_post(hs_s, gate, yw.reshape(DB, RWKV_W), r, k, v, g, p, red, n_p + DB, n_p, prior=cat)
    xb_s = proj[n_p:]
    conv_s = jnp.concatenate([lru_conv0[:, 1:].astype(f32), xb_s[:, None, :LRU_W]], axis=1)
    st_s = (hs_s, conv_s, xb_s[:, 2 * LRU_W:AB_COLS], wkv_s)
    return cat, st_p, st_s


def odd_project(proj, p, pos):
    B, T, _ = proj.shape
    sizes = [NSA_W] + [NSA_G * NSA_HD] * 6 + [3 * NSA_H, RET_H * RET_DK, RET_H * RET_DK, RET_W, RET_W]
    q, kc, vc, ks, vs, kw, vw, gt, rq, rk, rv, rg = jnp.split(
        proj, np.cumsum(sizes).tolist(), axis=-1)[:len(sizes)]
    kvs = (B, T, NSA_G, NSA_HD)
    q_n = rms_norm(q.reshape(B, T, NSA_H, NSA_HD), p['q_norm'])
    return {
        'q_n': q_n,
        'q_r': rope(q_n, pos, ROPE_DIMS, ROPE_THETA),
        'kc': kc.reshape(kvs), 'vc': vc.reshape(kvs),
        'ks': rope(rms_norm(ks.reshape(kvs), p['k_norm'][1]), pos, ROPE_DIMS, ROPE_THETA),
        'vs': vs.reshape(kvs),
        'kw': rope(rms_norm(kw.reshape(kvs), p['k_norm'][2]), pos, ROPE_DIMS, ROPE_THETA),
        'vw': vw.reshape(kvs),
        'gates': jax.nn.sigmoid(gt).reshape(B, T, NSA_H, 3),
        'rq': rope(rq.reshape(B, T, RET_H, RET_DK), pos, RET_DK, RET_THETA),
        'rk': rope(rk.reshape(B, T, RET_H, RET_DK), pos, RET_DK, RET_THETA) * (RET_DK ** -0.5),
        'rv': rv.reshape(B, T, RET_H, RET_DV),
        'rg': rg,
    }


def to_groups_q(q):
    B, T = q.shape[:2]
    return jnp.moveaxis(q.reshape(B, T, NSA_G, NSA_HPG, NSA_HD), 1, 3)


def to_groups_k(k):
    return jnp.moveaxis(k, 1, 2)


def nsa_compress(x, w1, b1, w2, b2):
    B, L = x.shape[:2]
    n_chunk = L // CMP_STRIDE
    n_cmp = n_chunk - CMP_R + 1
    ch = x[:, :n_chunk * CMP_STRIDE].reshape(B, n_chunk, CMP_STRIDE, NSA_G, NSA_HD)
    ch = jnp.moveaxis(ch, 3, 2).reshape(B, n_chunk, NSA_G, CMP_STRIDE * NSA_HD)
    part = jnp.einsum('bngc,rch->bngrh', ch, w1)
    pre = b1 + sum(part[:, m:m + n_cmp, :, m] for m in range(CMP_R))
    return jax.nn.gelu(pre) @ w2 + b2


def nsa_compressed_branch(qn, kc_raw, vc_raw, p, q_pos):
    kc = to_groups_k(rms_norm(nsa_compress(kc_raw, *p['ck']), p['k_norm'][0]))
    vc = to_groups_k(nsa_compress(vc_raw, *p['cv']))
    s = jnp.einsum('bghqd,bgcd->bghqc', qn, kc) * NSA_HD ** -0.5
    ends = jnp.arange(kc.shape[2]) * CMP_STRIDE + CMP_BLOCK - 1
    prob = masked_softmax(s, ends[None, :] <= q_pos[:, None])
    return jnp.einsum('bghqc,bgcd->bghqd', prob.astype(vc.dtype), vc), prob


def cmp_sel_overlap(n_cmp, n_sel):
    cs = np.arange(n_cmp) * CMP_STRIDE
    ss = np.arange(n_sel) * SEL_BLOCK
    ov = np.minimum(cs[None] + CMP_BLOCK, ss[:, None] + SEL_BLOCK) - np.maximum(cs[None], ss[:, None])
    return jnp.asarray(np.clip(ov, 0, None) / CMP_BLOCK, dtype=jnp.float32)


def nsa_select(p_cmp, q_pos, n_sel):
    imp = jnp.einsum('bgqc,sc->bgqs', p_cmp.sum(axis=2), cmp_sel_overlap(p_cmp.shape[-1], n_sel))
    j = jnp.arange(n_sel)[None, :]
    qb = (q_pos // SEL_BLOCK)[:, None]
    valid = j <= qb
    forced = (j == 0) | (j == qb) | (j == qb - 1)
    score = jnp.where(valid, jnp.where(forced, FORCE_SCORE, imp), -jnp.inf)
    _, idx = lax.top_k(score, min(SEL_TOP, n_sel))
    sel_ok = jnp.take_along_axis(jnp.broadcast_to(valid, score.shape), idx, axis=-1)
    return idx, sel_ok


def sel_blocks(x, n_sel):
    B, L = x.shape[:2]
    x = jnp.pad(x, ((0, 0), (0, n_sel * SEL_BLOCK - L), (0, 0), (0, 0)))
    return jnp.moveaxis(x.reshape(B, n_sel, SEL_BLOCK, NSA_G, NSA_HD), 3, 1)


def nsa_slc_attend(q, kb, vb, idx, sel_ok, q_pos):
    B, G = kb.shape[:2]
    bi = jnp.arange(B)[:, None, None, None]
    gi = jnp.arange(G)[None, :, None, None]
    kg = kb[bi, gi, idx]
    vg = vb[bi, gi, idx]
    s = jnp.einsum('bghqd,bgqnld->bghqnl', q, kg) * NSA_HD ** -0.5
    kpos = idx[..., None] * SEL_BLOCK + jnp.arange(SEL_BLOCK)
    mask = (kpos <= q_pos[None, None, :, None, None]) & sel_ok[..., None]
    sh = s.shape
    prob = masked_softmax(s.reshape(sh[:4] + (-1,)), mask.reshape(B, G, 1, sh[3], -1))
    return jnp.einsum('bghqnl,bgqnld->bghqd', prob.reshape(sh).astype(vg.dtype), vg)


def window_attend_banded(q, k, v):
    B, G, HPG, T, HD = q.shape
    nb = T // WIN_BLOCK
    npv = WINDOW // WIN_BLOCK
    pad = ((0, 0), (0, 0), (npv * WIN_BLOCK, 0), (0, 0))

    def band(z):
        zb = jnp.pad(z, pad).reshape(B, G, nb + npv, WIN_BLOCK, HD)
        return jnp.concatenate([zb[:, :, j:j + nb] for j in range(npv + 1)], axis=3)

    kb, vb = band(k), band(v)
    qb = q.reshape(B, G, HPG, nb, WIN_BLOCK, HD)
    s = jnp.einsum('bghiqd,bgikd->bghiqk', qb, kb) * NSA_HD ** -0.5
    blk = jnp.arange(nb)[:, None]
    q_pos = blk * WIN_BLOCK + jnp.arange(WIN_BLOCK)[None]
    k_pos = (blk - npv) * WIN_BLOCK + jnp.arange((npv + 1) * WIN_BLOCK)[None]
    diff = q_pos[:, :, None] - k_pos[:, None, :]
    mask = (diff >= 0) & (diff < WINDOW) & (k_pos[:, None, :] >= 0)
    prob = masked_softmax(s, mask)
    return jnp.einsum('bghiqk,bgikd->bghiqd', prob.astype(v.dtype), vb).reshape(B, G, HPG, T, HD)


def window_attend_cached(q, k, v, q_pos, k_pos):
    s = jnp.einsum('bghqd,blgd->bghql', q, k) * NSA_HD ** -0.5
    diff = q_pos[:, None] - k_pos[None, :]
    prob = masked_softmax(s, (diff >= 0) & (diff < WINDOW))
    return jnp.einsum('bghql,blgd->bghqd', prob.astype(v.dtype), v)


def retention_chunk(S, q, k, v):
    f32 = jnp.float32
    C = q.shape[1]
    lg = jnp.log1p(-jnp.exp2(-5.0 - jnp.arange(RET_H, dtype=f32)))
    i = jnp.arange(C, dtype=f32)
    diff = i[:, None] - i[None, :]
    causal = diff >= 0
    dmask = jnp.where(causal, jnp.exp(jnp.where(causal, diff, 0.0)[None] * lg[:, None, None]), 0.0)
    qf, kf, vf = q.astype(f32), k.astype(f32), v.astype(f32)
    s = jnp.einsum('bihd,bjhd->bhij', qf, kf) * dmask
    o = jnp.einsum('bhij,bjhe->bihe', s, vf)
    o = o + jnp.einsum('bihd,bhde->bihe', qf, S) * jnp.exp((i[:, None] + 1.0) * lg[None, :])[None, :, :, None]
    k_dec = kf * jnp.exp((C - 1.0 - i)[:, None] * lg[None, :])[None, :, :, None]
    S = S * jnp.exp(C * lg)[None, :, None, None] + jnp.einsum('bjhd,bjhe->bhde', k_dec, vf)
    return S, o


def retention_prompt(q, k, v):
    B, T = q.shape[:2]
    n = T // RET_CHUNK
    xs = tuple(jnp.moveaxis(z.reshape((B, n, RET_CHUNK) + z.shape[2:]), 1, 0) for z in (q, k, v))
    s0 = jnp.zeros((B, RET_H, RET_DK, RET_DV), jnp.float32)
    S, o = lax.scan(lambda S, c: retention_chunk(S, c[0], c[1], c[2]), s0, xs)
    return S, jnp.moveaxis(o, 0, 1).reshape(B, T, RET_H, RET_DV)


def odd_output(o_cmp, o_slc, o_win, o_ret, pr, p):
    gates = pr['gates']
    B, T = gates.shape[:2]
    gg = jnp.moveaxis(gates.reshape(B, T, NSA_G, NSA_HPG, 3), 1, 3)[..., None]
    o = gg[..., 0, :] * o_cmp + gg[..., 1, :] * o_slc + gg[..., 2, :] * o_win
    o_nsa = jnp.moveaxis(o, 3, 1).reshape(B, T, NSA_W)
    y_ret = head_group_norm(o_ret, p['gn_g'], p['gn_b'], 1e-5).astype(o_nsa.dtype) * jax.nn.silu(pr['rg'])
    return jnp.concatenate([o_nsa, y_ret], axis=-1)


def odd_mixer_prompt(proj, p):
    B, T, _ = proj.shape
    pos = jnp.arange(T)
    pr = odd_project(proj, p, pos)
    qn = pr['q_n'].reshape(B, T, NSA_W)
    qr = pr['q_r'].reshape(B, T, NSA_W)
    kc = rms_norm(nsa_compress(pr['kc'], *p['ck']), p['k_norm'][0])
    vc = nsa_compress(pr['vc'], *p['cv'])
    n_cmp = kc.shape[1]
    n_sel = -(-T // SEL_BLOCK)
    o_cmp, sel = nsa_cmp_select(qn, _tile_cmp(kc), _tile_cmp(vc), _overlap_T(n_cmp, n_sel),
                                n_cmp=n_cmp, n_sel=n_sel, q_pos0=0)
    o_slc = nsa_flash(qr, _tile_groups(pr['ks']), _tile_groups(pr['vs']), sel, _sel_expand(T))
    o_win = nsa_flash(qr, _tile_groups(pr['kw']), _tile_groups(pr['vw']))
    S, o_ret = retention_prompt_pallas(pr['rq'].reshape(B, T, -1), pr['rk'].reshape(B, T, -1),
                                       pr['rv'].reshape(B, T, -1))
    o_ret = o_ret.reshape(B, T, RET_H, RET_DV)
    gates = pr['gates']
    heads = (B, T, NSA_H, NSA_HD)
    o_nsa = (gates[..., 0:1] * o_cmp.reshape(heads) + gates[..., 1:2] * o_slc.reshape(heads)
             + gates[..., 2:3] * o_win.reshape(heads)).reshape(B, T, NSA_W)
    y_ret = head_group_norm(o_ret, p['gn_g'], p['gn_b'], 1e-5).astype(o_nsa.dtype) * jax.nn.silu(pr['rg'])
    out = jnp.concatenate([o_nsa, y_ret], axis=-1)
    kv_rows = jnp.stack([pr['kc'], pr['vc'], pr['ks'], pr['vs']], axis=2)
    win = jnp.stack([pr['kw'], pr['vw']], axis=2)[:, T - min(WINDOW, T):]
    return out, kv_rows, win, S


def odd_mixer_sample(proj, p, cache_layer, page_table, win_buf, ret_s0):
    B, T, _ = proj.shape
    assert T == DEC_SEQ == 1 and win_buf.shape[1] == WIN_BUF
    pos = PAST_LEN + jnp.arange(T)
    pr = odd_project(proj, p, pos)
    scale = NSA_HD ** -0.5
    new_rows = jnp.stack([pr['ks'], pr['vs'], pr['kw'], pr['vw']], axis=2)[:, 0].reshape(B, 4, SLOT_ROWS)
    cache_t = jnp.transpose(cache_layer, (0, 2, 3, 4, 1)).reshape(cache_layer.shape[0], KV_ROWS, PAGE_SIZE)
    win_t = jnp.transpose(win_buf, (0, 2, 3, 4, 1)).reshape(B, 2 * SLOT_ROWS, WIN_BUF)
    wk = _dec_cmp_weights(*p['ck'])
    wv = _dec_cmp_weights(*p['cv'])
    w1t, b1, w2t, b2t = (jnp.stack([a, b]) for a, b in zip(wk, wv))
    kn = jnp.tile(p['k_norm'][0], NSA_G).reshape(1, SLOT_ROWS)
    t = np.arange(PAST_LEN)
    expand = jnp.asarray(np.arange(CMP_PAD)[:, None] == (t // SEL_BLOCK)[None, :], dtype=BF16)
    h = np.arange(NSA_H)
    grp = jnp.asarray((h[:, None] // NSA_HPG) == (h[None, :] // NSA_HPG), dtype=BF16)
    o16 = dec_nsa(page_table, cache_t, win_t, _place_heads(pr['q_n'][:, 0] * scale),
                  _place_heads(pr['q_r'][:, 0] * scale), new_rows, pr['gates'][:, 0],
                  w1t, b1, w2t, b2t, kn, _overlap_T(DEC_N_CMP, DEC_N_SEL), expand, grp)
    o_nsa = _take_heads(o16)[:, None, :]
    S, o_ret = retention_chunk(ret_s0.astype(jnp.float32), pr['rq'], pr['rk'], pr['rv'])
    y_ret = head_group_norm(o_ret, p['gn_g'], p['gn_b'], 1e-5).astype(o_nsa.dtype) * jax.nn.silu(pr['rg'])
    out = jnp.concatenate([o_nsa, y_ret], axis=-1)
    rows = jnp.stack([pr['kc'], pr['vc'], pr['ks'], pr['vs']], axis=2).astype(cache_layer.dtype)
    new_col = jnp.stack([pr['kw'], pr['vw']], axis=2)[:, 0].reshape(B, 2 * SLOT_ROWS, 1).astype(win_buf.dtype)
    win_new = jnp.concatenate([win_t[:, :, T:], new_col], axis=2).reshape(B, 2, NSA_G, NSA_HD, WIN_BUF)
    return out, rows, jnp.transpose(win_new, (0, 4, 1, 2, 3)), S


def odd_mixer(proj, p, B, T, DB, cache_layer, page_table, win_buf, ret_s0):
    assert DEC_SEQ == 1 and win_buf.shape[1] == WIN_BUF
    n_p = B * T
    kv = (B, T, NSA_G, NSA_HD)
    cols = lambda rows, off, w: proj[rows, off:off + w]
    prompt = slice(0, n_p)
    dec = slice(n_p, n_p + DB)
    qn, qr, ks, kw, ks4, vs4, kw4, vw4, gates, rq, rk, kv_t, win_t_p = odd_pre(
        proj, jnp.tile(jnp.arange(T), B), p, 0, n_p, False, T)
    seq = lambda z: z.reshape(B, T, -1)
    kc_raw = cols(prompt, OFF_KC, KV_W).reshape(kv)
    vc_raw = cols(prompt, OFF_VC, KV_W).reshape(kv)
    kc = rms_norm(nsa_compress(kc_raw, *p['ck']), p['k_norm'][0])
    vc = nsa_compress(vc_raw, *p['cv'])
    n_cmp = kc.shape[1]
    n_sel = -(-T // SEL_BLOCK)
    o_cmp, sel = nsa_cmp_select(seq(qn), _tile_cmp(kc), _tile_cmp(vc), _overlap_T(n_cmp, n_sel),
                                n_cmp=n_cmp, n_sel=n_sel, q_pos0=0)
    o_slc = nsa_flash(seq(qr), seq(ks4), seq(vs4), sel, _sel_expand(T))
    o_win = nsa_flash(seq(qr), seq(kw4), seq(vw4))
    ret_p, o_ret = retention_prompt_pallas(rq, rk, proj, B, T, v_col0=OFF_RV)
    flat = lambda z: z.reshape(n_p, -1)
    cat = odd_post(flat(o_cmp), flat(o_slc), flat(o_win), gates, o_ret, proj, p, n_p + DB, 0)
    n_win = min(WINDOW, T)
    kv_rows_p = jnp.transpose(kv_t.reshape(B, KV_SLOTS, NSA_G, NSA_HD, T), (0, 4, 1, 2, 3))
    win_p = jnp.transpose(win_t_p[:, :, T - n_win:].reshape(B, 2, NSA_G, NSA_HD, n_win), (0, 4, 1, 2, 3))
    qn, qr, ks, kw, _, _, _, _, gates, rq, rk, kv_t, _ = odd_pre(proj, jnp.asarray(PAST_LEN), p, n_p, DB, True, DB)
    scale = NSA_HD ** -0.5
    heads = lambda z: z.reshape(DB, NSA_H, NSA_HD)
    vs, vw = cols(dec, OFF_VS, KV_W), cols(dec, OFF_VW, KV_W)
    new_rows = jnp.stack([ks, vs, kw, vw], axis=1)
    cache_t = jnp.transpose(cache_layer, (0, 2, 3, 4, 1)).reshape(cache_layer.shape[0], KV_ROWS, PAGE_SIZE)
    win_t = jnp.transpose(win_buf, (0, 2, 3, 4, 1)).reshape(DB, 2 * SLOT_ROWS, WIN_BUF)
    w1t, b1, w2t, b2t = (jnp.stack([a, b]) for a, b in zip(_dec_cmp_weights(*p['ck']), _dec_cmp_weights(*p['cv'])))
    kn = jnp.tile(p['k_norm'][0], NSA_G).reshape(1, SLOT_ROWS)
    t = np.arange(PAST_LEN)
    expand = jnp.asarray(np.arange(CMP_PAD)[:, None] == (t // SEL_BLOCK)[None, :], dtype=BF16)
    h = np.arange(NSA_H)
    grp = jnp.asarray((h[:, None] // NSA_HPG) == (h[None, :] // NSA_HPG), dtype=BF16)
    o16 = dec_nsa(page_table, cache_t, win_t, _place_heads(heads(qn) * scale), _place_heads(heads(qr) * scale),
                  new_rows, gates[:, :3 * NSA_H].reshape(DB, NSA_H, 3),
                  w1t, b1, w2t, b2t, kn, _overlap_T(DEC_N_CMP, DEC_N_SEL), expand, grp)
    o_nsa = _take_heads(o16)
    ret_s, o_ret = retention_chunk(ret_s0.astype(F32), rq.reshape(DB, 1, RET_H, RET_DK),
                                   rk.reshape(DB, 1, RET_H, RET_DK), cols(dec, OFF_RV, RET_W).reshape(DB, 1, RET_H, RET_DV))
    cat = odd_post(o_nsa, o_nsa, o_nsa, gates, o_ret.reshape(DB, RET_W), proj, p, n_p + DB, n_p, prior=cat, gated=True)
    rows_s = jnp.transpose(kv_t.reshape(DEC_SEQ, KV_SLOTS, NSA_G, NSA_HD, DB), (4, 0, 1, 2, 3)).astype(cache_layer.dtype)
    new_row = jnp.concatenate([kw, vw], axis=1)[:, None, :].astype(win_buf.dtype)
    win_new = win_shift(win_t, new_row).reshape(DB, 2, NSA_G, NSA_HD, WIN_BUF)
    win_s = jnp.transpose(win_new, (0, 4, 1, 2, 3))
    return cat, (kv_rows_p, win_p, ret_p), (rows_s, win_s, ret_s)


def _stack(xs, dt):
    return jnp.stack(xs).astype(dt)


def kernel(x_prompt, x_sample, state_lru_h, state_lru_conv, state_rwkv_shift, state_rwkv_wkv,
           cache_nsa_kv, cache_nsa_win, state_ret, page_table,
           norm_ffn1, ffn1_w_in, ffn1_w_out, norm_mix, norm_ffn2, ffn2_w_in, ffn2_w_out,
           ab_w_in, lru_conv_w, lru_conv_b, lru_wa, lru_ba, lru_wx, lru_bx, lru_lambda,
           rwkv_mu, rwkv_w0, rwkv_w2, rwkv_a0, rwkv_a2, rwkv_g2, rwkv_k_k, rwkv_k_a, rwkv_r_k,
           rwkv_ln_g, rwkv_ln_b, ab_w_out,
           cd_w_in, nsa_q_norm, nsa_k_norm, cmp_k_w1, cmp_k_b1, cmp_k_w2, cmp_k_b2,
           cmp_v_w1, cmp_v_b1, cmp_v_w2, cmp_v_b2, ret_gn_g, ret_gn_b, cd_w_out):
    dt = x_prompt.dtype
    B = x_prompt.shape[0]
    DB = x_sample.shape[0]
    y = jnp.concatenate([x_prompt.reshape(N_PROMPT, D_MODEL), x_sample.reshape(DB * DEC_SEQ, D_MODEL)], axis=0)
    lru_h_p, lru_h_s, lru_c_p, lru_c_s, sh_p, sh_s, wkv_p, wkv_s = [], [], [], [], [], [], [], []
    kv_p, kv_s, win_p, win_s, ret_p, ret_s = [], [], [], [], [], []
    for layer in range(DEPTH):
        li = layer // 2
        y = ffn_block(y, norm_ffn1[layer], *_prep_ffn_weights(ffn1_w_in, ffn1_w_out, layer))
        if layer % 2 == 0:
            p = {'conv_w': lru_conv_w[li], 'conv_b': lru_conv_b[li],
                 'wa': lru_wa[li], 'ba': lru_ba[li], 'wx': lru_wx[li], 'bx': lru_bx[li], 'lam': lru_lambda[li],
                 'mu': rwkv_mu[li], 'w0': rwkv_w0[li], 'w2': rwkv_w2[li], 'a0': rwkv_a0[li], 'a2': rwkv_a2[li],
                 'g2': rwkv_g2[li], 'k_k': rwkv_k_k[li], 'k_a': rwkv_k_a[li], 'r_k': rwkv_r_k[li],
                 'ln_g': rwkv_ln_g[li], 'ln_b': rwkv_ln_b[li]}
            proj = norm_matmul(y, norm_mix[layer], _prep_cols(ab_w_in[li], WIDE_COL_TILE), tn=WIDE_COL_TILE)
            cat, (a0, a1, a2, a3), (b0, b1, b2, b3) = even_mixer(
                proj, p, B, SEQ, DB, state_lru_h[li], state_lru_conv[li], state_rwkv_shift[li], state_rwkv_wkv[li])
            lru_h_p.append(a0); lru_c_p.append(a1); sh_p.append(a2); wkv_p.append(a3)
            lru_h_s.append(b0); lru_c_s.append(b1); sh_s.append(b2); wkv_s.append(b3)
            w_out = ab_w_out[li]
        else:
            p = {'q_norm': nsa_q_norm[li], 'k_norm': nsa_k_norm[li],
                 'ck': (cmp_k_w1[li], cmp_k_b1[li], cmp_k_w2[li], cmp_k_b2[li]),
                 'cv': (cmp_v_w1[li], cmp_v_b1[li], cmp_v_w2[li], cmp_v_b2[li]),
                 'gn_g': ret_gn_g[li], 'gn_b': ret_gn_b[li]}
            proj = norm_matmul(y, norm_mix[layer], _odd_weight_cols(cd_w_in[li]), tn=WIDE_COL_TILE)
            cat, (a0, a1, a2), (b0, b1, b2) = odd_mixer(
                proj, p, B, SEQ, DB, cache_nsa_kv[li], page_table, cache_nsa_win[li], state_ret[li])
            kv_p.append(a0); win_p.append(a1); ret_p.append(a2)
            kv_s.append(b0); win_s.append(b1); ret_s.append(b2)
            w_out = cd_w_out[li]
        y = matmul_residual(cat, w_out.astype(BF16), y, tn=WIDE_COL_TILE)
        y = ffn_block(y, norm_ffn2[layer], *_prep_ffn_weights(ffn2_w_in, ffn2_w_out, layer))
    yp = y[:N_PROMPT].reshape(B, SEQ, D_MODEL)
    ys = y[N_PROMPT:].reshape(DB, DEC_SEQ, D_MODEL)
    return (yp, ys,
            _stack(lru_h_p, dt), _stack(lru_h_s, dt), _stack(lru_c_p, dt), _stack(lru_c_s, dt),
            _stack(sh_p, dt), _stack(sh_s, dt), _stack(wkv_p, dt), _stack(wkv_s, dt),
            _stack(kv_p, dt), _stack(kv_s, dt), _stack(win_p, dt), _stack(win_s, dt),
            _stack(ret_p, dt), _stack(ret_s, dt))
```

```python
import functools

import jax
import jax.numpy as jnp
import numpy as np
from jax import lax
from jax.experimental import pallas as pl
from jax.experimental.pallas import tpu as pltpu

D_MODEL = 2048
BATCH = 4
SEQ = 2048
DEPTH = 2
DEC_BATCH = 128
DEC_SEQ = 1
PAST_LEN = 2048
PAGE_SIZE = 128
D_FF = 5504
LRU_W = D_MODEL // 2
LRU_BLOCKS = 16
LRU_BS = LRU_W // LRU_BLOCKS
CONV_W = 4
LRU_C = 8.0
RWKV_W = D_MODEL // 2
RWKV_HD = 64
RWKV_H = RWKV_W // RWKV_HD
W_LORA = 64
A_LORA = 64
G_LORA = 160
SHIFT_W = 3 * RWKV_W + W_LORA + A_LORA + G_LORA
AB_COLS = 2 * LRU_W + SHIFT_W
NSA_H = 16
NSA_G = 4
NSA_HPG = NSA_H // NSA_G
NSA_HD = 64
NSA_W = NSA_H * NSA_HD
ROPE_DIMS = NSA_HD // 4
ROPE_THETA = 500000.0
CMP_BLOCK = 32
CMP_STRIDE = 16
CMP_R = CMP_BLOCK // CMP_STRIDE
CMP_HID = 256
SEL_BLOCK = 64
SEL_TOP = 16
SEL_Q_BLOCK = 64
WINDOW = 512
WIN_BLOCK = 128
FORCE_SCORE = 1e4
KV_SLOTS = 4
RET_H = 8
RET_DK = 64
RET_DV = 128
RET_W = RET_H * RET_DV
RET_CHUNK = 128
RET_THETA = 10000.0
CD_COLS = NSA_W + 6 * NSA_G * NSA_HD + 3 * NSA_H + 2 * RET_H * RET_DK + 2 * RET_W

N_TOK = BATCH * SEQ + DEC_BATCH * DEC_SEQ
N_PROMPT = BATCH * SEQ

LANE = 128
VMEM_LIMIT_BYTES = 56 * 1024 * 1024
ROW_TILE = 640
FF_TILE = 512
D_FF_PAD = 5632
COL_TILE = 512
WIDE_COL_TILE = 2048

BF16 = jnp.bfloat16
F32 = jnp.float32


def _round_up(n, m):
    return -(-n // m) * m


def _rms_rows(x, g):
    ms = jnp.mean(x * x, axis=-1, keepdims=True)
    return x * lax.rsqrt(ms + 1e-6) * g


def _ffn_kernel(x_ref, g_ref, wg_ref, wu_ref, wo_ref, o_ref, xn_ref, acc_ref):
    k = pl.program_id(1)

    @pl.when(k == 0)
    def _():
        xn_ref[...] = _rms_rows(x_ref[...], g_ref[...]).astype(BF16)
        acc_ref[...] = jnp.zeros_like(acc_ref)

    xn = xn_ref[...]
    gate = jnp.dot(xn, wg_ref[...], preferred_element_type=F32)
    up = jnp.dot(xn, wu_ref[...], preferred_element_type=F32)
    act = gate * jax.nn.sigmoid(gate) * up
    acc_ref[...] += jnp.dot(act.astype(BF16), wo_ref[...], preferred_element_type=F32)

    @pl.when(k == pl.num_programs(1) - 1)
    def _():
        o_ref[...] = x_ref[...] + 0.5 * acc_ref[...]


def ffn_block(x, g, wg, wu, wo):
    m, d = x.shape
    return pl.pallas_call(
        _ffn_kernel,
        grid=(m // ROW_TILE, D_FF_PAD // FF_TILE),
        in_specs=[
            pl.BlockSpec((ROW_TILE, d), lambda i, k: (i, 0)),
            pl.BlockSpec((1, d), lambda i, k: (0, 0)),
            pl.BlockSpec((d, FF_TILE), lambda i, k: (0, k)),
            pl.BlockSpec((d, FF_TILE), lambda i, k: (0, k)),
            pl.BlockSpec((FF_TILE, d), lambda i, k: (k, 0)),
        ],
        out_specs=pl.BlockSpec((ROW_TILE, d), lambda i, k: (i, 0)),
        out_shape=jax.ShapeDtypeStruct((m, d), F32),
        scratch_shapes=[pltpu.VMEM((ROW_TILE, d), BF16), pltpu.VMEM((ROW_TILE, d), F32)],
        compiler_params=pltpu.CompilerParams(
            dimension_semantics=("parallel", "arbitrary"), vmem_limit_bytes=VMEM_LIMIT_BYTES),
        name="ffn_block",
    )(x, g.reshape(1, d), wg, wu, wo)


def _norm_matmul_kernel(x_ref, g_ref, w_ref, o_ref, xn_ref):
    @pl.when(pl.program_id(1) == 0)
    def _():
        xn_ref[...] = _rms_rows(x_ref[...], g_ref[...]).astype(BF16)

    o_ref[...] = jnp.dot(xn_ref[...], w_ref[...], preferred_element_type=F32)


def norm_matmul(x, g, w, tn=COL_TILE):
    m, k = x.shape
    n = w.shape[1]
    return pl.pallas_call(
        _norm_matmul_kernel,
        grid=(m // ROW_TILE, n // tn),
        in_specs=[
            pl.BlockSpec((ROW_TILE, k), lambda i, j: (i, 0)),
            pl.BlockSpec((1, k), lambda i, j: (0, 0)),
            pl.BlockSpec((k, tn), lambda i, j: (0, j)),
        ],
        out_specs=pl.BlockSpec((ROW_TILE, tn), lambda i, j: (i, j)),
        out_shape=jax.ShapeDtypeStruct((m, n), F32),
        scratch_shapes=[pltpu.VMEM((ROW_TILE, k), BF16)],
        compiler_params=pltpu.CompilerParams(
            dimension_semantics=("parallel", "arbitrary"), vmem_limit_bytes=VMEM_LIMIT_BYTES),
        name="norm_matmul",
    )(x, g.reshape(1, k), w)


def _matmul_residual_kernel(a_ref, w_ref, r_ref, o_ref):
    o_ref[...] = r_ref[...] + jnp.dot(a_ref[...].astype(BF16), w_ref[...], preferred_element_type=F32)


def matmul_residual(a, w, res, tn=COL_TILE):
    m, k = a.shape
    n = w.shape[1]
    return pl.pallas_call(
        _matmul_residual_kernel,
        grid=(m // ROW_TILE, n // tn),
        in_specs=[
            pl.BlockSpec((ROW_TILE, k), lambda i, j: (i, 0)),
            pl.BlockSpec((k, tn), lambda i, j: (0, j)),
            pl.BlockSpec((ROW_TILE, tn), lambda i, j: (i, j)),
        ],
        out_specs=pl.BlockSpec((ROW_TILE, tn), lambda i, j: (i, j)),
        out_shape=jax.ShapeDtypeStruct((m, n), F32),
        compiler_params=pltpu.CompilerParams(
            dimension_semantics=("parallel", "arbitrary"), vmem_limit_bytes=VMEM_LIMIT_BYTES),
        name="matmul_residual",
    )(a, w, res)


WCAST_ROWS = 256
WCAST_COLS = 512


def _cast_w_in_kernel(w_ref, wg_ref, wu_ref):
    pad = jnp.zeros((w_ref.shape[0], D_FF_PAD - D_FF), BF16)
    wg_ref[:, :D_FF] = w_ref[:, :D_FF].astype(BF16)
    wg_ref[:, D_FF:] = pad
    wu_ref[:, :D_FF] = w_ref[:, D_FF:].astype(BF16)
    wu_ref[:, D_FF:] = pad


def _cast_w_out_kernel(w_ref, wo_ref):
    wo_ref[:D_FF, :] = w_ref[...].astype(BF16)
    wo_ref[D_FF:, :] = jnp.zeros((D_FF_PAD - D_FF, w_ref.shape[1]), BF16)


def _prep_ffn_weights(w_in, w_out, layer):
    d = w_in.shape[1]
    wg, wu = pl.pallas_call(
        _cast_w_in_kernel,
        grid=(d // WCAST_ROWS,),
        in_specs=[pl.BlockSpec((None, WCAST_ROWS, 2 * D_FF), lambda i: (layer, i, 0))],
        out_specs=[pl.BlockSpec((WCAST_ROWS, D_FF_PAD), lambda i: (i, 0))] * 2,
        out_shape=[jax.ShapeDtypeStruct((d, D_FF_PAD), BF16)] * 2,
        compiler_params=pltpu.CompilerParams(dimension_semantics=("parallel",), vmem_limit_bytes=VMEM_LIMIT_BYTES),
        name="cast_w_in",
    )(w_in)
    wo = pl.pallas_call(
        _cast_w_out_kernel,
        grid=(d // WCAST_COLS,),
        in_specs=[pl.BlockSpec((None, D_FF, WCAST_COLS), lambda j: (layer, 0, j))],
        out_specs=pl.BlockSpec((D_FF_PAD, WCAST_COLS), lambda j: (0, j)),
        out_shape=jax.ShapeDtypeStruct((D_FF_PAD, d), BF16),
        compiler_params=pltpu.CompilerParams(dimension_semantics=("parallel",), vmem_limit_bytes=VMEM_LIMIT_BYTES),
        name="cast_w_out",
    )(w_out)
    return wg, wu, wo


def _prep_cols(w, tile):
    n = w.shape[1]
    return jnp.pad(w, ((0, 0), (0, _round_up(n, tile) - n))).astype(BF16)


SCAN_TILE = 256


def _lru_scan_kernel(a_ref, b_ref, h0_ref, o_ref, carry_ref):
    @pl.when(pl.program_id(1) == 0)
    def _():
        carry_ref[...] = h0_ref[...]

    a = a_ref[...]
    b = b_ref[...]
    rows = lax.broadcasted_iota(jnp.int32, a.shape, 0)
    k = 1
    while k < a.shape[0]:
        keep = rows >= k
        b = jnp.where(keep, a * pltpu.roll(b, k, 0) + b, b)
        a = jnp.where(keep, a * pltpu.roll(a, k, 0), a)
        k *= 2
    h = a * carry_ref[...] + b
    o_ref[...] = h
    carry_ref[...] = h[a.shape[0] - 1:, :]


def lru_scan(a, b, h0):
    B, T, W = a.shape
    tt = min(SCAN_TILE, T)
    return pl.pallas_call(
        _lru_scan_kernel,
        grid=(B, T // tt),
        in_specs=[
            pl.BlockSpec((None, tt, W), lambda i, t: (i, t, 0)),
            pl.BlockSpec((None, tt, W), lambda i, t: (i, t, 0)),
            pl.BlockSpec((None, 1, W), lambda i, t: (i, 0, 0)),
        ],
        out_specs=pl.BlockSpec((None, tt, W), lambda i, t: (i, t, 0)),
        out_shape=jax.ShapeDtypeStruct((B, T, W), F32),
        scratch_shapes=[pltpu.VMEM((1, W), F32)],
        compiler_params=pltpu.CompilerParams(
            dimension_semantics=("parallel", "arbitrary"), vmem_limit_bytes=VMEM_LIMIT_BYTES),
        name="lru_scan",
    )(a, b, h0.reshape(B, 1, W))


GROUP_W = NSA_HPG * NSA_HD
ATT_Q_TILE = 256
ATT_Q_TILE_WIN = 128
ATT_K_TILE = 256
CMP_PAD = 128
NEG_BIG = -1e30


def _stack_heads(q):
    head = lax.broadcasted_iota(jnp.int32, q.shape, 1) // NSA_HD
    return jnp.concatenate([jnp.where(head == h, q, 0.0) for h in range(NSA_HPG)], axis=0)


def _unstack_heads(o, tq):
    head = lax.broadcasted_iota(jnp.int32, (tq, GROUP_W), 1) // NSA_HD
    out = jnp.zeros((tq, GROUP_W), F32)
    for h in range(NSA_HPG):
        out = out + jnp.where(head == h, o[h * tq:(h + 1) * tq], 0.0)
    return out


def _cmp_select_kernel(q_ref, k_ref, v_ref, ov_ref, o_ref, sel_ref, *, n_cmp, n_sel, q_pos0):
    tq = q_ref.shape[0]
    i = pl.program_id(2)
    qs = _stack_heads(q_ref[...] * (NSA_HD ** -0.5)).astype(BF16)
    s = lax.dot_general(qs, k_ref[...], (((1,), (1,)), ((), ())), preferred_element_type=F32)
    q_pos = q_pos0 + i * tq + lax.broadcasted_iota(jnp.int32, (tq, CMP_PAD), 0)
    c = lax.broadcasted_iota(jnp.int32, (tq, CMP_PAD), 1)
    mask1 = (c < n_cmp) & (c * CMP_STRIDE + (CMP_BLOCK - 1) <= q_pos)
    mask = jnp.concatenate([mask1] * NSA_HPG, axis=0)
    s = jnp.where(mask, s, NEG_BIG)
    m = jnp.max(s, axis=-1, keepdims=True)
    e = jnp.where(mask, jnp.exp(s - m), 0.0)
    den = jnp.sum(e, axis=-1, keepdims=True)
    prob = e / jnp.where(den > 0, den, 1.0)
    o = jnp.dot(prob.astype(BF16), v_ref[...], preferred_element_type=F32)
    o_ref[...] = _unstack_heads(o, tq)
    psum = prob[0:tq]
    for h in range(1, NSA_HPG):
        psum = psum + prob[h * tq:(h + 1) * tq]
    imp = jnp.dot(psum.astype(BF16), ov_ref[...], preferred_element_type=F32)
    qb = q_pos // SEL_BLOCK
    valid = (c <= qb) & (c < n_sel)
    forced = (c == 0) | (c == qb) | (c == qb - 1)
    score = jnp.where(valid, jnp.where(forced, FORCE_SCORE, imp), -jnp.inf)
    k_top = min(SEL_TOP, n_sel)
    few_blocks = (q_pos0 + (i + 1) * tq - 1) // SEL_BLOCK < k_top

    @pl.when(few_blocks)
    def _():
        sel_ref[...] = jnp.where(valid, 1.0, 0.0)

    @pl.when(jnp.logical_not(few_blocks))
    def _():
        rank = jnp.zeros((tq, CMP_PAD), F32)
        for jp in range(n_sel):
            col = score[:, jp:jp + 1]
            beats = (col > score) | ((col == score) & (c > jp))
            rank = rank + jnp.where(beats, 1.0, 0.0)
        sel_ref[...] = jnp.where((rank < k_top) & (c < n_sel), 1.0, 0.0)


def nsa_cmp_select(qn, kc4, vc4, ovT, *, n_cmp, n_sel, q_pos0):
    B, T, _ = qn.shape
    tq = min(ATT_Q_TILE, T)
    return pl.pallas_call(
        functools.partial(_cmp_select_kernel, n_cmp=n_cmp, n_sel=n_sel, q_pos0=q_pos0),
        grid=(B, NSA_G, T // tq),
        in_specs=[
            pl.BlockSpec((None, tq, GROUP_W), lambda b, g, i: (b, i, g)),
            pl.BlockSpec((None, None, CMP_PAD, GROUP_W), lambda b, g, i: (b, g, 0, 0)),
            pl.BlockSpec((None, None, CMP_PAD, GROUP_W), lambda b, g, i: (b, g, 0, 0)),
            pl.BlockSpec((CMP_PAD, CMP_PAD), lambda b, g, i: (0, 0)),
        ],
        out_specs=[
            pl.BlockSpec((None, tq, GROUP_W), lambda b, g, i: (b, i, g)),
            pl.BlockSpec((None, None, tq, CMP_PAD), lambda b, g, i: (b, g, i, 0)),
        ],
        out_shape=[jax.ShapeDtypeStruct((B, T, NSA_W), F32),
                   jax.ShapeDtypeStruct((B, NSA_G, T, CMP_PAD), F32)],
        compiler_params=pltpu.CompilerParams(
            dimension_semantics=("parallel", "parallel", "parallel"), vmem_limit_bytes=VMEM_LIMIT_BYTES),
        name="nsa_cmp_select",
    )(qn, kc4, vc4, ovT)


def _flash_kernel(*refs, selected):
    if selected:
        q_ref, k_ref, v_ref, sel_ref, exp_ref, o_ref, m_ref, l_ref, acc_ref, s_a, s_b = refs
    else:
        q_ref, k_ref, v_ref, o_ref, m_ref, l_ref, acc_ref, s_a, s_b = refs
    tq = q_ref.shape[0]
    tk = ATT_K_TILE
    n_tiles = k_ref.shape[0] // tk
    i = pl.program_id(2)
    q = q_ref[...] * (NSA_HD ** -0.5)
    head = lax.broadcasted_iota(jnp.int32, q.shape, 1) // NSA_HD
    q4 = _stack_heads(q).astype(BF16)
    m_ref[...] = jnp.full(m_ref.shape, NEG_BIG, F32)
    l_ref[...] = jnp.zeros(l_ref.shape, F32)
    acc_ref[...] = jnp.zeros(acc_ref.shape, F32)
    q_pos = i * tq + lax.broadcasted_iota(jnp.int32, (tq, tk), 0)
    col = lax.broadcasted_iota(jnp.int32, (tq, tk), 1)
    if selected:
        sel = sel_ref[...].astype(BF16)
        lo = 0
    else:
        lo = jnp.maximum(i * tq - (WINDOW - 1), 0) // tk
    hi = (i * tq + tq - 1) // tk + 1

    def tile_start(j):
        return pl.multiple_of(jnp.minimum(j, n_tiles - 1) * tk, tk)

    def scores(j, s_ref):
        s_ref[...] = lax.dot_general(q4, k_ref[pl.ds(tile_start(j), tk), :], (((1,), (1,)), ((), ())),
                                     preferred_element_type=F32)

    def consume(j, s_ref):
        v = v_ref[pl.ds(tile_start(j), tk), :]
        k_pos = j * tk + col
        mask = k_pos <= q_pos
        if selected:
            mask = mask & (jnp.dot(sel, exp_ref[jnp.minimum(j, n_tiles - 1)], preferred_element_type=F32) > 0.5)
        else:
            mask = mask & (q_pos - k_pos < WINDOW)
        if not selected:
            bias = jnp.where(mask, 0.0, 2.0 * NEG_BIG)
        for h in range(NSA_HPG):
            m_old = m_ref[h]
            if selected:
                s = jnp.where(mask, s_ref[h * tq:(h + 1) * tq, :], NEG_BIG)
                m_new = jnp.maximum(m_old, jnp.max(s, axis=-1, keepdims=True))
                p = jnp.where(mask, jnp.exp(s - pltpu.repeat(m_new, tk // LANE, axis=1)), 0.0)
            else:
                s = s_ref[h * tq:(h + 1) * tq, :] + bias
                m_new = jnp.maximum(m_old, jnp.max(s, axis=-1, keepdims=True))
                p = jnp.exp(s - pltpu.repeat(m_new, tk // LANE, axis=1))
            alpha = jnp.exp(m_old - m_new)
            l_ref[h] = alpha * l_ref[h] + jnp.sum(p, axis=-1, keepdims=True)
            acc_ref[h] = (pltpu.repeat(alpha, GROUP_W // LANE, axis=1) * acc_ref[h]
                          + jnp.dot(p.astype(BF16), v, preferred_element_type=F32))
            m_ref[h] = m_new

    scores(lo, s_a)

    def body(t, carry):
        j = lo + 2 * t
        scores(j + 1, s_b)
        consume(j, s_a)
        scores(j + 2, s_a)
        consume(j + 1, s_b)
        return carry

    lax.fori_loop(0, (hi - lo + 1) // 2, body, 0)
    out = jnp.zeros((tq, GROUP_W), F32)
    for h in range(NSA_HPG):
        den = pltpu.repeat(l_ref[h], GROUP_W // LANE, axis=1)
        out = out + jnp.where(head == h, acc_ref[h] / jnp.where(den > 0, den, 1.0), 0.0)
    o_ref[...] = out


def nsa_flash(qr, k4, v4, sel=None, expand=None):
    B, T, _ = qr.shape
    selected = sel is not None
    tq = ATT_Q_TILE if selected else ATT_Q_TILE_WIN
    in_specs = [
        pl.BlockSpec((None, tq, GROUP_W), lambda b, g, i: (b, i, g)),
        pl.BlockSpec((None, T, GROUP_W), lambda b, g, i: (b, 0, g)),
        pl.BlockSpec((None, T, GROUP_W), lambda b, g, i: (b, 0, g)),
    ]
    args = [qr, k4, v4]
    if selected:
        in_specs += [
            pl.BlockSpec((None, None, tq, CMP_PAD), lambda b, g, i: (b, g, i, 0)),
            pl.BlockSpec(expand.shape, lambda b, g, i: (0, 0, 0)),
        ]
        args += [sel, expand]
    return pl.pallas_call(
        functools.partial(_flash_kernel, selected=selected),
        grid=(B, NSA_G, T // tq),
        in_specs=in_specs,
        out_specs=pl.BlockSpec((None, tq, GROUP_W), lambda b, g, i: (b, i, g)),
        out_shape=jax.ShapeDtypeStruct((B, T, NSA_W), F32),
        scratch_shapes=[pltpu.VMEM((NSA_HPG, tq, LANE), F32), pltpu.VMEM((NSA_HPG, tq, LANE), F32),
                        pltpu.VMEM((NSA_HPG, tq, GROUP_W), F32),
                        pltpu.VMEM((NSA_HPG * tq, ATT_K_TILE), F32), pltpu.VMEM((NSA_HPG * tq, ATT_K_TILE), F32)],
        compiler_params=pltpu.CompilerParams(
            dimension_semantics=("parallel", "parallel", "parallel"), vmem_limit_bytes=VMEM_LIMIT_BYTES),
        name="nsa_flash_sel" if selected else "nsa_flash_win",
    )(*args)


def _tile_groups(x):
    B, T = x.shape[:2]
    return jnp.broadcast_to(x[:, :, :, None, :], (B, T, NSA_G, NSA_HPG, NSA_HD)).reshape(B, T, NSA_W).astype(BF16)


def _tile_cmp(x):
    B, n = x.shape[:2]
    x = jnp.pad(jnp.moveaxis(x, 1, 2), ((0, 0), (0, 0), (0, CMP_PAD - n), (0, 0)))
    return jnp.tile(x, (1, 1, 1, NSA_HPG)).astype(BF16)


def _overlap_T(n_cmp, n_sel):
    ov = np.zeros((CMP_PAD, CMP_PAD), np.float32)
    cs = np.arange(n_cmp) * CMP_STRIDE
    ss = np.arange(n_sel) * SEL_BLOCK
    o = np.minimum(cs[None] + CMP_BLOCK, ss[:, None] + SEL_BLOCK) - np.maximum(cs[None], ss[:, None])
    ov[:n_cmp, :n_sel] = (np.clip(o, 0, None) / CMP_BLOCK).T
    return jnp.asarray(ov, dtype=BF16)


def _sel_expand(T):
    t = np.arange(T)
    e = (np.arange(CMP_PAD)[:, None] == (t // SEL_BLOCK)[None, :]).astype(np.float32)
    return jnp.asarray(e.reshape(CMP_PAD, T // ATT_K_TILE, ATT_K_TILE).transpose(1, 0, 2), dtype=BF16)


KV_ROWS = KV_SLOTS * NSA_G * NSA_HD
SLOT_ROWS = NSA_G * NSA_HD
N_PAGES = PAST_LEN // PAGE_SIZE
DEC_N_CHUNK = (PAST_LEN + DEC_SEQ) // CMP_STRIDE
DEC_N_CMP = DEC_N_CHUNK - CMP_R + 1
DEC_N_SEL = -(-(PAST_LEN + DEC_SEQ) // SEL_BLOCK)
WIN_BUF = min(WINDOW, PAST_LEN)


def _softmax_rows(s, mask, s_new=None):
    s = jnp.where(mask, s, NEG_BIG)
    m = jnp.max(s, axis=-1, keepdims=True)
    if s_new is not None:
        m = jnp.maximum(m, s_new)
    e = jnp.where(mask, jnp.exp(s - m), 0.0)
    den = jnp.sum(e, axis=-1, keepdims=True)
    if s_new is None:
        return e, den
    e_new = jnp.exp(s_new - m)
    return e, e_new, den + e_new


def _dec_nsa_kernel(pt_ref, *refs):
    pages = refs[:N_PAGES]
    (win_ref, qn_ref, qr_ref, new_ref, gate_ref, w1_ref, b1_ref, w2_ref, b2_ref, kn_ref,
     ov_ref, exp_ref, grp_ref, perm_ref, o_ref, xt_ref, acc_ref) = refs[N_PAGES:]
    del pt_ref
    f32 = F32
    half = 2 * NSA_HD
    n_chunk = DEC_N_CHUNK

    perm = perm_ref[...]
    per_page = PAGE_SIZE // CMP_STRIDE
    for p in range(N_PAGES):
        for sg in range(4):
            tile = pages[p][sg * half:(sg + 1) * half, :].astype(BF16)
            xt = lax.dot_general(perm, tile, (((1,), (1,)), ((), ())), preferred_element_type=f32)
            for r in range(CMP_STRIDE):
                xt_ref[sg, r, p * per_page:(p + 1) * per_page, :] = xt[r * per_page:(r + 1) * per_page, :]

    lane_lo = lax.broadcasted_iota(jnp.int32, (n_chunk, half), 1) < NSA_HD
    lane_grp = lax.broadcasted_iota(jnp.int32, (n_chunk, SLOT_ROWS), 1) // NSA_HD
    cmp_rows = []
    for slot in range(2):
        for gp in range(2):
            los, his = [], []
            for rp in range(CMP_STRIDE // 2):
                x0, x1 = (xt_ref[slot * 2 + gp, 2 * rp + j] for j in range(2))
                los.append(jnp.where(lane_lo, x0, pltpu.roll(x1, NSA_HD, 1)))
                his.append(jnp.where(lane_lo, pltpu.roll(x0, NSA_HD, 1), x1))
            lhs = jnp.concatenate([jnp.concatenate(los, axis=1), jnp.concatenate(his, axis=1)], axis=0).astype(BF16)
            acc_ref[pl.ds(gp * 2 * n_chunk, 2 * n_chunk), :] = jnp.dot(lhs, w1_ref[slot], preferred_element_type=f32)
        acc = acc_ref[...]
        pre = b1_ref[slot] + acc[:, :CMP_HID] + pltpu.roll(acc[:, CMP_HID:], NSA_G * n_chunk - 1, 0)
        out = jnp.dot(jax.nn.gelu(pre).astype(BF16), w2_ref[slot], preferred_element_type=f32) + b2_ref[slot]
        if slot == 0:
            out = _rms_rows(out, kn_ref[...])
        sel_rows = jnp.zeros((n_chunk, SLOT_ROWS), f32)
        for g in range(NSA_G):
            sel_rows = sel_rows + jnp.where(lane_grp == g, out[g * n_chunk:(g + 1) * n_chunk], 0.0)
        cmp_rows.append(sel_rows.astype(BF16))
    kc, vc = cmp_rows

    qn = qn_ref[...].astype(BF16)
    qr = qr_ref[...].astype(BF16)
    nt = (((1,), (1,)), ((), ()))
    c = lax.broadcasted_iota(jnp.int32, (NSA_H, CMP_PAD), 1)
    s = lax.dot_general(qn, kc, nt, preferred_element_type=f32)
    e, den = _softmax_rows(s, c < DEC_N_CMP)
    prob = e / jnp.where(den > 0, den, 1.0)
    o_cmp = jnp.dot(prob.astype(BF16), vc, preferred_element_type=f32)
    p_hi, p_mid = _split_bf16(prob)
    p_lo = (prob - p_hi.astype(f32) - p_mid.astype(f32)).astype(BF16)
    grp = grp_ref[...]
    psum = (jnp.dot(grp, p_hi, preferred_element_type=f32) + jnp.dot(grp, p_mid, preferred_element_type=f32)
            + jnp.dot(grp, p_lo, preferred_element_type=f32))
    imp = jnp.dot(psum.astype(BF16), ov_ref[...], preferred_element_type=f32)
    qb = (PAST_LEN + DEC_SEQ - 1) // SEL_BLOCK
    valid = c <= qb
    forced = (c == 0) | (c == qb) | (c == qb - 1)
    score = jnp.where(valid, jnp.where(forced, FORCE_SCORE, imp), -jnp.inf)
    rank = jnp.zeros((NSA_H, CMP_PAD), f32)
    for jp in range(DEC_N_SEL):
        col = score[:, jp:jp + 1]
        rank = rank + jnp.where((col > score) | ((col == score) & (c > jp)), 1.0, 0.0)
    sel = jnp.where((rank < min(SEL_TOP, DEC_N_SEL)) & (c < DEC_N_SEL), 1.0, 0.0).astype(BF16)

    new = new_ref[...]
    new_b = new.astype(BF16).astype(f32)
    qr_f = qr.astype(f32)
    s_pages = [jnp.dot(qr, pages[p][2 * SLOT_ROWS:3 * SLOT_ROWS, :].astype(BF16), preferred_element_type=f32)
               for p in range(N_PAGES)]
    s = jnp.concatenate(s_pages, axis=1)
    mask = jnp.dot(sel, exp_ref[...], preferred_element_type=f32) > 0.5
    s_new = jnp.sum(qr_f * new_b[0:1], axis=-1, keepdims=True)
    e, e_new, den = _softmax_rows(s, mask, s_new)
    e = e.astype(BF16)
    o_slc = e_new.astype(BF16).astype(f32) * new_b[1:2]
    for p in range(N_PAGES):
        o_slc = o_slc + lax.dot_general(e[:, p * PAGE_SIZE:(p + 1) * PAGE_SIZE],
                                        pages[p][3 * SLOT_ROWS:4 * SLOT_ROWS, :].astype(BF16), nt,
                                        preferred_element_type=f32)
    o_slc = o_slc / den

    s = jnp.dot(qr, win_ref[0:SLOT_ROWS, :].astype(BF16), preferred_element_type=f32)
    i_buf = lax.broadcasted_iota(jnp.int32, (NSA_H, WIN_BUF), 1)
    s_new = jnp.sum(qr_f * new_b[2:3], axis=-1, keepdims=True)
    e, e_new, den = _softmax_rows(s, WIN_BUF - i_buf < WINDOW, s_new)
    o_win = e_new.astype(BF16).astype(f32) * new_b[3:4] + lax.dot_general(
        e.astype(BF16), win_ref[SLOT_ROWS:2 * SLOT_ROWS, :].astype(BF16), nt, preferred_element_type=f32)
    o_win = o_win / den

    gates = gate_ref[...]
    o_ref[...] = gates[:, 0:1] * o_cmp + gates[:, 1:2] * o_slc + gates[:, 2:3] * o_win


def dec_nsa(page_table, cache_t, win_t, qn16, qr16, new_rows, gates, w1t, b1, w2t, b2t, kn, ovT, expand, grp):
    DB = qn16.shape[0]
    per_page = PAGE_SIZE // CMP_STRIDE
    tok = np.arange(PAGE_SIZE)
    perm = jnp.asarray((tok[:, None] // per_page == tok[None, :] % CMP_STRIDE)
                       & (tok[:, None] % per_page == tok[None, :] // CMP_STRIDE), dtype=BF16)
    const = lambda shape: pl.BlockSpec(shape, lambda b, pt: (0,) * len(shape))
    per_b = lambda shape: pl.BlockSpec((None,) + shape, lambda b, pt: (b,) + (0,) * len(shape))
    page_specs = [pl.BlockSpec((None, KV_ROWS, PAGE_SIZE), functools.partial(lambda b, pt, p: (pt[b, p], 0, 0), p=p))
                  for p in range(N_PAGES)]
    in_specs = page_specs + [
        per_b((2 * SLOT_ROWS, WIN_BUF)), per_b((NSA_H, SLOT_ROWS)), per_b((NSA_H, SLOT_ROWS)),
        per_b((4, SLOT_ROWS)), per_b((NSA_H, 3)),
        const(w1t.shape), const(b1.shape), const(w2t.shape), const(b2t.shape), const(kn.shape),
        const(ovT.shape), const(expand.shape), const(grp.shape), const(perm.shape),
    ]
    grid_spec = pltpu.PrefetchScalarGridSpec(
        num_scalar_prefetch=1, grid=(DB,), in_specs=in_specs,
        out_specs=pl.BlockSpec((None, NSA_H, SLOT_ROWS), lambda b, pt: (b, 0, 0)),
        scratch_shapes=[pltpu.VMEM((4, CMP_STRIDE, DEC_N_CHUNK, 2 * NSA_HD), F32),
                        pltpu.VMEM((NSA_G * DEC_N_CHUNK, CMP_R * CMP_HID), F32)])
    return pl.pallas_call(
        _dec_nsa_kernel,
        grid_spec=grid_spec,
        out_shape=jax.ShapeDtypeStruct((DB, NSA_H, SLOT_ROWS), F32),
        compiler_params=pltpu.CompilerParams(
            dimension_semantics=("arbitrary",), vmem_limit_bytes=VMEM_LIMIT_BYTES),
        name="dec_nsa",
    )(page_table, *([cache_t] * N_PAGES), win_t, qn16, qr16, new_rows, gates, w1t, b1, w2t, b2t, kn, ovT, expand, grp,
      perm)


WIN_SEQS_PER_STEP = 4


def _win_shift_kernel(win_ref, new_ref, o_ref):
    shape = win_ref.shape[1:]
    n = shape[1]
    row = lax.broadcasted_iota(jnp.int32, shape, 0)
    lane = lax.broadcasted_iota(jnp.int32, shape, 1)
    for s in range(win_ref.shape[0]):
        w = win_ref[s]
        col = jnp.sum(jnp.where(row == lane, jnp.broadcast_to(new_ref[s], shape), 0.0), axis=1, keepdims=True)
        o_ref[s] = jnp.where(lane == n - 1, col, pltpu.roll(w, n - 1, 1))


def win_shift(win_t, new_row):
    DB, R, W = win_t.shape
    ns = WIN_SEQS_PER_STEP
    assert R == W and DB % ns == 0
    return pl.pallas_call(
        _win_shift_kernel,
        grid=(DB // ns,),
        in_specs=[pl.BlockSpec((ns, R, W), lambda b: (b, 0, 0)), pl.BlockSpec((ns, 1, R), lambda b: (b, 0, 0))],
        out_specs=pl.BlockSpec((ns, R, W), lambda b: (b, 0, 0)),
        out_shape=jax.ShapeDtypeStruct((DB, R, W), win_t.dtype),
        compiler_params=pltpu.CompilerParams(dimension_semantics=("parallel",), vmem_limit_bytes=VMEM_LIMIT_BYTES),
        name="win_shift",
    )(win_t, new_row)


def _dec_cmp_weights(w1, b1, w2, b2):
    w = jnp.moveaxis(w1, 0, 1).reshape(CMP_STRIDE * NSA_HD, CMP_R * CMP_HID)
    return (w.astype(BF16), b1.reshape(1, CMP_HID), jnp.tile(w2, (1, NSA_G)).astype(BF16),
            jnp.tile(b2, NSA_G).reshape(1, SLOT_ROWS))


def _place_heads(q):
    own = (jnp.arange(NSA_H)[:, None] // NSA_HPG) == jnp.arange(NSA_G)[None, :]
    return jnp.where(own[None, :, :, None], q[:, :, None, :], 0.0).reshape(q.shape[0], NSA_H, SLOT_ROWS)


def _take_heads(o):
    o = o.reshape(o.shape[0], NSA_H, NSA_G, NSA_HD)
    return o[:, jnp.arange(NSA_H), jnp.arange(NSA_H) // NSA_HPG, :].reshape(o.shape[0], NSA_W)


WKV_C = 64
WKV_PAIR = 2 * RWKV_HD
WKV_T_TILE = 512
WKV_PAIRS_PER_STEP = 8


def _split_bf16(x):
    hi = x.astype(BF16)
    return hi, (x - hi.astype(F32)).astype(BF16)


def _dot3(a, b):
    a_hi, a_lo = _split_bf16(a)
    b_hi, b_lo = _split_bf16(b)
    return (jnp.dot(a_hi, b_hi, preferred_element_type=F32) + jnp.dot(a_hi, b_lo, preferred_element_type=F32)
            + jnp.dot(a_lo, b_hi, preferred_element_type=F32))


def _wkv_kernel(r_ref, lw_ref, k_ref, v_ref, a_ref, b_ref, s0_ref, y_ref, sT_ref, s_scr):
    C = WKV_C
    P = WKV_PAIR
    n_chunks = r_ref.shape[0] // C

    @pl.when(pl.program_id(2) == 0)
    def _():
        s_scr[...] = s0_ref[...]

    lo_lane = lax.broadcasted_iota(jnp.int32, (C, P), 1) < RWKV_HD
    row = lax.broadcasted_iota(jnp.int32, (2 * C, 2 * C), 0)
    col = lax.broadcasted_iota(jnp.int32, (2 * C, 2 * C), 1)
    same_head = (row // C) == (col // C)
    strict = same_head & (row > col)
    lower = same_head & (row >= col)
    eye = jnp.where(row == col, 1.0, 0.0)
    tril = jnp.where(lax.broadcasted_iota(jnp.int32, (C, C), 0) >= lax.broadcasted_iota(jnp.int32, (C, C), 1),
                     1.0, 0.0).astype(BF16)

    def stack(x):
        return jnp.concatenate([jnp.where(lo_lane, x, 0.0), jnp.where(lo_lane, 0.0, x)], axis=0)

    def chunk(c, carry):
        stages = [pair_chunk(c, q) for q in range(WKV_PAIRS_PER_STEP)]
        while stages:
            stages = [g for g in stages if next(g, True) is None]
        return carry

    def pair_chunk(c, q):
        sl = pl.ds(pl.multiple_of(c * C, C), C)
        lanes = slice(q * P, (q + 1) * P)
        r, lw, k, v, a, b = (ref[sl, lanes] for ref in (r_ref, lw_ref, k_ref, v_ref, a_ref, b_ref))
        lw_hi, lw_mid = _split_bf16(lw)
        lw_lo = (lw - lw_hi.astype(F32) - lw_mid.astype(F32)).astype(BF16)
        cs = (jnp.dot(tril, lw_hi, preferred_element_type=F32) + jnp.dot(tril, lw_mid, preferred_element_type=F32)
              + jnp.dot(tril, lw_lo, preferred_element_type=F32))
        yield
        g_inv = jnp.exp(-cs)
        g_end = jnp.exp(cs[C - 1:C, :] - cs)
        a2 = stack(a * jnp.exp(cs - lw))
        r2 = stack(r * jnp.exp(cs))
        b2 = stack(b * g_inv)
        k2 = stack(k * g_inv)
        v2 = stack(v)
        s_old = s_scr[q]
        ar = jnp.concatenate([a2, r2], axis=0).astype(BF16)
        bk = jnp.concatenate([b2, k2], axis=0).astype(BF16)
        nt = (((1,), (1,)), ((), ()))
        pp = lax.dot_general(ar, bk, nt, preferred_element_type=F32)
        from_state = lax.dot_general(ar, s_old.astype(BF16), nt, preferred_element_type=F32)
        yield
        l_ab = jnp.where(strict, pp[:2 * C, :2 * C], 0.0)
        l_ak = jnp.where(strict, pp[:2 * C, 2 * C:], 0.0)
        m_rb = jnp.where(lower, pp[2 * C:, :2 * C], 0.0)
        m_rk = jnp.where(lower, pp[2 * C:, 2 * C:], 0.0)
        v2b = v2.astype(BF16)
        rhs = from_state[:2 * C] + jnp.dot(l_ak.astype(BF16), v2b, preferred_element_type=F32)
        yield
        n = l_ab
        x = eye + n
        span = 2
        while span < C:
            n = _dot3(n, n)
            yield
            x = x + _dot3(n, x)
            yield
            span *= 2
        u2 = _dot3(x, rhs)
        yield
        uv = jnp.concatenate([u2, v2], axis=0).astype(BF16)
        y2 = from_state[2 * C:] + jnp.dot(jnp.concatenate([m_rb, m_rk], axis=1).astype(BF16), uv,
                                          preferred_element_type=F32)
        yield
        y_ref[sl, lanes] = y2[:C] + y2[C:]
        bk_end = jnp.concatenate([stack(b * g_end), stack(k * g_end)], axis=0).astype(BF16)
        s_scr[q] = s_old * jnp.exp(cs[C - 1:C, :]) + lax.dot_general(
            uv, bk_end, (((0,), (0,)), ((), ())), preferred_element_type=F32)

    lax.fori_loop(0, n_chunks, chunk, 0)

    @pl.when(pl.program_id(2) == pl.num_programs(2) - 1)
    def _():
        sT_ref[...] = s_scr[...]


def wkv7_chunked(r, lw, k, v, a, b, s0):
    B, T, W = r.shape
    n_pair = W // WKV_PAIR
    tt = min(WKV_T_TILE, T)
    s0p = s0.astype(F32).reshape(B, n_pair, 2, RWKV_HD, RWKV_HD)
    zero = jnp.zeros_like(s0p[:, :, 0])
    s0_bd = jnp.concatenate([jnp.concatenate([s0p[:, :, 0], zero], axis=-1),
                             jnp.concatenate([zero, s0p[:, :, 1]], axis=-1)], axis=-2)
    pps = WKV_PAIRS_PER_STEP
    seq = pl.BlockSpec((None, tt, pps * WKV_PAIR), lambda i, p, t: (i, t, p))
    st = pl.BlockSpec((None, pps, WKV_PAIR, WKV_PAIR), lambda i, p, t: (i, p, 0, 0))
    y, s_bd = pl.pallas_call(
        _wkv_kernel,
        grid=(B, n_pair // pps, T // tt),
        in_specs=[seq] * 6 + [st],
        out_specs=[seq, st],
        out_shape=[jax.ShapeDtypeStruct((B, T, W), F32),
                   jax.ShapeDtypeStruct((B, n_pair, WKV_PAIR, WKV_PAIR), F32)],
        scratch_shapes=[pltpu.VMEM((pps, WKV_PAIR, WKV_PAIR), F32)],
        compiler_params=pltpu.CompilerParams(
            dimension_semantics=("parallel", "parallel", "arbitrary"), vmem_limit_bytes=VMEM_LIMIT_BYTES),
        name="wkv7_chunked",
    )(r, lw, k, v, a, b, s0_bd)
    s_fin = jnp.stack([s_bd[:, :, :RWKV_HD, :RWKV_HD], s_bd[:, :, RWKV_HD:, RWKV_HD:]], axis=2)
    return y, s_fin.reshape(B, W // RWKV_HD, RWKV_HD, RWKV_HD)


AB_PAD = _round_up(AB_COLS, WIDE_COL_TILE)
SHIFT_PAD = _round_up(SHIFT_W, LANE)
LORA_PAD = SHIFT_PAD - 3 * RWKV_W
EVEN_ROWS = 256
POST_ROWS = 256
N_EVEN_PRE_OUT = 10


def _split3(x):
    hi = x.astype(BF16)
    r1 = x - hi.astype(F32)
    mid = r1.astype(BF16)
    return hi, mid, (r1 - mid.astype(F32)).astype(BF16)


def _dot_01(x, m):
    return sum(jnp.dot(part, m, preferred_element_type=F32) for part in _split3(x))


def _head_sum(x, red_ref, exp_ref):
    return _dot_01(_dot_01(x, red_ref[...]), exp_ref[...])


def _expm1(x):
    u = jnp.exp(x)
    d = u - 1.0
    log_u = jnp.where((d == 0.0) | (d == -1.0), 1.0, jnp.log(u))
    return jnp.where(d == 0.0, x, jnp.where(d == -1.0, -1.0, d * x / log_u))


def _even_pre_math(x_ref, prev, taps, prm, outs):
    (cw_ref, cb_ref, wa_ref, ba_ref, wx_ref, bx_ref, lam_ref, mu_ref, w0_ref, a0_ref, wl_ref,
     kk_ref, ka_ref, red_ref, exp_ref) = prm
    a_o, u_o, gate_o, r_o, lw_o, k_o, v_o, na_o, nb_o, g_o = outs
    t1, t2, t3 = taps
    xb = x_ref[:, 0:LRU_W]
    xc = cb_ref[...] + cw_ref[0:1] * t3 + cw_ref[1:2] * t2 + cw_ref[2:3] * t1 + cw_ref[3:4] * xb
    xcb = xc.astype(BF16)
    gate_r = jax.nn.sigmoid(jnp.dot(xcb, wa_ref[...], preferred_element_type=F32) + ba_ref[...])
    gate_i = jax.nn.sigmoid(jnp.dot(xcb, wx_ref[...], preferred_element_type=F32) + bx_ref[...])
    log_a = -LRU_C * gate_r * lam_ref[...]
    a_o[...] = jnp.exp(log_a)
    u_o[...] = jnp.sqrt(-_expm1(2.0 * log_a)) * (gate_i * xc)
    gate_o[...] = jax.nn.gelu(x_ref[:, LRU_W:2 * LRU_W])
    rw = x_ref[:, 2 * LRU_W:2 * LRU_W + SHIFT_PAD]
    rs = rw + mu_ref[...] * (prev - rw)
    r_o[...] = rs[:, 0:RWKV_W]
    k = rs[:, RWKV_W:2 * RWKV_W]
    v_o[...] = rs[:, 2 * RWKV_W:3 * RWKV_W]
    tail = rs[:, 3 * RWKV_W:]
    lane = lax.broadcasted_iota(jnp.int32, tail.shape, 1)
    act = jnp.where(lane < W_LORA, jnp.tanh(tail), jnp.where(lane < W_LORA + A_LORA, tail, jax.nn.sigmoid(tail)))
    z = jnp.dot(act.astype(BF16), wl_ref[...], preferred_element_type=F32)
    w_log = -jax.nn.softplus(-(w0_ref[...] + z[:, 0:RWKV_W])) - 0.5
    lw_o[...] = -jnp.exp(w_log)
    a_icl = jax.nn.sigmoid(a0_ref[...] + z[:, RWKV_W:2 * RWKV_W])
    g_o[...] = z[:, 2 * RWKV_W:]
    kk = k * kk_ref[...]
    kk = kk / jnp.maximum(jnp.sqrt(_head_sum(kk * kk, red_ref, exp_ref)), 1e-12)
    k_o[...] = k * (1.0 + (a_icl - 1.0) * ka_ref[...])
    na_o[...] = -kk
    nb_o[...] = kk * a_icl


def _even_pre_seq_kernel(x_ref, conv0_ref, shift0_ref, *refs):
    prm = refs[:15]
    outs = refs[15:15 + N_EVEN_PRE_OUT]
    conv_c, shift_c = refs[15 + N_EVEN_PRE_OUT:]
    rows = x_ref.shape[0]

    @pl.when(pl.program_id(1) == 0)
    def _():
        conv_c[...] = conv0_ref[...]
        shift_c[...] = shift0_ref[...]

    xb = x_ref[:, 0:LRU_W]
    row = lax.broadcasted_iota(jnp.int32, xb.shape, 0)
    taps = []
    for j in (1, 2, 3):
        tap = pltpu.roll(xb, j, 0)
        for i in range(j):
            tap = jnp.where(row == i, conv_c[8 - j + i:9 - j + i, :], tap)
        taps.append(tap)
    rw = x_ref[:, 2 * LRU_W:2 * LRU_W + SHIFT_PAD]
    row_w = lax.broadcasted_iota(jnp.int32, rw.shape, 0)
    prev = jnp.where(row_w == 0, shift_c[7:8, :], pltpu.roll(rw, 1, 0))
    _even_pre_math(x_ref, prev, taps, prm, outs)
    conv_c[...] = x_ref[rows - 8:rows, 0:LRU_W]
    shift_c[...] = x_ref[rows - 8:rows, 2 * LRU_W:2 * LRU_W + SHIFT_PAD]


def _even_pre_step_kernel(x_ref, prev_ref, t1_ref, t2_ref, t3_ref, *refs):
    _even_pre_math(x_ref, prev_ref[...], (t1_ref[...], t2_ref[...], t3_ref[...]), refs[:15], refs[15:])


def _even_params(p):
    def bd(w):
        eye = jnp.eye(LRU_BLOCKS, dtype=w.dtype)
        return (eye[:, None, :, None] * w[:, :, None, :]).reshape(LRU_W, LRU_W).astype(BF16)
    row = lambda v: v.reshape(1, -1).astype(F32)
    wl = jnp.zeros((LORA_PAD, 3 * RWKV_W), F32)
    wl = wl.at[0:W_LORA, 0:RWKV_W].set(p['w2'])
    wl = wl.at[W_LORA:W_LORA + A_LORA, RWKV_W:2 * RWKV_W].set(p['a2'])
    wl = wl.at[W_LORA + A_LORA:W_LORA + A_LORA + G_LORA, 2 * RWKV_W:].set(p['g2'])
    head = np.arange(RWKV_W) // RWKV_HD
    red = jnp.asarray(head[:, None] == np.arange(LANE)[None, :], dtype=BF16)
    mu = jnp.pad(p['mu'], (0, SHIFT_PAD - SHIFT_W))
    return [p['conv_w'].astype(F32), row(p['conv_b']), bd(p['wa']), row(p['ba']), bd(p['wx']), row(p['bx']),
            row(jax.nn.softplus(-p['lam'].astype(F32))), row(mu), row(p['w0']), row(p['a0']), wl.astype(BF16),
            row(p['k_k']), row(p['k_a']), red, red.T]


def _const_spec(a, n_grid):
    return pl.BlockSpec(a.shape, lambda *_: (0,) * a.ndim)


def even_pre_seq(proj, conv0, shift0, prm, B, T):
    tr = EVEN_ROWS
    nt = T // tr
    conv_pad = jnp.pad(conv0.astype(F32), ((0, 0), (8 - (CONV_W - 1), 0), (0, 0)))
    shift_pad = jnp.pad(shift0.astype(F32)[:, None, :], ((0, 0), (7, 0), (0, SHIFT_PAD - SHIFT_W)))
    out_spec = pl.BlockSpec((tr, LRU_W), lambda b, t: (b * nt + t, 0))
    return pl.pallas_call(
        _even_pre_seq_kernel,
        grid=(B, nt),
        in_specs=[pl.BlockSpec((tr, AB_PAD), lambda b, t: (b * nt + t, 0)),
                  pl.BlockSpec((None, 8, LRU_W), lambda b, t: (b, 0, 0)),
                  pl.BlockSpec((None, 8, SHIFT_PAD), lambda b, t: (b, 0, 0))] + [_const_spec(a, 2) for a in prm],
        out_specs=[out_spec] * N_EVEN_PRE_OUT,
        out_shape=[jax.ShapeDtypeStruct((B * T, LRU_W), F32)] * N_EVEN_PRE_OUT,
        scratch_shapes=[pltpu.VMEM((8, LRU_W), F32), pltpu.VMEM((8, SHIFT_PAD), F32)],
        compiler_params=pltpu.CompilerParams(
            dimension_semantics=("parallel", "arbitrary"), vmem_limit_bytes=VMEM_LIMIT_BYTES),
        name="even_pre_seq",
    )(proj, conv_pad, shift_pad, *prm)


def even_pre_step(proj, row0, conv0, shift0, prm):
    n = conv0.shape[0]
    shift_pad = jnp.pad(shift0.astype(F32), ((0, 0), (0, SHIFT_PAD - SHIFT_W)))
    taps = [conv0[:, CONV_W - 1 - j].astype(F32) for j in (1, 2, 3)]
    full = lambda w: pl.BlockSpec((n, w), lambda i: (0, 0))
    return pl.pallas_call(
        _even_pre_step_kernel,
        grid=(1,),
        in_specs=[pl.BlockSpec((n, AB_PAD), lambda i: (row0 // n, 0)), full(SHIFT_PAD)] + [full(LRU_W)] * 3
        + [_const_spec(a, 1) for a in prm],
        out_specs=[full(LRU_W)] * N_EVEN_PRE_OUT,
        out_shape=[jax.ShapeDtypeStruct((n, LRU_W), F32)] * N_EVEN_PRE_OUT,
        compiler_params=pltpu.CompilerParams(
            dimension_semantics=("arbitrary",), vmem_limit_bytes=VMEM_LIMIT_BYTES),
        name="even_pre_step",
    )(proj, shift_pad, *taps, *prm)


def _even_post_kernel(hs_ref, gate_ref, y_ref, r_ref, k_ref, v_ref, g_ref, lng_ref, lnb_ref, rk_ref,
                      red_ref, exp_ref, *rest):
    o_ref = rest[-1]
    y = y_ref[...]
    mu = _head_sum(y, red_ref, exp_ref) * (1.0 / RWKV_HD)
    d = y - mu
    var = _head_sum(d * d, red_ref, exp_ref) * (1.0 / RWKV_HD)
    yn = d * lax.rsqrt(var + 64e-5) * lng_ref[...] + lnb_ref[...]
    bonus = _head_sum(r_ref[...] * k_ref[...] * rk_ref[...], red_ref, exp_ref) * v_ref[...]
    o_ref[:, 0:LRU_W] = (hs_ref[...] * gate_ref[...]).astype(o_ref.dtype)
    o_ref[:, LRU_W:] = ((yn + bonus) * g_ref[...]).astype(o_ref.dtype)


def even_post(hs, gate, y, r, k, v, g, p, red, n_total, row0, prior=None):
    n = hs.shape[0]
    tr = min(POST_ROWS, n)
    row = lambda a: a.reshape(1, -1).astype(F32)
    consts = [row(p['ln_g']), row(p['ln_b']), row(p['r_k']), red, red.T]
    seq = pl.BlockSpec((tr, LRU_W), lambda i: (i, 0))
    args = [hs, gate, y, r, k, v, g] + consts
    in_specs = [seq] * 7 + [_const_spec(a, 1) for a in consts]
    aliases = {}
    if prior is not None:
        args.append(prior)
        in_specs.append(pl.BlockSpec(memory_space=pl.ANY))
        aliases = {len(args) - 1: 0}
    return pl.pallas_call(
        _even_post_kernel,
        grid=(n // tr,),
        in_specs=in_specs,
        out_specs=pl.BlockSpec((tr, D_MODEL), lambda i: (row0 // tr + i, 0)),
        out_shape=jax.ShapeDtypeStruct((n_total, D_MODEL), BF16),
        input_output_aliases=aliases,
        compiler_params=pltpu.CompilerParams(
            dimension_semantics=("parallel",), vmem_limit_bytes=VMEM_LIMIT_BYTES),
        name="even_post",
    )(*args)


def _retention_kernel(q_ref, k_ref, va_ref, vb_ref, dm_ref, rd_ref, kd_ref, sd_ref, o_ref, s_out_ref, s_scr):
    C = q_ref.shape[0]
    n_pair = RET_H // 2

    @pl.when(pl.program_id(1) == 0)
    def _():
        s_scr[...] = jnp.zeros(s_scr.shape, F32)

    lo = lax.broadcasted_iota(jnp.int32, (C, 2 * RET_DK), 1) < RET_DK

    def stack(x):
        return jnp.concatenate([jnp.where(lo, x, 0.0), jnp.where(lo, 0.0, x)], axis=0)

    for p in range(n_pair):
        qk = slice(p * 2 * RET_DK, (p + 1) * 2 * RET_DK)
        q2 = stack(q_ref[:, qk]).astype(BF16)
        k2 = stack(k_ref[:, qk])
        v0 = p * 2 * RET_DV
        v_ref = va_ref if p < n_pair // 2 else vb_ref
        vl = v0 % (RET_W // 2)
        v2 = jnp.concatenate([v_ref[:, vl:vl + RET_DV], v_ref[:, vl + RET_DV:vl + 2 * RET_DV]],
                             axis=0).astype(BF16)
        s = lax.dot_general(q2, k2.astype(BF16), (((1,), (1,)), ((), ())), preferred_element_type=F32) * dm_ref[p]
        s_old = s_scr[p]
        o2 = jnp.dot(s.astype(BF16), v2, preferred_element_type=F32) + jnp.dot(
            q2, s_old.astype(BF16), preferred_element_type=F32) * rd_ref[p]
        o_ref[:, v0:v0 + RET_DV] = o2[:C]
        o_ref[:, v0 + RET_DV:v0 + 2 * RET_DV] = o2[C:]
        s_scr[p] = s_old * sd_ref[p] + lax.dot_general((k2 * kd_ref[p]).astype(BF16), v2, (((0,), (0,)), ((), ())),
                                                       preferred_element_type=F32)

    @pl.when(pl.program_id(1) == pl.num_programs(1) - 1)
    def _():
        s_out_ref[...] = s_scr[...]


def retention_prompt_pallas(rq, rk, rv, B, T, v_col0=0):
    C = RET_CHUNK
    nc = T // C
    f32 = F32
    lg = jnp.log1p(-jnp.exp2(-5.0 - jnp.arange(RET_H, dtype=f32))).reshape(RET_H // 2, 2)
    i = jnp.arange(C, dtype=f32)
    diff = i[:, None] - i[None, :]
    causal = diff >= 0
    dmask = jnp.where(causal, jnp.exp(jnp.where(causal, diff, 0.0)[None, None] * lg[:, :, None, None]), 0.0)
    zero = jnp.zeros_like(dmask[:, 0])
    dm = jnp.concatenate([jnp.concatenate([dmask[:, 0], zero], axis=-1),
                          jnp.concatenate([zero, dmask[:, 1]], axis=-1)], axis=-2)
    rows = lambda x, w: jnp.broadcast_to(x[:, :, :, None], x.shape + (w,)).reshape(RET_H // 2, -1, w)
    rd = rows(jnp.exp((i[None, None, :] + 1.0) * lg[:, :, None]), RET_DV)
    kd = rows(jnp.exp((C - 1.0 - i)[None, None, :] * lg[:, :, None]), 2 * RET_DK)
    sd = rows(jnp.broadcast_to(jnp.exp(C * lg)[:, :, None], (RET_H // 2, 2, RET_DK)), RET_DV)
    half_w = RET_W // 2
    qk_spec = pl.BlockSpec((C, RET_H * RET_DK), lambda b, c: (b * nc + c, 0))
    v_spec = lambda k: pl.BlockSpec((C, half_w), lambda b, c: (b * nc + c, v_col0 // half_w + k))
    const = lambda a: pl.BlockSpec(a.shape, lambda b, c: (0, 0, 0))
    o, s = pl.pallas_call(
        _retention_kernel,
        grid=(B, nc),
        in_specs=[qk_spec, qk_spec, v_spec(0), v_spec(1), const(dm), const(rd), const(kd), const(sd)],
        out_specs=[pl.BlockSpec((C, RET_W), lambda b, c: (b * nc + c, 0)),
                   pl.BlockSpec((None, RET_H // 2, 2 * RET_DK, RET_DV), lambda b, c: (b, 0, 0, 0))],
        out_shape=[jax.ShapeDtypeStruct((B * T, RET_W), f32),
                   jax.ShapeDtypeStruct((B, RET_H // 2, 2 * RET_DK, RET_DV), f32)],
        scratch_shapes=[pltpu.VMEM((RET_H // 2, 2 * RET_DK, RET_DV), f32)],
        compiler_params=pltpu.CompilerParams(
            dimension_semantics=("parallel", "arbitrary"), vmem_limit_bytes=VMEM_LIMIT_BYTES),
        name="retention_prompt",
    )(rq, rk, rv, rv, dm, rd, kd, sd)
    return s.reshape(B, RET_H, RET_DK, RET_DV), o


KV_W = NSA_G * NSA_HD
RET_QK_W = RET_H * RET_DK
OFF_Q = 0
OFF_KC = OFF_Q + NSA_W
OFF_VC = OFF_KC + KV_W
OFF_KS = OFF_VC + KV_W
OFF_VS = OFF_KS + KV_W
OFF_KW = OFF_VS + KV_W
OFF_VW = OFF_KW + KV_W
OFF_RQ = OFF_VW + KV_W
OFF_RK = OFF_RQ + RET_QK_W
OFF_RV = OFF_RK + RET_QK_W
OFF_RG = OFF_RV + RET_W
OFF_GT = OFF_RG + RET_W
CD_PAD = _round_up(OFF_GT + LANE, COL_TILE)
ODD_ROWS = 256
N_ODD_PRE_OUT = 11


def _odd_weight_cols(w):
    gt0 = NSA_W + 6 * KV_W
    body = jnp.concatenate([w[:, :gt0], w[:, gt0 + 3 * NSA_H:]], axis=1)
    gt = w[:, gt0:gt0 + 3 * NSA_H]
    out = jnp.concatenate([body, gt], axis=1)
    return jnp.pad(out, ((0, 0), (0, CD_PAD - out.shape[1]))).astype(BF16)


def _rope_tables(pos, n_rot, theta, head):
    half = n_rot // 2
    inv = jnp.exp(-jnp.log(jnp.float32(theta)) * jnp.arange(half, dtype=jnp.float32) / half)
    ang = pos.astype(jnp.float32)[:, None] * inv[None, :]
    cos, sin = jnp.cos(ang), jnp.sin(ang)
    d = np.arange(LANE) % head
    cos_d, sin_d = cos[:, d % half], sin[:, d % half]
    c = jnp.where(d < n_rot, cos_d, 1.0)
    s1 = jnp.where(d < half, -sin_d, 0.0)
    s2 = jnp.where((d >= half) & (d < n_rot), sin_d, 0.0)
    return jnp.stack([c, s1, s2])


def _rope_lanes(x, tab_ref, half):
    w = x.shape[1]
    rep = w // LANE
    c, s1, s2 = (pltpu.repeat(tab_ref[i], rep, axis=1) for i in range(3))
    return x * c + pltpu.roll(x, w - half, 1) * s1 + pltpu.roll(x, half, 1) * s2


def _rms_heads(x, g_ref, red_ref, exp_ref):
    ms = _head_sum(x * x, red_ref, exp_ref) * (1.0 / NSA_HD)
    return x * lax.rsqrt(ms + 1e-6) * g_ref[...]


def _odd_pre_kernel(x_ref, nsa_tab, ret_tab, qg_ref, ksg_ref, kwg_ref, redq_ref, expq_ref, redk_ref, expk_ref,
                    tile_ref, qn_o, qr_o, ks_o, kw_o, ks4_o, vs4_o, kw4_o, vw4_o, gate_o, rq_o, rk_o, kvt_o, wint_o):
    nsa_half = ROPE_DIMS // 2
    qn = _rms_heads(x_ref[:, OFF_Q:OFF_Q + NSA_W], qg_ref, redq_ref, expq_ref)
    qn_o[...] = qn
    qr_o[...] = _rope_lanes(qn, nsa_tab, nsa_half)
    ks = _rope_lanes(_rms_heads(x_ref[:, OFF_KS:OFF_KS + KV_W], ksg_ref, redk_ref, expk_ref), nsa_tab, nsa_half)
    kw = _rope_lanes(_rms_heads(x_ref[:, OFF_KW:OFF_KW + KV_W], kwg_ref, redk_ref, expk_ref), nsa_tab, nsa_half)
    ks_o[...] = ks
    kw_o[...] = kw
    tile = tile_ref[...]
    for src, dst in ((ks, ks4_o), (x_ref[:, OFF_VS:OFF_VS + KV_W], vs4_o), (kw, kw4_o),
                     (x_ref[:, OFF_VW:OFF_VW + KV_W], vw4_o)):
        dst[...] = jnp.dot(src.astype(BF16), tile, preferred_element_type=F32).astype(BF16)
    gate_o[...] = jax.nn.sigmoid(x_ref[:, OFF_GT:OFF_GT + LANE])
    rq_o[...] = _rope_lanes(x_ref[:, OFF_RQ:OFF_RQ + RET_QK_W], ret_tab, RET_DK // 2)
    rk_o[...] = _rope_lanes(x_ref[:, OFF_RK:OFF_RK + RET_QK_W], ret_tab, RET_DK // 2) * (RET_DK ** -0.5)
    kv_pieces = (x_ref[:, OFF_KC:OFF_KC + KV_W], x_ref[:, OFF_VC:OFF_VC + KV_W], ks, x_ref[:, OFF_VS:OFF_VS + KV_W])
    for dst, pieces in ((kvt_o, kv_pieces), (wint_o, (kw, x_ref[:, OFF_VW:OFF_VW + KV_W]))):
        for s, piece in enumerate(pieces):
            for c in range(KV_W // LANE):
                dst[s * KV_W + c * LANE:s * KV_W + (c + 1) * LANE, :] = piece[:, c * LANE:(c + 1) * LANE].T


def odd_pre(proj, pos, p, row0, n_rows, same_pos, seq_len):
    tr = min(ODD_ROWS, n_rows)
    blk0 = row0 // tr
    n_tab = tr if same_pos else n_rows
    pos_rows = jnp.broadcast_to(pos, (n_tab,)) if same_pos else pos
    nsa_tab = _rope_tables(pos_rows, ROPE_DIMS, ROPE_THETA, NSA_HD)
    ret_tab = _rope_tables(pos_rows, RET_DK, RET_THETA, RET_DK)
    row = lambda v, rep: jnp.tile(v.astype(F32), rep).reshape(1, -1)
    lanes = np.arange(LANE)
    red_q = jnp.asarray((np.arange(NSA_W) // NSA_HD)[:, None] == lanes[None, :], dtype=BF16)
    red_k = jnp.asarray((np.arange(KV_W) // NSA_HD)[:, None] == lanes[None, :], dtype=BF16)
    src = np.arange(KV_W)
    dst = np.arange(NSA_W)
    tile = jnp.asarray((src[:, None] // NSA_HD == dst[None, :] // GROUP_W)
                       & (src[:, None] % NSA_HD == dst[None, :] % NSA_HD), dtype=BF16)
    consts = [row(p['q_norm'], NSA_H), row(p['k_norm'][1], NSA_G), row(p['k_norm'][2], NSA_G),
              red_q, red_q.T, red_k, red_k.T, tile]
    tab_spec = pl.BlockSpec((3, tr, LANE), (lambda i: (0, 0, 0)) if same_pos else (lambda i: (0, i, 0)))
    out = lambda w, dt: (pl.BlockSpec((tr, w), lambda i: (i, 0)), jax.ShapeDtypeStruct((n_rows, w), dt))
    seq_tiles = seq_len // tr
    out_t = lambda r: (pl.BlockSpec((None, r, tr), lambda i: (i // seq_tiles, 0, i % seq_tiles)),
                       jax.ShapeDtypeStruct((n_rows // seq_len, r, seq_len), F32))
    outs = [out(NSA_W, F32), out(NSA_W, F32), out(KV_W, F32), out(KV_W, F32)] + [out(NSA_W, BF16)] * 4 + [
        out(LANE, F32), out(RET_QK_W, F32), out(RET_QK_W, F32), out_t(KV_SLOTS * KV_W), out_t(2 * KV_W)]
    return pl.pallas_call(
        _odd_pre_kernel,
        grid=(n_rows // tr,),
        in_specs=[pl.BlockSpec((tr, CD_PAD), lambda i: (blk0 + i, 0)), tab_spec, tab_spec]
        + [_const_spec(a, 1) for a in consts],
        out_specs=[o[0] for o in outs],
        out_shape=[o[1] for o in outs],
        compiler_params=pltpu.CompilerParams(
            dimension_semantics=("parallel",), vmem_limit_bytes=VMEM_LIMIT_BYTES),
        name="odd_pre",
    )(proj, nsa_tab, ret_tab, *consts)


def _odd_post_kernel(oc_ref, os_ref, ow_ref, gate_ref, ret_ref, rg0_ref, rg1_ref, gng_ref, gnb_ref, ge_ref, *rest,
                     gated):
    o_ref = rest[-1]
    if gated:
        nsa = oc_ref[...]
    else:
        gates = gate_ref[...]
        nsa = jnp.zeros(oc_ref.shape, F32)
        for j, branch in enumerate((oc_ref, os_ref, ow_ref)):
            nsa = nsa + _dot_01(gates, ge_ref[j]) * branch[...]
    o_ref[:, 0:NSA_W] = nsa.astype(o_ref.dtype)
    for h in range(RET_H):
        lanes = slice(h * RET_DV, (h + 1) * RET_DV)
        x = ret_ref[:, lanes]
        mu = jnp.mean(x, axis=-1, keepdims=True)
        d = x - mu
        var = jnp.mean(d * d, axis=-1, keepdims=True)
        yn = d * lax.rsqrt(var + 1e-5) * gng_ref[:, lanes] + gnb_ref[:, lanes]
        rg = (rg0_ref if h < RET_H // 2 else rg1_ref)[:, (h % (RET_H // 2)) * RET_DV:(h % (RET_H // 2) + 1) * RET_DV]
        o_ref[:, NSA_W + h * RET_DV:NSA_W + (h + 1) * RET_DV] = (yn * (rg * jax.nn.sigmoid(rg))).astype(o_ref.dtype)


def odd_post(o_cmp, o_slc, o_win, gates, o_ret, proj, p, n_total, row0, prior=None, gated=False):
    n = o_cmp.shape[0]
    tr = min(POST_ROWS, n)
    blk0 = row0 // tr
    h = np.arange(NSA_W) // NSA_HD
    ge = jnp.asarray(np.stack([(np.arange(LANE)[:, None] == (3 * h + j)[None, :]) for j in range(3)]), dtype=BF16)
    row = lambda a: a.reshape(1, -1).astype(F32)
    consts = [row(p['gn_g']), row(p['gn_b']), ge]
    seq = lambda w: pl.BlockSpec((tr, w), lambda i: (i, 0))
    half = RET_W // 2
    rg_spec = lambda k: pl.BlockSpec((tr, half), lambda i: (blk0 + i, OFF_RG // half + k))
    args = [o_cmp, o_slc, o_win, gates, o_ret, proj, proj] + consts
    in_specs = [seq(NSA_W)] * 3 + [seq(LANE), seq(RET_W), rg_spec(0), rg_spec(1)] + [_const_spec(a, 1) for a in consts]
    aliases = {}
    if prior is not None:
        args.append(prior)
        in_specs.append(pl.BlockSpec(memory_space=pl.ANY))
        aliases = {len(args) - 1: 0}
    return pl.pallas_call(
        functools.partial(_odd_post_kernel, gated=gated),
        grid=(n // tr,),
        in_specs=in_specs,
        out_specs=pl.BlockSpec((tr, D_MODEL), lambda i: (blk0 + i, 0)),
        out_shape=jax.ShapeDtypeStruct((n_total, D_MODEL), BF16),
        input_output_aliases=aliases,
        compiler_params=pltpu.CompilerParams(
            dimension_semantics=("parallel",), vmem_limit_bytes=VMEM_LIMIT_BYTES),
        name="odd_post",
    )(*args)


def rms_norm(x, g, eps=1e-6):
    xf = x.astype(jnp.float32)
    y = xf * lax.rsqrt(jnp.mean(xf * xf, axis=-1, keepdims=True) + eps)
    return (y * g.astype(jnp.float32)).astype(x.dtype)


def head_group_norm(y, g, b, eps):
    yf = y.astype(jnp.float32)
    mu = jnp.mean(yf, axis=-1, keepdims=True)
    var = jnp.mean(jnp.square(yf - mu), axis=-1, keepdims=True)
    yn = ((yf - mu) * lax.rsqrt(var + eps)).reshape(y.shape[:-2] + (-1,))
    return (yn * g.astype(jnp.float32) + b.astype(jnp.float32)).astype(y.dtype)


def masked_softmax(s, mask):
    s = jnp.where(mask, s.astype(jnp.float32), -jnp.inf)
    m = jnp.max(s, axis=-1, keepdims=True)
    e = jnp.exp(s - jnp.where(jnp.isfinite(m), m, 0.0))
    den = jnp.sum(e, axis=-1, keepdims=True)
    return e / jnp.where(den > 0, den, 1.0)


def rope(x, pos, n_rot, theta):
    half = n_rot // 2
    inv = jnp.exp(-jnp.log(jnp.float32(theta)) * jnp.arange(half, dtype=jnp.float32) / half)
    ang = pos.astype(jnp.float32)[:, None] * inv[None, :]
    cos = jnp.cos(ang)[None, :, None, :]
    sin = jnp.sin(ang)[None, :, None, :]
    xf = x.astype(jnp.float32)
    x1, x2 = xf[..., :half], xf[..., half:n_rot]
    out = jnp.concatenate([x1 * cos - x2 * sin, x2 * cos + x1 * sin, xf[..., n_rot:]], axis=-1)
    return out.astype(x.dtype)


def linear_scan(a, b, h0):
    b = b.at[:, 0].add(a[:, 0] * h0)

    def combine(left, right):
        return left[0] * right[0], right[0] * left[1] + right[1]

    return lax.associative_scan(combine, (a, b), axis=1)[1]


def wkv7_scan(r, w, k, v, a, b, s0):
    xs = tuple(jnp.moveaxis(z.astype(jnp.float32), 1, 0) for z in (r, w, k, v, a, b))

    def step(S, inp):
        r_t, w_t, k_t, v_t, a_t, b_t = inp
        sa = jnp.einsum('bhij,bhj->bhi', S, a_t)
        S = S * w_t[:, :, None, :] + sa[..., None] * b_t[:, :, None, :] + v_t[..., None] * k_t[:, :, None, :]
        return S, jnp.einsum('bhij,bhj->bhi', S, r_t)

    S, ys = lax.scan(step, s0.astype(jnp.float32), xs)
    return jnp.moveaxis(ys, 0, 1), S


def even_mixer_core(proj, p, lru_h0, lru_conv0, shift0, wkv0):
    B, T, _ = proj.shape
    f32 = jnp.float32
    dt = proj.dtype
    xb, gb, rw = jnp.split(proj, [LRU_W, 2 * LRU_W], axis=-1)
    xcat = jnp.concatenate([lru_conv0.astype(dt), xb], axis=1)
    xc = p['conv_b'] + sum(p['conv_w'][j] * xcat[:, j:j + T] for j in range(CONV_W))
    xbd = xc.reshape(B, T, LRU_BLOCKS, LRU_BS)
    gate_r = jax.nn.sigmoid(jnp.einsum('btnc,ncd->btnd', xbd, p['wa']).reshape(B, T, LRU_W) + p['ba'])
    gate_i = jax.nn.sigmoid(jnp.einsum('btnc,ncd->btnd', xbd, p['wx']).reshape(B, T, LRU_W) + p['bx'])
    log_a = -LRU_C * gate_r.astype(f32) * jax.nn.softplus(-p['lam'].astype(f32))
    u = jnp.sqrt(-jnp.expm1(2.0 * log_a)) * (gate_i * xc).astype(f32)
    hs = lru_scan(jnp.exp(log_a), u, lru_h0.astype(f32))
    y_lru = hs.astype(dt) * jax.nn.gelu(gb)
    prev = jnp.concatenate([shift0.astype(dt)[:, None], rw[:, :-1]], axis=1)
    rs = rw + p['mu'] * (prev - rw)
    r, k, v, xw, xa, xg = jnp.split(
        rs, [RWKV_W, 2 * RWKV_W, 3 * RWKV_W, 3 * RWKV_W + W_LORA, 3 * RWKV_W + W_LORA + A_LORA], axis=-1)
    w_log = -jax.nn.softplus(-(p['w0'] + jnp.tanh(xw) @ p['w2']).astype(f32)) - 0.5
    log_decay = -jnp.exp(w_log)
    decay = jnp.exp(log_decay)
    a_icl = jax.nn.sigmoid(p['a0'] + xa @ p['a2'])
    g = jax.nn.sigmoid(xg) @ p['g2']
    heads = (B, T, RWKV_H, RWKV_HD)
    kk = (k * p['k_k']).reshape(heads).astype(f32)
    kk = kk / jnp.maximum(jnp.sqrt(jnp.sum(kk * kk, axis=-1, keepdims=True)), 1e-12)
    k = k * (1.0 + (a_icl - 1.0) * p['k_a'])
    rh, kh, vh, ah = (z.reshape(heads) for z in (r, k, v, a_icl))
    if T % WKV_C == 0:
        y, wkv = wkv7_chunked(r.astype(f32), log_decay, k.astype(f32), v.astype(f32),
                              (-kk).reshape(B, T, RWKV_W), (kk * ah.astype(f32)).reshape(B, T, RWKV_W), wkv0)
        y = y.reshape(heads)
    else:
        y, wkv = wkv7_scan(rh, decay.reshape(heads), kh, vh, -kk, kk * ah.astype(f32), wkv0)
    y = head_group_norm(y, p['ln_g'], p['ln_b'], 64e-5).astype(dt)
    bonus = (jnp.sum(rh * kh * p['r_k'], axis=-1, keepdims=True) * vh).reshape(B, T, RWKV_W)
    y_rwkv = (y + bonus) * g
    cat = jnp.concatenate([y_lru, y_rwkv], axis=-1)
    return cat, hs[:, -1], xcat[:, T:], rw[:, -1], wkv


def even_mixer(proj, p, B, T, DB, lru_h0, lru_conv0, shift0, wkv0):
    f32 = F32
    prm = _even_params(p)
    red = prm[-2]
    n_p = B * T
    zeros = lambda *s: jnp.zeros(s, f32)
    a, u, gate, r, lw, k, v, na, nb, g = even_pre_seq(proj, zeros(B, CONV_W - 1, LRU_W), zeros(B, SHIFT_W), prm, B, T)
    seq = lambda z: z.reshape(B, T, LRU_W)
    hs = lru_scan(seq(a), seq(u), zeros(B, LRU_W))
    yw, wkv_p = wkv7_chunked(seq(r), seq(lw), seq(k), seq(v), seq(na), seq(nb), zeros(B, RWKV_H, RWKV_HD, RWKV_HD))
    cat = even_post(hs.reshape(n_p, LRU_W), gate, yw.reshape(n_p, RWKV_W), r, k, v, g, p, red, n_p + DB, 0)
    tail = lambda b, n, c0, c1: proj[(b + 1) * T - n:(b + 1) * T, c0:c1]
    st_p = (hs[:, -1], jnp.stack([tail(b, CONV_W - 1, 0, LRU_W) for b in range(B)]),
            jnp.concatenate([tail(b, 1, 2 * LRU_W, AB_COLS) for b in range(B)], axis=0), wkv_p)
    a, u, gate, r, lw, k, v, na, nb, g = even_pre_step(proj, n_p, lru_conv0, shift0, prm)
    hs_s = a * lru_h0.astype(f32) + u
    heads = (DB, 1, RWKV_H, RWKV_HD)
    yw, wkv_s = wkv7_scan(r.reshape(heads), jnp.exp(lw).reshape(heads), k.reshape(heads), v.reshape(heads),
                          na.reshape(heads), nb.reshape(heads), wkv0)
    cat = even_post(hs_s, gate, yw.reshape(DB, RWKV_W), r, k, v, g, p, red, n_p + DB, n_p, prior=cat)
    xb_s = proj[n_p:]
    conv_s = jnp.concatenate([lru_conv0[:, 1:].astype(f32), xb_s[:, None, :LRU_W]], axis=1)
    st_s = (hs_s, conv_s, xb_s[:, 2 * LRU_W:AB_COLS], wkv_s)
    return cat, st_p, st_s


def odd_project(proj, p, pos):
    B, T, _ = proj.shape
    sizes = [NSA_W] + [NSA_G * NSA_HD] * 6 + [3 * NSA_H, RET_H * RET_DK, RET_H * RET_DK, RET_W, RET_W]
    q, kc, vc, ks, vs, kw, vw, gt, rq, rk, rv, rg = jnp.split(
        proj, np.cumsum(sizes).tolist(), axis=-1)[:len(sizes)]
    kvs = (B, T, NSA_G, NSA_HD)
    q_n = rms_norm(q.reshape(B, T, NSA_H, NSA_HD), p['q_norm'])
    return {
        'q_n': q_n,
        'q_r': rope(q_n, pos, ROPE_DIMS, ROPE_THETA),
        'kc': kc.reshape(kvs), 'vc': vc.reshape(kvs),
        'ks': rope(rms_norm(ks.reshape(kvs), p['k_norm'][1]), pos, ROPE_DIMS, ROPE_THETA),
        'vs': vs.reshape(kvs),
        'kw': rope(rms_norm(kw.reshape(kvs), p['k_norm'][2]), pos, ROPE_DIMS, ROPE_THETA),
        'vw': vw.reshape(kvs),
        'gates': jax.nn.sigmoid(gt).reshape(B, T, NSA_H, 3),
        'rq': rope(rq.reshape(B, T, RET_H, RET_DK), pos, RET_DK, RET_THETA),
        'rk': rope(rk.reshape(B, T, RET_H, RET_DK), pos, RET_DK, RET_THETA) * (RET_DK ** -0.5),
        'rv': rv.reshape(B, T, RET_H, RET_DV),
        'rg': rg,
    }


def to_groups_q(q):
    B, T = q.shape[:2]
    return jnp.moveaxis(q.reshape(B, T, NSA_G, NSA_HPG, NSA_HD), 1, 3)


def to_groups_k(k):
    return jnp.moveaxis(k, 1, 2)


def nsa_compress(x, w1, b1, w2, b2):
    B, L = x.shape[:2]
    n_chunk = L // CMP_STRIDE
    n_cmp = n_chunk - CMP_R + 1
    ch = x[:, :n_chunk * CMP_STRIDE].reshape(B, n_chunk, CMP_STRIDE, NSA_G, NSA_HD)
    ch = jnp.moveaxis(ch, 3, 2).reshape(B, n_chunk, NSA_G, CMP_STRIDE * NSA_HD)
    part = jnp.einsum('bngc,rch->bngrh', ch, w1)
    pre = b1 + sum(part[:, m:m + n_cmp, :, m] for m in range(CMP_R))
    return jax.nn.gelu(pre) @ w2 + b2


def nsa_compressed_branch(qn, kc_raw, vc_raw, p, q_pos):
    kc = to_groups_k(rms_norm(nsa_compress(kc_raw, *p['ck']), p['k_norm'][0]))
    vc = to_groups_k(nsa_compress(vc_raw, *p['cv']))
    s = jnp.einsum('bghqd,bgcd->bghqc', qn, kc) * NSA_HD ** -0.5
    ends = jnp.arange(kc.shape[2]) * CMP_STRIDE + CMP_BLOCK - 1
    prob = masked_softmax(s, ends[None, :] <= q_pos[:, None])
    return jnp.einsum('bghqc,bgcd->bghqd', prob.astype(vc.dtype), vc), prob


def cmp_sel_overlap(n_cmp, n_sel):
    cs = np.arange(n_cmp) * CMP_STRIDE
    ss = np.arange(n_sel) * SEL_BLOCK
    ov = np.minimum(cs[None] + CMP_BLOCK, ss[:, None] + SEL_BLOCK) - np.maximum(cs[None], ss[:, None])
    return jnp.asarray(np.clip(ov, 0, None) / CMP_BLOCK, dtype=jnp.float32)


def nsa_select(p_cmp, q_pos, n_sel):
    imp = jnp.einsum('bgqc,sc->bgqs', p_cmp.sum(axis=2), cmp_sel_overlap(p_cmp.shape[-1], n_sel))
    j = jnp.arange(n_sel)[None, :]
    qb = (q_pos // SEL_BLOCK)[:, None]
    valid = j <= qb
    forced = (j == 0) | (j == qb) | (j == qb - 1)
    score = jnp.where(valid, jnp.where(forced, FORCE_SCORE, imp), -jnp.inf)
    _, idx = lax.top_k(score, min(SEL_TOP, n_sel))
    sel_ok = jnp.take_along_axis(jnp.broadcast_to(valid, score.shape), idx, axis=-1)
    return idx, sel_ok


def sel_blocks(x, n_sel):
    B, L = x.shape[:2]
    x = jnp.pad(x, ((0, 0), (0, n_sel * SEL_BLOCK - L), (0, 0), (0, 0)))
    return jnp.moveaxis(x.reshape(B, n_sel, SEL_BLOCK, NSA_G, NSA_HD), 3, 1)


def nsa_slc_attend(q, kb, vb, idx, sel_ok, q_pos):
    B, G = kb.shape[:2]
    bi = jnp.arange(B)[:, None, None, None]
    gi = jnp.arange(G)[None, :, None, None]
    kg = kb[bi, gi, idx]
    vg = vb[bi, gi, idx]
    s = jnp.einsum('bghqd,bgqnld->bghqnl', q, kg) * NSA_HD ** -0.5
    kpos = idx[..., None] * SEL_BLOCK + jnp.arange(SEL_BLOCK)
    mask = (kpos <= q_pos[None, None, :, None, None]) & sel_ok[..., None]
    sh = s.shape
    prob = masked_softmax(s.reshape(sh[:4] + (-1,)), mask.reshape(B, G, 1, sh[3], -1))
    return jnp.einsum('bghqnl,bgqnld->bghqd', prob.reshape(sh).astype(vg.dtype), vg)


def window_attend_banded(q, k, v):
    B, G, HPG, T, HD = q.shape
    nb = T // WIN_BLOCK
    npv = WINDOW // WIN_BLOCK
    pad = ((0, 0), (0, 0), (npv * WIN_BLOCK, 0), (0, 0))

    def band(z):
        zb = jnp.pad(z, pad).reshape(B, G, nb + npv, WIN_BLOCK, HD)
        return jnp.concatenate([zb[:, :, j:j + nb] for j in range(npv + 1)], axis=3)

    kb, vb = band(k), band(v)
    qb = q.reshape(B, G, HPG, nb, WIN_BLOCK, HD)
    s = jnp.einsum('bghiqd,bgikd->bghiqk', qb, kb) * NSA_HD ** -0.5
    blk = jnp.arange(nb)[:, None]
    q_pos = blk * WIN_BLOCK + jnp.arange(WIN_BLOCK)[None]
    k_pos = (blk - npv) * WIN_BLOCK + jnp.arange((npv + 1) * WIN_BLOCK)[None]
    diff = q_pos[:, :, None] - k_pos[:, None, :]
    mask = (diff >= 0) & (diff < WINDOW) & (k_pos[:, None, :] >= 0)
    prob = masked_softmax(s, mask)
    return jnp.einsum('bghiqk,bgikd->bghiqd', prob.astype(v.dtype), vb).reshape(B, G, HPG, T, HD)


def window_attend_cached(q, k, v, q_pos, k_pos):
    s = jnp.einsum('bghqd,blgd->bghql', q, k) * NSA_HD ** -0.5
    diff = q_pos[:, None] - k_pos[None, :]
    prob = masked_softmax(s, (diff >= 0) & (diff < WINDOW))
    return jnp.einsum('bghql,blgd->bghqd', prob.astype(v.dtype), v)


def retention_chunk(S, q, k, v):
    f32 = jnp.float32
    C = q.shape[1]
    lg = jnp.log1p(-jnp.exp2(-5.0 - jnp.arange(RET_H, dtype=f32)))
    i = jnp.arange(C, dtype=f32)
    diff = i[:, None] - i[None, :]
    causal = diff >= 0
    dmask = jnp.where(causal, jnp.exp(jnp.where(causal, diff, 0.0)[None] * lg[:, None, None]), 0.0)
    qf, kf, vf = q.astype(f32), k.astype(f32), v.astype(f32)
    s = jnp.einsum('bihd,bjhd->bhij', qf, kf) * dmask
    o = jnp.einsum('bhij,bjhe->bihe', s, vf)
    o = o + jnp.einsum('bihd,bhde->bihe', qf, S) * jnp.exp((i[:, None] + 1.0) * lg[None, :])[None, :, :, None]
    k_dec = kf * jnp.exp((C - 1.0 - i)[:, None] * lg[None, :])[None, :, :, None]
    S = S * jnp.exp(C * lg)[None, :, None, None] + jnp.einsum('bjhd,bjhe->bhde', k_dec, vf)
    return S, o


def retention_prompt(q, k, v):
    B, T = q.shape[:2]
    n = T // RET_CHUNK
    xs = tuple(jnp.moveaxis(z.reshape((B, n, RET_CHUNK) + z.shape[2:]), 1, 0) for z in (q, k, v))
    s0 = jnp.zeros((B, RET_H, RET_DK, RET_DV), jnp.float32)
    S, o = lax.scan(lambda S, c: retention_chunk(S, c[0], c[1], c[2]), s0, xs)
    return S, jnp.moveaxis(o, 0, 1).reshape(B, T, RET_H, RET_DV)


def odd_output(o_cmp, o_slc, o_win, o_ret, pr, p):
    gates = pr['gates']
    B, T = gates.shape[:2]
    gg = jnp.moveaxis(gates.reshape(B, T, NSA_G, NSA_HPG, 3), 1, 3)[..., None]
    o = gg[..., 0, :] * o_cmp + gg[..., 1, :] * o_slc + gg[..., 2, :] * o_win
    o_nsa = jnp.moveaxis(o, 3, 1).reshape(B, T, NSA_W)
    y_ret = head_group_norm(o_ret, p['gn_g'], p['gn_b'], 1e-5).astype(o_nsa.dtype) * jax.nn.silu(pr['rg'])
    return jnp.concatenate([o_nsa, y_ret], axis=-1)


def odd_mixer_prompt(proj, p):
    B, T, _ = proj.shape
    pos = jnp.arange(T)
    pr = odd_project(proj, p, pos)
    qn = pr['q_n'].reshape(B, T, NSA_W)
    qr = pr['q_r'].reshape(B, T, NSA_W)
    kc = rms_norm(nsa_compress(pr['kc'], *p['ck']), p['k_norm'][0])
    vc = nsa_compress(pr['vc'], *p['cv'])
    n_cmp = kc.shape[1]
    n_sel = -(-T // SEL_BLOCK)
    o_cmp, sel = nsa_cmp_select(qn, _tile_cmp(kc), _tile_cmp(vc), _overlap_T(n_cmp, n_sel),
                                n_cmp=n_cmp, n_sel=n_sel, q_pos0=0)
    o_slc = nsa_flash(qr, _tile_groups(pr['ks']), _tile_groups(pr['vs']), sel, _sel_expand(T))
    o_win = nsa_flash(qr, _tile_groups(pr['kw']), _tile_groups(pr['vw']))
    S, o_ret = retention_prompt_pallas(pr['rq'].reshape(B, T, -1), pr['rk'].reshape(B, T, -1),
                                       pr['rv'].reshape(B, T, -1))
    o_ret = o_ret.reshape(B, T, RET_H, RET_DV)
    gates = pr['gates']
    heads = (B, T, NSA_H, NSA_HD)
    o_nsa = (gates[..., 0:1] * o_cmp.reshape(heads) + gates[..., 1:2] * o_slc.reshape(heads)
             + gates[..., 2:3] * o_win.reshape(heads)).reshape(B, T, NSA_W)
    y_ret = head_group_norm(o_ret, p['gn_g'], p['gn_b'], 1e-5).astype(o_nsa.dtype) * jax.nn.silu(pr['rg'])
    out = jnp.concatenate([o_nsa, y_ret], axis=-1)
    kv_rows = jnp.stack([pr['kc'], pr['vc'], pr['ks'], pr['vs']], axis=2)
    win = jnp.stack([pr['kw'], pr['vw']], axis=2)[:, T - min(WINDOW, T):]
    return out, kv_rows, win, S


def odd_mixer_sample(proj, p, cache_layer, page_table, win_buf, ret_s0):
    B, T, _ = proj.shape
    assert T == DEC_SEQ == 1 and win_buf.shape[1] == WIN_BUF
    pos = PAST_LEN + jnp.arange(T)
    pr = odd_project(proj, p, pos)
    scale = NSA_HD ** -0.5
    new_rows = jnp.stack([pr['ks'], pr['vs'], pr['kw'], pr['vw']], axis=2)[:, 0].reshape(B, 4, SLOT_ROWS)
    cache_t = jnp.transpose(cache_layer, (0, 2, 3, 4, 1)).reshape(cache_layer.shape[0], KV_ROWS, PAGE_SIZE)
    win_t = jnp.transpose(win_buf, (0, 2, 3, 4, 1)).reshape(B, 2 * SLOT_ROWS, WIN_BUF)
    wk = _dec_cmp_weights(*p['ck'])
    wv = _dec_cmp_weights(*p['cv'])
    w1t, b1, w2t, b2t = (jnp.stack([a, b]) for a, b in zip(wk, wv))
    kn = jnp.tile(p['k_norm'][0], NSA_G).reshape(1, SLOT_ROWS)
    t = np.arange(PAST_LEN)
    expand = jnp.asarray(np.arange(CMP_PAD)[:, None] == (t // SEL_BLOCK)[None, :], dtype=BF16)
    h = np.arange(NSA_H)
    grp = jnp.asarray((h[:, None] // NSA_HPG) == (h[None, :] // NSA_HPG), dtype=BF16)
    o16 = dec_nsa(page_table, cache_t, win_t, _place_heads(pr['q_n'][:, 0] * scale),
                  _place_heads(pr['q_r'][:, 0] * scale), new_rows, pr['gates'][:, 0],
                  w1t, b1, w2t, b2t, kn, _overlap_T(DEC_N_CMP, DEC_N_SEL), expand, grp)
    o_nsa = _take_heads(o16)[:, None, :]
    S, o_ret = retention_chunk(ret_s0.astype(jnp.float32), pr['rq'], pr['rk'], pr['rv'])
    y_ret = head_group_norm(o_ret, p['gn_g'], p['gn_b'], 1e-5).astype(o_nsa.dtype) * jax.nn.silu(pr['rg'])
    out = jnp.concatenate([o_nsa, y_ret], axis=-1)
    rows = jnp.stack([pr['kc'], pr['vc'], pr['ks'], pr['vs']], axis=2).astype(cache_layer.dtype)
    new_col = jnp.stack([pr['kw'], pr['vw']], axis=2)[:, 0].reshape(B, 2 * SLOT_ROWS, 1).astype(win_buf.dtype)
    win_new = jnp.concatenate([win_t[:, :, T:], new_col], axis=2).reshape(B, 2, NSA_G, NSA_HD, WIN_BUF)
    return out, rows, jnp.transpose(win_new, (0, 4, 1, 2, 3)), S


def odd_mixer(proj, p, B, T, DB, cache_layer, page_table, win_buf, ret_s0):
    assert DEC_SEQ == 1 and win_buf.shape[1] == WIN_BUF
    n_p = B * T
    kv = (B, T, NSA_G, NSA_HD)
    cols = lambda rows, off, w: proj[rows, off:off + w]
    prompt = slice(0, n_p)
    dec = slice(n_p, n_p + DB)
    qn, qr, ks, kw, ks4, vs4, kw4, vw4, gates, rq, rk, kv_t, win_t_p = odd_pre(
        proj, jnp.tile(jnp.arange(T), B), p, 0, n_p, False, T)
    seq = lambda z: z.reshape(B, T, -1)
    kc_raw = cols(prompt, OFF_KC, KV_W).reshape(kv)
    vc_raw = cols(prompt, OFF_VC, KV_W).reshape(kv)
    kc = rms_norm(nsa_compress(kc_raw, *p['ck']), p['k_norm'][0])
    vc = nsa_compress(vc_raw, *p['cv'])
    n_cmp = kc.shape[1]
    n_sel = -(-T // SEL_BLOCK)
    o_cmp, sel = nsa_cmp_select(seq(qn), _tile_cmp(kc), _tile_cmp(vc), _overlap_T(n_cmp, n_sel),
                                n_cmp=n_cmp, n_sel=n_sel, q_pos0=0)
    o_slc = nsa_flash(seq(qr), seq(ks4), seq(vs4), sel, _sel_expand(T))
    o_win = nsa_flash(seq(qr), seq(kw4), seq(vw4))
    ret_p, o_ret = retention_prompt_pallas(rq, rk, proj, B, T, v_col0=OFF_RV)
    flat = lambda z: z.reshape(n_p, -1)
    cat = odd_post(flat(o_cmp), flat(o_slc), flat(o_win), gates, o_ret, proj, p, n_p + DB, 0)
    n_win = min(WINDOW, T)
    kv_rows_p = jnp.transpose(kv_t.reshape(B, KV_SLOTS, NSA_G, NSA_HD, T), (0, 4, 1, 2, 3))
    win_p = jnp.transpose(win_t_p[:, :, T - n_win:].reshape(B, 2, NSA_G, NSA_HD, n_win), (0, 4, 1, 2, 3))
    qn, qr, ks, kw, _, _, _, _, gates, rq, rk, kv_t, _ = odd_pre(proj, jnp.asarray(PAST_LEN), p, n_p, DB, True, DB)
    scale = NSA_HD ** -0.5
    heads = lambda z: z.reshape(DB, NSA_H, NSA_HD)
    vs, vw = cols(dec, OFF_VS, KV_W), cols(dec, OFF_VW, KV_W)
    new_rows = jnp.stack([ks, vs, kw, vw], axis=1)
    cache_t = jnp.transpose(cache_layer, (0, 2, 3, 4, 1)).reshape(cache_layer.shape[0], KV_ROWS, PAGE_SIZE)
    win_t = jnp.transpose(win_buf, (0, 2, 3, 4, 1)).reshape(DB, 2 * SLOT_ROWS, WIN_BUF)
    w1t, b1, w2t, b2t = (jnp.stack([a, b]) for a, b in zip(_dec_cmp_weights(*p['ck']), _dec_cmp_weights(*p['cv'])))
    kn = jnp.tile(p['k_norm'][0], NSA_G).reshape(1, SLOT_ROWS)
    t = np.arange(PAST_LEN)
    expand = jnp.asarray(np.arange(CMP_PAD)[:, None] == (t // SEL_BLOCK)[None, :], dtype=BF16)
    h = np.arange(NSA_H)
    grp = jnp.asarray((h[:, None] // NSA_HPG) == (h[None, :] // NSA_HPG), dtype=BF16)
    o16 = dec_nsa(page_table, cache_t, win_t, _place_heads(heads(qn) * scale), _place_heads(heads(qr) * scale),
                  new_rows, gates[:, :3 * NSA_H].reshape(DB, NSA_H, 3),
                  w1t, b1, w2t, b2t, kn, _overlap_T(DEC_N_CMP, DEC_N_SEL), expand, grp)
    o_nsa = _take_heads(o16)
    ret_s, o_ret = retention_chunk(ret_s0.astype(F32), rq.reshape(DB, 1, RET_H, RET_DK),
                                   rk.reshape(DB, 1, RET_H, RET_DK), cols(dec, OFF_RV, RET_W).reshape(DB, 1, RET_H, RET_DV))
    cat = odd_post(o_nsa, o_nsa, o_nsa, gates, o_ret.reshape(DB, RET_W), proj, p, n_p + DB, n_p, prior=cat, gated=True)
    rows_s = jnp.transpose(kv_t.reshape(DEC_SEQ, KV_SLOTS, NSA_G, NSA_HD, DB), (4, 0, 1, 2, 3)).astype(cache_layer.dtype)
    new_row = jnp.concatenate([kw, vw], axis=1)[:, None, :].astype(win_buf.dtype)
    win_new = win_shift(win_t, new_row).reshape(DB, 2, NSA_G, NSA_HD, WIN_BUF)
    win_s = jnp.transpose(win_new, (0, 4, 1, 2, 3))
    return cat, (kv_rows_p, win_p, ret_p), (rows_s, win_s, ret_s)


def _stack(xs, dt):
    return jnp.stack(xs).astype(dt)


def kernel(x_prompt, x_sample, state_lru_h, state_lru_conv, state_rwkv_shift, state_rwkv_wkv,
           cache_nsa_kv, cache_nsa_win, state_ret, page_table,
           norm_ffn1, ffn1_w_in, ffn1_w_out, norm_mix, norm_ffn2, ffn2_w_in, ffn2_w_out,
           ab_w_in, lru_conv_w, lru_conv_b, lru_wa, lru_ba, lru_wx, lru_bx, lru_lambda,
           rwkv_mu, rwkv_w0, rwkv_w2, rwkv_a0, rwkv_a2, rwkv_g2, rwkv_k_k, rwkv_k_a, rwkv_r_k,
           rwkv_ln_g, rwkv_ln_b, ab_w_out,
           cd_w_in, nsa_q_norm, nsa_k_norm, cmp_k_w1, cmp_k_b1, cmp_k_w2, cmp_k_b2,
           cmp_v_w1, cmp_v_b1, cmp_v_w2, cmp_v_b2, ret_gn_g, ret_gn_b, cd_w_out):
    dt = x_prompt.dtype
    B = x_prompt.shape[0]
    DB = x_sample.shape[0]
    y = jnp.concatenate([x_prompt.reshape(N_PROMPT, D_MODEL), x_sample.reshape(DB * DEC_SEQ, D_MODEL)], axis=0)
    lru_h_p, lru_h_s, lru_c_p, lru_c_s, sh_p, sh_s, wkv_p, wkv_s = [], [], [], [], [], [], [], []
    kv_p, kv_s, win_p, win_s, ret_p, ret_s = [], [], [], [], [], []
    for layer in range(DEPTH):
        li = layer // 2
        y = ffn_block(y, norm_ffn1[layer], *_prep_ffn_weights(ffn1_w_in, ffn1_w_out, layer))
        if layer % 2 == 0:
            p = {'conv_w': lru_conv_w[li], 'conv_b': lru_conv_b[li],
                 'wa': lru_wa[li], 'ba': lru_ba[li], 'wx': lru_wx[li], 'bx': lru_bx[li], 'lam': lru_lambda[li],
                 'mu': rwkv_mu[li], 'w0': rwkv_w0[li], 'w2': rwkv_w2[li], 'a0': rwkv_a0[li], 'a2': rwkv_a2[li],
                 'g2': rwkv_g2[li], 'k_k': rwkv_k_k[li], 'k_a': rwkv_k_a[li], 'r_k': rwkv_r_k[li],
                 'ln_g': rwkv_ln_g[li], 'ln_b': rwkv_ln_b[li]}
            proj = norm_matmul(y, norm_mix[layer], _prep_cols(ab_w_in[li], WIDE_COL_TILE), tn=WIDE_COL_TILE)
            cat, (a0, a1, a2, a3), (b0, b1, b2, b3) = even_mixer(
                proj, p, B, SEQ, DB, state_lru_h[li], state_lru_conv[li], state_rwkv_shift[li], state_rwkv_wkv[li])
            lru_h_p.append(a0); lru_c_p.append(a1); sh_p.append(a2); wkv_p.append(a3)
            lru_h_s.append(b0); lru_c_s.append(b1); sh_s.append(b2); wkv_s.append(b3)
            w_out = ab_w_out[li]
        else:
            p = {'q_norm': nsa_q_norm[li], 'k_norm': nsa_k_norm[li],
                 'ck': (cmp_k_w1[li], cmp_k_b1[li], cmp_k_w2[li], cmp_k_b2[li]),
                 'cv': (cmp_v_w1[li], cmp_v_b1[li], cmp_v_w2[li], cmp_v_b2[li]),
                 'gn_g': ret_gn_g[li], 'gn_b': ret_gn_b[li]}
            proj = norm_matmul(y, norm_mix[layer], _odd_weight_cols(cd_w_in[li]), tn=WIDE_COL_TILE)
            cat, (a0, a1, a2), (b0, b1, b2) = odd_mixer(
                proj, p, B, SEQ, DB, cache_nsa_kv[li], page_table, cache_nsa_win[li], state_ret[li])
            kv_p.append(a0); win_p.append(a1); ret_p.append(a2)
            kv_s.append(b0); win_s.append(b1); ret_s.append(b2)
            w_out = cd_w_out[li]
        y = matmul_residual(cat, w_out.astype(BF16), y, tn=WIDE_COL_TILE)
        y = ffn_block(y, norm_ffn2[layer], *_prep_ffn_weights(ffn2_w_in, ffn2_w_out, layer))
    yp = y[:N_PROMPT].reshape(B, SEQ, D_MODEL)
    ys = y[N_PROMPT:].reshape(DB, DEC_SEQ, D_MODEL)
    return (yp, ys,
            _stack(lru_h_p, dt), _stack(lru_h_s, dt), _stack(lru_c_p, dt), _stack(lru_c_s, dt),
            _stack(sh_p, dt), _stack(sh_s, dt), _stack(wkv_p, dt), _stack(wkv_s, dt),
            _stack(kv_p, dt), _stack(kv_s, dt), _stack(win_p, dt), _stack(win_s, dt),
            _stack(ret_p, dt), _stack(ret_s, dt))
```

```python
import functools

import jax
import jax.numpy as jnp
import numpy as np
from jax import lax
from jax.experimental import pallas as pl
from jax.experimental.pallas import tpu as pltpu

D_MODEL = 2048
BATCH = 4
SEQ = 2048
DEPTH = 2
DEC_BATCH = 128
DEC_SEQ = 1
PAST_LEN = 2048
PAGE_SIZE = 128
D_FF = 5504
LRU_W = D_MODEL // 2
LRU_BLOCKS = 16
LRU_BS = LRU_W // LRU_BLOCKS
CONV_W = 4
LRU_C = 8.0
RWKV_W = D_MODEL // 2
RWKV_HD = 64
RWKV_H = RWKV_W // RWKV_HD
W_LORA = 64
A_LORA = 64
G_LORA = 160
SHIFT_W = 3 * RWKV_W + W_LORA + A_LORA + G_LORA
AB_COLS = 2 * LRU_W + SHIFT_W
NSA_H = 16
NSA_G = 4
NSA_HPG = NSA_H // NSA_G
NSA_HD = 64
NSA_W = NSA_H * NSA_HD
ROPE_DIMS = NSA_HD // 4
ROPE_THETA = 500000.0
CMP_BLOCK = 32
CMP_STRIDE = 16
CMP_R = CMP_BLOCK // CMP_STRIDE
CMP_HID = 256
SEL_BLOCK = 64
SEL_TOP = 16
SEL_Q_BLOCK = 64
WINDOW = 512
WIN_BLOCK = 128
FORCE_SCORE = 1e4
KV_SLOTS = 4
RET_H = 8
RET_DK = 64
RET_DV = 128
RET_W = RET_H * RET_DV
RET_CHUNK = 128
RET_THETA = 10000.0
CD_COLS = NSA_W + 6 * NSA_G * NSA_HD + 3 * NSA_H + 2 * RET_H * RET_DK + 2 * RET_W

N_TOK = BATCH * SEQ + DEC_BATCH * DEC_SEQ
N_PROMPT = BATCH * SEQ

LANE = 128
VMEM_LIMIT_BYTES = 56 * 1024 * 1024
ROW_TILE = 640
FF_TILE = 512
D_FF_PAD = 5632
COL_TILE = 512
WIDE_COL_TILE = 2048

BF16 = jnp.bfloat16
F32 = jnp.float32


def _round_up(n, m):
    return -(-n // m) * m


def _rms_rows(x, g):
    ms = jnp.mean(x * x, axis=-1, keepdims=True)
    return x * lax.rsqrt(ms + 1e-6) * g


def _ffn_kernel(*refs, tail_in, tail_out):
    refs = list(refs)
    x_ref = refs.pop(0)
    xt_ref = refs.pop(0) if tail_in else None
    g_ref, wg_ref, wu_ref, wo_ref, o_ref = refs[:5]
    ot_ref = refs[5] if tail_out else None
    xn_ref, acc_ref = refs[-2:]
    i = pl.program_id(0)
    k = pl.program_id(1)
    last_tile = i == pl.num_programs(0) - 1

    def start(x):
        xn_ref[...] = _rms_rows(x, g_ref[...]).astype(BF16)
        acc_ref[...] = 2.0 * x

    if tail_in:
        n_tail = xt_ref.shape[0]

        @pl.when((k == 0) & last_tile)
        def _():
            start(jnp.concatenate([x_ref[0:x_ref.shape[0] - n_tail, :], xt_ref[...]], axis=0))

        @pl.when((k == 0) & jnp.logical_not(last_tile))
        def _():
            start(x_ref[...])
    else:
        @pl.when(k == 0)
        def _():
            start(x_ref[...])

    xn = xn_ref[...]
    gate = jnp.dot(xn, wg_ref[...], preferred_element_type=F32)
    up = jnp.dot(xn, wu_ref[...], preferred_element_type=F32)
    act = gate * jax.nn.sigmoid(gate) * up
    acc_ref[...] += jnp.dot(act.astype(BF16), wo_ref[...], preferred_element_type=F32)

    @pl.when(k == pl.num_programs(1) - 1)
    def _():
        o_ref[...] = 0.5 * acc_ref[...]

    if tail_out:
        n_tail = ot_ref.shape[0]

        @pl.when((k == pl.num_programs(1) - 1) & last_tile)
        def _():
            ot_ref[...] = 0.5 * acc_ref[acc_ref.shape[0] - n_tail:, :]


def ffn_block(x, g, wg, wu, wo, x_tail=None, n_tail_out=0):
    d = x.shape[1]
    m = x.shape[0] + (0 if x_tail is None else x_tail.shape[0])
    assert m % ROW_TILE == 0
    row_spec = pl.BlockSpec((ROW_TILE, d), lambda i, k: (i, 0))
    in_specs = [row_spec]
    args = [x]
    if x_tail is not None:
        in_specs.append(pl.BlockSpec(x_tail.shape, lambda i, k: (0, 0)))
        args.append(x_tail)
    in_specs += [
        pl.BlockSpec((1, d), lambda i, k: (0, 0)),
        pl.BlockSpec((d, FF_TILE), lambda i, k: (0, k)),
        pl.BlockSpec((d, FF_TILE), lambda i, k: (0, k)),
        pl.BlockSpec((FF_TILE, d), lambda i, k: (k, 0)),
    ]
    args += [g.reshape(1, d), wg, wu, wo]
    if n_tail_out:
        out_specs = [row_spec, pl.BlockSpec((n_tail_out, d), lambda i, k: (0, 0))]
        out_shape = [jax.ShapeDtypeStruct((m - n_tail_out, d), F32), jax.ShapeDtypeStruct((n_tail_out, d), F32)]
    else:
        out_specs = row_spec
        out_shape = jax.ShapeDtypeStruct((m, d), F32)
    return pl.pallas_call(
        functools.partial(_ffn_kernel, tail_in=x_tail is not None, tail_out=bool(n_tail_out)),
        grid=(m // ROW_TILE, D_FF_PAD // FF_TILE),
        in_specs=in_specs,
        out_specs=out_specs,
        out_shape=out_shape,
        scratch_shapes=[pltpu.VMEM((ROW_TILE, d), BF16), pltpu.VMEM((ROW_TILE, d), F32)],
        compiler_params=pltpu.CompilerParams(
            dimension_semantics=("arbitrary", "arbitrary"), vmem_limit_bytes=VMEM_LIMIT_BYTES),
        name="ffn_block",
    )(*args)


def _norm_matmul_kernel(x_ref, g_ref, w_ref, o_ref, xn_ref):
    @pl.when(pl.program_id(1) == 0)
    def _():
        xn_ref[...] = _rms_rows(x_ref[...], g_ref[...]).astype(BF16)

    o_ref[...] = jnp.dot(xn_ref[...], w_ref[...], preferred_element_type=F32)


def norm_matmul(x, g, w, tn=COL_TILE):
    m, k = x.shape
    n = w.shape[1]
    return pl.pallas_call(
        _norm_matmul_kernel,
        grid=(m // ROW_TILE, n // tn),
        in_specs=[
            pl.BlockSpec((ROW_TILE, k), lambda i, j: (i, 0)),
            pl.BlockSpec((1, k), lambda i, j: (0, 0)),
            pl.BlockSpec((k, tn), lambda i, j: (0, j)),
        ],
        out_specs=pl.BlockSpec((ROW_TILE, tn), lambda i, j: (i, j)),
        out_shape=jax.ShapeDtypeStruct((m, n), F32),
        scratch_shapes=[pltpu.VMEM((ROW_TILE, k), BF16)],
        compiler_params=pltpu.CompilerParams(
            dimension_semantics=("parallel", "arbitrary"), vmem_limit_bytes=VMEM_LIMIT_BYTES),
        name="norm_matmul",
    )(x, g.reshape(1, k), w)


def _matmul_residual_kernel(a_ref, w_ref, r_ref, o_ref):
    o_ref[...] = r_ref[...] + jnp.dot(a_ref[...].astype(BF16), w_ref[...], preferred_element_type=F32)


def matmul_residual(a, w, res, tn=COL_TILE):
    m, k = a.shape
    n = w.shape[1]
    return pl.pallas_call(
        _matmul_residual_kernel,
        grid=(m // ROW_TILE, n // tn),
        in_specs=[
            pl.BlockSpec((ROW_TILE, k), lambda i, j: (i, 0)),
            pl.BlockSpec((k, tn), lambda i, j: (0, j)),
            pl.BlockSpec((ROW_TILE, tn), lambda i, j: (i, j)),
        ],
        out_specs=pl.BlockSpec((ROW_TILE, tn), lambda i, j: (i, j)),
        out_shape=jax.ShapeDtypeStruct((m, n), F32),
        compiler_params=pltpu.CompilerParams(
            dimension_semantics=("parallel", "arbitrary"), vmem_limit_bytes=VMEM_LIMIT_BYTES),
        name="matmul_residual",
    )(a, w, res)


WCAST_ROWS = 256
WCAST_COLS = 512


def _cast_w_in_kernel(w_ref, wg_ref, wu_ref):
    pad = jnp.zeros((w_ref.shape[0], D_FF_PAD - D_FF), BF16)
    wg_ref[:, :D_FF] = w_ref[:, :D_FF].astype(BF16)
    wg_ref[:, D_FF:] = pad
    wu_ref[:, :D_FF] = w_ref[:, D_FF:].astype(BF16)
    wu_ref[:, D_FF:] = pad


def _cast_w_out_kernel(w_ref, wo_ref):
    wo_ref[:D_FF, :] = w_ref[...].astype(BF16)
    wo_ref[D_FF:, :] = jnp.zeros((D_FF_PAD - D_FF, w_ref.shape[1]), BF16)


def _prep_ffn_weights(w_in, w_out, layer):
    d = w_in.shape[1]
    wg, wu = pl.pallas_call(
        _cast_w_in_kernel,
        grid=(d // WCAST_ROWS,),
        in_specs=[pl.BlockSpec((None, WCAST_ROWS, 2 * D_FF), lambda i: (layer, i, 0))],
        out_specs=[pl.BlockSpec((WCAST_ROWS, D_FF_PAD), lambda i: (i, 0))] * 2,
        out_shape=[jax.ShapeDtypeStruct((d, D_FF_PAD), BF16)] * 2,
        compiler_params=pltpu.CompilerParams(dimension_semantics=("parallel",), vmem_limit_bytes=VMEM_LIMIT_BYTES),
        name="cast_w_in",
    )(w_in)
    wo = pl.pallas_call(
        _cast_w_out_kernel,
        grid=(d // WCAST_COLS,),
        in_specs=[pl.BlockSpec((None, D_FF, WCAST_COLS), lambda j: (layer, 0, j))],
        out_specs=pl.BlockSpec((D_FF_PAD, WCAST_COLS), lambda j: (0, j)),
        out_shape=jax.ShapeDtypeStruct((D_FF_PAD, d), BF16),
        compiler_params=pltpu.CompilerParams(dimension_semantics=("parallel",), vmem_limit_bytes=VMEM_LIMIT_BYTES),
        name="cast_w_out",
    )(w_out)
    return wg, wu, wo


def _prep_cols(w, tile):
    n = w.shape[1]
    return jnp.pad(w, ((0, 0), (0, _round_up(n, tile) - n))).astype(BF16)


SCAN_TILE = 256


def _lru_scan_kernel(a_ref, b_ref, h0_ref, o_ref, carry_ref):
    @pl.when(pl.program_id(1) == 0)
    def _():
        carry_ref[...] = h0_ref[...]

    a = a_ref[...]
    b = b_ref[...]
    rows = lax.broadcasted_iota(jnp.int32, a.shape, 0)
    k = 1
    while k < a.shape[0]:
        keep = rows >= k
        b = jnp.where(keep, a * pltpu.roll(b, k, 0) + b, b)
        a = jnp.where(keep, a * pltpu.roll(a, k, 0), a)
        k *= 2
    h = a * carry_ref[...] + b
    o_ref[...] = h
    carry_ref[...] = h[a.shape[0] - 1:, :]


def lru_scan(a, b, h0):
    B, T, W = a.shape
    tt = min(SCAN_TILE, T)
    return pl.pallas_call(
        _lru_scan_kernel,
        grid=(B, T // tt),
        in_specs=[
            pl.BlockSpec((None, tt, W), lambda i, t: (i, t, 0)),
            pl.BlockSpec((None, tt, W), lambda i, t: (i, t, 0)),
            pl.BlockSpec((None, 1, W), lambda i, t: (i, 0, 0)),
        ],
        out_specs=pl.BlockSpec((None, tt, W), lambda i, t: (i, t, 0)),
        out_shape=jax.ShapeDtypeStruct((B, T, W), F32),
        scratch_shapes=[pltpu.VMEM((1, W), F32)],
        compiler_params=pltpu.CompilerParams(
            dimension_semantics=("parallel", "arbitrary"), vmem_limit_bytes=VMEM_LIMIT_BYTES),
        name="lru_scan",
    )(a, b, h0.reshape(B, 1, W))


GROUP_W = NSA_HPG * NSA_HD
ATT_Q_TILE = 256
ATT_Q_TILE_WIN = 128
ATT_K_TILE = 256
ATT_K_TILE_WIN = 256
CMP_PAD = 128
NEG_BIG = -1e30


def _stack_heads(q):
    head = lax.broadcasted_iota(jnp.int32, q.shape, 1) // NSA_HD
    return jnp.concatenate([jnp.where(head == h, q, 0.0) for h in range(NSA_HPG)], axis=0)


def _unstack_heads(o, tq):
    head = lax.broadcasted_iota(jnp.int32, (tq, GROUP_W), 1) // NSA_HD
    out = jnp.zeros((tq, GROUP_W), F32)
    for h in range(NSA_HPG):
        out = out + jnp.where(head == h, o[h * tq:(h + 1) * tq], 0.0)
    return out


def _cmp_select_kernel(q_ref, k_ref, v_ref, ov_ref, o_ref, sel_ref, *, n_cmp, n_sel, q_pos0):
    tq = q_ref.shape[0]
    i = pl.program_id(2)
    qs = _stack_heads(q_ref[...] * (NSA_HD ** -0.5)).astype(BF16)
    s = lax.dot_general(qs, k_ref[...], (((1,), (1,)), ((), ())), preferred_element_type=F32)
    q_pos = q_pos0 + i * tq + lax.broadcasted_iota(jnp.int32, (tq, CMP_PAD), 0)
    c = lax.broadcasted_iota(jnp.int32, (tq, CMP_PAD), 1)
    mask1 = (c < n_cmp) & (c * CMP_STRIDE + (CMP_BLOCK - 1) <= q_pos)
    mask = jnp.concatenate([mask1] * NSA_HPG, axis=0)
    s = jnp.where(mask, s, NEG_BIG)
    m = jnp.max(s, axis=-1, keepdims=True)
    e = jnp.where(mask, jnp.exp(s - m), 0.0)
    den = jnp.sum(e, axis=-1, keepdims=True)
    prob = e / jnp.where(den > 0, den, 1.0)
    o = jnp.dot(prob.astype(BF16), v_ref[...], preferred_element_type=F32)
    o_ref[...] = _unstack_heads(o, tq)
    psum = prob[0:tq]
    for h in range(1, NSA_HPG):
        psum = psum + prob[h * tq:(h + 1) * tq]
    imp = jnp.dot(psum.astype(BF16), ov_ref[...], preferred_element_type=F32)
    qb = q_pos // SEL_BLOCK
    valid = (c <= qb) & (c < n_sel)
    forced = (c == 0) | (c == qb) | (c == qb - 1)
    score = jnp.where(valid, jnp.where(forced, FORCE_SCORE, imp), -jnp.inf)
    k_top = min(SEL_TOP, n_sel)
    few_blocks = (q_pos0 + (i + 1) * tq - 1) // SEL_BLOCK < k_top

    @pl.when(few_blocks)
    def _():
        sel_ref[...] = jnp.where(valid, 1.0, 0.0)

    @pl.when(jnp.logical_not(few_blocks))
    def _():
        rank = jnp.zeros((tq, CMP_PAD), F32)
        for jp in range(n_sel):
            col = score[:, jp:jp + 1]
            beats = (col > score) | ((col == score) & (c > jp))
            rank = rank + jnp.where(beats, 1.0, 0.0)
        sel_ref[...] = jnp.where((rank < k_top) & (c < n_sel), 1.0, 0.0)


def nsa_cmp_select(qn, kc4, vc4, ovT, *, n_cmp, n_sel, q_pos0):
    B, T, _ = qn.shape
    tq = min(ATT_Q_TILE, T)
    return pl.pallas_call(
        functools.partial(_cmp_select_kernel, n_cmp=n_cmp, n_sel=n_sel, q_pos0=q_pos0),
        grid=(B, NSA_G, T // tq),
        in_specs=[
            pl.BlockSpec((None, tq, GROUP_W), lambda b, g, i: (b, i, g)),
            pl.BlockSpec((None, None, CMP_PAD, GROUP_W), lambda b, g, i: (b, g, 0, 0)),
            pl.BlockSpec((None, None, CMP_PAD, GROUP_W), lambda b, g, i: (b, g, 0, 0)),
            pl.BlockSpec((CMP_PAD, CMP_PAD), lambda b, g, i: (0, 0)),
        ],
        out_specs=[
            pl.BlockSpec((None, tq, GROUP_W), lambda b, g, i: (b, i, g)),
            pl.BlockSpec((None, None, tq, CMP_PAD), lambda b, g, i: (b, g, i, 0)),
        ],
        out_shape=[jax.ShapeDtypeStruct((B, T, NSA_W), F32),
                   jax.ShapeDtypeStruct((B, NSA_G, T, CMP_PAD), F32)],
        compiler_params=pltpu.CompilerParams(
            dimension_semantics=("parallel", "parallel", "parallel"), vmem_limit_bytes=VMEM_LIMIT_BYTES),
        name="nsa_cmp_select",
    )(qn, kc4, vc4, ovT)


def _flash_kernel(*refs, selected):
    if selected:
        q_ref, k_ref, v_ref, sel_ref, exp_ref, o_ref, m_ref, l_ref, acc_ref, s_a, s_b = refs
    else:
        q_ref, k_ref, v_ref, o_ref, m_ref, l_ref, acc_ref, s_a, s_b = refs
    tq = q_ref.shape[0]
    tk = s_a.shape[1]
    n_tiles = k_ref.shape[0] // tk
    i = pl.program_id(2)
    q = q_ref[...] * (NSA_HD ** -0.5)
    head = lax.broadcasted_iota(jnp.int32, q.shape, 1) // NSA_HD
    q4 = _stack_heads(q).astype(BF16)
    m_ref[...] = jnp.full(m_ref.shape, NEG_BIG, F32)
    l_ref[...] = jnp.zeros(l_ref.shape, F32)
    acc_ref[...] = jnp.zeros(acc_ref.shape, F32)
    q_pos = i * tq + lax.broadcasted_iota(jnp.int32, (tq, tk), 0)
    col = lax.broadcasted_iota(jnp.int32, (tq, tk), 1)
    if selected:
        sel = sel_ref[...].astype(BF16)
        lo = 0
    else:
        lo = jnp.maximum(i * tq - (WINDOW - 1), 0) // tk
    hi = (i * tq + tq - 1) // tk + 1

    def tile_start(j):
        return pl.multiple_of(jnp.minimum(j, n_tiles - 1) * tk, tk)

    def scores(j, s_ref):
        s_ref[...] = lax.dot_general(q4, k_ref[pl.ds(tile_start(j), tk), :], (((1,), (1,)), ((), ())),
                                     preferred_element_type=F32)

    def consume(j, s_ref):
        v = v_ref[pl.ds(tile_start(j), tk), :]
        k_pos = j * tk + col
        mask = k_pos <= q_pos
        if selected:
            mask = mask & (jnp.dot(sel, exp_ref[jnp.minimum(j, n_tiles - 1)], preferred_element_type=F32) > 0.5)
        else:
            mask = mask & (q_pos - k_pos < WINDOW)
        if not selected:
            bias = jnp.where(mask, 0.0, 2.0 * NEG_BIG)
        for h in range(NSA_HPG):
            m_old = m_ref[h]
            if selected:
                s = jnp.where(mask, s_ref[h * tq:(h + 1) * tq, :], NEG_BIG)
                m_new = jnp.maximum(m_old, jnp.max(s, axis=-1, keepdims=True))
                p = jnp.where(mask, jnp.exp(s - pltpu.repeat(m_new, tk // LANE, axis=1)), 0.0)
            else:
                s = s_ref[h * tq:(h + 1) * tq, :] + bias
                m_new = jnp.maximum(m_old, jnp.max(s, axis=-1, keepdims=True))
                p = jnp.exp(s - pltpu.repeat(m_new, tk // LANE, axis=1))
            alpha = jnp.exp(m_old - m_new)
            l_ref[h] = alpha * l_ref[h] + jnp.sum(p, axis=-1, keepdims=True)
            acc_ref[h] = (pltpu.repeat(alpha, GROUP_W // LANE, axis=1) * acc_ref[h]
                          + jnp.dot(p.astype(BF16), v, preferred_element_type=F32))
            m_ref[h] = m_new

    scores(lo, s_a)

    def body(t, carry):
        j = lo + 2 * t
        scores(j + 1, s_b)
        consume(j, s_a)
        scores(j + 2, s_a)
        consume(j + 1, s_b)
        return carry

    lax.fori_loop(0, (hi - lo + 1) // 2, body, 0)
    out = jnp.zeros((tq, GROUP_W), F32)
    for h in range(NSA_HPG):
        den = pltpu.repeat(l_ref[h], GROUP_W // LANE, axis=1)
        out = out + jnp.where(head == h, acc_ref[h] / jnp.where(den > 0, den, 1.0), 0.0)
    o_ref[...] = out


def nsa_flash(qr, k4, v4, sel=None, expand=None):
    B, T, _ = qr.shape
    selected = sel is not None
    tq = ATT_Q_TILE if selected else ATT_Q_TILE_WIN
    tk = ATT_K_TILE if selected else ATT_K_TILE_WIN
    in_specs = [
        pl.BlockSpec((None, tq, GROUP_W), lambda b, g, i: (b, i, g)),
        pl.BlockSpec((None, T, GROUP_W), lambda b, g, i: (b, 0, g)),
        pl.BlockSpec((None, T, GROUP_W), lambda b, g, i: (b, 0, g)),
    ]
    args = [qr, k4, v4]
    if selected:
        in_specs += [
            pl.BlockSpec((None, None, tq, CMP_PAD), lambda b, g, i: (b, g, i, 0)),
            pl.BlockSpec(expand.shape, lambda b, g, i: (0, 0, 0)),
        ]
        args += [sel, expand]
    return pl.pallas_call(
        functools.partial(_flash_kernel, selected=selected),
        grid=(B, NSA_G, T // tq),
        in_specs=in_specs,
        out_specs=pl.BlockSpec((None, tq, GROUP_W), lambda b, g, i: (b, i, g)),
        out_shape=jax.ShapeDtypeStruct((B, T, NSA_W), F32),
        scratch_shapes=[pltpu.VMEM((NSA_HPG, tq, LANE), F32), pltpu.VMEM((NSA_HPG, tq, LANE), F32),
                        pltpu.VMEM((NSA_HPG, tq, GROUP_W), F32),
                        pltpu.VMEM((NSA_HPG * tq, tk), F32), pltpu.VMEM((NSA_HPG * tq, tk), F32)],
        compiler_params=pltpu.CompilerParams(
            dimension_semantics=("parallel", "parallel", "parallel"), vmem_limit_bytes=VMEM_LIMIT_BYTES),
        name="nsa_flash_sel" if selected else "nsa_flash_win",
    )(*args)


def _tile_groups(x):
    B, T = x.shape[:2]
    return jnp.broadcast_to(x[:, :, :, None, :], (B, T, NSA_G, NSA_HPG, NSA_HD)).reshape(B, T, NSA_W).astype(BF16)


def _tile_cmp(x):
    B, n = x.shape[:2]
    x = jnp.pad(jnp.moveaxis(x, 1, 2), ((0, 0), (0, 0), (0, CMP_PAD - n), (0, 0)))
    return jnp.tile(x, (1, 1, 1, NSA_HPG)).astype(BF16)


def _overlap_T(n_cmp, n_sel):
    ov = np.zeros((CMP_PAD, CMP_PAD), np.float32)
    cs = np.arange(n_cmp) * CMP_STRIDE
    ss = np.arange(n_sel) * SEL_BLOCK
    o = np.minimum(cs[None] + CMP_BLOCK, ss[:, None] + SEL_BLOCK) - np.maximum(cs[None], ss[:, None])
    ov[:n_cmp, :n_sel] = (np.clip(o, 0, None) / CMP_BLOCK).T
    return jnp.asarray(ov, dtype=BF16)


def _sel_expand(T):
    t = np.arange(T)
    e = (np.arange(CMP_PAD)[:, None] == (t // SEL_BLOCK)[None, :]).astype(np.float32)
    return jnp.asarray(e.reshape(CMP_PAD, T // ATT_K_TILE, ATT_K_TILE).transpose(1, 0, 2), dtype=BF16)


KV_ROWS = KV_SLOTS * NSA_G * NSA_HD
SLOT_ROWS = NSA_G * NSA_HD
N_PAGES = PAST_LEN // PAGE_SIZE
DEC_N_CHUNK = (PAST_LEN + DEC_SEQ) // CMP_STRIDE
DEC_N_CMP = DEC_N_CHUNK - CMP_R + 1
DEC_N_SEL = -(-(PAST_LEN + DEC_SEQ) // SEL_BLOCK)
WIN_BUF = min(WINDOW, PAST_LEN)


def _softmax_rows(s, mask, s_new=None):
    s = jnp.where(mask, s, NEG_BIG)
    m = jnp.max(s, axis=-1, keepdims=True)
    if s_new is not None:
        m = jnp.maximum(m, s_new)
    e = jnp.where(mask, jnp.exp(s - m), 0.0)
    den = jnp.sum(e, axis=-1, keepdims=True)
    if s_new is None:
        return e, den
    e_new = jnp.exp(s_new - m)
    return e, e_new, den + e_new


def _dec_nsa_kernel(pt_ref, *refs):
    pages = refs[:N_PAGES]
    (win_ref, qn_ref, qr_ref, new_ref, gate_ref, w1_ref, b1_ref, w2_ref, b2_ref, kn_ref,
     ov_ref, exp_ref, grp_ref, perm_ref, o_ref, xt_ref, acc_ref) = refs[N_PAGES:]
    del pt_ref
    f32 = F32
    half = 2 * NSA_HD
    n_chunk = DEC_N_CHUNK

    perm = perm_ref[...]
    per_page = PAGE_SIZE // CMP_STRIDE
    for p in range(N_PAGES):
        for sg in range(4):
            tile = pages[p][sg * half:(sg + 1) * half, :].astype(BF16)
            xt = lax.dot_general(perm, tile, (((1,), (1,)), ((), ())), preferred_element_type=f32)
            for r in range(CMP_STRIDE):
                xt_ref[sg, r, p * per_page:(p + 1) * per_page, :] = xt[r * per_page:(r + 1) * per_page, :]

    lane_lo = lax.broadcasted_iota(jnp.int32, (n_chunk, half), 1) < NSA_HD
    lane_grp = lax.broadcasted_iota(jnp.int32, (n_chunk, SLOT_ROWS), 1) // NSA_HD
    cmp_rows = []
    for slot in range(2):
        for gp in range(2):
            los, his = [], []
            for rp in range(CMP_STRIDE // 2):
                x0, x1 = (xt_ref[slot * 2 + gp, 2 * rp + j] for j in range(2))
                los.append(jnp.where(lane_lo, x0, pltpu.roll(x1, NSA_HD, 1)))
                his.append(jnp.where(lane_lo, pltpu.roll(x0, NSA_HD, 1), x1))
            lhs = jnp.concatenate([jnp.concatenate(los, axis=1), jnp.concatenate(his, axis=1)], axis=0).astype(BF16)
            acc_ref[pl.ds(gp * 2 * n_chunk, 2 * n_chunk), :] = jnp.dot(lhs, w1_ref[slot], preferred_element_type=f32)
        acc = acc_ref[...]
        pre = b1_ref[slot] + acc[:, :CMP_HID] + pltpu.roll(acc[:, CMP_HID:], NSA_G * n_chunk - 1, 0)
        out = jnp.dot(jax.nn.gelu(pre).astype(BF16), w2_ref[slot], preferred_element_type=f32) + b2_ref[slot]
        if slot == 0:
            out = _rms_rows(out, kn_ref[...])
        sel_rows = jnp.zeros((n_chunk, SLOT_ROWS), f32)
        for g in range(NSA_G):
            sel_rows = sel_rows + jnp.where(lane_grp == g, out[g * n_chunk:(g + 1) * n_chunk], 0.0)
        cmp_rows.append(sel_rows.astype(BF16))
    kc, vc = cmp_rows

    qn = qn_ref[...].astype(BF16)
    qr = qr_ref[...].astype(BF16)
    nt = (((1,), (1,)), ((), ()))
    c = lax.broadcasted_iota(jnp.int32, (NSA_H, CMP_PAD), 1)
    s = lax.dot_general(qn, kc, nt, preferred_element_type=f32)
    e, den = _softmax_rows(s, c < DEC_N_CMP)
    prob = e / jnp.where(den > 0, den, 1.0)
    o_cmp = jnp.dot(prob.astype(BF16), vc, preferred_element_type=f32)
    p_hi, p_mid = _split_bf16(prob)
    p_lo = (prob - p_hi.astype(f32) - p_mid.astype(f32)).astype(BF16)
    grp = grp_ref[...]
    psum = (jnp.dot(grp, p_hi, preferred_element_type=f32) + jnp.dot(grp, p_mid, preferred_element_type=f32)
            + jnp.dot(grp, p_lo, preferred_element_type=f32))
    imp = jnp.dot(psum.astype(BF16), ov_ref[...], preferred_element_type=f32)
    qb = (PAST_LEN + DEC_SEQ - 1) // SEL_BLOCK
    valid = c <= qb
    forced = (c == 0) | (c == qb) | (c == qb - 1)
    score = jnp.where(valid, jnp.where(forced, FORCE_SCORE, imp), -jnp.inf)
    rank = jnp.zeros((NSA_H, CMP_PAD), f32)
    for jp in range(DEC_N_SEL):
        col = score[:, jp:jp + 1]
        rank = rank + jnp.where((col > score) | ((col == score) & (c > jp)), 1.0, 0.0)
    sel = jnp.where((rank < min(SEL_TOP, DEC_N_SEL)) & (c < DEC_N_SEL), 1.0, 0.0).astype(BF16)

    new = new_ref[...]
    new_b = new.astype(BF16).astype(f32)
    qr_f = qr.astype(f32)
    s_pages = [jnp.dot(qr, pages[p][2 * SLOT_ROWS:3 * SLOT_ROWS, :].astype(BF16), preferred_element_type=f32)
               for p in range(N_PAGES)]
    s = jnp.concatenate(s_pages, axis=1)
    mask = jnp.dot(sel, exp_ref[...], preferred_element_type=f32) > 0.5
    s_new = jnp.sum(qr_f * new_b[0:1], axis=-1, keepdims=True)
    e, e_new, den = _softmax_rows(s, mask, s_new)
    e = e.astype(BF16)
    o_slc = e_new.astype(BF16).astype(f32) * new_b[1:2]
    for p in range(N_PAGES):
        o_slc = o_slc + lax.dot_general(e[:, p * PAGE_SIZE:(p + 1) * PAGE_SIZE],
                                        pages[p][3 * SLOT_ROWS:4 * SLOT_ROWS, :].astype(BF16), nt,
                                        preferred_element_type=f32)
    o_slc = o_slc / den

    s = jnp.dot(qr, win_ref[0:SLOT_ROWS, :].astype(BF16), preferred_element_type=f32)
    i_buf = lax.broadcasted_iota(jnp.int32, (NSA_H, WIN_BUF), 1)
    s_new = jnp.sum(qr_f * new_b[2:3], axis=-1, keepdims=True)
    e, e_new, den = _softmax_rows(s, WIN_BUF - i_buf < WINDOW, s_new)
    o_win = e_new.astype(BF16).astype(f32) * new_b[3:4] + lax.dot_general(
        e.astype(BF16), win_ref[SLOT_ROWS:2 * SLOT_ROWS, :].astype(BF16), nt, preferred_element_type=f32)
    o_win = o_win / den

    gates = gate_ref[...]
    o_ref[...] = gates[:, 0:1] * o_cmp + gates[:, 1:2] * o_slc + gates[:, 2:3] * o_win


def dec_nsa(page_table, cache_t, win_t, qn16, qr16, new_rows, gates, w1t, b1, w2t, b2t, kn, ovT, expand, grp):
    DB = qn16.shape[0]
    per_page = PAGE_SIZE // CMP_STRIDE
    tok = np.arange(PAGE_SIZE)
    perm = jnp.asarray((tok[:, None] // per_page == tok[None, :] % CMP_STRIDE)
                       & (tok[:, None] % per_page == tok[None, :] // CMP_STRIDE), dtype=BF16)
    const = lambda shape: pl.BlockSpec(shape, lambda b, pt: (0,) * len(shape))
    per_b = lambda shape: pl.BlockSpec((None,) + shape, lambda b, pt: (b,) + (0,) * len(shape))
    page_specs = [pl.BlockSpec((None, KV_ROWS, PAGE_SIZE), functools.partial(lambda b, pt, p: (pt[b, p], 0, 0), p=p))
                  for p in range(N_PAGES)]
    in_specs = page_specs + [
        per_b((2 * SLOT_ROWS, WIN_BUF)), per_b((NSA_H, SLOT_ROWS)), per_b((NSA_H, SLOT_ROWS)),
        per_b((4, SLOT_ROWS)), per_b((NSA_H, 3)),
        const(w1t.shape), const(b1.shape), const(w2t.shape), const(b2t.shape), const(kn.shape),
        const(ovT.shape), const(expand.shape), const(grp.shape), const(perm.shape),
    ]
    grid_spec = pltpu.PrefetchScalarGridSpec(
        num_scalar_prefetch=1, grid=(DB,), in_specs=in_specs,
        out_specs=pl.BlockSpec((None, NSA_H, SLOT_ROWS), lambda b, pt: (b, 0, 0)),
        scratch_shapes=[pltpu.VMEM((4, CMP_STRIDE, DEC_N_CHUNK, 2 * NSA_HD), F32),
                        pltpu.VMEM((NSA_G * DEC_N_CHUNK, CMP_R * CMP_HID), F32)])
    return pl.pallas_call(
        _dec_nsa_kernel,
        grid_spec=grid_spec,
        out_shape=jax.ShapeDtypeStruct((DB, NSA_H, SLOT_ROWS), F32),
        compiler_params=pltpu.CompilerParams(
            dimension_semantics=("arbitrary",), vmem_limit_bytes=VMEM_LIMIT_BYTES),
        name="dec_nsa",
    )(page_table, *([cache_t] * N_PAGES), win_t, qn16, qr16, new_rows, gates, w1t, b1, w2t, b2t, kn, ovT, expand, grp,
      perm)


WIN_SEQS_PER_STEP = 4


def _win_shift_kernel(win_ref, new_ref, o_ref):
    shape = win_ref.shape[1:]
    n = shape[1]
    row = lax.broadcasted_iota(jnp.int32, shape, 0)
    lane = lax.broadcasted_iota(jnp.int32, shape, 1)
    for s in range(win_ref.shape[0]):
        w = win_ref[s]
        col = jnp.sum(jnp.where(row == lane, jnp.broadcast_to(new_ref[s], shape), 0.0), axis=1, keepdims=True)
        o_ref[s] = jnp.where(lane == n - 1, col, pltpu.roll(w, n - 1, 1))


def win_shift(win_t, new_row):
    DB, R, W = win_t.shape
    ns = WIN_SEQS_PER_STEP
    assert R == W and DB % ns == 0
    return pl.pallas_call(
        _win_shift_kernel,
        grid=(DB // ns,),
        in_specs=[pl.BlockSpec((ns, R, W), lambda b: (b, 0, 0)), pl.BlockSpec((ns, 1, R), lambda b: (b, 0, 0))],
        out_specs=pl.BlockSpec((ns, R, W), lambda b: (b, 0, 0)),
        out_shape=jax.ShapeDtypeStruct((DB, R, W), win_t.dtype),
        compiler_params=pltpu.CompilerParams(dimension_semantics=("parallel",), vmem_limit_bytes=VMEM_LIMIT_BYTES),
        name="win_shift",
    )(win_t, new_row)


def _dec_cmp_weights(w1, b1, w2, b2):
    w = jnp.moveaxis(w1, 0, 1).reshape(CMP_STRIDE * NSA_HD, CMP_R * CMP_HID)
    return (w.astype(BF16), b1.reshape(1, CMP_HID), jnp.tile(w2, (1, NSA_G)).astype(BF16),
            jnp.tile(b2, NSA_G).reshape(1, SLOT_ROWS))


def _place_heads(q):
    own = (jnp.arange(NSA_H)[:, None] // NSA_HPG) == jnp.arange(NSA_G)[None, :]
    return jnp.where(own[None, :, :, None], q[:, :, None, :], 0.0).reshape(q.shape[0], NSA_H, SLOT_ROWS)


def _take_heads(o):
    o = o.reshape(o.shape[0], NSA_H, NSA_G, NSA_HD)
    return o[:, jnp.arange(NSA_H), jnp.arange(NSA_H) // NSA_HPG, :].reshape(o.shape[0], NSA_W)


WKV_C = 64
WKV_PAIR = 2 * RWKV_HD
WKV_T_TILE = 512
WKV_PAIRS_PER_STEP = 8


def _split_bf16(x):
    hi = x.astype(BF16)
    return hi, (x - hi.astype(F32)).astype(BF16)


def _dot3(a, b):
    a_hi, a_lo = _split_bf16(a)
    b_hi, b_lo = _split_bf16(b)
    return (jnp.dot(a_hi, b_hi, preferred_element_type=F32) + jnp.dot(a_hi, b_lo, preferred_element_type=F32)
            + jnp.dot(a_lo, b_hi, preferred_element_type=F32))


def _wkv_kernel(r_ref, lw_ref, k_ref, v_ref, a_ref, b_ref, s0_ref, y_ref, sT_ref, s_scr):
    C = WKV_C
    P = WKV_PAIR
    n_chunks = r_ref.shape[0] // C

    @pl.when(pl.program_id(2) == 0)
    def _():
        s_scr[...] = s0_ref[...]

    lo_lane = lax.broadcasted_iota(jnp.int32, (C, P), 1) < RWKV_HD
    row = lax.broadcasted_iota(jnp.int32, (2 * C, 2 * C), 0)
    col = lax.broadcasted_iota(jnp.int32, (2 * C, 2 * C), 1)
    same_head = (row // C) == (col // C)
    strict = same_head & (row > col)
    lower = same_head & (row >= col)
    eye = jnp.where(row == col, 1.0, 0.0)
    tril = jnp.where(lax.broadcasted_iota(jnp.int32, (C, C), 0) >= lax.broadcasted_iota(jnp.int32, (C, C), 1),
                     1.0, 0.0).astype(BF16)

    def stack(x):
        return jnp.concatenate([jnp.where(lo_lane, x, 0.0), jnp.where(lo_lane, 0.0, x)], axis=0)

    def chunk(c, carry):
        stages = [pair_chunk(c, q) for q in range(WKV_PAIRS_PER_STEP)]
        while stages:
            stages = [g for g in stages if next(g, True) is None]
        return carry

    def pair_chunk(c, q):
        sl = pl.ds(pl.multiple_of(c * C, C), C)
        lanes = slice(q * P, (q + 1) * P)
        r, lw, k, v, a, b = (ref[sl, lanes] for ref in (r_ref, lw_ref, k_ref, v_ref, a_ref, b_ref))
        lw_hi, lw_mid = _split_bf16(lw)
        lw_lo = (lw - lw_hi.astype(F32) - lw_mid.astype(F32)).astype(BF16)
        cs = (jnp.dot(tril, lw_hi, preferred_element_type=F32) + jnp.dot(tril, lw_mid, preferred_element_type=F32)
              + jnp.dot(tril, lw_lo, preferred_element_type=F32))
        yield
        g_inv = jnp.exp(-cs)
        g_end = jnp.exp(cs[C - 1:C, :] - cs)
        a2 = stack(a * jnp.exp(cs - lw))
        r2 = stack(r * jnp.exp(cs))
        b2 = stack(b * g_inv)
        k2 = stack(k * g_inv)
        v2 = stack(v)
        s_old = s_scr[q]
        ar = jnp.concatenate([a2, r2], axis=0).astype(BF16)
        bk = jnp.concatenate([b2, k2], axis=0).astype(BF16)
        nt = (((1,), (1,)), ((), ()))
        pp = lax.dot_general(ar, bk, nt, preferred_element_type=F32)
        from_state = lax.dot_general(ar, s_old.astype(BF16), nt, preferred_element_type=F32)
        yield
        l_ab = jnp.where(strict, pp[:2 * C, :2 * C], 0.0)
        l_ak = jnp.where(strict, pp[:2 * C, 2 * C:], 0.0)
        m_rb = jnp.where(lower, pp[2 * C:, :2 * C], 0.0)
        m_rk = jnp.where(lower, pp[2 * C:, 2 * C:], 0.0)
        v2b = v2.astype(BF16)
        rhs = from_state[:2 * C] + jnp.dot(l_ak.astype(BF16), v2b, preferred_element_type=F32)
        yield
        n = l_ab
        x = eye + n
        span = 2
        while span < C:
            n = _dot3(n, n)
            yield
            x = x + _dot3(n, x)
            yield
            span *= 2
        u2 = _dot3(x, rhs)
        yield
        uv = jnp.concatenate([u2, v2], axis=0).astype(BF16)
        y2 = from_state[2 * C:] + jnp.dot(jnp.concatenate([m_rb, m_rk], axis=1).astype(BF16), uv,
                                          preferred_element_type=F32)
        yield
        y_ref[sl, lanes] = y2[:C] + y2[C:]
        bk_end = jnp.concatenate([stack(b * g_end), stack(k * g_end)], axis=0).astype(BF16)
        s_scr[q] = s_old * jnp.exp(cs[C - 1:C, :]) + lax.dot_general(
            uv, bk_end, (((0,), (0,)), ((), ())), preferred_element_type=F32)

    lax.fori_loop(0, n_chunks, chunk, 0)

    @pl.when(pl.program_id(2) == pl.num_programs(2) - 1)
    def _():
        sT_ref[...] = s_scr[...]


def wkv7_chunked(r, lw, k, v, a, b, s0):
    B, T, W = r.shape
    n_pair = W // WKV_PAIR
    tt = min(WKV_T_TILE, T)
    s0p = s0.astype(F32).reshape(B, n_pair, 2, RWKV_HD, RWKV_HD)
    zero = jnp.zeros_like(s0p[:, :, 0])
    s0_bd = jnp.concatenate([jnp.concatenate([s0p[:, :, 0], zero], axis=-1),
                             jnp.concatenate([zero, s0p[:, :, 1]], axis=-1)], axis=-2)
    pps = WKV_PAIRS_PER_STEP
    seq = pl.BlockSpec((None, tt, pps * WKV_PAIR), lambda i, p, t: (i, t, p))
    st = pl.BlockSpec((None, pps, WKV_PAIR, WKV_PAIR), lambda i, p, t: (i, p, 0, 0))
    y, s_bd = pl.pallas_call(
        _wkv_kernel,
        grid=(B, n_pair // pps, T // tt),
        in_specs=[seq] * 6 + [st],
        out_specs=[seq, st],
        out_shape=[jax.ShapeDtypeStruct((B, T, W), F32),
                   jax.ShapeDtypeStruct((B, n_pair, WKV_PAIR, WKV_PAIR), F32)],
        scratch_shapes=[pltpu.VMEM((pps, WKV_PAIR, WKV_PAIR), F32)],
        compiler_params=pltpu.CompilerParams(
            dimension_semantics=("parallel", "parallel", "arbitrary"), vmem_limit_bytes=VMEM_LIMIT_BYTES),
        name="wkv7_chunked",
    )(r, lw, k, v, a, b, s0_bd)
    s_fin = jnp.stack([s_bd[:, :, :RWKV_HD, :RWKV_HD], s_bd[:, :, RWKV_HD:, RWKV_HD:]], axis=2)
    return y, s_fin.reshape(B, W // RWKV_HD, RWKV_HD, RWKV_HD)


AB_PAD = _round_up(AB_COLS, WIDE_COL_TILE)
SHIFT_PAD = _round_up(SHIFT_W, LANE)
LORA_PAD = SHIFT_PAD - 3 * RWKV_W
EVEN_ROWS = 256
POST_ROWS = 256
N_EVEN_PRE_OUT = 10


def _split3(x):
    hi = x.astype(BF16)
    r1 = x - hi.astype(F32)
    mid = r1.astype(BF16)
    return hi, mid, (r1 - mid.astype(F32)).astype(BF16)


def _dot_01(x, m):
    return sum(jnp.dot(part, m, preferred_element_type=F32) for part in _split3(x))


def _head_sum(x, red_ref, exp_ref):
    return _dot_01(_dot_01(x, red_ref[...]), exp_ref[...])


def _expm1(x):
    u = jnp.exp(x)
    d = u - 1.0
    log_u = jnp.where((d == 0.0) | (d == -1.0), 1.0, jnp.log(u))
    return jnp.where(d == 0.0, x, jnp.where(d == -1.0, -1.0, d * x / log_u))


def _even_pre_math(x_ref, prev, taps, prm, outs):
    (cw_ref, cb_ref, wa_ref, ba_ref, wx_ref, bx_ref, lam_ref, mu_ref, w0_ref, a0_ref, wl_ref,
     kk_ref, ka_ref, red_ref, exp_ref) = prm
    a_o, u_o, gate_o, r_o, lw_o, k_o, v_o, na_o, nb_o, g_o = outs
    t1, t2, t3 = taps
    xb = x_ref[:, 0:LRU_W]
    xc = cb_ref[...] + cw_ref[0:1] * t3 + cw_ref[1:2] * t2 + cw_ref[2:3] * t1 + cw_ref[3:4] * xb
    xcb = xc.astype(BF16)
    gate_r = jax.nn.sigmoid(jnp.dot(xcb, wa_ref[...], preferred_element_type=F32) + ba_ref[...])
    gate_i = jax.nn.sigmoid(jnp.dot(xcb, wx_ref[...], preferred_element_type=F32) + bx_ref[...])
    log_a = -LRU_C * gate_r * lam_ref[...]
    a_o[...] = jnp.exp(log_a)
    u_o[...] = jnp.sqrt(-_expm1(2.0 * log_a)) * (gate_i * xc)
    gate_o[...] = jax.nn.gelu(x_ref[:, LRU_W:2 * LRU_W])
    rw = x_ref[:, 2 * LRU_W:2 * LRU_W + SHIFT_PAD]
    rs = rw + mu_ref[...] * (prev - rw)
    r_o[...] = rs[:, 0:RWKV_W]
    k = rs[:, RWKV_W:2 * RWKV_W]
    v_o[...] = rs[:, 2 * RWKV_W:3 * RWKV_W]
    tail = rs[:, 3 * RWKV_W:]
    lane = lax.broadcasted_iota(jnp.int32, tail.shape, 1)
    act = jnp.where(lane < W_LORA, jnp.tanh(tail), jnp.where(lane < W_LORA + A_LORA, tail, jax.nn.sigmoid(tail)))
    z = jnp.dot(act.astype(BF16), wl_ref[...], preferred_element_type=F32)
    w_log = -jax.nn.softplus(-(w0_ref[...] + z[:, 0:RWKV_W])) - 0.5
    lw_o[...] = -jnp.exp(w_log)
    a_icl = jax.nn.sigmoid(a0_ref[...] + z[:, RWKV_W:2 * RWKV_W])
    g_o[...] = z[:, 2 * RWKV_W:]
    kk = k * kk_ref[...]
    kk = kk / jnp.maximum(jnp.sqrt(_head_sum(kk * kk, red_ref, exp_ref)), 1e-12)
    k_o[...] = k * (1.0 + (a_icl - 1.0) * ka_ref[...])
    na_o[...] = -kk
    nb_o[...] = kk * a_icl


def _even_pre_seq_kernel(x_ref, conv0_ref, shift0_ref, *refs):
    prm = refs[:15]
    outs = refs[15:15 + N_EVEN_PRE_OUT]
    conv_c, shift_c = refs[15 + N_EVEN_PRE_OUT:]
    rows = x_ref.shape[0]

    @pl.when(pl.program_id(1) == 0)
    def _():
        conv_c[...] = conv0_ref[...]
        shift_c[...] = shift0_ref[...]

    xb = x_ref[:, 0:LRU_W]
    row = lax.broadcasted_iota(jnp.int32, xb.shape, 0)
    taps = []
    for j in (1, 2, 3):
        tap = pltpu.roll(xb, j, 0)
        for i in range(j):
            tap = jnp.where(row == i, conv_c[8 - j + i:9 - j + i, :], tap)
        taps.append(tap)
    rw = x_ref[:, 2 * LRU_W:2 * LRU_W + SHIFT_PAD]
    row_w = lax.broadcasted_iota(jnp.int32, rw.shape, 0)
    prev = jnp.where(row_w == 0, shift_c[7:8, :], pltpu.roll(rw, 1, 0))
    _even_pre_math(x_ref, prev, taps, prm, outs)
    conv_c[...] = x_ref[rows - 8:rows, 0:LRU_W]
    shift_c[...] = x_ref[rows - 8:rows, 2 * LRU_W:2 * LRU_W + SHIFT_PAD]


def _even_pre_step_kernel(x_ref, prev_ref, t1_ref, t2_ref, t3_ref, *refs):
    _even_pre_math(x_ref, prev_ref[...], (t1_ref[...], t2_ref[...], t3_ref[...]), refs[:15], refs[15:])


def _even_params(p):
    def bd(w):
        eye = jnp.eye(LRU_BLOCKS, dtype=w.dtype)
        return (eye[:, None, :, None] * w[:, :, None, :]).reshape(LRU_W, LRU_W).astype(BF16)
    row = lambda v: v.reshape(1, -1).astype(F32)
    wl = jnp.zeros((LORA_PAD, 3 * RWKV_W), F32)
    wl = wl.at[0:W_LORA, 0:RWKV_W].set(p['w2'])
    wl = wl.at[W_LORA:W_LORA + A_LORA, RWKV_W:2 * RWKV_W].set(p['a2'])
    wl = wl.at[W_LORA + A_LORA:W_LORA + A_LORA + G_LORA, 2 * RWKV_W:].set(p['g2'])
    head = np.arange(RWKV_W) // RWKV_HD
    red = jnp.asarray(head[:, None] == np.arange(LANE)[None, :], dtype=BF16)
    mu = jnp.pad(p['mu'], (0, SHIFT_PAD - SHIFT_W))
    return [p['conv_w'].astype(F32), row(p['conv_b']), bd(p['wa']), row(p['ba']), bd(p['wx']), row(p['bx']),
            row(jax.nn.softplus(-p['lam'].astype(F32))), row(mu), row(p['w0']), row(p['a0']), wl.astype(BF16),
            row(p['k_k']), row(p['k_a']), red, red.T]


def _const_spec(a, n_grid):
    return pl.BlockSpec(a.shape, lambda *_: (0,) * a.ndim)


def even_pre_seq(proj, conv0, shift0, prm, B, T):
    tr = EVEN_ROWS
    nt = T // tr
    conv_pad = jnp.pad(conv0.astype(F32), ((0, 0), (8 - (CONV_W - 1), 0), (0, 0)))
    shift_pad = jnp.pad(shift0.astype(F32)[:, None, :], ((0, 0), (7, 0), (0, SHIFT_PAD - SHIFT_W)))
    out_spec = pl.BlockSpec((tr, LRU_W), lambda b, t: (b * nt + t, 0))
    return pl.pallas_call(
        _even_pre_seq_kernel,
        grid=(B, nt),
        in_specs=[pl.BlockSpec((tr, AB_PAD), lambda b, t: (b * nt + t, 0)),
                  pl.BlockSpec((None, 8, LRU_W), lambda b, t: (b, 0, 0)),
                  pl.BlockSpec((None, 8, SHIFT_PAD), lambda b, t: (b, 0, 0))] + [_const_spec(a, 2) for a in prm],
        out_specs=[out_spec] * N_EVEN_PRE_OUT,
        out_shape=[jax.ShapeDtypeStruct((B * T, LRU_W), F32)] * N_EVEN_PRE_OUT,
        scratch_shapes=[pltpu.VMEM((8, LRU_W), F32), pltpu.VMEM((8, SHIFT_PAD), F32)],
        compiler_params=pltpu.CompilerParams(
            dimension_semantics=("parallel", "arbitrary"), vmem_limit_bytes=VMEM_LIMIT_BYTES),
        name="even_pre_seq",
    )(proj, conv_pad, shift_pad, *prm)


def even_pre_step(proj, row0, conv0, shift0, prm):
    n = conv0.shape[0]
    shift_pad = jnp.pad(shift0.astype(F32), ((0, 0), (0, SHIFT_PAD - SHIFT_W)))
    taps = [conv0[:, CONV_W - 1 - j].astype(F32) for j in (1, 2, 3)]
    full = lambda w: pl.BlockSpec((n, w), lambda i: (0, 0))
    return pl.pallas_call(
        _even_pre_step_kernel,
        grid=(1,),
        in_specs=[pl.BlockSpec((n, AB_PAD), lambda i: (row0 // n, 0)), full(SHIFT_PAD)] + [full(LRU_W)] * 3
        + [_const_spec(a, 1) for a in prm],
        out_specs=[full(LRU_W)] * N_EVEN_PRE_OUT,
        out_shape=[jax.ShapeDtypeStruct((n, LRU_W), F32)] * N_EVEN_PRE_OUT,
        compiler_params=pltpu.CompilerParams(
            dimension_semantics=("arbitrary",), vmem_limit_bytes=VMEM_LIMIT_BYTES),
        name="even_pre_step",
    )(proj, shift_pad, *taps, *prm)


def _even_post_kernel(hs_ref, gate_ref, y_ref, r_ref, k_ref, v_ref, g_ref, lng_ref, lnb_ref, rk_ref,
                      red_ref, exp_ref, *rest):
    o_ref = rest[-1]
    y = y_ref[...]
    mu = _head_sum(y, red_ref, exp_ref) * (1.0 / RWKV_HD)
    d = y - mu
    var = _head_sum(d * d, red_ref, exp_ref) * (1.0 / RWKV_HD)
    yn = d * lax.rsqrt(var + 64e-5) * lng_ref[...] + lnb_ref[...]
    bonus = _head_sum(r_ref[...] * k_ref[...] * rk_ref[...], red_ref, exp_ref) * v_ref[...]
    o_ref[:, 0:LRU_W] = (hs_ref[...] * gate_ref[...]).astype(o_ref.dtype)
    o_ref[:, LRU_W:] = ((yn + bonus) * g_ref[...]).astype(o_ref.dtype)


def even_post(hs, gate, y, r, k, v, g, p, red, n_total, row0, prior=None):
    n = hs.shape[0]
    tr = min(POST_ROWS, n)
    row = lambda a: a.reshape(1, -1).astype(F32)
    consts = [row(p['ln_g']), row(p['ln_b']), row(p['r_k']), red, red.T]
    seq = pl.BlockSpec((tr, LRU_W), lambda i: (i, 0))
    args = [hs, gate, y, r, k, v, g] + consts
    in_specs = [seq] * 7 + [_const_spec(a, 1) for a in consts]
    aliases = {}
    if prior is not None:
        args.append(prior)
        in_specs.append(pl.BlockSpec(memory_space=pl.ANY))
        aliases = {len(args) - 1: 0}
    return pl.pallas_call(
        _even_post_kernel,
        grid=(n // tr,),
        in_specs=in_specs,
        out_specs=pl.BlockSpec((tr, D_MODEL), lambda i: (row0 // tr + i, 0)),
        out_shape=jax.ShapeDtypeStruct((n_total, D_MODEL), BF16),
        input_output_aliases=aliases,
        compiler_params=pltpu.CompilerParams(
            dimension_semantics=("parallel",), vmem_limit_bytes=VMEM_LIMIT_BYTES),
        name="even_post",
    )(*args)


def _retention_kernel(q_ref, k_ref, va_ref, vb_ref, dm_ref, rd_ref, kd_ref, sd_ref, o_ref, s_out_ref, s_scr):
    C = q_ref.shape[0]
    n_pair = RET_H // 2

    @pl.when(pl.program_id(1) == 0)
    def _():
        s_scr[...] = jnp.zeros(s_scr.shape, F32)

    lo = lax.broadcasted_iota(jnp.int32, (C, 2 * RET_DK), 1) < RET_DK

    def stack(x):
        return jnp.concatenate([jnp.where(lo, x, 0.0), jnp.where(lo, 0.0, x)], axis=0)

    for p in range(n_pair):
        qk = slice(p * 2 * RET_DK, (p + 1) * 2 * RET_DK)
        q2 = stack(q_ref[:, qk]).astype(BF16)
        k2 = stack(k_ref[:, qk])
        v0 = p * 2 * RET_DV
        v_ref = va_ref if p < n_pair // 2 else vb_ref
        vl = v0 % (RET_W // 2)
        v2 = jnp.concatenate([v_ref[:, vl:vl + RET_DV], v_ref[:, vl + RET_DV:vl + 2 * RET_DV]],
                             axis=0).astype(BF16)
        s = lax.dot_general(q2, k2.astype(BF16), (((1,), (1,)), ((), ())), preferred_element_type=F32) * dm_ref[p]
        s_old = s_scr[p]
        o2 = jnp.dot(s.astype(BF16), v2, preferred_element_type=F32) + jnp.dot(
            q2, s_old.astype(BF16), preferred_element_type=F32) * rd_ref[p]
        o_ref[:, v0:v0 + RET_DV] = o2[:C]
        o_ref[:, v0 + RET_DV:v0 + 2 * RET_DV] = o2[C:]
        s_scr[p] = s_old * sd_ref[p] + lax.dot_general((k2 * kd_ref[p]).astype(BF16), v2, (((0,), (0,)), ((), ())),
                                                       preferred_element_type=F32)

    @pl.when(pl.program_id(1) == pl.num_programs(1) - 1)
    def _():
        s_out_ref[...] = s_scr[...]


def retention_prompt_pallas(rq, rk, rv, B, T, v_col0=0):
    C = RET_CHUNK
    nc = T // C
    f32 = F32
    lg = jnp.log1p(-jnp.exp2(-5.0 - jnp.arange(RET_H, dtype=f32))).reshape(RET_H // 2, 2)
    i = jnp.arange(C, dtype=f32)
    diff = i[:, None] - i[None, :]
    causal = diff >= 0
    dmask = jnp.where(causal, jnp.exp(jnp.where(causal, diff, 0.0)[None, None] * lg[:, :, None, None]), 0.0)
    zero = jnp.zeros_like(dmask[:, 0])
    dm = jnp.concatenate([jnp.concatenate([dmask[:, 0], zero], axis=-1),
                          jnp.concatenate([zero, dmask[:, 1]], axis=-1)], axis=-2)
    rows = lambda x, w: jnp.broadcast_to(x[:, :, :, None], x.shape + (w,)).reshape(RET_H // 2, -1, w)
    rd = rows(jnp.exp((i[None, None, :] + 1.0) * lg[:, :, None]), RET_DV)
    kd = rows(jnp.exp((C - 1.0 - i)[None, None, :] * lg[:, :, None]), 2 * RET_DK)
    sd = rows(jnp.broadcast_to(jnp.exp(C * lg)[:, :, None], (RET_H // 2, 2, RET_DK)), RET_DV)
    half_w = RET_W // 2
    qk_spec = pl.BlockSpec((C, RET_H * RET_DK), lambda b, c: (b * nc + c, 0))
    v_spec = lambda k: pl.BlockSpec((C, half_w), lambda b, c: (b * nc + c, v_col0 // half_w + k))
    const = lambda a: pl.BlockSpec(a.shape, lambda b, c: (0, 0, 0))
    o, s = pl.pallas_call(
        _retention_kernel,
        grid=(B, nc),
        in_specs=[qk_spec, qk_spec, v_spec(0), v_spec(1), const(dm), const(rd), const(kd), const(sd)],
        out_specs=[pl.BlockSpec((C, RET_W), lambda b, c: (b * nc + c, 0)),
                   pl.BlockSpec((None, RET_H // 2, 2 * RET_DK, RET_DV), lambda b, c: (b, 0, 0, 0))],
        out_shape=[jax.ShapeDtypeStruct((B * T, RET_W), f32),
                   jax.ShapeDtypeStruct((B, RET_H // 2, 2 * RET_DK, RET_DV), f32)],
        scratch_shapes=[pltpu.VMEM((RET_H // 2, 2 * RET_DK, RET_DV), f32)],
        compiler_params=pltpu.CompilerParams(
            dimension_semantics=("parallel", "arbitrary"), vmem_limit_bytes=VMEM_LIMIT_BYTES),
        name="retention_prompt",
    )(rq, rk, rv, rv, dm, rd, kd, sd)
    return s.reshape(B, RET_H, RET_DK, RET_DV), o


KV_W = NSA_G * NSA_HD
RET_QK_W = RET_H * RET_DK
OFF_Q = 0
OFF_KC = OFF_Q + NSA_W
OFF_VC = OFF_KC + KV_W
OFF_KS = OFF_VC + KV_W
OFF_VS = OFF_KS + KV_W
OFF_KW = OFF_VS + KV_W
OFF_VW = OFF_KW + KV_W
OFF_RQ = OFF_VW + KV_W
OFF_RK = OFF_RQ + RET_QK_W
OFF_RV = OFF_RK + RET_QK_W
OFF_RG = OFF_RV + RET_W
OFF_GT = OFF_RG + RET_W
CD_PAD = _round_up(OFF_GT + LANE, COL_TILE)
ODD_ROWS = 256
N_ODD_PRE_OUT = 11


def _odd_weight_cols(w):
    gt0 = NSA_W + 6 * KV_W
    body = jnp.concatenate([w[:, :gt0], w[:, gt0 + 3 * NSA_H:]], axis=1)
    gt = w[:, gt0:gt0 + 3 * NSA_H]
    out = jnp.concatenate([body, gt], axis=1)
    return jnp.pad(out, ((0, 0), (0, CD_PAD - out.shape[1]))).astype(BF16)


def _rope_tables(pos, n_rot, theta, head):
    half = n_rot // 2
    inv = jnp.exp(-jnp.log(jnp.float32(theta)) * jnp.arange(half, dtype=jnp.float32) / half)
    ang = pos.astype(jnp.float32)[:, None] * inv[None, :]
    cos, sin = jnp.cos(ang), jnp.sin(ang)
    d = np.arange(LANE) % head
    cos_d, sin_d = cos[:, d % half], sin[:, d % half]
    c = jnp.where(d < n_rot, cos_d, 1.0)
    s1 = jnp.where(d < half, -sin_d, 0.0)
    s2 = jnp.where((d >= half) & (d < n_rot), sin_d, 0.0)
    return jnp.stack([c, s1, s2])


def _rope_lanes(x, tab_ref, half):
    w = x.shape[1]
    rep = w // LANE
    c, s1, s2 = (pltpu.repeat(tab_ref[i], rep, axis=1) for i in range(3))
    return x * c + pltpu.roll(x, w - half, 1) * s1 + pltpu.roll(x, half, 1) * s2


def _rms_heads(x, g_ref, red_ref, exp_ref):
    ms = _head_sum(x * x, red_ref, exp_ref) * (1.0 / NSA_HD)
    return x * lax.rsqrt(ms + 1e-6) * g_ref[...]


def _odd_pre_kernel(x_ref, nsa_tab, ret_tab, qg_ref, ksg_ref, kwg_ref, redq_ref, expq_ref, redk_ref, expk_ref,
                    tile_ref, qn_o, qr_o, ks_o, kw_o, ks4_o, vs4_o, kw4_o, vw4_o, gate_o, rq_o, rk_o, kvt_o, wint_o):
    nsa_half = ROPE_DIMS // 2
    qn = _rms_heads(x_ref[:, OFF_Q:OFF_Q + NSA_W], qg_ref, redq_ref, expq_ref)
    qn_o[...] = qn
    qr_o[...] = _rope_lanes(qn, nsa_tab, nsa_half)
    ks = _rope_lanes(_rms_heads(x_ref[:, OFF_KS:OFF_KS + KV_W], ksg_ref, redk_ref, expk_ref), nsa_tab, nsa_half)
    kw = _rope_lanes(_rms_heads(x_ref[:, OFF_KW:OFF_KW + KV_W], kwg_ref, redk_ref, expk_ref), nsa_tab, nsa_half)
    ks_o[...] = ks
    kw_o[...] = kw
    tile = tile_ref[...]
    for src, dst in ((ks, ks4_o), (x_ref[:, OFF_VS:OFF_VS + KV_W], vs4_o), (kw, kw4_o),
                     (x_ref[:, OFF_VW:OFF_VW + KV_W], vw4_o)):
        dst[...] = jnp.dot(src.astype(BF16), tile, preferred_element_type=F32).astype(BF16)
    gate_o[...] = jax.nn.sigmoid(x_ref[:, OFF_GT:OFF_GT + LANE])
    rq_o[...] = _rope_lanes(x_ref[:, OFF_RQ:OFF_RQ + RET_QK_W], ret_tab, RET_DK // 2)
    rk_o[...] = _rope_lanes(x_ref[:, OFF_RK:OFF_RK + RET_QK_W], ret_tab, RET_DK // 2) * (RET_DK ** -0.5)
    kv_pieces = (x_ref[:, OFF_KC:OFF_KC + KV_W], x_ref[:, OFF_VC:OFF_VC + KV_W], ks, x_ref[:, OFF_VS:OFF_VS + KV_W])
    for dst, pieces in ((kvt_o, kv_pieces), (wint_o, (kw, x_ref[:, OFF_VW:OFF_VW + KV_W]))):
        for s, piece in enumerate(pieces):
            for c in range(KV_W // LANE):
                dst[s * KV_W + c * LANE:s * KV_W + (c + 1) * LANE, :] = piece[:, c * LANE:(c + 1) * LANE].T


def odd_pre(proj, pos, p, row0, n_rows, same_pos, seq_len):
    tr = min(ODD_ROWS, n_rows)
    blk0 = row0 // tr
    n_tab = tr if same_pos else n_rows
    pos_rows = jnp.broadcast_to(pos, (n_tab,)) if same_pos else pos
    nsa_tab = _rope_tables(pos_rows, ROPE_DIMS, ROPE_THETA, NSA_HD)
    ret_tab = _rope_tables(pos_rows, RET_DK, RET_THETA, RET_DK)
    row = lambda v, rep: jnp.tile(v.astype(F32), rep).reshape(1, -1)
    lanes = np.arange(LANE)
    red_q = jnp.asarray((np.arange(NSA_W) // NSA_HD)[:, None] == lanes[None, :], dtype=BF16)
    red_k = jnp.asarray((np.arange(KV_W) // NSA_HD)[:, None] == lanes[None, :], dtype=BF16)
    src = np.arange(KV_W)
    dst = np.arange(NSA_W)
    tile = jnp.asarray((src[:, None] // NSA_HD == dst[None, :] // GROUP_W)
                       & (src[:, None] % NSA_HD == dst[None, :] % NSA_HD), dtype=BF16)
    consts = [row(p['q_norm'], NSA_H), row(p['k_norm'][1], NSA_G), row(p['k_norm'][2], NSA_G),
              red_q, red_q.T, red_k, red_k.T, tile]
    tab_spec = pl.BlockSpec((3, tr, LANE), (lambda i: (0, 0, 0)) if same_pos else (lambda i: (0, i, 0)))
    out = lambda w, dt: (pl.BlockSpec((tr, w), lambda i: (i, 0)), jax.ShapeDtypeStruct((n_rows, w), dt))
    seq_tiles = seq_len // tr
    out_t = lambda r: (pl.BlockSpec((None, r, tr), lambda i: (i // seq_tiles, 0, i % seq_tiles)),
                       jax.ShapeDtypeStruct((n_rows // seq_len, r, seq_len), F32))
    outs = [out(NSA_W, F32), out(NSA_W, F32), out(KV_W, F32), out(KV_W, F32)] + [out(NSA_W, BF16)] * 4 + [
        out(LANE, F32), out(RET_QK_W, F32), out(RET_QK_W, F32), out_t(KV_SLOTS * KV_W), out_t(2 * KV_W)]
    return pl.pallas_call(
        _odd_pre_kernel,
        grid=(n_rows // tr,),
        in_specs=[pl.BlockSpec((tr, CD_PAD), lambda i: (blk0 + i, 0)), tab_spec, tab_spec]
        + [_const_spec(a, 1) for a in consts],
        out_specs=[o[0] for o in outs],
        out_shape=[o[1] for o in outs],
        compiler_params=pltpu.CompilerParams(
            dimension_semantics=("parallel",), vmem_limit_bytes=VMEM_LIMIT_BYTES),
        name="odd_pre",
    )(proj, nsa_tab, ret_tab, *consts)


def _odd_post_kernel(oc_ref, os_ref, ow_ref, gate_ref, ret_ref, rg0_ref, rg1_ref, gng_ref, gnb_ref, ge_ref, *rest,
                     gated):
    o_ref = rest[-1]
    if gated:
        nsa = oc_ref[...]
    else:
        gates = gate_ref[...]
        nsa = jnp.zeros(oc_ref.shape, F32)
        for j, branch in enumerate((oc_ref, os_ref, ow_ref)):
            nsa = nsa + _dot_01(gates, ge_ref[j]) * branch[...]
    o_ref[:, 0:NSA_W] = nsa.astype(o_ref.dtype)
    for h in range(RET_H):
        lanes = slice(h * RET_DV, (h + 1) * RET_DV)
        x = ret_ref[:, lanes]
        mu = jnp.mean(x, axis=-1, keepdims=True)
        d = x - mu
        var = jnp.mean(d * d, axis=-1, keepdims=True)
        yn = d * lax.rsqrt(var + 1e-5) * gng_ref[:, lanes] + gnb_ref[:, lanes]
        rg = (rg0_ref if h < RET_H // 2 else rg1_ref)[:, (h % (RET_H // 2)) * RET_DV:(h % (RET_H // 2) + 1) * RET_DV]
        o_ref[:, NSA_W + h * RET_DV:NSA_W + (h + 1) * RET_DV] = (yn * (rg * jax.nn.sigmoid(rg))).astype(o_ref.dtype)


def odd_post(o_cmp, o_slc, o_win, gates, o_ret, proj, p, n_total, row0, prior=None, gated=False):
    n = o_cmp.shape[0]
    tr = min(POST_ROWS, n)
    blk0 = row0 // tr
    h = np.arange(NSA_W) // NSA_HD
    ge = jnp.asarray(np.stack([(np.arange(LANE)[:, None] == (3 * h + j)[None, :]) for j in range(3)]), dtype=BF16)
    row = lambda a: a.reshape(1, -1).astype(F32)
    consts = [row(p['gn_g']), row(p['gn_b']), ge]
    seq = lambda w: pl.BlockSpec((tr, w), lambda i: (i, 0))
    half = RET_W // 2
    rg_spec = lambda k: pl.BlockSpec((tr, half), lambda i: (blk0 + i, OFF_RG // half + k))
    args = [o_cmp, o_slc, o_win, gates, o_ret, proj, proj] + consts
    in_specs = [seq(NSA_W)] * 3 + [seq(LANE), seq(RET_W), rg_spec(0), rg_spec(1)] + [_const_spec(a, 1) for a in consts]
    aliases = {}
    if prior is not None:
        args.append(prior)
        in_specs.append(pl.BlockSpec(memory_space=pl.ANY))
        aliases = {len(args) - 1: 0}
    return pl.pallas_call(
        functools.partial(_odd_post_kernel, gated=gated),
        grid=(n // tr,),
        in_specs=in_specs,
        out_specs=pl.BlockSpec((tr, D_MODEL), lambda i: (blk0 + i, 0)),
        out_shape=jax.ShapeDtypeStruct((n_total, D_MODEL), BF16),
        input_output_aliases=aliases,
        compiler_params=pltpu.CompilerParams(
            dimension_semantics=("parallel",), vmem_limit_bytes=VMEM_LIMIT_BYTES),
        name="odd_post",
    )(*args)


def rms_norm(x, g, eps=1e-6):
    xf = x.astype(jnp.float32)
    y = xf * lax.rsqrt(jnp.mean(xf * xf, axis=-1, keepdims=True) + eps)
    return (y * g.astype(jnp.float32)).astype(x.dtype)


def head_group_norm(y, g, b, eps):
    yf = y.astype(jnp.float32)
    mu = jnp.mean(yf, axis=-1, keepdims=True)
    var = jnp.mean(jnp.square(yf - mu), axis=-1, keepdims=True)
    yn = ((yf - mu) * lax.rsqrt(var + eps)).reshape(y.shape[:-2] + (-1,))
    return (yn * g.astype(jnp.float32) + b.astype(jnp.float32)).astype(y.dtype)


def masked_softmax(s, mask):
    s = jnp.where(mask, s.astype(jnp.float32), -jnp.inf)
    m = jnp.max(s, axis=-1, keepdims=True)
    e = jnp.exp(s - jnp.where(jnp.isfinite(m), m, 0.0))
    den = jnp.sum(e, axis=-1, keepdims=True)
    return e / jnp.where(den > 0, den, 1.0)


def rope(x, pos, n_rot, theta):
    half = n_rot // 2
    inv = jnp.exp(-jnp.log(jnp.float32(theta)) * jnp.arange(half, dtype=jnp.float32) / half)
    ang = pos.astype(jnp.float32)[:, None] * inv[None, :]
    cos = jnp.cos(ang)[None, :, None, :]
    sin = jnp.sin(ang)[None, :, None, :]
    xf = x.astype(jnp.float32)
    x1, x2 = xf[..., :half], xf[..., half:n_rot]
    out = jnp.concatenate([x1 * cos - x2 * sin, x2 * cos + x1 * sin, xf[..., n_rot:]], axis=-1)
    return out.astype(x.dtype)


def linear_scan(a, b, h0):
    b = b.at[:, 0].add(a[:, 0] * h0)

    def combine(left, right):
        return left[0] * right[0], right[0] * left[1] + right[1]

    return lax.associative_scan(combine, (a, b), axis=1)[1]


def wkv7_scan(r, w, k, v, a, b, s0):
    xs = tuple(jnp.moveaxis(z.astype(jnp.float32), 1, 0) for z in (r, w, k, v, a, b))

    def step(S, inp):
        r_t, w_t, k_t, v_t, a_t, b_t = inp
        sa = jnp.einsum('bhij,bhj->bhi', S, a_t)
        S = S * w_t[:, :, None, :] + sa[..., None] * b_t[:, :, None, :] + v_t[..., None] * k_t[:, :, None, :]
        return S, jnp.einsum('bhij,bhj->bhi', S, r_t)

    S, ys = lax.scan(step, s0.astype(jnp.float32), xs)
    return jnp.moveaxis(ys, 0, 1), S


def even_mixer_core(proj, p, lru_h0, lru_conv0, shift0, wkv0):
    B, T, _ = proj.shape
    f32 = jnp.float32
    dt = proj.dtype
    xb, gb, rw = jnp.split(proj, [LRU_W, 2 * LRU_W], axis=-1)
    xcat = jnp.concatenate([lru_conv0.astype(dt), xb], axis=1)
    xc = p['conv_b'] + sum(p['conv_w'][j] * xcat[:, j:j + T] for j in range(CONV_W))
    xbd = xc.reshape(B, T, LRU_BLOCKS, LRU_BS)
    gate_r = jax.nn.sigmoid(jnp.einsum('btnc,ncd->btnd', xbd, p['wa']).reshape(B, T, LRU_W) + p['ba'])
    gate_i = jax.nn.sigmoid(jnp.einsum('btnc,ncd->btnd', xbd, p['wx']).reshape(B, T, LRU_W) + p['bx'])
    log_a = -LRU_C * gate_r.astype(f32) * jax.nn.softplus(-p['lam'].astype(f32))
    u = jnp.sqrt(-jnp.expm1(2.0 * log_a)) * (gate_i * xc).astype(f32)
    hs = lru_scan(jnp.exp(log_a), u, lru_h0.astype(f32))
    y_lru = hs.astype(dt) * jax.nn.gelu(gb)
    prev = jnp.concatenate([shift0.astype(dt)[:, None], rw[:, :-1]], axis=1)
    rs = rw + p['mu'] * (prev - rw)
    r, k, v, xw, xa, xg = jnp.split(
        rs, [RWKV_W, 2 * RWKV_W, 3 * RWKV_W, 3 * RWKV_W + W_LORA, 3 * RWKV_W + W_LORA + A_LORA], axis=-1)
    w_log = -jax.nn.softplus(-(p['w0'] + jnp.tanh(xw) @ p['w2']).astype(f32)) - 0.5
    log_decay = -jnp.exp(w_log)
    decay = jnp.exp(log_decay)
    a_icl = jax.nn.sigmoid(p['a0'] + xa @ p['a2'])
    g = jax.nn.sigmoid(xg) @ p['g2']
    heads = (B, T, RWKV_H, RWKV_HD)
    kk = (k * p['k_k']).reshape(heads).astype(f32)
    kk = kk / jnp.maximum(jnp.sqrt(jnp.sum(kk * kk, axis=-1, keepdims=True)), 1e-12)
    k = k * (1.0 + (a_icl - 1.0) * p['k_a'])
    rh, kh, vh, ah = (z.reshape(heads) for z in (r, k, v, a_icl))
    if T % WKV_C == 0:
        y, wkv = wkv7_chunked(r.astype(f32), log_decay, k.astype(f32), v.astype(f32),
                              (-kk).reshape(B, T, RWKV_W), (kk * ah.astype(f32)).reshape(B, T, RWKV_W), wkv0)
        y = y.reshape(heads)
    else:
        y, wkv = wkv7_scan(rh, decay.reshape(heads), kh, vh, -kk, kk * ah.astype(f32), wkv0)
    y = head_group_norm(y, p['ln_g'], p['ln_b'], 64e-5).astype(dt)
    bonus = (jnp.sum(rh * kh * p['r_k'], axis=-1, keepdims=True) * vh).reshape(B, T, RWKV_W)
    y_rwkv = (y + bonus) * g
    cat = jnp.concatenate([y_lru, y_rwkv], axis=-1)
    return cat, hs[:, -1], xcat[:, T:], rw[:, -1], wkv


def even_mixer(proj, p, B, T, DB, lru_h0, lru_conv0, shift0, wkv0):
    f32 = F32
    prm = _even_params(p)
    red = prm[-2]
    n_p = B * T
    zeros = lambda *s: jnp.zeros(s, f32)
    a, u, gate, r, lw, k, v, na, nb, g = even_pre_seq(proj, zeros(B, CONV_W - 1, LRU_W), zeros(B, SHIFT_W), prm, B, T)
    seq = lambda z: z.reshape(B, T, LRU_W)
    hs = lru_scan(seq(a), seq(u), zeros(B, LRU_W))
    yw, wkv_p = wkv7_chunked(seq(r), seq(lw), seq(k), seq(v), seq(na), seq(nb), zeros(B, RWKV_H, RWKV_HD, RWKV_HD))
    cat = even_post(hs.reshape(n_p, LRU_W), gate, yw.reshape(n_p, RWKV_W), r, k, v, g, p, red, n_p + DB, 0)
    tail = lambda b, n, c0, c1: proj[(b + 1) * T - n:(b + 1) * T, c0:c1]
    st_p = (hs[:, -1], jnp.stack([tail(b, CONV_W - 1, 0, LRU_W) for b in range(B)]),
            jnp.concatenate([tail(b, 1, 2 * LRU_W, AB_COLS) for b in range(B)], axis=0), wkv_p)
    a, u, gate, r, lw, k, v, na, nb, g = even_pre_step(proj, n_p, lru_conv0, shift0, prm)
    hs_s = a * lru_h0.astype(f32) + u
    heads = (DB, 1, RWKV_H, RWKV_HD)
    yw, wkv_s = wkv7_scan(r.reshape(heads), jnp.exp(lw).reshape(heads), k.reshape(heads), v.reshape(heads),
                          na.reshape(heads), nb.reshape(heads), wkv0)
    cat = even_post(hs_s, gate, yw.reshape(DB, RWKV_W), r, k, v, g, p, red, n_p + DB, n_p, prior=cat)
    xb_s = proj[n_p:]
    conv_s = jnp.concatenate([lru_conv0[:, 1:].astype(f32), xb_s[:, None, :LRU_W]], axis=1)
    st_s = (hs_s, conv_s, xb_s[:, 2 * LRU_W:AB_COLS], wkv_s)
    return cat, st_p, st_s


def odd_project(proj, p, pos):
    B, T, _ = proj.shape
    sizes = [NSA_W] + [NSA_G * NSA_HD] * 6 + [3 * NSA_H, RET_H * RET_DK, RET_H * RET_DK, RET_W, RET_W]
    q, kc, vc, ks, vs, kw, vw, gt, rq, rk, rv, rg = jnp.split(
        proj, np.cumsum(sizes).tolist(), axis=-1)[:len(sizes)]
    kvs = (B, T, NSA_G, NSA_HD)
    q_n = rms_norm(q.reshape(B, T, NSA_H, NSA_HD), p['q_norm'])
    return {
        'q_n': q_n,
        'q_r': rope(q_n, pos, ROPE_DIMS, ROPE_THETA),
        'kc': kc.reshape(kvs), 'vc': vc.reshape(kvs),
        'ks': rope(rms_norm(ks.reshape(kvs), p['k_norm'][1]), pos, ROPE_DIMS, ROPE_THETA),
        'vs': vs.reshape(kvs),
        'kw': rope(rms_norm(kw.reshape(kvs), p['k_norm'][2]), pos, ROPE_DIMS, ROPE_THETA),
        'vw': vw.reshape(kvs),
        'gates': jax.nn.sigmoid(gt).reshape(B, T, NSA_H, 3),
        'rq': rope(rq.reshape(B, T, RET_H, RET_DK), pos, RET_DK, RET_THETA),
        'rk': rope(rk.reshape(B, T, RET_H, RET_DK), pos, RET_DK, RET_THETA) * (RET_DK ** -0.5),
        'rv': rv.reshape(B, T, RET_H, RET_DV),
        'rg': rg,
    }


def to_groups_q(q):
    B, T = q.shape[:2]
    return jnp.moveaxis(q.reshape(B, T, NSA_G, NSA_HPG, NSA_HD), 1, 3)


def to_groups_k(k):
    return jnp.moveaxis(k, 1, 2)


def nsa_compress(x, w1, b1, w2, b2):
    B, L = x.shape[:2]
    n_chunk = L // CMP_STRIDE
    n_cmp = n_chunk - CMP_R + 1
    ch = x[:, :n_chunk * CMP_STRIDE].reshape(B, n_chunk, CMP_STRIDE, NSA_G, NSA_HD)
    ch = jnp.moveaxis(ch, 3, 2).reshape(B, n_chunk, NSA_G, CMP_STRIDE * NSA_HD)
    part = jnp.einsum('bngc,rch->bngrh', ch, w1)
    pre = b1 + sum(part[:, m:m + n_cmp, :, m] for m in range(CMP_R))
    return jax.nn.gelu(pre) @ w2 + b2


def nsa_compressed_branch(qn, kc_raw, vc_raw, p, q_pos):
    kc = to_groups_k(rms_norm(nsa_compress(kc_raw, *p['ck']), p['k_norm'][0]))
    vc = to_groups_k(nsa_compress(vc_raw, *p['cv']))
    s = jnp.einsum('bghqd,bgcd->bghqc', qn, kc) * NSA_HD ** -0.5
    ends = jnp.arange(kc.shape[2]) * CMP_STRIDE + CMP_BLOCK - 1
    prob = masked_softmax(s, ends[None, :] <= q_pos[:, None])
    return jnp.einsum('bghqc,bgcd->bghqd', prob.astype(vc.dtype), vc), prob


def cmp_sel_overlap(n_cmp, n_sel):
    cs = np.arange(n_cmp) * CMP_STRIDE
    ss = np.arange(n_sel) * SEL_BLOCK
    ov = np.minimum(cs[None] + CMP_BLOCK, ss[:, None] + SEL_BLOCK) - np.maximum(cs[None], ss[:, None])
    return jnp.asarray(np.clip(ov, 0, None) / CMP_BLOCK, dtype=jnp.float32)


def nsa_select(p_cmp, q_pos, n_sel):
    imp = jnp.einsum('bgqc,sc->bgqs', p_cmp.sum(axis=2), cmp_sel_overlap(p_cmp.shape[-1], n_sel))
    j = jnp.arange(n_sel)[None, :]
    qb = (q_pos // SEL_BLOCK)[:, None]
    valid = j <= qb
    forced = (j == 0) | (j == qb) | (j == qb - 1)
    score = jnp.where(valid, jnp.where(forced, FORCE_SCORE, imp), -jnp.inf)
    _, idx = lax.top_k(score, min(SEL_TOP, n_sel))
    sel_ok = jnp.take_along_axis(jnp.broadcast_to(valid, score.shape), idx, axis=-1)
    return idx, sel_ok


def sel_blocks(x, n_sel):
    B, L = x.shape[:2]
    x = jnp.pad(x, ((0, 0), (0, n_sel * SEL_BLOCK - L), (0, 0), (0, 0)))
    return jnp.moveaxis(x.reshape(B, n_sel, SEL_BLOCK, NSA_G, NSA_HD), 3, 1)


def nsa_slc_attend(q, kb, vb, idx, sel_ok, q_pos):
    B, G = kb.shape[:2]
    bi = jnp.arange(B)[:, None, None, None]
    gi = jnp.arange(G)[None, :, None, None]
    kg = kb[bi, gi, idx]
    vg = vb[bi, gi, idx]
    s = jnp.einsum('bghqd,bgqnld->bghqnl', q, kg) * NSA_HD ** -0.5
    kpos = idx[..., None] * SEL_BLOCK + jnp.arange(SEL_BLOCK)
    mask = (kpos <= q_pos[None, None, :, None, None]) & sel_ok[..., None]
    sh = s.shape
    prob = masked_softmax(s.reshape(sh[:4] + (-1,)), mask.reshape(B, G, 1, sh[3], -1))
    return jnp.einsum('bghqnl,bgqnld->bghqd', prob.reshape(sh).astype(vg.dtype), vg)


def window_attend_banded(q, k, v):
    B, G, HPG, T, HD = q.shape
    nb = T // WIN_BLOCK
    npv = WINDOW // WIN_BLOCK
    pad = ((0, 0), (0, 0), (npv * WIN_BLOCK, 0), (0, 0))

    def band(z):
        zb = jnp.pad(z, pad).reshape(B, G, nb + npv, WIN_BLOCK, HD)
        return jnp.concatenate([zb[:, :, j:j + nb] for j in range(npv + 1)], axis=3)

    kb, vb = band(k), band(v)
    qb = q.reshape(B, G, HPG, nb, WIN_BLOCK, HD)
    s = jnp.einsum('bghiqd,bgikd->bghiqk', qb, kb) * NSA_HD ** -0.5
    blk = jnp.arange(nb)[:, None]
    q_pos = blk * WIN_BLOCK + jnp.arange(WIN_BLOCK)[None]
    k_pos = (blk - npv) * WIN_BLOCK + jnp.arange((npv + 1) * WIN_BLOCK)[None]
    diff = q_pos[:, :, None] - k_pos[:, None, :]
    mask = (diff >= 0) & (diff < WINDOW) & (k_pos[:, None, :] >= 0)
    prob = masked_softmax(s, mask)
    return jnp.einsum('bghiqk,bgikd->bghiqd', prob.astype(v.dtype), vb).reshape(B, G, HPG, T, HD)


def window_attend_cached(q, k, v, q_pos, k_pos):
    s = jnp.einsum('bghqd,blgd->bghql', q, k) * NSA_HD ** -0.5
    diff = q_pos[:, None] - k_pos[None, :]
    prob = masked_softmax(s, (diff >= 0) & (diff < WINDOW))
    return jnp.einsum('bghql,blgd->bghqd', prob.astype(v.dtype), v)


def retention_chunk(S, q, k, v):
    f32 = jnp.float32
    C = q.shape[1]
    lg = jnp.log1p(-jnp.exp2(-5.0 - jnp.arange(RET_H, dtype=f32)))
    i = jnp.arange(C, dtype=f32)
    diff = i[:, None] - i[None, :]
    causal = diff >= 0
    dmask = jnp.where(causal, jnp.exp(jnp.where(causal, diff, 0.0)[None] * lg[:, None, None]), 0.0)
    qf, kf, vf = q.astype(f32), k.astype(f32), v.astype(f32)
    s = jnp.einsum('bihd,bjhd->bhij', qf, kf) * dmask
    o = jnp.einsum('bhij,bjhe->bihe', s, vf)
    o = o + jnp.einsum('bihd,bhde->bihe', qf, S) * jnp.exp((i[:, None] + 1.0) * lg[None, :])[None, :, :, None]
    k_dec = kf * jnp.exp((C - 1.0 - i)[:, None] * lg[None, :])[None, :, :, None]
    S = S * jnp.exp(C * lg)[None, :, None, None] + jnp.einsum('bjhd,bjhe->bhde', k_dec, vf)
    return S, o


def retention_prompt(q, k, v):
    B, T = q.shape[:2]
    n = T // RET_CHUNK
    xs = tuple(jnp.moveaxis(z.reshape((B, n, RET_CHUNK) + z.shape[2:]), 1, 0) for z in (q, k, v))
    s0 = jnp.zeros((B, RET_H, RET_DK, RET_DV), jnp.float32)
    S, o = lax.scan(lambda S, c: retention_chunk(S, c[0], c[1], c[2]), s0, xs)
    return S, jnp.moveaxis(o, 0, 1).reshape(B, T, RET_H, RET_DV)


def odd_output(o_cmp, o_slc, o_win, o_ret, pr, p):
    gates = pr['gates']
    B, T = gates.shape[:2]
    gg = jnp.moveaxis(gates.reshape(B, T, NSA_G, NSA_HPG, 3), 1, 3)[..., None]
    o = gg[..., 0, :] * o_cmp + gg[..., 1, :] * o_slc + gg[..., 2, :] * o_win
    o_nsa = jnp.moveaxis(o, 3, 1).reshape(B, T, NSA_W)
    y_ret = head_group_norm(o_ret, p['gn_g'], p['gn_b'], 1e-5).astype(o_nsa.dtype) * jax.nn.silu(pr['rg'])
    return jnp.concatenate([o_nsa, y_ret], axis=-1)


def odd_mixer_prompt(proj, p):
    B, T, _ = proj.shape
    pos = jnp.arange(T)
    pr = odd_project(proj, p, pos)
    qn = pr['q_n'].reshape(B, T, NSA_W)
    qr = pr['q_r'].reshape(B, T, NSA_W)
    kc = rms_norm(nsa_compress(pr['kc'], *p['ck']), p['k_norm'][0])
    vc = nsa_compress(pr['vc'], *p['cv'])
    n_cmp = kc.shape[1]
    n_sel = -(-T // SEL_BLOCK)
    o_cmp, sel = nsa_cmp_select(qn, _tile_cmp(kc), _tile_cmp(vc), _overlap_T(n_cmp, n_sel),
                                n_cmp=n_cmp, n_sel=n_sel, q_pos0=0)
    o_slc = nsa_flash(qr, _tile_groups(pr['ks']), _tile_groups(pr['vs']), sel, _sel_expand(T))
    o_win = nsa_flash(qr, _tile_groups(pr['kw']), _tile_groups(pr['vw']))
    S, o_ret = retention_prompt_pallas(pr['rq'].reshape(B, T, -1), pr['rk'].reshape(B, T, -1),
                                       pr['rv'].reshape(B, T, -1))
    o_ret = o_ret.reshape(B, T, RET_H, RET_DV)
    gates = pr['gates']
    heads = (B, T, NSA_H, NSA_HD)
    o_nsa = (gates[..., 0:1] * o_cmp.reshape(heads) + gates[..., 1:2] * o_slc.reshape(heads)
             + gates[..., 2:3] * o_win.reshape(heads)).reshape(B, T, NSA_W)
    y_ret = head_group_norm(o_ret, p['gn_g'], p['gn_b'], 1e-5).astype(o_nsa.dtype) * jax.nn.silu(pr['rg'])
    out = jnp.concatenate([o_nsa, y_ret], axis=-1)
    kv_rows = jnp.stack([pr['kc'], pr['vc'], pr['ks'], pr['vs']], axis=2)
    win = jnp.stack([pr['kw'], pr['vw']], axis=2)[:, T - min(WINDOW, T):]
    return out, kv_rows, win, S


def odd_mixer_sample(proj, p, cache_layer, page_table, win_buf, ret_s0):
    B, T, _ = proj.shape
    assert T == DEC_SEQ == 1 and win_buf.shape[1] == WIN_BUF
    pos = PAST_LEN + jnp.arange(T)
    pr = odd_project(proj, p, pos)
    scale = NSA_HD ** -0.5
    new_rows = jnp.stack([pr['ks'], pr['vs'], pr['kw'], pr['vw']], axis=2)[:, 0].reshape(B, 4, SLOT_ROWS)
    cache_t = jnp.transpose(cache_layer, (0, 2, 3, 4, 1)).reshape(cache_layer.shape[0], KV_ROWS, PAGE_SIZE)
    win_t = jnp.transpose(win_buf, (0, 2, 3, 4, 1)).reshape(B, 2 * SLOT_ROWS, WIN_BUF)
    wk = _dec_cmp_weights(*p['ck'])
    wv = _dec_cmp_weights(*p['cv'])
    w1t, b1, w2t, b2t = (jnp.stack([a, b]) for a, b in zip(wk, wv))
    kn = jnp.tile(p['k_norm'][0], NSA_G).reshape(1, SLOT_ROWS)
    t = np.arange(PAST_LEN)
    expand = jnp.asarray(np.arange(CMP_PAD)[:, None] == (t // SEL_BLOCK)[None, :], dtype=BF16)
    h = np.arange(NSA_H)
    grp = jnp.asarray((h[:, None] // NSA_HPG) == (h[None, :] // NSA_HPG), dtype=BF16)
    o16 = dec_nsa(page_table, cache_t, win_t, _place_heads(pr['q_n'][:, 0] * scale),
                  _place_heads(pr['q_r'][:, 0] * scale), new_rows, pr['gates'][:, 0],
                  w1t, b1, w2t, b2t, kn, _overlap_T(DEC_N_CMP, DEC_N_SEL), expand, grp)
    o_nsa = _take_heads(o16)[:, None, :]
    S, o_ret = retention_chunk(ret_s0.astype(jnp.float32), pr['rq'], pr['rk'], pr['rv'])
    y_ret = head_group_norm(o_ret, p['gn_g'], p['gn_b'], 1e-5).astype(o_nsa.dtype) * jax.nn.silu(pr['rg'])
    out = jnp.concatenate([o_nsa, y_ret], axis=-1)
    rows = jnp.stack([pr['kc'], pr['vc'], pr['ks'], pr['vs']], axis=2).astype(cache_layer.dtype)
    new_col = jnp.stack([pr['kw'], pr['vw']], axis=2)[:, 0].reshape(B, 2 * SLOT_ROWS, 1).astype(win_buf.dtype)
    win_new = jnp.concatenate([win_t[:, :, T:], new_col], axis=2).reshape(B, 2, NSA_G, NSA_HD, WIN_BUF)
    return out, rows, jnp.transpose(win_new, (0, 4, 1, 2, 3)), S


def odd_mixer(proj, p, B, T, DB, cache_layer, page_table, win_buf, ret_s0):
    assert DEC_SEQ == 1 and win_buf.shape[1] == WIN_BUF
    n_p = B * T
    kv = (B, T, NSA_G, NSA_HD)
    cols = lambda rows, off, w: proj[rows, off:off + w]
    prompt = slice(0, n_p)
    dec = slice(n_p, n_p + DB)
    qn, qr, ks, kw, ks4, vs4, kw4, vw4, gates, rq, rk, kv_t, win_t_p = odd_pre(
        proj, jnp.tile(jnp.arange(T), B), p, 0, n_p, False, T)
    seq = lambda z: z.reshape(B, T, -1)
    kc_raw = cols(prompt, OFF_KC, KV_W).reshape(kv)
    vc_raw = cols(prompt, OFF_VC, KV_W).reshape(kv)
    kc = rms_norm(nsa_compress(kc_raw, *p['ck']), p['k_norm'][0])
    vc = nsa_compress(vc_raw, *p['cv'])
    n_cmp = kc.shape[1]
    n_sel = -(-T // SEL_BLOCK)
    o_cmp, sel = nsa_cmp_select(seq(qn), _tile_cmp(kc), _tile_cmp(vc), _overlap_T(n_cmp, n_sel),
                                n_cmp=n_cmp, n_sel=n_sel, q_pos0=0)
    o_slc = nsa_flash(seq(qr), seq(ks4), seq(vs4), sel, _sel_expand(T))
    o_win = nsa_flash(seq(qr), seq(kw4), seq(vw4))
    ret_p, o_ret = retention_prompt_pallas(rq, rk, proj, B, T, v_col0=OFF_RV)
    flat = lambda z: z.reshape(n_p, -1)
    cat = odd_post(flat(o_cmp), flat(o_slc), flat(o_win), gates, o_ret, proj, p, n_p + DB, 0)
    n_win = min(WINDOW, T)
    kv_rows_p = jnp.transpose(kv_t.reshape(B, KV_SLOTS, NSA_G, NSA_HD, T), (0, 4, 1, 2, 3))
    win_p = jnp.transpose(win_t_p[:, :, T - n_win:].reshape(B, 2, NSA_G, NSA_HD, n_win), (0, 4, 1, 2, 3))
    qn, qr, ks, kw, _, _, _, _, gates, rq, rk, kv_t, _ = odd_pre(proj, jnp.asarray(PAST_LEN), p, n_p, DB, True, DB)
    scale = NSA_HD ** -0.5
    heads = lambda z: z.reshape(DB, NSA_H, NSA_HD)
    vs, vw = cols(dec, OFF_VS, KV_W), cols(dec, OFF_VW, KV_W)
    new_rows = jnp.stack([ks, vs, kw, vw], axis=1)
    cache_t = jnp.transpose(cache_layer, (0, 2, 3, 4, 1)).reshape(cache_layer.shape[0], KV_ROWS, PAGE_SIZE)
    win_t = jnp.transpose(win_buf, (0, 2, 3, 4, 1)).reshape(DB, 2 * SLOT_ROWS, WIN_BUF)
    w1t, b1, w2t, b2t = (jnp.stack([a, b]) for a, b in zip(_dec_cmp_weights(*p['ck']), _dec_cmp_weights(*p['cv'])))
    kn = jnp.tile(p['k_norm'][0], NSA_G).reshape(1, SLOT_ROWS)
    t = np.arange(PAST_LEN)
    expand = jnp.asarray(np.arange(CMP_PAD)[:, None] == (t // SEL_BLOCK)[None, :], dtype=BF16)
    h = np.arange(NSA_H)
    grp = jnp.asarray((h[:, None] // NSA_HPG) == (h[None, :] // NSA_HPG), dtype=BF16)
    o16 = dec_nsa(page_table, cache_t, win_t, _place_heads(heads(qn) * scale), _place_heads(heads(qr) * scale),
                  new_rows, gates[:, :3 * NSA_H].reshape(DB, NSA_H, 3),
                  w1t, b1, w2t, b2t, kn, _overlap_T(DEC_N_CMP, DEC_N_SEL), expand, grp)
    o_nsa = _take_heads(o16)
    ret_s, o_ret = retention_chunk(ret_s0.astype(F32), rq.reshape(DB, 1, RET_H, RET_DK),
                                   rk.reshape(DB, 1, RET_H, RET_DK), cols(dec, OFF_RV, RET_W).reshape(DB, 1, RET_H, RET_DV))
    cat = odd_post(o_nsa, o_nsa, o_nsa, gates, o_ret.reshape(DB, RET_W), proj, p, n_p + DB, n_p, prior=cat, gated=True)
    rows_s = jnp.transpose(kv_t.reshape(DEC_SEQ, KV_SLOTS, NSA_G, NSA_HD, DB), (4, 0, 1, 2, 3)).astype(cache_layer.dtype)
    new_row = jnp.concatenate([kw, vw], axis=1)[:, None, :].astype(win_buf.dtype)
    win_new = win_shift(win_t, new_row).reshape(DB, 2, NSA_G, NSA_HD, WIN_BUF)
    win_s = jnp.transpose(win_new, (0, 4, 1, 2, 3))
    return cat, (kv_rows_p, win_p, ret_p), (rows_s, win_s, ret_s)


def _stack(xs, dt):
    return jnp.stack(xs).astype(dt)


def kernel(x_prompt, x_sample, state_lru_h, state_lru_conv, state_rwkv_shift, state_rwkv_wkv,
           cache_nsa_kv, cache_nsa_win, state_ret, page_table,
           norm_ffn1, ffn1_w_in, ffn1_w_out, norm_mix, norm_ffn2, ffn2_w_in, ffn2_w_out,
           ab_w_in, lru_conv_w, lru_conv_b, lru_wa, lru_ba, lru_wx, lru_bx, lru_lambda,
           rwkv_mu, rwkv_w0, rwkv_w2, rwkv_a0, rwkv_a2, rwkv_g2, rwkv_k_k, rwkv_k_a, rwkv_r_k,
           rwkv_ln_g, rwkv_ln_b, ab_w_out,
           cd_w_in, nsa_q_norm, nsa_k_norm, cmp_k_w1, cmp_k_b1, cmp_k_w2, cmp_k_b2,
           cmp_v_w1, cmp_v_b1, cmp_v_w2, cmp_v_b2, ret_gn_g, ret_gn_b, cd_w_out):
    dt = x_prompt.dtype
    B = x_prompt.shape[0]
    DB = x_sample.shape[0]
    y = None
    lru_h_p, lru_h_s, lru_c_p, lru_c_s, sh_p, sh_s, wkv_p, wkv_s = [], [], [], [], [], [], [], []
    kv_p, kv_s, win_p, win_s, ret_p, ret_s = [], [], [], [], [], []
    for layer in range(DEPTH):
        li = layer // 2
        w1 = _prep_ffn_weights(ffn1_w_in, ffn1_w_out, layer)
        if layer == 0:
            y = ffn_block(x_prompt.reshape(N_PROMPT, D_MODEL), norm_ffn1[layer], *w1,
                          x_tail=x_sample.reshape(DB * DEC_SEQ, D_MODEL))
        else:
            y = ffn_block(y, norm_ffn1[layer], *w1)
        if layer % 2 == 0:
            p = {'conv_w': lru_conv_w[li], 'conv_b': lru_conv_b[li],
                 'wa': lru_wa[li], 'ba': lru_ba[li], 'wx': lru_wx[li], 'bx': lru_bx[li], 'lam': lru_lambda[li],
                 'mu': rwkv_mu[li], 'w0': rwkv_w0[li], 'w2': rwkv_w2[li], 'a0': rwkv_a0[li], 'a2': rwkv_a2[li],
                 'g2': rwkv_g2[li], 'k_k': rwkv_k_k[li], 'k_a': rwkv_k_a[li], 'r_k': rwkv_r_k[li],
                 'ln_g': rwkv_ln_g[li], 'ln_b': rwkv_ln_b[li]}
            proj = norm_matmul(y, norm_mix[layer], _prep_cols(ab_w_in[li], WIDE_COL_TILE), tn=WIDE_COL_TILE)
            cat, (a0, a1, a2, a3), (b0, b1, b2, b3) = even_mixer(
                proj, p, B, SEQ, DB, state_lru_h[li], state_lru_conv[li], state_rwkv_shift[li], state_rwkv_wkv[li])
            lru_h_p.append(a0); lru_c_p.append(a1); sh_p.append(a2); wkv_p.append(a3)
            lru_h_s.append(b0); lru_c_s.append(b1); sh_s.append(b2); wkv_s.append(b3)
            w_out = ab_w_out[li]
        else:
            p = {'q_norm': nsa_q_norm[li], 'k_norm': nsa_k_norm[li],
                 'ck': (cmp_k_w1[li], cmp_k_b1[li], cmp_k_w2[li], cmp_k_b2[li]),
                 'cv': (cmp_v_w1[li], cmp_v_b1[li], cmp_v_w2[li], cmp_v_b2[li]),
                 'gn_g': ret_gn_g[li], 'gn_b': ret_gn_b[li]}
            proj = norm_matmul(y, norm_mix[layer], _odd_weight_cols(cd_w_in[li]), tn=WIDE_COL_TILE)
            cat, (a0, a1, a2), (b0, b1, b2) = odd_mixer(
                proj, p, B, SEQ, DB, cache_nsa_kv[li], page_table, cache_nsa_win[li], state_ret[li])
            kv_p.append(a0); win_p.append(a1); ret_p.append(a2)
            kv_s.append(b0); win_s.append(b1); ret_s.append(b2)
            w_out = cd_w_out[li]
        y = matmul_residual(cat, w_out.astype(BF16), y, tn=WIDE_COL_TILE)
        w2 = _prep_ffn_weights(ffn2_w_in, ffn2_w_out, layer)
        if layer == DEPTH - 1:
            yp, ys = ffn_block(y, norm_ffn2[layer], *w2, n_tail_out=DB * DEC_SEQ)
        else:
            y = ffn_block(y, norm_ffn2[layer], *w2)
    yp = yp.reshape(B, SEQ, D_MODEL)
    ys = ys.reshape(DB, DEC_SEQ, D_MODEL)
    return (yp, ys,
            _stack(lru_h_p, dt), _stack(lru_h_s, dt), _stack(lru_c_p, dt), _stack(lru_c_s, dt),
            _stack(sh_p, dt), _stack(sh_s, dt), _stack(wkv_p, dt), _stack(wkv_s, dt),
            _stack(kv_p, dt), _stack(kv_s, dt), _stack(win_p, dt), _stack(win_s, dt),
            _stack(ret_p, dt), _stack(ret_s, dt))
```

```python
import functools

import jax
import jax.numpy as jnp
import numpy as np
from jax import lax
from jax.experimental import pallas as pl
from jax.experimental.pallas import tpu as pltpu

D_MODEL = 2048
BATCH = 4
SEQ = 2048
DEPTH = 2
DEC_BATCH = 128
DEC_SEQ = 1
PAST_LEN = 2048
PAGE_SIZE = 128
D_FF = 5504
LRU_W = D_MODEL // 2
LRU_BLOCKS = 16
LRU_BS = LRU_W // LRU_BLOCKS
CONV_W = 4
LRU_C = 8.0
RWKV_W = D_MODEL // 2
RWKV_HD = 64
RWKV_H = RWKV_W // RWKV_HD
W_LORA = 64
A_LORA = 64
G_LORA = 160
SHIFT_W = 3 * RWKV_W + W_LORA + A_LORA + G_LORA
AB_COLS = 2 * LRU_W + SHIFT_W
NSA_H = 16
NSA_G = 4
NSA_HPG = NSA_H // NSA_G
NSA_HD = 64
NSA_W = NSA_H * NSA_HD
ROPE_DIMS = NSA_HD // 4
ROPE_THETA = 500000.0
CMP_BLOCK = 32
CMP_STRIDE = 16
CMP_R = CMP_BLOCK // CMP_STRIDE
CMP_HID = 256
SEL_BLOCK = 64
SEL_TOP = 16
SEL_Q_BLOCK = 64
WINDOW = 512
WIN_BLOCK = 128
FORCE_SCORE = 1e4
KV_SLOTS = 4
RET_H = 8
RET_DK = 64
RET_DV = 128
RET_W = RET_H * RET_DV
RET_CHUNK = 128
RET_THETA = 10000.0
CD_COLS = NSA_W + 6 * NSA_G * NSA_HD + 3 * NSA_H + 2 * RET_H * RET_DK + 2 * RET_W

N_TOK = BATCH * SEQ + DEC_BATCH * DEC_SEQ
N_PROMPT = BATCH * SEQ

LANE = 128
VMEM_LIMIT_BYTES = 56 * 1024 * 1024
ROW_TILE = 640
FF_TILE = 512
D_FF_PAD = 5632
COL_TILE = 512
WIDE_COL_TILE = 2048

BF16 = jnp.bfloat16
F32 = jnp.float32


def _round_up(n, m):
    return -(-n // m) * m


def _rms_rows(x, g):
    ms = jnp.mean(x * x, axis=-1, keepdims=True)
    return x * lax.rsqrt(ms + 1e-6) * g


def _ffn_kernel(*refs, tail_in, tail_out):
    refs = list(refs)
    x_ref = refs.pop(0)
    xt_ref = refs.pop(0) if tail_in else None
    g_ref, wg_ref, wu_ref, wo_ref, o_ref = refs[:5]
    ot_ref = refs[5] if tail_out else None
    xn_ref, acc_ref = refs[-2:]
    i = pl.program_id(0)
    k = pl.program_id(1)
    last_tile = i == pl.num_programs(0) - 1

    def start(x):
        xn_ref[...] = _rms_rows(x, g_ref[...]).astype(BF16)
        acc_ref[...] = 2.0 * x

    if tail_in:
        n_tail = xt_ref.shape[0]

        @pl.when((k == 0) & last_tile)
        def _():
            start(jnp.concatenate([x_ref[0:x_ref.shape[0] - n_tail, :], xt_ref[...]], axis=0))

        @pl.when((k == 0) & jnp.logical_not(last_tile))
        def _():
            start(x_ref[...])
    else:
        @pl.when(k == 0)
        def _():
            start(x_ref[...])

    xn = xn_ref[...]
    gate = jnp.dot(xn, wg_ref[...], preferred_element_type=F32)
    up = jnp.dot(xn, wu_ref[...], preferred_element_type=F32)
    act = gate * jax.nn.sigmoid(gate) * up
    acc_ref[...] += jnp.dot(act.astype(BF16), wo_ref[...], preferred_element_type=F32)

    @pl.when(k == pl.num_programs(1) - 1)
    def _():
        o_ref[...] = 0.5 * acc_ref[...]

    if tail_out:
        n_tail = ot_ref.shape[0]

        @pl.when((k == pl.num_programs(1) - 1) & last_tile)
        def _():
            ot_ref[...] = 0.5 * acc_ref[acc_ref.shape[0] - n_tail:, :]


def ffn_block(x, g, wg, wu, wo, x_tail=None, n_tail_out=0):
    d = x.shape[1]
    m = x.shape[0] + (0 if x_tail is None else x_tail.shape[0])
    assert m % ROW_TILE == 0
    row_spec = pl.BlockSpec((ROW_TILE, d), lambda i, k: (i, 0))
    in_specs = [row_spec]
    args = [x]
    if x_tail is not None:
        in_specs.append(pl.BlockSpec(x_tail.shape, lambda i, k: (0, 0)))
        args.append(x_tail)
    in_specs += [
        pl.BlockSpec((1, d), lambda i, k: (0, 0)),
        pl.BlockSpec((d, FF_TILE), lambda i, k: (0, k)),
        pl.BlockSpec((d, FF_TILE), lambda i, k: (0, k)),
        pl.BlockSpec((FF_TILE, d), lambda i, k: (k, 0)),
    ]
    args += [g.reshape(1, d), wg, wu, wo]
    if n_tail_out:
        out_specs = [row_spec, pl.BlockSpec((n_tail_out, d), lambda i, k: (0, 0))]
        out_shape = [jax.ShapeDtypeStruct((m - n_tail_out, d), F32), jax.ShapeDtypeStruct((n_tail_out, d), F32)]
    else:
        out_specs = row_spec
        out_shape = jax.ShapeDtypeStruct((m, d), F32)
    return pl.pallas_call(
        functools.partial(_ffn_kernel, tail_in=x_tail is not None, tail_out=bool(n_tail_out)),
        grid=(m // ROW_TILE, D_FF_PAD // FF_TILE),
        in_specs=in_specs,
        out_specs=out_specs,
        out_shape=out_shape,
        scratch_shapes=[pltpu.VMEM((ROW_TILE, d), BF16), pltpu.VMEM((ROW_TILE, d), F32)],
        compiler_params=pltpu.CompilerParams(
            dimension_semantics=("arbitrary", "arbitrary"), vmem_limit_bytes=VMEM_LIMIT_BYTES),
        name="ffn_block",
    )(*args)


def _norm_matmul_kernel(x_ref, g_ref, w_ref, o_ref, xn_ref):
    @pl.when(pl.program_id(1) == 0)
    def _():
        xn_ref[...] = _rms_rows(x_ref[...], g_ref[...]).astype(BF16)

    o_ref[...] = jnp.dot(xn_ref[...], w_ref[...], preferred_element_type=F32)


def norm_matmul(x, g, w, tn=COL_TILE):
    m, k = x.shape
    n = w.shape[1]
    return pl.pallas_call(
        _norm_matmul_kernel,
        grid=(m // ROW_TILE, n // tn),
        in_specs=[
            pl.BlockSpec((ROW_TILE, k), lambda i, j: (i, 0)),
            pl.BlockSpec((1, k), lambda i, j: (0, 0)),
            pl.BlockSpec((k, tn), lambda i, j: (0, j)),
        ],
        out_specs=pl.BlockSpec((ROW_TILE, tn), lambda i, j: (i, j)),
        out_shape=jax.ShapeDtypeStruct((m, n), F32),
        scratch_shapes=[pltpu.VMEM((ROW_TILE, k), BF16)],
        compiler_params=pltpu.CompilerParams(
            dimension_semantics=("parallel", "arbitrary"), vmem_limit_bytes=VMEM_LIMIT_BYTES),
        name="norm_matmul",
    )(x, g.reshape(1, k), w)


def _matmul_residual_kernel(a_ref, w_ref, r_ref, o_ref):
    o_ref[...] = r_ref[...] + jnp.dot(a_ref[...].astype(BF16), w_ref[...], preferred_element_type=F32)


def matmul_residual(a, w, res, tn=COL_TILE):
    m, k = a.shape
    n = w.shape[1]
    return pl.pallas_call(
        _matmul_residual_kernel,
        grid=(m // ROW_TILE, n // tn),
        in_specs=[
            pl.BlockSpec((ROW_TILE, k), lambda i, j: (i, 0)),
            pl.BlockSpec((k, tn), lambda i, j: (0, j)),
            pl.BlockSpec((ROW_TILE, tn), lambda i, j: (i, j)),
        ],
        out_specs=pl.BlockSpec((ROW_TILE, tn), lambda i, j: (i, j)),
        out_shape=jax.ShapeDtypeStruct((m, n), F32),
        compiler_params=pltpu.CompilerParams(
            dimension_semantics=("parallel", "arbitrary"), vmem_limit_bytes=VMEM_LIMIT_BYTES),
        name="matmul_residual",
    )(a, w, res)


WCAST_ROWS = 256
WCAST_COLS = 512


def _cast_w_in_kernel(w_ref, wg_ref, wu_ref):
    pad = jnp.zeros((w_ref.shape[0], D_FF_PAD - D_FF), BF16)
    wg_ref[:, :D_FF] = w_ref[:, :D_FF].astype(BF16)
    wg_ref[:, D_FF:] = pad
    wu_ref[:, :D_FF] = w_ref[:, D_FF:].astype(BF16)
    wu_ref[:, D_FF:] = pad


def _cast_w_out_kernel(w_ref, wo_ref):
    wo_ref[:D_FF, :] = w_ref[...].astype(BF16)
    wo_ref[D_FF:, :] = jnp.zeros((D_FF_PAD - D_FF, w_ref.shape[1]), BF16)


def _prep_ffn_weights(w_in, w_out, layer):
    d = w_in.shape[1]
    wg, wu = pl.pallas_call(
        _cast_w_in_kernel,
        grid=(d // WCAST_ROWS,),
        in_specs=[pl.BlockSpec((None, WCAST_ROWS, 2 * D_FF), lambda i: (layer, i, 0))],
        out_specs=[pl.BlockSpec((WCAST_ROWS, D_FF_PAD), lambda i: (i, 0))] * 2,
        out_shape=[jax.ShapeDtypeStruct((d, D_FF_PAD), BF16)] * 2,
        compiler_params=pltpu.CompilerParams(dimension_semantics=("parallel",), vmem_limit_bytes=VMEM_LIMIT_BYTES),
        name="cast_w_in",
    )(w_in)
    wo = pl.pallas_call(
        _cast_w_out_kernel,
        grid=(d // WCAST_COLS,),
        in_specs=[pl.BlockSpec((None, D_FF, WCAST_COLS), lambda j: (layer, 0, j))],
        out_specs=pl.BlockSpec((D_FF_PAD, WCAST_COLS), lambda j: (0, j)),
        out_shape=jax.ShapeDtypeStruct((D_FF_PAD, d), BF16),
        compiler_params=pltpu.CompilerParams(dimension_semantics=("parallel",), vmem_limit_bytes=VMEM_LIMIT_BYTES),
        name="cast_w_out",
    )(w_out)
    return wg, wu, wo


def _prep_cols(w, tile):
    n = w.shape[1]
    return jnp.pad(w, ((0, 0), (0, _round_up(n, tile) - n))).astype(BF16)


SCAN_TILE = 256


def _lru_scan_kernel(a_ref, b_ref, h0_ref, o_ref, carry_ref):
    @pl.when(pl.program_id(1) == 0)
    def _():
        carry_ref[...] = h0_ref[...]

    a = a_ref[...]
    b = b_ref[...]
    rows = lax.broadcasted_iota(jnp.int32, a.shape, 0)
    k = 1
    while k < a.shape[0]:
        keep = rows >= k
        b = jnp.where(keep, a * pltpu.roll(b, k, 0) + b, b)
        a = jnp.where(keep, a * pltpu.roll(a, k, 0), a)
        k *= 2
    h = a * carry_ref[...] + b
    o_ref[...] = h
    carry_ref[...] = h[a.shape[0] - 1:, :]


def lru_scan(a, b, h0):
    B, T, W = a.shape
    tt = min(SCAN_TILE, T)
    return pl.pallas_call(
        _lru_scan_kernel,
        grid=(B, T // tt),
        in_specs=[
            pl.BlockSpec((None, tt, W), lambda i, t: (i, t, 0)),
            pl.BlockSpec((None, tt, W), lambda i, t: (i, t, 0)),
            pl.BlockSpec((None, 1, W), lambda i, t: (i, 0, 0)),
        ],
        out_specs=pl.BlockSpec((None, tt, W), lambda i, t: (i, t, 0)),
        out_shape=jax.ShapeDtypeStruct((B, T, W), F32),
        scratch_shapes=[pltpu.VMEM((1, W), F32)],
        compiler_params=pltpu.CompilerParams(
            dimension_semantics=("parallel", "arbitrary"), vmem_limit_bytes=VMEM_LIMIT_BYTES),
        name="lru_scan",
    )(a, b, h0.reshape(B, 1, W))


GROUP_W = NSA_HPG * NSA_HD
ATT_Q_TILE = 256
ATT_Q_TILE_WIN = 128
ATT_K_TILE = 256
ATT_K_TILE_WIN = 256
CMP_PAD = 128
NEG_BIG = -1e30


def _stack_heads(q):
    head = lax.broadcasted_iota(jnp.int32, q.shape, 1) // NSA_HD
    return jnp.concatenate([jnp.where(head == h, q, 0.0) for h in range(NSA_HPG)], axis=0)


def _unstack_heads(o, tq):
    head = lax.broadcasted_iota(jnp.int32, (tq, GROUP_W), 1) // NSA_HD
    out = jnp.zeros((tq, GROUP_W), F32)
    for h in range(NSA_HPG):
        out = out + jnp.where(head == h, o[h * tq:(h + 1) * tq], 0.0)
    return out


def _cmp_select_kernel(q_ref, k_ref, v_ref, ov_ref, o_ref, sel_ref, *, n_cmp, n_sel, q_pos0):
    tq = q_ref.shape[0]
    i = pl.program_id(2)
    qs = _stack_heads(q_ref[...] * (NSA_HD ** -0.5)).astype(BF16)
    s = lax.dot_general(qs, k_ref[...], (((1,), (1,)), ((), ())), preferred_element_type=F32)
    q_pos = q_pos0 + i * tq + lax.broadcasted_iota(jnp.int32, (tq, CMP_PAD), 0)
    c = lax.broadcasted_iota(jnp.int32, (tq, CMP_PAD), 1)
    mask1 = (c < n_cmp) & (c * CMP_STRIDE + (CMP_BLOCK - 1) <= q_pos)
    mask = jnp.concatenate([mask1] * NSA_HPG, axis=0)
    s = jnp.where(mask, s, NEG_BIG)
    m = jnp.max(s, axis=-1, keepdims=True)
    e = jnp.where(mask, jnp.exp(s - m), 0.0)
    den = jnp.sum(e, axis=-1, keepdims=True)
    prob = e / jnp.where(den > 0, den, 1.0)
    o = jnp.dot(prob.astype(BF16), v_ref[...], preferred_element_type=F32)
    o_ref[...] = _unstack_heads(o, tq)
    psum = prob[0:tq]
    for h in range(1, NSA_HPG):
        psum = psum + prob[h * tq:(h + 1) * tq]
    imp = jnp.dot(psum.astype(BF16), ov_ref[...], preferred_element_type=F32)
    qb = q_pos // SEL_BLOCK
    valid = (c <= qb) & (c < n_sel)
    forced = (c == 0) | (c == qb) | (c == qb - 1)
    score = jnp.where(valid, jnp.where(forced, FORCE_SCORE, imp), -jnp.inf)
    k_top = min(SEL_TOP, n_sel)
    few_blocks = (q_pos0 + (i + 1) * tq - 1) // SEL_BLOCK < k_top

    @pl.when(few_blocks)
    def _():
        sel_ref[...] = jnp.where(valid, 1.0, 0.0)

    @pl.when(jnp.logical_not(few_blocks))
    def _():
        rank = jnp.zeros((tq, CMP_PAD), F32)
        for jp in range(n_sel):
            col = score[:, jp:jp + 1]
            beats = (col > score) | ((col == score) & (c > jp))
            rank = rank + jnp.where(beats, 1.0, 0.0)
        sel_ref[...] = jnp.where((rank < k_top) & (c < n_sel), 1.0, 0.0)


def nsa_cmp_select(qn, kc4, vc4, ovT, *, n_cmp, n_sel, q_pos0):
    B, T, _ = qn.shape
    tq = min(ATT_Q_TILE, T)
    return pl.pallas_call(
        functools.partial(_cmp_select_kernel, n_cmp=n_cmp, n_sel=n_sel, q_pos0=q_pos0),
        grid=(B, NSA_G, T // tq),
        in_specs=[
            pl.BlockSpec((None, tq, GROUP_W), lambda b, g, i: (b, i, g)),
            pl.BlockSpec((None, None, CMP_PAD, GROUP_W), lambda b, g, i: (b, g, 0, 0)),
            pl.BlockSpec((None, None, CMP_PAD, GROUP_W), lambda b, g, i: (b, g, 0, 0)),
            pl.BlockSpec((CMP_PAD, CMP_PAD), lambda b, g, i: (0, 0)),
        ],
        out_specs=[
            pl.BlockSpec((None, tq, GROUP_W), lambda b, g, i: (b, i, g)),
            pl.BlockSpec((None, None, tq, CMP_PAD), lambda b, g, i: (b, g, i, 0)),
        ],
        out_shape=[jax.ShapeDtypeStruct((B, T, NSA_W), F32),
                   jax.ShapeDtypeStruct((B, NSA_G, T, CMP_PAD), F32)],
        compiler_params=pltpu.CompilerParams(
            dimension_semantics=("parallel", "parallel", "parallel"), vmem_limit_bytes=VMEM_LIMIT_BYTES),
        name="nsa_cmp_select",
    )(qn, kc4, vc4, ovT)


def _flash_kernel(*refs, selected):
    if selected:
        q_ref, k_ref, v_ref, sel_ref, exp_ref, o_ref, m_ref, l_ref, acc_ref, s_a, s_b = refs
    else:
        q_ref, k_ref, v_ref, o_ref, m_ref, l_ref, acc_ref, s_a, s_b = refs
    tq = q_ref.shape[0]
    tk = s_a.shape[1]
    n_tiles = k_ref.shape[0] // tk
    i = pl.program_id(2)
    q = q_ref[...] * (NSA_HD ** -0.5)
    head = lax.broadcasted_iota(jnp.int32, q.shape, 1) // NSA_HD
    q4 = _stack_heads(q).astype(BF16)
    m_ref[...] = jnp.full(m_ref.shape, NEG_BIG, F32)
    l_ref[...] = jnp.zeros(l_ref.shape, F32)
    acc_ref[...] = jnp.zeros(acc_ref.shape, F32)
    q_pos = i * tq + lax.broadcasted_iota(jnp.int32, (tq, tk), 0)
    col = lax.broadcasted_iota(jnp.int32, (tq, tk), 1)
    if selected:
        sel = sel_ref[...].astype(BF16)
        lo = 0
    else:
        lo = jnp.maximum(i * tq - (WINDOW - 1), 0) // tk
    hi = (i * tq + tq - 1) // tk + 1

    def tile_start(j):
        return pl.multiple_of(jnp.minimum(j, n_tiles - 1) * tk, tk)

    def scores(j, s_ref):
        s_ref[...] = lax.dot_general(q4, k_ref[pl.ds(tile_start(j), tk), :], (((1,), (1,)), ((), ())),
                                     preferred_element_type=F32)

    def consume(j, s_ref):
        v = v_ref[pl.ds(tile_start(j), tk), :]
        k_pos = j * tk + col
        mask = k_pos <= q_pos
        if selected:
            mask = mask & (jnp.dot(sel, exp_ref[jnp.minimum(j, n_tiles - 1)], preferred_element_type=F32) > 0.5)
        else:
            mask = mask & (q_pos - k_pos < WINDOW)
        if not selected:
            bias = jnp.where(mask, 0.0, 2.0 * NEG_BIG)
        for h in range(NSA_HPG):
            m_old = m_ref[h]
            if selected:
                s = jnp.where(mask, s_ref[h * tq:(h + 1) * tq, :], NEG_BIG)
                m_new = jnp.maximum(m_old, jnp.max(s, axis=-1, keepdims=True))
                p = jnp.where(mask, jnp.exp(s - pltpu.repeat(m_new, tk // LANE, axis=1)), 0.0)
            else:
                s = s_ref[h * tq:(h + 1) * tq, :] + bias
                m_new = jnp.maximum(m_old, jnp.max(s, axis=-1, keepdims=True))
                p = jnp.exp(s - pltpu.repeat(m_new, tk // LANE, axis=1))
            alpha = jnp.exp(m_old - m_new)
            l_ref[h] = alpha * l_ref[h] + jnp.sum(p, axis=-1, keepdims=True)
            acc_ref[h] = (pltpu.repeat(alpha, GROUP_W // LANE, axis=1) * acc_ref[h]
                          + jnp.dot(p.astype(BF16), v, preferred_element_type=F32))
            m_ref[h] = m_new

    scores(lo, s_a)

    def body(t, carry):
        j = lo + 2 * t
        scores(j + 1, s_b)
        consume(j, s_a)
        scores(j + 2, s_a)
        consume(j + 1, s_b)
        return carry

    lax.fori_loop(0, (hi - lo + 1) // 2, body, 0)
    out = jnp.zeros((tq, GROUP_W), F32)
    for h in range(NSA_HPG):
        den = pltpu.repeat(l_ref[h], GROUP_W // LANE, axis=1)
        out = out + jnp.where(head == h, acc_ref[h] / jnp.where(den > 0, den, 1.0), 0.0)
    o_ref[...] = out


def nsa_flash(qr, k4, v4, sel=None, expand=None):
    B, T, _ = qr.shape
    selected = sel is not None
    tq = ATT_Q_TILE if selected else ATT_Q_TILE_WIN
    tk = ATT_K_TILE if selected else ATT_K_TILE_WIN
    in_specs = [
        pl.BlockSpec((None, tq, GROUP_W), lambda b, g, i: (b, i, g)),
        pl.BlockSpec((None, T, GROUP_W), lambda b, g, i: (b, 0, g)),
        pl.BlockSpec((None, T, GROUP_W), lambda b, g, i: (b, 0, g)),
    ]
    args = [qr, k4, v4]
    if selected:
        in_specs += [
            pl.BlockSpec((None, None, tq, CMP_PAD), lambda b, g, i: (b, g, i, 0)),
            pl.BlockSpec(expand.shape, lambda b, g, i: (0, 0, 0)),
        ]
        args += [sel, expand]
    return pl.pallas_call(
        functools.partial(_flash_kernel, selected=selected),
        grid=(B, NSA_G, T // tq),
        in_specs=in_specs,
        out_specs=pl.BlockSpec((None, tq, GROUP_W), lambda b, g, i: (b, i, g)),
        out_shape=jax.ShapeDtypeStruct((B, T, NSA_W), F32),
        scratch_shapes=[pltpu.VMEM((NSA_HPG, tq, LANE), F32), pltpu.VMEM((NSA_HPG, tq, LANE), F32),
                        pltpu.VMEM((NSA_HPG, tq, GROUP_W), F32),
                        pltpu.VMEM((NSA_HPG * tq, tk), F32), pltpu.VMEM((NSA_HPG * tq, tk), F32)],
        compiler_params=pltpu.CompilerParams(
            dimension_semantics=("parallel", "parallel", "parallel"), vmem_limit_bytes=VMEM_LIMIT_BYTES),
        name="nsa_flash_sel" if selected else "nsa_flash_win",
    )(*args)


def _tile_cmp(x):
    B, n = x.shape[:2]
    x = jnp.pad(jnp.moveaxis(x, 1, 2), ((0, 0), (0, 0), (0, CMP_PAD - n), (0, 0)))
    return jnp.tile(x, (1, 1, 1, NSA_HPG)).astype(BF16)


def _overlap_T(n_cmp, n_sel):
    ov = np.zeros((CMP_PAD, CMP_PAD), np.float32)
    cs = np.arange(n_cmp) * CMP_STRIDE
    ss = np.arange(n_sel) * SEL_BLOCK
    o = np.minimum(cs[None] + CMP_BLOCK, ss[:, None] + SEL_BLOCK) - np.maximum(cs[None], ss[:, None])
    ov[:n_cmp, :n_sel] = (np.clip(o, 0, None) / CMP_BLOCK).T
    return jnp.asarray(ov, dtype=BF16)


def _sel_expand(T):
    t = np.arange(T)
    e = (np.arange(CMP_PAD)[:, None] == (t // SEL_BLOCK)[None, :]).astype(np.float32)
    return jnp.asarray(e.reshape(CMP_PAD, T // ATT_K_TILE, ATT_K_TILE).transpose(1, 0, 2), dtype=BF16)


KV_ROWS = KV_SLOTS * NSA_G * NSA_HD
SLOT_ROWS = NSA_G * NSA_HD
N_PAGES = PAST_LEN // PAGE_SIZE
DEC_N_CHUNK = (PAST_LEN + DEC_SEQ) // CMP_STRIDE
DEC_N_CMP = DEC_N_CHUNK - CMP_R + 1
DEC_N_SEL = -(-(PAST_LEN + DEC_SEQ) // SEL_BLOCK)
WIN_BUF = min(WINDOW, PAST_LEN)


def _softmax_rows(s, mask, s_new=None):
    s = jnp.where(mask, s, NEG_BIG)
    m = jnp.max(s, axis=-1, keepdims=True)
    if s_new is not None:
        m = jnp.maximum(m, s_new)
    e = jnp.where(mask, jnp.exp(s - m), 0.0)
    den = jnp.sum(e, axis=-1, keepdims=True)
    if s_new is None:
        return e, den
    e_new = jnp.exp(s_new - m)
    return e, e_new, den + e_new


def _dec_nsa_kernel(pt_ref, *refs):
    pages = refs[:N_PAGES]
    (win_ref, qn_ref, qr_ref, new_ref, gate_ref, w1_ref, b1_ref, w2_ref, b2_ref, kn_ref,
     ov_ref, exp_ref, grp_ref, perm_ref, o_ref, xt_ref, acc_ref) = refs[N_PAGES:]
    del pt_ref
    f32 = F32
    half = 2 * NSA_HD
    n_chunk = DEC_N_CHUNK

    perm = perm_ref[...]
    per_page = PAGE_SIZE // CMP_STRIDE
    for p in range(N_PAGES):
        for sg in range(4):
            tile = pages[p][sg * half:(sg + 1) * half, :].astype(BF16)
            xt = lax.dot_general(perm, tile, (((1,), (1,)), ((), ())), preferred_element_type=f32)
            for r in range(CMP_STRIDE):
                xt_ref[sg, r, p * per_page:(p + 1) * per_page, :] = xt[r * per_page:(r + 1) * per_page, :]

    lane_lo = lax.broadcasted_iota(jnp.int32, (n_chunk, half), 1) < NSA_HD
    lane_grp = lax.broadcasted_iota(jnp.int32, (n_chunk, SLOT_ROWS), 1) // NSA_HD
    cmp_rows = []
    for slot in range(2):
        for gp in range(2):
            los, his = [], []
            for rp in range(CMP_STRIDE // 2):
                x0, x1 = (xt_ref[slot * 2 + gp, 2 * rp + j] for j in range(2))
                los.append(jnp.where(lane_lo, x0, pltpu.roll(x1, NSA_HD, 1)))
                his.append(jnp.where(lane_lo, pltpu.roll(x0, NSA_HD, 1), x1))
            lhs = jnp.concatenate([jnp.concatenate(los, axis=1), jnp.concatenate(his, axis=1)], axis=0).astype(BF16)
            acc_ref[pl.ds(gp * 2 * n_chunk, 2 * n_chunk), :] = jnp.dot(lhs, w1_ref[slot], preferred_element_type=f32)
        acc = acc_ref[...]
        pre = b1_ref[slot] + acc[:, :CMP_HID] + pltpu.roll(acc[:, CMP_HID:], NSA_G * n_chunk - 1, 0)
        out = jnp.dot(jax.nn.gelu(pre).astype(BF16), w2_ref[slot], preferred_element_type=f32) + b2_ref[slot]
        if slot == 0:
            out = _rms_rows(out, kn_ref[...])
        sel_rows = jnp.zeros((n_chunk, SLOT_ROWS), f32)
        for g in range(NSA_G):
            sel_rows = sel_rows + jnp.where(lane_grp == g, out[g * n_chunk:(g + 1) * n_chunk], 0.0)
        cmp_rows.append(sel_rows.astype(BF16))
    kc, vc = cmp_rows

    qn = qn_ref[...].astype(BF16)
    qr = qr_ref[...].astype(BF16)
    nt = (((1,), (1,)), ((), ()))
    c = lax.broadcasted_iota(jnp.int32, (NSA_H, CMP_PAD), 1)
    s = lax.dot_general(qn, kc, nt, preferred_element_type=f32)
    e, den = _softmax_rows(s, c < DEC_N_CMP)
    prob = e / jnp.where(den > 0, den, 1.0)
    o_cmp = jnp.dot(prob.astype(BF16), vc, preferred_element_type=f32)
    p_hi, p_mid = _split_bf16(prob)
    p_lo = (prob - p_hi.astype(f32) - p_mid.astype(f32)).astype(BF16)
    grp = grp_ref[...]
    psum = (jnp.dot(grp, p_hi, preferred_element_type=f32) + jnp.dot(grp, p_mid, preferred_element_type=f32)
            + jnp.dot(grp, p_lo, preferred_element_type=f32))
    imp = jnp.dot(psum.astype(BF16), ov_ref[...], preferred_element_type=f32)
    qb = (PAST_LEN + DEC_SEQ - 1) // SEL_BLOCK
    valid = c <= qb
    forced = (c == 0) | (c == qb) | (c == qb - 1)
    score = jnp.where(valid, jnp.where(forced, FORCE_SCORE, imp), -jnp.inf)
    rank = jnp.zeros((NSA_H, CMP_PAD), f32)
    for jp in range(DEC_N_SEL):
        col = score[:, jp:jp + 1]
        rank = rank + jnp.where((col > score) | ((col == score) & (c > jp)), 1.0, 0.0)
    sel = jnp.where((rank < min(SEL_TOP, DEC_N_SEL)) & (c < DEC_N_SEL), 1.0, 0.0).astype(BF16)

    new = new_ref[...]
    new_b = new.astype(BF16).astype(f32)
    qr_f = qr.astype(f32)
    s_pages = [jnp.dot(qr, pages[p][2 * SLOT_ROWS:3 * SLOT_ROWS, :].astype(BF16), preferred_element_type=f32)
               for p in range(N_PAGES)]
    s = jnp.concatenate(s_pages, axis=1)
    mask = jnp.dot(sel, exp_ref[...], preferred_element_type=f32) > 0.5
    s_new = jnp.sum(qr_f * new_b[0:1], axis=-1, keepdims=True)
    e, e_new, den = _softmax_rows(s, mask, s_new)
    e = e.astype(BF16)
    o_slc = e_new.astype(BF16).astype(f32) * new_b[1:2]
    for p in range(N_PAGES):
        o_slc = o_slc + lax.dot_general(e[:, p * PAGE_SIZE:(p + 1) * PAGE_SIZE],
                                        pages[p][3 * SLOT_ROWS:4 * SLOT_ROWS, :].astype(BF16), nt,
                                        preferred_element_type=f32)
    o_slc = o_slc / den

    s = jnp.dot(qr, win_ref[0:SLOT_ROWS, :].astype(BF16), preferred_element_type=f32)
    i_buf = lax.broadcasted_iota(jnp.int32, (NSA_H, WIN_BUF), 1)
    s_new = jnp.sum(qr_f * new_b[2:3], axis=-1, keepdims=True)
    e, e_new, den = _softmax_rows(s, WIN_BUF - i_buf < WINDOW, s_new)
    o_win = e_new.astype(BF16).astype(f32) * new_b[3:4] + lax.dot_general(
        e.astype(BF16), win_ref[SLOT_ROWS:2 * SLOT_ROWS, :].astype(BF16), nt, preferred_element_type=f32)
    o_win = o_win / den

    gates = gate_ref[...]
    o_ref[...] = gates[:, 0:1] * o_cmp + gates[:, 1:2] * o_slc + gates[:, 2:3] * o_win


def dec_nsa(page_table, cache_t, win_t, qn16, qr16, new_rows, gates, w1t, b1, w2t, b2t, kn, ovT, expand, grp):
    DB = qn16.shape[0]
    per_page = PAGE_SIZE // CMP_STRIDE
    tok = np.arange(PAGE_SIZE)
    perm = jnp.asarray((tok[:, None] // per_page == tok[None, :] % CMP_STRIDE)
                       & (tok[:, None] % per_page == tok[None, :] // CMP_STRIDE), dtype=BF16)
    const = lambda shape: pl.BlockSpec(shape, lambda b, pt: (0,) * len(shape))
    per_b = lambda shape: pl.BlockSpec((None,) + shape, lambda b, pt: (b,) + (0,) * len(shape))
    page_specs = [pl.BlockSpec((None, KV_ROWS, PAGE_SIZE), functools.partial(lambda b, pt, p: (pt[b, p], 0, 0), p=p))
                  for p in range(N_PAGES)]
    in_specs = page_specs + [
        per_b((2 * SLOT_ROWS, WIN_BUF)), per_b((NSA_H, SLOT_ROWS)), per_b((NSA_H, SLOT_ROWS)),
        per_b((4, SLOT_ROWS)), per_b((NSA_H, 3)),
        const(w1t.shape), const(b1.shape), const(w2t.shape), const(b2t.shape), const(kn.shape),
        const(ovT.shape), const(expand.shape), const(grp.shape), const(perm.shape),
    ]
    grid_spec = pltpu.PrefetchScalarGridSpec(
        num_scalar_prefetch=1, grid=(DB,), in_specs=in_specs,
        out_specs=pl.BlockSpec((None, NSA_H, SLOT_ROWS), lambda b, pt: (b, 0, 0)),
        scratch_shapes=[pltpu.VMEM((4, CMP_STRIDE, DEC_N_CHUNK, 2 * NSA_HD), F32),
                        pltpu.VMEM((NSA_G * DEC_N_CHUNK, CMP_R * CMP_HID), F32)])
    return pl.pallas_call(
        _dec_nsa_kernel,
        grid_spec=grid_spec,
        out_shape=jax.ShapeDtypeStruct((DB, NSA_H, SLOT_ROWS), F32),
        compiler_params=pltpu.CompilerParams(
            dimension_semantics=("arbitrary",), vmem_limit_bytes=VMEM_LIMIT_BYTES),
        name="dec_nsa",
    )(page_table, *([cache_t] * N_PAGES), win_t, qn16, qr16, new_rows, gates, w1t, b1, w2t, b2t, kn, ovT, expand, grp,
      perm)


WIN_SEQS_PER_STEP = 4


def _win_shift_kernel(win_ref, new_ref, o_ref):
    shape = win_ref.shape[1:]
    n = shape[1]
    row = lax.broadcasted_iota(jnp.int32, shape, 0)
    lane = lax.broadcasted_iota(jnp.int32, shape, 1)
    for s in range(win_ref.shape[0]):
        w = win_ref[s]
        col = jnp.sum(jnp.where(row == lane, jnp.broadcast_to(new_ref[s], shape), 0.0), axis=1, keepdims=True)
        o_ref[s] = jnp.where(lane == n - 1, col, pltpu.roll(w, n - 1, 1))


def win_shift(win_t, new_row):
    DB, R, W = win_t.shape
    ns = WIN_SEQS_PER_STEP
    assert R == W and DB % ns == 0
    return pl.pallas_call(
        _win_shift_kernel,
        grid=(DB // ns,),
        in_specs=[pl.BlockSpec((ns, R, W), lambda b: (b, 0, 0)), pl.BlockSpec((ns, 1, R), lambda b: (b, 0, 0))],
        out_specs=pl.BlockSpec((ns, R, W), lambda b: (b, 0, 0)),
        out_shape=jax.ShapeDtypeStruct((DB, R, W), win_t.dtype),
        compiler_params=pltpu.CompilerParams(dimension_semantics=("parallel",), vmem_limit_bytes=VMEM_LIMIT_BYTES),
        name="win_shift",
    )(win_t, new_row)


def _dec_cmp_weights(w1, b1, w2, b2):
    w = jnp.moveaxis(w1, 0, 1).reshape(CMP_STRIDE * NSA_HD, CMP_R * CMP_HID)
    return (w.astype(BF16), b1.reshape(1, CMP_HID), jnp.tile(w2, (1, NSA_G)).astype(BF16),
            jnp.tile(b2, NSA_G).reshape(1, SLOT_ROWS))


def _place_heads(q):
    own = (jnp.arange(NSA_H)[:, None] // NSA_HPG) == jnp.arange(NSA_G)[None, :]
    return jnp.where(own[None, :, :, None], q[:, :, None, :], 0.0).reshape(q.shape[0], NSA_H, SLOT_ROWS)


def _take_heads(o):
    o = o.reshape(o.shape[0], NSA_H, NSA_G, NSA_HD)
    return o[:, jnp.arange(NSA_H), jnp.arange(NSA_H) // NSA_HPG, :].reshape(o.shape[0], NSA_W)


WKV_C = 64
WKV_PAIR = 2 * RWKV_HD
WKV_T_TILE = 512
WKV_PAIRS_PER_STEP = 8


def _split_bf16(x):
    hi = x.astype(BF16)
    return hi, (x - hi.astype(F32)).astype(BF16)


def _dot3(a, b):
    a_hi, a_lo = _split_bf16(a)
    b_hi, b_lo = _split_bf16(b)
    return (jnp.dot(a_hi, b_hi, preferred_element_type=F32) + jnp.dot(a_hi, b_lo, preferred_element_type=F32)
            + jnp.dot(a_lo, b_hi, preferred_element_type=F32))


def _wkv_kernel(r_ref, lw_ref, k_ref, v_ref, a_ref, b_ref, s0_ref, y_ref, sT_ref, s_scr):
    C = WKV_C
    P = WKV_PAIR
    n_chunks = r_ref.shape[0] // C

    @pl.when(pl.program_id(2) == 0)
    def _():
        s_scr[...] = s0_ref[...]

    lo_lane = lax.broadcasted_iota(jnp.int32, (C, P), 1) < RWKV_HD
    row = lax.broadcasted_iota(jnp.int32, (2 * C, 2 * C), 0)
    col = lax.broadcasted_iota(jnp.int32, (2 * C, 2 * C), 1)
    same_head = (row // C) == (col // C)
    strict = same_head & (row > col)
    lower = same_head & (row >= col)
    eye = jnp.where(row == col, 1.0, 0.0)
    tril = jnp.where(lax.broadcasted_iota(jnp.int32, (C, C), 0) >= lax.broadcasted_iota(jnp.int32, (C, C), 1),
                     1.0, 0.0).astype(BF16)

    def stack(x):
        return jnp.concatenate([jnp.where(lo_lane, x, 0.0), jnp.where(lo_lane, 0.0, x)], axis=0)

    def chunk(c, carry):
        stages = [pair_chunk(c, q) for q in range(WKV_PAIRS_PER_STEP)]
        while stages:
            stages = [g for g in stages if next(g, True) is None]
        return carry

    def pair_chunk(c, q):
        sl = pl.ds(pl.multiple_of(c * C, C), C)
        lanes = slice(q * P, (q + 1) * P)
        r, lw, k, v, a, b = (ref[sl, lanes] for ref in (r_ref, lw_ref, k_ref, v_ref, a_ref, b_ref))
        lw_hi, lw_mid = _split_bf16(lw)
        lw_lo = (lw - lw_hi.astype(F32) - lw_mid.astype(F32)).astype(BF16)
        cs = (jnp.dot(tril, lw_hi, preferred_element_type=F32) + jnp.dot(tril, lw_mid, preferred_element_type=F32)
              + jnp.dot(tril, lw_lo, preferred_element_type=F32))
        yield
        g_inv = jnp.exp(-cs)
        g_end = jnp.exp(cs[C - 1:C, :] - cs)
        a2 = stack(a * jnp.exp(cs - lw))
        r2 = stack(r * jnp.exp(cs))
        b2 = stack(b * g_inv)
        k2 = stack(k * g_inv)
        v2 = stack(v)
        s_old = s_scr[q]
        ar = jnp.concatenate([a2, r2], axis=0).astype(BF16)
        bk = jnp.concatenate([b2, k2], axis=0).astype(BF16)
        nt = (((1,), (1,)), ((), ()))
        pp = lax.dot_general(ar, bk, nt, preferred_element_type=F32)
        from_state = lax.dot_general(ar, s_old.astype(BF16), nt, preferred_element_type=F32)
        yield
        l_ab = jnp.where(strict, pp[:2 * C, :2 * C], 0.0)
        l_ak = jnp.where(strict, pp[:2 * C, 2 * C:], 0.0)
        m_rb = jnp.where(lower, pp[2 * C:, :2 * C], 0.0)
        m_rk = jnp.where(lower, pp[2 * C:, 2 * C:], 0.0)
        v2b = v2.astype(BF16)
        rhs = from_state[:2 * C] + jnp.dot(l_ak.astype(BF16), v2b, preferred_element_type=F32)
        yield
        n = l_ab
        x = eye + n
        span = 2
        while span < C:
            n = _dot3(n, n)
            yield
            x = x + _dot3(n, x)
            yield
            span *= 2
        u2 = _dot3(x, rhs)
        yield
        uv = jnp.concatenate([u2, v2], axis=0).astype(BF16)
        y2 = from_state[2 * C:] + jnp.dot(jnp.concatenate([m_rb, m_rk], axis=1).astype(BF16), uv,
                                          preferred_element_type=F32)
        yield
        y_ref[sl, lanes] = y2[:C] + y2[C:]
        bk_end = jnp.concatenate([stack(b * g_end), stack(k * g_end)], axis=0).astype(BF16)
        s_scr[q] = s_old * jnp.exp(cs[C - 1:C, :]) + lax.dot_general(
            uv, bk_end, (((0,), (0,)), ((), ())), preferred_element_type=F32)

    lax.fori_loop(0, n_chunks, chunk, 0)

    @pl.when(pl.program_id(2) == pl.num_programs(2) - 1)
    def _():
        sT_ref[...] = s_scr[...]


def wkv7_chunked(r, lw, k, v, a, b, s0):
    B, T, W = r.shape
    n_pair = W // WKV_PAIR
    tt = min(WKV_T_TILE, T)
    s0p = s0.astype(F32).reshape(B, n_pair, 2, RWKV_HD, RWKV_HD)
    zero = jnp.zeros_like(s0p[:, :, 0])
    s0_bd = jnp.concatenate([jnp.concatenate([s0p[:, :, 0], zero], axis=-1),
                             jnp.concatenate([zero, s0p[:, :, 1]], axis=-1)], axis=-2)
    pps = WKV_PAIRS_PER_STEP
    seq = pl.BlockSpec((None, tt, pps * WKV_PAIR), lambda i, p, t: (i, t, p))
    st = pl.BlockSpec((None, pps, WKV_PAIR, WKV_PAIR), lambda i, p, t: (i, p, 0, 0))
    y, s_bd = pl.pallas_call(
        _wkv_kernel,
        grid=(B, n_pair // pps, T // tt),
        in_specs=[seq] * 6 + [st],
        out_specs=[seq, st],
        out_shape=[jax.ShapeDtypeStruct((B, T, W), F32),
                   jax.ShapeDtypeStruct((B, n_pair, WKV_PAIR, WKV_PAIR), F32)],
        scratch_shapes=[pltpu.VMEM((pps, WKV_PAIR, WKV_PAIR), F32)],
        compiler_params=pltpu.CompilerParams(
            dimension_semantics=("parallel", "parallel", "arbitrary"), vmem_limit_bytes=VMEM_LIMIT_BYTES),
        name="wkv7_chunked",
    )(r, lw, k, v, a, b, s0_bd)
    s_fin = jnp.stack([s_bd[:, :, :RWKV_HD, :RWKV_HD], s_bd[:, :, RWKV_HD:, RWKV_HD:]], axis=2)
    return y, s_fin.reshape(B, W // RWKV_HD, RWKV_HD, RWKV_HD)


AB_PAD = _round_up(AB_COLS, WIDE_COL_TILE)
SHIFT_PAD = _round_up(SHIFT_W, LANE)
LORA_PAD = SHIFT_PAD - 3 * RWKV_W
EVEN_ROWS = 256
POST_ROWS = 256
N_EVEN_PRE_OUT = 10


def _split3(x):
    hi = x.astype(BF16)
    r1 = x - hi.astype(F32)
    mid = r1.astype(BF16)
    return hi, mid, (r1 - mid.astype(F32)).astype(BF16)


def _dot_01(x, m):
    return sum(jnp.dot(part, m, preferred_element_type=F32) for part in _split3(x))


def _head_sum(x, red_ref, exp_ref):
    return _dot_01(_dot_01(x, red_ref[...]), exp_ref[...])


def _expm1(x):
    u = jnp.exp(x)
    d = u - 1.0
    log_u = jnp.where((d == 0.0) | (d == -1.0), 1.0, jnp.log(u))
    return jnp.where(d == 0.0, x, jnp.where(d == -1.0, -1.0, d * x / log_u))


def _even_pre_math(x_ref, prev, taps, prm, outs):
    (cw_ref, cb_ref, wa_ref, ba_ref, wx_ref, bx_ref, lam_ref, mu_ref, w0_ref, a0_ref, wl_ref,
     kk_ref, ka_ref, red_ref, exp_ref) = prm
    a_o, u_o, gate_o, r_o, lw_o, k_o, v_o, na_o, nb_o, g_o = outs
    t1, t2, t3 = taps
    xb = x_ref[:, 0:LRU_W]
    xc = cb_ref[...] + cw_ref[0:1] * t3 + cw_ref[1:2] * t2 + cw_ref[2:3] * t1 + cw_ref[3:4] * xb
    xcb = xc.astype(BF16)
    gate_r = jax.nn.sigmoid(jnp.dot(xcb, wa_ref[...], preferred_element_type=F32) + ba_ref[...])
    gate_i = jax.nn.sigmoid(jnp.dot(xcb, wx_ref[...], preferred_element_type=F32) + bx_ref[...])
    log_a = -LRU_C * gate_r * lam_ref[...]
    a_o[...] = jnp.exp(log_a)
    u_o[...] = jnp.sqrt(-_expm1(2.0 * log_a)) * (gate_i * xc)
    gate_o[...] = jax.nn.gelu(x_ref[:, LRU_W:2 * LRU_W])
    rw = x_ref[:, 2 * LRU_W:2 * LRU_W + SHIFT_PAD]
    rs = rw + mu_ref[...] * (prev - rw)
    r_o[...] = rs[:, 0:RWKV_W]
    k = rs[:, RWKV_W:2 * RWKV_W]
    v_o[...] = rs[:, 2 * RWKV_W:3 * RWKV_W]
    tail = rs[:, 3 * RWKV_W:]
    lane = lax.broadcasted_iota(jnp.int32, tail.shape, 1)
    act = jnp.where(lane < W_LORA, jnp.tanh(tail), jnp.where(lane < W_LORA + A_LORA, tail, jax.nn.sigmoid(tail)))
    z = jnp.dot(act.astype(BF16), wl_ref[...], preferred_element_type=F32)
    w_log = -jax.nn.softplus(-(w0_ref[...] + z[:, 0:RWKV_W])) - 0.5
    lw_o[...] = -jnp.exp(w_log)
    a_icl = jax.nn.sigmoid(a0_ref[...] + z[:, RWKV_W:2 * RWKV_W])
    g_o[...] = z[:, 2 * RWKV_W:]
    kk = k * kk_ref[...]
    kk = kk / jnp.maximum(jnp.sqrt(_head_sum(kk * kk, red_ref, exp_ref)), 1e-12)
    k_o[...] = k * (1.0 + (a_icl - 1.0) * ka_ref[...])
    na_o[...] = -kk
    nb_o[...] = kk * a_icl


def _even_pre_seq_kernel(x_ref, conv0_ref, shift0_ref, *refs):
    prm = refs[:15]
    outs = refs[15:15 + N_EVEN_PRE_OUT]
    conv_c, shift_c = refs[15 + N_EVEN_PRE_OUT:]
    rows = x_ref.shape[0]

    @pl.when(pl.program_id(1) == 0)
    def _():
        conv_c[...] = conv0_ref[...]
        shift_c[...] = shift0_ref[...]

    xb = x_ref[:, 0:LRU_W]
    row = lax.broadcasted_iota(jnp.int32, xb.shape, 0)
    taps = []
    for j in (1, 2, 3):
        tap = pltpu.roll(xb, j, 0)
        for i in range(j):
            tap = jnp.where(row == i, conv_c[8 - j + i:9 - j + i, :], tap)
        taps.append(tap)
    rw = x_ref[:, 2 * LRU_W:2 * LRU_W + SHIFT_PAD]
    row_w = lax.broadcasted_iota(jnp.int32, rw.shape, 0)
    prev = jnp.where(row_w == 0, shift_c[7:8, :], pltpu.roll(rw, 1, 0))
    _even_pre_math(x_ref, prev, taps, prm, outs)
    conv_c[...] = x_ref[rows - 8:rows, 0:LRU_W]
    shift_c[...] = x_ref[rows - 8:rows, 2 * LRU_W:2 * LRU_W + SHIFT_PAD]


def _even_pre_step_kernel(x_ref, prev_ref, t1_ref, t2_ref, t3_ref, *refs):
    _even_pre_math(x_ref, prev_ref[...], (t1_ref[...], t2_ref[...], t3_ref[...]), refs[:15], refs[15:])


def _even_params(p):
    def bd(w):
        eye = jnp.eye(LRU_BLOCKS, dtype=w.dtype)
        return (eye[:, None, :, None] * w[:, :, None, :]).reshape(LRU_W, LRU_W).astype(BF16)
    row = lambda v: v.reshape(1, -1).astype(F32)
    wl = jnp.zeros((LORA_PAD, 3 * RWKV_W), F32)
    wl = wl.at[0:W_LORA, 0:RWKV_W].set(p['w2'])
    wl = wl.at[W_LORA:W_LORA + A_LORA, RWKV_W:2 * RWKV_W].set(p['a2'])
    wl = wl.at[W_LORA + A_LORA:W_LORA + A_LORA + G_LORA, 2 * RWKV_W:].set(p['g2'])
    head = np.arange(RWKV_W) // RWKV_HD
    red = jnp.asarray(head[:, None] == np.arange(LANE)[None, :], dtype=BF16)
    mu = jnp.pad(p['mu'], (0, SHIFT_PAD - SHIFT_W))
    return [p['conv_w'].astype(F32), row(p['conv_b']), bd(p['wa']), row(p['ba']), bd(p['wx']), row(p['bx']),
            row(jax.nn.softplus(-p['lam'].astype(F32))), row(mu), row(p['w0']), row(p['a0']), wl.astype(BF16),
            row(p['k_k']), row(p['k_a']), red, red.T]


def _const_spec(a, n_grid):
    return pl.BlockSpec(a.shape, lambda *_: (0,) * a.ndim)


def even_pre_seq(proj, conv0, shift0, prm, B, T):
    tr = EVEN_ROWS
    nt = T // tr
    conv_pad = jnp.pad(conv0.astype(F32), ((0, 0), (8 - (CONV_W - 1), 0), (0, 0)))
    shift_pad = jnp.pad(shift0.astype(F32)[:, None, :], ((0, 0), (7, 0), (0, SHIFT_PAD - SHIFT_W)))
    out_spec = pl.BlockSpec((tr, LRU_W), lambda b, t: (b * nt + t, 0))
    return pl.pallas_call(
        _even_pre_seq_kernel,
        grid=(B, nt),
        in_specs=[pl.BlockSpec((tr, AB_PAD), lambda b, t: (b * nt + t, 0)),
                  pl.BlockSpec((None, 8, LRU_W), lambda b, t: (b, 0, 0)),
                  pl.BlockSpec((None, 8, SHIFT_PAD), lambda b, t: (b, 0, 0))] + [_const_spec(a, 2) for a in prm],
        out_specs=[out_spec] * N_EVEN_PRE_OUT,
        out_shape=[jax.ShapeDtypeStruct((B * T, LRU_W), F32)] * N_EVEN_PRE_OUT,
        scratch_shapes=[pltpu.VMEM((8, LRU_W), F32), pltpu.VMEM((8, SHIFT_PAD), F32)],
        compiler_params=pltpu.CompilerParams(
            dimension_semantics=("parallel", "arbitrary"), vmem_limit_bytes=VMEM_LIMIT_BYTES),
        name="even_pre_seq",
    )(proj, conv_pad, shift_pad, *prm)


def even_pre_step(proj, row0, conv0, shift0, prm):
    n = conv0.shape[0]
    shift_pad = jnp.pad(shift0.astype(F32), ((0, 0), (0, SHIFT_PAD - SHIFT_W)))
    taps = [conv0[:, CONV_W - 1 - j].astype(F32) for j in (1, 2, 3)]
    full = lambda w: pl.BlockSpec((n, w), lambda i: (0, 0))
    return pl.pallas_call(
        _even_pre_step_kernel,
        grid=(1,),
        in_specs=[pl.BlockSpec((n, AB_PAD), lambda i: (row0 // n, 0)), full(SHIFT_PAD)] + [full(LRU_W)] * 3
        + [_const_spec(a, 1) for a in prm],
        out_specs=[full(LRU_W)] * N_EVEN_PRE_OUT,
        out_shape=[jax.ShapeDtypeStruct((n, LRU_W), F32)] * N_EVEN_PRE_OUT,
        compiler_params=pltpu.CompilerParams(
            dimension_semantics=("arbitrary",), vmem_limit_bytes=VMEM_LIMIT_BYTES),
        name="even_pre_step",
    )(proj, shift_pad, *taps, *prm)


def _even_post_kernel(hs_ref, gate_ref, y_ref, r_ref, k_ref, v_ref, g_ref, lng_ref, lnb_ref, rk_ref,
                      red_ref, exp_ref, *rest):
    o_ref = rest[-1]
    y = y_ref[...]
    mu = _head_sum(y, red_ref, exp_ref) * (1.0 / RWKV_HD)
    d = y - mu
    var = _head_sum(d * d, red_ref, exp_ref) * (1.0 / RWKV_HD)
    yn = d * lax.rsqrt(var + 64e-5) * lng_ref[...] + lnb_ref[...]
    bonus = _head_sum(r_ref[...] * k_ref[...] * rk_ref[...], red_ref, exp_ref) * v_ref[...]
    o_ref[:, 0:LRU_W] = (hs_ref[...] * gate_ref[...]).astype(o_ref.dtype)
    o_ref[:, LRU_W:] = ((yn + bonus) * g_ref[...]).astype(o_ref.dtype)


def even_post(hs, gate, y, r, k, v, g, p, red, n_total, row0, prior=None):
    n = hs.shape[0]
    tr = min(POST_ROWS, n)
    row = lambda a: a.reshape(1, -1).astype(F32)
    consts = [row(p['ln_g']), row(p['ln_b']), row(p['r_k']), red, red.T]
    seq = pl.BlockSpec((tr, LRU_W), lambda i: (i, 0))
    args = [hs, gate, y, r, k, v, g] + consts
    in_specs = [seq] * 7 + [_const_spec(a, 1) for a in consts]
    aliases = {}
    if prior is not None:
        args.append(prior)
        in_specs.append(pl.BlockSpec(memory_space=pl.ANY))
        aliases = {len(args) - 1: 0}
    return pl.pallas_call(
        _even_post_kernel,
        grid=(n // tr,),
        in_specs=in_specs,
        out_specs=pl.BlockSpec((tr, D_MODEL), lambda i: (row0 // tr + i, 0)),
        out_shape=jax.ShapeDtypeStruct((n_total, D_MODEL), BF16),
        input_output_aliases=aliases,
        compiler_params=pltpu.CompilerParams(
            dimension_semantics=("parallel",), vmem_limit_bytes=VMEM_LIMIT_BYTES),
        name="even_post",
    )(*args)


def _retention_kernel(q_ref, k_ref, va_ref, vb_ref, dm_ref, rd_ref, kd_ref, sd_ref, o_ref, s_out_ref, s_scr):
    C = q_ref.shape[0]
    n_pair = RET_H // 2

    @pl.when(pl.program_id(1) == 0)
    def _():
        s_scr[...] = jnp.zeros(s_scr.shape, F32)

    lo = lax.broadcasted_iota(jnp.int32, (C, 2 * RET_DK), 1) < RET_DK

    def stack(x):
        return jnp.concatenate([jnp.where(lo, x, 0.0), jnp.where(lo, 0.0, x)], axis=0)

    for p in range(n_pair):
        qk = slice(p * 2 * RET_DK, (p + 1) * 2 * RET_DK)
        q2 = stack(q_ref[:, qk]).astype(BF16)
        k2 = stack(k_ref[:, qk])
        v0 = p * 2 * RET_DV
        v_ref = va_ref if p < n_pair // 2 else vb_ref
        vl = v0 % (RET_W // 2)
        v2 = jnp.concatenate([v_ref[:, vl:vl + RET_DV], v_ref[:, vl + RET_DV:vl + 2 * RET_DV]],
                             axis=0).astype(BF16)
        s = lax.dot_general(q2, k2.astype(BF16), (((1,), (1,)), ((), ())), preferred_element_type=F32) * dm_ref[p]
        s_old = s_scr[p]
        o2 = jnp.dot(s.astype(BF16), v2, preferred_element_type=F32) + jnp.dot(
            q2, s_old.astype(BF16), preferred_element_type=F32) * rd_ref[p]
        o_ref[:, v0:v0 + RET_DV] = o2[:C]
        o_ref[:, v0 + RET_DV:v0 + 2 * RET_DV] = o2[C:]
        s_scr[p] = s_old * sd_ref[p] + lax.dot_general((k2 * kd_ref[p]).astype(BF16), v2, (((0,), (0,)), ((), ())),
                                                       preferred_element_type=F32)

    @pl.when(pl.program_id(1) == pl.num_programs(1) - 1)
    def _():
        s_out_ref[...] = s_scr[...]


def retention_prompt_pallas(rq, rk, rv, B, T, v_col0=0):
    C = RET_CHUNK
    nc = T // C
    f32 = F32
    lg = jnp.log1p(-jnp.exp2(-5.0 - jnp.arange(RET_H, dtype=f32))).reshape(RET_H // 2, 2)
    i = jnp.arange(C, dtype=f32)
    diff = i[:, None] - i[None, :]
    causal = diff >= 0
    dmask = jnp.where(causal, jnp.exp(jnp.where(causal, diff, 0.0)[None, None] * lg[:, :, None, None]), 0.0)
    zero = jnp.zeros_like(dmask[:, 0])
    dm = jnp.concatenate([jnp.concatenate([dmask[:, 0], zero], axis=-1),
                          jnp.concatenate([zero, dmask[:, 1]], axis=-1)], axis=-2)
    rows = lambda x, w: jnp.broadcast_to(x[:, :, :, None], x.shape + (w,)).reshape(RET_H // 2, -1, w)
    rd = rows(jnp.exp((i[None, None, :] + 1.0) * lg[:, :, None]), RET_DV)
    kd = rows(jnp.exp((C - 1.0 - i)[None, None, :] * lg[:, :, None]), 2 * RET_DK)
    sd = rows(jnp.broadcast_to(jnp.exp(C * lg)[:, :, None], (RET_H // 2, 2, RET_DK)), RET_DV)
    half_w = RET_W // 2
    qk_spec = pl.BlockSpec((C, RET_H * RET_DK), lambda b, c: (b * nc + c, 0))
    v_spec = lambda k: pl.BlockSpec((C, half_w), lambda b, c: (b * nc + c, v_col0 // half_w + k))
    const = lambda a: pl.BlockSpec(a.shape, lambda b, c: (0, 0, 0))
    o, s = pl.pallas_call(
        _retention_kernel,
        grid=(B, nc),
        in_specs=[qk_spec, qk_spec, v_spec(0), v_spec(1), const(dm), const(rd), const(kd), const(sd)],
        out_specs=[pl.BlockSpec((C, RET_W), lambda b, c: (b * nc + c, 0)),
                   pl.BlockSpec((None, RET_H // 2, 2 * RET_DK, RET_DV), lambda b, c: (b, 0, 0, 0))],
        out_shape=[jax.ShapeDtypeStruct((B * T, RET_W), f32),
                   jax.ShapeDtypeStruct((B, RET_H // 2, 2 * RET_DK, RET_DV), f32)],
        scratch_shapes=[pltpu.VMEM((RET_H // 2, 2 * RET_DK, RET_DV), f32)],
        compiler_params=pltpu.CompilerParams(
            dimension_semantics=("parallel", "arbitrary"), vmem_limit_bytes=VMEM_LIMIT_BYTES),
        name="retention_prompt",
    )(rq, rk, rv, rv, dm, rd, kd, sd)
    return s.reshape(B, RET_H, RET_DK, RET_DV), o


KV_W = NSA_G * NSA_HD
RET_QK_W = RET_H * RET_DK
OFF_Q = 0
OFF_KC = OFF_Q + NSA_W
OFF_VC = OFF_KC + KV_W
OFF_KS = OFF_VC + KV_W
OFF_VS = OFF_KS + KV_W
OFF_KW = OFF_VS + KV_W
OFF_VW = OFF_KW + KV_W
OFF_RQ = OFF_VW + KV_W
OFF_RK = OFF_RQ + RET_QK_W
OFF_RV = OFF_RK + RET_QK_W
OFF_RG = OFF_RV + RET_W
OFF_GT = OFF_RG + RET_W
CD_PAD = _round_up(OFF_GT + LANE, COL_TILE)
ODD_ROWS = 256
N_ODD_PRE_OUT = 11


def _odd_weight_cols(w):
    gt0 = NSA_W + 6 * KV_W
    body = jnp.concatenate([w[:, :gt0], w[:, gt0 + 3 * NSA_H:]], axis=1)
    gt = w[:, gt0:gt0 + 3 * NSA_H]
    out = jnp.concatenate([body, gt], axis=1)
    return jnp.pad(out, ((0, 0), (0, CD_PAD - out.shape[1]))).astype(BF16)


def _rope_tables(pos, n_rot, theta, head):
    half = n_rot // 2
    inv = jnp.exp(-jnp.log(jnp.float32(theta)) * jnp.arange(half, dtype=jnp.float32) / half)
    ang = pos.astype(jnp.float32)[:, None] * inv[None, :]
    cos, sin = jnp.cos(ang), jnp.sin(ang)
    d = np.arange(LANE) % head
    cos_d, sin_d = cos[:, d % half], sin[:, d % half]
    c = jnp.where(d < n_rot, cos_d, 1.0)
    s1 = jnp.where(d < half, -sin_d, 0.0)
    s2 = jnp.where((d >= half) & (d < n_rot), sin_d, 0.0)
    return jnp.stack([c, s1, s2])


def _rope_lanes(x, tab_ref, half):
    w = x.shape[1]
    rep = w // LANE
    c, s1, s2 = (pltpu.repeat(tab_ref[i], rep, axis=1) for i in range(3))
    return x * c + pltpu.roll(x, w - half, 1) * s1 + pltpu.roll(x, half, 1) * s2


def _rms_heads(x, g_ref, red_ref, exp_ref):
    ms = _head_sum(x * x, red_ref, exp_ref) * (1.0 / NSA_HD)
    return x * lax.rsqrt(ms + 1e-6) * g_ref[...]


def _odd_pre_kernel(x_ref, nsa_tab, ret_tab, qg_ref, ksg_ref, kwg_ref, redq_ref, expq_ref, redk_ref, expk_ref,
                    tile_ref, qn_o, qr_o, ks_o, kw_o, ks4_o, vs4_o, kw4_o, vw4_o, gate_o, rq_o, rk_o, kvt_o, wint_o):
    nsa_half = ROPE_DIMS // 2
    qn = _rms_heads(x_ref[:, OFF_Q:OFF_Q + NSA_W], qg_ref, redq_ref, expq_ref)
    qn_o[...] = qn
    qr_o[...] = _rope_lanes(qn, nsa_tab, nsa_half)
    ks = _rope_lanes(_rms_heads(x_ref[:, OFF_KS:OFF_KS + KV_W], ksg_ref, redk_ref, expk_ref), nsa_tab, nsa_half)
    kw = _rope_lanes(_rms_heads(x_ref[:, OFF_KW:OFF_KW + KV_W], kwg_ref, redk_ref, expk_ref), nsa_tab, nsa_half)
    ks_o[...] = ks
    kw_o[...] = kw
    tile = tile_ref[...]
    for src, dst in ((ks, ks4_o), (x_ref[:, OFF_VS:OFF_VS + KV_W], vs4_o), (kw, kw4_o),
                     (x_ref[:, OFF_VW:OFF_VW + KV_W], vw4_o)):
        dst[...] = jnp.dot(src.astype(BF16), tile, preferred_element_type=F32).astype(BF16)
    gate_o[...] = jax.nn.sigmoid(x_ref[:, OFF_GT:OFF_GT + LANE])
    rq_o[...] = _rope_lanes(x_ref[:, OFF_RQ:OFF_RQ + RET_QK_W], ret_tab, RET_DK // 2)
    rk_o[...] = _rope_lanes(x_ref[:, OFF_RK:OFF_RK + RET_QK_W], ret_tab, RET_DK // 2) * (RET_DK ** -0.5)
    kv_pieces = (x_ref[:, OFF_KC:OFF_KC + KV_W], x_ref[:, OFF_VC:OFF_VC + KV_W], ks, x_ref[:, OFF_VS:OFF_VS + KV_W])
    for dst, pieces in ((kvt_o, kv_pieces), (wint_o, (kw, x_ref[:, OFF_VW:OFF_VW + KV_W]))):
        for s, piece in enumerate(pieces):
            for c in range(KV_W // LANE):
                dst[s * KV_W + c * LANE:s * KV_W + (c + 1) * LANE, :] = piece[:, c * LANE:(c + 1) * LANE].T


def odd_pre(proj, pos, p, row0, n_rows, same_pos, seq_len):
    tr = min(ODD_ROWS, n_rows)
    blk0 = row0 // tr
    n_tab = tr if same_pos else n_rows
    pos_rows = jnp.broadcast_to(pos, (n_tab,)) if same_pos else pos
    nsa_tab = _rope_tables(pos_rows, ROPE_DIMS, ROPE_THETA, NSA_HD)
    ret_tab = _rope_tables(pos_rows, RET_DK, RET_THETA, RET_DK)
    row = lambda v, rep: jnp.tile(v.astype(F32), rep).reshape(1, -1)
    lanes = np.arange(LANE)
    red_q = jnp.asarray((np.arange(NSA_W) // NSA_HD)[:, None] == lanes[None, :], dtype=BF16)
    red_k = jnp.asarray((np.arange(KV_W) // NSA_HD)[:, None] == lanes[None, :], dtype=BF16)
    src = np.arange(KV_W)
    dst = np.arange(NSA_W)
    tile = jnp.asarray((src[:, None] // NSA_HD == dst[None, :] // GROUP_W)
                       & (src[:, None] % NSA_HD == dst[None, :] % NSA_HD), dtype=BF16)
    consts = [row(p['q_norm'], NSA_H), row(p['k_norm'][1], NSA_G), row(p['k_norm'][2], NSA_G),
              red_q, red_q.T, red_k, red_k.T, tile]
    tab_spec = pl.BlockSpec((3, tr, LANE), (lambda i: (0, 0, 0)) if same_pos else (lambda i: (0, i, 0)))
    out = lambda w, dt: (pl.BlockSpec((tr, w), lambda i: (i, 0)), jax.ShapeDtypeStruct((n_rows, w), dt))
    seq_tiles = seq_len // tr
    out_t = lambda r: (pl.BlockSpec((None, r, tr), lambda i: (i // seq_tiles, 0, i % seq_tiles)),
                       jax.ShapeDtypeStruct((n_rows // seq_len, r, seq_len), F32))
    outs = [out(NSA_W, F32), out(NSA_W, F32), out(KV_W, F32), out(KV_W, F32)] + [out(NSA_W, BF16)] * 4 + [
        out(LANE, F32), out(RET_QK_W, F32), out(RET_QK_W, F32), out_t(KV_SLOTS * KV_W), out_t(2 * KV_W)]
    return pl.pallas_call(
        _odd_pre_kernel,
        grid=(n_rows // tr,),
        in_specs=[pl.BlockSpec((tr, CD_PAD), lambda i: (blk0 + i, 0)), tab_spec, tab_spec]
        + [_const_spec(a, 1) for a in consts],
        out_specs=[o[0] for o in outs],
        out_shape=[o[1] for o in outs],
        compiler_params=pltpu.CompilerParams(
            dimension_semantics=("parallel",), vmem_limit_bytes=VMEM_LIMIT_BYTES),
        name="odd_pre",
    )(proj, nsa_tab, ret_tab, *consts)


def _odd_post_kernel(oc_ref, os_ref, ow_ref, gate_ref, ret_ref, rg0_ref, rg1_ref, gng_ref, gnb_ref, ge_ref, *rest,
                     gated):
    o_ref = rest[-1]
    if gated:
        nsa = oc_ref[...]
    else:
        gates = gate_ref[...]
        nsa = jnp.zeros(oc_ref.shape, F32)
        for j, branch in enumerate((oc_ref, os_ref, ow_ref)):
            nsa = nsa + _dot_01(gates, ge_ref[j]) * branch[...]
    o_ref[:, 0:NSA_W] = nsa.astype(o_ref.dtype)
    for h in range(RET_H):
        lanes = slice(h * RET_DV, (h + 1) * RET_DV)
        x = ret_ref[:, lanes]
        mu = jnp.mean(x, axis=-1, keepdims=True)
        d = x - mu
        var = jnp.mean(d * d, axis=-1, keepdims=True)
        yn = d * lax.rsqrt(var + 1e-5) * gng_ref[:, lanes] + gnb_ref[:, lanes]
        rg = (rg0_ref if h < RET_H // 2 else rg1_ref)[:, (h % (RET_H // 2)) * RET_DV:(h % (RET_H // 2) + 1) * RET_DV]
        o_ref[:, NSA_W + h * RET_DV:NSA_W + (h + 1) * RET_DV] = (yn * (rg * jax.nn.sigmoid(rg))).astype(o_ref.dtype)


def odd_post(o_cmp, o_slc, o_win, gates, o_ret, proj, p, n_total, row0, prior=None, gated=False):
    n = o_cmp.shape[0]
    tr = min(POST_ROWS, n)
    blk0 = row0 // tr
    h = np.arange(NSA_W) // NSA_HD
    ge = jnp.asarray(np.stack([(np.arange(LANE)[:, None] == (3 * h + j)[None, :]) for j in range(3)]), dtype=BF16)
    row = lambda a: a.reshape(1, -1).astype(F32)
    consts = [row(p['gn_g']), row(p['gn_b']), ge]
    seq = lambda w: pl.BlockSpec((tr, w), lambda i: (i, 0))
    half = RET_W // 2
    rg_spec = lambda k: pl.BlockSpec((tr, half), lambda i: (blk0 + i, OFF_RG // half + k))
    args = [o_cmp, o_slc, o_win, gates, o_ret, proj, proj] + consts
    in_specs = [seq(NSA_W)] * 3 + [seq(LANE), seq(RET_W), rg_spec(0), rg_spec(1)] + [_const_spec(a, 1) for a in consts]
    aliases = {}
    if prior is not None:
        args.append(prior)
        in_specs.append(pl.BlockSpec(memory_space=pl.ANY))
        aliases = {len(args) - 1: 0}
    return pl.pallas_call(
        functools.partial(_odd_post_kernel, gated=gated),
        grid=(n // tr,),
        in_specs=in_specs,
        out_specs=pl.BlockSpec((tr, D_MODEL), lambda i: (blk0 + i, 0)),
        out_shape=jax.ShapeDtypeStruct((n_total, D_MODEL), BF16),
        input_output_aliases=aliases,
        compiler_params=pltpu.CompilerParams(
            dimension_semantics=("parallel",), vmem_limit_bytes=VMEM_LIMIT_BYTES),
        name="odd_post",
    )(*args)


def wkv7_scan(r, w, k, v, a, b, s0):
    xs = tuple(jnp.moveaxis(z.astype(jnp.float32), 1, 0) for z in (r, w, k, v, a, b))

    def step(S, inp):
        r_t, w_t, k_t, v_t, a_t, b_t = inp
        sa = jnp.einsum('bhij,bhj->bhi', S, a_t)
        S = S * w_t[:, :, None, :] + sa[..., None] * b_t[:, :, None, :] + v_t[..., None] * k_t[:, :, None, :]
        return S, jnp.einsum('bhij,bhj->bhi', S, r_t)

    S, ys = lax.scan(step, s0.astype(jnp.float32), xs)
    return jnp.moveaxis(ys, 0, 1), S


def even_mixer(proj, p, B, T, DB, lru_h0, lru_conv0, shift0, wkv0):
    f32 = F32
    prm = _even_params(p)
    red = prm[-2]
    n_p = B * T
    zeros = lambda *s: jnp.zeros(s, f32)
    a, u, gate, r, lw, k, v, na, nb, g = even_pre_seq(proj, zeros(B, CONV_W - 1, LRU_W), zeros(B, SHIFT_W), prm, B, T)
    seq = lambda z: z.reshape(B, T, LRU_W)
    hs = lru_scan(seq(a), seq(u), zeros(B, LRU_W))
    yw, wkv_p = wkv7_chunked(seq(r), seq(lw), seq(k), seq(v), seq(na), seq(nb), zeros(B, RWKV_H, RWKV_HD, RWKV_HD))
    cat = even_post(hs.reshape(n_p, LRU_W), gate, yw.reshape(n_p, RWKV_W), r, k, v, g, p, red, n_p + DB, 0)
    tail = lambda b, n, c0, c1: proj[(b + 1) * T - n:(b + 1) * T, c0:c1]
    st_p = (hs[:, -1], jnp.stack([tail(b, CONV_W - 1, 0, LRU_W) for b in range(B)]),
            jnp.concatenate([tail(b, 1, 2 * LRU_W, AB_COLS) for b in range(B)], axis=0), wkv_p)
    a, u, gate, r, lw, k, v, na, nb, g = even_pre_step(proj, n_p, lru_conv0, shift0, prm)
    hs_s = a * lru_h0.astype(f32) + u
    heads = (DB, 1, RWKV_H, RWKV_HD)
    yw, wkv_s = wkv7_scan(r.reshape(heads), jnp.exp(lw).reshape(heads), k.reshape(heads), v.reshape(heads),
                          na.reshape(heads), nb.reshape(heads), wkv0)
    cat = even_post(hs_s, gate, yw.reshape(DB, RWKV_W), r, k, v, g, p, red, n_p + DB, n_p, prior=cat)
    xb_s = proj[n_p:]
    conv_s = jnp.concatenate([lru_conv0[:, 1:].astype(f32), xb_s[:, None, :LRU_W]], axis=1)
    st_s = (hs_s, conv_s, xb_s[:, 2 * LRU_W:AB_COLS], wkv_s)
    return cat, st_p, st_s


def _cmp_mlp_kernel(ch_ref, w1_ref, b1_ref, w2_ref, b2_ref, kn_ref, o_ref, *, normalize):
    part = jnp.dot(ch_ref[...].astype(BF16), w1_ref[...], preferred_element_type=F32)
    rows = part.shape[0]
    pre = b1_ref[...] + part[:, :CMP_HID] + pltpu.roll(part[:, CMP_HID:], rows - NSA_G, 0)
    out = jnp.dot(jax.nn.gelu(pre).astype(BF16), w2_ref[...], preferred_element_type=F32) + b2_ref[...]
    o_ref[...] = _rms_rows(out, kn_ref[...]) if normalize else out


def nsa_compress(x, w1, b1, w2, b2, k_norm=None):
    assert CMP_R == 2
    B, L = x.shape[:2]
    n_chunk = L // CMP_STRIDE
    n_cmp = n_chunk - CMP_R + 1
    ch = x[:, :n_chunk * CMP_STRIDE].reshape(B, n_chunk, CMP_STRIDE, NSA_G, NSA_HD)
    ch = jnp.moveaxis(ch, 3, 2).reshape(B * n_chunk * NSA_G, CMP_STRIDE * NSA_HD)
    w1c = jnp.moveaxis(w1, 0, 1).reshape(CMP_STRIDE * NSA_HD, CMP_R * CMP_HID).astype(BF16)
    kn = jnp.ones((NSA_HD,), F32) if k_norm is None else k_norm.astype(F32)
    consts = [w1c, b1.reshape(1, CMP_HID).astype(F32), w2.astype(BF16), b2.reshape(1, NSA_HD).astype(F32),
              kn.reshape(1, NSA_HD)]
    out = pl.pallas_call(
        functools.partial(_cmp_mlp_kernel, normalize=k_norm is not None),
        grid=(1,),
        in_specs=[_const_spec(ch, 1)] + [_const_spec(a, 1) for a in consts],
        out_specs=pl.BlockSpec((ch.shape[0], NSA_HD), lambda i: (0, 0)),
        out_shape=jax.ShapeDtypeStruct((ch.shape[0], NSA_HD), F32),
        compiler_params=pltpu.CompilerParams(dimension_semantics=("arbitrary",), vmem_limit_bytes=VMEM_LIMIT_BYTES),
        name="nsa_compress",
    )(ch, *consts)
    return out.reshape(B, n_chunk, NSA_G, NSA_HD)[:, :n_cmp]


RET_STEP_SEQS = 8


def _retention_step_kernel(q_ref, k_ref, v_ref, s_ref, g_ref, grow_ref, o_ref, s_out_ref):
    head = lax.broadcasted_iota(jnp.int32, (RET_H, RET_QK_W), 0)
    own = lax.broadcasted_iota(jnp.int32, (RET_H, RET_QK_W), 1) // RET_DK == head
    gam = g_ref[...]
    gam_rows = grow_ref[...]
    for s in range(q_ref.shape[0]):
        q8 = jnp.where(own, jnp.broadcast_to(q_ref[s], (RET_H, RET_QK_W)), 0.0).astype(BF16)
        k8 = jnp.where(own, jnp.broadcast_to(k_ref[s], (RET_H, RET_QK_W)), 0.0).astype(BF16)
        v8 = jnp.concatenate([v_ref[s, :, h * RET_DV:(h + 1) * RET_DV] for h in range(RET_H)], axis=0).astype(BF16)
        state = s_ref[s].reshape(RET_H * RET_DK, RET_DV)
        qk = jnp.sum(q8.astype(F32) * k8.astype(F32), axis=-1, keepdims=True)
        o_ref[s] = qk.astype(BF16).astype(F32) * v8.astype(F32) + gam * jnp.dot(
            q8, state.astype(BF16), preferred_element_type=F32)
        upd = lax.dot_general(k8, v8, (((0,), (0,)), ((), ())), preferred_element_type=F32)
        s_out_ref[s] = (state * gam_rows + upd).reshape(RET_H, RET_DK, RET_DV)


def retention_step(rq, rk, rv, s0):
    DB = rq.shape[0]
    ns = RET_STEP_SEQS
    lg = jnp.log1p(-jnp.exp2(-5.0 - jnp.arange(RET_H, dtype=F32)))
    gam = jnp.exp(lg).reshape(RET_H, 1)
    gam_rows = jnp.repeat(gam, RET_DK, axis=0)
    row = lambda w: pl.BlockSpec((ns, 1, w), lambda i: (i, 0, 0))
    st = pl.BlockSpec((ns, RET_H, RET_DK, RET_DV), lambda i: (i, 0, 0, 0))
    return pl.pallas_call(
        _retention_step_kernel,
        grid=(DB // ns,),
        in_specs=[row(RET_QK_W), row(RET_QK_W), row(RET_W), st, _const_spec(gam, 1), _const_spec(gam_rows, 1)],
        out_specs=[pl.BlockSpec((ns, RET_H, RET_DV), lambda i: (i, 0, 0)), st],
        out_shape=[jax.ShapeDtypeStruct((DB, RET_H, RET_DV), F32),
                   jax.ShapeDtypeStruct((DB, RET_H, RET_DK, RET_DV), F32)],
        compiler_params=pltpu.CompilerParams(dimension_semantics=("parallel",), vmem_limit_bytes=VMEM_LIMIT_BYTES),
        name="retention_step",
    )(rq, rk, rv, s0.astype(F32), gam, gam_rows)


def odd_mixer(proj, p, B, T, DB, cache_layer, page_table, win_buf, ret_s0):
    assert DEC_SEQ == 1 and win_buf.shape[1] == WIN_BUF
    n_p = B * T
    kv = (B, T, NSA_G, NSA_HD)
    cols = lambda rows, off, w: proj[rows, off:off + w]
    prompt = slice(0, n_p)
    dec = slice(n_p, n_p + DB)
    qn, qr, ks, kw, ks4, vs4, kw4, vw4, gates, rq, rk, kv_t, win_t_p = odd_pre(
        proj, jnp.tile(jnp.arange(T), B), p, 0, n_p, False, T)
    seq = lambda z: z.reshape(B, T, -1)
    kc_raw = cols(prompt, OFF_KC, KV_W).reshape(kv)
    vc_raw = cols(prompt, OFF_VC, KV_W).reshape(kv)
    kc = nsa_compress(kc_raw, *p['ck'], k_norm=p['k_norm'][0])
    vc = nsa_compress(vc_raw, *p['cv'])
    n_cmp = kc.shape[1]
    n_sel = -(-T // SEL_BLOCK)
    o_cmp, sel = nsa_cmp_select(seq(qn), _tile_cmp(kc), _tile_cmp(vc), _overlap_T(n_cmp, n_sel),
                                n_cmp=n_cmp, n_sel=n_sel, q_pos0=0)
    o_slc = nsa_flash(seq(qr), seq(ks4), seq(vs4), sel, _sel_expand(T))
    o_win = nsa_flash(seq(qr), seq(kw4), seq(vw4))
    ret_p, o_ret = retention_prompt_pallas(rq, rk, proj, B, T, v_col0=OFF_RV)
    flat = lambda z: z.reshape(n_p, -1)
    cat = odd_post(flat(o_cmp), flat(o_slc), flat(o_win), gates, o_ret, proj, p, n_p + DB, 0)
    n_win = min(WINDOW, T)
    kv_rows_p = jnp.transpose(kv_t.reshape(B, KV_SLOTS, NSA_G, NSA_HD, T), (0, 4, 1, 2, 3))
    win_p = jnp.transpose(win_t_p[:, :, T - n_win:].reshape(B, 2, NSA_G, NSA_HD, n_win), (0, 4, 1, 2, 3))
    qn, qr, ks, kw, _, _, _, _, gates, rq, rk, kv_t, _ = odd_pre(proj, jnp.asarray(PAST_LEN), p, n_p, DB, True, DB)
    scale = NSA_HD ** -0.5
    heads = lambda z: z.reshape(DB, NSA_H, NSA_HD)
    vs, vw = cols(dec, OFF_VS, KV_W), cols(dec, OFF_VW, KV_W)
    new_rows = jnp.stack([ks, vs, kw, vw], axis=1)
    cache_t = jnp.transpose(cache_layer, (0, 2, 3, 4, 1)).reshape(cache_layer.shape[0], KV_ROWS, PAGE_SIZE)
    win_t = jnp.transpose(win_buf, (0, 2, 3, 4, 1)).reshape(DB, 2 * SLOT_ROWS, WIN_BUF)
    w1t, b1, w2t, b2t = (jnp.stack([a, b]) for a, b in zip(_dec_cmp_weights(*p['ck']), _dec_cmp_weights(*p['cv'])))
    kn = jnp.tile(p['k_norm'][0], NSA_G).reshape(1, SLOT_ROWS)
    t = np.arange(PAST_LEN)
    expand = jnp.asarray(np.arange(CMP_PAD)[:, None] == (t // SEL_BLOCK)[None, :], dtype=BF16)
    h = np.arange(NSA_H)
    grp = jnp.asarray((h[:, None] // NSA_HPG) == (h[None, :] // NSA_HPG), dtype=BF16)
    o16 = dec_nsa(page_table, cache_t, win_t, _place_heads(heads(qn) * scale), _place_heads(heads(qr) * scale),
                  new_rows, gates[:, :3 * NSA_H].reshape(DB, NSA_H, 3),
                  w1t, b1, w2t, b2t, kn, _overlap_T(DEC_N_CMP, DEC_N_SEL), expand, grp)
    o_nsa = _take_heads(o16)
    o_ret, ret_s = retention_step(rq[:, None, :], rk[:, None, :], cols(dec, OFF_RV, RET_W)[:, None, :], ret_s0)
    cat = odd_post(o_nsa, o_nsa, o_nsa, gates, o_ret.reshape(DB, RET_W), proj, p, n_p + DB, n_p, prior=cat, gated=True)
    rows_s = jnp.transpose(kv_t.reshape(DEC_SEQ, KV_SLOTS, NSA_G, NSA_HD, DB), (4, 0, 1, 2, 3)).astype(cache_layer.dtype)
    new_row = jnp.concatenate([kw, vw], axis=1)[:, None, :].astype(win_buf.dtype)
    win_new = win_shift(win_t, new_row).reshape(DB, 2, NSA_G, NSA_HD, WIN_BUF)
    win_s = jnp.transpose(win_new, (0, 4, 1, 2, 3))
    return cat, (kv_rows_p, win_p, ret_p), (rows_s, win_s, ret_s)


def _stack(xs, dt):
    return jnp.stack(xs).astype(dt)


def kernel(x_prompt, x_sample, state_lru_h, state_lru_conv, state_rwkv_shift, state_rwkv_wkv,
           cache_nsa_kv, cache_nsa_win, state_ret, page_table,
           norm_ffn1, ffn1_w_in, ffn1_w_out, norm_mix, norm_ffn2, ffn2_w_in, ffn2_w_out,
           ab_w_in, lru_conv_w, lru_conv_b, lru_wa, lru_ba, lru_wx, lru_bx, lru_lambda,
           rwkv_mu, rwkv_w0, rwkv_w2, rwkv_a0, rwkv_a2, rwkv_g2, rwkv_k_k, rwkv_k_a, rwkv_r_k,
           rwkv_ln_g, rwkv_ln_b, ab_w_out,
           cd_w_in, nsa_q_norm, nsa_k_norm, cmp_k_w1, cmp_k_b1, cmp_k_w2, cmp_k_b2,
           cmp_v_w1, cmp_v_b1, cmp_v_w2, cmp_v_b2, ret_gn_g, ret_gn_b, cd_w_out):
    dt = x_prompt.dtype
    B = x_prompt.shape[0]
    DB = x_sample.shape[0]
    y = None
    lru_h_p, lru_h_s, lru_c_p, lru_c_s, sh_p, sh_s, wkv_p, wkv_s = [], [], [], [], [], [], [], []
    kv_p, kv_s, win_p, win_s, ret_p, ret_s = [], [], [], [], [], []
    for layer in range(DEPTH):
        li = layer // 2
        w1 = _prep_ffn_weights(ffn1_w_in, ffn1_w_out, layer)
        if layer == 0:
            y = ffn_block(x_prompt.reshape(N_PROMPT, D_MODEL), norm_ffn1[layer], *w1,
                          x_tail=x_sample.reshape(DB * DEC_SEQ, D_MODEL))
        else:
            y = ffn_block(y, norm_ffn1[layer], *w1)
        if layer % 2 == 0:
            p = {'conv_w': lru_conv_w[li], 'conv_b': lru_conv_b[li],
                 'wa': lru_wa[li], 'ba': lru_ba[li], 'wx': lru_wx[li], 'bx': lru_bx[li], 'lam': lru_lambda[li],
                 'mu': rwkv_mu[li], 'w0': rwkv_w0[li], 'w2': rwkv_w2[li], 'a0': rwkv_a0[li], 'a2': rwkv_a2[li],
                 'g2': rwkv_g2[li], 'k_k': rwkv_k_k[li], 'k_a': rwkv_k_a[li], 'r_k': rwkv_r_k[li],
                 'ln_g': rwkv_ln_g[li], 'ln_b': rwkv_ln_b[li]}
            proj = norm_matmul(y, norm_mix[layer], _prep_cols(ab_w_in[li], WIDE_COL_TILE), tn=WIDE_COL_TILE)
            cat, (a0, a1, a2, a3), (b0, b1, b2, b3) = even_mixer(
                proj, p, B, SEQ, DB, state_lru_h[li], state_lru_conv[li], state_rwkv_shift[li], state_rwkv_wkv[li])
            lru_h_p.append(a0); lru_c_p.append(a1); sh_p.append(a2); wkv_p.append(a3)
            lru_h_s.append(b0); lru_c_s.append(b1); sh_s.append(b2); wkv_s.append(b3)
            w_out = ab_w_out[li]
        else:
            p = {'q_norm': nsa_q_norm[li], 'k_norm': nsa_k_norm[li],
                 'ck': (cmp_k_w1[li], cmp_k_b1[li], cmp_k_w2[li], cmp_k_b2[li]),
                 'cv': (cmp_v_w1[li], cmp_v_b1[li], cmp_v_w2[li], cmp_v_b2[li]),
                 'gn_g': ret_gn_g[li], 'gn_b': ret_gn_b[li]}
            proj = norm_matmul(y, norm_mix[layer], _odd_weight_cols(cd_w_in[li]), tn=WIDE_COL_TILE)
            cat, (a0, a1, a2), (b0, b1, b2) = odd_mixer(
                proj, p, B, SEQ, DB, cache_nsa_kv[li], page_table, cache_nsa_win[li], state_ret[li])
            kv_p.append(a0); win_p.append(a1); ret_p.append(a2)
            kv_s.append(b0); win_s.append(b1); ret_s.append(b2)
            w_out = cd_w_out[li]
        y = matmul_residual(cat, w_out.astype(BF16), y, tn=WIDE_COL_TILE)
        w2 = _prep_ffn_weights(ffn2_w_in, ffn2_w_out, layer)
        if layer == DEPTH - 1:
            yp, ys = ffn_block(y, norm_ffn2[layer], *w2, n_tail_out=DB * DEC_SEQ)
        else:
            y = ffn_block(y, norm_ffn2[layer], *w2)
    yp = yp.reshape(B, SEQ, D_MODEL)
    ys = ys.reshape(DB, DEC_SEQ, D_MODEL)
    return (yp, ys,
            _stack(lru_h_p, dt), _stack(lru_h_s, dt), _stack(lru_c_p, dt), _stack(lru_c_s, dt),
            _stack(sh_p, dt), _stack(sh_s, dt), _stack(wkv_p, dt), _stack(wkv_s, dt),
            _stack(kv_p, dt), _stack(kv_s, dt), _stack(win_p, dt), _stack(win_s, dt),
            _stack(ret_p, dt), _stack(ret_s, dt))
```

```python
import functools

import jax
import jax.numpy as jnp
import numpy as np
from jax import lax
from jax.experimental import pallas as pl
from jax.experimental.pallas import tpu as pltpu

D_MODEL = 2048
BATCH = 4
SEQ = 2048
DEPTH = 2
DEC_BATCH = 128
DEC_SEQ = 1
PAST_LEN = 2048
PAGE_SIZE = 128
D_FF = 5504
LRU_W = D_MODEL // 2
LRU_BLOCKS = 16
LRU_BS = LRU_W // LRU_BLOCKS
CONV_W = 4
LRU_C = 8.0
RWKV_W = D_MODEL // 2
RWKV_HD = 64
RWKV_H = RWKV_W // RWKV_HD
W_LORA = 64
A_LORA = 64
G_LORA = 160
SHIFT_W = 3 * RWKV_W + W_LORA + A_LORA + G_LORA
AB_COLS = 2 * LRU_W + SHIFT_W
NSA_H = 16
NSA_G = 4
NSA_HPG = NSA_H // NSA_G
NSA_HD = 64
NSA_W = NSA_H * NSA_HD
ROPE_DIMS = NSA_HD // 4
ROPE_THETA = 500000.0
CMP_BLOCK = 32
CMP_STRIDE = 16
CMP_R = CMP_BLOCK // CMP_STRIDE
CMP_HID = 256
SEL_BLOCK = 64
SEL_TOP = 16
SEL_Q_BLOCK = 64
WINDOW = 512
WIN_BLOCK = 128
FORCE_SCORE = 1e4
KV_SLOTS = 4
RET_H = 8
RET_DK = 64
RET_DV = 128
RET_W = RET_H * RET_DV
RET_CHUNK = 128
RET_THETA = 10000.0
CD_COLS = NSA_W + 6 * NSA_G * NSA_HD + 3 * NSA_H + 2 * RET_H * RET_DK + 2 * RET_W

N_TOK = BATCH * SEQ + DEC_BATCH * DEC_SEQ
N_PROMPT = BATCH * SEQ

LANE = 128
VMEM_LIMIT_BYTES = 56 * 1024 * 1024
ROW_TILE = 640
FF_TILE = 512
D_FF_PAD = 5632
COL_TILE = 512
WIDE_COL_TILE = 2048

BF16 = jnp.bfloat16
F32 = jnp.float32


def _round_up(n, m):
    return -(-n // m) * m


def _rms_rows(x, g):
    ms = jnp.mean(x * x, axis=-1, keepdims=True)
    return x * lax.rsqrt(ms + 1e-6) * g


def _ffn_kernel(*refs, tail_in, tail_out):
    refs = list(refs)
    x_ref = refs.pop(0)
    xt_ref = refs.pop(0) if tail_in else None
    g_ref, wg_ref, wu_ref, wo_ref, o_ref = refs[:5]
    ot_ref = refs[5] if tail_out else None
    xn_ref, acc_ref = refs[-2:]
    i = pl.program_id(0)
    k = pl.program_id(1)
    last_tile = i == pl.num_programs(0) - 1

    def start(x):
        xn_ref[...] = _rms_rows(x, g_ref[...]).astype(BF16)
        acc_ref[...] = 2.0 * x

    if tail_in:
        n_tail = xt_ref.shape[0]

        @pl.when((k == 0) & last_tile)
        def _():
            start(jnp.concatenate([x_ref[0:x_ref.shape[0] - n_tail, :], xt_ref[...]], axis=0))

        @pl.when((k == 0) & jnp.logical_not(last_tile))
        def _():
            start(x_ref[...])
    else:
        @pl.when(k == 0)
        def _():
            start(x_ref[...])

    xn = xn_ref[...]
    gate = jnp.dot(xn, wg_ref[...], preferred_element_type=F32)
    up = jnp.dot(xn, wu_ref[...], preferred_element_type=F32)
    act = gate * jax.nn.sigmoid(gate) * up
    acc_ref[...] += jnp.dot(act.astype(BF16), wo_ref[...], preferred_element_type=F32)

    @pl.when(k == pl.num_programs(1) - 1)
    def _():
        o_ref[...] = 0.5 * acc_ref[...]

    if tail_out:
        n_tail = ot_ref.shape[0]

        @pl.when((k == pl.num_programs(1) - 1) & last_tile)
        def _():
            ot_ref[...] = 0.5 * acc_ref[acc_ref.shape[0] - n_tail:, :]


def ffn_block(x, g, wg, wu, wo, x_tail=None, n_tail_out=0):
    d = x.shape[1]
    m = x.shape[0] + (0 if x_tail is None else x_tail.shape[0])
    assert m % ROW_TILE == 0
    row_spec = pl.BlockSpec((ROW_TILE, d), lambda i, k: (i, 0))
    in_specs = [row_spec]
    args = [x]
    if x_tail is not None:
        in_specs.append(pl.BlockSpec(x_tail.shape, lambda i, k: (0, 0)))
        args.append(x_tail)
    in_specs += [
        pl.BlockSpec((1, d), lambda i, k: (0, 0)),
        pl.BlockSpec((d, FF_TILE), lambda i, k: (0, k)),
        pl.BlockSpec((d, FF_TILE), lambda i, k: (0, k)),
        pl.BlockSpec((FF_TILE, d), lambda i, k: (k, 0)),
    ]
    args += [g.reshape(1, d), wg, wu, wo]
    if n_tail_out:
        out_specs = [row_spec, pl.BlockSpec((n_tail_out, d), lambda i, k: (0, 0))]
        out_shape = [jax.ShapeDtypeStruct((m - n_tail_out, d), F32), jax.ShapeDtypeStruct((n_tail_out, d), F32)]
    else:
        out_specs = row_spec
        out_shape = jax.ShapeDtypeStruct((m, d), F32)
    return pl.pallas_call(
        functools.partial(_ffn_kernel, tail_in=x_tail is not None, tail_out=bool(n_tail_out)),
        grid=(m // ROW_TILE, D_FF_PAD // FF_TILE),
        in_specs=in_specs,
        out_specs=out_specs,
        out_shape=out_shape,
        scratch_shapes=[pltpu.VMEM((ROW_TILE, d), BF16), pltpu.VMEM((ROW_TILE, d), F32)],
        compiler_params=pltpu.CompilerParams(
            dimension_semantics=("arbitrary", "arbitrary"), vmem_limit_bytes=VMEM_LIMIT_BYTES),
        name="ffn_block",
    )(*args)


def _norm_matmul_kernel(x_ref, g_ref, w_ref, o_ref, xn_ref):
    @pl.when(pl.program_id(1) == 0)
    def _():
        xn_ref[...] = _rms_rows(x_ref[...], g_ref[...]).astype(BF16)

    o_ref[...] = jnp.dot(xn_ref[...], w_ref[...], preferred_element_type=F32)


def norm_matmul(x, g, w, tn=COL_TILE):
    m, k = x.shape
    n = w.shape[1]
    return pl.pallas_call(
        _norm_matmul_kernel,
        grid=(m // ROW_TILE, n // tn),
        in_specs=[
            pl.BlockSpec((ROW_TILE, k), lambda i, j: (i, 0)),
            pl.BlockSpec((1, k), lambda i, j: (0, 0)),
            pl.BlockSpec((k, tn), lambda i, j: (0, j)),
        ],
        out_specs=pl.BlockSpec((ROW_TILE, tn), lambda i, j: (i, j)),
        out_shape=jax.ShapeDtypeStruct((m, n), F32),
        scratch_shapes=[pltpu.VMEM((ROW_TILE, k), BF16)],
        compiler_params=pltpu.CompilerParams(
            dimension_semantics=("parallel", "arbitrary"), vmem_limit_bytes=VMEM_LIMIT_BYTES),
        name="norm_matmul",
    )(x, g.reshape(1, k), w)


def _matmul_residual_kernel(a_ref, w_ref, r_ref, o_ref):
    o_ref[...] = r_ref[...] + jnp.dot(a_ref[...].astype(BF16), w_ref[...], preferred_element_type=F32)


def matmul_residual(a, w, res, tn=COL_TILE):
    m, k = a.shape
    n = w.shape[1]
    return pl.pallas_call(
        _matmul_residual_kernel,
        grid=(m // ROW_TILE, n // tn),
        in_specs=[
            pl.BlockSpec((ROW_TILE, k), lambda i, j: (i, 0)),
            pl.BlockSpec((k, tn), lambda i, j: (0, j)),
            pl.BlockSpec((ROW_TILE, tn), lambda i, j: (i, j)),
        ],
        out_specs=pl.BlockSpec((ROW_TILE, tn), lambda i, j: (i, j)),
        out_shape=jax.ShapeDtypeStruct((m, n), F32),
        compiler_params=pltpu.CompilerParams(
            dimension_semantics=("parallel", "arbitrary"), vmem_limit_bytes=VMEM_LIMIT_BYTES),
        name="matmul_residual",
    )(a, w, res)


WCAST_ROWS = 256
WCAST_COLS = 512


def _cast_w_in_kernel(w_ref, wg_ref, wu_ref):
    pad = jnp.zeros((w_ref.shape[0], D_FF_PAD - D_FF), BF16)
    wg_ref[:, :D_FF] = w_ref[:, :D_FF].astype(BF16)
    wg_ref[:, D_FF:] = pad
    wu_ref[:, :D_FF] = w_ref[:, D_FF:].astype(BF16)
    wu_ref[:, D_FF:] = pad


def _cast_w_out_kernel(w_ref, wo_ref):
    wo_ref[:D_FF, :] = w_ref[...].astype(BF16)
    wo_ref[D_FF:, :] = jnp.zeros((D_FF_PAD - D_FF, w_ref.shape[1]), BF16)


def _prep_ffn_weights(w_in, w_out, layer):
    d = w_in.shape[1]
    wg, wu = pl.pallas_call(
        _cast_w_in_kernel,
        grid=(d // WCAST_ROWS,),
        in_specs=[pl.BlockSpec((None, WCAST_ROWS, 2 * D_FF), lambda i: (layer, i, 0))],
        out_specs=[pl.BlockSpec((WCAST_ROWS, D_FF_PAD), lambda i: (i, 0))] * 2,
        out_shape=[jax.ShapeDtypeStruct((d, D_FF_PAD), BF16)] * 2,
        compiler_params=pltpu.CompilerParams(dimension_semantics=("parallel",), vmem_limit_bytes=VMEM_LIMIT_BYTES),
        name="cast_w_in",
    )(w_in)
    wo = pl.pallas_call(
        _cast_w_out_kernel,
        grid=(d // WCAST_COLS,),
        in_specs=[pl.BlockSpec((None, D_FF, WCAST_COLS), lambda j: (layer, 0, j))],
        out_specs=pl.BlockSpec((D_FF_PAD, WCAST_COLS), lambda j: (0, j)),
        out_shape=jax.ShapeDtypeStruct((D_FF_PAD, d), BF16),
        compiler_params=pltpu.CompilerParams(dimension_semantics=("parallel",), vmem_limit_bytes=VMEM_LIMIT_BYTES),
        name="cast_w_out",
    )(w_out)
    return wg, wu, wo


def _prep_cols(w, tile):
    n = w.shape[1]
    return jnp.pad(w, ((0, 0), (0, _round_up(n, tile) - n))).astype(BF16)


SCAN_TILE = 256


def _lru_scan_kernel(a_ref, b_ref, h0_ref, o_ref, carry_ref):
    @pl.when(pl.program_id(1) == 0)
    def _():
        carry_ref[...] = h0_ref[...]

    a = a_ref[...]
    b = b_ref[...]
    rows = lax.broadcasted_iota(jnp.int32, a.shape, 0)
    k = 1
    while k < a.shape[0]:
        keep = rows >= k
        b = jnp.where(keep, a * pltpu.roll(b, k, 0) + b, b)
        a = jnp.where(keep, a * pltpu.roll(a, k, 0), a)
        k *= 2
    h = a * carry_ref[...] + b
    o_ref[...] = h
    carry_ref[...] = h[a.shape[0] - 1:, :]


def lru_scan(a, b, h0):
    B, T, W = a.shape
    tt = min(SCAN_TILE, T)
    return pl.pallas_call(
        _lru_scan_kernel,
        grid=(B, T // tt),
        in_specs=[
            pl.BlockSpec((None, tt, W), lambda i, t: (i, t, 0)),
            pl.BlockSpec((None, tt, W), lambda i, t: (i, t, 0)),
            pl.BlockSpec((None, 1, W), lambda i, t: (i, 0, 0)),
        ],
        out_specs=pl.BlockSpec((None, tt, W), lambda i, t: (i, t, 0)),
        out_shape=jax.ShapeDtypeStruct((B, T, W), F32),
        scratch_shapes=[pltpu.VMEM((1, W), F32)],
        compiler_params=pltpu.CompilerParams(
            dimension_semantics=("parallel", "arbitrary"), vmem_limit_bytes=VMEM_LIMIT_BYTES),
        name="lru_scan",
    )(a, b, h0.reshape(B, 1, W))


GROUP_W = NSA_HPG * NSA_HD
ATT_Q_TILE = 256
ATT_Q_TILE_WIN = 128
ATT_K_TILE = 256
ATT_K_TILE_WIN = 256
CMP_PAD = 128
NEG_BIG = -1e30


def _stack_heads(q):
    head = lax.broadcasted_iota(jnp.int32, q.shape, 1) // NSA_HD
    return jnp.concatenate([jnp.where(head == h, q, 0.0) for h in range(NSA_HPG)], axis=0)


def _unstack_heads(o, tq):
    head = lax.broadcasted_iota(jnp.int32, (tq, GROUP_W), 1) // NSA_HD
    out = jnp.zeros((tq, GROUP_W), F32)
    for h in range(NSA_HPG):
        out = out + jnp.where(head == h, o[h * tq:(h + 1) * tq], 0.0)
    return out


def _cmp_select_kernel(q_ref, k_ref, v_ref, ov_ref, o_ref, sel_ref, *, n_cmp, n_sel, q_pos0):
    tq = q_ref.shape[0]
    i = pl.program_id(2)
    qs = _stack_heads(q_ref[...] * (NSA_HD ** -0.5)).astype(BF16)
    s = lax.dot_general(qs, k_ref[...], (((1,), (1,)), ((), ())), preferred_element_type=F32)
    q_pos = q_pos0 + i * tq + lax.broadcasted_iota(jnp.int32, (tq, CMP_PAD), 0)
    c = lax.broadcasted_iota(jnp.int32, (tq, CMP_PAD), 1)
    mask1 = (c < n_cmp) & (c * CMP_STRIDE + (CMP_BLOCK - 1) <= q_pos)
    mask = jnp.concatenate([mask1] * NSA_HPG, axis=0)
    s = jnp.where(mask, s, NEG_BIG)
    m = jnp.max(s, axis=-1, keepdims=True)
    e = jnp.where(mask, jnp.exp(s - m), 0.0)
    den = jnp.sum(e, axis=-1, keepdims=True)
    prob = e / jnp.where(den > 0, den, 1.0)
    o = jnp.dot(prob.astype(BF16), v_ref[...], preferred_element_type=F32)
    o_ref[...] = _unstack_heads(o, tq)
    psum = prob[0:tq]
    for h in range(1, NSA_HPG):
        psum = psum + prob[h * tq:(h + 1) * tq]
    imp = jnp.dot(psum.astype(BF16), ov_ref[...], preferred_element_type=F32)
    qb = q_pos // SEL_BLOCK
    valid = (c <= qb) & (c < n_sel)
    forced = (c == 0) | (c == qb) | (c == qb - 1)
    score = jnp.where(valid, jnp.where(forced, FORCE_SCORE, imp), -jnp.inf)
    k_top = min(SEL_TOP, n_sel)
    few_blocks = (q_pos0 + (i + 1) * tq - 1) // SEL_BLOCK < k_top

    @pl.when(few_blocks)
    def _():
        sel_ref[...] = jnp.where(valid, 1.0, 0.0)

    @pl.when(jnp.logical_not(few_blocks))
    def _():
        rank = jnp.zeros((tq, CMP_PAD), F32)
        for jp in range(n_sel):
            col = score[:, jp:jp + 1]
            beats = (col > score) | ((col == score) & (c > jp))
            rank = rank + jnp.where(beats, 1.0, 0.0)
        sel_ref[...] = jnp.where((rank < k_top) & (c < n_sel), 1.0, 0.0)


def nsa_cmp_select(qn, kc4, vc4, ovT, *, n_cmp, n_sel, q_pos0):
    B, T, _ = qn.shape
    tq = min(ATT_Q_TILE, T)
    return pl.pallas_call(
        functools.partial(_cmp_select_kernel, n_cmp=n_cmp, n_sel=n_sel, q_pos0=q_pos0),
        grid=(B, NSA_G, T // tq),
        in_specs=[
            pl.BlockSpec((None, tq, GROUP_W), lambda b, g, i: (b, i, g)),
            pl.BlockSpec((None, None, CMP_PAD, GROUP_W), lambda b, g, i: (b, g, 0, 0)),
            pl.BlockSpec((None, None, CMP_PAD, GROUP_W), lambda b, g, i: (b, g, 0, 0)),
            pl.BlockSpec((CMP_PAD, CMP_PAD), lambda b, g, i: (0, 0)),
        ],
        out_specs=[
            pl.BlockSpec((None, tq, GROUP_W), lambda b, g, i: (b, i, g)),
            pl.BlockSpec((None, None, tq, CMP_PAD), lambda b, g, i: (b, g, i, 0)),
        ],
        out_shape=[jax.ShapeDtypeStruct((B, T, NSA_W), F32),
                   jax.ShapeDtypeStruct((B, NSA_G, T, CMP_PAD), F32)],
        compiler_params=pltpu.CompilerParams(
            dimension_semantics=("parallel", "parallel", "parallel"), vmem_limit_bytes=VMEM_LIMIT_BYTES),
        name="nsa_cmp_select",
    )(qn, kc4, vc4, ovT)


def _flash_kernel(*refs, selected):
    if selected:
        q_ref, k_ref, v_ref, sel_ref, exp_ref, o_ref, m_ref, l_ref, acc_ref, s_a, s_b = refs
    else:
        q_ref, k_ref, v_ref, o_ref, m_ref, l_ref, acc_ref, s_a, s_b = refs
    tq = q_ref.shape[0]
    tk = s_a.shape[1]
    n_tiles = k_ref.shape[0] // tk
    i = pl.program_id(2)
    q = q_ref[...] * (NSA_HD ** -0.5)
    head = lax.broadcasted_iota(jnp.int32, q.shape, 1) // NSA_HD
    q4 = _stack_heads(q).astype(BF16)
    m_ref[...] = jnp.full(m_ref.shape, NEG_BIG, F32)
    l_ref[...] = jnp.zeros(l_ref.shape, F32)
    acc_ref[...] = jnp.zeros(acc_ref.shape, F32)
    q_pos = i * tq + lax.broadcasted_iota(jnp.int32, (tq, tk), 0)
    col = lax.broadcasted_iota(jnp.int32, (tq, tk), 1)
    if selected:
        sel = sel_ref[...].astype(BF16)
        lo = 0
    else:
        lo = jnp.maximum(i * tq - (WINDOW - 1), 0) // tk
    hi = (i * tq + tq - 1) // tk + 1

    def tile_start(j):
        return pl.multiple_of(jnp.minimum(j, n_tiles - 1) * tk, tk)

    def scores(j, s_ref):
        s_ref[...] = lax.dot_general(q4, k_ref[pl.ds(tile_start(j), tk), :], (((1,), (1,)), ((), ())),
                                     preferred_element_type=F32)

    def consume(j, s_ref):
        v = v_ref[pl.ds(tile_start(j), tk), :]
        k_pos = j * tk + col
        mask = k_pos <= q_pos
        if selected:
            mask = mask & (jnp.dot(sel, exp_ref[jnp.minimum(j, n_tiles - 1)], preferred_element_type=F32) > 0.5)
        else:
            mask = mask & (q_pos - k_pos < WINDOW)
        if not selected:
            bias = jnp.where(mask, 0.0, 2.0 * NEG_BIG)
        for h in range(NSA_HPG):
            m_old = m_ref[h]
            if selected:
                s = jnp.where(mask, s_ref[h * tq:(h + 1) * tq, :], NEG_BIG)
                m_new = jnp.maximum(m_old, jnp.max(s, axis=-1, keepdims=True))
                p = jnp.where(mask, jnp.exp(s - pltpu.repeat(m_new, tk // LANE, axis=1)), 0.0)
            else:
                s = s_ref[h * tq:(h + 1) * tq, :] + bias
                m_new = jnp.maximum(m_old, jnp.max(s, axis=-1, keepdims=True))
                p = jnp.exp(s - pltpu.repeat(m_new, tk // LANE, axis=1))
            alpha = jnp.exp(m_old - m_new)
            l_ref[h] = alpha * l_ref[h] + jnp.sum(p, axis=-1, keepdims=True)
            acc_ref[h] = (pltpu.repeat(alpha, GROUP_W // LANE, axis=1) * acc_ref[h]
                          + jnp.dot(p.astype(BF16), v, preferred_element_type=F32))
            m_ref[h] = m_new

    scores(lo, s_a)

    def body(t, carry):
        j = lo + 2 * t
        scores(j + 1, s_b)
        consume(j, s_a)
        scores(j + 2, s_a)
        consume(j + 1, s_b)
        return carry

    lax.fori_loop(0, (hi - lo + 1) // 2, body, 0)
    out = jnp.zeros((tq, GROUP_W), F32)
    for h in range(NSA_HPG):
        den = pltpu.repeat(l_ref[h], GROUP_W // LANE, axis=1)
        out = out + jnp.where(head == h, acc_ref[h] / jnp.where(den > 0, den, 1.0), 0.0)
    o_ref[...] = out


def nsa_flash(qr, k4, v4, sel=None, expand=None):
    B, T, _ = qr.shape
    selected = sel is not None
    tq = ATT_Q_TILE if selected else ATT_Q_TILE_WIN
    tk = ATT_K_TILE if selected else ATT_K_TILE_WIN
    in_specs = [
        pl.BlockSpec((None, tq, GROUP_W), lambda b, g, i: (b, i, g)),
        pl.BlockSpec((None, T, GROUP_W), lambda b, g, i: (b, 0, g)),
        pl.BlockSpec((None, T, GROUP_W), lambda b, g, i: (b, 0, g)),
    ]
    args = [qr, k4, v4]
    if selected:
        in_specs += [
            pl.BlockSpec((None, None, tq, CMP_PAD), lambda b, g, i: (b, g, i, 0)),
            pl.BlockSpec(expand.shape, lambda b, g, i: (0, 0, 0)),
        ]
        args += [sel, expand]
    return pl.pallas_call(
        functools.partial(_flash_kernel, selected=selected),
        grid=(B, NSA_G, T // tq),
        in_specs=in_specs,
        out_specs=pl.BlockSpec((None, tq, GROUP_W), lambda b, g, i: (b, i, g)),
        out_shape=jax.ShapeDtypeStruct((B, T, NSA_W), F32),
        scratch_shapes=[pltpu.VMEM((NSA_HPG, tq, LANE), F32), pltpu.VMEM((NSA_HPG, tq, LANE), F32),
                        pltpu.VMEM((NSA_HPG, tq, GROUP_W), F32),
                        pltpu.VMEM((NSA_HPG * tq, tk), F32), pltpu.VMEM((NSA_HPG * tq, tk), F32)],
        compiler_params=pltpu.CompilerParams(
            dimension_semantics=("parallel", "parallel", "parallel"), vmem_limit_bytes=VMEM_LIMIT_BYTES),
        name="nsa_flash_sel" if selected else "nsa_flash_win",
    )(*args)


def _tile_cmp(x):
    B, n = x.shape[:2]
    x = jnp.pad(jnp.moveaxis(x, 1, 2), ((0, 0), (0, 0), (0, CMP_PAD - n), (0, 0)))
    return jnp.tile(x, (1, 1, 1, NSA_HPG)).astype(BF16)


def _overlap_T(n_cmp, n_sel):
    ov = np.zeros((CMP_PAD, CMP_PAD), np.float32)
    cs = np.arange(n_cmp) * CMP_STRIDE
    ss = np.arange(n_sel) * SEL_BLOCK
    o = np.minimum(cs[None] + CMP_BLOCK, ss[:, None] + SEL_BLOCK) - np.maximum(cs[None], ss[:, None])
    ov[:n_cmp, :n_sel] = (np.clip(o, 0, None) / CMP_BLOCK).T
    return jnp.asarray(ov, dtype=BF16)


def _sel_expand(T):
    t = np.arange(T)
    e = (np.arange(CMP_PAD)[:, None] == (t // SEL_BLOCK)[None, :]).astype(np.float32)
    return jnp.asarray(e.reshape(CMP_PAD, T // ATT_K_TILE, ATT_K_TILE).transpose(1, 0, 2), dtype=BF16)


KV_ROWS = KV_SLOTS * NSA_G * NSA_HD
SLOT_ROWS = NSA_G * NSA_HD
N_PAGES = PAST_LEN // PAGE_SIZE
DEC_N_CHUNK = (PAST_LEN + DEC_SEQ) // CMP_STRIDE
DEC_N_CMP = DEC_N_CHUNK - CMP_R + 1
DEC_N_SEL = -(-(PAST_LEN + DEC_SEQ) // SEL_BLOCK)
WIN_BUF = min(WINDOW, PAST_LEN)


def _softmax_rows(s, mask, s_new=None):
    s = jnp.where(mask, s, NEG_BIG)
    m = jnp.max(s, axis=-1, keepdims=True)
    if s_new is not None:
        m = jnp.maximum(m, s_new)
    e = jnp.where(mask, jnp.exp(s - m), 0.0)
    den = jnp.sum(e, axis=-1, keepdims=True)
    if s_new is None:
        return e, den
    e_new = jnp.exp(s_new - m)
    return e, e_new, den + e_new


def _dec_nsa_kernel(pt_ref, *refs):
    pages = refs[:N_PAGES]
    (win_ref, qn_ref, qr_ref, new_ref, gate_ref, w1_ref, b1_ref, w2_ref, b2_ref, kn_ref,
     ov_ref, exp_ref, grp_ref, perm_ref, o_ref, xt_ref, acc_ref) = refs[N_PAGES:]
    del pt_ref
    f32 = F32
    half = 2 * NSA_HD
    n_chunk = DEC_N_CHUNK

    perm = perm_ref[...]
    per_page = PAGE_SIZE // CMP_STRIDE
    for p in range(N_PAGES):
        for sg in range(4):
            tile = pages[p][sg * half:(sg + 1) * half, :].astype(BF16)
            xt = lax.dot_general(perm, tile, (((1,), (1,)), ((), ())), preferred_element_type=f32)
            for r in range(CMP_STRIDE):
                xt_ref[sg, r, p * per_page:(p + 1) * per_page, :] = xt[r * per_page:(r + 1) * per_page, :]

    lane_lo = lax.broadcasted_iota(jnp.int32, (n_chunk, half), 1) < NSA_HD
    lane_grp = lax.broadcasted_iota(jnp.int32, (n_chunk, SLOT_ROWS), 1) // NSA_HD
    cmp_rows = []
    for slot in range(2):
        for gp in range(2):
            los, his = [], []
            for rp in range(CMP_STRIDE // 2):
                x0, x1 = (xt_ref[slot * 2 + gp, 2 * rp + j] for j in range(2))
                los.append(jnp.where(lane_lo, x0, pltpu.roll(x1, NSA_HD, 1)))
                his.append(jnp.where(lane_lo, pltpu.roll(x0, NSA_HD, 1), x1))
            lhs = jnp.concatenate([jnp.concatenate(los, axis=1), jnp.concatenate(his, axis=1)], axis=0).astype(BF16)
            acc_ref[pl.ds(gp * 2 * n_chunk, 2 * n_chunk), :] = jnp.dot(lhs, w1_ref[slot], preferred_element_type=f32)
        acc = acc_ref[...]
        pre = b1_ref[slot] + acc[:, :CMP_HID] + pltpu.roll(acc[:, CMP_HID:], NSA_G * n_chunk - 1, 0)
        out = jnp.dot(jax.nn.gelu(pre).astype(BF16), w2_ref[slot], preferred_element_type=f32) + b2_ref[slot]
        if slot == 0:
            out = _rms_rows(out, kn_ref[...])
        sel_rows = jnp.zeros((n_chunk, SLOT_ROWS), f32)
        for g in range(NSA_G):
            sel_rows = sel_rows + jnp.where(lane_grp == g, out[g * n_chunk:(g + 1) * n_chunk], 0.0)
        cmp_rows.append(sel_rows.astype(BF16))
    kc, vc = cmp_rows

    qn = qn_ref[...].astype(BF16)
    qr = qr_ref[...].astype(BF16)
    nt = (((1,), (1,)), ((), ()))
    c = lax.broadcasted_iota(jnp.int32, (NSA_H, CMP_PAD), 1)
    s = lax.dot_general(qn, kc, nt, preferred_element_type=f32)
    e, den = _softmax_rows(s, c < DEC_N_CMP)
    prob = e / jnp.where(den > 0, den, 1.0)
    o_cmp = jnp.dot(prob.astype(BF16), vc, preferred_element_type=f32)
    p_hi, p_mid = _split_bf16(prob)
    p_lo = (prob - p_hi.astype(f32) - p_mid.astype(f32)).astype(BF16)
    grp = grp_ref[...]
    psum = (jnp.dot(grp, p_hi, preferred_element_type=f32) + jnp.dot(grp, p_mid, preferred_element_type=f32)
            + jnp.dot(grp, p_lo, preferred_element_type=f32))
    imp = jnp.dot(psum.astype(BF16), ov_ref[...], preferred_element_type=f32)
    qb = (PAST_LEN + DEC_SEQ - 1) // SEL_BLOCK
    valid = c <= qb
    forced = (c == 0) | (c == qb) | (c == qb - 1)
    score = jnp.where(valid, jnp.where(forced, FORCE_SCORE, imp), -jnp.inf)
    rank = jnp.zeros((NSA_H, CMP_PAD), f32)
    for jp in range(DEC_N_SEL):
        col = score[:, jp:jp + 1]
        rank = rank + jnp.where((col > score) | ((col == score) & (c > jp)), 1.0, 0.0)
    sel = jnp.where((rank < min(SEL_TOP, DEC_N_SEL)) & (c < DEC_N_SEL), 1.0, 0.0).astype(BF16)

    new = new_ref[...]
    new_b = new.astype(BF16).astype(f32)
    qr_f = qr.astype(f32)
    s_pages = [jnp.dot(qr, pages[p][2 * SLOT_ROWS:3 * SLOT_ROWS, :].astype(BF16), preferred_element_type=f32)
               for p in range(N_PAGES)]
    s = jnp.concatenate(s_pages, axis=1)
    mask = jnp.dot(sel, exp_ref[...], preferred_element_type=f32) > 0.5
    s_new = jnp.sum(qr_f * new_b[0:1], axis=-1, keepdims=True)
    e, e_new, den = _softmax_rows(s, mask, s_new)
    e = e.astype(BF16)
    o_slc = e_new.astype(BF16).astype(f32) * new_b[1:2]
    for p in range(N_PAGES):
        o_slc = o_slc + lax.dot_general(e[:, p * PAGE_SIZE:(p + 1) * PAGE_SIZE],
                                        pages[p][3 * SLOT_ROWS:4 * SLOT_ROWS, :].astype(BF16), nt,
                                        preferred_element_type=f32)
    o_slc = o_slc / den

    s = jnp.dot(qr, win_ref[0:SLOT_ROWS, :].astype(BF16), preferred_element_type=f32)
    i_buf = lax.broadcasted_iota(jnp.int32, (NSA_H, WIN_BUF), 1)
    s_new = jnp.sum(qr_f * new_b[2:3], axis=-1, keepdims=True)
    e, e_new, den = _softmax_rows(s, WIN_BUF - i_buf < WINDOW, s_new)
    o_win = e_new.astype(BF16).astype(f32) * new_b[3:4] + lax.dot_general(
        e.astype(BF16), win_ref[SLOT_ROWS:2 * SLOT_ROWS, :].astype(BF16), nt, preferred_element_type=f32)
    o_win = o_win / den

    gates = gate_ref[...]
    o_ref[...] = gates[:, 0:1] * o_cmp + gates[:, 1:2] * o_slc + gates[:, 2:3] * o_win


def dec_nsa(page_table, cache_t, win_t, qn16, qr16, new_rows, gates, w1t, b1, w2t, b2t, kn, ovT, expand, grp):
    DB = qn16.shape[0]
    per_page = PAGE_SIZE // CMP_STRIDE
    tok = np.arange(PAGE_SIZE)
    perm = jnp.asarray((tok[:, None] // per_page == tok[None, :] % CMP_STRIDE)
                       & (tok[:, None] % per_page == tok[None, :] // CMP_STRIDE), dtype=BF16)
    const = lambda shape: pl.BlockSpec(shape, lambda b, pt: (0,) * len(shape))
    per_b = lambda shape: pl.BlockSpec((None,) + shape, lambda b, pt: (b,) + (0,) * len(shape))
    page_specs = [pl.BlockSpec((None, KV_ROWS, PAGE_SIZE), functools.partial(lambda b, pt, p: (pt[b, p], 0, 0), p=p))
                  for p in range(N_PAGES)]
    in_specs = page_specs + [
        per_b((2 * SLOT_ROWS, WIN_BUF)), per_b((NSA_H, SLOT_ROWS)), per_b((NSA_H, SLOT_ROWS)),
        per_b((4, SLOT_ROWS)), per_b((NSA_H, 3)),
        const(w1t.shape), const(b1.shape), const(w2t.shape), const(b2t.shape), const(kn.shape),
        const(ovT.shape), const(expand.shape), const(grp.shape), const(perm.shape),
    ]
    grid_spec = pltpu.PrefetchScalarGridSpec(
        num_scalar_prefetch=1, grid=(DB,), in_specs=in_specs,
        out_specs=pl.BlockSpec((None, NSA_H, SLOT_ROWS), lambda b, pt: (b, 0, 0)),
        scratch_shapes=[pltpu.VMEM((4, CMP_STRIDE, DEC_N_CHUNK, 2 * NSA_HD), F32),
                        pltpu.VMEM((NSA_G * DEC_N_CHUNK, CMP_R * CMP_HID), F32)])
    return pl.pallas_call(
        _dec_nsa_kernel,
        grid_spec=grid_spec,
        out_shape=jax.ShapeDtypeStruct((DB, NSA_H, SLOT_ROWS), F32),
        compiler_params=pltpu.CompilerParams(
            dimension_semantics=("arbitrary",), vmem_limit_bytes=VMEM_LIMIT_BYTES),
        name="dec_nsa",
    )(page_table, *([cache_t] * N_PAGES), win_t, qn16, qr16, new_rows, gates, w1t, b1, w2t, b2t, kn, ovT, expand, grp,
      perm)


WIN_SEQS_PER_STEP = 4


def _win_shift_kernel(win_ref, new_ref, o_ref):
    shape = win_ref.shape[1:]
    n = shape[1]
    row = lax.broadcasted_iota(jnp.int32, shape, 0)
    lane = lax.broadcasted_iota(jnp.int32, shape, 1)
    for s in range(win_ref.shape[0]):
        w = win_ref[s]
        col = jnp.sum(jnp.where(row == lane, jnp.broadcast_to(new_ref[s], shape), 0.0), axis=1, keepdims=True)
        o_ref[s] = jnp.where(lane == n - 1, col, pltpu.roll(w, n - 1, 1))


def win_shift(win_t, new_row):
    DB, R, W = win_t.shape
    ns = WIN_SEQS_PER_STEP
    assert R == W and DB % ns == 0
    return pl.pallas_call(
        _win_shift_kernel,
        grid=(DB // ns,),
        in_specs=[pl.BlockSpec((ns, R, W), lambda b: (b, 0, 0)), pl.BlockSpec((ns, 1, R), lambda b: (b, 0, 0))],
        out_specs=pl.BlockSpec((ns, R, W), lambda b: (b, 0, 0)),
        out_shape=jax.ShapeDtypeStruct((DB, R, W), win_t.dtype),
        compiler_params=pltpu.CompilerParams(dimension_semantics=("parallel",), vmem_limit_bytes=VMEM_LIMIT_BYTES),
        name="win_shift",
    )(win_t, new_row)


def _dec_cmp_weights(w1, b1, w2, b2):
    w = jnp.moveaxis(w1, 0, 1).reshape(CMP_STRIDE * NSA_HD, CMP_R * CMP_HID)
    return (w.astype(BF16), b1.reshape(1, CMP_HID), jnp.tile(w2, (1, NSA_G)).astype(BF16),
            jnp.tile(b2, NSA_G).reshape(1, SLOT_ROWS))


def _place_heads(q):
    own = (jnp.arange(NSA_H)[:, None] // NSA_HPG) == jnp.arange(NSA_G)[None, :]
    return jnp.where(own[None, :, :, None], q[:, :, None, :], 0.0).reshape(q.shape[0], NSA_H, SLOT_ROWS)


def _take_heads(o):
    o = o.reshape(o.shape[0], NSA_H, NSA_G, NSA_HD)
    return o[:, jnp.arange(NSA_H), jnp.arange(NSA_H) // NSA_HPG, :].reshape(o.shape[0], NSA_W)


WKV_C = 64
WKV_PAIR = 2 * RWKV_HD
WKV_T_TILE = 512
WKV_PAIRS_PER_STEP = 8


def _split_bf16(x):
    hi = x.astype(BF16)
    return hi, (x - hi.astype(F32)).astype(BF16)


def _dot3(a, b):
    a_hi, a_lo = _split_bf16(a)
    b_hi, b_lo = _split_bf16(b)
    return (jnp.dot(a_hi, b_hi, preferred_element_type=F32) + jnp.dot(a_hi, b_lo, preferred_element_type=F32)
            + jnp.dot(a_lo, b_hi, preferred_element_type=F32))


def _wkv_kernel(r_ref, lw_ref, k_ref, v_ref, a_ref, b_ref, s0_ref, y_ref, sT_ref, s_scr):
    C = WKV_C
    P = WKV_PAIR
    n_chunks = r_ref.shape[0] // C

    @pl.when(pl.program_id(2) == 0)
    def _():
        s_scr[...] = s0_ref[...]

    lo_lane = lax.broadcasted_iota(jnp.int32, (C, P), 1) < RWKV_HD
    row = lax.broadcasted_iota(jnp.int32, (2 * C, 2 * C), 0)
    col = lax.broadcasted_iota(jnp.int32, (2 * C, 2 * C), 1)
    same_head = (row // C) == (col // C)
    strict = same_head & (row > col)
    lower = same_head & (row >= col)
    eye = jnp.where(row == col, 1.0, 0.0)
    tril = jnp.where(lax.broadcasted_iota(jnp.int32, (C, C), 0) >= lax.broadcasted_iota(jnp.int32, (C, C), 1),
                     1.0, 0.0).astype(BF16)

    def stack(x):
        return jnp.concatenate([jnp.where(lo_lane, x, 0.0), jnp.where(lo_lane, 0.0, x)], axis=0)

    def chunk(c, carry):
        stages = [pair_chunk(c, q) for q in range(WKV_PAIRS_PER_STEP)]
        while stages:
            stages = [g for g in stages if next(g, True) is None]
        return carry

    def pair_chunk(c, q):
        sl = pl.ds(pl.multiple_of(c * C, C), C)
        lanes = slice(q * P, (q + 1) * P)
        r, lw, k, v, a, b = (ref[sl, lanes] for ref in (r_ref, lw_ref, k_ref, v_ref, a_ref, b_ref))
        lw_hi, lw_mid = _split_bf16(lw)
        lw_lo = (lw - lw_hi.astype(F32) - lw_mid.astype(F32)).astype(BF16)
        cs = (jnp.dot(tril, lw_hi, preferred_element_type=F32) + jnp.dot(tril, lw_mid, preferred_element_type=F32)
              + jnp.dot(tril, lw_lo, preferred_element_type=F32))
        yield
        g_inv = jnp.exp(-cs)
        g_end = jnp.exp(cs[C - 1:C, :] - cs)
        a2 = stack(a * jnp.exp(cs - lw))
        r2 = stack(r * jnp.exp(cs))
        b2 = stack(b * g_inv)
        k2 = stack(k * g_inv)
        v2 = stack(v)
        s_old = s_scr[q]
        ar = jnp.concatenate([a2, r2], axis=0).astype(BF16)
        bk = jnp.concatenate([b2, k2], axis=0).astype(BF16)
        nt = (((1,), (1,)), ((), ()))
        pp = lax.dot_general(ar, bk, nt, preferred_element_type=F32)
        from_state = lax.dot_general(ar, s_old.astype(BF16), nt, preferred_element_type=F32)
        yield
        l_ab = jnp.where(strict, pp[:2 * C, :2 * C], 0.0)
        l_ak = jnp.where(strict, pp[:2 * C, 2 * C:], 0.0)
        m_rb = jnp.where(lower, pp[2 * C:, :2 * C], 0.0)
        m_rk = jnp.where(lower, pp[2 * C:, 2 * C:], 0.0)
        v2b = v2.astype(BF16)
        rhs = from_state[:2 * C] + jnp.dot(l_ak.astype(BF16), v2b, preferred_element_type=F32)
        yield
        n = l_ab
        x = eye + n
        span = 2
        while span < C:
            n = _dot3(n, n)
            yield
            x = x + _dot3(n, x)
            yield
            span *= 2
        u2 = _dot3(x, rhs)
        yield
        uv = jnp.concatenate([u2, v2], axis=0).astype(BF16)
        y2 = from_state[2 * C:] + jnp.dot(jnp.concatenate([m_rb, m_rk], axis=1).astype(BF16), uv,
                                          preferred_element_type=F32)
        yield
        y_ref[sl, lanes] = y2[:C] + y2[C:]
        bk_end = jnp.concatenate([stack(b * g_end), stack(k * g_end)], axis=0).astype(BF16)
        s_scr[q] = s_old * jnp.exp(cs[C - 1:C, :]) + lax.dot_general(
            uv, bk_end, (((0,), (0,)), ((), ())), preferred_element_type=F32)

    lax.fori_loop(0, n_chunks, chunk, 0)

    @pl.when(pl.program_id(2) == pl.num_programs(2) - 1)
    def _():
        sT_ref[...] = s_scr[...]


def wkv7_chunked(r, lw, k, v, a, b, s0):
    B, T, W = r.shape
    n_pair = W // WKV_PAIR
    tt = min(WKV_T_TILE, T)
    s0p = s0.astype(F32).reshape(B, n_pair, 2, RWKV_HD, RWKV_HD)
    zero = jnp.zeros_like(s0p[:, :, 0])
    s0_bd = jnp.concatenate([jnp.concatenate([s0p[:, :, 0], zero], axis=-1),
                             jnp.concatenate([zero, s0p[:, :, 1]], axis=-1)], axis=-2)
    pps = WKV_PAIRS_PER_STEP
    seq = pl.BlockSpec((None, tt, pps * WKV_PAIR), lambda i, p, t: (i, t, p))
    st = pl.BlockSpec((None, pps, WKV_PAIR, WKV_PAIR), lambda i, p, t: (i, p, 0, 0))
    y, s_bd = pl.pallas_call(
        _wkv_kernel,
        grid=(B, n_pair // pps, T // tt),
        in_specs=[seq] * 6 + [st],
        out_specs=[seq, st],
        out_shape=[jax.ShapeDtypeStruct((B, T, W), F32),
                   jax.ShapeDtypeStruct((B, n_pair, WKV_PAIR, WKV_PAIR), F32)],
        scratch_shapes=[pltpu.VMEM((pps, WKV_PAIR, WKV_PAIR), F32)],
        compiler_params=pltpu.CompilerParams(
            dimension_semantics=("parallel", "parallel", "arbitrary"), vmem_limit_bytes=VMEM_LIMIT_BYTES),
        name="wkv7_chunked",
    )(r, lw, k, v, a, b, s0_bd)
    s_fin = jnp.stack([s_bd[:, :, :RWKV_HD, :RWKV_HD], s_bd[:, :, RWKV_HD:, RWKV_HD:]], axis=2)
    return y, s_fin.reshape(B, W // RWKV_HD, RWKV_HD, RWKV_HD)


AB_PAD = _round_up(AB_COLS, WIDE_COL_TILE)
SHIFT_PAD = _round_up(SHIFT_W, LANE)
LORA_PAD = SHIFT_PAD - 3 * RWKV_W
EVEN_ROWS = 256
POST_ROWS = 256
N_EVEN_PRE_OUT = 10


def _split3(x):
    hi = x.astype(BF16)
    r1 = x - hi.astype(F32)
    mid = r1.astype(BF16)
    return hi, mid, (r1 - mid.astype(F32)).astype(BF16)


def _dot_01(x, m):
    return sum(jnp.dot(part, m, preferred_element_type=F32) for part in _split3(x))


def _head_sum(x, red_ref, exp_ref):
    return _dot_01(_dot_01(x, red_ref[...]), exp_ref[...])


def _expm1(x):
    u = jnp.exp(x)
    d = u - 1.0
    log_u = jnp.where((d == 0.0) | (d == -1.0), 1.0, jnp.log(u))
    return jnp.where(d == 0.0, x, jnp.where(d == -1.0, -1.0, d * x / log_u))


def _even_pre_math(x_ref, prev, taps, prm, outs):
    (cw_ref, cb_ref, wa_ref, ba_ref, wx_ref, bx_ref, lam_ref, mu_ref, w0_ref, a0_ref, wl_ref,
     kk_ref, ka_ref, red_ref, exp_ref) = prm
    a_o, u_o, gate_o, r_o, lw_o, k_o, v_o, na_o, nb_o, g_o = outs
    t1, t2, t3 = taps
    xb = x_ref[:, 0:LRU_W]
    xc = cb_ref[...] + cw_ref[0:1] * t3 + cw_ref[1:2] * t2 + cw_ref[2:3] * t1 + cw_ref[3:4] * xb
    xcb = xc.astype(BF16)
    gate_r = jax.nn.sigmoid(jnp.dot(xcb, wa_ref[...], preferred_element_type=F32) + ba_ref[...])
    gate_i = jax.nn.sigmoid(jnp.dot(xcb, wx_ref[...], preferred_element_type=F32) + bx_ref[...])
    log_a = -LRU_C * gate_r * lam_ref[...]
    a_o[...] = jnp.exp(log_a)
    u_o[...] = jnp.sqrt(-_expm1(2.0 * log_a)) * (gate_i * xc)
    gate_o[...] = jax.nn.gelu(x_ref[:, LRU_W:2 * LRU_W])
    rw = x_ref[:, 2 * LRU_W:2 * LRU_W + SHIFT_PAD]
    rs = rw + mu_ref[...] * (prev - rw)
    r_o[...] = rs[:, 0:RWKV_W]
    k = rs[:, RWKV_W:2 * RWKV_W]
    v_o[...] = rs[:, 2 * RWKV_W:3 * RWKV_W]
    tail = rs[:, 3 * RWKV_W:]
    lane = lax.broadcasted_iota(jnp.int32, tail.shape, 1)
    act = jnp.where(lane < W_LORA, jnp.tanh(tail), jnp.where(lane < W_LORA + A_LORA, tail, jax.nn.sigmoid(tail)))
    z = jnp.dot(act.astype(BF16), wl_ref[...], preferred_element_type=F32)
    w_log = -jax.nn.softplus(-(w0_ref[...] + z[:, 0:RWKV_W])) - 0.5
    lw_o[...] = -jnp.exp(w_log)
    a_icl = jax.nn.sigmoid(a0_ref[...] + z[:, RWKV_W:2 * RWKV_W])
    g_o[...] = z[:, 2 * RWKV_W:]
    kk = k * kk_ref[...]
    kk = kk / jnp.maximum(jnp.sqrt(_head_sum(kk * kk, red_ref, exp_ref)), 1e-12)
    k_o[...] = k * (1.0 + (a_icl - 1.0) * ka_ref[...])
    na_o[...] = -kk
    nb_o[...] = kk * a_icl


def _even_pre_seq_kernel(x_ref, conv0_ref, shift0_ref, *refs):
    prm = refs[:15]
    outs = refs[15:15 + N_EVEN_PRE_OUT]
    conv_c, shift_c = refs[15 + N_EVEN_PRE_OUT:]
    rows = x_ref.shape[0]

    @pl.when(pl.program_id(1) == 0)
    def _():
        conv_c[...] = conv0_ref[...]
        shift_c[...] = shift0_ref[...]

    xb = x_ref[:, 0:LRU_W]
    row = lax.broadcasted_iota(jnp.int32, xb.shape, 0)
    taps = []
    for j in (1, 2, 3):
        tap = pltpu.roll(xb, j, 0)
        for i in range(j):
            tap = jnp.where(row == i, conv_c[8 - j + i:9 - j + i, :], tap)
        taps.append(tap)
    rw = x_ref[:, 2 * LRU_W:2 * LRU_W + SHIFT_PAD]
    row_w = lax.broadcasted_iota(jnp.int32, rw.shape, 0)
    prev = jnp.where(row_w == 0, shift_c[7:8, :], pltpu.roll(rw, 1, 0))
    _even_pre_math(x_ref, prev, taps, prm, outs)
    conv_c[...] = x_ref[rows - 8:rows, 0:LRU_W]
    shift_c[...] = x_ref[rows - 8:rows, 2 * LRU_W:2 * LRU_W + SHIFT_PAD]


def _even_pre_step_kernel(x_ref, prev_ref, t1_ref, t2_ref, t3_ref, *refs):
    _even_pre_math(x_ref, prev_ref[...], (t1_ref[...], t2_ref[...], t3_ref[...]), refs[:15], refs[15:])


def _even_params(p):
    def bd(w):
        eye = jnp.eye(LRU_BLOCKS, dtype=w.dtype)
        return (eye[:, None, :, None] * w[:, :, None, :]).reshape(LRU_W, LRU_W).astype(BF16)
    row = lambda v: v.reshape(1, -1).astype(F32)
    wl = jnp.zeros((LORA_PAD, 3 * RWKV_W), F32)
    wl = wl.at[0:W_LORA, 0:RWKV_W].set(p['w2'])
    wl = wl.at[W_LORA:W_LORA + A_LORA, RWKV_W:2 * RWKV_W].set(p['a2'])
    wl = wl.at[W_LORA + A_LORA:W_LORA + A_LORA + G_LORA, 2 * RWKV_W:].set(p['g2'])
    head = np.arange(RWKV_W) // RWKV_HD
    red = jnp.asarray(head[:, None] == np.arange(LANE)[None, :], dtype=BF16)
    mu = jnp.pad(p['mu'], (0, SHIFT_PAD - SHIFT_W))
    return [p['conv_w'].astype(F32), row(p['conv_b']), bd(p['wa']), row(p['ba']), bd(p['wx']), row(p['bx']),
            row(jax.nn.softplus(-p['lam'].astype(F32))), row(mu), row(p['w0']), row(p['a0']), wl.astype(BF16),
            row(p['k_k']), row(p['k_a']), red, red.T]


def _const_spec(a, n_grid):
    return pl.BlockSpec(a.shape, lambda *_: (0,) * a.ndim)


def even_pre_seq(proj, conv0, shift0, prm, B, T):
    tr = EVEN_ROWS
    nt = T // tr
    conv_pad = jnp.pad(conv0.astype(F32), ((0, 0), (8 - (CONV_W - 1), 0), (0, 0)))
    shift_pad = jnp.pad(shift0.astype(F32)[:, None, :], ((0, 0), (7, 0), (0, SHIFT_PAD - SHIFT_W)))
    out_spec = pl.BlockSpec((tr, LRU_W), lambda b, t: (b * nt + t, 0))
    return pl.pallas_call(
        _even_pre_seq_kernel,
        grid=(B, nt),
        in_specs=[pl.BlockSpec((tr, AB_PAD), lambda b, t: (b * nt + t, 0)),
                  pl.BlockSpec((None, 8, LRU_W), lambda b, t: (b, 0, 0)),
                  pl.BlockSpec((None, 8, SHIFT_PAD), lambda b, t: (b, 0, 0))] + [_const_spec(a, 2) for a in prm],
        out_specs=[out_spec] * N_EVEN_PRE_OUT,
        out_shape=[jax.ShapeDtypeStruct((B * T, LRU_W), F32)] * N_EVEN_PRE_OUT,
        scratch_shapes=[pltpu.VMEM((8, LRU_W), F32), pltpu.VMEM((8, SHIFT_PAD), F32)],
        compiler_params=pltpu.CompilerParams(
            dimension_semantics=("parallel", "arbitrary"), vmem_limit_bytes=VMEM_LIMIT_BYTES),
        name="even_pre_seq",
    )(proj, conv_pad, shift_pad, *prm)


def even_pre_step(proj, row0, conv0, shift0, prm):
    n = conv0.shape[0]
    shift_pad = jnp.pad(shift0.astype(F32), ((0, 0), (0, SHIFT_PAD - SHIFT_W)))
    taps = [conv0[:, CONV_W - 1 - j].astype(F32) for j in (1, 2, 3)]
    full = lambda w: pl.BlockSpec((n, w), lambda i: (0, 0))
    return pl.pallas_call(
        _even_pre_step_kernel,
        grid=(1,),
        in_specs=[pl.BlockSpec((n, AB_PAD), lambda i: (row0 // n, 0)), full(SHIFT_PAD)] + [full(LRU_W)] * 3
        + [_const_spec(a, 1) for a in prm],
        out_specs=[full(LRU_W)] * N_EVEN_PRE_OUT,
        out_shape=[jax.ShapeDtypeStruct((n, LRU_W), F32)] * N_EVEN_PRE_OUT,
        compiler_params=pltpu.CompilerParams(
            dimension_semantics=("arbitrary",), vmem_limit_bytes=VMEM_LIMIT_BYTES),
        name="even_pre_step",
    )(proj, shift_pad, *taps, *prm)


def _even_post_kernel(hs_ref, gate_ref, y_ref, r_ref, k_ref, v_ref, g_ref, lng_ref, lnb_ref, rk_ref,
                      red_ref, exp_ref, *rest):
    o_ref = rest[-1]
    y = y_ref[...]
    mu = _head_sum(y, red_ref, exp_ref) * (1.0 / RWKV_HD)
    d = y - mu
    var = _head_sum(d * d, red_ref, exp_ref) * (1.0 / RWKV_HD)
    yn = d * lax.rsqrt(var + 64e-5) * lng_ref[...] + lnb_ref[...]
    bonus = _head_sum(r_ref[...] * k_ref[...] * rk_ref[...], red_ref, exp_ref) * v_ref[...]
    o_ref[:, 0:LRU_W] = (hs_ref[...] * gate_ref[...]).astype(o_ref.dtype)
    o_ref[:, LRU_W:] = ((yn + bonus) * g_ref[...]).astype(o_ref.dtype)


def even_post(hs, gate, y, r, k, v, g, p, red, n_total, row0, prior=None):
    n = hs.shape[0]
    tr = min(POST_ROWS, n)
    row = lambda a: a.reshape(1, -1).astype(F32)
    consts = [row(p['ln_g']), row(p['ln_b']), row(p['r_k']), red, red.T]
    seq = pl.BlockSpec((tr, LRU_W), lambda i: (i, 0))
    args = [hs, gate, y, r, k, v, g] + consts
    in_specs = [seq] * 7 + [_const_spec(a, 1) for a in consts]
    aliases = {}
    if prior is not None:
        args.append(prior)
        in_specs.append(pl.BlockSpec(memory_space=pl.ANY))
        aliases = {len(args) - 1: 0}
    return pl.pallas_call(
        _even_post_kernel,
        grid=(n // tr,),
        in_specs=in_specs,
        out_specs=pl.BlockSpec((tr, D_MODEL), lambda i: (row0 // tr + i, 0)),
        out_shape=jax.ShapeDtypeStruct((n_total, D_MODEL), BF16),
        input_output_aliases=aliases,
        compiler_params=pltpu.CompilerParams(
            dimension_semantics=("parallel",), vmem_limit_bytes=VMEM_LIMIT_BYTES),
        name="even_post",
    )(*args)


def _retention_kernel(q_ref, k_ref, va_ref, vb_ref, dm_ref, rd_ref, kd_ref, sd_ref, o_ref, s_out_ref, s_scr):
    C = q_ref.shape[0]
    n_pair = RET_H // 2

    @pl.when(pl.program_id(1) == 0)
    def _():
        s_scr[...] = jnp.zeros(s_scr.shape, F32)

    lo = lax.broadcasted_iota(jnp.int32, (C, 2 * RET_DK), 1) < RET_DK

    def stack(x):
        return jnp.concatenate([jnp.where(lo, x, 0.0), jnp.where(lo, 0.0, x)], axis=0)

    for p in range(n_pair):
        qk = slice(p * 2 * RET_DK, (p + 1) * 2 * RET_DK)
        q2 = stack(q_ref[:, qk]).astype(BF16)
        k2 = stack(k_ref[:, qk])
        v0 = p * 2 * RET_DV
        v_ref = va_ref if p < n_pair // 2 else vb_ref
        vl = v0 % (RET_W // 2)
        v2 = jnp.concatenate([v_ref[:, vl:vl + RET_DV], v_ref[:, vl + RET_DV:vl + 2 * RET_DV]],
                             axis=0).astype(BF16)
        s = lax.dot_general(q2, k2.astype(BF16), (((1,), (1,)), ((), ())), preferred_element_type=F32) * dm_ref[p]
        s_old = s_scr[p]
        o2 = jnp.dot(s.astype(BF16), v2, preferred_element_type=F32) + jnp.dot(
            q2, s_old.astype(BF16), preferred_element_type=F32) * rd_ref[p]
        o_ref[:, v0:v0 + RET_DV] = o2[:C]
        o_ref[:, v0 + RET_DV:v0 + 2 * RET_DV] = o2[C:]
        s_scr[p] = s_old * sd_ref[p] + lax.dot_general((k2 * kd_ref[p]).astype(BF16), v2, (((0,), (0,)), ((), ())),
                                                       preferred_element_type=F32)

    @pl.when(pl.program_id(1) == pl.num_programs(1) - 1)
    def _():
        s_out_ref[...] = s_scr[...]


def retention_prompt_pallas(rq, rk, rv, B, T, v_col0=0):
    C = RET_CHUNK
    nc = T // C
    f32 = F32
    lg = jnp.log1p(-jnp.exp2(-5.0 - jnp.arange(RET_H, dtype=f32))).reshape(RET_H // 2, 2)
    i = jnp.arange(C, dtype=f32)
    diff = i[:, None] - i[None, :]
    causal = diff >= 0
    dmask = jnp.where(causal, jnp.exp(jnp.where(causal, diff, 0.0)[None, None] * lg[:, :, None, None]), 0.0)
    zero = jnp.zeros_like(dmask[:, 0])
    dm = jnp.concatenate([jnp.concatenate([dmask[:, 0], zero], axis=-1),
                          jnp.concatenate([zero, dmask[:, 1]], axis=-1)], axis=-2)
    rows = lambda x, w: jnp.broadcast_to(x[:, :, :, None], x.shape + (w,)).reshape(RET_H // 2, -1, w)
    rd = rows(jnp.exp((i[None, None, :] + 1.0) * lg[:, :, None]), RET_DV)
    kd = rows(jnp.exp((C - 1.0 - i)[None, None, :] * lg[:, :, None]), 2 * RET_DK)
    sd = rows(jnp.broadcast_to(jnp.exp(C * lg)[:, :, None], (RET_H // 2, 2, RET_DK)), RET_DV)
    half_w = RET_W // 2
    qk_spec = pl.BlockSpec((C, RET_H * RET_DK), lambda b, c: (b * nc + c, 0))
    v_spec = lambda k: pl.BlockSpec((C, half_w), lambda b, c: (b * nc + c, v_col0 // half_w + k))
    const = lambda a: pl.BlockSpec(a.shape, lambda b, c: (0, 0, 0))
    o, s = pl.pallas_call(
        _retention_kernel,
        grid=(B, nc),
        in_specs=[qk_spec, qk_spec, v_spec(0), v_spec(1), const(dm), const(rd), const(kd), const(sd)],
        out_specs=[pl.BlockSpec((C, RET_W), lambda b, c: (b * nc + c, 0)),
                   pl.BlockSpec((None, RET_H // 2, 2 * RET_DK, RET_DV), lambda b, c: (b, 0, 0, 0))],
        out_shape=[jax.ShapeDtypeStruct((B * T, RET_W), f32),
                   jax.ShapeDtypeStruct((B, RET_H // 2, 2 * RET_DK, RET_DV), f32)],
        scratch_shapes=[pltpu.VMEM((RET_H // 2, 2 * RET_DK, RET_DV), f32)],
        compiler_params=pltpu.CompilerParams(
            dimension_semantics=("parallel", "arbitrary"), vmem_limit_bytes=VMEM_LIMIT_BYTES),
        name="retention_prompt",
    )(rq, rk, rv, rv, dm, rd, kd, sd)
    return s.reshape(B, RET_H, RET_DK, RET_DV), o


KV_W = NSA_G * NSA_HD
RET_QK_W = RET_H * RET_DK
OFF_Q = 0
OFF_KC = OFF_Q + NSA_W
OFF_VC = OFF_KC + KV_W
OFF_KS = OFF_VC + KV_W
OFF_VS = OFF_KS + KV_W
OFF_KW = OFF_VS + KV_W
OFF_VW = OFF_KW + KV_W
OFF_RQ = OFF_VW + KV_W
OFF_RK = OFF_RQ + RET_QK_W
OFF_RV = OFF_RK + RET_QK_W
OFF_RG = OFF_RV + RET_W
OFF_GT = OFF_RG + RET_W
CD_PAD = _round_up(OFF_GT + LANE, COL_TILE)
ODD_ROWS = 256
N_ODD_PRE_OUT = 11


def _odd_weight_cols(w):
    gt0 = NSA_W + 6 * KV_W
    body = jnp.concatenate([w[:, :gt0], w[:, gt0 + 3 * NSA_H:]], axis=1)
    gt = w[:, gt0:gt0 + 3 * NSA_H]
    out = jnp.concatenate([body, gt], axis=1)
    return jnp.pad(out, ((0, 0), (0, CD_PAD - out.shape[1]))).astype(BF16)


def _rope_tables(pos, n_rot, theta, head):
    half = n_rot // 2
    inv = jnp.exp(-jnp.log(jnp.float32(theta)) * jnp.arange(half, dtype=jnp.float32) / half)
    ang = pos.astype(jnp.float32)[:, None] * inv[None, :]
    cos, sin = jnp.cos(ang), jnp.sin(ang)
    d = np.arange(LANE) % head
    cos_d, sin_d = cos[:, d % half], sin[:, d % half]
    c = jnp.where(d < n_rot, cos_d, 1.0)
    s1 = jnp.where(d < half, -sin_d, 0.0)
    s2 = jnp.where((d >= half) & (d < n_rot), sin_d, 0.0)
    return jnp.stack([c, s1, s2])


def _rope_lanes(x, tab_ref, half):
    w = x.shape[1]
    rep = w // LANE
    c, s1, s2 = (pltpu.repeat(tab_ref[i], rep, axis=1) for i in range(3))
    return x * c + pltpu.roll(x, w - half, 1) * s1 + pltpu.roll(x, half, 1) * s2


def _rms_heads(x, g_ref, red_ref, exp_ref):
    ms = _head_sum(x * x, red_ref, exp_ref) * (1.0 / NSA_HD)
    return x * lax.rsqrt(ms + 1e-6) * g_ref[...]


def _odd_pre_kernel(x_ref, nsa_tab, ret_tab, qg_ref, ksg_ref, kwg_ref, redq_ref, expq_ref, redk_ref, expk_ref,
                    tile_ref, qn_o, qr_o, ks_o, kw_o, ks4_o, vs4_o, kw4_o, vw4_o, gate_o, rq_o, rk_o, kvt_o, wint_o):
    nsa_half = ROPE_DIMS // 2
    qn = _rms_heads(x_ref[:, OFF_Q:OFF_Q + NSA_W], qg_ref, redq_ref, expq_ref)
    qn_o[...] = qn
    qr_o[...] = _rope_lanes(qn, nsa_tab, nsa_half)
    ks = _rope_lanes(_rms_heads(x_ref[:, OFF_KS:OFF_KS + KV_W], ksg_ref, redk_ref, expk_ref), nsa_tab, nsa_half)
    kw = _rope_lanes(_rms_heads(x_ref[:, OFF_KW:OFF_KW + KV_W], kwg_ref, redk_ref, expk_ref), nsa_tab, nsa_half)
    ks_o[...] = ks
    kw_o[...] = kw
    tile = tile_ref[...]
    for src, dst in ((ks, ks4_o), (x_ref[:, OFF_VS:OFF_VS + KV_W], vs4_o), (kw, kw4_o),
                     (x_ref[:, OFF_VW:OFF_VW + KV_W], vw4_o)):
        dst[...] = jnp.dot(src.astype(BF16), tile, preferred_element_type=F32).astype(BF16)
    gate_o[...] = jax.nn.sigmoid(x_ref[:, OFF_GT:OFF_GT + LANE])
    rq_o[...] = _rope_lanes(x_ref[:, OFF_RQ:OFF_RQ + RET_QK_W], ret_tab, RET_DK // 2)
    rk_o[...] = _rope_lanes(x_ref[:, OFF_RK:OFF_RK + RET_QK_W], ret_tab, RET_DK // 2) * (RET_DK ** -0.5)
    kv_pieces = (x_ref[:, OFF_KC:OFF_KC + KV_W], x_ref[:, OFF_VC:OFF_VC + KV_W], ks, x_ref[:, OFF_VS:OFF_VS + KV_W])
    for dst, pieces in ((kvt_o, kv_pieces), (wint_o, (kw, x_ref[:, OFF_VW:OFF_VW + KV_W]))):
        for s, piece in enumerate(pieces):
            for c in range(KV_W // LANE):
                dst[s * KV_W + c * LANE:s * KV_W + (c + 1) * LANE, :] = piece[:, c * LANE:(c + 1) * LANE].T


def odd_pre(proj, pos, p, row0, n_rows, same_pos, seq_len):
    tr = min(ODD_ROWS, n_rows)
    blk0 = row0 // tr
    n_tab = tr if same_pos else n_rows
    pos_rows = jnp.broadcast_to(pos, (n_tab,)) if same_pos else pos
    nsa_tab = _rope_tables(pos_rows, ROPE_DIMS, ROPE_THETA, NSA_HD)
    ret_tab = _rope_tables(pos_rows, RET_DK, RET_THETA, RET_DK)
    row = lambda v, rep: jnp.tile(v.astype(F32), rep).reshape(1, -1)
    lanes = np.arange(LANE)
    red_q = jnp.asarray((np.arange(NSA_W) // NSA_HD)[:, None] == lanes[None, :], dtype=BF16)
    red_k = jnp.asarray((np.arange(KV_W) // NSA_HD)[:, None] == lanes[None, :], dtype=BF16)
    src = np.arange(KV_W)
    dst = np.arange(NSA_W)
    tile = jnp.asarray((src[:, None] // NSA_HD == dst[None, :] // GROUP_W)
                       & (src[:, None] % NSA_HD == dst[None, :] % NSA_HD), dtype=BF16)
    consts = [row(p['q_norm'], NSA_H), row(p['k_norm'][1], NSA_G), row(p['k_norm'][2], NSA_G),
              red_q, red_q.T, red_k, red_k.T, tile]
    tab_spec = pl.BlockSpec((3, tr, LANE), (lambda i: (0, 0, 0)) if same_pos else (lambda i: (0, i, 0)))
    out = lambda w, dt: (pl.BlockSpec((tr, w), lambda i: (i, 0)), jax.ShapeDtypeStruct((n_rows, w), dt))
    seq_tiles = seq_len // tr
    out_t = lambda r: (pl.BlockSpec((None, r, tr), lambda i: (i // seq_tiles, 0, i % seq_tiles)),
                       jax.ShapeDtypeStruct((n_rows // seq_len, r, seq_len), F32))
    outs = [out(NSA_W, F32), out(NSA_W, F32), out(KV_W, F32), out(KV_W, F32)] + [out(NSA_W, BF16)] * 4 + [
        out(LANE, F32), out(RET_QK_W, F32), out(RET_QK_W, F32), out_t(KV_SLOTS * KV_W), out_t(2 * KV_W)]
    return pl.pallas_call(
        _odd_pre_kernel,
        grid=(n_rows // tr,),
        in_specs=[pl.BlockSpec((tr, CD_PAD), lambda i: (blk0 + i, 0)), tab_spec, tab_spec]
        + [_const_spec(a, 1) for a in consts],
        out_specs=[o[0] for o in outs],
        out_shape=[o[1] for o in outs],
        compiler_params=pltpu.CompilerParams(
            dimension_semantics=("parallel",), vmem_limit_bytes=VMEM_LIMIT_BYTES),
        name="odd_pre",
    )(proj, nsa_tab, ret_tab, *consts)


def _odd_post_kernel(oc_ref, os_ref, ow_ref, gate_ref, ret_ref, rg0_ref, rg1_ref, gng_ref, gnb_ref, ge_ref, *rest,
                     gated):
    o_ref = rest[-1]
    if gated:
        nsa = oc_ref[...]
    else:
        gates = gate_ref[...]
        nsa = jnp.zeros(oc_ref.shape, F32)
        for j, branch in enumerate((oc_ref, os_ref, ow_ref)):
            nsa = nsa + _dot_01(gates, ge_ref[j]) * branch[...]
    o_ref[:, 0:NSA_W] = nsa.astype(o_ref.dtype)
    for h in range(RET_H):
        lanes = slice(h * RET_DV, (h + 1) * RET_DV)
        x = ret_ref[:, lanes]
        mu = jnp.mean(x, axis=-1, keepdims=True)
        d = x - mu
        var = jnp.mean(d * d, axis=-1, keepdims=True)
        yn = d * lax.rsqrt(var + 1e-5) * gng_ref[:, lanes] + gnb_ref[:, lanes]
        rg = (rg0_ref if h < RET_H // 2 else rg1_ref)[:, (h % (RET_H // 2)) * RET_DV:(h % (RET_H // 2) + 1) * RET_DV]
        o_ref[:, NSA_W + h * RET_DV:NSA_W + (h + 1) * RET_DV] = (yn * (rg * jax.nn.sigmoid(rg))).astype(o_ref.dtype)


def odd_post(o_cmp, o_slc, o_win, gates, o_ret, proj, p, n_total, row0, prior=None, gated=False):
    n = o_cmp.shape[0]
    tr = min(POST_ROWS, n)
    blk0 = row0 // tr
    h = np.arange(NSA_W) // NSA_HD
    ge = jnp.asarray(np.stack([(np.arange(LANE)[:, None] == (3 * h + j)[None, :]) for j in range(3)]), dtype=BF16)
    row = lambda a: a.reshape(1, -1).astype(F32)
    consts = [row(p['gn_g']), row(p['gn_b']), ge]
    seq = lambda w: pl.BlockSpec((tr, w), lambda i: (i, 0))
    half = RET_W // 2
    rg_spec = lambda k: pl.BlockSpec((tr, half), lambda i: (blk0 + i, OFF_RG // half + k))
    args = [o_cmp, o_slc, o_win, gates, o_ret, proj, proj] + consts
    in_specs = [seq(NSA_W)] * 3 + [seq(LANE), seq(RET_W), rg_spec(0), rg_spec(1)] + [_const_spec(a, 1) for a in consts]
    aliases = {}
    if prior is not None:
        args.append(prior)
        in_specs.append(pl.BlockSpec(memory_space=pl.ANY))
        aliases = {len(args) - 1: 0}
    return pl.pallas_call(
        functools.partial(_odd_post_kernel, gated=gated),
        grid=(n // tr,),
        in_specs=in_specs,
        out_specs=pl.BlockSpec((tr, D_MODEL), lambda i: (blk0 + i, 0)),
        out_shape=jax.ShapeDtypeStruct((n_total, D_MODEL), BF16),
        input_output_aliases=aliases,
        compiler_params=pltpu.CompilerParams(
            dimension_semantics=("parallel",), vmem_limit_bytes=VMEM_LIMIT_BYTES),
        name="odd_post",
    )(*args)


def wkv7_scan(r, w, k, v, a, b, s0):
    xs = tuple(jnp.moveaxis(z.astype(jnp.float32), 1, 0) for z in (r, w, k, v, a, b))

    def step(S, inp):
        r_t, w_t, k_t, v_t, a_t, b_t = inp
        sa = jnp.einsum('bhij,bhj->bhi', S, a_t)
        S = S * w_t[:, :, None, :] + sa[..., None] * b_t[:, :, None, :] + v_t[..., None] * k_t[:, :, None, :]
        return S, jnp.einsum('bhij,bhj->bhi', S, r_t)

    S, ys = lax.scan(step, s0.astype(jnp.float32), xs)
    return jnp.moveaxis(ys, 0, 1), S


def even_mixer(proj, p, B, T, DB, lru_h0, lru_conv0, shift0, wkv0):
    f32 = F32
    prm = _even_params(p)
    red = prm[-2]
    n_p = B * T
    zeros = lambda *s: jnp.zeros(s, f32)
    a, u, gate, r, lw, k, v, na, nb, g = even_pre_seq(proj, zeros(B, CONV_W - 1, LRU_W), zeros(B, SHIFT_W), prm, B, T)
    seq = lambda z: z.reshape(B, T, LRU_W)
    hs = lru_scan(seq(a), seq(u), zeros(B, LRU_W))
    yw, wkv_p = wkv7_chunked(seq(r), seq(lw), seq(k), seq(v), seq(na), seq(nb), zeros(B, RWKV_H, RWKV_HD, RWKV_HD))
    cat = even_post(hs.reshape(n_p, LRU_W), gate, yw.reshape(n_p, RWKV_W), r, k, v, g, p, red, n_p + DB, 0)
    tail = lambda b, n, c0, c1: proj[(b + 1) * T - n:(b + 1) * T, c0:c1]
    st_p = (hs[:, -1], jnp.stack([tail(b, CONV_W - 1, 0, LRU_W) for b in range(B)]),
            jnp.concatenate([tail(b, 1, 2 * LRU_W, AB_COLS) for b in range(B)], axis=0), wkv_p)
    a, u, gate, r, lw, k, v, na, nb, g = even_pre_step(proj, n_p, lru_conv0, shift0, prm)
    hs_s = a * lru_h0.astype(f32) + u
    heads = (DB, 1, RWKV_H, RWKV_HD)
    yw, wkv_s = wkv7_scan(r.reshape(heads), jnp.exp(lw).reshape(heads), k.reshape(heads), v.reshape(heads),
                          na.reshape(heads), nb.reshape(heads), wkv0)
    cat = even_post(hs_s, gate, yw.reshape(DB, RWKV_W), r, k, v, g, p, red, n_p + DB, n_p, prior=cat)
    xb_s = proj[n_p:]
    conv_s = jnp.concatenate([lru_conv0[:, 1:].astype(f32), xb_s[:, None, :LRU_W]], axis=1)
    st_s = (hs_s, conv_s, xb_s[:, 2 * LRU_W:AB_COLS], wkv_s)
    return cat, st_p, st_s


def _cmp_mlp_kernel(ch_ref, w1_ref, b1_ref, w2_ref, b2_ref, kn_ref, o_ref, *, normalize):
    part = jnp.dot(ch_ref[...].astype(BF16), w1_ref[...], preferred_element_type=F32)
    rows = part.shape[0]
    pre = b1_ref[...] + part[:, :CMP_HID] + pltpu.roll(part[:, CMP_HID:], rows - NSA_G, 0)
    out = jnp.dot(jax.nn.gelu(pre).astype(BF16), w2_ref[...], preferred_element_type=F32) + b2_ref[...]
    o_ref[...] = _rms_rows(out, kn_ref[...]) if normalize else out


def nsa_compress(x, w1, b1, w2, b2, k_norm=None):
    assert CMP_R == 2
    B, L = x.shape[:2]
    n_chunk = L // CMP_STRIDE
    n_cmp = n_chunk - CMP_R + 1
    ch = x[:, :n_chunk * CMP_STRIDE].astype(BF16).reshape(B, n_chunk, CMP_STRIDE, NSA_G, NSA_HD)
    ch = jnp.moveaxis(ch, 3, 2).reshape(B * n_chunk * NSA_G, CMP_STRIDE * NSA_HD)
    w1c = jnp.moveaxis(w1, 0, 1).reshape(CMP_STRIDE * NSA_HD, CMP_R * CMP_HID).astype(BF16)
    kn = jnp.ones((NSA_HD,), F32) if k_norm is None else k_norm.astype(F32)
    consts = [w1c, b1.reshape(1, CMP_HID).astype(F32), w2.astype(BF16), b2.reshape(1, NSA_HD).astype(F32),
              kn.reshape(1, NSA_HD)]
    out = pl.pallas_call(
        functools.partial(_cmp_mlp_kernel, normalize=k_norm is not None),
        grid=(1,),
        in_specs=[_const_spec(ch, 1)] + [_const_spec(a, 1) for a in consts],
        out_specs=pl.BlockSpec((ch.shape[0], NSA_HD), lambda i: (0, 0)),
        out_shape=jax.ShapeDtypeStruct((ch.shape[0], NSA_HD), F32),
        compiler_params=pltpu.CompilerParams(dimension_semantics=("arbitrary",), vmem_limit_bytes=VMEM_LIMIT_BYTES),
        name="nsa_compress",
    )(ch, *consts)
    return out.reshape(B, n_chunk, NSA_G, NSA_HD)[:, :n_cmp]


RET_STEP_SEQS = 8


def _retention_step_kernel(q_ref, k_ref, v_ref, s_ref, g_ref, grow_ref, o_ref, s_out_ref):
    head = lax.broadcasted_iota(jnp.int32, (RET_H, RET_QK_W), 0)
    own = lax.broadcasted_iota(jnp.int32, (RET_H, RET_QK_W), 1) // RET_DK == head
    gam = g_ref[...]
    gam_rows = grow_ref[...]
    for s in range(q_ref.shape[0]):
        q8 = jnp.where(own, jnp.broadcast_to(q_ref[s], (RET_H, RET_QK_W)), 0.0).astype(BF16)
        k8 = jnp.where(own, jnp.broadcast_to(k_ref[s], (RET_H, RET_QK_W)), 0.0).astype(BF16)
        v8 = jnp.concatenate([v_ref[s, :, h * RET_DV:(h + 1) * RET_DV] for h in range(RET_H)], axis=0).astype(BF16)
        state = s_ref[s].reshape(RET_H * RET_DK, RET_DV)
        qk = jnp.sum(q8.astype(F32) * k8.astype(F32), axis=-1, keepdims=True)
        o_ref[s] = qk.astype(BF16).astype(F32) * v8.astype(F32) + gam * jnp.dot(
            q8, state.astype(BF16), preferred_element_type=F32)
        upd = lax.dot_general(k8, v8, (((0,), (0,)), ((), ())), preferred_element_type=F32)
        s_out_ref[s] = (state * gam_rows + upd).reshape(RET_H, RET_DK, RET_DV)


def retention_step(rq, rk, rv, s0):
    DB = rq.shape[0]
    ns = RET_STEP_SEQS
    lg = jnp.log1p(-jnp.exp2(-5.0 - jnp.arange(RET_H, dtype=F32)))
    gam = jnp.exp(lg).reshape(RET_H, 1)
    gam_rows = jnp.repeat(gam, RET_DK, axis=0)
    row = lambda w: pl.BlockSpec((ns, 1, w), lambda i: (i, 0, 0))
    st = pl.BlockSpec((ns, RET_H, RET_DK, RET_DV), lambda i: (i, 0, 0, 0))
    return pl.pallas_call(
        _retention_step_kernel,
        grid=(DB // ns,),
        in_specs=[row(RET_QK_W), row(RET_QK_W), row(RET_W), st, _const_spec(gam, 1), _const_spec(gam_rows, 1)],
        out_specs=[pl.BlockSpec((ns, RET_H, RET_DV), lambda i: (i, 0, 0)), st],
        out_shape=[jax.ShapeDtypeStruct((DB, RET_H, RET_DV), F32),
                   jax.ShapeDtypeStruct((DB, RET_H, RET_DK, RET_DV), F32)],
        compiler_params=pltpu.CompilerParams(dimension_semantics=("parallel",), vmem_limit_bytes=VMEM_LIMIT_BYTES),
        name="retention_step",
    )(rq, rk, rv, s0.astype(F32), gam, gam_rows)


def odd_mixer(proj, p, B, T, DB, cache_layer, page_table, win_buf, ret_s0):
    assert DEC_SEQ == 1 and win_buf.shape[1] == WIN_BUF
    n_p = B * T
    kv = (B, T, NSA_G, NSA_HD)
    cols = lambda rows, off, w: proj[rows, off:off + w]
    prompt = slice(0, n_p)
    dec = slice(n_p, n_p + DB)
    qn, qr, ks, kw, ks4, vs4, kw4, vw4, gates, rq, rk, kv_t, win_t_p = odd_pre(
        proj, jnp.tile(jnp.arange(T), B), p, 0, n_p, False, T)
    seq = lambda z: z.reshape(B, T, -1)
    kc_raw = cols(prompt, OFF_KC, KV_W).reshape(kv)
    vc_raw = cols(prompt, OFF_VC, KV_W).reshape(kv)
    kc = nsa_compress(kc_raw, *p['ck'], k_norm=p['k_norm'][0])
    vc = nsa_compress(vc_raw, *p['cv'])
    n_cmp = kc.shape[1]
    n_sel = -(-T // SEL_BLOCK)
    o_cmp, sel = nsa_cmp_select(seq(qn), _tile_cmp(kc), _tile_cmp(vc), _overlap_T(n_cmp, n_sel),
                                n_cmp=n_cmp, n_sel=n_sel, q_pos0=0)
    o_slc = nsa_flash(seq(qr), seq(ks4), seq(vs4), sel, _sel_expand(T))
    o_win = nsa_flash(seq(qr), seq(kw4), seq(vw4))
    ret_p, o_ret = retention_prompt_pallas(rq, rk, proj, B, T, v_col0=OFF_RV)
    flat = lambda z: z.reshape(n_p, -1)
    cat = odd_post(flat(o_cmp), flat(o_slc), flat(o_win), gates, o_ret, proj, p, n_p + DB, 0)
    n_win = min(WINDOW, T)
    kv_rows_p = jnp.transpose(kv_t.reshape(B, KV_SLOTS, NSA_G, NSA_HD, T), (0, 4, 1, 2, 3))
    win_p = jnp.transpose(win_t_p[:, :, T - n_win:].reshape(B, 2, NSA_G, NSA_HD, n_win), (0, 4, 1, 2, 3))
    qn, qr, ks, kw, _, _, _, _, gates, rq, rk, kv_t, _ = odd_pre(proj, jnp.asarray(PAST_LEN), p, n_p, DB, True, DB)
    scale = NSA_HD ** -0.5
    heads = lambda z: z.reshape(DB, NSA_H, NSA_HD)
    vs, vw = cols(dec, OFF_VS, KV_W), cols(dec, OFF_VW, KV_W)
    new_rows = jnp.stack([ks, vs, kw, vw], axis=1)
    cache_t = jnp.transpose(cache_layer, (0, 2, 3, 4, 1)).reshape(cache_layer.shape[0], KV_ROWS, PAGE_SIZE)
    win_t = jnp.transpose(win_buf, (0, 2, 3, 4, 1)).reshape(DB, 2 * SLOT_ROWS, WIN_BUF)
    w1t, b1, w2t, b2t = (jnp.stack([a, b]) for a, b in zip(_dec_cmp_weights(*p['ck']), _dec_cmp_weights(*p['cv'])))
    kn = jnp.tile(p['k_norm'][0], NSA_G).reshape(1, SLOT_ROWS)
    t = np.arange(PAST_LEN)
    expand = jnp.asarray(np.arange(CMP_PAD)[:, None] == (t // SEL_BLOCK)[None, :], dtype=BF16)
    h = np.arange(NSA_H)
    grp = jnp.asarray((h[:, None] // NSA_HPG) == (h[None, :] // NSA_HPG), dtype=BF16)
    o16 = dec_nsa(page_table, cache_t, win_t, _place_heads(heads(qn) * scale), _place_heads(heads(qr) * scale),
                  new_rows, gates[:, :3 * NSA_H].reshape(DB, NSA_H, 3),
                  w1t, b1, w2t, b2t, kn, _overlap_T(DEC_N_CMP, DEC_N_SEL), expand, grp)
    o_nsa = _take_heads(o16)
    o_ret, ret_s = retention_step(rq[:, None, :], rk[:, None, :], cols(dec, OFF_RV, RET_W)[:, None, :], ret_s0)
    cat = odd_post(o_nsa, o_nsa, o_nsa, gates, o_ret.reshape(DB, RET_W), proj, p, n_p + DB, n_p, prior=cat, gated=True)
    rows_s = jnp.transpose(kv_t.reshape(DEC_SEQ, KV_SLOTS, NSA_G, NSA_HD, DB), (4, 0, 1, 2, 3)).astype(cache_layer.dtype)
    new_row = jnp.concatenate([kw, vw], axis=1)[:, None, :].astype(win_buf.dtype)
    win_new = win_shift(win_t, new_row).reshape(DB, 2, NSA_G, NSA_HD, WIN_BUF)
    win_s = jnp.transpose(win_new, (0, 4, 1, 2, 3))
    return cat, (kv_rows_p, win_p, ret_p), (rows_s, win_s, ret_s)


def _stack(xs, dt):
    return jnp.stack(xs).astype(dt)


def kernel(x_prompt, x_sample, state_lru_h, state_lru_conv, state_rwkv_shift, state_rwkv_wkv,
           cache_nsa_kv, cache_nsa_win, state_ret, page_table,
           norm_ffn1, ffn1_w_in, ffn1_w_out, norm_mix, norm_ffn2, ffn2_w_in, ffn2_w_out,
           ab_w_in, lru_conv_w, lru_conv_b, lru_wa, lru_ba, lru_wx, lru_bx, lru_lambda,
           rwkv_mu, rwkv_w0, rwkv_w2, rwkv_a0, rwkv_a2, rwkv_g2, rwkv_k_k, rwkv_k_a, rwkv_r_k,
           rwkv_ln_g, rwkv_ln_b, ab_w_out,
           cd_w_in, nsa_q_norm, nsa_k_norm, cmp_k_w1, cmp_k_b1, cmp_k_w2, cmp_k_b2,
           cmp_v_w1, cmp_v_b1, cmp_v_w2, cmp_v_b2, ret_gn_g, ret_gn_b, cd_w_out):
    dt = x_prompt.dtype
    B = x_prompt.shape[0]
    DB = x_sample.shape[0]
    y = None
    lru_h_p, lru_h_s, lru_c_p, lru_c_s, sh_p, sh_s, wkv_p, wkv_s = [], [], [], [], [], [], [], []
    kv_p, kv_s, win_p, win_s, ret_p, ret_s = [], [], [], [], [], []
    for layer in range(DEPTH):
        li = layer // 2
        w1 = _prep_ffn_weights(ffn1_w_in, ffn1_w_out, layer)
        if layer == 0:
            y = ffn_block(x_prompt.reshape(N_PROMPT, D_MODEL), norm_ffn1[layer], *w1,
                          x_tail=x_sample.reshape(DB * DEC_SEQ, D_MODEL))
        else:
            y = ffn_block(y, norm_ffn1[layer], *w1)
        if layer % 2 == 0:
            p = {'conv_w': lru_conv_w[li], 'conv_b': lru_conv_b[li],
                 'wa': lru_wa[li], 'ba': lru_ba[li], 'wx': lru_wx[li], 'bx': lru_bx[li], 'lam': lru_lambda[li],
                 'mu': rwkv_mu[li], 'w0': rwkv_w0[li], 'w2': rwkv_w2[li], 'a0': rwkv_a0[li], 'a2': rwkv_a2[li],
                 'g2': rwkv_g2[li], 'k_k': rwkv_k_k[li], 'k_a': rwkv_k_a[li], 'r_k': rwkv_r_k[li],
                 'ln_g': rwkv_ln_g[li], 'ln_b': rwkv_ln_b[li]}
            proj = norm_matmul(y, norm_mix[layer], _prep_cols(ab_w_in[li], WIDE_COL_TILE), tn=WIDE_COL_TILE)
            cat, (a0, a1, a2, a3), (b0, b1, b2, b3) = even_mixer(
                proj, p, B, SEQ, DB, state_lru_h[li], state_lru_conv[li], state_rwkv_shift[li], state_rwkv_wkv[li])
            lru_h_p.append(a0); lru_c_p.append(a1); sh_p.append(a2); wkv_p.append(a3)
            lru_h_s.append(b0); lru_c_s.append(b1); sh_s.append(b2); wkv_s.append(b3)
            w_out = ab_w_out[li]
        else:
            p = {'q_norm': nsa_q_norm[li], 'k_norm': nsa_k_norm[li],
                 'ck': (cmp_k_w1[li], cmp_k_b1[li], cmp_k_w2[li], cmp_k_b2[li]),
                 'cv': (cmp_v_w1[li], cmp_v_b1[li], cmp_v_w2[li], cmp_v_b2[li]),
                 'gn_g': ret_gn_g[li], 'gn_b': ret_gn_b[li]}
            proj = norm_matmul(y, norm_mix[layer], _odd_weight_cols(cd_w_in[li]), tn=WIDE_COL_TILE)
            cat, (a0, a1, a2), (b0, b1, b2) = odd_mixer(
                proj, p, B, SEQ, DB, cache_nsa_kv[li], page_table, cache_nsa_win[li], state_ret[li])
            kv_p.append(a0); win_p.append(a1); ret_p.append(a2)
            kv_s.append(b0); win_s.append(b1); ret_s.append(b2)
            w_out = cd_w_out[li]
        y = matmul_residual(cat, w_out.astype(BF16), y, tn=WIDE_COL_TILE)
        w2 = _prep_ffn_weights(ffn2_w_in, ffn2_w_out, layer)
        if layer == DEPTH - 1:
            yp, ys = ffn_block(y, norm_ffn2[layer], *w2, n_tail_out=DB * DEC_SEQ)
        else:
            y = ffn_block(y, norm_ffn2[layer], *w2)
    yp = yp.reshape(B, SEQ, D_MODEL)
    ys = ys.reshape(DB, DEC_SEQ, D_MODEL)
    return (yp, ys,
            _stack(lru_h_p, dt), _stack(lru_h_s, dt), _stack(lru_c_p, dt), _stack(lru_c_s, dt),
            _stack(sh_p, dt), _stack(sh_s, dt), _stack(wkv_p, dt), _stack(wkv_s, dt),
            _stack(kv_p, dt), _stack(kv_s, dt), _stack(win_p, dt), _stack(win_s, dt),
            _stack(ret_p, dt), _stack(ret_s, dt))
```
